```python
import math
import jax, jax.numpy as jnp
from jax import lax
import numpy as np


D_MODEL = 1024
BATCH = 16
SEQ = 4096
DEPTH = 4

CHUNK = 64
N_META = 16
Q_BLOCK = 2 * CHUNK
SSD_CHUNK = 2 * CHUNK
NORM_EPS = 1e-6

ATTN_HEADS = 16
ATTN_HEAD_DIM = 64
D_ATTN = ATTN_HEADS * ATTN_HEAD_DIM
FORGET_BIAS_INIT = 3.0

SSD_HEAD_DIM = 64
D_SSD = D_MODEL
SSD_HEADS = D_SSD // SSD_HEAD_DIM
SSD_GROUPS = 2
SSD_STATE = 128
SSD_CONV = 4
D_XBC = D_SSD + 2 * SSD_GROUPS * SSD_STATE

D_LRU = D_MODEL
LRU_BLOCKS = 16
LRU_BLOCK_DIM = D_LRU // LRU_BLOCKS
LRU_CONV = 4
LRU_C = 8.0

D_FF = 2816

N_BRANCH = 3
IN_SIZES = (D_ATTN, D_ATTN, D_ATTN, ATTN_HEADS, D_SSD, D_XBC, SSD_HEADS, D_LRU, D_LRU, N_BRANCH * D_MODEL)
N_IN = 3 * D_ATTN + ATTN_HEADS + D_SSD + D_XBC + SSD_HEADS + 2 * D_LRU + N_BRANCH * D_MODEL

kernel_name = 'hybrid_fox_ssd_rglru_macaron'


def _in_split_points():
    return [int(v) for v in np.cumsum(IN_SIZES)[:-1]]


def rms_norm(x, g):
    xf = x.astype(jnp.float32)
    y = xf * lax.rsqrt(jnp.mean(xf * xf, axis=-1, keepdims=True) + NORM_EPS)
    return (y * g.astype(jnp.float32)).astype(x.dtype)


def swiglu_ffn(h, w_gate_up, w_down):
    g, u = jnp.split(h @ w_gate_up, 2, axis=-1)
    return (jax.nn.silu(g) * u) @ w_down


def causal_depthwise_conv(x, w, b):
    k_width, c = w.shape
    y = lax.conv_general_dilated(x, w[:, None, :].astype(x.dtype), window_strides=(1,),
                                 padding=[(k_width - 1, 0)],
                                 dimension_numbers=('NWC', 'WIO', 'NWC'),
                                 feature_group_count=c)
    return y + b.astype(y.dtype)


def forgetting_attention(q, k, v, log_f):
    b, L, h, dh = q.shape
    c = jnp.cumsum(log_f, axis=1).transpose(0, 2, 1)
    scale = dh ** -0.5
    outs = []
    for start in range(0, L, Q_BLOCK):
        end = start + Q_BLOCK
        s = jnp.einsum('bqhd,bkhd->bhqk', q[:, start:end], k[:, :end],
                       preferred_element_type=jnp.float32) * scale
        bias = c[:, :, start:end, None] - c[:, :, None, :end]
        visible = jnp.arange(end)[None, :] <= jnp.arange(start, end)[:, None]
        p = jax.nn.softmax(jnp.where(visible, s + bias, -jnp.inf), axis=-1)
        outs.append(jnp.einsum('bhqk,bkhd->bqhd', p.astype(v.dtype), v[:, :end]))
    return jnp.concatenate(outs, axis=1)


def segsum(a):
    t = a.shape[-1]
    a_rep = jnp.broadcast_to(a[..., :, None], a.shape + (t,))
    a_rep = jnp.where(jnp.tril(jnp.ones((t, t), bool), -1), a_rep, 0.0)
    cs = jnp.cumsum(a_rep, axis=-2)
    return jnp.where(jnp.tril(jnp.ones((t, t), bool)), cs, -jnp.inf)


def ssd_chunked_scan(x, dt, a, bm, cm):
    b, L, h, p = x.shape
    g, n = bm.shape[2], bm.shape[3]
    e = h // g
    nc = L // SSD_CHUNK
    q = SSD_CHUNK
    xdt = (x * dt[..., None]).reshape(b, nc, q, g, e, p)
    da = (dt * a).reshape(b, nc, q, g, e).transpose(0, 3, 4, 1, 2)
    bc = bm.reshape(b, nc, q, g, n)
    cc = cm.reshape(b, nc, q, g, n)
    a_cum = jnp.cumsum(da, axis=-1)
    decay_in = jnp.exp(segsum(da))
    cb = jnp.einsum('bclgn,bcsgn->bgcls', cc, bc)
    y_diag = jnp.einsum('bgecls,bcsgep->bclgep', cb[:, :, None] * decay_in, xdt)
    decay_to_end = jnp.exp(a_cum[..., -1:] - a_cum).transpose(0, 3, 4, 1, 2)
    states = jnp.einsum('bclgn,bclgep->bcgepn', bc, xdt * decay_to_end[..., None])
    states = jnp.concatenate([jnp.zeros_like(states[:, :1]), states], axis=1)
    chunk_a = jnp.pad(a_cum[..., -1], [(0, 0), (0, 0), (0, 0), (1, 0)])
    chunk_decay = jnp.exp(segsum(chunk_a))
    prev_states = jnp.einsum('bgezc,bcgepn->bzgepn', chunk_decay, states)[:, :-1]
    decay_from_start = jnp.exp(a_cum).transpose(0, 3, 4, 1, 2)
    y_off = jnp.einsum('bclgn,bcgepn->bclgep', cc, prev_states) * decay_from_start[..., None]
    return (y_diag + y_off).reshape(b, L, h, p)


def mamba2_branch(z, xbc, dt, conv_w, conv_b, dt_bias, a_log, d_skip, norm_w):
    b, L, _ = z.shape
    f32 = jnp.float32
    xbc = jax.nn.silu(causal_depthwise_conv(xbc, conv_w, conv_b)).astype(f32)
    xs, bm, cm = jnp.split(xbc, [D_SSD, D_SSD + SSD_GROUPS * SSD_STATE], axis=-1)
    dt = jax.nn.softplus(dt.astype(f32) + dt_bias.astype(f32))
    a = -jnp.exp(a_log.astype(f32))
    xh = xs.reshape(b, L, SSD_HEADS, SSD_HEAD_DIM)
    y = ssd_chunked_scan(xh, dt, a,
                         bm.reshape(b, L, SSD_GROUPS, SSD_STATE),
                         cm.reshape(b, L, SSD_GROUPS, SSD_STATE))
    y = y + d_skip.astype(f32)[:, None] * xh
    gsz = D_SSD // SSD_GROUPS
    y = y.reshape(b, L, SSD_GROUPS, gsz) * jax.nn.silu(z.astype(f32)).reshape(b, L, SSD_GROUPS, gsz)
    y = y * lax.rsqrt(jnp.mean(y * y, axis=-1, keepdims=True) + NORM_EPS)
    return (y.reshape(b, L, D_SSD) * norm_w.astype(f32)).astype(z.dtype)


def rglru_branch(xr, gate, conv_w, conv_b, w_a, b_a, w_x, b_x, lam):
    b, L, _ = xr.shape
    f32 = jnp.float32
    xc = causal_depthwise_conv(xr, conv_w, conv_b).astype(f32)
    xb = xc.reshape(b, L, LRU_BLOCKS, LRU_BLOCK_DIM)
    r = jax.nn.sigmoid(jnp.einsum('blhi,hij->blhj', xb, w_a.astype(f32)).reshape(b, L, D_LRU) + b_a.astype(f32))
    i = jax.nn.sigmoid(jnp.einsum('blhi,hij->blhj', xb, w_x.astype(f32)).reshape(b, L, D_LRU) + b_x.astype(f32))
    log_a = LRU_C * r * jax.nn.log_sigmoid(lam.astype(f32))
    a = jnp.exp(log_a)
    mult = jnp.sqrt(-jnp.expm1(2.0 * log_a))
    mult = jnp.where(jnp.arange(L)[None, :, None] == 0, 1.0, mult)
    u = mult * (i * xc)

    def combine(left, right):
        a_l, u_l = left
        a_r, u_r = right
        return a_l * a_r, a_r * u_l + u_r

    _, hs = lax.associative_scan(combine, (a, u), axis=1)
    return (hs * jax.nn.gelu(gate.astype(f32))).astype(xr.dtype)


def mixer_block(h, w_in, fox_forget_bias, ssd_conv_w, ssd_conv_b, ssd_dt_bias, ssd_a_log, ssd_d, ssd_norm,
                lru_conv_w, lru_conv_b, lru_w_a, lru_b_a, lru_w_x, lru_b_x, lru_lambda,
                w_branch_attn, w_branch_ssd, w_branch_lru, w_out):
    b, L, _ = h.shape
    proj = h @ w_in
    q, k, v, f_logit, z, xbc, dt, xr, gate_r, merge = jnp.split(proj, _in_split_points(), axis=-1)
    log_f = jax.nn.log_sigmoid(f_logit.astype(jnp.float32) + fox_forget_bias.astype(jnp.float32))
    split_heads = lambda t: t.reshape(b, L, ATTN_HEADS, ATTN_HEAD_DIM)
    y_a = forgetting_attention(split_heads(q), split_heads(k), split_heads(v), log_f).reshape(b, L, D_ATTN)
    y_b = mamba2_branch(z, xbc, dt, ssd_conv_w, ssd_conv_b, ssd_dt_bias, ssd_a_log, ssd_d, ssd_norm)
    y_c = rglru_branch(xr, gate_r, lru_conv_w, lru_conv_b, lru_w_a, lru_b_a, lru_w_x, lru_b_x, lru_lambda)
    g_a, g_b, g_c = jnp.split(jax.nn.sigmoid(merge), N_BRANCH, axis=-1)
    mixed = g_a * (y_a @ w_branch_attn) + g_b * (y_b @ w_branch_ssd) + g_c * (y_c @ w_branch_lru)
    return mixed @ w_out


def _fwd_setup_inputs(seed: int = 0) -> dict:
    key = jax.random.key(seed)
    keys = iter(jax.random.split(key, 40))

    def normal(shape, scale):
        return jax.random.normal(next(keys), shape, jnp.float32) * scale

    def uniform(shape, lo, hi):
        return jax.random.uniform(next(keys), shape, jnp.float32, lo, hi)

    def gain(shape):
        return 1.0 + normal(shape, 0.02)

    x = normal((BATCH, SEQ, D_MODEL), 1.0)
    meta_tokens = normal((N_META, D_MODEL), 1.0)
    ffn1_norm = gain((DEPTH, D_MODEL))
    ffn1_w_gate_up = normal((DEPTH, D_MODEL, 2 * D_FF), D_MODEL ** -0.5)
    ffn1_w_down = normal((DEPTH, D_FF, D_MODEL), D_FF ** -0.5)
    mix_norm = gain((DEPTH, D_MODEL))
    w_in = normal((DEPTH, D_MODEL, N_IN), D_MODEL ** -0.5)
    fox_forget_bias = FORGET_BIAS_INIT + normal((DEPTH, ATTN_HEADS), 0.5)
    ssd_conv_w = normal((DEPTH, SSD_CONV, D_XBC), SSD_CONV ** -0.5)
    ssd_conv_b = normal((DEPTH, D_XBC), 0.02)
    dt0 = jnp.exp(uniform((DEPTH, SSD_HEADS), math.log(1e-3), math.log(1e-1)))
    ssd_dt_bias = dt0 + jnp.log(-jnp.expm1(-dt0))
    ssd_a_log = jnp.log(uniform((DEPTH, SSD_HEADS), 1.0, 16.0))
    ssd_d = gain((DEPTH, SSD_HEADS))
    ssd_norm = gain((DEPTH, D_SSD))
    lru_conv_w = normal((DEPTH, LRU_CONV, D_LRU), LRU_CONV ** -0.5)
    lru_conv_b = normal((DEPTH, D_LRU), 0.02)
    lru_w_a = normal((DEPTH, LRU_BLOCKS, LRU_BLOCK_DIM, LRU_BLOCK_DIM), LRU_BLOCK_DIM ** -0.5)
    lru_b_a = normal((DEPTH, D_LRU), 0.02)
    lru_w_x = normal((DEPTH, LRU_BLOCKS, LRU_BLOCK_DIM, LRU_BLOCK_DIM), LRU_BLOCK_DIM ** -0.5)
    lru_b_x = normal((DEPTH, D_LRU), 0.02)
    a_pow_c = uniform((DEPTH, D_LRU), 0.9, 0.999)
    a0 = a_pow_c ** (1.0 / LRU_C)
    lru_lambda = jnp.log(a0) - jnp.log1p(-a0)
    w_branch_attn = normal((DEPTH, D_ATTN, D_MODEL), D_ATTN ** -0.5)
    w_branch_ssd = normal((DEPTH, D_SSD, D_MODEL), D_SSD ** -0.5)
    w_branch_lru = normal((DEPTH, D_LRU, D_MODEL), D_LRU ** -0.5)
    w_out = normal((DEPTH, D_MODEL, D_MODEL), D_MODEL ** -0.5)
    ffn2_norm = gain((DEPTH, D_MODEL))
    ffn2_w_gate_up = normal((DEPTH, D_MODEL, 2 * D_FF), D_MODEL ** -0.5)
    ffn2_w_down = normal((DEPTH, D_FF, D_MODEL), D_FF ** -0.5)
    final_norm = gain((D_MODEL,))
    return {'x': x, 'meta_tokens': meta_tokens,
            'ffn1_norm': ffn1_norm, 'ffn1_w_gate_up': ffn1_w_gate_up, 'ffn1_w_down': ffn1_w_down,
            'mix_norm': mix_norm, 'w_in': w_in, 'fox_forget_bias': fox_forget_bias,
            'ssd_conv_w': ssd_conv_w, 'ssd_conv_b': ssd_conv_b, 'ssd_dt_bias': ssd_dt_bias,
            'ssd_a_log': ssd_a_log, 'ssd_d': ssd_d, 'ssd_norm': ssd_norm,
            'lru_conv_w': lru_conv_w, 'lru_conv_b': lru_conv_b, 'lru_w_a': lru_w_a, 'lru_b_a': lru_b_a,
            'lru_w_x': lru_w_x, 'lru_b_x': lru_b_x, 'lru_lambda': lru_lambda,
            'w_branch_attn': w_branch_attn, 'w_branch_ssd': w_branch_ssd, 'w_branch_lru': w_branch_lru,
            'w_out': w_out,
            'ffn2_norm': ffn2_norm, 'ffn2_w_gate_up': ffn2_w_gate_up, 'ffn2_w_down': ffn2_w_down,
            'final_norm': final_norm}


def _fwd_reference(x, meta_tokens, ffn1_norm, ffn1_w_gate_up, ffn1_w_down, mix_norm, w_in, fox_forget_bias,
              ssd_conv_w, ssd_conv_b, ssd_dt_bias, ssd_a_log, ssd_d, ssd_norm,
              lru_conv_w, lru_conv_b, lru_w_a, lru_b_a, lru_w_x, lru_b_x, lru_lambda,
              w_branch_attn, w_branch_ssd, w_branch_lru, w_out,
              ffn2_norm, ffn2_w_gate_up, ffn2_w_down, final_norm):
    b, s, d = x.shape
    length = N_META + s
    padded = -(-length // Q_BLOCK) * Q_BLOCK
    meta = jnp.broadcast_to(meta_tokens.astype(x.dtype)[None], (b, N_META, d))
    h = jnp.concatenate([meta, x, jnp.zeros((b, padded - length, d), x.dtype)], axis=1)
    for l in range(DEPTH):
        h = h + 0.5 * swiglu_ffn(rms_norm(h, ffn1_norm[l]), ffn1_w_gate_up[l], ffn1_w_down[l])
        h = h + mixer_block(rms_norm(h, mix_norm[l]), w_in[l], fox_forget_bias[l],
                            ssd_conv_w[l], ssd_conv_b[l], ssd_dt_bias[l], ssd_a_log[l], ssd_d[l], ssd_norm[l],
                            lru_conv_w[l], lru_conv_b[l], lru_w_a[l], lru_b_a[l], lru_w_x[l], lru_b_x[l],
                            lru_lambda[l], w_branch_attn[l], w_branch_ssd[l], w_branch_lru[l], w_out[l])
        h = h + 0.5 * swiglu_ffn(rms_norm(h, ffn2_norm[l]), ffn2_w_gate_up[l], ffn2_w_down[l])
    return rms_norm(h, final_norm)[:, N_META:N_META + s]


import jax as _jax
import jax.numpy as _jnp

TWIN_FORMAT = 'train_step'
FWD_PARAMS = ['x', 'meta_tokens', 'ffn1_norm', 'ffn1_w_gate_up', 'ffn1_w_down', 'mix_norm', 'w_in', 'fox_forget_bias', 'ssd_conv_w', 'ssd_conv_b', 'ssd_dt_bias', 'ssd_a_log', 'ssd_d', 'ssd_norm', 'lru_conv_w', 'lru_conv_b', 'lru_w_a', 'lru_b_a', 'lru_w_x', 'lru_b_x', 'lru_lambda', 'w_branch_attn', 'w_branch_ssd', 'w_branch_lru', 'w_out', 'ffn2_norm', 'ffn2_w_gate_up', 'ffn2_w_down', 'final_norm']
TWIN_WEIGHTS = ['meta_tokens', 'ffn1_norm', 'ffn1_w_gate_up', 'ffn1_w_down', 'mix_norm', 'w_in', 'fox_forget_bias', 'ssd_conv_w', 'ssd_conv_b', 'ssd_dt_bias', 'ssd_a_log', 'ssd_d', 'ssd_norm', 'lru_conv_w', 'lru_conv_b', 'lru_w_a', 'lru_b_a', 'lru_w_x', 'lru_b_x', 'lru_lambda', 'w_branch_attn', 'w_branch_ssd', 'w_branch_lru', 'w_out', 'ffn2_norm', 'ffn2_w_gate_up', 'ffn2_w_down', 'final_norm']
TWIN_DIFF_INPUT = 'x'
TWIN_INPUTS = ['x', 'meta_tokens', 'ffn1_norm', 'ffn1_w_gate_up', 'ffn1_w_down', 'mix_norm', 'w_in', 'fox_forget_bias', 'ssd_conv_w', 'ssd_conv_b', 'ssd_dt_bias', 'ssd_a_log', 'ssd_d', 'ssd_norm', 'lru_conv_w', 'lru_conv_b', 'lru_w_a', 'lru_b_a', 'lru_w_x', 'lru_b_x', 'lru_lambda', 'w_branch_attn', 'w_branch_ssd', 'w_branch_lru', 'w_out', 'ffn2_norm', 'ffn2_w_gate_up', 'ffn2_w_down', 'final_norm', 'loss_target', 'm_meta_tokens', 'm_ffn1_norm', 'm_ffn1_w_gate_up', 'm_ffn1_w_down', 'm_mix_norm', 'm_w_in', 'm_fox_forget_bias', 'm_ssd_conv_w', 'm_ssd_conv_b', 'm_ssd_dt_bias', 'm_ssd_a_log', 'm_ssd_d', 'm_ssd_norm', 'm_lru_conv_w', 'm_lru_conv_b', 'm_lru_w_a', 'm_lru_b_a', 'm_lru_w_x', 'm_lru_b_x', 'm_lru_lambda', 'm_w_branch_attn', 'm_w_branch_ssd', 'm_w_branch_lru', 'm_w_out', 'm_ffn2_norm', 'm_ffn2_w_gate_up', 'm_ffn2_w_down', 'm_final_norm', 'v_meta_tokens', 'v_ffn1_norm', 'v_ffn1_w_gate_up', 'v_ffn1_w_down', 'v_mix_norm', 'v_w_in', 'v_fox_forget_bias', 'v_ssd_conv_w', 'v_ssd_conv_b', 'v_ssd_dt_bias', 'v_ssd_a_log', 'v_ssd_d', 'v_ssd_norm', 'v_lru_conv_w', 'v_lru_conv_b', 'v_lru_w_a', 'v_lru_b_a', 'v_lru_w_x', 'v_lru_b_x', 'v_lru_lambda', 'v_w_branch_attn', 'v_w_branch_ssd', 'v_w_branch_lru', 'v_w_out', 'v_ffn2_norm', 'v_ffn2_w_gate_up', 'v_ffn2_w_down', 'v_final_norm']
TWIN_OUTPUTS = ['loss', 'grad_x', 'grad_meta_tokens', 'grad_ffn1_norm', 'grad_ffn1_w_gate_up', 'grad_ffn1_w_down', 'grad_mix_norm', 'grad_w_in', 'grad_fox_forget_bias', 'grad_ssd_conv_w', 'grad_ssd_conv_b', 'grad_ssd_dt_bias', 'grad_ssd_a_log', 'grad_ssd_d', 'grad_ssd_norm', 'grad_lru_conv_w', 'grad_lru_conv_b', 'grad_lru_w_a', 'grad_lru_b_a', 'grad_lru_w_x', 'grad_lru_b_x', 'grad_lru_lambda', 'grad_w_branch_attn', 'grad_w_branch_ssd', 'grad_w_branch_lru', 'grad_w_out', 'grad_ffn2_norm', 'grad_ffn2_w_gate_up', 'grad_ffn2_w_down', 'grad_final_norm', 'delta_meta_tokens', 'delta_ffn1_norm', 'delta_ffn1_w_gate_up', 'delta_ffn1_w_down', 'delta_mix_norm', 'delta_w_in', 'delta_fox_forget_bias', 'delta_ssd_conv_w', 'delta_ssd_conv_b', 'delta_ssd_dt_bias', 'delta_ssd_a_log', 'delta_ssd_d', 'delta_ssd_norm', 'delta_lru_conv_w', 'delta_lru_conv_b', 'delta_lru_w_a', 'delta_lru_b_a', 'delta_lru_w_x', 'delta_lru_b_x', 'delta_lru_lambda', 'delta_w_branch_attn', 'delta_w_branch_ssd', 'delta_w_branch_lru', 'delta_w_out', 'delta_ffn2_norm', 'delta_ffn2_w_gate_up', 'delta_ffn2_w_down', 'delta_final_norm', 'new_m_meta_tokens', 'new_m_ffn1_norm', 'new_m_ffn1_w_gate_up', 'new_m_ffn1_w_down', 'new_m_mix_norm', 'new_m_w_in', 'new_m_fox_forget_bias', 'new_m_ssd_conv_w', 'new_m_ssd_conv_b', 'new_m_ssd_dt_bias', 'new_m_ssd_a_log', 'new_m_ssd_d', 'new_m_ssd_norm', 'new_m_lru_conv_w', 'new_m_lru_conv_b', 'new_m_lru_w_a', 'new_m_lru_b_a', 'new_m_lru_w_x', 'new_m_lru_b_x', 'new_m_lru_lambda', 'new_m_w_branch_attn', 'new_m_w_branch_ssd', 'new_m_w_branch_lru', 'new_m_w_out', 'new_m_ffn2_norm', 'new_m_ffn2_w_gate_up', 'new_m_ffn2_w_down', 'new_m_final_norm', 'new_v_meta_tokens', 'new_v_ffn1_norm', 'new_v_ffn1_w_gate_up', 'new_v_ffn1_w_down', 'new_v_mix_norm', 'new_v_w_in', 'new_v_fox_forget_bias', 'new_v_ssd_conv_w', 'new_v_ssd_conv_b', 'new_v_ssd_dt_bias', 'new_v_ssd_a_log', 'new_v_ssd_d', 'new_v_ssd_norm', 'new_v_lru_conv_w', 'new_v_lru_conv_b', 'new_v_lru_w_a', 'new_v_lru_b_a', 'new_v_lru_w_x', 'new_v_lru_b_x', 'new_v_lru_lambda', 'new_v_w_branch_attn', 'new_v_w_branch_ssd', 'new_v_w_branch_lru', 'new_v_w_out', 'new_v_ffn2_norm', 'new_v_ffn2_w_gate_up', 'new_v_ffn2_w_down', 'new_v_final_norm']
TWIN_LEAF_KINDS = {'loss': 'loss', 'grad_x': 'grad_x', 'grad_meta_tokens': 'grad_w', 'grad_ffn1_norm': 'grad_w', 'grad_ffn1_w_gate_up': 'grad_w', 'grad_ffn1_w_down': 'grad_w', 'grad_mix_norm': 'grad_w', 'grad_w_in': 'grad_w', 'grad_fox_forget_bias': 'grad_w', 'grad_ssd_conv_w': 'grad_w', 'grad_ssd_conv_b': 'grad_w', 'grad_ssd_dt_bias': 'grad_w', 'grad_ssd_a_log': 'grad_w', 'grad_ssd_d': 'grad_w', 'grad_ssd_norm': 'grad_w', 'grad_lru_conv_w': 'grad_w', 'grad_lru_conv_b': 'grad_w', 'grad_lru_w_a': 'grad_w', 'grad_lru_b_a': 'grad_w', 'grad_lru_w_x': 'grad_w', 'grad_lru_b_x': 'grad_w', 'grad_lru_lambda': 'grad_w', 'grad_w_branch_attn': 'grad_w', 'grad_w_branch_ssd': 'grad_w', 'grad_w_branch_lru': 'grad_w', 'grad_w_out': 'grad_w', 'grad_ffn2_norm': 'grad_w', 'grad_ffn2_w_gate_up': 'grad_w', 'grad_ffn2_w_down': 'grad_w', 'grad_final_norm': 'grad_w', 'delta_meta_tokens': 'delta_w', 'delta_ffn1_norm': 'delta_w', 'delta_ffn1_w_gate_up': 'delta_w', 'delta_ffn1_w_down': 'delta_w', 'delta_mix_norm': 'delta_w', 'delta_w_in': 'delta_w', 'delta_fox_forget_bias': 'delta_w', 'delta_ssd_conv_w': 'delta_w', 'delta_ssd_conv_b': 'delta_w', 'delta_ssd_dt_bias': 'delta_w', 'delta_ssd_a_log': 'delta_w', 'delta_ssd_d': 'delta_w', 'delta_ssd_norm': 'delta_w', 'delta_lru_conv_w': 'delta_w', 'delta_lru_conv_b': 'delta_w', 'delta_lru_w_a': 'delta_w', 'delta_lru_b_a': 'delta_w', 'delta_lru_w_x': 'delta_w', 'delta_lru_b_x': 'delta_w', 'delta_lru_lambda': 'delta_w', 'delta_w_branch_attn': 'delta_w', 'delta_w_branch_ssd': 'delta_w', 'delta_w_branch_lru': 'delta_w', 'delta_w_out': 'delta_w', 'delta_ffn2_norm': 'delta_w', 'delta_ffn2_w_gate_up': 'delta_w', 'delta_ffn2_w_down': 'delta_w', 'delta_final_norm': 'delta_w', 'new_m_meta_tokens': 'new_m', 'new_m_ffn1_norm': 'new_m', 'new_m_ffn1_w_gate_up': 'new_m', 'new_m_ffn1_w_down': 'new_m', 'new_m_mix_norm': 'new_m', 'new_m_w_in': 'new_m', 'new_m_fox_forget_bias': 'new_m', 'new_m_ssd_conv_w': 'new_m', 'new_m_ssd_conv_b': 'new_m', 'new_m_ssd_dt_bias': 'new_m', 'new_m_ssd_a_log': 'new_m', 'new_m_ssd_d': 'new_m', 'new_m_ssd_norm': 'new_m', 'new_m_lru_conv_w': 'new_m', 'new_m_lru_conv_b': 'new_m', 'new_m_lru_w_a': 'new_m', 'new_m_lru_b_a': 'new_m', 'new_m_lru_w_x': 'new_m', 'new_m_lru_b_x': 'new_m', 'new_m_lru_lambda': 'new_m', 'new_m_w_branch_attn': 'new_m', 'new_m_w_branch_ssd': 'new_m', 'new_m_w_branch_lru': 'new_m', 'new_m_w_out': 'new_m', 'new_m_ffn2_norm': 'new_m', 'new_m_ffn2_w_gate_up': 'new_m', 'new_m_ffn2_w_down': 'new_m', 'new_m_final_norm': 'new_m', 'new_v_meta_tokens': 'new_v', 'new_v_ffn1_norm': 'new_v', 'new_v_ffn1_w_gate_up': 'new_v', 'new_v_ffn1_w_down': 'new_v', 'new_v_mix_norm': 'new_v', 'new_v_w_in': 'new_v', 'new_v_fox_forget_bias': 'new_v', 'new_v_ssd_conv_w': 'new_v', 'new_v_ssd_conv_b': 'new_v', 'new_v_ssd_dt_bias': 'new_v', 'new_v_ssd_a_log': 'new_v', 'new_v_ssd_d': 'new_v', 'new_v_ssd_norm': 'new_v', 'new_v_lru_conv_w': 'new_v', 'new_v_lru_conv_b': 'new_v', 'new_v_lru_w_a': 'new_v', 'new_v_lru_b_a': 'new_v', 'new_v_lru_w_x': 'new_v', 'new_v_lru_b_x': 'new_v', 'new_v_lru_lambda': 'new_v', 'new_v_w_branch_attn': 'new_v', 'new_v_w_branch_ssd': 'new_v', 'new_v_w_branch_lru': 'new_v', 'new_v_w_out': 'new_v', 'new_v_ffn2_norm': 'new_v', 'new_v_ffn2_w_gate_up': 'new_v', 'new_v_ffn2_w_down': 'new_v', 'new_v_final_norm': 'new_v'}


def _forward(args):
    return _fwd_reference(*[args[k] for k in FWD_PARAMS])


def _output_shape():
    out = _jax.eval_shape(lambda: _forward(_fwd_setup_inputs(0)))
    return out.shape, out.dtype

N_MICROBATCH = 1
ADAM_LR = 0.001
ADAM_B1 = 0.9
ADAM_B2 = 0.999
ADAM_EPS = 1e-08
ADAM_WD = 0.01
ADAM_STEP = 10
PER_EXAMPLE_BATCH_AXIS = {'x': 0, 'loss_target': 0}
SHARED_INPUTS = []
_WEIGHT_DTYPES = {'meta_tokens': _jnp.float32, 'ffn1_norm': _jnp.float32, 'ffn1_w_gate_up': _jnp.float32, 'ffn1_w_down': _jnp.float32, 'mix_norm': _jnp.float32, 'w_in': _jnp.float32, 'fox_forget_bias': _jnp.float32, 'ssd_conv_w': _jnp.float32, 'ssd_conv_b': _jnp.float32, 'ssd_dt_bias': _jnp.float32, 'ssd_a_log': _jnp.float32, 'ssd_d': _jnp.float32, 'ssd_norm': _jnp.float32, 'lru_conv_w': _jnp.float32, 'lru_conv_b': _jnp.float32, 'lru_w_a': _jnp.float32, 'lru_b_a': _jnp.float32, 'lru_w_x': _jnp.float32, 'lru_b_x': _jnp.float32, 'lru_lambda': _jnp.float32, 'w_branch_attn': _jnp.float32, 'w_branch_ssd': _jnp.float32, 'w_branch_lru': _jnp.float32, 'w_out': _jnp.float32, 'ffn2_norm': _jnp.float32, 'ffn2_w_gate_up': _jnp.float32, 'ffn2_w_down': _jnp.float32, 'final_norm': _jnp.float32}
MOMENT_SCALE = {'meta_tokens': 1.288134e-02, 'ffn1_norm': 1.250508e-01, 'ffn1_w_gate_up': 5.285320e-02, 'ffn1_w_down': 8.629552e-02, 'mix_norm': 2.425087e-01, 'w_in': 6.849619e-02, 'fox_forget_bias': 1.593798e-01, 'ssd_conv_w': 1.088175e-01, 'ssd_conv_b': 1.657415e-01, 'ssd_dt_bias': 2.273967e-01, 'ssd_a_log': 3.327467e-01, 'ssd_d': 6.233178e-01, 'ssd_norm': 1.234731e-01, 'lru_conv_w': 9.917806e-02, 'lru_conv_b': 9.124020e-01, 'lru_w_a': 3.042389e-02, 'lru_b_a': 2.818840e-02, 'lru_w_x': 5.416853e-02, 'lru_b_x': 3.333785e-02, 'lru_lambda': 5.605786e-02, 'w_branch_attn': 4.188188e-02, 'w_branch_ssd': 1.246151e-01, 'w_branch_lru': 1.039335e-01, 'w_out': 1.605465e-01, 'ffn2_norm': 9.119166e-02, 'ffn2_w_gate_up': 3.933573e-02, 'ffn2_w_down': 6.413003e-02, 'final_norm': 6.387303e+01}


def _to_microbatches(a, axis):
    t = _jnp.moveaxis(a, axis, 0)
    t = t.reshape((N_MICROBATCH, t.shape[0] // N_MICROBATCH) + t.shape[1:])
    return _jnp.moveaxis(t, 1, axis + 1)


def setup_inputs(seed: int = 0) -> dict:
    inp = _fwd_setup_inputs(seed)
    key = _jax.random.fold_in(_jax.random.key(seed), 7919)
    shape, _ = _output_shape()
    out = dict(inp)
    out["loss_target"] = _jax.random.normal(_jax.random.fold_in(key, 0), shape, _jnp.float32)
    for i, name in enumerate(TWIN_WEIGHTS):
        w = inp[name].astype(_jnp.float32)
        if MOMENT_SCALE is None:
            s = _jnp.sqrt(_jnp.mean(_jnp.square(w)) + 1e-30)
        else:
            s = MOMENT_SCALE[name]
        km, kv = _jax.random.split(_jax.random.fold_in(key, i + 1))
        out[name] = w
        out["m_" + name] = s * _jax.random.normal(km, w.shape, _jnp.float32)
        out["v_" + name] = (s * s) * _jax.random.uniform(kv, w.shape, _jnp.float32, 0.5, 1.5)
    if N_MICROBATCH > 1:
        for name, axis in PER_EXAMPLE_BATCH_AXIS.items():
            out[name] = _to_microbatches(out[name], axis)
    return {'x': out['x'], 'meta_tokens': out['meta_tokens'], 'ffn1_norm': out['ffn1_norm'], 'ffn1_w_gate_up': out['ffn1_w_gate_up'], 'ffn1_w_down': out['ffn1_w_down'], 'mix_norm': out['mix_norm'], 'w_in': out['w_in'], 'fox_forget_bias': out['fox_forget_bias'], 'ssd_conv_w': out['ssd_conv_w'], 'ssd_conv_b': out['ssd_conv_b'], 'ssd_dt_bias': out['ssd_dt_bias'], 'ssd_a_log': out['ssd_a_log'], 'ssd_d': out['ssd_d'], 'ssd_norm': out['ssd_norm'], 'lru_conv_w': out['lru_conv_w'], 'lru_conv_b': out['lru_conv_b'], 'lru_w_a': out['lru_w_a'], 'lru_b_a': out['lru_b_a'], 'lru_w_x': out['lru_w_x'], 'lru_b_x': out['lru_b_x'], 'lru_lambda': out['lru_lambda'], 'w_branch_attn': out['w_branch_attn'], 'w_branch_ssd': out['w_branch_ssd'], 'w_branch_lru': out['w_branch_lru'], 'w_out': out['w_out'], 'ffn2_norm': out['ffn2_norm'], 'ffn2_w_gate_up': out['ffn2_w_gate_up'], 'ffn2_w_down': out['ffn2_w_down'], 'final_norm': out['final_norm'], 'loss_target': out['loss_target'], 'm_meta_tokens': out['m_meta_tokens'], 'm_ffn1_norm': out['m_ffn1_norm'], 'm_ffn1_w_gate_up': out['m_ffn1_w_gate_up'], 'm_ffn1_w_down': out['m_ffn1_w_down'], 'm_mix_norm': out['m_mix_norm'], 'm_w_in': out['m_w_in'], 'm_fox_forget_bias': out['m_fox_forget_bias'], 'm_ssd_conv_w': out['m_ssd_conv_w'], 'm_ssd_conv_b': out['m_ssd_conv_b'], 'm_ssd_dt_bias': out['m_ssd_dt_bias'], 'm_ssd_a_log': out['m_ssd_a_log'], 'm_ssd_d': out['m_ssd_d'], 'm_ssd_norm': out['m_ssd_norm'], 'm_lru_conv_w': out['m_lru_conv_w'], 'm_lru_conv_b': out['m_lru_conv_b'], 'm_lru_w_a': out['m_lru_w_a'], 'm_lru_b_a': out['m_lru_b_a'], 'm_lru_w_x': out['m_lru_w_x'], 'm_lru_b_x': out['m_lru_b_x'], 'm_lru_lambda': out['m_lru_lambda'], 'm_w_branch_attn': out['m_w_branch_attn'], 'm_w_branch_ssd': out['m_w_branch_ssd'], 'm_w_branch_lru': out['m_w_branch_lru'], 'm_w_out': out['m_w_out'], 'm_ffn2_norm': out['m_ffn2_norm'], 'm_ffn2_w_gate_up': out['m_ffn2_w_gate_up'], 'm_ffn2_w_down': out['m_ffn2_w_down'], 'm_final_norm': out['m_final_norm'], 'v_meta_tokens': out['v_meta_tokens'], 'v_ffn1_norm': out['v_ffn1_norm'], 'v_ffn1_w_gate_up': out['v_ffn1_w_gate_up'], 'v_ffn1_w_down': out['v_ffn1_w_down'], 'v_mix_norm': out['v_mix_norm'], 'v_w_in': out['v_w_in'], 'v_fox_forget_bias': out['v_fox_forget_bias'], 'v_ssd_conv_w': out['v_ssd_conv_w'], 'v_ssd_conv_b': out['v_ssd_conv_b'], 'v_ssd_dt_bias': out['v_ssd_dt_bias'], 'v_ssd_a_log': out['v_ssd_a_log'], 'v_ssd_d': out['v_ssd_d'], 'v_ssd_norm': out['v_ssd_norm'], 'v_lru_conv_w': out['v_lru_conv_w'], 'v_lru_conv_b': out['v_lru_conv_b'], 'v_lru_w_a': out['v_lru_w_a'], 'v_lru_b_a': out['v_lru_b_a'], 'v_lru_w_x': out['v_lru_w_x'], 'v_lru_b_x': out['v_lru_b_x'], 'v_lru_lambda': out['v_lru_lambda'], 'v_w_branch_attn': out['v_w_branch_attn'], 'v_w_branch_ssd': out['v_w_branch_ssd'], 'v_w_branch_lru': out['v_w_branch_lru'], 'v_w_out': out['v_w_out'], 'v_ffn2_norm': out['v_ffn2_norm'], 'v_ffn2_w_gate_up': out['v_ffn2_w_gate_up'], 'v_ffn2_w_down': out['v_ffn2_w_down'], 'v_final_norm': out['v_final_norm']}


def _loss(weights, diff, rest, loss_target):
    with _jax.named_scope("forward"):
        args = {**rest, TWIN_DIFF_INPUT: diff, **{k: w.astype(_WEIGHT_DTYPES[k]) for k, w in weights.items()}}
        y = _forward(args)
    with _jax.named_scope("loss_head"):
        err = _jnp.square(y.astype(_jnp.float32) - loss_target)
        return 0.5 * _jnp.sum(_jnp.mean(err, axis=-1)) if err.ndim else 0.5 * err


def _adamw(w, g, m, v):
    m = ADAM_B1 * m + (1.0 - ADAM_B1) * g
    v = ADAM_B2 * v + (1.0 - ADAM_B2) * _jnp.square(g)
    m_hat = m / (1.0 - ADAM_B1 ** ADAM_STEP)
    v_hat = v / (1.0 - ADAM_B2 ** ADAM_STEP)
    delta = -ADAM_LR * (m_hat / (_jnp.sqrt(v_hat) + ADAM_EPS) + ADAM_WD * w)
    return delta, m, v


def reference(x, meta_tokens, ffn1_norm, ffn1_w_gate_up, ffn1_w_down, mix_norm, w_in, fox_forget_bias, ssd_conv_w, ssd_conv_b, ssd_dt_bias, ssd_a_log, ssd_d, ssd_norm, lru_conv_w, lru_conv_b, lru_w_a, lru_b_a, lru_w_x, lru_b_x, lru_lambda, w_branch_attn, w_branch_ssd, w_branch_lru, w_out, ffn2_norm, ffn2_w_gate_up, ffn2_w_down, final_norm, loss_target, m_meta_tokens, m_ffn1_norm, m_ffn1_w_gate_up, m_ffn1_w_down, m_mix_norm, m_w_in, m_fox_forget_bias, m_ssd_conv_w, m_ssd_conv_b, m_ssd_dt_bias, m_ssd_a_log, m_ssd_d, m_ssd_norm, m_lru_conv_w, m_lru_conv_b, m_lru_w_a, m_lru_b_a, m_lru_w_x, m_lru_b_x, m_lru_lambda, m_w_branch_attn, m_w_branch_ssd, m_w_branch_lru, m_w_out, m_ffn2_norm, m_ffn2_w_gate_up, m_ffn2_w_down, m_final_norm, v_meta_tokens, v_ffn1_norm, v_ffn1_w_gate_up, v_ffn1_w_down, v_mix_norm, v_w_in, v_fox_forget_bias, v_ssd_conv_w, v_ssd_conv_b, v_ssd_dt_bias, v_ssd_a_log, v_ssd_d, v_ssd_norm, v_lru_conv_w, v_lru_conv_b, v_lru_w_a, v_lru_b_a, v_lru_w_x, v_lru_b_x, v_lru_lambda, v_w_branch_attn, v_w_branch_ssd, v_w_branch_lru, v_w_out, v_ffn2_norm, v_ffn2_w_gate_up, v_ffn2_w_down, v_final_norm):
    given = dict(x=x, meta_tokens=meta_tokens, ffn1_norm=ffn1_norm, ffn1_w_gate_up=ffn1_w_gate_up, ffn1_w_down=ffn1_w_down, mix_norm=mix_norm, w_in=w_in, fox_forget_bias=fox_forget_bias, ssd_conv_w=ssd_conv_w, ssd_conv_b=ssd_conv_b, ssd_dt_bias=ssd_dt_bias, ssd_a_log=ssd_a_log, ssd_d=ssd_d, ssd_norm=ssd_norm, lru_conv_w=lru_conv_w, lru_conv_b=lru_conv_b, lru_w_a=lru_w_a, lru_b_a=lru_b_a, lru_w_x=lru_w_x, lru_b_x=lru_b_x, lru_lambda=lru_lambda, w_branch_attn=w_branch_attn, w_branch_ssd=w_branch_ssd, w_branch_lru=w_branch_lru, w_out=w_out, ffn2_norm=ffn2_norm, ffn2_w_gate_up=ffn2_w_gate_up, ffn2_w_down=ffn2_w_down, final_norm=final_norm, loss_target=loss_target, m_meta_tokens=m_meta_tokens, m_ffn1_norm=m_ffn1_norm, m_ffn1_w_gate_up=m_ffn1_w_gate_up, m_ffn1_w_down=m_ffn1_w_down, m_mix_norm=m_mix_norm, m_w_in=m_w_in, m_fox_forget_bias=m_fox_forget_bias, m_ssd_conv_w=m_ssd_conv_w, m_ssd_conv_b=m_ssd_conv_b, m_ssd_dt_bias=m_ssd_dt_bias, m_ssd_a_log=m_ssd_a_log, m_ssd_d=m_ssd_d, m_ssd_norm=m_ssd_norm, m_lru_conv_w=m_lru_conv_w, m_lru_conv_b=m_lru_conv_b, m_lru_w_a=m_lru_w_a, m_lru_b_a=m_lru_b_a, m_lru_w_x=m_lru_w_x, m_lru_b_x=m_lru_b_x, m_lru_lambda=m_lru_lambda, m_w_branch_attn=m_w_branch_attn, m_w_branch_ssd=m_w_branch_ssd, m_w_branch_lru=m_w_branch_lru, m_w_out=m_w_out, m_ffn2_norm=m_ffn2_norm, m_ffn2_w_gate_up=m_ffn2_w_gate_up, m_ffn2_w_down=m_ffn2_w_down, m_final_norm=m_final_norm, v_meta_tokens=v_meta_tokens, v_ffn1_norm=v_ffn1_norm, v_ffn1_w_gate_up=v_ffn1_w_gate_up, v_ffn1_w_down=v_ffn1_w_down, v_mix_norm=v_mix_norm, v_w_in=v_w_in, v_fox_forget_bias=v_fox_forget_bias, v_ssd_conv_w=v_ssd_conv_w, v_ssd_conv_b=v_ssd_conv_b, v_ssd_dt_bias=v_ssd_dt_bias, v_ssd_a_log=v_ssd_a_log, v_ssd_d=v_ssd_d, v_ssd_norm=v_ssd_norm, v_lru_conv_w=v_lru_conv_w, v_lru_conv_b=v_lru_conv_b, v_lru_w_a=v_lru_w_a, v_lru_b_a=v_lru_b_a, v_lru_w_x=v_lru_w_x, v_lru_b_x=v_lru_b_x, v_lru_lambda=v_lru_lambda, v_w_branch_attn=v_w_branch_attn, v_w_branch_ssd=v_w_branch_ssd, v_w_branch_lru=v_w_branch_lru, v_w_out=v_w_out, v_ffn2_norm=v_ffn2_norm, v_ffn2_w_gate_up=v_ffn2_w_gate_up, v_ffn2_w_down=v_ffn2_w_down, v_final_norm=v_final_norm)
    weights = {n: given[n] for n in TWIN_WEIGHTS}
    shared = {n: given[n] for n in SHARED_INPUTS}
    per_example = {n: given[n] for n in ['x']}
    grad_fn = _jax.value_and_grad(_loss, argnums=(0, 1))

    def one_microbatch(ex, loss_target):
        ex = dict(ex)
        diff = ex.pop(TWIN_DIFF_INPUT)
        return grad_fn(weights, diff, {**shared, **ex}, loss_target)

    if N_MICROBATCH == 1:
        loss, (grad_w, grad_x) = one_microbatch(per_example, given["loss_target"])
    else:
        def body(carry, xs):
            loss_sum, grad_sum = carry
            l_k, (gw_k, gx_k) = one_microbatch(xs[0], xs[1])
            with _jax.named_scope("update"):
                return (loss_sum + l_k, _jax.tree.map(_jnp.add, grad_sum, gw_k)), gx_k

        init = (_jnp.zeros((), _jnp.float32), _jax.tree.map(_jnp.zeros_like, weights))
        (loss, grad_w), grad_x = _jax.lax.scan(body, init, (per_example, given["loss_target"]))
    with _jax.named_scope("update"):
        delta_w, new_m, new_v = {}, {}, {}
        for n in TWIN_WEIGHTS:
            delta_w[n], new_m[n], new_v[n] = _adamw(weights[n], grad_w[n], given["m_" + n], given["v_" + n])
    return (loss, grad_x, *[grad_w[n] for n in TWIN_WEIGHTS], *[delta_w[n] for n in TWIN_WEIGHTS],
            *[new_m[n] for n in TWIN_WEIGHTS], *[new_v[n] for n in TWIN_WEIGHTS])
```

```python
import functools
import math

import numpy as np
import jax
import jax.numpy as jnp
from jax import lax
from jax.experimental import pallas as pl
from jax.experimental.pallas import tpu as pltpu

F32 = jnp.float32
BF16 = jnp.bfloat16
HI = lax.Precision.HIGHEST
VMEM_LIMIT_BYTES = 56 * 1024 * 1024
NEG = -1e30

D_MODEL = 1024
SEQ = 4096
DEPTH = 4
N_META = 16
Q_BLOCK = 128
SSD_CHUNK = 128
NORM_EPS = 1e-6
ATTN_HEADS = 16
ATTN_HEAD_DIM = 64
SSD_HEAD_DIM = 64
SSD_GROUPS = 2
SSD_STATE = 128
CONV_K = 4
LRU_BLOCKS = 16
LRU_C = 8.0
D_FF = 2816
ADAM_LR = 0.001
ADAM_B1 = 0.9
ADAM_B2 = 0.999
ADAM_EPS = 1e-08
ADAM_WD = 0.01
ADAM_STEP = 10
SMALL_W = 128
_ROWWISE_TILE_ELEMS = 512 * 1024

WEIGHTS = ['meta_tokens', 'ffn1_norm', 'ffn1_w_gate_up', 'ffn1_w_down', 'mix_norm', 'w_in', 'fox_forget_bias',
           'ssd_conv_w', 'ssd_conv_b', 'ssd_dt_bias', 'ssd_a_log', 'ssd_d', 'ssd_norm', 'lru_conv_w', 'lru_conv_b',
           'lru_w_a', 'lru_b_a', 'lru_w_x', 'lru_b_x', 'lru_lambda', 'w_branch_attn', 'w_branch_ssd', 'w_branch_lru',
           'w_out', 'ffn2_norm', 'ffn2_w_gate_up', 'ffn2_w_down', 'final_norm']
BIG = {'ffn1_w_gate_up': 1, 'ffn1_w_down': 0, 'w_in': 1, 'w_branch_attn': 0, 'w_branch_ssd': 0, 'w_branch_lru': 0,
       'w_out': 0, 'ffn2_w_gate_up': 1, 'ffn2_w_down': 0}
BIG_NAMES = [n for n in WEIGHTS if n in BIG]
COLSHARD_SMALL = ['meta_tokens', 'ssd_conv_w', 'lru_conv_w']
SMALL_NAMES = [n for n in WEIGHTS if n not in BIG]


def _pick(n, cands):
    for c in cands:
        if n % c == 0:
            return c
    raise ValueError(f"no tile for {n} in {cands}")


def _pcall(body, **kw):
    return pl.pallas_call(body, **kw)


def _cparams(sem):
    return pltpu.CompilerParams(dimension_semantics=sem, vmem_limit_bytes=VMEM_LIMIT_BYTES)


def _sds(shape, dtype):
    return jax.ShapeDtypeStruct(tuple(shape), dtype)


def _dot(a, b, hi=False):
    return jnp.dot(a, b, precision=HI if hi else None, preferred_element_type=F32)


def _dot_nt(a, b):
    return lax.dot_general(a, b, (((1,), (1,)), ((), ())), preferred_element_type=F32)


def _dot_tn(a, b):
    return lax.dot_general(a, b, (((0,), (0,)), ((), ())), preferred_element_type=F32)


def _sigmoid(x):
    return 1.0 / (1.0 + jnp.exp(-x))


def _softplus(x):
    return jnp.maximum(x, 0.0) + jnp.log1p(jnp.exp(-jnp.abs(x)))


def _silu(x):
    return x * _sigmoid(x)


def _dsilu(x):
    s = _sigmoid(x)
    return s * (1.0 + x * (1.0 - s))


_GELU_C = math.sqrt(2.0 / math.pi)


def _gelu(x):
    return 0.5 * x * (1.0 + jnp.tanh(_GELU_C * (x + 0.044715 * x * x * x)))


def _dgelu(x):
    t = jnp.tanh(_GELU_C * (x + 0.044715 * x * x * x))
    return 0.5 * (1.0 + t) + 0.5 * x * (1.0 - t * t) * _GELU_C * (1.0 + 3.0 * 0.044715 * x * x)


def _expm1(x):
    series = x * (1.0 + x * 0.5 * (1.0 + x * (1.0 / 3.0) * (1.0 + x * 0.25 * (1.0 + x * 0.2))))
    return jnp.where(jnp.abs(x) < 0.05, series, jnp.exp(x) - 1.0)


def _rowwise(fn, ins, outs, *, bcast=(), reds=(), tm=None, name, period=None):
    t_rows = ins[0].shape[0]
    if tm is None:
        widest = max([a.shape[1] for a in ins] + [c for c, _ in outs])
        tm = _pick(math.gcd(t_rows, period or t_rows),
                   [c for c in (384, 256, 128, 64, 32, 16, 8) if c * widest <= _ROWWISE_TILE_ELEMS or c == 8])
    nt = t_rows // tm
    assert t_rows % tm == 0 and (period is None or period % tm == 0)
    n_in, n_out = len(ins) + len(bcast), len(outs)

    def body(*refs):
        i = pl.program_id(0)
        pos = None
        if period is not None:
            pos = (i * tm) % period + lax.broadcasted_iota(jnp.int32, (tm, 1), 0)
        res = fn(pos, *[r[...] for r in refs[:n_in]])
        res = res if isinstance(res, tuple) else (res,)
        for r, v in zip(refs[n_in:n_in + n_out], res[:n_out]):
            r[...] = v.astype(r.dtype)
        red_refs = refs[n_in + n_out:]
        if red_refs:
            @pl.when(i == 0)
            def _():
                for r in red_refs:
                    r[...] = jnp.zeros_like(r)
            for r, v in zip(red_refs, res[n_out:]):
                r[...] += v

    in_specs = [pl.BlockSpec((tm, a.shape[1]), lambda i: (i, 0)) for a in ins]
    in_specs += [pl.BlockSpec(b.shape, lambda i, n=b.ndim: (0,) * n) for b in bcast]
    out_specs = [pl.BlockSpec((tm, c), lambda i: (i, 0)) for c, _ in outs]
    out_specs += [pl.BlockSpec(s, lambda i: (0, 0)) for s in reds]
    out_shape = [_sds((t_rows, c), dt) for c, dt in outs] + [_sds(s, F32) for s in reds]
    res = _pcall(body, name=name, grid=(nt,), in_specs=in_specs, out_specs=out_specs, out_shape=out_shape,
                 compiler_params=_cparams(("arbitrary",) if reds else ("parallel",)))(*ins, *bcast)
    return res


_TM = (768, 384, 256, 128)
_TN = (512, 640, 384, 256, 128)
_TK = (1024, 1408, 512, 384, 256, 128)


def _mm_nn(a, b, out_dtype, *, res=None, alpha=1.0, name):
    m, k = a.shape
    k2, n = b.shape
    assert k == k2
    tm, tn, tk = _pick(m, _TM), _pick(n, _TN), _pick(k, _TK)
    nk = k // tk

    def body(*refs):
        if res is None:
            a_ref, b_ref, o_ref, acc = refs
            r_ref = None
        else:
            a_ref, b_ref, r_ref, o_ref, acc = refs
        kk = pl.program_id(2)

        @pl.when(kk == 0)
        def _():
            acc[...] = jnp.zeros_like(acc)

        acc[...] += _dot(a_ref[...].astype(BF16), b_ref[...].astype(BF16))

        @pl.when(kk == nk - 1)
        def _():
            v = acc[...]
            if alpha != 1.0:
                v = v * alpha
            if r_ref is not None:
                v = r_ref[...].astype(F32) + v
            o_ref[...] = v.astype(o_ref.dtype)

    in_specs = [pl.BlockSpec((tm, tk), lambda j, i, kk: (i, kk)), pl.BlockSpec((tk, tn), lambda j, i, kk: (kk, j))]
    args = [a, b]
    if res is not None:
        in_specs.append(pl.BlockSpec((tm, tn), lambda j, i, kk: (i, j)))
        args.append(res)
    return _pcall(body, name=name, grid=(n // tn, m // tm, nk), in_specs=in_specs,
                  out_specs=pl.BlockSpec((tm, tn), lambda j, i, kk: (i, j)), out_shape=_sds((m, n), out_dtype),
                  scratch_shapes=[pltpu.VMEM((tm, tn), F32)],
                  compiler_params=_cparams(("parallel", "parallel", "arbitrary")))(*args)


def _mm_tn(a, b, *, alpha=1.0, name):
    m, k = a.shape
    m2, n = b.shape
    assert m == m2
    tm, tn, tko = _pick(m, _TM), _pick(n, _TN), _pick(k, _TK)
    nm = m // tm

    def body(a_ref, b_ref, o_ref, acc):
        mm = pl.program_id(2)

        @pl.when(mm == 0)
        def _():
            acc[...] = jnp.zeros_like(acc)

        acc[...] += _dot_tn(a_ref[...].astype(BF16), b_ref[...].astype(BF16))

        @pl.when(mm == nm - 1)
        def _():
            v = acc[...]
            o_ref[...] = v * alpha if alpha != 1.0 else v

    return _pcall(body, name=name, grid=(k // tko, n // tn, nm),
                  in_specs=[pl.BlockSpec((tm, tko), lambda i, j, mm: (mm, i)),
                            pl.BlockSpec((tm, tn), lambda i, j, mm: (mm, j))],
                  out_specs=pl.BlockSpec((tko, tn), lambda i, j, mm: (i, j)), out_shape=_sds((k, n), F32),
                  scratch_shapes=[pltpu.VMEM((tko, tn), F32)],
                  compiler_params=_cparams(("parallel", "parallel", "arbitrary")))(a, b)


def _rms_fwd(h, g, name):
    def fn(_, hv, gv):
        r = lax.rsqrt(jnp.mean(hv * hv, axis=1, keepdims=True) + NORM_EPS)
        return hv * r * gv
    return _rowwise(fn, [h], [(h.shape[1], BF16)], bcast=[g], name=name)[0]


def _rms_bwd(h, dxn, dres, g, name):
    d = h.shape[1]

    def fn(_, hv, dv, rv, gv):
        r = lax.rsqrt(jnp.mean(hv * hv, axis=1, keepdims=True) + NORM_EPS)
        xh = hv * r
        dxh = dv * gv
        dh = r * (dxh - xh * jnp.mean(dxh * xh, axis=1, keepdims=True))
        return rv + dh, jnp.sum(dv * xh, axis=0, keepdims=True)
    return _rowwise(fn, [h, dxn, dres], [(d, F32)], bcast=[g], reds=[(1, d)], name=name)


def _swiglu_fwd(gu, name):
    f = gu.shape[1] // 2

    def fn(_, v):
        v = v.astype(F32)
        return _silu(v[:, :f]) * v[:, f:]
    return _rowwise(fn, [gu], [(f, BF16)], name=name)[0]


def _swiglu_bwd(gu, dact, name):
    f = gu.shape[1] // 2

    def fn(_, v, dv):
        v = v.astype(F32)
        dv = dv.astype(F32)
        g, u = v[:, :f], v[:, f:]
        return jnp.concatenate([dv * u * _dsilu(g), dv * _silu(g)], axis=1)
    return _rowwise(fn, [gu, dact], [(2 * f, BF16)], name=name)[0]


def _merge_fwd(mg, ba, bb, bc, name):
    d = ba.shape[1]

    def fn(_, m, a, b, c):
        g = _sigmoid(m.astype(F32))
        return g[:, :d] * a.astype(F32) + g[:, d:2 * d] * b.astype(F32) + g[:, 2 * d:] * c.astype(F32)
    return _rowwise(fn, [mg, ba, bb, bc], [(d, BF16)], name=name)[0]


def _merge_bwd(mg, ba, bb, bc, dmix, name):
    d = ba.shape[1]

    def fn(_, m, a, b, c, dm):
        g = _sigmoid(m.astype(F32))
        dm = dm.astype(F32)
        br = (a.astype(F32), b.astype(F32), c.astype(F32))
        douts, dgs = [], []
        for j in range(3):
            gj = g[:, j * d:(j + 1) * d]
            douts.append(dm * gj)
            dgs.append(dm * br[j] * gj * (1.0 - gj))
        return (*douts, jnp.concatenate(dgs, axis=1))
    return _rowwise(fn, [mg, ba, bb, bc, dmix], [(d, BF16)] * 3 + [(3 * d, BF16)], name=name)


def _loss_head(h, tgt, g, seq_len, name):
    d = h.shape[1]

    def fn(pos, hv, tv, gv):
        r = lax.rsqrt(jnp.mean(hv * hv, axis=1, keepdims=True) + NORM_EPS)
        xh = hv * r
        real = (pos >= N_META) & (pos < N_META + SEQ)
        e = jnp.where(real, xh * gv - tv, 0.0)
        part = jnp.sum(jnp.sum(e * e, axis=1, keepdims=True), axis=0, keepdims=True) * (0.5 / d)
        dy = e * (1.0 / d)
        dxh = dy * gv
        dh = r * (dxh - xh * jnp.mean(dxh * xh, axis=1, keepdims=True))
        return dh, jnp.broadcast_to(part, (1, 128)), jnp.sum(dy * xh, axis=0, keepdims=True)
    return _rowwise(fn, [h, tgt], [(d, F32)], bcast=[g], reds=[(1, 128), (1, d)], name=name, period=seq_len)


def _adamw(w, g, m, v, name):
    c1 = 1.0 - ADAM_B1 ** ADAM_STEP
    c2 = 1.0 - ADAM_B2 ** ADAM_STEP
    wd = w.shape[1]

    def fn(_, wv, gv, mv, vv):
        mn = ADAM_B1 * mv + (1.0 - ADAM_B1) * gv
        vn = ADAM_B2 * vv + (1.0 - ADAM_B2) * (gv * gv)
        delta = -ADAM_LR * ((mn / c1) / (jnp.sqrt(vn / c2) + ADAM_EPS) + ADAM_WD * wv)
        return delta, mn, vn
    return _rowwise(fn, [w, g, m, v], [(wd, F32)] * 3, name=name, tm=_pick(w.shape[0], (256, 128, 64, 32, 16, 8)))


def _sum_rows(parts, out_dtype, name):
    def fn(_, *vs):
        acc = vs[0].astype(F32)
        for v in vs[1:]:
            acc = acc + v.astype(F32)
        return acc
    return _rowwise(fn, list(parts), [(parts[0].shape[1], out_dtype)], name=name,
                    tm=_pick(parts[0].shape[0], (256, 128, 64, 32, 16, 8)))[0]


def _cumsum_seq(x, reverse, name):
    b, l, w = x.shape
    q = 128
    nc = l // q

    def body(x_ref, o_ref):
        row = lax.broadcasted_iota(jnp.int32, (q, q), 0)
        col = lax.broadcasted_iota(jnp.int32, (q, q), 1)
        tri = ((row <= col) if reverse else (row >= col)).astype(F32)
        rsel = lax.broadcasted_iota(jnp.int32, (q, w), 0) == (0 if reverse else q - 1)

        def step(i, carry):
            j = (nc - 1 - i) if reverse else i
            start = pl.multiple_of(j * q, q)
            cs = _dot(tri, x_ref[pl.ds(start, q), :], hi=True) + carry
            o_ref[pl.ds(start, q), :] = cs
            return jnp.sum(jnp.where(rsel, cs, 0.0), axis=0, keepdims=True)

        lax.fori_loop(0, nc, step, jnp.zeros((1, w), F32))

    return _pcall(body, name=name, grid=(b,), in_specs=[pl.BlockSpec((None, l, w), lambda i: (i, 0, 0))],
                  out_specs=pl.BlockSpec((None, l, w), lambda i: (i, 0, 0)), out_shape=_sds(x.shape, F32),
                  compiler_params=_cparams(("parallel",)))(x)


_HALO = 16


def _conv_tiles(l, c):
    return _pick(l, (384, 256, 128)), _pick(c, (512, 256, 128))


def _conv_fwd(x, w, bias, out_dtype, name):
    b, l, c = x.shape
    tt, cw = _conv_tiles(l, c)

    def body(x_ref, h_ref, w_ref, b_ref, o_ref):
        t = pl.program_id(2)
        halo = jnp.where(t == 0, 0.0, h_ref[...].astype(F32))
        xe = jnp.concatenate([halo, x_ref[...].astype(F32)], axis=0)
        wv = w_ref[...]
        acc = b_ref[...] + wv[CONV_K - 1:CONV_K, :] * xe[_HALO:]
        for j in range(CONV_K - 1):
            acc = acc + wv[j:j + 1, :] * pltpu.roll(xe, CONV_K - 1 - j, 0)[_HALO:]
        o_ref[...] = acc.astype(o_ref.dtype)

    return _pcall(body, name=name, grid=(b, c // cw, l // tt),
                  in_specs=[pl.BlockSpec((None, tt, cw), lambda i, j, t: (i, t, j)),
                            pl.BlockSpec((None, _HALO, cw), lambda i, j, t: (i, jnp.maximum(t * (tt // _HALO) - 1, 0), j)),
                            pl.BlockSpec((CONV_K, cw), lambda i, j, t: (0, j)),
                            pl.BlockSpec((1, cw), lambda i, j, t: (0, j))],
                  out_specs=pl.BlockSpec((None, tt, cw), lambda i, j, t: (i, t, j)), out_shape=_sds(x.shape, out_dtype),
                  compiler_params=_cparams(("parallel", "parallel", "parallel")))(x, x, w, bias)


def _conv_bwd(x, dy, w, name):
    b, l, c = x.shape
    tt, cw = _conv_tiles(l, c)
    nt = l // tt

    def body(x_ref, xh_ref, d_ref, dh_ref, w_ref, dx_ref, dw_ref):
        i, t = pl.program_id(1), pl.program_id(2)
        halo = jnp.where(t == 0, 0.0, xh_ref[...].astype(F32))
        xe = jnp.concatenate([halo, x_ref[...].astype(F32)], axis=0)
        dv = d_ref[...].astype(F32)
        nxt = jnp.where(t == nt - 1, 0.0, dh_ref[...].astype(F32))
        de = jnp.concatenate([dv, nxt], axis=0)
        wv = w_ref[...]
        dx = wv[CONV_K - 1:CONV_K, :] * dv
        rowid = lax.broadcasted_iota(jnp.int32, (8, 1), 0)
        part = jnp.where(rowid == CONV_K, jnp.sum(dv, axis=0, keepdims=True), 0.0)
        part = part + jnp.where(rowid == CONV_K - 1, jnp.sum(dv * xe[_HALO:], axis=0, keepdims=True), 0.0)
        for j in range(CONV_K - 1):
            s = CONV_K - 1 - j
            dx = dx + wv[j:j + 1, :] * pltpu.roll(de, tt + _HALO - s, 0)[:tt]
            xs = pltpu.roll(xe, s, 0)[_HALO:]
            part = part + jnp.where(rowid == j, jnp.sum(dv * xs, axis=0, keepdims=True), 0.0)
        dx_ref[...] = dx.astype(dx_ref.dtype)

        @pl.when((i == 0) & (t == 0))
        def _():
            dw_ref[...] = jnp.zeros_like(dw_ref)
        dw_ref[...] += part

    return _pcall(body, name=name, grid=(c // cw, b, nt),
                  in_specs=[pl.BlockSpec((None, tt, cw), lambda j, i, t: (i, t, j)),
                            pl.BlockSpec((None, _HALO, cw), lambda j, i, t: (i, jnp.maximum(t * (tt // _HALO) - 1, 0), j)),
                            pl.BlockSpec((None, tt, cw), lambda j, i, t: (i, t, j)),
                            pl.BlockSpec((None, _HALO, cw),
                                         lambda j, i, t: (i, jnp.minimum((t + 1) * (tt // _HALO), l // _HALO - 1), j)),
                            pl.BlockSpec((CONV_K, cw), lambda j, i, t: (0, j))],
                  out_specs=[pl.BlockSpec((None, tt, cw), lambda j, i, t: (i, t, j)),
                             pl.BlockSpec((8, cw), lambda j, i, t: (0, j))],
                  out_shape=[_sds(x.shape, BF16), _sds((8, c), F32)],
                  compiler_params=_cparams(("parallel", "arbitrary", "arbitrary")))(x, x, dy, dy, w)


def _linear_scan(a, u, reverse, name):
    b, l, c = a.shape
    tt = 128
    cw = _pick(c, (512, 256, 128))
    nt = l // tt

    def body(a_ref, u_ref, h_ref, carry):
        t = pl.program_id(2)

        @pl.when(t == 0)
        def _():
            carry[...] = jnp.zeros_like(carry)

        av, uv = a_ref[...], u_ref[...]
        row = lax.broadcasted_iota(jnp.int32, (tt, cw), 0)
        k = 1
        while k < tt:
            if reverse:
                keep = row < tt - k
                a_sh = jnp.where(keep, pltpu.roll(av, tt - k, 0), 1.0)
                u_sh = jnp.where(keep, pltpu.roll(uv, tt - k, 0), 0.0)
            else:
                keep = row >= k
                a_sh = jnp.where(keep, pltpu.roll(av, k, 0), 1.0)
                u_sh = jnp.where(keep, pltpu.roll(uv, k, 0), 0.0)
            uv = uv + av * u_sh
            av = av * a_sh
            k *= 2
        hv = uv + av * carry[0:1, :]
        h_ref[...] = hv
        edge = jnp.sum(jnp.where(row == (0 if reverse else tt - 1), hv, 0.0), axis=0, keepdims=True)
        carry[...] = jnp.broadcast_to(edge, carry.shape)

    tmap = (lambda i, j, t: (i, nt - 1 - t, j)) if reverse else (lambda i, j, t: (i, t, j))
    spec = pl.BlockSpec((None, tt, cw), tmap)
    return _pcall(body, name=name, grid=(b, c // cw, nt), in_specs=[spec, spec], out_specs=spec,
                  out_shape=_sds(a.shape, F32), scratch_shapes=[pltpu.VMEM((8, cw), F32)],
                  compiler_params=_cparams(("parallel", "parallel", "arbitrary")))(a, u)


def _attn_blk(l):
    return _pick(l, (384, 256, 128))


def _flash_fwd(q, k, v, cq, ck, name):
    b, h, l, dh = q.shape
    blk = ck.shape[-1]
    nb = l // blk
    scale = dh ** -0.5
    kr, vr = k.reshape(b, h, nb, blk, dh), v.reshape(b, h, nb, blk, dh)

    def body(q_ref, k_ref, v_ref, cq_ref, ck_ref, o_ref, lse_ref):
        i = pl.program_id(2)
        qv, cqv = q_ref[...], cq_ref[...]

        def scores(j):
            return _dot_nt(qv, k_ref[j]) * scale + (cqv - ck_ref[j])

        def update(s, j, carry):
            m, lsum, acc = carry
            mn = jnp.maximum(m, jnp.max(s, axis=1, keepdims=True))
            p = jnp.exp(s - mn)
            al = jnp.exp(m - mn)
            return mn, al * lsum + jnp.sum(p, axis=1, keepdims=True), al * acc + _dot(p.astype(BF16), v_ref[j])

        init = (jnp.full((blk, 1), NEG, F32), jnp.zeros((blk, 1), F32), jnp.zeros((blk, dh), F32))
        carry = lax.fori_loop(0, i, lambda j, c: update(scores(j), j, c), init)
        row = lax.broadcasted_iota(jnp.int32, (blk, blk), 0)
        col = lax.broadcasted_iota(jnp.int32, (blk, blk), 1)
        m, lsum, acc = update(jnp.where(col <= row, scores(i), NEG), i, carry)
        o_ref[...] = (acc / lsum).astype(o_ref.dtype)
        lse_ref[...] = m + jnp.log(lsum)

    blk4 = lambda w: pl.BlockSpec((None, None, blk, w), lambda bi, hi, i: (bi, hi, i, 0))
    full5 = lambda r, w: pl.BlockSpec((None, None, nb, r, w), lambda bi, hi, i: (bi, hi, 0, 0, 0))
    return _pcall(body, name=name, grid=(b, h, nb),
                  in_specs=[blk4(dh), full5(blk, dh), full5(blk, dh), blk4(1), full5(1, blk)],
                  out_specs=[blk4(dh), blk4(1)], out_shape=[_sds(q.shape, BF16), _sds((b, h, l, 1), F32)],
                  compiler_params=_cparams(("parallel", "parallel", "parallel")))(q, kr, vr, cq, ck)


def _flash_bwd_kv(q, k, v, do, cq, ck, lse, dd, name):
    b, h, l, dh = q.shape
    blk = ck.shape[-1]
    nb = l // blk
    scale = dh ** -0.5
    r5 = lambda t: t.reshape(b, h, nb, blk, t.shape[-1])

    def body(k_ref, v_ref, ck_ref, q_ref, do_ref, cq_ref, lse_ref, dd_ref, dk_ref, dv_ref, dc_ref):
        j = pl.program_id(2)
        kv, vv, ckv = k_ref[...], v_ref[...], ck_ref[...]
        row = lax.broadcasted_iota(jnp.int32, (blk, blk), 0)
        col = lax.broadcasted_iota(jnp.int32, (blk, blk), 1)

        def contrib(i, masked, carry):
            dk, dv, dc = carry
            qv, dov = q_ref[i], do_ref[i]
            s = _dot_nt(qv, kv) * scale + (cq_ref[i] - ckv)
            p = jnp.exp(s - lse_ref[i])
            if masked:
                p = jnp.where(col <= row, p, 0.0)
            ds = p * (_dot_nt(dov, vv) - dd_ref[i])
            dv = dv + _dot_tn(p.astype(BF16), dov)
            dk = dk + _dot_tn(ds.astype(BF16), qv) * scale
            return dk, dv, dc + jnp.sum(ds, axis=0, keepdims=True)

        zero = (jnp.zeros((blk, dh), F32), jnp.zeros((blk, dh), F32), jnp.zeros((1, blk), F32))
        carry = contrib(j, True, zero)
        dk, dv, dc = lax.fori_loop(j + 1, nb, lambda i, c: contrib(i, False, c), carry)
        dk_ref[...] = dk
        dv_ref[...] = dv
        dc_ref[...] = -dc

    blk4 = lambda w: pl.BlockSpec((None, None, blk, w), lambda bi, hi, j: (bi, hi, j, 0))
    full5 = lambda r, w: pl.BlockSpec((None, None, nb, r, w), lambda bi, hi, j: (bi, hi, 0, 0, 0))
    ckspec = pl.BlockSpec((None, None, None, 1, blk), lambda bi, hi, j: (bi, hi, j, 0, 0))
    return _pcall(body, name=name, grid=(b, h, nb),
                  in_specs=[blk4(dh), blk4(dh), ckspec, full5(blk, dh), full5(blk, dh), full5(blk, 1), full5(blk, 1),
                            full5(blk, 1)],
                  out_specs=[blk4(dh), blk4(dh), ckspec],
                  out_shape=[_sds(q.shape, F32), _sds(q.shape, F32), _sds(ck.shape, F32)],
                  compiler_params=_cparams(("parallel", "parallel", "parallel")))(
                      k, v, ck, r5(q), r5(do), r5(cq), r5(lse), r5(dd))


def _flash_bwd_q(q, k, v, do, cq, ck, lse, dd, name):
    b, h, l, dh = q.shape
    blk = ck.shape[-1]
    nb = l // blk
    scale = dh ** -0.5
    kr, vr = k.reshape(b, h, nb, blk, dh), v.reshape(b, h, nb, blk, dh)

    def body(q_ref, do_ref, cq_ref, lse_ref, dd_ref, k_ref, v_ref, ck_ref, dq_ref, dc_ref):
        i = pl.program_id(2)
        qv, dov, cqv, lsev, ddv = q_ref[...], do_ref[...], cq_ref[...], lse_ref[...], dd_ref[...]
        row = lax.broadcasted_iota(jnp.int32, (blk, blk), 0)
        col = lax.broadcasted_iota(jnp.int32, (blk, blk), 1)

        def contrib(j, masked, carry):
            dq, dc = carry
            s = _dot_nt(qv, k_ref[j]) * scale + (cqv - ck_ref[j])
            p = jnp.exp(s - lsev)
            if masked:
                p = jnp.where(col <= row, p, 0.0)
            ds = p * (_dot_nt(dov, v_ref[j]) - ddv)
            return dq + _dot(ds.astype(BF16), k_ref[j]) * scale, dc + jnp.sum(ds, axis=1, keepdims=True)

        zero = (jnp.zeros((blk, dh), F32), jnp.zeros((blk, 1), F32))
        dq, dc = contrib(i, True, lax.fori_loop(0, i, lambda j, c: contrib(j, False, c), zero))
        dq_ref[...] = dq
        dc_ref[...] = dc

    blk4 = lambda w: pl.BlockSpec((None, None, blk, w), lambda bi, hi, i: (bi, hi, i, 0))
    full5 = lambda r, w: pl.BlockSpec((None, None, nb, r, w), lambda bi, hi, i: (bi, hi, 0, 0, 0))
    return _pcall(body, name=name, grid=(b, h, nb),
                  in_specs=[blk4(dh), blk4(dh), blk4(1), blk4(1), blk4(1), full5(blk, dh), full5(blk, dh), full5(1, blk)],
                  out_specs=[blk4(dh), blk4(1)], out_shape=[_sds(q.shape, F32), _sds((b, h, l, 1), F32)],
                  compiler_params=_cparams(("parallel", "parallel", "parallel")))(q, do, cq, lse, dd, kr, vr, ck)


def _ssd_dims(d_ssd):
    heads = d_ssd // SSD_HEAD_DIM
    return heads, heads // SSD_GROUPS, d_ssd // SSD_GROUPS


def _ssd_specs(l, ds, seq_map):
    q = SSD_CHUNK
    gn = SSD_GROUPS * SSD_STATE
    row3 = lambda w, cb: pl.BlockSpec((None, q, w), lambda i, c, cb=cb: (i, seq_map(c), cb))
    return dict(
        xs=row3(ds, 0), bm=row3(gn, ds // gn), cm=row3(gn, ds // gn + 1), z=row3(ds, 0), dt=row3(SMALL_W, 0),
        da=row3(SMALL_W, 0), dat=pl.BlockSpec((None, SMALL_W, q), lambda i, c: (i, 0, seq_map(c))),
        e=pl.BlockSpec((SMALL_W, ds), lambda i, c: (0, 0)), et=pl.BlockSpec((ds, SMALL_W), lambda i, c: (0, 0)),
        vec=pl.BlockSpec((1, ds), lambda i, c: (0, 0)), vec128=pl.BlockSpec((1, SMALL_W), lambda i, c: (0, 0)),
        hin=pl.BlockSpec((None, None, SSD_STATE, ds), lambda i, c: (i, seq_map(c), 0, 0)))


def _ssd_common(da, dat, dt, e_mat, xs):
    q = SSD_CHUNK
    row = lax.broadcasted_iota(jnp.int32, (q, q), 0)
    col = lax.broadcasted_iota(jnp.int32, (q, q), 1)
    lower = row >= col
    cs = _dot(lower.astype(F32), da, hi=True)
    cst = _dot(dat, (row <= col).astype(F32), hi=True)
    dtx = _dot(dt, e_mat, hi=True)
    csx = _dot(cs, e_mat, hi=True)
    rowx = lax.broadcasted_iota(jnp.int32, csx.shape, 0)
    totx = jnp.sum(jnp.where(rowx == q - 1, csx, 0.0), axis=0, keepdims=True)
    xf = xs.astype(F32)
    return lower, cs, cst, dtx, csx, totx, xf, xf * dtx


def _ssd_fwd(xbc, z, dt, da, dat, e_mat, dx, nw, name):
    b, l, _ = xbc.shape
    ds = z.shape[2]
    heads, hpg, gw = _ssd_dims(ds)
    q, n = SSD_CHUNK, SSD_STATE
    nc = l // q
    hcol0 = ATTN_HEADS

    def body(xs_ref, bm_ref, cm_ref, z_ref, dt_ref, da_ref, dat_ref, e_ref, dx_ref, nw_ref, y_ref, yraw_ref, hin_ref,
             hst, ydiag):
        c = pl.program_id(1)

        @pl.when(c == 0)
        def _():
            hst[...] = jnp.zeros_like(hst)

        hin = hst[...]
        hin_ref[...] = hin
        lower, cs, cst, dtx, csx, totx, xf, xdt = _ssd_common(da_ref[...], dat_ref[...], dt_ref[...], e_ref[...],
                                                               xs_ref[...])
        bm, cm = bm_ref[...], cm_ref[...]
        dec_end = jnp.exp(totx - csx)
        for g in range(SSD_GROUPS):
            gs = slice(g * gw, (g + 1) * gw)
            bg, cg = bm[:, g * n:(g + 1) * n], cm[:, g * n:(g + 1) * n]
            cb = _dot_nt(cg, bg)
            for e in range(hpg):
                hh = g * hpg + e
                cc = hcol0 + hh
                lm = jnp.exp(jnp.where(lower, cs[:, cc:cc + 1] - cst[cc:cc + 1, :], NEG))
                hs = slice(hh * SSD_HEAD_DIM, (hh + 1) * SSD_HEAD_DIM)
                ydiag[:, hs] = _dot((cb * lm).astype(BF16), xdt[:, hs].astype(BF16))
            sg = _dot_tn(bg, (xdt[:, gs] * dec_end[:, gs]).astype(BF16))
            hst[:, gs] = jnp.exp(totx[:, gs]) * hin[:, gs] + sg
            ydiag[:, gs] += _dot(cg, hin[:, gs].astype(BF16)) * jnp.exp(csx[:, gs])
        yraw = ydiag[...] + dx_ref[...] * xf
        yraw_ref[...] = yraw.astype(yraw_ref.dtype)
        yg = yraw * _silu(z_ref[...].astype(F32))
        nwv = nw_ref[...]
        for g in range(SSD_GROUPS):
            gs = slice(g * gw, (g + 1) * gw)
            r = lax.rsqrt(jnp.mean(yg[:, gs] * yg[:, gs], axis=1, keepdims=True) + NORM_EPS)
            y_ref[:, gs] = (yg[:, gs] * r * nwv[:, gs]).astype(y_ref.dtype)

    sp = _ssd_specs(l, ds, lambda c: c)
    return _pcall(body, name=name, grid=(b, nc),
                  in_specs=[sp['xs'], sp['bm'], sp['cm'], sp['z'], sp['dt'], sp['da'], sp['dat'], sp['e'], sp['vec'],
                            sp['vec']],
                  out_specs=[sp['z'], sp['z'], sp['hin']],
                  out_shape=[_sds((b, l, ds), BF16), _sds((b, l, ds), BF16), _sds((b, nc, n, ds), F32)],
                  scratch_shapes=[pltpu.VMEM((n, ds), F32), pltpu.VMEM((q, ds), F32)],
                  compiler_params=_cparams(("parallel", "arbitrary")))(xbc, xbc, xbc, z, dt, da, dat, e_mat, dx, nw)


def _ssd_bwd(xbc, z, dt, da, dat, e_mat, et_mat, dx, nw, a128, yraw, hin, dy, name):
    b, l, dxw = xbc.shape
    ds = z.shape[2]
    heads, hpg, gw = _ssd_dims(ds)
    q, n = SSD_CHUNK, SSD_STATE
    gn = SSD_GROUPS * n
    nc = l // q
    hcol0 = ATTN_HEADS

    def body(xs_ref, bm_ref, cm_ref, z_ref, dt_ref, da_ref, dat_ref, e_ref, et_ref, dx_ref, nw_ref, a_ref, yraw_ref,
             hin_ref, dy_ref, dxs_ref, dbm_ref, dcm_ref, dz_ref, ddt_ref, dd_ref, dnw_ref, dap_ref, dhs, dxdt, dcsx,
             dtotx):
        i, c = pl.program_id(0), pl.program_id(1)

        @pl.when(c == 0)
        def _():
            dhs[...] = jnp.zeros_like(dhs)

        @pl.when((i == 0) & (c == 0))
        def _():
            dd_ref[...] = jnp.zeros_like(dd_ref)
            dnw_ref[...] = jnp.zeros_like(dnw_ref)
            dap_ref[...] = jnp.zeros_like(dap_ref)

        dtv = dt_ref[...]
        lower, cs, cst, dtx, csx, totx, xf, xdt = _ssd_common(da_ref[...], dat_ref[...], dtv, e_ref[...], xs_ref[...])
        upper = jnp.logical_not(lower) | (lax.broadcasted_iota(jnp.int32, (q, q), 0)
                                          == lax.broadcasted_iota(jnp.int32, (q, q), 1))
        bm, cm = bm_ref[...], cm_ref[...]
        ecs, dec_end, etot = jnp.exp(csx), jnp.exp(totx - csx), jnp.exp(totx)
        yraw = yraw_ref[...].astype(F32)
        zv = z_ref[...].astype(F32)
        sz = _silu(zv)
        yg = yraw * sz
        dyn_ = dy_ref[...].astype(F32)
        nwv = nw_ref[...]
        dygs, dnws = [], []
        for g in range(SSD_GROUPS):
            gs = slice(g * gw, (g + 1) * gw)
            r = lax.rsqrt(jnp.mean(yg[:, gs] * yg[:, gs], axis=1, keepdims=True) + NORM_EPS)
            yn = yg[:, gs] * r
            dn = dyn_[:, gs] * nwv[:, gs]
            dnws.append(jnp.sum(dyn_[:, gs] * yn, axis=0, keepdims=True))
            dygs.append(r * (dn - yn * jnp.mean(dn * yn, axis=1, keepdims=True)))
        dyg = jnp.concatenate(dygs, axis=1)
        dnw_ref[...] += jnp.concatenate(dnws, axis=1)
        dz_ref[...] = (dyg * yraw * _dsilu(zv)).astype(dz_ref.dtype)
        dyv = dyg * sz
        dd_ref[...] += jnp.sum(dyv * xf, axis=0, keepdims=True)
        hin, dh = hin_ref[...], dhs[...]
        lane128 = lax.broadcasted_iota(jnp.int32, (1, SMALL_W), 1)
        dcs = jnp.zeros((q, SMALL_W), F32)
        for g in range(SSD_GROUPS):
            gs = slice(g * gw, (g + 1) * gw)
            bg, cg = bm[:, g * n:(g + 1) * n], cm[:, g * n:(g + 1) * n]
            hg, dhg = hin[:, gs], dh[:, gs]
            hgb, dsb = hg.astype(BF16), dhg.astype(BF16)
            yoff = _dot(cg, hgb) * ecs[:, gs]
            dch = (dyv[:, gs] * ecs[:, gs]).astype(BF16)
            dcg = _dot_nt(dch, hgb)
            dhs[:, gs] = _dot_tn(cg, dch) + etot[:, gs] * dhg
            zg = xdt[:, gs] * dec_end[:, gs]
            dzz = _dot(bg, dsb)
            dbg = _dot_nt(zg.astype(BF16), dsb)
            dxdt_g = dzz * dec_end[:, gs]
            w_end = dzz * zg
            dtotx[:, gs] = jnp.sum(dhg * hg, axis=0, keepdims=True) * etot[:, gs] + jnp.sum(w_end, axis=0, keepdims=True)
            dcsx[:, gs] = dyv[:, gs] * yoff - w_end
            cb, cbt = _dot_nt(cg, bg), _dot_nt(bg, cg)
            dgm = jnp.zeros((q, q), F32)
            for e in range(hpg):
                hh = g * hpg + e
                cc = hcol0 + hh
                ccol, crow = cs[:, cc:cc + 1], cst[cc:cc + 1, :]
                lm = jnp.exp(jnp.where(lower, ccol - crow, NEG))
                lmt = jnp.exp(jnp.where(upper, crow - ccol, NEG))
                mm, mt = cb * lm, cbt * lmt
                hs = slice(hh * SSD_HEAD_DIM, (hh + 1) * SSD_HEAD_DIM)
                dye, xe = dyv[:, hs].astype(BF16), xdt[:, hs].astype(BF16)
                dm, dmt = _dot_nt(dye, xe), _dot_nt(xe, dye)
                dxdt[:, hs] = dxdt_g[:, e * SSD_HEAD_DIM:(e + 1) * SSD_HEAD_DIM] + _dot(mt.astype(BF16), dye)
                dgm = dgm + dm * lm
                rs = jnp.sum(dm * mm, axis=1, keepdims=True) - jnp.sum(dmt * mt, axis=1, keepdims=True)
                dcs = dcs + rs * (lane128 == cc).astype(F32)
            dgb = dgm.astype(BF16)
            dcm_ref[:, g * n:(g + 1) * n] = (dcg + _dot(dgb, bg)).astype(dcm_ref.dtype)
            dbm_ref[:, g * n:(g + 1) * n] = (dbg + _dot_tn(dgb, cg)).astype(dbm_ref.dtype)
        dxd = dxdt[...]
        dxs_ref[...] = (dx_ref[...] * dyv + dxd * dtx).astype(dxs_ref.dtype)
        et = et_ref[...]
        ddt = _dot(dxd * xf, et, hi=True)
        dtot128 = _dot(jnp.broadcast_to(dtotx[...], (8, ds)), et, hi=True)[0:1, :]
        row128 = lax.broadcasted_iota(jnp.int32, (q, SMALL_W), 0)
        dcs = dcs + _dot(dcsx[...], et, hi=True) + jnp.where(row128 == q - 1, dtot128, 0.0)
        dda = _dot(upper.astype(F32), dcs, hi=True)
        ddt_ref[...] = ddt + dda * a_ref[...]
        dap_ref[...] += jnp.sum(dda * dtv, axis=0, keepdims=True)

    rev = lambda c: nc - 1 - c
    sp = _ssd_specs(l, ds, rev)
    row3 = lambda w: pl.BlockSpec((None, q, w), lambda i, c: (i, rev(c), 0))
    acc = lambda w: pl.BlockSpec((1, w), lambda i, c: (0, 0))
    return _pcall(body, name=name, grid=(b, nc),
                  in_specs=[sp['xs'], sp['bm'], sp['cm'], sp['z'], sp['dt'], sp['da'], sp['dat'], sp['e'], sp['et'],
                            sp['vec'], sp['vec'], sp['vec128'], sp['z'], sp['hin'], sp['z']],
                  out_specs=[row3(ds), row3(gn), row3(gn), row3(ds), row3(SMALL_W), acc(ds), acc(ds), acc(SMALL_W)],
                  out_shape=[_sds((b, l, ds), BF16), _sds((b, l, gn), BF16), _sds((b, l, gn), BF16), _sds((b, l, ds), BF16),
                             _sds((b, l, SMALL_W), F32), _sds((1, ds), F32), _sds((1, ds), F32), _sds((1, SMALL_W), F32)],
                  scratch_shapes=[pltpu.VMEM((n, ds), F32), pltpu.VMEM((q, ds), F32), pltpu.VMEM((q, ds), F32),
                                  pltpu.VMEM((1, ds), F32)],
                  compiler_params=_cparams(("arbitrary", "arbitrary")))(
                      xbc, xbc, xbc, z, dt, da, dat, e_mat, et_mat, dx, nw, a128, yraw, hin, dy)


_GROUP_SIZE = {'c': 2, 'xy': 4, 'xyc': 8}


def _exchange(src, group, scatter, name):
    n = _GROUP_SIZE[group]
    piece = src.shape[1:] if scatter else src.shape

    def body(src_ref, out_ref, send_sems, recv_sems, local_sem):
        x, y, c = lax.axis_index("x"), lax.axis_index("y"), lax.axis_index("c")
        if group == 'c':
            rank = c
            dev = lambda r: (x, y, r)
        elif group == 'xy':
            rank = 2 * x + y
            dev = lambda r: (r // 2, r % 2, c)
        else:
            rank = 4 * x + 2 * y + c
            dev = lambda r: (r // 4, (r // 2) % 2, r % 2)

        def mine_for(r):
            return src_ref.at[r] if scatter else src_ref

        local = pltpu.make_async_copy(mine_for(rank), out_ref.at[rank], local_sem)
        local.start()
        peers = [jnp.bitwise_xor(rank, k + 1) for k in range(n - 1)]
        copies = []
        for k, pr in enumerate(peers):
            cp = pltpu.make_async_remote_copy(src_ref=mine_for(pr), dst_ref=out_ref.at[rank], send_sem=send_sems.at[k],
                                              recv_sem=recv_sems.at[k], device_id=dev(pr),
                                              device_id_type=pl.DeviceIdType.MESH)
            cp.start()
            copies.append(cp)
        for k, pr in enumerate(peers):
            pltpu.make_async_remote_copy(src_ref=mine_for(pr), dst_ref=out_ref.at[pr], send_sem=send_sems.at[k],
                                         recv_sem=recv_sems.at[k], device_id=dev(pr),
                                         device_id_type=pl.DeviceIdType.MESH).wait_recv()
        for cp in copies:
            cp.wait_send()
        local.wait()

    return _pcall(body, name=name, in_specs=[pl.BlockSpec(memory_space=pl.ANY)],
                  out_specs=pl.BlockSpec(memory_space=pl.ANY), out_shape=_sds((n,) + tuple(piece), src.dtype),
                  scratch_shapes=[pltpu.SemaphoreType.DMA((n - 1,)), pltpu.SemaphoreType.DMA((n - 1,)),
                                  pltpu.SemaphoreType.DMA])(src)


def _dims():
    d = D_MODEL
    h = ATTN_HEADS
    d_ssd = d
    d_xbc = d_ssd + 2 * SSD_GROUPS * SSD_STATE
    sizes = (d, d, d, h, d_ssd, d_xbc, d_ssd // SSD_HEAD_DIM, d, d, 3 * d)
    return d, h, d_ssd, d_xbc, sizes


def _w_in_split(w):
    d, h, d_ssd, d_xbc, sizes = _dims()
    off = np.concatenate([[0], np.cumsum(sizes)])
    seg = lambda i: w[..., off[i]:off[i + 1]]
    main = jnp.concatenate([seg(0), seg(1), seg(2), seg(4), seg(5), seg(7), seg(8), seg(9)], axis=-1)
    pad = jnp.zeros(w.shape[:-1] + (SMALL_W - sizes[3] - sizes[6],), w.dtype)
    small = jnp.concatenate([seg(3), seg(6), pad], axis=-1)
    return main, small


def _w_in_merge(main, small):
    d, h, d_ssd, d_xbc, sizes = _dims()
    order = (0, 1, 2, 4, 5, 7, 8, 9)
    moff = np.concatenate([[0], np.cumsum([sizes[i] for i in order])])
    pieces = {i: main[..., moff[j]:moff[j + 1]] for j, i in enumerate(order)}
    pieces[3] = small[..., :sizes[3]]
    pieces[6] = small[..., sizes[3]:sizes[3] + sizes[6]]
    return jnp.concatenate([pieces[i] for i in range(10)], axis=-1)


def _main_offsets():
    d, h, d_ssd, d_xbc, sizes = _dims()
    names = ('q', 'k', 'v', 'z', 'xbc', 'xr', 'gate', 'merge')
    widths = (d, d, d, d_ssd, d_xbc, d, d, 3 * d)
    off = np.concatenate([[0], np.cumsum(widths)])
    return {nm: (int(off[i]), int(off[i + 1])) for i, nm in enumerate(names)}


def _to_heads(t, b, l):
    return t.reshape(b, l, ATTN_HEADS, ATTN_HEAD_DIM).transpose(0, 2, 1, 3).astype(BF16)


def _from_heads(t):
    b, h, l, dh = t.shape
    return t.transpose(0, 2, 1, 3).reshape(b * l, h * dh)


def _block_diag(w):
    nb, s, _ = w.shape
    eye = jnp.eye(nb, dtype=w.dtype)
    return (eye[:, None, :, None] * w[:, :, None, :]).reshape(nb * s, nb * s)


def _diag_blocks(wd, nb):
    s = wd.shape[0] // nb
    return jnp.stack([wd[i * s:(i + 1) * s, i * s:(i + 1) * s] for i in range(nb)])


def _vec128(*parts):
    v = jnp.concatenate([p.astype(F32) for p in parts])
    return jnp.pad(v, (0, SMALL_W - v.shape[0]))[None, :]


def _ffn_fwd(h, gnorm, wgu, wd, tag):
    xn = _rms_fwd(h, gnorm[None, :], f"{tag}_norm")
    gu = _mm_nn(xn, wgu, BF16, name=f"{tag}_gu")
    act = _swiglu_fwd(gu, f"{tag}_act")
    out = _mm_nn(act, wd, F32, res=h, alpha=0.5, name=f"{tag}_down")
    return out, (h, xn, gu, act)


def _ffn_bwd(dout, saved, gnorm, wgu_t, wd_t, tag):
    h, xn, gu, act = saved
    dact = _mm_nn(dout, wd_t, BF16, alpha=0.5, name=f"{tag}_dact")
    dwd = _mm_tn(act, dout, alpha=0.5, name=f"{tag}_dwd")
    dgu = _swiglu_bwd(gu, dact, f"{tag}_dgu")
    dwgu = _mm_tn(xn, dgu, name=f"{tag}_dwgu")
    dxn = _mm_nn(dgu, wgu_t, F32, name=f"{tag}_dxn")
    dh, dg = _rms_bwd(h, dxn, dout, gnorm[None, :], f"{tag}_dnorm")
    return dh, dg[0], dwgu, dwd


def _mixer_fwd(h, p, b, l):
    d, nh, d_ssd, d_xbc, sizes = _dims()
    t = b * l
    off = _main_offsets()
    xn = _rms_fwd(h, p['mix_norm'][None, :], "mix_norm")
    pm = _mm_nn(xn, p['w_main'], BF16, name="mix_in_main")
    ps = _mm_nn(xn, p['w_small'], F32, name="mix_in_small")
    col = lambda nm: pm[:, off[nm][0]:off[nm][1]]
    heads_ssd = d_ssd // SSD_HEAD_DIM
    a_neg = -jnp.exp(p['ssd_a_log'])
    fb = _vec128(p['fox_forget_bias'])
    dtb = _vec128(jnp.zeros((nh,), F32), p['ssd_dt_bias'])
    a128 = _vec128(jnp.zeros((nh,), F32), a_neg)

    def prep(_, v, fbv, dtbv, av):
        lane = lax.broadcasted_iota(jnp.int32, (1, SMALL_W), 1)
        logf = jnp.where(lane < nh, -_softplus(-(v + fbv)), 0.0)
        dtv = jnp.where((lane >= nh) & (lane < nh + heads_ssd), _softplus(v + dtbv), 0.0)
        return logf, dtv, dtv * av
    logf, dt, da = _rowwise(prep, [ps], [(SMALL_W, F32)] * 3, bcast=[fb, dtb, a128], name="mix_prep")

    blk = _attn_blk(l)
    nb = l // blk
    cum = _cumsum_seq(logf.reshape(b, l, SMALL_W), False, "fox_cumsum")
    ch = cum[:, :, :nh].transpose(0, 2, 1)
    cq, ck = ch[..., None], ch.reshape(b, nh, nb, 1, blk)
    qh, kh, vh = _to_heads(col('q'), b, l), _to_heads(col('k'), b, l), _to_heads(col('v'), b, l)
    oh, lse = _flash_fwd(qh, kh, vh, cq, ck, "fox_fwd")
    y_a = _from_heads(oh)

    xbc = col('xbc').reshape(b, l, d_xbc)
    pre_b = _conv_fwd(xbc, p['ssd_conv_w'], p['ssd_conv_b'][None, :], BF16, "ssd_conv")
    xbc_act = _rowwise(lambda _, v: _silu(v.astype(F32)), [pre_b.reshape(t, d_xbc)], [(d_xbc, BF16)],
                       name="ssd_conv_act")[0].reshape(b, l, d_xbc)
    z = col('z').reshape(b, l, d_ssd)
    dt3, da3 = dt.reshape(b, l, SMALL_W), da.reshape(b, l, SMALL_W)
    dat3 = da3.transpose(0, 2, 1)
    e_mat = _expand_matrix(nh, heads_ssd)
    dx = jnp.repeat(p['ssd_d'], SSD_HEAD_DIM)[None, :]
    nw = p['ssd_norm'][None, :]
    y_b3, yraw, hin = _ssd_fwd(xbc_act, z, dt3, da3, dat3, e_mat, dx, nw, "ssd_fwd")
    y_b = y_b3.reshape(t, d_ssd)

    xr = col('xr').reshape(b, l, d)
    xc = _conv_fwd(xr, p['lru_conv_w'], p['lru_conv_b'][None, :], F32, "lru_conv").reshape(t, d)
    pre_ri = _mm_nn(xc, p['lru_w_ri'], F32, name="lru_gates")
    lvec = (p['lru_b_a'][None, :], p['lru_b_x'][None, :], p['lru_lambda'][None, :])
    a_l, u_l = _rowwise(_lru_point_fwd, [pre_ri, xc], [(d, F32)] * 2, bcast=lvec, name="lru_point", period=l)
    hs = _linear_scan(a_l.reshape(b, l, d), u_l.reshape(b, l, d), False, "lru_scan").reshape(t, d)
    gate = col('gate')
    y_c = _rowwise(lambda _, hv, gv: hv * _gelu(gv.astype(F32)), [hs, gate], [(d, BF16)], name="lru_out")[0]

    ba = _mm_nn(y_a, p['w_branch_attn'], BF16, name="branch_attn")
    bb = _mm_nn(y_b, p['w_branch_ssd'], BF16, name="branch_ssd")
    bc = _mm_nn(y_c, p['w_branch_lru'], BF16, name="branch_lru")
    mg = col('merge')
    mixed = _merge_fwd(mg, ba, bb, bc, "merge")
    out = _mm_nn(mixed, p['w_out'], F32, res=h, name="mix_out")
    saved = dict(h=h, xn=xn, ps=ps, fb=fb, dtb=dtb, a128=a128, cq=cq, ck=ck, qh=qh, kh=kh, vh=vh, oh=oh, lse=lse,
                 xbc=xbc, pre_b=pre_b, xbc_act=xbc_act, z=z, dt3=dt3, da3=da3, dat3=dat3, e_mat=e_mat, dx=dx, nw=nw,
                 yraw=yraw, hin=hin, xr=xr, xc=xc, pre_ri=pre_ri, lvec=lvec, a_l=a_l, hs=hs, gate=gate, y_a=y_a, y_b=y_b,
                 y_c=y_c, ba=ba, bb=bb, bc=bc, mg=mg, mixed=mixed)
    return out, saved


def _expand_matrix(nh, heads_ssd):
    e = np.zeros((SMALL_W, heads_ssd * SSD_HEAD_DIM), np.float32)
    for hh in range(heads_ssd):
        e[nh + hh, hh * SSD_HEAD_DIM:(hh + 1) * SSD_HEAD_DIM] = 1.0
    return jnp.asarray(e)


def _lru_gates(pre, xc, bav, bxv, lamv, pos):
    d = xc.shape[1]
    r = _sigmoid(pre[:, :d] + bav)
    i = _sigmoid(pre[:, d:] + bxv)
    ls = -_softplus(-lamv)
    la = LRU_C * r * ls
    a = jnp.exp(la)
    mult = jnp.where(pos == 0, 1.0, jnp.sqrt(-_expm1(2.0 * la)))
    return r, i, ls, a, mult


def _lru_point_fwd(pos, pre, xc, bav, bxv, lamv):
    r, i, ls, a, mult = _lru_gates(pre, xc, bav, bxv, lamv, pos)
    return a, mult * (i * xc)


def _lru_point_bwd(pos, g, hprev, pre, xc, bav, bxv, lamv):
    r, i, ls, a, mult = _lru_gates(pre, xc, bav, bxv, lamv, pos)
    da = g * hprev
    di = g * mult * xc
    dxc = g * mult * i
    dmult = jnp.where(pos == 0, 0.0, g * i * xc)
    dla = da * a - dmult * (a * a) / mult
    dpre_r = dla * (LRU_C * ls) * r * (1.0 - r)
    dpre_i = di * i * (1.0 - i)
    dlam = jnp.sum(dla * (LRU_C * r), axis=0, keepdims=True) * _sigmoid(-lamv)
    return (jnp.concatenate([dpre_r, dpre_i], axis=1), dxc, dlam, jnp.sum(dpre_r, axis=0, keepdims=True),
            jnp.sum(dpre_i, axis=0, keepdims=True))


def _mixer_bwd(dout, s, p, b, l):
    d, nh, d_ssd, d_xbc, sizes = _dims()
    t = b * l
    heads_ssd = d_ssd // SSD_HEAD_DIM
    g = {}
    dmixed = _mm_nn(dout, p['w_out_t'], BF16, name="mix_out_dx")
    g['w_out'] = _mm_tn(s['mixed'], dout, name="mix_out_dw")
    dba, dbb, dbc, dmerge = _merge_bwd(s['mg'], s['ba'], s['bb'], s['bc'], dmixed, "merge_bwd")
    g['w_branch_attn'] = _mm_tn(s['y_a'], dba, name="branch_attn_dw")
    g['w_branch_ssd'] = _mm_tn(s['y_b'], dbb, name="branch_ssd_dw")
    g['w_branch_lru'] = _mm_tn(s['y_c'], dbc, name="branch_lru_dw")
    dy_a = _mm_nn(dba, p['w_branch_attn_t'], BF16, name="branch_attn_dx")
    dy_b = _mm_nn(dbb, p['w_branch_ssd_t'], BF16, name="branch_ssd_dx")
    dy_c = _mm_nn(dbc, p['w_branch_lru_t'], F32, name="branch_lru_dx")

    dgate, dhs = _rowwise(lambda _, dv, hv, gv: (dv * hv * _dgelu(gv.astype(F32)), dv * _gelu(gv.astype(F32))),
                          [dy_c, s['hs'], s['gate']], [(d, BF16), (d, F32)], name="lru_out_bwd")
    a3 = s['a_l'].reshape(b, l, d)
    a_next = jnp.concatenate([a3[:, 1:], jnp.zeros((b, 1, d), F32)], axis=1)
    gs = _linear_scan(a_next, dhs.reshape(b, l, d), True, "lru_scan_bwd").reshape(t, d)
    h3 = s['hs'].reshape(b, l, d)
    hprev = jnp.concatenate([jnp.zeros((b, 1, d), F32), h3[:, :-1]], axis=1).reshape(t, d)
    dpre_ri, dxc0, dlam, dba_, dbx_ = _rowwise(_lru_point_bwd, [gs, hprev, s['pre_ri'], s['xc']],
                                               [(2 * d, BF16), (d, F32)], bcast=s['lvec'],
                                               reds=[(1, d)] * 3, name="lru_point_bwd", period=l)
    g['lru_lambda'], g['lru_b_a'], g['lru_b_x'] = dlam[0], dba_[0], dbx_[0]
    dxc = _mm_nn(dpre_ri, p['lru_w_ri_t'], BF16, res=dxc0, name="lru_gates_dx")
    dw_ri = _mm_tn(s['xc'], dpre_ri, name="lru_gates_dw")
    g['lru_w_a'] = _diag_blocks(dw_ri[:, :d], LRU_BLOCKS)
    g['lru_w_x'] = _diag_blocks(dw_ri[:, d:], LRU_BLOCKS)
    dxr, dwl = _conv_bwd(s['xr'], dxc.reshape(b, l, d), p['lru_conv_w'], "lru_conv_bwd")
    g['lru_conv_w'], g['lru_conv_b'] = dwl[:CONV_K], dwl[CONV_K]

    et_mat = s['e_mat'].T
    dxs, dbm, dcm, dz, ddt, dd_l, dnw, dap = _ssd_bwd(s['xbc_act'], s['z'], s['dt3'], s['da3'], s['dat3'], s['e_mat'],
                                                      et_mat, s['dx'], s['nw'], s['a128'], s['yraw'], s['hin'],
                                                      dy_b.reshape(b, l, d_ssd), "ssd_bwd")
    g['ssd_d'] = dd_l.reshape(heads_ssd, SSD_HEAD_DIM).sum(axis=1)
    g['ssd_norm'] = dnw[0]
    g['ssd_a_log'] = dap[0, nh:nh + heads_ssd] * (-jnp.exp(p['ssd_a_log']))
    dxbc_act = jnp.concatenate([dxs, dbm, dcm], axis=2).reshape(t, d_xbc)
    dpre_b = _rowwise(lambda _, dv, pv: dv.astype(F32) * _dsilu(pv.astype(F32)),
                      [dxbc_act, s['pre_b'].reshape(t, d_xbc)], [(d_xbc, BF16)], name="ssd_conv_act_bwd")[0]
    dxbc, dws = _conv_bwd(s['xbc'], dpre_b.reshape(b, l, d_xbc), p['ssd_conv_w'], "ssd_conv_bwd")
    g['ssd_conv_w'], g['ssd_conv_b'] = dws[:CONV_K], dws[CONV_K]

    doh = _to_heads(dy_a, b, l)
    dd = _rowwise(lambda _, ov, dv: jnp.sum(ov.astype(F32) * dv.astype(F32), axis=1, keepdims=True),
                  [s['oh'].reshape(-1, ATTN_HEAD_DIM), doh.reshape(-1, ATTN_HEAD_DIM)], [(1, F32)],
                  name="fox_delta")[0].reshape(b, nh, l, 1)
    dkh, dvh, dck = _flash_bwd_kv(s['qh'], s['kh'], s['vh'], doh, s['cq'], s['ck'], s['lse'], dd, "fox_bwd_kv")
    dqh, dcq = _flash_bwd_q(s['qh'], s['kh'], s['vh'], doh, s['cq'], s['ck'], s['lse'], dd, "fox_bwd_q")
    dcum = jnp.pad((dck.reshape(b, nh, l) + dcq.reshape(b, nh, l)).transpose(0, 2, 1),
                   ((0, 0), (0, 0), (0, SMALL_W - nh)))
    dlogf = _cumsum_seq(dcum, True, "fox_cumsum_bwd").reshape(t, SMALL_W)

    def prep_bwd(_, v, dlf, ddtv, fbv, dtbv):
        a_ = dlf * _sigmoid(-(v + fbv))
        b_ = ddtv * _sigmoid(v + dtbv)
        return a_ + b_, jnp.sum(a_, axis=0, keepdims=True), jnp.sum(b_, axis=0, keepdims=True)
    dps, dfb, ddtb = _rowwise(prep_bwd, [s['ps'], dlogf, ddt.reshape(t, SMALL_W)], [(SMALL_W, F32)],
                              bcast=[s['fb'], s['dtb']], reds=[(1, SMALL_W)] * 2, name="mix_prep_bwd")
    g['fox_forget_bias'] = dfb[0, :nh]
    g['ssd_dt_bias'] = ddtb[0, nh:nh + heads_ssd]

    dpm = jnp.concatenate([_from_heads(dqh).astype(BF16), _from_heads(dkh).astype(BF16), _from_heads(dvh).astype(BF16),
                           dz.reshape(t, d_ssd), dxbc.reshape(t, d_xbc), dxr.reshape(t, d), dgate, dmerge], axis=1)
    dxn = _mm_nn(dps, p['w_small_t'], F32, name="mix_in_small_dx")
    dxn = _mm_nn(dpm, p['w_main_t'], F32, res=dxn, name="mix_in_main_dx")
    g['w_main'] = _mm_tn(s['xn'], dpm, name="mix_in_main_dw")
    g['w_small'] = _mm_tn(s['xn'], dps, name="mix_in_small_dw")
    dh, dg = _rms_bwd(s['h'], dxn, dout, p['mix_norm'][None, :], "mix_norm_bwd")
    g['mix_norm'] = dg[0]
    return dh, g


def _layer_params(w, li):
    p = {n: w[n][li] for n in WEIGHTS if n not in ('meta_tokens', 'final_norm')}
    bf = lambda a: a.astype(BF16)
    p['ffn1_gu'], p['ffn1_gu_t'] = bf(p['ffn1_w_gate_up']), bf(p['ffn1_w_gate_up']).T
    p['ffn1_d'], p['ffn1_d_t'] = bf(p['ffn1_w_down']), bf(p['ffn1_w_down']).T
    p['ffn2_gu'], p['ffn2_gu_t'] = bf(p['ffn2_w_gate_up']), bf(p['ffn2_w_gate_up']).T
    p['ffn2_d'], p['ffn2_d_t'] = bf(p['ffn2_w_down']), bf(p['ffn2_w_down']).T
    wm, ws = _w_in_split(bf(p['w_in']))
    p['w_main'], p['w_main_t'], p['w_small'], p['w_small_t'] = wm, wm.T, ws, ws.T
    for n in ('w_branch_attn', 'w_branch_ssd', 'w_branch_lru', 'w_out'):
        p[n + '_t'] = bf(p[n]).T
        p[n] = bf(p[n])
    wri = jnp.concatenate([_block_diag(p['lru_w_a']), _block_diag(p['lru_w_x'])], axis=1)
    p['lru_w_ri'], p['lru_w_ri_t'] = bf(wri), bf(wri).T
    return p


def _local_step(x, loss_target, w):
    b, seq, d = x.shape
    length = N_META + seq
    l = -(-length // Q_BLOCK) * Q_BLOCK
    t = b * l
    meta = jnp.broadcast_to(w['meta_tokens'].astype(F32)[None], (b, N_META, d))
    h = jnp.concatenate([meta, x, jnp.zeros((b, l - length, d), F32)], axis=1).reshape(t, d)
    tgt = jnp.concatenate([jnp.zeros((b, N_META, d), F32), loss_target, jnp.zeros((b, l - length, d), F32)],
                          axis=1).reshape(t, d)
    params, saves = [], []
    for li in range(DEPTH):
        p = _layer_params(w, li)
        h, s1 = _ffn_fwd(h, p['ffn1_norm'], p['ffn1_gu'], p['ffn1_d'], "ffn1")
        h, sm = _mixer_fwd(h, p, b, l)
        h, s2 = _ffn_fwd(h, p['ffn2_norm'], p['ffn2_gu'], p['ffn2_d'], "ffn2")
        params.append(p)
        saves.append((s1, sm, s2))
    dh, loss, dgf = _loss_head(h, tgt, w['final_norm'][None, :], l, "loss_head")
    layer_grads = [None] * DEPTH
    for li in reversed(range(DEPTH)):
        p = params[li]
        s1, sm, s2 = saves[li]
        g = {}
        dh, g['ffn2_norm'], g['ffn2_w_gate_up'], g['ffn2_w_down'] = _ffn_bwd(dh, s2, p['ffn2_norm'], p['ffn2_gu_t'],
                                                                              p['ffn2_d_t'], "ffn2b")
        dh, gm = _mixer_bwd(dh, sm, p, b, l)
        g.update(gm)
        g['w_in'] = _w_in_merge(g.pop('w_main'), g.pop('w_small'))
        dh, g['ffn1_norm'], g['ffn1_w_gate_up'], g['ffn1_w_down'] = _ffn_bwd(dh, s1, p['ffn1_norm'], p['ffn1_gu_t'],
                                                                              p['ffn1_d_t'], "ffn1b")
        layer_grads[li] = g
    grads = {n: jnp.stack([layer_grads[li][n] for li in range(DEPTH)]) for n in layer_grads[0]}
    for n in ('lru_w_a', 'lru_w_x'):
        grads[n] = grads[n].reshape(w[n].shape)
    dh3 = dh.reshape(b, l, d)
    grads['meta_tokens'] = jnp.sum(dh3[:, :N_META], axis=0)
    grads['final_norm'] = dgf[0]
    return loss, dh3[:, N_META:N_META + seq], grads


_ROW_W = 1024


def _half_size(total, row_quant):
    q = _ROW_W * row_quant
    return -(-(-(-total // 2)) // q) * q


def _flat_shard(parts, half):
    v = jnp.concatenate([p.reshape(-1) for p in parts])
    return jnp.pad(v, (0, 2 * half - v.shape[0]))


def _chip_shards(full, axis, nchip):
    return jnp.split(full, nchip, axis=axis + 1)


def _unflatten(flat, shapes):
    out, o = [], 0
    for sh in shapes:
        n = int(np.prod(sh))
        out.append(flat[o:o + n].reshape(sh))
        o += n
    return out


def kernel(x, meta_tokens, ffn1_norm, ffn1_w_gate_up, ffn1_w_down, mix_norm, w_in, fox_forget_bias, ssd_conv_w, ssd_conv_b, ssd_dt_bias, ssd_a_log, ssd_d, ssd_norm, lru_conv_w, lru_conv_b, lru_w_a, lru_b_a, lru_w_x, lru_b_x, lru_lambda, w_branch_attn, w_branch_ssd, w_branch_lru, w_out, ffn2_norm, ffn2_w_gate_up, ffn2_w_down, final_norm, loss_target, m_meta_tokens, m_ffn1_norm, m_ffn1_w_gate_up, m_ffn1_w_down, m_mix_norm, m_w_in, m_fox_forget_bias, m_ssd_conv_w, m_ssd_conv_b, m_ssd_dt_bias, m_ssd_a_log, m_ssd_d, m_ssd_norm, m_lru_conv_w, m_lru_conv_b, m_lru_w_a, m_lru_b_a, m_lru_w_x, m_lru_b_x, m_lru_lambda, m_w_branch_attn, m_w_branch_ssd, m_w_branch_lru, m_w_out, m_ffn2_norm, m_ffn2_w_gate_up, m_ffn2_w_down, m_final_norm, v_meta_tokens, v_ffn1_norm, v_ffn1_w_gate_up, v_ffn1_w_down, v_mix_norm, v_w_in, v_fox_forget_bias, v_ssd_conv_w, v_ssd_conv_b, v_ssd_dt_bias, v_ssd_a_log, v_ssd_d, v_ssd_norm, v_lru_conv_w, v_lru_conv_b, v_lru_w_a, v_lru_b_a, v_lru_w_x, v_lru_b_x, v_lru_lambda, v_w_branch_attn, v_w_branch_ssd, v_w_branch_lru, v_w_out, v_ffn2_norm, v_ffn2_w_gate_up, v_ffn2_w_down, v_final_norm):
    args = locals()
    wloc = {n: args[n] for n in WEIGHTS}
    mloc = {n: args['m_' + n] for n in WEIGHTS}
    vloc = {n: args['v_' + n] for n in WEIGHTS}
    nchip = 4
    chip = 2 * lax.axis_index("x") + lax.axis_index("y")
    core = lax.axis_index("c")
    row_quant = 256 if D_MODEL >= 1024 else 8

    big_shapes = [wloc[n].shape for n in BIG_NAMES]
    total = sum(int(np.prod(s)) for s in big_shapes)
    half = _half_size(total, row_quant)
    hrows = half // _ROW_W
    wflat = _flat_shard([wloc[n].astype(BF16) for n in BIG_NAMES], half).reshape(2, hrows, _ROW_W)
    my_half = lax.dynamic_index_in_dim(wflat, core, axis=0, keepdims=False)
    got = _exchange(my_half, 'xy', False, "gather_w_chips")
    both = _exchange(got, 'c', False, "gather_w_cores")
    wall = both.transpose(1, 0, 2, 3).reshape(nchip, 2 * half)
    full = {}
    per_chip = [_unflatten(wall[j], big_shapes) for j in range(nchip)]
    for i, n in enumerate(BIG_NAMES):
        full[n] = jnp.concatenate([per_chip[j][i] for j in range(nchip)], axis=BIG[n] + 1)
    cs_shapes = [wloc[n].shape for n in COLSHARD_SMALL]
    cs_total = sum(int(np.prod(s)) for s in cs_shapes)
    cs_rows = -(-cs_total // (8 * 128)) * 8
    cs_flat = jnp.concatenate([wloc[n].reshape(-1) for n in COLSHARD_SMALL])
    cs_flat = jnp.pad(cs_flat, (0, cs_rows * 128 - cs_total)).reshape(cs_rows, 128)
    cs_all = _exchange(cs_flat, 'xy', False, "gather_small").reshape(nchip, -1)
    cs_chip = [_unflatten(cs_all[j], cs_shapes) for j in range(nchip)]
    for i, n in enumerate(COLSHARD_SMALL):
        full[n] = jnp.concatenate([cs_chip[j][i] for j in range(nchip)], axis=-1)
    for n in SMALL_NAMES:
        if n not in COLSHARD_SMALL:
            full[n] = wloc[n]

    loss_part, grad_x, grads = _local_step(x, loss_target, full)

    gsh = [[s for s in _chip_shards(grads[n], BIG[n], nchip)] for n in BIG_NAMES]
    gflat = jnp.stack([_flat_shard([gsh[i][j] for i in range(len(BIG_NAMES))], half) for j in range(nchip)])
    gflat = gflat.reshape(nchip, 2, hrows, _ROW_W).transpose(1, 0, 2, 3).reshape(2, nchip * hrows, _ROW_W)
    give = lax.dynamic_index_in_dim(gflat, 1 - core, axis=0, keepdims=False)
    keep = lax.dynamic_index_in_dim(gflat, core, axis=0, keepdims=False)
    pair = _exchange(give, 'c', False, "reduce_cores")
    theirs = lax.dynamic_index_in_dim(pair, 1 - core, axis=0, keepdims=False)
    psum = _sum_rows([keep, theirs], BF16, "reduce_cores_sum").reshape(nchip, hrows, _ROW_W)
    parts = _exchange(psum, 'xy', True, "reduce_chips")
    rsum = _sum_rows([parts[j] for j in range(nchip)], F32, "reduce_chips_sum")
    halves = _exchange(rsum, 'c', False, "reduce_share")
    gshard = halves.reshape(-1)
    gbig = dict(zip(BIG_NAMES, _unflatten(gshard, big_shapes)))

    sm_shapes = [grads[n].shape for n in SMALL_NAMES]
    sm_total = sum(int(np.prod(s)) for s in sm_shapes) + 128
    sm_rows = -(-sm_total // (8 * 128)) * 8
    sm_flat = jnp.concatenate([loss_part.reshape(-1)] + [grads[n].reshape(-1) for n in SMALL_NAMES])
    sm_flat = jnp.pad(sm_flat, (0, sm_rows * 128 - sm_total)).reshape(sm_rows, 128)
    sm_all = _exchange(sm_flat, 'xyc', False, "gather_small_grads")
    sm_sum = _sum_rows([sm_all[j] for j in range(8)], F32, "small_grads_sum").reshape(-1)
    loss = sm_sum[0]
    gsmall_full = dict(zip(SMALL_NAMES, _unflatten(sm_sum[128:], sm_shapes)))
    gsmall = {}
    for n in SMALL_NAMES:
        gfull = gsmall_full[n]
        if n in COLSHARD_SMALL:
            wcols = wloc[n].shape[-1]
            gfull = lax.dynamic_slice_in_dim(gfull, chip * wcols, wcols, axis=gfull.ndim - 1)
        gsmall[n] = gfull

    def flat_big(dct):
        return _flat_shard([dct[n] for n in BIG_NAMES], half).reshape(2 * hrows, _ROW_W)
    dl, mn, vn = _adamw(flat_big(wloc), gshard.reshape(2 * hrows, _ROW_W), flat_big(mloc), flat_big(vloc), "adamw_big")
    big_out = [dict(zip(BIG_NAMES, _unflatten(a.reshape(-1), big_shapes))) for a in (dl, mn, vn)]
    loc_shapes = [wloc[n].shape for n in SMALL_NAMES]
    loc_total = sum(int(np.prod(s)) for s in loc_shapes)
    loc_rows = -(-loc_total // (8 * 128)) * 8

    def flat_small(dct):
        v = jnp.concatenate([dct[n].reshape(-1) for n in SMALL_NAMES])
        return jnp.pad(v, (0, loc_rows * 128 - loc_total)).reshape(loc_rows, 128)
    dls, mns, vns = _adamw(flat_small(wloc), flat_small(gsmall), flat_small(mloc), flat_small(vloc), "adamw_small")
    small_out = [dict(zip(SMALL_NAMES, _unflatten(a.reshape(-1), loc_shapes))) for a in (dls, mns, vns)]

    grad_w = {**gbig, **gsmall}
    outs = [loss, grad_x] + [grad_w[n] for n in WEIGHTS]
    for k in range(3):
        merged = {**big_out[k], **small_out[k]}
        outs += [merged[n] for n in WEIGHTS]
    return tuple(outs)
```

```python
import functools
import math

import numpy as np
import jax
import jax.numpy as jnp
from jax import lax
from jax.experimental import pallas as pl
from jax.experimental.pallas import tpu as pltpu

F32 = jnp.float32
BF16 = jnp.bfloat16
HI = lax.Precision.HIGHEST
VMEM_LIMIT_BYTES = 56 * 1024 * 1024
NEG = -1e30

D_MODEL = 1024
SEQ = 4096
DEPTH = 4
N_META = 16
Q_BLOCK = 128
SSD_CHUNK = 128
NORM_EPS = 1e-6
ATTN_HEADS = 16
ATTN_HEAD_DIM = 64
SSD_HEAD_DIM = 64
SSD_GROUPS = 2
SSD_STATE = 128
CONV_K = 4
LRU_BLOCKS = 16
LRU_C = 8.0
D_FF = 2816
ADAM_LR = 0.001
ADAM_B1 = 0.9
ADAM_B2 = 0.999
ADAM_EPS = 1e-08
ADAM_WD = 0.01
ADAM_STEP = 10
SMALL_W = 128
_ROWWISE_TILE_ELEMS = 512 * 1024

WEIGHTS = ['meta_tokens', 'ffn1_norm', 'ffn1_w_gate_up', 'ffn1_w_down', 'mix_norm', 'w_in', 'fox_forget_bias',
           'ssd_conv_w', 'ssd_conv_b', 'ssd_dt_bias', 'ssd_a_log', 'ssd_d', 'ssd_norm', 'lru_conv_w', 'lru_conv_b',
           'lru_w_a', 'lru_b_a', 'lru_w_x', 'lru_b_x', 'lru_lambda', 'w_branch_attn', 'w_branch_ssd', 'w_branch_lru',
           'w_out', 'ffn2_norm', 'ffn2_w_gate_up', 'ffn2_w_down', 'final_norm']
BIG = {'ffn1_w_gate_up': 1, 'ffn1_w_down': 0, 'w_in': 1, 'w_branch_attn': 0, 'w_branch_ssd': 0, 'w_branch_lru': 0,
       'w_out': 0, 'ffn2_w_gate_up': 1, 'ffn2_w_down': 0}
BIG_NAMES = [n for n in WEIGHTS if n in BIG]
COLSHARD_SMALL = ['meta_tokens', 'ssd_conv_w', 'lru_conv_w']
SMALL_NAMES = [n for n in WEIGHTS if n not in BIG]


def _pick(n, cands):
    for c in cands:
        if n % c == 0:
            return c
    raise ValueError(f"no tile for {n} in {cands}")


def _pcall(body, **kw):
    return pl.pallas_call(body, **kw)


def _cparams(sem):
    return pltpu.CompilerParams(dimension_semantics=sem, vmem_limit_bytes=VMEM_LIMIT_BYTES)


def _sds(shape, dtype):
    return jax.ShapeDtypeStruct(tuple(shape), dtype)


def _dot(a, b, hi=False):
    return jnp.dot(a, b, precision=HI if hi else None, preferred_element_type=F32)


def _dot_nt(a, b):
    return lax.dot_general(a, b, (((1,), (1,)), ((), ())), preferred_element_type=F32)


def _dot_tn(a, b):
    return lax.dot_general(a, b, (((0,), (0,)), ((), ())), preferred_element_type=F32)


def _sigmoid(x):
    return 1.0 / (1.0 + jnp.exp(-x))


def _softplus(x):
    return jnp.maximum(x, 0.0) + jnp.log1p(jnp.exp(-jnp.abs(x)))


def _silu(x):
    return x * _sigmoid(x)


def _dsilu(x):
    s = _sigmoid(x)
    return s * (1.0 + x * (1.0 - s))


_GELU_C = math.sqrt(2.0 / math.pi)


def _gelu(x):
    return 0.5 * x * (1.0 + jnp.tanh(_GELU_C * (x + 0.044715 * x * x * x)))


def _dgelu(x):
    t = jnp.tanh(_GELU_C * (x + 0.044715 * x * x * x))
    return 0.5 * (1.0 + t) + 0.5 * x * (1.0 - t * t) * _GELU_C * (1.0 + 3.0 * 0.044715 * x * x)


def _expm1(x):
    series = x * (1.0 + x * 0.5 * (1.0 + x * (1.0 / 3.0) * (1.0 + x * 0.25 * (1.0 + x * 0.2))))
    return jnp.where(jnp.abs(x) < 0.05, series, jnp.exp(x) - 1.0)


def _rowwise(fn, ins, outs, *, bcast=(), reds=(), tm=None, name, period=None):
    t_rows = ins[0].shape[0]
    if tm is None:
        widest = max([a.shape[1] for a in ins] + [c for c, _ in outs])
        tm = _pick(math.gcd(t_rows, period or t_rows),
                   [c for c in (384, 256, 128, 64, 32, 16, 8) if c * widest <= _ROWWISE_TILE_ELEMS or c == 8])
    nt = t_rows // tm
    assert t_rows % tm == 0 and (period is None or period % tm == 0)
    n_in, n_out = len(ins) + len(bcast), len(outs)

    def body(*refs):
        i = pl.program_id(0)
        pos = None
        if period is not None:
            pos = (i * tm) % period + lax.broadcasted_iota(jnp.int32, (tm, 1), 0)
        res = fn(pos, *[r[...] for r in refs[:n_in]])
        res = res if isinstance(res, tuple) else (res,)
        for r, v in zip(refs[n_in:n_in + n_out], res[:n_out]):
            r[...] = v.astype(r.dtype)
        red_refs = refs[n_in + n_out:]
        if red_refs:
            @pl.when(i == 0)
            def _():
                for r in red_refs:
                    r[...] = jnp.zeros_like(r)
            for r, v in zip(red_refs, res[n_out:]):
                r[...] += v

    in_specs = [pl.BlockSpec((tm, a.shape[1]), lambda i: (i, 0)) for a in ins]
    in_specs += [pl.BlockSpec(b.shape, lambda i, n=b.ndim: (0,) * n) for b in bcast]
    out_specs = [pl.BlockSpec((tm, c), lambda i: (i, 0)) for c, _ in outs]
    out_specs += [pl.BlockSpec(s, lambda i: (0, 0)) for s in reds]
    out_shape = [_sds((t_rows, c), dt) for c, dt in outs] + [_sds(s, F32) for s in reds]
    res = _pcall(body, name=name, grid=(nt,), in_specs=in_specs, out_specs=out_specs, out_shape=out_shape,
                 compiler_params=_cparams(("arbitrary",) if reds else ("parallel",)))(*ins, *bcast)
    return res


_TM = (768, 384, 256, 128)
_TN = (1536, 1408, 1024, 768, 512, 640, 384, 256, 128)
_TK = (1024, 1408, 512, 384, 256, 128)


def _mm_nn(a, b, out_dtype, *, res=None, alpha=1.0, name):
    m, k = a.shape
    k2, n = b.shape
    assert k == k2
    tm, tn, tk = _pick(m, _TM), _pick(n, _TN), _pick(k, _TK)
    nk = k // tk

    def body(*refs):
        if res is None:
            a_ref, b_ref, o_ref, acc = refs
            r_ref = None
        else:
            a_ref, b_ref, r_ref, o_ref, acc = refs
        kk = pl.program_id(2)

        @pl.when(kk == 0)
        def _():
            acc[...] = jnp.zeros_like(acc)

        acc[...] += _dot(a_ref[...].astype(BF16), b_ref[...].astype(BF16))

        @pl.when(kk == nk - 1)
        def _():
            v = acc[...]
            if alpha != 1.0:
                v = v * alpha
            if r_ref is not None:
                v = r_ref[...].astype(F32) + v
            o_ref[...] = v.astype(o_ref.dtype)

    in_specs = [pl.BlockSpec((tm, tk), lambda j, i, kk: (i, kk)), pl.BlockSpec((tk, tn), lambda j, i, kk: (kk, j))]
    args = [a, b]
    if res is not None:
        in_specs.append(pl.BlockSpec((tm, tn), lambda j, i, kk: (i, j)))
        args.append(res)
    return _pcall(body, name=name, grid=(n // tn, m // tm, nk), in_specs=in_specs,
                  out_specs=pl.BlockSpec((tm, tn), lambda j, i, kk: (i, j)), out_shape=_sds((m, n), out_dtype),
                  scratch_shapes=[pltpu.VMEM((tm, tn), F32)],
                  compiler_params=_cparams(("parallel", "parallel", "arbitrary")))(*args)


def _mm_tn(a, b, *, alpha=1.0, name):
    m, k = a.shape
    m2, n = b.shape
    assert m == m2
    tm, tn, tko = _pick(m, _TM), _pick(n, _TN), _pick(k, _TK)
    nm = m // tm

    def body(a_ref, b_ref, o_ref, acc):
        mm = pl.program_id(2)

        @pl.when(mm == 0)
        def _():
            acc[...] = jnp.zeros_like(acc)

        acc[...] += _dot_tn(a_ref[...].astype(BF16), b_ref[...].astype(BF16))

        @pl.when(mm == nm - 1)
        def _():
            v = acc[...]
            o_ref[...] = v * alpha if alpha != 1.0 else v

    return _pcall(body, name=name, grid=(k // tko, n // tn, nm),
                  in_specs=[pl.BlockSpec((tm, tko), lambda i, j, mm: (mm, i)),
                            pl.BlockSpec((tm, tn), lambda i, j, mm: (mm, j))],
                  out_specs=pl.BlockSpec((tko, tn), lambda i, j, mm: (i, j)), out_shape=_sds((k, n), F32),
                  scratch_shapes=[pltpu.VMEM((tko, tn), F32)],
                  compiler_params=_cparams(("parallel", "parallel", "arbitrary")))(a, b)


def _rms_fwd(h, g, name):
    def fn(_, hv, gv):
        r = lax.rsqrt(jnp.mean(hv * hv, axis=1, keepdims=True) + NORM_EPS)
        return hv * r * gv
    return _rowwise(fn, [h], [(h.shape[1], BF16)], bcast=[g], name=name)[0]


def _rms_bwd(h, dxn, dres, g, name):
    d = h.shape[1]

    def fn(_, hv, dv, rv, gv):
        r = lax.rsqrt(jnp.mean(hv * hv, axis=1, keepdims=True) + NORM_EPS)
        xh = hv * r
        dxh = dv * gv
        dh = r * (dxh - xh * jnp.mean(dxh * xh, axis=1, keepdims=True))
        return rv + dh, jnp.sum(dv * xh, axis=0, keepdims=True)
    return _rowwise(fn, [h, dxn, dres], [(d, F32)], bcast=[g], reds=[(1, d)], name=name)


def _swiglu_fwd(gu, name):
    f = gu.shape[1] // 2

    def fn(_, v):
        v = v.astype(F32)
        return _silu(v[:, :f]) * v[:, f:]
    return _rowwise(fn, [gu], [(f, BF16)], name=name)[0]


def _swiglu_bwd(gu, dact, name):
    f = gu.shape[1] // 2

    def fn(_, v, dv):
        v = v.astype(F32)
        dv = dv.astype(F32)
        g, u = v[:, :f], v[:, f:]
        return jnp.concatenate([dv * u * _dsilu(g), dv * _silu(g)], axis=1)
    return _rowwise(fn, [gu, dact], [(2 * f, BF16)], name=name)[0]


def _merge_fwd(mg, ba, bb, bc, name):
    d = ba.shape[1]

    def fn(_, m, a, b, c):
        g = _sigmoid(m.astype(F32))
        return g[:, :d] * a.astype(F32) + g[:, d:2 * d] * b.astype(F32) + g[:, 2 * d:] * c.astype(F32)
    return _rowwise(fn, [mg, ba, bb, bc], [(d, BF16)], name=name)[0]


def _merge_bwd(mg, ba, bb, bc, dmix, name):
    d = ba.shape[1]

    def fn(_, m, a, b, c, dm):
        g = _sigmoid(m.astype(F32))
        dm = dm.astype(F32)
        br = (a.astype(F32), b.astype(F32), c.astype(F32))
        douts, dgs = [], []
        for j in range(3):
            gj = g[:, j * d:(j + 1) * d]
            douts.append(dm * gj)
            dgs.append(dm * br[j] * gj * (1.0 - gj))
        return (*douts, jnp.concatenate(dgs, axis=1))
    return _rowwise(fn, [mg, ba, bb, bc, dmix], [(d, BF16)] * 3 + [(3 * d, BF16)], name=name)


def _loss_head(h, tgt, g, seq_len, name):
    d = h.shape[1]

    def fn(pos, hv, tv, gv):
        r = lax.rsqrt(jnp.mean(hv * hv, axis=1, keepdims=True) + NORM_EPS)
        xh = hv * r
        real = (pos >= N_META) & (pos < N_META + SEQ)
        e = jnp.where(real, xh * gv - tv, 0.0)
        part = jnp.sum(jnp.sum(e * e, axis=1, keepdims=True), axis=0, keepdims=True) * (0.5 / d)
        dy = e * (1.0 / d)
        dxh = dy * gv
        dh = r * (dxh - xh * jnp.mean(dxh * xh, axis=1, keepdims=True))
        return dh, jnp.broadcast_to(part, (1, 128)), jnp.sum(dy * xh, axis=0, keepdims=True)
    return _rowwise(fn, [h, tgt], [(d, F32)], bcast=[g], reds=[(1, 128), (1, d)], name=name, period=seq_len)


def _adamw(w, g, m, v, name):
    c1 = 1.0 - ADAM_B1 ** ADAM_STEP
    c2 = 1.0 - ADAM_B2 ** ADAM_STEP
    wd = w.shape[1]

    def fn(_, wv, gv, mv, vv):
        mn = ADAM_B1 * mv + (1.0 - ADAM_B1) * gv
        vn = ADAM_B2 * vv + (1.0 - ADAM_B2) * (gv * gv)
        delta = -ADAM_LR * ((mn / c1) / (jnp.sqrt(vn / c2) + ADAM_EPS) + ADAM_WD * wv)
        return delta, mn, vn
    return _rowwise(fn, [w, g, m, v], [(wd, F32)] * 3, name=name, tm=_pick(w.shape[0], (256, 128, 64, 32, 16, 8)))


def _sum_rows(parts, out_dtype, name):
    def fn(_, *vs):
        acc = vs[0].astype(F32)
        for v in vs[1:]:
            acc = acc + v.astype(F32)
        return acc
    return _rowwise(fn, list(parts), [(parts[0].shape[1], out_dtype)], name=name,
                    tm=_pick(parts[0].shape[0], (256, 128, 64, 32, 16, 8)))[0]


def _cumsum_seq(x, reverse, name):
    b, l, w = x.shape
    q = 128
    nc = l // q

    def body(x_ref, o_ref):
        row = lax.broadcasted_iota(jnp.int32, (q, q), 0)
        col = lax.broadcasted_iota(jnp.int32, (q, q), 1)
        tri = ((row <= col) if reverse else (row >= col)).astype(F32)
        rsel = lax.broadcasted_iota(jnp.int32, (q, w), 0) == (0 if reverse else q - 1)

        def step(i, carry):
            j = (nc - 1 - i) if reverse else i
            start = pl.multiple_of(j * q, q)
            cs = _dot(tri, x_ref[pl.ds(start, q), :], hi=True) + carry
            o_ref[pl.ds(start, q), :] = cs
            return jnp.sum(jnp.where(rsel, cs, 0.0), axis=0, keepdims=True)

        lax.fori_loop(0, nc, step, jnp.zeros((1, w), F32))

    return _pcall(body, name=name, grid=(b,), in_specs=[pl.BlockSpec((None, l, w), lambda i: (i, 0, 0))],
                  out_specs=pl.BlockSpec((None, l, w), lambda i: (i, 0, 0)), out_shape=_sds(x.shape, F32),
                  compiler_params=_cparams(("parallel",)))(x)


_HALO = 16


def _conv_tiles(l, c):
    return _pick(l, (384, 256, 128)), _pick(c, (512, 256, 128))


def _conv_fwd(x, w, bias, out_dtype, name):
    b, l, c = x.shape
    tt, cw = _conv_tiles(l, c)

    def body(x_ref, h_ref, w_ref, b_ref, o_ref):
        t = pl.program_id(2)
        halo = jnp.where(t == 0, 0.0, h_ref[...].astype(F32))
        xe = jnp.concatenate([halo, x_ref[...].astype(F32)], axis=0)
        wv = w_ref[...]
        acc = b_ref[...] + wv[CONV_K - 1:CONV_K, :] * xe[_HALO:]
        for j in range(CONV_K - 1):
            acc = acc + wv[j:j + 1, :] * pltpu.roll(xe, CONV_K - 1 - j, 0)[_HALO:]
        o_ref[...] = acc.astype(o_ref.dtype)

    return _pcall(body, name=name, grid=(b, c // cw, l // tt),
                  in_specs=[pl.BlockSpec((None, tt, cw), lambda i, j, t: (i, t, j)),
                            pl.BlockSpec((None, _HALO, cw), lambda i, j, t: (i, jnp.maximum(t * (tt // _HALO) - 1, 0), j)),
                            pl.BlockSpec((CONV_K, cw), lambda i, j, t: (0, j)),
                            pl.BlockSpec((1, cw), lambda i, j, t: (0, j))],
                  out_specs=pl.BlockSpec((None, tt, cw), lambda i, j, t: (i, t, j)), out_shape=_sds(x.shape, out_dtype),
                  compiler_params=_cparams(("parallel", "parallel", "parallel")))(x, x, w, bias)


def _conv_bwd(x, dy, w, name):
    b, l, c = x.shape
    tt, cw = _conv_tiles(l, c)
    nt = l // tt

    def body(x_ref, xh_ref, d_ref, dh_ref, w_ref, dx_ref, dw_ref):
        i, t = pl.program_id(1), pl.program_id(2)
        halo = jnp.where(t == 0, 0.0, xh_ref[...].astype(F32))
        xe = jnp.concatenate([halo, x_ref[...].astype(F32)], axis=0)
        dv = d_ref[...].astype(F32)
        nxt = jnp.where(t == nt - 1, 0.0, dh_ref[...].astype(F32))
        de = jnp.concatenate([dv, nxt], axis=0)
        wv = w_ref[...]
        dx = wv[CONV_K - 1:CONV_K, :] * dv
        rowid = lax.broadcasted_iota(jnp.int32, (8, 1), 0)
        part = jnp.where(rowid == CONV_K, jnp.sum(dv, axis=0, keepdims=True), 0.0)
        part = part + jnp.where(rowid == CONV_K - 1, jnp.sum(dv * xe[_HALO:], axis=0, keepdims=True), 0.0)
        for j in range(CONV_K - 1):
            s = CONV_K - 1 - j
            dx = dx + wv[j:j + 1, :] * pltpu.roll(de, tt + _HALO - s, 0)[:tt]
            xs = pltpu.roll(xe, s, 0)[_HALO:]
            part = part + jnp.where(rowid == j, jnp.sum(dv * xs, axis=0, keepdims=True), 0.0)
        dx_ref[...] = dx.astype(dx_ref.dtype)

        @pl.when((i == 0) & (t == 0))
        def _():
            dw_ref[...] = jnp.zeros_like(dw_ref)
        dw_ref[...] += part

    return _pcall(body, name=name, grid=(c // cw, b, nt),
                  in_specs=[pl.BlockSpec((None, tt, cw), lambda j, i, t: (i, t, j)),
                            pl.BlockSpec((None, _HALO, cw), lambda j, i, t: (i, jnp.maximum(t * (tt // _HALO) - 1, 0), j)),
                            pl.BlockSpec((None, tt, cw), lambda j, i, t: (i, t, j)),
                            pl.BlockSpec((None, _HALO, cw),
                                         lambda j, i, t: (i, jnp.minimum((t + 1) * (tt // _HALO), l // _HALO - 1), j)),
                            pl.BlockSpec((CONV_K, cw), lambda j, i, t: (0, j))],
                  out_specs=[pl.BlockSpec((None, tt, cw), lambda j, i, t: (i, t, j)),
                             pl.BlockSpec((8, cw), lambda j, i, t: (0, j))],
                  out_shape=[_sds(x.shape, BF16), _sds((8, c), F32)],
                  compiler_params=_cparams(("parallel", "arbitrary", "arbitrary")))(x, x, dy, dy, w)


def _linear_scan(a, u, reverse, name):
    b, l, c = a.shape
    tt = 128
    cw = _pick(c, (512, 256, 128))
    nt = l // tt

    def body(a_ref, u_ref, h_ref, carry):
        t = pl.program_id(2)

        @pl.when(t == 0)
        def _():
            carry[...] = jnp.zeros_like(carry)

        av, uv = a_ref[...], u_ref[...]
        row = lax.broadcasted_iota(jnp.int32, (tt, cw), 0)
        k = 1
        while k < tt:
            if reverse:
                keep = row < tt - k
                a_sh = jnp.where(keep, pltpu.roll(av, tt - k, 0), 1.0)
                u_sh = jnp.where(keep, pltpu.roll(uv, tt - k, 0), 0.0)
            else:
                keep = row >= k
                a_sh = jnp.where(keep, pltpu.roll(av, k, 0), 1.0)
                u_sh = jnp.where(keep, pltpu.roll(uv, k, 0), 0.0)
            uv = uv + av * u_sh
            av = av * a_sh
            k *= 2
        hv = uv + av * carry[0:1, :]
        h_ref[...] = hv
        edge = jnp.sum(jnp.where(row == (0 if reverse else tt - 1), hv, 0.0), axis=0, keepdims=True)
        carry[...] = jnp.broadcast_to(edge, carry.shape)

    tmap = (lambda i, j, t: (i, nt - 1 - t, j)) if reverse else (lambda i, j, t: (i, t, j))
    spec = pl.BlockSpec((None, tt, cw), tmap)
    return _pcall(body, name=name, grid=(b, c // cw, nt), in_specs=[spec, spec], out_specs=spec,
                  out_shape=_sds(a.shape, F32), scratch_shapes=[pltpu.VMEM((8, cw), F32)],
                  compiler_params=_cparams(("parallel", "parallel", "arbitrary")))(a, u)


ATTN_W = 128
_AUG_C = 0
_AUG_ONE = 3
_AUG_LSE = 6


def _attn_blk(l):
    return _pick(l, (384, 256, 128))


def _split3(x):
    rnd = lambda v: lax.reduce_precision(v, exponent_bits=8, mantissa_bits=7)
    x1 = rnd(x)
    x2 = rnd(x - x1)
    x3 = rnd(x - x1 - x2)
    return [p.astype(BF16)[..., None] for p in (x1, x2, x3)]


def _attn_operands(qh, kh, vh, ch):
    dh = qh.shape[-1]
    lead = qh.shape[:-1]
    ones = jnp.ones(lead + (3,), BF16)
    c3 = _split3(ch)
    scale = dh ** -0.5
    qa = jnp.concatenate([(qh.astype(F32) * scale).astype(BF16)] + c3 + [ones], axis=-1)
    ka = jnp.concatenate([kh, ones] + [-p for p in c3] + [ones, jnp.zeros(lead + (ATTN_W - dh - 9,), BF16)], axis=-1)
    va = jnp.concatenate([vh, ones, jnp.zeros(lead + (ATTN_W - dh - 3,), BF16)], axis=-1)
    return qa, ka, va


def _pad_lanes(parts, width):
    lead = parts[0].shape[:-1]
    have = sum(p.shape[-1] for p in parts)
    return jnp.concatenate(list(parts) + [jnp.zeros(lead + (width - have,), parts[0].dtype)], axis=-1)


def _flash_fwd(qa, ka, va, dh, blk, name):
    b, h, l, w = qa.shape
    nb = l // blk
    kr, vr = ka.reshape(b, h, nb, blk, w), va.reshape(b, h, nb, blk, w)

    def body(q_ref, k_ref, v_ref, o_ref, lse_ref):
        i = pl.program_id(2)
        qv = q_ref[...]

        def update(s, j, carry):
            m, acc = carry
            mn = jnp.maximum(m, jnp.max(s, axis=1, keepdims=True))
            return mn, jnp.exp(m - mn) * acc + _dot(jnp.exp(s - mn).astype(BF16), v_ref[j])

        init = (jnp.full((blk, 1), NEG, F32), jnp.zeros((blk, w), F32))
        carry = lax.fori_loop(0, i, lambda j, c: update(_dot_nt(qv, k_ref[j]), j, c), init)
        row = lax.broadcasted_iota(jnp.int32, (blk, blk), 0)
        col = lax.broadcasted_iota(jnp.int32, (blk, blk), 1)
        m, acc = update(jnp.where(col <= row, _dot_nt(qv, k_ref[i]), NEG), i, carry)
        lsum = acc[:, dh:dh + 1]
        o_ref[...] = (acc[:, :dh] / lsum).astype(o_ref.dtype)
        lse_ref[...] = m + jnp.log(lsum)

    blk4 = lambda ww: pl.BlockSpec((None, None, blk, ww), lambda bi, hi, i: (bi, hi, i, 0))
    full5 = pl.BlockSpec((None, None, nb, blk, w), lambda bi, hi, i: (bi, hi, 0, 0, 0))
    return _pcall(body, name=name, grid=(b, h, nb), in_specs=[blk4(w), full5, full5],
                  out_specs=[blk4(dh), blk4(1)], out_shape=[_sds((b, h, l, dh), BF16), _sds((b, h, l, 1), F32)],
                  compiler_params=_cparams(("parallel", "parallel", "parallel")))(qa, kr, vr)


def _flash_bwd_kv(qa, ka, va, doa, blk, name):
    b, h, l, w = qa.shape
    nb = l // blk
    r5 = lambda t: t.reshape(b, h, nb, blk, w)

    def body(k_ref, v_ref, q_ref, do_ref, dk_ref, dv_ref):
        j = pl.program_id(2)
        kv, vv = k_ref[...], v_ref[...]
        row = lax.broadcasted_iota(jnp.int32, (blk, blk), 0)
        col = lax.broadcasted_iota(jnp.int32, (blk, blk), 1)

        def contrib(i, masked, carry):
            dk, dv = carry
            qv, dov = q_ref[i], do_ref[i]
            p = jnp.exp(_dot_nt(qv, kv))
            if masked:
                p = jnp.where(col <= row, p, 0.0)
            ds = p * _dot_nt(dov, vv)
            return dk + _dot_tn(ds.astype(BF16), qv), dv + _dot_tn(p.astype(BF16), dov)

        zero = (jnp.zeros((blk, w), F32), jnp.zeros((blk, w), F32))
        dk, dv = lax.fori_loop(j + 1, nb, lambda i, c: contrib(i, False, c), contrib(j, True, zero))
        dk_ref[...] = dk
        dv_ref[...] = dv

    blk4 = pl.BlockSpec((None, None, blk, w), lambda bi, hi, j: (bi, hi, j, 0))
    full5 = pl.BlockSpec((None, None, nb, blk, w), lambda bi, hi, j: (bi, hi, 0, 0, 0))
    return _pcall(body, name=name, grid=(b, h, nb), in_specs=[blk4, blk4, full5, full5], out_specs=[blk4, blk4],
                  out_shape=[_sds(qa.shape, F32), _sds(qa.shape, F32)],
                  compiler_params=_cparams(("parallel", "parallel", "parallel")))(ka, va, r5(qa), r5(doa))


def _flash_bwd_q(qa, ka, va, doa, dh, blk, name):
    b, h, l, w = qa.shape
    nb = l // blk
    scale = dh ** -0.5
    kr, vr = ka.reshape(b, h, nb, blk, w), va.reshape(b, h, nb, blk, w)

    def body(q_ref, do_ref, k_ref, v_ref, dq_ref):
        i = pl.program_id(2)
        qv, dov = q_ref[...], do_ref[...]
        row = lax.broadcasted_iota(jnp.int32, (blk, blk), 0)
        col = lax.broadcasted_iota(jnp.int32, (blk, blk), 1)

        def contrib(j, masked, dq):
            p = jnp.exp(_dot_nt(qv, k_ref[j]))
            if masked:
                p = jnp.where(col <= row, p, 0.0)
            ds = p * _dot_nt(dov, v_ref[j])
            return dq + _dot(ds.astype(BF16), k_ref[j])

        dq = contrib(i, True, lax.fori_loop(0, i, lambda j, c: contrib(j, False, c), jnp.zeros((blk, w), F32)))
        lane = lax.broadcasted_iota(jnp.int32, (1, w), 1)
        dq_ref[...] = dq * jnp.where(lane < dh, scale, 1.0)

    blk4 = pl.BlockSpec((None, None, blk, w), lambda bi, hi, i: (bi, hi, i, 0))
    full5 = pl.BlockSpec((None, None, nb, blk, w), lambda bi, hi, i: (bi, hi, 0, 0, 0))
    return _pcall(body, name=name, grid=(b, h, nb), in_specs=[blk4, blk4, full5, full5], out_specs=blk4,
                  out_shape=_sds(qa.shape, F32),
                  compiler_params=_cparams(("parallel", "parallel", "parallel")))(qa, doa, kr, vr)


def _ssd_dims(d_ssd):
    heads = d_ssd // SSD_HEAD_DIM
    return heads, heads // SSD_GROUPS, d_ssd // SSD_GROUPS


def _ssd_specs(l, ds, seq_map):
    q = SSD_CHUNK
    gn = SSD_GROUPS * SSD_STATE
    row3 = lambda w, cb: pl.BlockSpec((None, q, w), lambda i, c, cb=cb: (i, seq_map(c), cb))
    return dict(
        xs=row3(ds, 0), bm=row3(gn, ds // gn), cm=row3(gn, ds // gn + 1), z=row3(ds, 0), dt=row3(SMALL_W, 0),
        da=row3(SMALL_W, 0), dat=pl.BlockSpec((None, SMALL_W, q), lambda i, c: (i, 0, seq_map(c))),
        e=pl.BlockSpec((SMALL_W, ds), lambda i, c: (0, 0)), et=pl.BlockSpec((ds, SMALL_W), lambda i, c: (0, 0)),
        vec=pl.BlockSpec((1, ds), lambda i, c: (0, 0)), vec128=pl.BlockSpec((1, SMALL_W), lambda i, c: (0, 0)),
        hin=pl.BlockSpec((None, None, SSD_STATE, ds), lambda i, c: (i, seq_map(c), 0, 0)))


def _ssd_common(da, dat, dt, e_mat, xs):
    q = SSD_CHUNK
    row = lax.broadcasted_iota(jnp.int32, (q, q), 0)
    col = lax.broadcasted_iota(jnp.int32, (q, q), 1)
    lower = row >= col
    cs = _dot(lower.astype(F32), da, hi=True)
    cst = _dot(dat, (row <= col).astype(F32), hi=True)
    dtx = _dot(dt, e_mat, hi=True)
    csx = _dot(cs, e_mat, hi=True)
    rowx = lax.broadcasted_iota(jnp.int32, csx.shape, 0)
    totx = jnp.sum(jnp.where(rowx == q - 1, csx, 0.0), axis=0, keepdims=True)
    xf = xs.astype(F32)
    return lower, cs, cst, dtx, csx, totx, xf, xf * dtx


def _ssd_fwd(xbc, z, dt, da, dat, e_mat, dx, nw, name):
    b, l, _ = xbc.shape
    ds = z.shape[2]
    heads, hpg, gw = _ssd_dims(ds)
    q, n = SSD_CHUNK, SSD_STATE
    nc = l // q
    hcol0 = ATTN_HEADS

    def body(xs_ref, bm_ref, cm_ref, z_ref, dt_ref, da_ref, dat_ref, e_ref, dx_ref, nw_ref, y_ref, yraw_ref, hin_ref,
             hst, ydiag):
        c = pl.program_id(1)

        @pl.when(c == 0)
        def _():
            hst[...] = jnp.zeros_like(hst)

        hin = hst[...]
        hin_ref[...] = hin
        lower, cs, cst, dtx, csx, totx, xf, xdt = _ssd_common(da_ref[...], dat_ref[...], dt_ref[...], e_ref[...],
                                                               xs_ref[...])
        bm, cm = bm_ref[...], cm_ref[...]
        dec_end = jnp.exp(totx - csx)
        for g in range(SSD_GROUPS):
            gs = slice(g * gw, (g + 1) * gw)
            bg, cg = bm[:, g * n:(g + 1) * n], cm[:, g * n:(g + 1) * n]
            cb = _dot_nt(cg, bg)
            for e in range(hpg):
                hh = g * hpg + e
                cc = hcol0 + hh
                lm = jnp.exp(jnp.where(lower, cs[:, cc:cc + 1] - cst[cc:cc + 1, :], NEG))
                hs = slice(hh * SSD_HEAD_DIM, (hh + 1) * SSD_HEAD_DIM)
                ydiag[:, hs] = _dot((cb * lm).astype(BF16), xdt[:, hs].astype(BF16))
            sg = _dot_tn(bg, (xdt[:, gs] * dec_end[:, gs]).astype(BF16))
            hst[:, gs] = jnp.exp(totx[:, gs]) * hin[:, gs] + sg
            ydiag[:, gs] += _dot(cg, hin[:, gs].astype(BF16)) * jnp.exp(csx[:, gs])
        yraw = ydiag[...] + dx_ref[...] * xf
        yraw_ref[...] = yraw.astype(yraw_ref.dtype)
        yg = yraw * _silu(z_ref[...].astype(F32))
        nwv = nw_ref[...]
        for g in range(SSD_GROUPS):
            gs = slice(g * gw, (g + 1) * gw)
            r = lax.rsqrt(jnp.mean(yg[:, gs] * yg[:, gs], axis=1, keepdims=True) + NORM_EPS)
            y_ref[:, gs] = (yg[:, gs] * r * nwv[:, gs]).astype(y_ref.dtype)

    sp = _ssd_specs(l, ds, lambda c: c)
    return _pcall(body, name=name, grid=(b, nc),
                  in_specs=[sp['xs'], sp['bm'], sp['cm'], sp['z'], sp['dt'], sp['da'], sp['dat'], sp['e'], sp['vec'],
                            sp['vec']],
                  out_specs=[sp['z'], sp['z'], sp['hin']],
                  out_shape=[_sds((b, l, ds), BF16), _sds((b, l, ds), BF16), _sds((b, nc, n, ds), F32)],
                  scratch_shapes=[pltpu.VMEM((n, ds), F32), pltpu.VMEM((q, ds), F32)],
                  compiler_params=_cparams(("parallel", "arbitrary")))(xbc, xbc, xbc, z, dt, da, dat, e_mat, dx, nw)


def _ssd_bwd(xbc, z, dt, da, dat, e_mat, et_mat, dx, nw, a128, yraw, hin, dy, name):
    b, l, dxw = xbc.shape
    ds = z.shape[2]
    heads, hpg, gw = _ssd_dims(ds)
    q, n = SSD_CHUNK, SSD_STATE
    gn = SSD_GROUPS * n
    nc = l // q
    hcol0 = ATTN_HEADS

    def body(xs_ref, bm_ref, cm_ref, z_ref, dt_ref, da_ref, dat_ref, e_ref, et_ref, dx_ref, nw_ref, a_ref, yraw_ref,
             hin_ref, dy_ref, dxs_ref, dbm_ref, dcm_ref, dz_ref, ddt_ref, dd_ref, dnw_ref, dap_ref, dhs, dxdt, dcsx,
             dtotx):
        i, c = pl.program_id(0), pl.program_id(1)

        @pl.when(c == 0)
        def _():
            dhs[...] = jnp.zeros_like(dhs)

        @pl.when((i == 0) & (c == 0))
        def _():
            dd_ref[...] = jnp.zeros_like(dd_ref)
            dnw_ref[...] = jnp.zeros_like(dnw_ref)
            dap_ref[...] = jnp.zeros_like(dap_ref)

        dtv = dt_ref[...]
        lower, cs, cst, dtx, csx, totx, xf, xdt = _ssd_common(da_ref[...], dat_ref[...], dtv, e_ref[...], xs_ref[...])
        upper = jnp.logical_not(lower) | (lax.broadcasted_iota(jnp.int32, (q, q), 0)
                                          == lax.broadcasted_iota(jnp.int32, (q, q), 1))
        bm, cm = bm_ref[...], cm_ref[...]
        ecs, dec_end, etot = jnp.exp(csx), jnp.exp(totx - csx), jnp.exp(totx)
        yraw = yraw_ref[...].astype(F32)
        zv = z_ref[...].astype(F32)
        sz = _silu(zv)
        yg = yraw * sz
        dyn_ = dy_ref[...].astype(F32)
        nwv = nw_ref[...]
        dygs, dnws = [], []
        for g in range(SSD_GROUPS):
            gs = slice(g * gw, (g + 1) * gw)
            r = lax.rsqrt(jnp.mean(yg[:, gs] * yg[:, gs], axis=1, keepdims=True) + NORM_EPS)
            yn = yg[:, gs] * r
            dn = dyn_[:, gs] * nwv[:, gs]
            dnws.append(jnp.sum(dyn_[:, gs] * yn, axis=0, keepdims=True))
            dygs.append(r * (dn - yn * jnp.mean(dn * yn, axis=1, keepdims=True)))
        dyg = jnp.concatenate(dygs, axis=1)
        dnw_ref[...] += jnp.concatenate(dnws, axis=1)
        dz_ref[...] = (dyg * yraw * _dsilu(zv)).astype(dz_ref.dtype)
        dyv = dyg * sz
        dd_ref[...] += jnp.sum(dyv * xf, axis=0, keepdims=True)
        hin, dh = hin_ref[...], dhs[...]
        lane128 = lax.broadcasted_iota(jnp.int32, (1, SMALL_W), 1)
        dcs = jnp.zeros((q, SMALL_W), F32)
        for g in range(SSD_GROUPS):
            gs = slice(g * gw, (g + 1) * gw)
            bg, cg = bm[:, g * n:(g + 1) * n], cm[:, g * n:(g + 1) * n]
            hg, dhg = hin[:, gs], dh[:, gs]
            hgb, dsb = hg.astype(BF16), dhg.astype(BF16)
            yoff = _dot(cg, hgb) * ecs[:, gs]
            dch = (dyv[:, gs] * ecs[:, gs]).astype(BF16)
            dcg = _dot_nt(dch, hgb)
            dhs[:, gs] = _dot_tn(cg, dch) + etot[:, gs] * dhg
            zg = xdt[:, gs] * dec_end[:, gs]
            dzz = _dot(bg, dsb)
            dbg = _dot_nt(zg.astype(BF16), dsb)
            dxdt_g = dzz * dec_end[:, gs]
            w_end = dzz * zg
            dtotx[:, gs] = jnp.sum(dhg * hg, axis=0, keepdims=True) * etot[:, gs] + jnp.sum(w_end, axis=0, keepdims=True)
            dcsx[:, gs] = dyv[:, gs] * yoff - w_end
            cb, cbt = _dot_nt(cg, bg), _dot_nt(bg, cg)
            dgm = jnp.zeros((q, q), F32)
            for e in range(hpg):
                hh = g * hpg + e
                cc = hcol0 + hh
                ccol, crow = cs[:, cc:cc + 1], cst[cc:cc + 1, :]
                lm = jnp.exp(jnp.where(lower, ccol - crow, NEG))
                lmt = jnp.exp(jnp.where(upper, crow - ccol, NEG))
                mm, mt = cb * lm, cbt * lmt
                hs = slice(hh * SSD_HEAD_DIM, (hh + 1) * SSD_HEAD_DIM)
                dye, xe = dyv[:, hs].astype(BF16), xdt[:, hs].astype(BF16)
                dm, dmt = _dot_nt(dye, xe), _dot_nt(xe, dye)
                dxdt[:, hs] = dxdt_g[:, e * SSD_HEAD_DIM:(e + 1) * SSD_HEAD_DIM] + _dot(mt.astype(BF16), dye)
                dgm = dgm + dm * lm
                rs = jnp.sum(dm * mm, axis=1, keepdims=True) - jnp.sum(dmt * mt, axis=1, keepdims=True)
                dcs = dcs + rs * (lane128 == cc).astype(F32)
            dgb = dgm.astype(BF16)
            dcm_ref[:, g * n:(g + 1) * n] = (dcg + _dot(dgb, bg)).astype(dcm_ref.dtype)
            dbm_ref[:, g * n:(g + 1) * n] = (dbg + _dot_tn(dgb, cg)).astype(dbm_ref.dtype)
        dxd = dxdt[...]
        dxs_ref[...] = (dx_ref[...] * dyv + dxd * dtx).astype(dxs_ref.dtype)
        et = et_ref[...]
        ddt = _dot(dxd * xf, et, hi=True)
        dtot128 = _dot(jnp.broadcast_to(dtotx[...], (8, ds)), et, hi=True)[0:1, :]
        row128 = lax.broadcasted_iota(jnp.int32, (q, SMALL_W), 0)
        dcs = dcs + _dot(dcsx[...], et, hi=True) + jnp.where(row128 == q - 1, dtot128, 0.0)
        dda = _dot(upper.astype(F32), dcs, hi=True)
        ddt_ref[...] = ddt + dda * a_ref[...]
        dap_ref[...] += jnp.sum(dda * dtv, axis=0, keepdims=True)

    rev = lambda c: nc - 1 - c
    sp = _ssd_specs(l, ds, rev)
    row3 = lambda w: pl.BlockSpec((None, q, w), lambda i, c: (i, rev(c), 0))
    acc = lambda w: pl.BlockSpec((1, w), lambda i, c: (0, 0))
    return _pcall(body, name=name, grid=(b, nc),
                  in_specs=[sp['xs'], sp['bm'], sp['cm'], sp['z'], sp['dt'], sp['da'], sp['dat'], sp['e'], sp['et'],
                            sp['vec'], sp['vec'], sp['vec128'], sp['z'], sp['hin'], sp['z']],
                  out_specs=[row3(ds), row3(gn), row3(gn), row3(ds), row3(SMALL_W), acc(ds), acc(ds), acc(SMALL_W)],
                  out_shape=[_sds((b, l, ds), BF16), _sds((b, l, gn), BF16), _sds((b, l, gn), BF16), _sds((b, l, ds), BF16),
                             _sds((b, l, SMALL_W), F32), _sds((1, ds), F32), _sds((1, ds), F32), _sds((1, SMALL_W), F32)],
                  scratch_shapes=[pltpu.VMEM((n, ds), F32), pltpu.VMEM((q, ds), F32), pltpu.VMEM((q, ds), F32),
                                  pltpu.VMEM((1, ds), F32)],
                  compiler_params=_cparams(("arbitrary", "arbitrary")))(
                      xbc, xbc, xbc, z, dt, da, dat, e_mat, et_mat, dx, nw, a128, yraw, hin, dy)


_GROUP_SIZE = {'c': 2, 'xy': 4, 'xyc': 8}


def _exchange(src, group, scatter, name, nsplit=1, copy_own=True):
    n = _GROUP_SIZE[group]
    rows, width = src.shape[-2:]
    assert src.ndim == (3 if scatter else 2)
    while rows % (8 * nsplit):
        nsplit //= 2
    crow = rows // nsplit

    def body(src_ref, out_ref, send_sems, recv_sems, local_sem):
        x, y, c = lax.axis_index("x"), lax.axis_index("y"), lax.axis_index("c")
        if group == 'c':
            rank = c
            dev = lambda r: (x, y, r)
        elif group == 'xy':
            rank = 2 * x + y
            dev = lambda r: (r // 2, r % 2, c)
        else:
            rank = 4 * x + 2 * y + c
            dev = lambda r: (r // 4, (r // 2) % 2, r % 2)

        def mine_for(r, ck):
            piece = src_ref.at[r] if scatter else src_ref
            return piece.at[pl.ds(ck * crow, crow)]

        def copy(k, ck, pr, dst_rank):
            return pltpu.make_async_remote_copy(
                src_ref=mine_for(pr, ck), dst_ref=out_ref.at[dst_rank].at[pl.ds(ck * crow, crow)],
                send_sem=send_sems.at[k * nsplit + ck], recv_sem=recv_sems.at[k * nsplit + ck], device_id=dev(pr),
                device_id_type=pl.DeviceIdType.MESH)

        if copy_own:
            local = pltpu.make_async_copy(src_ref.at[rank] if scatter else src_ref, out_ref.at[rank], local_sem)
            local.start()
        peers = [jnp.bitwise_xor(rank, k + 1) for k in range(n - 1)]
        sends = [copy(k, ck, pr, rank) for ck in range(nsplit) for k, pr in enumerate(peers)]
        for cp in sends:
            cp.start()
        for ck in range(nsplit):
            for k, pr in enumerate(peers):
                copy(k, ck, pr, pr).wait_recv()
        for cp in sends:
            cp.wait_send()
        if copy_own:
            local.wait()

    return _pcall(body, name=name, in_specs=[pl.BlockSpec(memory_space=pl.ANY)],
                  out_specs=pl.BlockSpec(memory_space=pl.ANY), out_shape=_sds((n, rows, width), src.dtype),
                  scratch_shapes=[pltpu.SemaphoreType.DMA(((n - 1) * nsplit,)),
                                  pltpu.SemaphoreType.DMA(((n - 1) * nsplit,)), pltpu.SemaphoreType.DMA])(src)


def _dims():
    d = D_MODEL
    h = ATTN_HEADS
    d_ssd = d
    d_xbc = d_ssd + 2 * SSD_GROUPS * SSD_STATE
    sizes = (d, d, d, h, d_ssd, d_xbc, d_ssd // SSD_HEAD_DIM, d, d, 3 * d)
    return d, h, d_ssd, d_xbc, sizes


def _w_in_split(w):
    d, h, d_ssd, d_xbc, sizes = _dims()
    off = np.concatenate([[0], np.cumsum(sizes)])
    seg = lambda i: w[..., off[i]:off[i + 1]]
    main = jnp.concatenate([seg(0), seg(1), seg(2), seg(4), seg(5), seg(7), seg(8), seg(9)], axis=-1)
    pad = jnp.zeros(w.shape[:-1] + (SMALL_W - sizes[3] - sizes[6],), w.dtype)
    small = jnp.concatenate([seg(3), seg(6), pad], axis=-1)
    return main, small


def _w_in_merge(main, small):
    d, h, d_ssd, d_xbc, sizes = _dims()
    order = (0, 1, 2, 4, 5, 7, 8, 9)
    moff = np.concatenate([[0], np.cumsum([sizes[i] for i in order])])
    pieces = {i: main[..., moff[j]:moff[j + 1]] for j, i in enumerate(order)}
    pieces[3] = small[..., :sizes[3]]
    pieces[6] = small[..., sizes[3]:sizes[3] + sizes[6]]
    return jnp.concatenate([pieces[i] for i in range(10)], axis=-1)


def _main_offsets():
    d, h, d_ssd, d_xbc, sizes = _dims()
    names = ('q', 'k', 'v', 'z', 'xbc', 'xr', 'gate', 'merge')
    widths = (d, d, d, d_ssd, d_xbc, d, d, 3 * d)
    off = np.concatenate([[0], np.cumsum(widths)])
    return {nm: (int(off[i]), int(off[i + 1])) for i, nm in enumerate(names)}


def _to_heads(t, b, l):
    return t.reshape(b, l, ATTN_HEADS, ATTN_HEAD_DIM).transpose(0, 2, 1, 3).astype(BF16)


def _from_heads(t):
    b, h, l, dh = t.shape
    return t.transpose(0, 2, 1, 3).reshape(b * l, h * dh)


def _block_diag(w):
    nb, s, _ = w.shape
    eye = jnp.eye(nb, dtype=w.dtype)
    return (eye[:, None, :, None] * w[:, :, None, :]).reshape(nb * s, nb * s)


def _diag_blocks(wd, nb):
    s = wd.shape[0] // nb
    return jnp.stack([wd[i * s:(i + 1) * s, i * s:(i + 1) * s] for i in range(nb)])


def _vec128(*parts):
    v = jnp.concatenate([p.astype(F32) for p in parts])
    return jnp.pad(v, (0, SMALL_W - v.shape[0]))[None, :]


def _ffn_fwd(h, gnorm, wgu, wd, tag):
    xn = _rms_fwd(h, gnorm[None, :], f"{tag}_norm")
    gu = _mm_nn(xn, wgu, BF16, name=f"{tag}_gu")
    act = _swiglu_fwd(gu, f"{tag}_act")
    out = _mm_nn(act, wd, F32, res=h, alpha=0.5, name=f"{tag}_down")
    return out, (h, xn, gu, act)


def _ffn_bwd(dout, saved, gnorm, wgu_t, wd_t, tag):
    h, xn, gu, act = saved
    dact = _mm_nn(dout, wd_t, BF16, alpha=0.5, name=f"{tag}_dact")
    dwd = _mm_tn(act, dout, alpha=0.5, name=f"{tag}_dwd")
    dgu = _swiglu_bwd(gu, dact, f"{tag}_dgu")
    dwgu = _mm_tn(xn, dgu, name=f"{tag}_dwgu")
    dxn = _mm_nn(dgu, wgu_t, F32, name=f"{tag}_dxn")
    dh, dg = _rms_bwd(h, dxn, dout, gnorm[None, :], f"{tag}_dnorm")
    return dh, dg[0], dwgu, dwd


def _mixer_fwd(h, p, b, l):
    d, nh, d_ssd, d_xbc, sizes = _dims()
    t = b * l
    off = _main_offsets()
    xn = _rms_fwd(h, p['mix_norm'][None, :], "mix_norm")
    pm = _mm_nn(xn, p['w_main'], BF16, name="mix_in_main")
    ps = _mm_nn(xn, p['w_small'], F32, name="mix_in_small")
    col = lambda nm: pm[:, off[nm][0]:off[nm][1]]
    heads_ssd = d_ssd // SSD_HEAD_DIM
    a_neg = -jnp.exp(p['ssd_a_log'])
    fb = _vec128(p['fox_forget_bias'])
    dtb = _vec128(jnp.zeros((nh,), F32), p['ssd_dt_bias'])
    a128 = _vec128(jnp.zeros((nh,), F32), a_neg)

    def prep(_, v, fbv, dtbv, av):
        lane = lax.broadcasted_iota(jnp.int32, (1, SMALL_W), 1)
        logf = jnp.where(lane < nh, -_softplus(-(v + fbv)), 0.0)
        dtv = jnp.where((lane >= nh) & (lane < nh + heads_ssd), _softplus(v + dtbv), 0.0)
        return logf, dtv, dtv * av
    logf, dt, da = _rowwise(prep, [ps], [(SMALL_W, F32)] * 3, bcast=[fb, dtb, a128], name="mix_prep")

    blk = _attn_blk(l)
    cum = _cumsum_seq(logf.reshape(b, l, SMALL_W), False, "fox_cumsum")
    ch = cum[:, :, :nh].transpose(0, 2, 1)
    qa, ka, va = _attn_operands(_to_heads(col('q'), b, l), _to_heads(col('k'), b, l), _to_heads(col('v'), b, l), ch)
    oh, lse = _flash_fwd(_pad_lanes([qa], ATTN_W), ka, va, ATTN_HEAD_DIM, blk, "fox_fwd")
    y_a = _from_heads(oh)

    xbc = col('xbc').reshape(b, l, d_xbc)
    pre_b = _conv_fwd(xbc, p['ssd_conv_w'], p['ssd_conv_b'][None, :], BF16, "ssd_conv")
    xbc_act = _rowwise(lambda _, v: _silu(v.astype(F32)), [pre_b.reshape(t, d_xbc)], [(d_xbc, BF16)],
                       name="ssd_conv_act")[0].reshape(b, l, d_xbc)
    z = col('z').reshape(b, l, d_ssd)
    dt3, da3 = dt.reshape(b, l, SMALL_W), da.reshape(b, l, SMALL_W)
    dat3 = da3.transpose(0, 2, 1)
    e_mat = _expand_matrix(nh, heads_ssd)
    dx = jnp.repeat(p['ssd_d'], SSD_HEAD_DIM)[None, :]
    nw = p['ssd_norm'][None, :]
    y_b3, yraw, hin = _ssd_fwd(xbc_act, z, dt3, da3, dat3, e_mat, dx, nw, "ssd_fwd")
    y_b = y_b3.reshape(t, d_ssd)

    xr = col('xr').reshape(b, l, d)
    xc = _conv_fwd(xr, p['lru_conv_w'], p['lru_conv_b'][None, :], F32, "lru_conv").reshape(t, d)
    pre_ri = _mm_nn(xc, p['lru_w_ri'], F32, name="lru_gates")
    lvec = (p['lru_b_a'][None, :], p['lru_b_x'][None, :], p['lru_lambda'][None, :])
    a_l, u_l = _rowwise(_lru_point_fwd, [pre_ri, xc], [(d, F32)] * 2, bcast=lvec, name="lru_point", period=l)
    hs = _linear_scan(a_l.reshape(b, l, d), u_l.reshape(b, l, d), False, "lru_scan").reshape(t, d)
    gate = col('gate')
    y_c = _rowwise(lambda _, hv, gv: hv * _gelu(gv.astype(F32)), [hs, gate], [(d, BF16)], name="lru_out")[0]

    ba = _mm_nn(y_a, p['w_branch_attn'], BF16, name="branch_attn")
    bb = _mm_nn(y_b, p['w_branch_ssd'], BF16, name="branch_ssd")
    bc = _mm_nn(y_c, p['w_branch_lru'], BF16, name="branch_lru")
    mg = col('merge')
    mixed = _merge_fwd(mg, ba, bb, bc, "merge")
    out = _mm_nn(mixed, p['w_out'], F32, res=h, name="mix_out")
    saved = dict(h=h, xn=xn, ps=ps, fb=fb, dtb=dtb, a128=a128, qa=qa, ka=ka, va=va, oh=oh, lse=lse,
                 xbc=xbc, pre_b=pre_b, xbc_act=xbc_act, z=z, dt3=dt3, da3=da3, dat3=dat3, e_mat=e_mat, dx=dx, nw=nw,
                 yraw=yraw, hin=hin, xr=xr, xc=xc, pre_ri=pre_ri, lvec=lvec, a_l=a_l, hs=hs, gate=gate, y_a=y_a, y_b=y_b,
                 y_c=y_c, ba=ba, bb=bb, bc=bc, mg=mg, mixed=mixed)
    return out, saved


def _expand_matrix(nh, heads_ssd):
    e = np.zeros((SMALL_W, heads_ssd * SSD_HEAD_DIM), np.float32)
    for hh in range(heads_ssd):
        e[nh + hh, hh * SSD_HEAD_DIM:(hh + 1) * SSD_HEAD_DIM] = 1.0
    return jnp.asarray(e)


def _lru_gates(pre, xc, bav, bxv, lamv, pos):
    d = xc.shape[1]
    r = _sigmoid(pre[:, :d] + bav)
    i = _sigmoid(pre[:, d:] + bxv)
    ls = -_softplus(-lamv)
    la = LRU_C * r * ls
    a = jnp.exp(la)
    mult = jnp.where(pos == 0, 1.0, jnp.sqrt(-_expm1(2.0 * la)))
    return r, i, ls, a, mult


def _lru_point_fwd(pos, pre, xc, bav, bxv, lamv):
    r, i, ls, a, mult = _lru_gates(pre, xc, bav, bxv, lamv, pos)
    return a, mult * (i * xc)


def _lru_point_bwd(pos, g, hprev, pre, xc, bav, bxv, lamv):
    r, i, ls, a, mult = _lru_gates(pre, xc, bav, bxv, lamv, pos)
    da = g * hprev
    di = g * mult * xc
    dxc = g * mult * i
    dmult = jnp.where(pos == 0, 0.0, g * i * xc)
    dla = da * a - dmult * (a * a) / mult
    dpre_r = dla * (LRU_C * ls) * r * (1.0 - r)
    dpre_i = di * i * (1.0 - i)
    dlam = jnp.sum(dla * (LRU_C * r), axis=0, keepdims=True) * _sigmoid(-lamv)
    return (jnp.concatenate([dpre_r, dpre_i], axis=1), dxc, dlam, jnp.sum(dpre_r, axis=0, keepdims=True),
            jnp.sum(dpre_i, axis=0, keepdims=True))


def _mixer_bwd(dout, s, p, b, l):
    d, nh, d_ssd, d_xbc, sizes = _dims()
    t = b * l
    heads_ssd = d_ssd // SSD_HEAD_DIM
    g = {}
    dmixed = _mm_nn(dout, p['w_out_t'], BF16, name="mix_out_dx")
    g['w_out'] = _mm_tn(s['mixed'], dout, name="mix_out_dw")
    dba, dbb, dbc, dmerge = _merge_bwd(s['mg'], s['ba'], s['bb'], s['bc'], dmixed, "merge_bwd")
    g['w_branch_attn'] = _mm_tn(s['y_a'], dba, name="branch_attn_dw")
    g['w_branch_ssd'] = _mm_tn(s['y_b'], dbb, name="branch_ssd_dw")
    g['w_branch_lru'] = _mm_tn(s['y_c'], dbc, name="branch_lru_dw")
    dy_a = _mm_nn(dba, p['w_branch_attn_t'], BF16, name="branch_attn_dx")
    dy_b = _mm_nn(dbb, p['w_branch_ssd_t'], BF16, name="branch_ssd_dx")
    dy_c = _mm_nn(dbc, p['w_branch_lru_t'], F32, name="branch_lru_dx")

    dgate, dhs = _rowwise(lambda _, dv, hv, gv: (dv * hv * _dgelu(gv.astype(F32)), dv * _gelu(gv.astype(F32))),
                          [dy_c, s['hs'], s['gate']], [(d, BF16), (d, F32)], name="lru_out_bwd")
    a3 = s['a_l'].reshape(b, l, d)
    a_next = jnp.concatenate([a3[:, 1:], jnp.zeros((b, 1, d), F32)], axis=1)
    gs = _linear_scan(a_next, dhs.reshape(b, l, d), True, "lru_scan_bwd").reshape(t, d)
    h3 = s['hs'].reshape(b, l, d)
    hprev = jnp.concatenate([jnp.zeros((b, 1, d), F32), h3[:, :-1]], axis=1).reshape(t, d)
    dpre_ri, dxc0, dlam, dba_, dbx_ = _rowwise(_lru_point_bwd, [gs, hprev, s['pre_ri'], s['xc']],
                                               [(2 * d, BF16), (d, F32)], bcast=s['lvec'],
                                               reds=[(1, d)] * 3, name="lru_point_bwd", period=l)
    g['lru_lambda'], g['lru_b_a'], g['lru_b_x'] = dlam[0], dba_[0], dbx_[0]
    dxc = _mm_nn(dpre_ri, p['lru_w_ri_t'], BF16, res=dxc0, name="lru_gates_dx")
    dw_ri = _mm_tn(s['xc'], dpre_ri, name="lru_gates_dw")
    g['lru_w_a'] = _diag_blocks(dw_ri[:, :d], LRU_BLOCKS)
    g['lru_w_x'] = _diag_blocks(dw_ri[:, d:], LRU_BLOCKS)
    dxr, dwl = _conv_bwd(s['xr'], dxc.reshape(b, l, d), p['lru_conv_w'], "lru_conv_bwd")
    g['lru_conv_w'], g['lru_conv_b'] = dwl[:CONV_K], dwl[CONV_K]

    et_mat = s['e_mat'].T
    dxs, dbm, dcm, dz, ddt, dd_l, dnw, dap = _ssd_bwd(s['xbc_act'], s['z'], s['dt3'], s['da3'], s['dat3'], s['e_mat'],
                                                      et_mat, s['dx'], s['nw'], s['a128'], s['yraw'], s['hin'],
                                                      dy_b.reshape(b, l, d_ssd), "ssd_bwd")
    g['ssd_d'] = dd_l.reshape(heads_ssd, SSD_HEAD_DIM).sum(axis=1)
    g['ssd_norm'] = dnw[0]
    g['ssd_a_log'] = dap[0, nh:nh + heads_ssd] * (-jnp.exp(p['ssd_a_log']))
    dxbc_act = jnp.concatenate([dxs, dbm, dcm], axis=2).reshape(t, d_xbc)
    dpre_b = _rowwise(lambda _, dv, pv: dv.astype(F32) * _dsilu(pv.astype(F32)),
                      [dxbc_act, s['pre_b'].reshape(t, d_xbc)], [(d_xbc, BF16)], name="ssd_conv_act_bwd")[0]
    dxbc, dws = _conv_bwd(s['xbc'], dpre_b.reshape(b, l, d_xbc), p['ssd_conv_w'], "ssd_conv_bwd")
    g['ssd_conv_w'], g['ssd_conv_b'] = dws[:CONV_K], dws[CONV_K]

    doh = _to_heads(dy_a, b, l)
    dd = _rowwise(lambda _, ov, dv: jnp.sum(ov.astype(F32) * dv.astype(F32), axis=1, keepdims=True),
                  [s['oh'].reshape(-1, ATTN_HEAD_DIM), doh.reshape(-1, ATTN_HEAD_DIM)], [(1, F32)],
                  name="fox_delta")[0].reshape(b, nh, l, 1)
    blk = _attn_blk(l)
    dh_ = ATTN_HEAD_DIM
    qa2 = _pad_lanes([s['qa']] + [-p for p in _split3(s['lse'][..., 0])], ATTN_W)
    doa = _pad_lanes([doh] + [-p for p in _split3(dd[..., 0])], ATTN_W)
    dka, dva = _flash_bwd_kv(qa2, s['ka'], s['va'], doa, blk, "fox_bwd_kv")
    dqa = _flash_bwd_q(qa2, s['ka'], s['va'], doa, dh_, blk, "fox_bwd_q")
    dqh, dkh, dvh = dqa[..., :dh_], dka[..., :dh_], dva[..., :dh_]
    dcum = jnp.pad((dqa[..., dh_ + _AUG_C] - dka[..., dh_ + _AUG_ONE]).transpose(0, 2, 1),
                   ((0, 0), (0, 0), (0, SMALL_W - nh)))
    dlogf = _cumsum_seq(dcum, True, "fox_cumsum_bwd").reshape(t, SMALL_W)

    def prep_bwd(_, v, dlf, ddtv, fbv, dtbv):
        a_ = dlf * _sigmoid(-(v + fbv))
        b_ = ddtv * _sigmoid(v + dtbv)
        return a_ + b_, jnp.sum(a_, axis=0, keepdims=True), jnp.sum(b_, axis=0, keepdims=True)
    dps, dfb, ddtb = _rowwise(prep_bwd, [s['ps'], dlogf, ddt.reshape(t, SMALL_W)], [(SMALL_W, F32)],
                              bcast=[s['fb'], s['dtb']], reds=[(1, SMALL_W)] * 2, name="mix_prep_bwd")
    g['fox_forget_bias'] = dfb[0, :nh]
    g['ssd_dt_bias'] = ddtb[0, nh:nh + heads_ssd]

    dpm = jnp.concatenate([_from_heads(dqh).astype(BF16), _from_heads(dkh).astype(BF16), _from_heads(dvh).astype(BF16),
                           dz.reshape(t, d_ssd), dxbc.reshape(t, d_xbc), dxr.reshape(t, d), dgate, dmerge], axis=1)
    dxn = _mm_nn(dps, p['w_small_t'], F32, name="mix_in_small_dx")
    dxn = _mm_nn(dpm, p['w_main_t'], F32, res=dxn, name="mix_in_main_dx")
    g['w_main'] = _mm_tn(s['xn'], dpm, name="mix_in_main_dw")
    g['w_small'] = _mm_tn(s['xn'], dps, name="mix_in_small_dw")
    dh, dg = _rms_bwd(s['h'], dxn, dout, p['mix_norm'][None, :], "mix_norm_bwd")
    g['mix_norm'] = dg[0]
    return dh, g


def _layer_params(w, li):
    p = {n: w[n][li] for n in WEIGHTS if n not in ('meta_tokens', 'final_norm')}
    bf = lambda a: a.astype(BF16)
    p['ffn1_gu'], p['ffn1_gu_t'] = bf(p['ffn1_w_gate_up']), bf(p['ffn1_w_gate_up']).T
    p['ffn1_d'], p['ffn1_d_t'] = bf(p['ffn1_w_down']), bf(p['ffn1_w_down']).T
    p['ffn2_gu'], p['ffn2_gu_t'] = bf(p['ffn2_w_gate_up']), bf(p['ffn2_w_gate_up']).T
    p['ffn2_d'], p['ffn2_d_t'] = bf(p['ffn2_w_down']), bf(p['ffn2_w_down']).T
    wm, ws = _w_in_split(bf(p['w_in']))
    p['w_main'], p['w_main_t'], p['w_small'], p['w_small_t'] = wm, wm.T, ws, ws.T
    for n in ('w_branch_attn', 'w_branch_ssd', 'w_branch_lru', 'w_out'):
        p[n + '_t'] = bf(p[n]).T
        p[n] = bf(p[n])
    wri = jnp.concatenate([_block_diag(p['lru_w_a']), _block_diag(p['lru_w_x'])], axis=1)
    p['lru_w_ri'], p['lru_w_ri_t'] = bf(wri), bf(wri).T
    return p


def _local_step(x, loss_target, w):
    b, seq, d = x.shape
    length = N_META + seq
    l = -(-length // Q_BLOCK) * Q_BLOCK
    t = b * l
    meta = jnp.broadcast_to(w['meta_tokens'].astype(F32)[None], (b, N_META, d))
    h = jnp.concatenate([meta, x, jnp.zeros((b, l - length, d), F32)], axis=1).reshape(t, d)
    tgt = jnp.concatenate([jnp.zeros((b, N_META, d), F32), loss_target, jnp.zeros((b, l - length, d), F32)],
                          axis=1).reshape(t, d)
    params, saves = [], []
    for li in range(DEPTH):
        p = _layer_params(w, li)
        h, s1 = _ffn_fwd(h, p['ffn1_norm'], p['ffn1_gu'], p['ffn1_d'], "ffn1")
        h, sm = _mixer_fwd(h, p, b, l)
        h, s2 = _ffn_fwd(h, p['ffn2_norm'], p['ffn2_gu'], p['ffn2_d'], "ffn2")
        params.append(p)
        saves.append((s1, sm, s2))
    dh, loss, dgf = _loss_head(h, tgt, w['final_norm'][None, :], l, "loss_head")
    layer_grads = [None] * DEPTH
    for li in reversed(range(DEPTH)):
        p = params[li]
        s1, sm, s2 = saves[li]
        g = {}
        dh, g['ffn2_norm'], g['ffn2_w_gate_up'], g['ffn2_w_down'] = _ffn_bwd(dh, s2, p['ffn2_norm'], p['ffn2_gu_t'],
                                                                              p['ffn2_d_t'], "ffn2b")
        dh, gm = _mixer_bwd(dh, sm, p, b, l)
        g.update(gm)
        g['w_in'] = _w_in_merge(g.pop('w_main'), g.pop('w_small'))
        dh, g['ffn1_norm'], g['ffn1_w_gate_up'], g['ffn1_w_down'] = _ffn_bwd(dh, s1, p['ffn1_norm'], p['ffn1_gu_t'],
                                                                              p['ffn1_d_t'], "ffn1b")
        layer_grads[li] = g
    grads = {n: jnp.stack([layer_grads[li][n] for li in range(DEPTH)]) for n in layer_grads[0]}
    for n in ('lru_w_a', 'lru_w_x'):
        grads[n] = grads[n].reshape(w[n].shape)
    dh3 = dh.reshape(b, l, d)
    grads['meta_tokens'] = jnp.sum(dh3[:, :N_META], axis=0)
    grads['final_norm'] = dgf[0]
    return loss, dh3[:, N_META:N_META + seq], grads


_ROW_W = 1024
_D2D_SPLIT = 16


def _half_size(total, row_quant):
    q = _ROW_W * row_quant
    return -(-(-(-total // 2)) // q) * q


def _flat_shard(parts, half):
    v = jnp.concatenate([p.reshape(-1) for p in parts])
    return jnp.pad(v, (0, 2 * half - v.shape[0]))


def _chip_shards(full, axis, nchip):
    return jnp.split(full, nchip, axis=axis + 1)


def _unflatten(flat, shapes):
    out, o = [], 0
    for sh in shapes:
        n = int(np.prod(sh))
        out.append(flat[o:o + n].reshape(sh))
        o += n
    return out


def kernel(x, meta_tokens, ffn1_norm, ffn1_w_gate_up, ffn1_w_down, mix_norm, w_in, fox_forget_bias, ssd_conv_w, ssd_conv_b, ssd_dt_bias, ssd_a_log, ssd_d, ssd_norm, lru_conv_w, lru_conv_b, lru_w_a, lru_b_a, lru_w_x, lru_b_x, lru_lambda, w_branch_attn, w_branch_ssd, w_branch_lru, w_out, ffn2_norm, ffn2_w_gate_up, ffn2_w_down, final_norm, loss_target, m_meta_tokens, m_ffn1_norm, m_ffn1_w_gate_up, m_ffn1_w_down, m_mix_norm, m_w_in, m_fox_forget_bias, m_ssd_conv_w, m_ssd_conv_b, m_ssd_dt_bias, m_ssd_a_log, m_ssd_d, m_ssd_norm, m_lru_conv_w, m_lru_conv_b, m_lru_w_a, m_lru_b_a, m_lru_w_x, m_lru_b_x, m_lru_lambda, m_w_branch_attn, m_w_branch_ssd, m_w_branch_lru, m_w_out, m_ffn2_norm, m_ffn2_w_gate_up, m_ffn2_w_down, m_final_norm, v_meta_tokens, v_ffn1_norm, v_ffn1_w_gate_up, v_ffn1_w_down, v_mix_norm, v_w_in, v_fox_forget_bias, v_ssd_conv_w, v_ssd_conv_b, v_ssd_dt_bias, v_ssd_a_log, v_ssd_d, v_ssd_norm, v_lru_conv_w, v_lru_conv_b, v_lru_w_a, v_lru_b_a, v_lru_w_x, v_lru_b_x, v_lru_lambda, v_w_branch_attn, v_w_branch_ssd, v_w_branch_lru, v_w_out, v_ffn2_norm, v_ffn2_w_gate_up, v_ffn2_w_down, v_final_norm):
    args = locals()
    wloc = {n: args[n] for n in WEIGHTS}
    mloc = {n: args['m_' + n] for n in WEIGHTS}
    vloc = {n: args['v_' + n] for n in WEIGHTS}
    nchip = 4
    chip = 2 * lax.axis_index("x") + lax.axis_index("y")
    core = lax.axis_index("c")
    row_quant = 256 if D_MODEL >= 1024 else 8

    big_shapes = [wloc[n].shape for n in BIG_NAMES]
    total = sum(int(np.prod(s)) for s in big_shapes)
    half = _half_size(total, row_quant)
    hrows = half // _ROW_W
    wflat = _flat_shard([wloc[n].astype(BF16) for n in BIG_NAMES], half).reshape(2, hrows, _ROW_W)
    my_half = lax.dynamic_index_in_dim(wflat, core, axis=0, keepdims=False)
    got = _exchange(my_half, 'xy', False, "gather_w_chips")
    both = _exchange(got.reshape(nchip * hrows, _ROW_W), 'c', False, "gather_w_cores", nsplit=_D2D_SPLIT)
    wall = both.reshape(2, nchip, hrows, _ROW_W).transpose(1, 0, 2, 3).reshape(nchip, 2 * half)
    full = {}
    per_chip = [_unflatten(wall[j], big_shapes) for j in range(nchip)]
    for i, n in enumerate(BIG_NAMES):
        full[n] = jnp.concatenate([per_chip[j][i] for j in range(nchip)], axis=BIG[n] + 1)
    cs_shapes = [wloc[n].shape for n in COLSHARD_SMALL]
    cs_total = sum(int(np.prod(s)) for s in cs_shapes)
    cs_rows = -(-cs_total // (8 * 128)) * 8
    cs_flat = jnp.concatenate([wloc[n].reshape(-1) for n in COLSHARD_SMALL])
    cs_flat = jnp.pad(cs_flat, (0, cs_rows * 128 - cs_total)).reshape(cs_rows, 128)
    cs_all = _exchange(cs_flat, 'xy', False, "gather_small").reshape(nchip, -1)
    cs_chip = [_unflatten(cs_all[j], cs_shapes) for j in range(nchip)]
    for i, n in enumerate(COLSHARD_SMALL):
        full[n] = jnp.concatenate([cs_chip[j][i] for j in range(nchip)], axis=-1)
    for n in SMALL_NAMES:
        if n not in COLSHARD_SMALL:
            full[n] = wloc[n]

    loss_part, grad_x, grads = _local_step(x, loss_target, full)

    gsh = [[s for s in _chip_shards(grads[n], BIG[n], nchip)] for n in BIG_NAMES]
    gflat = jnp.stack([_flat_shard([gsh[i][j] for i in range(len(BIG_NAMES))], half) for j in range(nchip)])
    gflat = gflat.reshape(nchip, 2, hrows, _ROW_W).transpose(1, 0, 2, 3).reshape(2, nchip * hrows, _ROW_W)
    give = lax.dynamic_index_in_dim(gflat, 1 - core, axis=0, keepdims=False)
    keep = lax.dynamic_index_in_dim(gflat, core, axis=0, keepdims=False)
    pair = _exchange(give, 'c', False, "reduce_cores", nsplit=_D2D_SPLIT, copy_own=False)
    theirs = lax.dynamic_index_in_dim(pair, 1 - core, axis=0, keepdims=False)
    psum = _sum_rows([keep, theirs], BF16, "reduce_cores_sum").reshape(nchip, hrows, _ROW_W)
    parts = _exchange(psum, 'xy', True, "reduce_chips")
    rsum = _sum_rows([parts[j] for j in range(nchip)], F32, "reduce_chips_sum")
    halves = _exchange(rsum, 'c', False, "reduce_share", nsplit=_D2D_SPLIT)
    gshard = halves.reshape(-1)
    gbig = dict(zip(BIG_NAMES, _unflatten(gshard, big_shapes)))

    sm_shapes = [grads[n].shape for n in SMALL_NAMES]
    sm_total = sum(int(np.prod(s)) for s in sm_shapes) + 128
    sm_rows = -(-sm_total // (8 * 128)) * 8
    sm_flat = jnp.concatenate([loss_part.reshape(-1)] + [grads[n].reshape(-1) for n in SMALL_NAMES])
    sm_flat = jnp.pad(sm_flat, (0, sm_rows * 128 - sm_total)).reshape(sm_rows, 128)
    sm_all = _exchange(sm_flat, 'xyc', False, "gather_small_grads")
    sm_sum = _sum_rows([sm_all[j] for j in range(8)], F32, "small_grads_sum").reshape(-1)
    loss = sm_sum[0]
    gsmall_full = dict(zip(SMALL_NAMES, _unflatten(sm_sum[128:], sm_shapes)))
    gsmall = {}
    for n in SMALL_NAMES:
        gfull = gsmall_full[n]
        if n in COLSHARD_SMALL:
            wcols = wloc[n].shape[-1]
            gfull = lax.dynamic_slice_in_dim(gfull, chip * wcols, wcols, axis=gfull.ndim - 1)
        gsmall[n] = gfull

    def flat_big(dct):
        return _flat_shard([dct[n] for n in BIG_NAMES], half).reshape(2 * hrows, _ROW_W)
    dl, mn, vn = _adamw(flat_big(wloc), gshard.reshape(2 * hrows, _ROW_W), flat_big(mloc), flat_big(vloc), "adamw_big")
    big_out = [dict(zip(BIG_NAMES, _unflatten(a.reshape(-1), big_shapes))) for a in (dl, mn, vn)]
    loc_shapes = [wloc[n].shape for n in SMALL_NAMES]
    loc_total = sum(int(np.prod(s)) for s in loc_shapes)
    loc_rows = -(-loc_total // (8 * 128)) * 8

    def flat_small(dct):
        v = jnp.concatenate([dct[n].reshape(-1) for n in SMALL_NAMES])
        return jnp.pad(v, (0, loc_rows * 128 - loc_total)).reshape(loc_rows, 128)
    dls, mns, vns = _adamw(flat_small(wloc), flat_small(gsmall), flat_small(mloc), flat_small(vloc), "adamw_small")
    small_out = [dict(zip(SMALL_NAMES, _unflatten(a.reshape(-1), loc_shapes))) for a in (dls, mns, vns)]

    grad_w = {**gbig, **gsmall}
    outs = [loss, grad_x] + [grad_w[n] for n in WEIGHTS]
    for k in range(3):
        merged = {**big_out[k], **small_out[k]}
        outs += [merged[n] for n in WEIGHTS]
    return tuple(outs)
```

```python
import functools
import math

import numpy as np
import jax
import jax.numpy as jnp
from jax import lax
from jax.experimental import pallas as pl
from jax.experimental.pallas import tpu as pltpu

F32 = jnp.float32
BF16 = jnp.bfloat16
HI = lax.Precision.HIGHEST
VMEM_LIMIT_BYTES = 56 * 1024 * 1024
NEG = -1e30

D_MODEL = 1024
SEQ = 4096
DEPTH = 4
N_META = 16
Q_BLOCK = 128
SSD_CHUNK = 128
NORM_EPS = 1e-6
ATTN_HEADS = 16
ATTN_HEAD_DIM = 64
SSD_HEAD_DIM = 64
SSD_GROUPS = 2
SSD_STATE = 128
CONV_K = 4
LRU_BLOCKS = 16
LRU_C = 8.0
D_FF = 2816
ADAM_LR = 0.001
ADAM_B1 = 0.9
ADAM_B2 = 0.999
ADAM_EPS = 1e-08
ADAM_WD = 0.01
ADAM_STEP = 10
SMALL_W = 128
_ROWWISE_TILE_ELEMS = 512 * 1024

WEIGHTS = ['meta_tokens', 'ffn1_norm', 'ffn1_w_gate_up', 'ffn1_w_down', 'mix_norm', 'w_in', 'fox_forget_bias',
           'ssd_conv_w', 'ssd_conv_b', 'ssd_dt_bias', 'ssd_a_log', 'ssd_d', 'ssd_norm', 'lru_conv_w', 'lru_conv_b',
           'lru_w_a', 'lru_b_a', 'lru_w_x', 'lru_b_x', 'lru_lambda', 'w_branch_attn', 'w_branch_ssd', 'w_branch_lru',
           'w_out', 'ffn2_norm', 'ffn2_w_gate_up', 'ffn2_w_down', 'final_norm']
BIG = {'ffn1_w_gate_up': 1, 'ffn1_w_down': 0, 'w_in': 1, 'w_branch_attn': 0, 'w_branch_ssd': 0, 'w_branch_lru': 0,
       'w_out': 0, 'ffn2_w_gate_up': 1, 'ffn2_w_down': 0}
BIG_NAMES = [n for n in WEIGHTS if n in BIG]
COLSHARD_SMALL = ['meta_tokens', 'ssd_conv_w', 'lru_conv_w']
SMALL_NAMES = [n for n in WEIGHTS if n not in BIG]


def _pick(n, cands):
    for c in cands:
        if n % c == 0:
            return c
    raise ValueError(f"no tile for {n} in {cands}")


def _pcall(body, **kw):
    return pl.pallas_call(body, **kw)


def _cparams(sem):
    return pltpu.CompilerParams(dimension_semantics=sem, vmem_limit_bytes=VMEM_LIMIT_BYTES)


def _sds(shape, dtype):
    return jax.ShapeDtypeStruct(tuple(shape), dtype)


def _dot(a, b, hi=False):
    return jnp.dot(a, b, precision=HI if hi else None, preferred_element_type=F32)


def _dot_nt(a, b):
    return lax.dot_general(a, b, (((1,), (1,)), ((), ())), preferred_element_type=F32)


def _dot_tn(a, b):
    return lax.dot_general(a, b, (((0,), (0,)), ((), ())), preferred_element_type=F32)


def _sigmoid(x):
    return 1.0 / (1.0 + jnp.exp(-x))


def _softplus(x):
    return jnp.maximum(x, 0.0) + jnp.log1p(jnp.exp(-jnp.abs(x)))


def _silu(x):
    return x * _sigmoid(x)


def _dsilu(x):
    s = _sigmoid(x)
    return s * (1.0 + x * (1.0 - s))


_GELU_C = math.sqrt(2.0 / math.pi)


def _gelu(x):
    return 0.5 * x * (1.0 + jnp.tanh(_GELU_C * (x + 0.044715 * x * x * x)))


def _dgelu(x):
    t = jnp.tanh(_GELU_C * (x + 0.044715 * x * x * x))
    return 0.5 * (1.0 + t) + 0.5 * x * (1.0 - t * t) * _GELU_C * (1.0 + 3.0 * 0.044715 * x * x)


def _expm1(x):
    series = x * (1.0 + x * 0.5 * (1.0 + x * (1.0 / 3.0) * (1.0 + x * 0.25 * (1.0 + x * 0.2))))
    return jnp.where(jnp.abs(x) < 0.05, series, jnp.exp(x) - 1.0)


def _rowwise(fn, ins, outs, *, bcast=(), reds=(), tm=None, name, period=None):
    t_rows = ins[0].shape[0]
    if tm is None:
        widest = max([a.shape[1] for a in ins] + [c for c, _ in outs])
        tm = _pick(math.gcd(t_rows, period or t_rows),
                   [c for c in (384, 256, 128, 64, 32, 16, 8) if c * widest <= _ROWWISE_TILE_ELEMS or c == 8])
    nt = t_rows // tm
    assert t_rows % tm == 0 and (period is None or period % tm == 0)
    n_in, n_out = len(ins) + len(bcast), len(outs)

    def body(*refs):
        i = pl.program_id(0)
        pos = None
        if period is not None:
            pos = (i * tm) % period + lax.broadcasted_iota(jnp.int32, (tm, 1), 0)
        res = fn(pos, *[r[...] for r in refs[:n_in]])
        res = res if isinstance(res, tuple) else (res,)
        for r, v in zip(refs[n_in:n_in + n_out], res[:n_out]):
            r[...] = v.astype(r.dtype)
        red_refs = refs[n_in + n_out:]
        if red_refs:
            @pl.when(i == 0)
            def _():
                for r in red_refs:
                    r[...] = jnp.zeros_like(r)
            for r, v in zip(red_refs, res[n_out:]):
                r[...] += v

    in_specs = [pl.BlockSpec((tm, a.shape[1]), lambda i: (i, 0)) for a in ins]
    in_specs += [pl.BlockSpec(b.shape, lambda i, n=b.ndim: (0,) * n) for b in bcast]
    out_specs = [pl.BlockSpec((tm, c), lambda i: (i, 0)) for c, _ in outs]
    out_specs += [pl.BlockSpec(s, lambda i: (0, 0)) for s in reds]
    out_shape = [_sds((t_rows, c), dt) for c, dt in outs] + [_sds(s, F32) for s in reds]
    res = _pcall(body, name=name, grid=(nt,), in_specs=in_specs, out_specs=out_specs, out_shape=out_shape,
                 compiler_params=_cparams(("arbitrary",) if reds else ("parallel",)))(*ins, *bcast)
    return res


_TM = (768, 384, 256, 128)
_TN = (1536, 1408, 1024, 768, 512, 640, 384, 256, 128)
_TK = (1024, 1408, 512, 384, 256, 128)


def _mm_nn(a, b, out_dtype, *, res=None, alpha=1.0, name):
    m, k = a.shape
    k2, n = b.shape
    assert k == k2
    tm, tn, tk = _pick(m, _TM), _pick(n, _TN), _pick(k, _TK)
    nk = k // tk

    def body(*refs):
        if res is None:
            a_ref, b_ref, o_ref, acc = refs
            r_ref = None
        else:
            a_ref, b_ref, r_ref, o_ref, acc = refs
        kk = pl.program_id(2)

        @pl.when(kk == 0)
        def _():
            acc[...] = jnp.zeros_like(acc)

        acc[...] += _dot(a_ref[...].astype(BF16), b_ref[...].astype(BF16))

        @pl.when(kk == nk - 1)
        def _():
            v = acc[...]
            if alpha != 1.0:
                v = v * alpha
            if r_ref is not None:
                v = r_ref[...].astype(F32) + v
            o_ref[...] = v.astype(o_ref.dtype)

    in_specs = [pl.BlockSpec((tm, tk), lambda j, i, kk: (i, kk)), pl.BlockSpec((tk, tn), lambda j, i, kk: (kk, j))]
    args = [a, b]
    if res is not None:
        in_specs.append(pl.BlockSpec((tm, tn), lambda j, i, kk: (i, j)))
        args.append(res)
    return _pcall(body, name=name, grid=(n // tn, m // tm, nk), in_specs=in_specs,
                  out_specs=pl.BlockSpec((tm, tn), lambda j, i, kk: (i, j)), out_shape=_sds((m, n), out_dtype),
                  scratch_shapes=[pltpu.VMEM((tm, tn), F32)],
                  compiler_params=_cparams(("parallel", "parallel", "arbitrary")))(*args)


def _mm_tn(a, b, *, alpha=1.0, name):
    m, k = a.shape
    m2, n = b.shape
    assert m == m2
    tm, tn, tko = _pick(m, _TM), _pick(n, _TN), _pick(k, _TK)
    nm = m // tm

    def body(a_ref, b_ref, o_ref, acc):
        mm = pl.program_id(2)

        @pl.when(mm == 0)
        def _():
            acc[...] = jnp.zeros_like(acc)

        acc[...] += _dot_tn(a_ref[...].astype(BF16), b_ref[...].astype(BF16))

        @pl.when(mm == nm - 1)
        def _():
            v = acc[...]
            o_ref[...] = v * alpha if alpha != 1.0 else v

    return _pcall(body, name=name, grid=(k // tko, n // tn, nm),
                  in_specs=[pl.BlockSpec((tm, tko), lambda i, j, mm: (mm, i)),
                            pl.BlockSpec((tm, tn), lambda i, j, mm: (mm, j))],
                  out_specs=pl.BlockSpec((tko, tn), lambda i, j, mm: (i, j)), out_shape=_sds((k, n), F32),
                  scratch_shapes=[pltpu.VMEM((tko, tn), F32)],
                  compiler_params=_cparams(("parallel", "parallel", "arbitrary")))(a, b)


def _rms_fwd(h, g, name):
    def fn(_, hv, gv):
        r = lax.rsqrt(jnp.mean(hv * hv, axis=1, keepdims=True) + NORM_EPS)
        return hv * r * gv
    return _rowwise(fn, [h], [(h.shape[1], BF16)], bcast=[g], name=name)[0]


def _rms_bwd(h, dxn, dres, g, name):
    d = h.shape[1]

    def fn(_, hv, dv, rv, gv):
        r = lax.rsqrt(jnp.mean(hv * hv, axis=1, keepdims=True) + NORM_EPS)
        xh = hv * r
        dxh = dv * gv
        dh = r * (dxh - xh * jnp.mean(dxh * xh, axis=1, keepdims=True))
        return rv + dh, jnp.sum(dv * xh, axis=0, keepdims=True)
    return _rowwise(fn, [h, dxn, dres], [(d, F32)], bcast=[g], reds=[(1, d)], name=name)


def _swiglu_fwd(gu, name):
    f = gu.shape[1] // 2

    def fn(_, v):
        v = v.astype(F32)
        return _silu(v[:, :f]) * v[:, f:]
    return _rowwise(fn, [gu], [(f, BF16)], name=name)[0]


def _swiglu_bwd(gu, dact, name):
    f = gu.shape[1] // 2

    def fn(_, v, dv):
        v = v.astype(F32)
        dv = dv.astype(F32)
        g, u = v[:, :f], v[:, f:]
        return jnp.concatenate([dv * u * _dsilu(g), dv * _silu(g)], axis=1)
    return _rowwise(fn, [gu, dact], [(2 * f, BF16)], name=name)[0]


def _merge_fwd(mg, ba, bb, bc, name):
    d = ba.shape[1]

    def fn(_, m, a, b, c):
        g = _sigmoid(m.astype(F32))
        return g[:, :d] * a.astype(F32) + g[:, d:2 * d] * b.astype(F32) + g[:, 2 * d:] * c.astype(F32)
    return _rowwise(fn, [mg, ba, bb, bc], [(d, BF16)], name=name)[0]


def _merge_bwd(mg, ba, bb, bc, dmix, name):
    d = ba.shape[1]

    def fn(_, m, a, b, c, dm):
        g = _sigmoid(m.astype(F32))
        dm = dm.astype(F32)
        br = (a.astype(F32), b.astype(F32), c.astype(F32))
        douts, dgs = [], []
        for j in range(3):
            gj = g[:, j * d:(j + 1) * d]
            douts.append(dm * gj)
            dgs.append(dm * br[j] * gj * (1.0 - gj))
        return (*douts, jnp.concatenate(dgs, axis=1))
    return _rowwise(fn, [mg, ba, bb, bc, dmix], [(d, BF16)] * 3 + [(3 * d, BF16)], name=name)


def _loss_head(h, tgt, g, seq_len, name):
    d = h.shape[1]

    def fn(pos, hv, tv, gv):
        r = lax.rsqrt(jnp.mean(hv * hv, axis=1, keepdims=True) + NORM_EPS)
        xh = hv * r
        real = (pos >= N_META) & (pos < N_META + SEQ)
        e = jnp.where(real, xh * gv - tv, 0.0)
        part = jnp.sum(jnp.sum(e * e, axis=1, keepdims=True), axis=0, keepdims=True) * (0.5 / d)
        dy = e * (1.0 / d)
        dxh = dy * gv
        dh = r * (dxh - xh * jnp.mean(dxh * xh, axis=1, keepdims=True))
        return dh, jnp.broadcast_to(part, (1, 128)), jnp.sum(dy * xh, axis=0, keepdims=True)
    return _rowwise(fn, [h, tgt], [(d, F32)], bcast=[g], reds=[(1, 128), (1, d)], name=name, period=seq_len)


def _adamw(w, g, m, v, name):
    c1 = 1.0 - ADAM_B1 ** ADAM_STEP
    c2 = 1.0 - ADAM_B2 ** ADAM_STEP
    wd = w.shape[1]

    def fn(_, wv, gv, mv, vv):
        mn = ADAM_B1 * mv + (1.0 - ADAM_B1) * gv
        vn = ADAM_B2 * vv + (1.0 - ADAM_B2) * (gv * gv)
        delta = -ADAM_LR * ((mn / c1) / (jnp.sqrt(vn / c2) + ADAM_EPS) + ADAM_WD * wv)
        return delta, mn, vn
    return _rowwise(fn, [w, g, m, v], [(wd, F32)] * 3, name=name, tm=_pick(w.shape[0], (256, 128, 64, 32, 16, 8)))


def _sum_rows(parts, out_dtype, name):
    def fn(_, *vs):
        acc = vs[0].astype(F32)
        for v in vs[1:]:
            acc = acc + v.astype(F32)
        return acc
    return _rowwise(fn, list(parts), [(parts[0].shape[1], out_dtype)], name=name,
                    tm=_pick(parts[0].shape[0], (256, 128, 64, 32, 16, 8)))[0]


def _cumsum_seq(x, reverse, name):
    b, l, w = x.shape
    q = 128
    nc = l // q

    def body(x_ref, o_ref):
        row = lax.broadcasted_iota(jnp.int32, (q, q), 0)
        col = lax.broadcasted_iota(jnp.int32, (q, q), 1)
        tri = ((row <= col) if reverse else (row >= col)).astype(F32)
        rsel = lax.broadcasted_iota(jnp.int32, (q, w), 0) == (0 if reverse else q - 1)

        def step(i, carry):
            j = (nc - 1 - i) if reverse else i
            start = pl.multiple_of(j * q, q)
            cs = _dot(tri, x_ref[pl.ds(start, q), :], hi=True) + carry
            o_ref[pl.ds(start, q), :] = cs
            return jnp.sum(jnp.where(rsel, cs, 0.0), axis=0, keepdims=True)

        lax.fori_loop(0, nc, step, jnp.zeros((1, w), F32))

    return _pcall(body, name=name, grid=(b,), in_specs=[pl.BlockSpec((None, l, w), lambda i: (i, 0, 0))],
                  out_specs=pl.BlockSpec((None, l, w), lambda i: (i, 0, 0)), out_shape=_sds(x.shape, F32),
                  compiler_params=_cparams(("parallel",)))(x)


_HALO = 16


def _conv_tiles(l, c):
    return _pick(l, (384, 256, 128)), _pick(c, (512, 256, 128))


def _conv_fwd(x, w, bias, out_dtype, name):
    b, l, c = x.shape
    tt, cw = _conv_tiles(l, c)

    def body(x_ref, h_ref, w_ref, b_ref, o_ref):
        t = pl.program_id(2)
        halo = jnp.where(t == 0, 0.0, h_ref[...].astype(F32))
        xe = jnp.concatenate([halo, x_ref[...].astype(F32)], axis=0)
        wv = w_ref[...]
        acc = b_ref[...] + wv[CONV_K - 1:CONV_K, :] * xe[_HALO:]
        for j in range(CONV_K - 1):
            acc = acc + wv[j:j + 1, :] * pltpu.roll(xe, CONV_K - 1 - j, 0)[_HALO:]
        o_ref[...] = acc.astype(o_ref.dtype)

    return _pcall(body, name=name, grid=(b, c // cw, l // tt),
                  in_specs=[pl.BlockSpec((None, tt, cw), lambda i, j, t: (i, t, j)),
                            pl.BlockSpec((None, _HALO, cw), lambda i, j, t: (i, jnp.maximum(t * (tt // _HALO) - 1, 0), j)),
                            pl.BlockSpec((CONV_K, cw), lambda i, j, t: (0, j)),
                            pl.BlockSpec((1, cw), lambda i, j, t: (0, j))],
                  out_specs=pl.BlockSpec((None, tt, cw), lambda i, j, t: (i, t, j)), out_shape=_sds(x.shape, out_dtype),
                  compiler_params=_cparams(("parallel", "parallel", "parallel")))(x, x, w, bias)


def _conv_bwd(x, dy, w, name):
    b, l, c = x.shape
    tt, cw = _conv_tiles(l, c)
    nt = l // tt

    def body(x_ref, xh_ref, d_ref, dh_ref, w_ref, dx_ref, dw_ref):
        i, t = pl.program_id(1), pl.program_id(2)
        halo = jnp.where(t == 0, 0.0, xh_ref[...].astype(F32))
        xe = jnp.concatenate([halo, x_ref[...].astype(F32)], axis=0)
        dv = d_ref[...].astype(F32)
        nxt = jnp.where(t == nt - 1, 0.0, dh_ref[...].astype(F32))
        de = jnp.concatenate([dv, nxt], axis=0)
        wv = w_ref[...]
        dx = wv[CONV_K - 1:CONV_K, :] * dv
        rowid = lax.broadcasted_iota(jnp.int32, (8, 1), 0)
        part = jnp.where(rowid == CONV_K, jnp.sum(dv, axis=0, keepdims=True), 0.0)
        part = part + jnp.where(rowid == CONV_K - 1, jnp.sum(dv * xe[_HALO:], axis=0, keepdims=True), 0.0)
        for j in range(CONV_K - 1):
            s = CONV_K - 1 - j
            dx = dx + wv[j:j + 1, :] * pltpu.roll(de, tt + _HALO - s, 0)[:tt]
            xs = pltpu.roll(xe, s, 0)[_HALO:]
            part = part + jnp.where(rowid == j, jnp.sum(dv * xs, axis=0, keepdims=True), 0.0)
        dx_ref[...] = dx.astype(dx_ref.dtype)

        @pl.when((i == 0) & (t == 0))
        def _():
            dw_ref[...] = jnp.zeros_like(dw_ref)
        dw_ref[...] += part

    return _pcall(body, name=name, grid=(c // cw, b, nt),
                  in_specs=[pl.BlockSpec((None, tt, cw), lambda j, i, t: (i, t, j)),
                            pl.BlockSpec((None, _HALO, cw), lambda j, i, t: (i, jnp.maximum(t * (tt // _HALO) - 1, 0), j)),
                            pl.BlockSpec((None, tt, cw), lambda j, i, t: (i, t, j)),
                            pl.BlockSpec((None, _HALO, cw),
                                         lambda j, i, t: (i, jnp.minimum((t + 1) * (tt // _HALO), l // _HALO - 1), j)),
                            pl.BlockSpec((CONV_K, cw), lambda j, i, t: (0, j))],
                  out_specs=[pl.BlockSpec((None, tt, cw), lambda j, i, t: (i, t, j)),
                             pl.BlockSpec((8, cw), lambda j, i, t: (0, j))],
                  out_shape=[_sds(x.shape, BF16), _sds((8, c), F32)],
                  compiler_params=_cparams(("parallel", "arbitrary", "arbitrary")))(x, x, dy, dy, w)


def _linear_scan(a, u, reverse, name):
    b, l, c = a.shape
    tt = 128
    cw = _pick(c, (512, 256, 128))
    nt = l // tt

    def body(a_ref, u_ref, h_ref, carry):
        t = pl.program_id(2)

        @pl.when(t == 0)
        def _():
            carry[...] = jnp.zeros_like(carry)

        av, uv = a_ref[...], u_ref[...]
        row = lax.broadcasted_iota(jnp.int32, (tt, cw), 0)
        k = 1
        while k < tt:
            if reverse:
                keep = row < tt - k
                a_sh = jnp.where(keep, pltpu.roll(av, tt - k, 0), 1.0)
                u_sh = jnp.where(keep, pltpu.roll(uv, tt - k, 0), 0.0)
            else:
                keep = row >= k
                a_sh = jnp.where(keep, pltpu.roll(av, k, 0), 1.0)
                u_sh = jnp.where(keep, pltpu.roll(uv, k, 0), 0.0)
            uv = uv + av * u_sh
            av = av * a_sh
            k *= 2
        hv = uv + av * carry[0:1, :]
        h_ref[...] = hv
        edge = jnp.sum(jnp.where(row == (0 if reverse else tt - 1), hv, 0.0), axis=0, keepdims=True)
        carry[...] = jnp.broadcast_to(edge, carry.shape)

    tmap = (lambda i, j, t: (i, nt - 1 - t, j)) if reverse else (lambda i, j, t: (i, t, j))
    spec = pl.BlockSpec((None, tt, cw), tmap)
    return _pcall(body, name=name, grid=(b, c // cw, nt), in_specs=[spec, spec], out_specs=spec,
                  out_shape=_sds(a.shape, F32), scratch_shapes=[pltpu.VMEM((8, cw), F32)],
                  compiler_params=_cparams(("parallel", "parallel", "arbitrary")))(a, u)


ATTN_W = 128
_AUG_C = 0
_AUG_ONE = 3
_AUG_LSE = 6


def _attn_blk(l):
    return _pick(l, (384, 256, 128))


def _split3(x):
    x1 = x.astype(BF16).astype(F32)
    x2 = (x - x1).astype(BF16).astype(F32)
    x3 = (x - x1 - x2).astype(BF16).astype(F32)
    return x1, x2, x3


def _aug_lanes(lane, base, vals):
    out = 0.0
    for k, v in enumerate(vals):
        out = jnp.where(lane == base + k, v, out)
    return out


def _pair_specs(blk, nb):
    at = lambda ww: pl.BlockSpec((None, 2, blk, ww), lambda bi, p, i: (bi, p, i, 0))
    whole = pl.BlockSpec((None, 2, nb, blk, ATTN_W), lambda bi, p, i: (bi, p, 0, 0, 0))
    rows = pl.BlockSpec((None, blk, ATTN_W), lambda bi, p, i: (bi, i, p))
    return at, whole, rows


def _attn_prep(pm3, cum, col0, name):
    b, l, _ = pm3.shape
    dh, nh = ATTN_HEAD_DIM, ATTN_HEADS
    blk = _attn_blk(l)
    scale = dh ** -0.5

    def body(q_ref, k_ref, v_ref, c_ref, qa_ref, ka_ref, va_ref):
        pair = pl.program_id(1)
        lane = lax.broadcasted_iota(jnp.int32, (1, ATTN_W), 1)
        head = lane < dh
        cv = c_ref[...]
        qf, kf, vf = (r[...].astype(F32) for r in (q_ref, k_ref, v_ref))
        for e in range(2):
            c1, c2, c3 = _split3(jnp.sum(jnp.where(lane == 2 * pair + e, cv, 0.0), axis=1, keepdims=True))
            qe, ke, ve = (pltpu.roll(t, dh, 1) for t in (qf, kf, vf)) if e else (qf, kf, vf)
            qa_ref[e] = jnp.where(head, qe * scale, _aug_lanes(lane, dh, (c1, c2, c3, 1.0, 1.0, 1.0))).astype(BF16)
            ka_ref[e] = jnp.where(head, ke, _aug_lanes(lane, dh, (1.0, 1.0, 1.0, -c1, -c2, -c3, 1.0, 1.0, 1.0))).astype(BF16)
            va_ref[e] = jnp.where(head, ve, _aug_lanes(lane, dh, (1.0, 1.0, 1.0))).astype(BF16)

    at, _, _ = _pair_specs(blk, l // blk)
    cols = lambda c0: pl.BlockSpec((None, blk, ATTN_W), lambda bi, p, i, c0=c0: (bi, i, c0 // ATTN_W + p))
    return _pcall(body, name=name, grid=(b, nh // 2, l // blk),
                  in_specs=[cols(col0[0]), cols(col0[1]), cols(col0[2]),
                            pl.BlockSpec((None, blk, SMALL_W), lambda bi, p, i: (bi, i, 0))],
                  out_specs=[at(ATTN_W)] * 3, out_shape=[_sds((b, nh, l, ATTN_W), BF16)] * 3,
                  compiler_params=_cparams(("parallel", "parallel", "parallel")))(pm3, pm3, pm3, cum)


def _attn_prep_bwd(dy3, y3, qa, lse, name):
    b, nh, l, _ = qa.shape
    dh = ATTN_HEAD_DIM
    blk = _attn_blk(l)

    def body(dy_ref, y_ref, qa_ref, lse_ref, qa2_ref, doa_ref):
        lane = lax.broadcasted_iota(jnp.int32, (1, ATTN_W), 1)
        dyf = dy_ref[...].astype(F32)
        prod = dyf * y_ref[...].astype(F32)
        for e in range(2):
            mine = (lane >= dh) if e else (lane < dh)
            d1, d2, d3 = _split3(jnp.sum(jnp.where(mine, prod, 0.0), axis=1, keepdims=True))
            l1, l2, l3 = _split3(lse_ref[e])
            dye = pltpu.roll(dyf, dh, 1) if e else dyf
            doa_ref[e] = jnp.where(lane < dh, dye, _aug_lanes(lane, dh, (-d1, -d2, -d3))).astype(BF16)
            on_lse = (lane >= dh + _AUG_LSE) & (lane < dh + _AUG_LSE + 3)
            qa2_ref[e] = jnp.where(on_lse, _aug_lanes(lane, dh + _AUG_LSE, (-l1, -l2, -l3)),
                                   qa_ref[e].astype(F32)).astype(BF16)

    at, _, rows = _pair_specs(blk, l // blk)
    return _pcall(body, name=name, grid=(b, nh // 2, l // blk), in_specs=[rows, rows, at(ATTN_W), at(1)],
                  out_specs=[at(ATTN_W)] * 2, out_shape=[_sds(qa.shape, BF16)] * 2,
                  compiler_params=_cparams(("parallel", "parallel", "parallel")))(dy3, y3, qa, lse)


def _flash_fwd(qa, ka, va, d_model, name):
    b, h, l, w = qa.shape
    dh = ATTN_HEAD_DIM
    blk = _attn_blk(l)
    nb = l // blk
    kr, vr = ka.reshape(b, h, nb, blk, w), va.reshape(b, h, nb, blk, w)

    def body(q_ref, k_ref, v_ref, o_ref, lse_ref):
        i = pl.program_id(2)
        row = lax.broadcasted_iota(jnp.int32, (blk, blk), 0)
        col = lax.broadcasted_iota(jnp.int32, (blk, blk), 1)

        def step(j, carry, masked):
            out = []
            for e in range(2):
                m, acc = carry[2 * e], carry[2 * e + 1]
                s = _dot_nt(q_ref[e], k_ref[e, j])
                if masked:
                    s = jnp.where(col <= row, s, NEG)
                mn = jnp.maximum(m, jnp.max(s, axis=1, keepdims=True))
                out += [mn, jnp.exp(m - mn) * acc + _dot(jnp.exp(s - mn).astype(BF16), v_ref[e, j])]
            return tuple(out)

        init = (jnp.full((blk, 1), NEG, F32), jnp.zeros((blk, w), F32)) * 2
        m0, a0, m1, a1 = step(i, lax.fori_loop(0, i, lambda j, c: step(j, c, False), init), True)
        l0, l1 = a0[:, dh:dh + 1], a1[:, dh:dh + 1]
        lane = lax.broadcasted_iota(jnp.int32, (1, w), 1)
        o_ref[...] = jnp.where(lane < dh, a0 / l0, pltpu.roll(a1 / l1, dh, 1)).astype(o_ref.dtype)
        lse_ref[0] = m0 + jnp.log(l0)
        lse_ref[1] = m1 + jnp.log(l1)

    at, whole, rows = _pair_specs(blk, nb)
    return _pcall(body, name=name, grid=(b, h // 2, nb), in_specs=[at(w), whole, whole],
                  out_specs=[rows, at(1)], out_shape=[_sds((b, l, d_model), BF16), _sds((b, h, l, 1), F32)],
                  compiler_params=_cparams(("parallel", "parallel", "parallel")))(qa, kr, vr)


def _flash_bwd_kv(qa, ka, va, doa, d_model, name):
    b, h, l, w = qa.shape
    dh = ATTN_HEAD_DIM
    blk = _attn_blk(l)
    nb = l // blk
    r5 = lambda t: t.reshape(b, h, nb, blk, w)

    def body(k_ref, v_ref, q_ref, do_ref, dk_ref, dv_ref, dc_ref):
        j = pl.program_id(2)
        row = lax.broadcasted_iota(jnp.int32, (blk, blk), 0)
        col = lax.broadcasted_iota(jnp.int32, (blk, blk), 1)

        def contrib(i, masked, carry):
            out = []
            for e in range(2):
                qv, dov = q_ref[e, i], do_ref[e, i]
                p = jnp.exp(_dot_nt(qv, k_ref[e]))
                if masked:
                    p = jnp.where(col <= row, p, 0.0)
                ds = p * _dot_nt(dov, v_ref[e])
                out += [carry[2 * e] + _dot_tn(ds.astype(BF16), qv), carry[2 * e + 1] + _dot_tn(p.astype(BF16), dov)]
            return tuple(out)

        zero = (jnp.zeros((blk, w), F32),) * 4
        dk0, dv0, dk1, dv1 = lax.fori_loop(j + 1, nb, lambda i, c: contrib(i, False, c), contrib(j, True, zero))
        lane = lax.broadcasted_iota(jnp.int32, (1, w), 1)
        dk_ref[...] = jnp.where(lane < dh, dk0, pltpu.roll(dk1, dh, 1)).astype(dk_ref.dtype)
        dv_ref[...] = jnp.where(lane < dh, dv0, pltpu.roll(dv1, dh, 1)).astype(dv_ref.dtype)
        for e, dk in enumerate((dk0, dk1)):
            dc_ref[e] = jnp.sum(jnp.where(lane == dh + _AUG_ONE, dk, 0.0), axis=1, keepdims=True)

    at, whole, rows = _pair_specs(blk, nb)
    return _pcall(body, name=name, grid=(b, h // 2, nb), in_specs=[at(w), at(w), whole, whole],
                  out_specs=[rows, rows, at(1)],
                  out_shape=[_sds((b, l, d_model), BF16), _sds((b, l, d_model), BF16), _sds((b, h, l, 1), F32)],
                  compiler_params=_cparams(("parallel", "parallel", "parallel")))(ka, va, r5(qa), r5(doa))


def _flash_bwd_q(qa, ka, va, doa, d_model, name):
    b, h, l, w = qa.shape
    dh = ATTN_HEAD_DIM
    blk = _attn_blk(l)
    nb = l // blk
    scale = dh ** -0.5
    kr, vr = ka.reshape(b, h, nb, blk, w), va.reshape(b, h, nb, blk, w)

    def body(q_ref, do_ref, k_ref, v_ref, dq_ref, dc_ref):
        i = pl.program_id(2)
        row = lax.broadcasted_iota(jnp.int32, (blk, blk), 0)
        col = lax.broadcasted_iota(jnp.int32, (blk, blk), 1)

        def contrib(j, masked, carry):
            out = []
            for e in range(2):
                p = jnp.exp(_dot_nt(q_ref[e], k_ref[e, j]))
                if masked:
                    p = jnp.where(col <= row, p, 0.0)
                ds = p * _dot_nt(do_ref[e], v_ref[e, j])
                out.append(carry[e] + _dot(ds.astype(BF16), k_ref[e, j]))
            return tuple(out)

        zero = (jnp.zeros((blk, w), F32),) * 2
        dq0, dq1 = contrib(i, True, lax.fori_loop(0, i, lambda j, c: contrib(j, False, c), zero))
        lane = lax.broadcasted_iota(jnp.int32, (1, w), 1)
        dq_ref[...] = (jnp.where(lane < dh, dq0, pltpu.roll(dq1, dh, 1)) * scale).astype(dq_ref.dtype)
        for e, dq in enumerate((dq0, dq1)):
            dc_ref[e] = jnp.sum(jnp.where(lane == dh + _AUG_C, dq, 0.0), axis=1, keepdims=True)

    at, whole, rows = _pair_specs(blk, nb)
    return _pcall(body, name=name, grid=(b, h // 2, nb), in_specs=[at(w), at(w), whole, whole],
                  out_specs=[rows, at(1)], out_shape=[_sds((b, l, d_model), BF16), _sds((b, h, l, 1), F32)],
                  compiler_params=_cparams(("parallel", "parallel", "parallel")))(qa, doa, kr, vr)


def _ssd_dims(d_ssd):
    heads = d_ssd // SSD_HEAD_DIM
    return heads, heads // SSD_GROUPS, d_ssd // SSD_GROUPS


def _ssd_specs(l, ds, seq_map):
    q = SSD_CHUNK
    gn = SSD_GROUPS * SSD_STATE
    row3 = lambda w, cb: pl.BlockSpec((None, q, w), lambda i, c, cb=cb: (i, seq_map(c), cb))
    return dict(
        xs=row3(ds, 0), bm=row3(gn, ds // gn), cm=row3(gn, ds // gn + 1), z=row3(ds, 0), dt=row3(SMALL_W, 0),
        da=row3(SMALL_W, 0), dat=pl.BlockSpec((None, SMALL_W, q), lambda i, c: (i, 0, seq_map(c))),
        e=pl.BlockSpec((SMALL_W, ds), lambda i, c: (0, 0)), et=pl.BlockSpec((ds, SMALL_W), lambda i, c: (0, 0)),
        vec=pl.BlockSpec((1, ds), lambda i, c: (0, 0)), vec128=pl.BlockSpec((1, SMALL_W), lambda i, c: (0, 0)),
        hin=pl.BlockSpec((None, None, SSD_STATE, ds), lambda i, c: (i, seq_map(c), 0, 0)))


def _ssd_common(da, dat, dt, e_mat, xs):
    q = SSD_CHUNK
    row = lax.broadcasted_iota(jnp.int32, (q, q), 0)
    col = lax.broadcasted_iota(jnp.int32, (q, q), 1)
    lower = row >= col
    cs = _dot(lower.astype(F32), da, hi=True)
    cst = _dot(dat, (row <= col).astype(F32), hi=True)
    dtx = _dot(dt, e_mat, hi=True)
    csx = _dot(cs, e_mat, hi=True)
    rowx = lax.broadcasted_iota(jnp.int32, csx.shape, 0)
    totx = jnp.sum(jnp.where(rowx == q - 1, csx, 0.0), axis=0, keepdims=True)
    xf = xs.astype(F32)
    return lower, cs, cst, dtx, csx, totx, xf, xf * dtx


def _ssd_fwd(xbc, z, dt, da, dat, e_mat, dx, nw, name):
    b, l, _ = xbc.shape
    ds = z.shape[2]
    heads, hpg, gw = _ssd_dims(ds)
    q, n = SSD_CHUNK, SSD_STATE
    nc = l // q
    hcol0 = ATTN_HEADS

    def body(xs_ref, bm_ref, cm_ref, z_ref, dt_ref, da_ref, dat_ref, e_ref, dx_ref, nw_ref, y_ref, yraw_ref, hin_ref,
             hst, ydiag):
        c = pl.program_id(1)

        @pl.when(c == 0)
        def _():
            hst[...] = jnp.zeros_like(hst)

        hin = hst[...]
        hin_ref[...] = hin
        lower, cs, cst, dtx, csx, totx, xf, xdt = _ssd_common(da_ref[...], dat_ref[...], dt_ref[...], e_ref[...],
                                                               xs_ref[...])
        bm, cm = bm_ref[...], cm_ref[...]
        dec_end = jnp.exp(totx - csx)
        for g in range(SSD_GROUPS):
            gs = slice(g * gw, (g + 1) * gw)
            bg, cg = bm[:, g * n:(g + 1) * n], cm[:, g * n:(g + 1) * n]
            cb = _dot_nt(cg, bg)
            for e in range(hpg):
                hh = g * hpg + e
                cc = hcol0 + hh
                lm = jnp.exp(jnp.where(lower, cs[:, cc:cc + 1] - cst[cc:cc + 1, :], NEG))
                hs = slice(hh * SSD_HEAD_DIM, (hh + 1) * SSD_HEAD_DIM)
                ydiag[:, hs] = _dot((cb * lm).astype(BF16), xdt[:, hs].astype(BF16))
            sg = _dot_tn(bg, (xdt[:, gs] * dec_end[:, gs]).astype(BF16))
            hst[:, gs] = jnp.exp(totx[:, gs]) * hin[:, gs] + sg
            ydiag[:, gs] += _dot(cg, hin[:, gs].astype(BF16)) * jnp.exp(csx[:, gs])
        yraw = ydiag[...] + dx_ref[...] * xf
        yraw_ref[...] = yraw.astype(yraw_ref.dtype)
        yg = yraw * _silu(z_ref[...].astype(F32))
        nwv = nw_ref[...]
        for g in range(SSD_GROUPS):
            gs = slice(g * gw, (g + 1) * gw)
            r = lax.rsqrt(jnp.mean(yg[:, gs] * yg[:, gs], axis=1, keepdims=True) + NORM_EPS)
            y_ref[:, gs] = (yg[:, gs] * r * nwv[:, gs]).astype(y_ref.dtype)

    sp = _ssd_specs(l, ds, lambda c: c)
    return _pcall(body, name=name, grid=(b, nc),
                  in_specs=[sp['xs'], sp['bm'], sp['cm'], sp['z'], sp['dt'], sp['da'], sp['dat'], sp['e'], sp['vec'],
                            sp['vec']],
                  out_specs=[sp['z'], sp['z'], sp['hin']],
                  out_shape=[_sds((b, l, ds), BF16), _sds((b, l, ds), BF16), _sds((b, nc, n, ds), F32)],
                  scratch_shapes=[pltpu.VMEM((n, ds), F32), pltpu.VMEM((q, ds), F32)],
                  compiler_params=_cparams(("parallel", "arbitrary")))(xbc, xbc, xbc, z, dt, da, dat, e_mat, dx, nw)


def _ssd_bwd(xbc, z, dt, da, dat, e_mat, et_mat, dx, nw, a128, yraw, hin, dy, name):
    b, l, dxw = xbc.shape
    ds = z.shape[2]
    heads, hpg, gw = _ssd_dims(ds)
    q, n = SSD_CHUNK, SSD_STATE
    gn = SSD_GROUPS * n
    nc = l // q
    hcol0 = ATTN_HEADS

    def body(xs_ref, bm_ref, cm_ref, z_ref, dt_ref, da_ref, dat_ref, e_ref, et_ref, dx_ref, nw_ref, a_ref, yraw_ref,
             hin_ref, dy_ref, dxs_ref, dbm_ref, dcm_ref, dz_ref, ddt_ref, dd_ref, dnw_ref, dap_ref, dhs, dxdt, dcsx,
             dtotx):
        i, c = pl.program_id(0), pl.program_id(1)

        @pl.when(c == 0)
        def _():
            dhs[...] = jnp.zeros_like(dhs)

        @pl.when((i == 0) & (c == 0))
        def _():
            dd_ref[...] = jnp.zeros_like(dd_ref)
            dnw_ref[...] = jnp.zeros_like(dnw_ref)
            dap_ref[...] = jnp.zeros_like(dap_ref)

        dtv = dt_ref[...]
        lower, cs, cst, dtx, csx, totx, xf, xdt = _ssd_common(da_ref[...], dat_ref[...], dtv, e_ref[...], xs_ref[...])
        upper = jnp.logical_not(lower) | (lax.broadcasted_iota(jnp.int32, (q, q), 0)
                                          == lax.broadcasted_iota(jnp.int32, (q, q), 1))
        bm, cm = bm_ref[...], cm_ref[...]
        ecs, dec_end, etot = jnp.exp(csx), jnp.exp(totx - csx), jnp.exp(totx)
        yraw = yraw_ref[...].astype(F32)
        zv = z_ref[...].astype(F32)
        sz = _silu(zv)
        yg = yraw * sz
        dyn_ = dy_ref[...].astype(F32)
        nwv = nw_ref[...]
        dygs, dnws = [], []
        for g in range(SSD_GROUPS):
            gs = slice(g * gw, (g + 1) * gw)
            r = lax.rsqrt(jnp.mean(yg[:, gs] * yg[:, gs], axis=1, keepdims=True) + NORM_EPS)
            yn = yg[:, gs] * r
            dn = dyn_[:, gs] * nwv[:, gs]
            dnws.append(jnp.sum(dyn_[:, gs] * yn, axis=0, keepdims=True))
            dygs.append(r * (dn - yn * jnp.mean(dn * yn, axis=1, keepdims=True)))
        dyg = jnp.concatenate(dygs, axis=1)
        dnw_ref[...] += jnp.concatenate(dnws, axis=1)
        dz_ref[...] = (dyg * yraw * _dsilu(zv)).astype(dz_ref.dtype)
        dyv = dyg * sz
        dd_ref[...] += jnp.sum(dyv * xf, axis=0, keepdims=True)
        hin, dh = hin_ref[...], dhs[...]
        lane128 = lax.broadcasted_iota(jnp.int32, (1, SMALL_W), 1)
        dcs = jnp.zeros((q, SMALL_W), F32)
        for g in range(SSD_GROUPS):
            gs = slice(g * gw, (g + 1) * gw)
            bg, cg = bm[:, g * n:(g + 1) * n], cm[:, g * n:(g + 1) * n]
            hg, dhg = hin[:, gs], dh[:, gs]
            hgb, dsb = hg.astype(BF16), dhg.astype(BF16)
            yoff = _dot(cg, hgb) * ecs[:, gs]
            dch = (dyv[:, gs] * ecs[:, gs]).astype(BF16)
            dcg = _dot_nt(dch, hgb)
            dhs[:, gs] = _dot_tn(cg, dch) + etot[:, gs] * dhg
            zg = xdt[:, gs] * dec_end[:, gs]
            dzz = _dot(bg, dsb)
            dbg = _dot_nt(zg.astype(BF16), dsb)
            dxdt_g = dzz * dec_end[:, gs]
            w_end = dzz * zg
            dtotx[:, gs] = jnp.sum(dhg * hg, axis=0, keepdims=True) * etot[:, gs] + jnp.sum(w_end, axis=0, keepdims=True)
            dcsx[:, gs] = dyv[:, gs] * yoff - w_end
            cb, cbt = _dot_nt(cg, bg), _dot_nt(bg, cg)
            dgm = jnp.zeros((q, q), F32)
            for e in range(hpg):
                hh = g * hpg + e
                cc = hcol0 + hh
                ccol, crow = cs[:, cc:cc + 1], cst[cc:cc + 1, :]
                lm = jnp.exp(jnp.where(lower, ccol - crow, NEG))
                lmt = jnp.exp(jnp.where(upper, crow - ccol, NEG))
                mm, mt = cb * lm, cbt * lmt
                hs = slice(hh * SSD_HEAD_DIM, (hh + 1) * SSD_HEAD_DIM)
                dye, xe = dyv[:, hs].astype(BF16), xdt[:, hs].astype(BF16)
                dm, dmt = _dot_nt(dye, xe), _dot_nt(xe, dye)
                dxdt[:, hs] = dxdt_g[:, e * SSD_HEAD_DIM:(e + 1) * SSD_HEAD_DIM] + _dot(mt.astype(BF16), dye)
                dgm = dgm + dm * lm
                rs = jnp.sum(dm * mm, axis=1, keepdims=True) - jnp.sum(dmt * mt, axis=1, keepdims=True)
                dcs = dcs + rs * (lane128 == cc).astype(F32)
            dgb = dgm.astype(BF16)
            dcm_ref[:, g * n:(g + 1) * n] = (dcg + _dot(dgb, bg)).astype(dcm_ref.dtype)
            dbm_ref[:, g * n:(g + 1) * n] = (dbg + _dot_tn(dgb, cg)).astype(dbm_ref.dtype)
        dxd = dxdt[...]
        dxs_ref[...] = (dx_ref[...] * dyv + dxd * dtx).astype(dxs_ref.dtype)
        et = et_ref[...]
        ddt = _dot(dxd * xf, et, hi=True)
        dtot128 = _dot(jnp.broadcast_to(dtotx[...], (8, ds)), et, hi=True)[0:1, :]
        row128 = lax.broadcasted_iota(jnp.int32, (q, SMALL_W), 0)
        dcs = dcs + _dot(dcsx[...], et, hi=True) + jnp.where(row128 == q - 1, dtot128, 0.0)
        dda = _dot(upper.astype(F32), dcs, hi=True)
        ddt_ref[...] = ddt + dda * a_ref[...]
        dap_ref[...] += jnp.sum(dda * dtv, axis=0, keepdims=True)

    rev = lambda c: nc - 1 - c
    sp = _ssd_specs(l, ds, rev)
    row3 = lambda w: pl.BlockSpec((None, q, w), lambda i, c: (i, rev(c), 0))
    acc = lambda w: pl.BlockSpec((1, w), lambda i, c: (0, 0))
    return _pcall(body, name=name, grid=(b, nc),
                  in_specs=[sp['xs'], sp['bm'], sp['cm'], sp['z'], sp['dt'], sp['da'], sp['dat'], sp['e'], sp['et'],
                            sp['vec'], sp['vec'], sp['vec128'], sp['z'], sp['hin'], sp['z']],
                  out_specs=[row3(ds), row3(gn), row3(gn), row3(ds), row3(SMALL_W), acc(ds), acc(ds), acc(SMALL_W)],
                  out_shape=[_sds((b, l, ds), BF16), _sds((b, l, gn), BF16), _sds((b, l, gn), BF16), _sds((b, l, ds), BF16),
                             _sds((b, l, SMALL_W), F32), _sds((1, ds), F32), _sds((1, ds), F32), _sds((1, SMALL_W), F32)],
                  scratch_shapes=[pltpu.VMEM((n, ds), F32), pltpu.VMEM((q, ds), F32), pltpu.VMEM((q, ds), F32),
                                  pltpu.VMEM((1, ds), F32)],
                  compiler_params=_cparams(("arbitrary", "arbitrary")))(
                      xbc, xbc, xbc, z, dt, da, dat, e_mat, et_mat, dx, nw, a128, yraw, hin, dy)


_GROUP_SIZE = {'c': 2, 'xy': 4, 'xyc': 8}
_LOCAL_SPLIT = 16


def _exchange(src, group, scatter, name, nsplit=1, copy_own=True):
    n = _GROUP_SIZE[group]
    rows, width = src.shape[-2:]
    assert src.ndim == (3 if scatter else 2)
    while rows % (8 * nsplit):
        nsplit //= 2
    crow = rows // nsplit
    nlocal = _LOCAL_SPLIT
    while rows % (8 * nlocal):
        nlocal //= 2
    lrow = rows // nlocal

    def body(src_ref, out_ref, send_sems, recv_sems, local_sems):
        x, y, c = lax.axis_index("x"), lax.axis_index("y"), lax.axis_index("c")
        if group == 'c':
            rank = c
            dev = lambda r: (x, y, r)
        elif group == 'xy':
            rank = 2 * x + y
            dev = lambda r: (r // 2, r % 2, c)
        else:
            rank = 4 * x + 2 * y + c
            dev = lambda r: (r // 4, (r // 2) % 2, r % 2)

        def mine_for(r, ck):
            piece = src_ref.at[r] if scatter else src_ref
            return piece.at[pl.ds(ck * crow, crow)]

        def copy(k, ck, pr, dst_rank):
            return pltpu.make_async_remote_copy(
                src_ref=mine_for(pr, ck), dst_ref=out_ref.at[dst_rank].at[pl.ds(ck * crow, crow)],
                send_sem=send_sems.at[k * nsplit + ck], recv_sem=recv_sems.at[k * nsplit + ck], device_id=dev(pr),
                device_id_type=pl.DeviceIdType.MESH)

        locals_ = []
        if copy_own:
            own = src_ref.at[rank] if scatter else src_ref
            for ck in range(nlocal):
                rs = pl.ds(ck * lrow, lrow)
                locals_.append(pltpu.make_async_copy(own.at[rs], out_ref.at[rank].at[rs], local_sems.at[ck]))
                locals_[-1].start()
        peers = [jnp.bitwise_xor(rank, k + 1) for k in range(n - 1)]
        sends = [copy(k, ck, pr, rank) for ck in range(nsplit) for k, pr in enumerate(peers)]
        for cp in sends:
            cp.start()
        for ck in range(nsplit):
            for k, pr in enumerate(peers):
                copy(k, ck, pr, pr).wait_recv()
        for cp in sends:
            cp.wait_send()
        for cp in locals_:
            cp.wait()

    return _pcall(body, name=name, in_specs=[pl.BlockSpec(memory_space=pl.ANY)],
                  out_specs=pl.BlockSpec(memory_space=pl.ANY), out_shape=_sds((n, rows, width), src.dtype),
                  scratch_shapes=[pltpu.SemaphoreType.DMA(((n - 1) * nsplit,)),
                                  pltpu.SemaphoreType.DMA(((n - 1) * nsplit,)),
                                  pltpu.SemaphoreType.DMA((nlocal,))])(src)


def _dims():
    d = D_MODEL
    h = ATTN_HEADS
    d_ssd = d
    d_xbc = d_ssd + 2 * SSD_GROUPS * SSD_STATE
    sizes = (d, d, d, h, d_ssd, d_xbc, d_ssd // SSD_HEAD_DIM, d, d, 3 * d)
    return d, h, d_ssd, d_xbc, sizes


def _w_in_split(w):
    d, h, d_ssd, d_xbc, sizes = _dims()
    off = np.concatenate([[0], np.cumsum(sizes)])
    seg = lambda i: w[..., off[i]:off[i + 1]]
    main = jnp.concatenate([seg(0), seg(1), seg(2), seg(4), seg(5), seg(7), seg(8), seg(9)], axis=-1)
    pad = jnp.zeros(w.shape[:-1] + (SMALL_W - sizes[3] - sizes[6],), w.dtype)
    small = jnp.concatenate([seg(3), seg(6), pad], axis=-1)
    return main, small


def _w_in_merge(main, small):
    d, h, d_ssd, d_xbc, sizes = _dims()
    order = (0, 1, 2, 4, 5, 7, 8, 9)
    moff = np.concatenate([[0], np.cumsum([sizes[i] for i in order])])
    pieces = {i: main[..., moff[j]:moff[j + 1]] for j, i in enumerate(order)}
    pieces[3] = small[..., :sizes[3]]
    pieces[6] = small[..., sizes[3]:sizes[3] + sizes[6]]
    return jnp.concatenate([pieces[i] for i in range(10)], axis=-1)


def _main_offsets():
    d, h, d_ssd, d_xbc, sizes = _dims()
    names = ('q', 'k', 'v', 'z', 'xbc', 'xr', 'gate', 'merge')
    widths = (d, d, d, d_ssd, d_xbc, d, d, 3 * d)
    off = np.concatenate([[0], np.cumsum(widths)])
    return {nm: (int(off[i]), int(off[i + 1])) for i, nm in enumerate(names)}


def _block_diag(w):
    nb, s, _ = w.shape
    eye = jnp.eye(nb, dtype=w.dtype)
    return (eye[:, None, :, None] * w[:, :, None, :]).reshape(nb * s, nb * s)


def _diag_blocks(wd, nb):
    s = wd.shape[0] // nb
    return jnp.stack([wd[i * s:(i + 1) * s, i * s:(i + 1) * s] for i in range(nb)])


def _vec128(*parts):
    v = jnp.concatenate([p.astype(F32) for p in parts])
    return jnp.pad(v, (0, SMALL_W - v.shape[0]))[None, :]


def _ffn_fwd(h, gnorm, wgu, wd, tag):
    xn = _rms_fwd(h, gnorm[None, :], f"{tag}_norm")
    gu = _mm_nn(xn, wgu, BF16, name=f"{tag}_gu")
    act = _swiglu_fwd(gu, f"{tag}_act")
    out = _mm_nn(act, wd, F32, res=h, alpha=0.5, name=f"{tag}_down")
    return out, (h, xn, gu, act)


def _ffn_bwd(dout, saved, gnorm, wgu_t, wd_t, tag):
    h, xn, gu, act = saved
    dact = _mm_nn(dout, wd_t, BF16, alpha=0.5, name=f"{tag}_dact")
    dwd = _mm_tn(act, dout, alpha=0.5, name=f"{tag}_dwd")
    dgu = _swiglu_bwd(gu, dact, f"{tag}_dgu")
    dwgu = _mm_tn(xn, dgu, name=f"{tag}_dwgu")
    dxn = _mm_nn(dgu, wgu_t, F32, name=f"{tag}_dxn")
    dh, dg = _rms_bwd(h, dxn, dout, gnorm[None, :], f"{tag}_dnorm")
    return dh, dg[0], dwgu, dwd


def _mixer_fwd(h, p, b, l):
    d, nh, d_ssd, d_xbc, sizes = _dims()
    t = b * l
    off = _main_offsets()
    xn = _rms_fwd(h, p['mix_norm'][None, :], "mix_norm")
    pm = _mm_nn(xn, p['w_main'], BF16, name="mix_in_main")
    ps = _mm_nn(xn, p['w_small'], F32, name="mix_in_small")
    col = lambda nm: pm[:, off[nm][0]:off[nm][1]]
    heads_ssd = d_ssd // SSD_HEAD_DIM
    a_neg = -jnp.exp(p['ssd_a_log'])
    fb = _vec128(p['fox_forget_bias'])
    dtb = _vec128(jnp.zeros((nh,), F32), p['ssd_dt_bias'])
    a128 = _vec128(jnp.zeros((nh,), F32), a_neg)

    def prep(_, v, fbv, dtbv, av):
        lane = lax.broadcasted_iota(jnp.int32, (1, SMALL_W), 1)
        logf = jnp.where(lane < nh, -_softplus(-(v + fbv)), 0.0)
        dtv = jnp.where((lane >= nh) & (lane < nh + heads_ssd), _softplus(v + dtbv), 0.0)
        return logf, dtv, dtv * av
    logf, dt, da = _rowwise(prep, [ps], [(SMALL_W, F32)] * 3, bcast=[fb, dtb, a128], name="mix_prep")

    cum = _cumsum_seq(logf.reshape(b, l, SMALL_W), False, "fox_cumsum")
    qa, ka, va = _attn_prep(pm.reshape(b, l, -1), cum, (off['q'][0], off['k'][0], off['v'][0]), "fox_prep")
    y_a3, lse = _flash_fwd(qa, ka, va, d, "fox_fwd")
    y_a = y_a3.reshape(t, d)

    xbc = col('xbc').reshape(b, l, d_xbc)
    pre_b = _conv_fwd(xbc, p['ssd_conv_w'], p['ssd_conv_b'][None, :], BF16, "ssd_conv")
    xbc_act = _rowwise(lambda _, v: _silu(v.astype(F32)), [pre_b.reshape(t, d_xbc)], [(d_xbc, BF16)],
                       name="ssd_conv_act")[0].reshape(b, l, d_xbc)
    z = col('z').reshape(b, l, d_ssd)
    dt3, da3 = dt.reshape(b, l, SMALL_W), da.reshape(b, l, SMALL_W)
    dat3 = da3.transpose(0, 2, 1)
    e_mat = _expand_matrix(nh, heads_ssd)
    dx = jnp.repeat(p['ssd_d'], SSD_HEAD_DIM)[None, :]
    nw = p['ssd_norm'][None, :]
    y_b3, yraw, hin = _ssd_fwd(xbc_act, z, dt3, da3, dat3, e_mat, dx, nw, "ssd_fwd")
    y_b = y_b3.reshape(t, d_ssd)

    xr = col('xr').reshape(b, l, d)
    xc = _conv_fwd(xr, p['lru_conv_w'], p['lru_conv_b'][None, :], F32, "lru_conv").reshape(t, d)
    pre_ri = _mm_nn(xc, p['lru_w_ri'], F32, name="lru_gates")
    lvec = (p['lru_b_a'][None, :], p['lru_b_x'][None, :], p['lru_lambda'][None, :])
    a_l, u_l = _rowwise(_lru_point_fwd, [pre_ri, xc], [(d, F32)] * 2, bcast=lvec, name="lru_point", period=l)
    hs = _linear_scan(a_l.reshape(b, l, d), u_l.reshape(b, l, d), False, "lru_scan").reshape(t, d)
    gate = col('gate')
    y_c = _rowwise(lambda _, hv, gv: hv * _gelu(gv.astype(F32)), [hs, gate], [(d, BF16)], name="lru_out")[0]

    ba = _mm_nn(y_a, p['w_branch_attn'], BF16, name="branch_attn")
    bb = _mm_nn(y_b, p['w_branch_ssd'], BF16, name="branch_ssd")
    bc = _mm_nn(y_c, p['w_branch_lru'], BF16, name="branch_lru")
    mg = col('merge')
    mixed = _merge_fwd(mg, ba, bb, bc, "merge")
    out = _mm_nn(mixed, p['w_out'], F32, res=h, name="mix_out")
    saved = dict(h=h, xn=xn, ps=ps, fb=fb, dtb=dtb, a128=a128, qa=qa, ka=ka, va=va, lse=lse,
                 xbc=xbc, pre_b=pre_b, xbc_act=xbc_act, z=z, dt3=dt3, da3=da3, dat3=dat3, e_mat=e_mat, dx=dx, nw=nw,
                 yraw=yraw, hin=hin, xr=xr, xc=xc, pre_ri=pre_ri, lvec=lvec, a_l=a_l, hs=hs, gate=gate, y_a=y_a, y_b=y_b,
                 y_c=y_c, ba=ba, bb=bb, bc=bc, mg=mg, mixed=mixed)
    return out, saved


def _expand_matrix(nh, heads_ssd):
    e = np.zeros((SMALL_W, heads_ssd * SSD_HEAD_DIM), np.float32)
    for hh in range(heads_ssd):
        e[nh + hh, hh * SSD_HEAD_DIM:(hh + 1) * SSD_HEAD_DIM] = 1.0
    return jnp.asarray(e)


def _lru_gates(pre, xc, bav, bxv, lamv, pos):
    d = xc.shape[1]
    r = _sigmoid(pre[:, :d] + bav)
    i = _sigmoid(pre[:, d:] + bxv)
    ls = -_softplus(-lamv)
    la = LRU_C * r * ls
    a = jnp.exp(la)
    mult = jnp.where(pos == 0, 1.0, jnp.sqrt(-_expm1(2.0 * la)))
    return r, i, ls, a, mult


def _lru_point_fwd(pos, pre, xc, bav, bxv, lamv):
    r, i, ls, a, mult = _lru_gates(pre, xc, bav, bxv, lamv, pos)
    return a, mult * (i * xc)


def _lru_point_bwd(pos, g, hprev, pre, xc, bav, bxv, lamv):
    r, i, ls, a, mult = _lru_gates(pre, xc, bav, bxv, lamv, pos)
    da = g * hprev
    di = g * mult * xc
    dxc = g * mult * i
    dmult = jnp.where(pos == 0, 0.0, g * i * xc)
    dla = da * a - dmult * (a * a) / mult
    dpre_r = dla * (LRU_C * ls) * r * (1.0 - r)
    dpre_i = di * i * (1.0 - i)
    dlam = jnp.sum(dla * (LRU_C * r), axis=0, keepdims=True) * _sigmoid(-lamv)
    return (jnp.concatenate([dpre_r, dpre_i], axis=1), dxc, dlam, jnp.sum(dpre_r, axis=0, keepdims=True),
            jnp.sum(dpre_i, axis=0, keepdims=True))


def _mixer_bwd(dout, s, p, b, l):
    d, nh, d_ssd, d_xbc, sizes = _dims()
    t = b * l
    heads_ssd = d_ssd // SSD_HEAD_DIM
    g = {}
    dmixed = _mm_nn(dout, p['w_out_t'], BF16, name="mix_out_dx")
    g['w_out'] = _mm_tn(s['mixed'], dout, name="mix_out_dw")
    dba, dbb, dbc, dmerge = _merge_bwd(s['mg'], s['ba'], s['bb'], s['bc'], dmixed, "merge_bwd")
    g['w_branch_attn'] = _mm_tn(s['y_a'], dba, name="branch_attn_dw")
    g['w_branch_ssd'] = _mm_tn(s['y_b'], dbb, name="branch_ssd_dw")
    g['w_branch_lru'] = _mm_tn(s['y_c'], dbc, name="branch_lru_dw")
    dy_a = _mm_nn(dba, p['w_branch_attn_t'], BF16, name="branch_attn_dx")
    dy_b = _mm_nn(dbb, p['w_branch_ssd_t'], BF16, name="branch_ssd_dx")
    dy_c = _mm_nn(dbc, p['w_branch_lru_t'], F32, name="branch_lru_dx")

    dgate, dhs = _rowwise(lambda _, dv, hv, gv: (dv * hv * _dgelu(gv.astype(F32)), dv * _gelu(gv.astype(F32))),
                          [dy_c, s['hs'], s['gate']], [(d, BF16), (d, F32)], name="lru_out_bwd")
    a3 = s['a_l'].reshape(b, l, d)
    a_next = jnp.concatenate([a3[:, 1:], jnp.zeros((b, 1, d), F32)], axis=1)
    gs = _linear_scan(a_next, dhs.reshape(b, l, d), True, "lru_scan_bwd").reshape(t, d)
    h3 = s['hs'].reshape(b, l, d)
    hprev = jnp.concatenate([jnp.zeros((b, 1, d), F32), h3[:, :-1]], axis=1).reshape(t, d)
    dpre_ri, dxc0, dlam, dba_, dbx_ = _rowwise(_lru_point_bwd, [gs, hprev, s['pre_ri'], s['xc']],
                                               [(2 * d, BF16), (d, F32)], bcast=s['lvec'],
                                               reds=[(1, d)] * 3, name="lru_point_bwd", period=l)
    g['lru_lambda'], g['lru_b_a'], g['lru_b_x'] = dlam[0], dba_[0], dbx_[0]
    dxc = _mm_nn(dpre_ri, p['lru_w_ri_t'], BF16, res=dxc0, name="lru_gates_dx")
    dw_ri = _mm_tn(s['xc'], dpre_ri, name="lru_gates_dw")
    g['lru_w_a'] = _diag_blocks(dw_ri[:, :d], LRU_BLOCKS)
    g['lru_w_x'] = _diag_blocks(dw_ri[:, d:], LRU_BLOCKS)
    dxr, dwl = _conv_bwd(s['xr'], dxc.reshape(b, l, d), p['lru_conv_w'], "lru_conv_bwd")
    g['lru_conv_w'], g['lru_conv_b'] = dwl[:CONV_K], dwl[CONV_K]

    et_mat = s['e_mat'].T
    dxs, dbm, dcm, dz, ddt, dd_l, dnw, dap = _ssd_bwd(s['xbc_act'], s['z'], s['dt3'], s['da3'], s['dat3'], s['e_mat'],
                                                      et_mat, s['dx'], s['nw'], s['a128'], s['yraw'], s['hin'],
                                                      dy_b.reshape(b, l, d_ssd), "ssd_bwd")
    g['ssd_d'] = dd_l.reshape(heads_ssd, SSD_HEAD_DIM).sum(axis=1)
    g['ssd_norm'] = dnw[0]
    g['ssd_a_log'] = dap[0, nh:nh + heads_ssd] * (-jnp.exp(p['ssd_a_log']))
    dxbc_act = jnp.concatenate([dxs, dbm, dcm], axis=2).reshape(t, d_xbc)
    dpre_b = _rowwise(lambda _, dv, pv: dv.astype(F32) * _dsilu(pv.astype(F32)),
                      [dxbc_act, s['pre_b'].reshape(t, d_xbc)], [(d_xbc, BF16)], name="ssd_conv_act_bwd")[0]
    dxbc, dws = _conv_bwd(s['xbc'], dpre_b.reshape(b, l, d_xbc), p['ssd_conv_w'], "ssd_conv_bwd")
    g['ssd_conv_w'], g['ssd_conv_b'] = dws[:CONV_K], dws[CONV_K]

    qa2, doa = _attn_prep_bwd(dy_a.reshape(b, l, d), s['y_a'].reshape(b, l, d), s['qa'], s['lse'], "fox_prep_bwd")
    dk3, dv3, dck = _flash_bwd_kv(qa2, s['ka'], s['va'], doa, d, "fox_bwd_kv")
    dq3, dcq = _flash_bwd_q(qa2, s['ka'], s['va'], doa, d, "fox_bwd_q")
    dcum = jnp.pad((dcq - dck)[..., 0].transpose(0, 2, 1), ((0, 0), (0, 0), (0, SMALL_W - nh)))
    dlogf = _cumsum_seq(dcum, True, "fox_cumsum_bwd").reshape(t, SMALL_W)

    def prep_bwd(_, v, dlf, ddtv, fbv, dtbv):
        a_ = dlf * _sigmoid(-(v + fbv))
        b_ = ddtv * _sigmoid(v + dtbv)
        return a_ + b_, jnp.sum(a_, axis=0, keepdims=True), jnp.sum(b_, axis=0, keepdims=True)
    dps, dfb, ddtb = _rowwise(prep_bwd, [s['ps'], dlogf, ddt.reshape(t, SMALL_W)], [(SMALL_W, F32)],
                              bcast=[s['fb'], s['dtb']], reds=[(1, SMALL_W)] * 2, name="mix_prep_bwd")
    g['fox_forget_bias'] = dfb[0, :nh]
    g['ssd_dt_bias'] = ddtb[0, nh:nh + heads_ssd]

    dpm = jnp.concatenate([dq3.reshape(t, d), dk3.reshape(t, d), dv3.reshape(t, d),
                           dz.reshape(t, d_ssd), dxbc.reshape(t, d_xbc), dxr.reshape(t, d), dgate, dmerge], axis=1)
    dxn = _mm_nn(dps, p['w_small_t'], F32, name="mix_in_small_dx")
    dxn = _mm_nn(dpm, p['w_main_t'], F32, res=dxn, name="mix_in_main_dx")
    g['w_main'] = _mm_tn(s['xn'], dpm, name="mix_in_main_dw")
    g['w_small'] = _mm_tn(s['xn'], dps, name="mix_in_small_dw")
    dh, dg = _rms_bwd(s['h'], dxn, dout, p['mix_norm'][None, :], "mix_norm_bwd")
    g['mix_norm'] = dg[0]
    return dh, g


def _layer_params(w, li):
    p = {n: w[n][li] for n in WEIGHTS if n not in ('meta_tokens', 'final_norm')}
    bf = lambda a: a.astype(BF16)
    p['ffn1_gu'], p['ffn1_gu_t'] = bf(p['ffn1_w_gate_up']), bf(p['ffn1_w_gate_up']).T
    p['ffn1_d'], p['ffn1_d_t'] = bf(p['ffn1_w_down']), bf(p['ffn1_w_down']).T
    p['ffn2_gu'], p['ffn2_gu_t'] = bf(p['ffn2_w_gate_up']), bf(p['ffn2_w_gate_up']).T
    p['ffn2_d'], p['ffn2_d_t'] = bf(p['ffn2_w_down']), bf(p['ffn2_w_down']).T
    wm, ws = _w_in_split(bf(p['w_in']))
    p['w_main'], p['w_main_t'], p['w_small'], p['w_small_t'] = wm, wm.T, ws, ws.T
    for n in ('w_branch_attn', 'w_branch_ssd', 'w_branch_lru', 'w_out'):
        p[n + '_t'] = bf(p[n]).T
        p[n] = bf(p[n])
    wri = jnp.concatenate([_block_diag(p['lru_w_a']), _block_diag(p['lru_w_x'])], axis=1)
    p['lru_w_ri'], p['lru_w_ri_t'] = bf(wri), bf(wri).T
    return p


def _local_step(x, loss_target, w):
    b, seq, d = x.shape
    length = N_META + seq
    l = -(-length // Q_BLOCK) * Q_BLOCK
    t = b * l
    meta = jnp.broadcast_to(w['meta_tokens'].astype(F32)[None], (b, N_META, d))
    h = jnp.concatenate([meta, x, jnp.zeros((b, l - length, d), F32)], axis=1).reshape(t, d)
    tgt = jnp.concatenate([jnp.zeros((b, N_META, d), F32), loss_target, jnp.zeros((b, l - length, d), F32)],
                          axis=1).reshape(t, d)
    params, saves = [], []
    for li in range(DEPTH):
        p = _layer_params(w, li)
        h, s1 = _ffn_fwd(h, p['ffn1_norm'], p['ffn1_gu'], p['ffn1_d'], "ffn1")
        h, sm = _mixer_fwd(h, p, b, l)
        h, s2 = _ffn_fwd(h, p['ffn2_norm'], p['ffn2_gu'], p['ffn2_d'], "ffn2")
        params.append(p)
        saves.append((s1, sm, s2))
    dh, loss, dgf = _loss_head(h, tgt, w['final_norm'][None, :], l, "loss_head")
    layer_grads = [None] * DEPTH
    for li in reversed(range(DEPTH)):
        p = params[li]
        s1, sm, s2 = saves[li]
        g = {}
        dh, g['ffn2_norm'], g['ffn2_w_gate_up'], g['ffn2_w_down'] = _ffn_bwd(dh, s2, p['ffn2_norm'], p['ffn2_gu_t'],
                                                                              p['ffn2_d_t'], "ffn2b")
        dh, gm = _mixer_bwd(dh, sm, p, b, l)
        g.update(gm)
        g['w_in'] = _w_in_merge(g.pop('w_main'), g.pop('w_small'))
        dh, g['ffn1_norm'], g['ffn1_w_gate_up'], g['ffn1_w_down'] = _ffn_bwd(dh, s1, p['ffn1_norm'], p['ffn1_gu_t'],
                                                                              p['ffn1_d_t'], "ffn1b")
        layer_grads[li] = g
    grads = {n: jnp.stack([layer_grads[li][n] for li in range(DEPTH)]) for n in layer_grads[0]}
    for n in ('lru_w_a', 'lru_w_x'):
        grads[n] = grads[n].reshape(w[n].shape)
    dh3 = dh.reshape(b, l, d)
    grads['meta_tokens'] = jnp.sum(dh3[:, :N_META], axis=0)
    grads['final_norm'] = dgf[0]
    return loss, dh3[:, N_META:N_META + seq], grads


_ROW_W = 1024
_D2D_SPLIT = 16


def _half_size(total, row_quant):
    q = _ROW_W * row_quant
    return -(-(-(-total // 2)) // q) * q


def _flat_shard(parts, half):
    v = jnp.concatenate([p.reshape(-1) for p in parts])
    return jnp.pad(v, (0, 2 * half - v.shape[0]))


def _chip_shards(full, axis, nchip):
    return jnp.split(full, nchip, axis=axis + 1)


def _unflatten(flat, shapes):
    out, o = [], 0
    for sh in shapes:
        n = int(np.prod(sh))
        out.append(flat[o:o + n].reshape(sh))
        o += n
    return out


def kernel(x, meta_tokens, ffn1_norm, ffn1_w_gate_up, ffn1_w_down, mix_norm, w_in, fox_forget_bias, ssd_conv_w, ssd_conv_b, ssd_dt_bias, ssd_a_log, ssd_d, ssd_norm, lru_conv_w, lru_conv_b, lru_w_a, lru_b_a, lru_w_x, lru_b_x, lru_lambda, w_branch_attn, w_branch_ssd, w_branch_lru, w_out, ffn2_norm, ffn2_w_gate_up, ffn2_w_down, final_norm, loss_target, m_meta_tokens, m_ffn1_norm, m_ffn1_w_gate_up, m_ffn1_w_down, m_mix_norm, m_w_in, m_fox_forget_bias, m_ssd_conv_w, m_ssd_conv_b, m_ssd_dt_bias, m_ssd_a_log, m_ssd_d, m_ssd_norm, m_lru_conv_w, m_lru_conv_b, m_lru_w_a, m_lru_b_a, m_lru_w_x, m_lru_b_x, m_lru_lambda, m_w_branch_attn, m_w_branch_ssd, m_w_branch_lru, m_w_out, m_ffn2_norm, m_ffn2_w_gate_up, m_ffn2_w_down, m_final_norm, v_meta_tokens, v_ffn1_norm, v_ffn1_w_gate_up, v_ffn1_w_down, v_mix_norm, v_w_in, v_fox_forget_bias, v_ssd_conv_w, v_ssd_conv_b, v_ssd_dt_bias, v_ssd_a_log, v_ssd_d, v_ssd_norm, v_lru_conv_w, v_lru_conv_b, v_lru_w_a, v_lru_b_a, v_lru_w_x, v_lru_b_x, v_lru_lambda, v_w_branch_attn, v_w_branch_ssd, v_w_branch_lru, v_w_out, v_ffn2_norm, v_ffn2_w_gate_up, v_ffn2_w_down, v_final_norm):
    args = locals()
    wloc = {n: args[n] for n in WEIGHTS}
    mloc = {n: args['m_' + n] for n in WEIGHTS}
    vloc = {n: args['v_' + n] for n in WEIGHTS}
    nchip = 4
    chip = 2 * lax.axis_index("x") + lax.axis_index("y")
    core = lax.axis_index("c")
    row_quant = 256 if D_MODEL >= 1024 else 8

    big_shapes = [wloc[n].shape for n in BIG_NAMES]
    total = sum(int(np.prod(s)) for s in big_shapes)
    half = _half_size(total, row_quant)
    hrows = half // _ROW_W
    wflat = _flat_shard([wloc[n].astype(BF16) for n in BIG_NAMES], half).reshape(2, hrows, _ROW_W)
    my_half = lax.dynamic_index_in_dim(wflat, core, axis=0, keepdims=False)
    got = _exchange(my_half, 'xy', False, "gather_w_chips")
    both = _exchange(got.reshape(nchip * hrows, _ROW_W), 'c', False, "gather_w_cores", nsplit=_D2D_SPLIT)
    wall = both.reshape(2, nchip, hrows, _ROW_W).transpose(1, 0, 2, 3).reshape(nchip, 2 * half)
    full = {}
    per_chip = [_unflatten(wall[j], big_shapes) for j in range(nchip)]
    for i, n in enumerate(BIG_NAMES):
        full[n] = jnp.concatenate([per_chip[j][i] for j in range(nchip)], axis=BIG[n] + 1)
    cs_shapes = [wloc[n].shape for n in COLSHARD_SMALL]
    cs_total = sum(int(np.prod(s)) for s in cs_shapes)
    cs_rows = -(-cs_total // (8 * 128)) * 8
    cs_flat = jnp.concatenate([wloc[n].reshape(-1) for n in COLSHARD_SMALL])
    cs_flat = jnp.pad(cs_flat, (0, cs_rows * 128 - cs_total)).reshape(cs_rows, 128)
    cs_all = _exchange(cs_flat, 'xy', False, "gather_small").reshape(nchip, -1)
    cs_chip = [_unflatten(cs_all[j], cs_shapes) for j in range(nchip)]
    for i, n in enumerate(COLSHARD_SMALL):
        full[n] = jnp.concatenate([cs_chip[j][i] for j in range(nchip)], axis=-1)
    for n in SMALL_NAMES:
        if n not in COLSHARD_SMALL:
            full[n] = wloc[n]

    loss_part, grad_x, grads = _local_step(x, loss_target, full)

    gsh = [[s for s in _chip_shards(grads[n], BIG[n], nchip)] for n in BIG_NAMES]
    gflat = jnp.stack([_flat_shard([gsh[i][j] for i in range(len(BIG_NAMES))], half) for j in range(nchip)])
    gflat = gflat.reshape(nchip, 2, hrows, _ROW_W).transpose(1, 0, 2, 3).reshape(2, nchip * hrows, _ROW_W)
    give = lax.dynamic_index_in_dim(gflat, 1 - core, axis=0, keepdims=False)
    keep = lax.dynamic_index_in_dim(gflat, core, axis=0, keepdims=False)
    pair = _exchange(give, 'c', False, "reduce_cores", nsplit=_D2D_SPLIT, copy_own=False)
    theirs = lax.dynamic_index_in_dim(pair, 1 - core, axis=0, keepdims=False)
    psum = _sum_rows([keep, theirs], BF16, "reduce_cores_sum").reshape(nchip, hrows, _ROW_W)
    parts = _exchange(psum, 'xy', True, "reduce_chips")
    rsum = _sum_rows([parts[j] for j in range(nchip)], F32, "reduce_chips_sum")
    halves = _exchange(rsum, 'c', False, "reduce_share", nsplit=_D2D_SPLIT)
    gshard = halves.reshape(-1)
    gbig = dict(zip(BIG_NAMES, _unflatten(gshard, big_shapes)))

    sm_shapes = [grads[n].shape for n in SMALL_NAMES]
    sm_total = sum(int(np.prod(s)) for s in sm_shapes) + 128
    sm_rows = -(-sm_total // (8 * 128)) * 8
    sm_flat = jnp.concatenate([loss_part.reshape(-1)] + [grads[n].reshape(-1) for n in SMALL_NAMES])
    sm_flat = jnp.pad(sm_flat, (0, sm_rows * 128 - sm_total)).reshape(sm_rows, 128)
    sm_all = _exchange(sm_flat, 'xyc', False, "gather_small_grads")
    sm_sum = _sum_rows([sm_all[j] for j in range(8)], F32, "small_grads_sum").reshape(-1)
    loss = sm_sum[0]
    gsmall_full = dict(zip(SMALL_NAMES, _unflatten(sm_sum[128:], sm_shapes)))
    gsmall = {}
    for n in SMALL_NAMES:
        gfull = gsmall_full[n]
        if n in COLSHARD_SMALL:
            wcols = wloc[n].shape[-1]
            gfull = lax.dynamic_slice_in_dim(gfull, chip * wcols, wcols, axis=gfull.ndim - 1)
        gsmall[n] = gfull

    def flat_big(dct):
        return _flat_shard([dct[n] for n in BIG_NAMES], half).reshape(2 * hrows, _ROW_W)
    dl, mn, vn = _adamw(flat_big(wloc), gshard.reshape(2 * hrows, _ROW_W), flat_big(mloc), flat_big(vloc), "adamw_big")
    big_out = [dict(zip(BIG_NAMES, _unflatten(a.reshape(-1), big_shapes))) for a in (dl, mn, vn)]
    loc_shapes = [wloc[n].shape for n in SMALL_NAMES]
    loc_total = sum(int(np.prod(s)) for s in loc_shapes)
    loc_rows = -(-loc_total // (8 * 128)) * 8

    def flat_small(dct):
        v = jnp.concatenate([dct[n].reshape(-1) for n in SMALL_NAMES])
        return jnp.pad(v, (0, loc_rows * 128 - loc_total)).reshape(loc_rows, 128)
    dls, mns, vns = _adamw(flat_small(wloc), flat_small(gsmall), flat_small(mloc), flat_small(vloc), "adamw_small")
    small_out = [dict(zip(SMALL_NAMES, _unflatten(a.reshape(-1), loc_shapes))) for a in (dls, mns, vns)]

    grad_w = {**gbig, **gsmall}
    outs = [loss, grad_x] + [grad_w[n] for n in WEIGHTS]
    for k in range(3):
        merged = {**big_out[k], **small_out[k]}
        outs += [merged[n] for n in WEIGHTS]
    return tuple(outs)
```

```python
import functools
import math

import numpy as np
import jax
import jax.numpy as jnp
from jax import lax
from jax.experimental import pallas as pl
from jax.experimental.pallas import tpu as pltpu

F32 = jnp.float32
BF16 = jnp.bfloat16
HI = lax.Precision.HIGHEST
VMEM_LIMIT_BYTES = 56 * 1024 * 1024
NEG = -1e30

D_MODEL = 1024
SEQ = 4096
DEPTH = 4
N_META = 16
Q_BLOCK = 128
SSD_CHUNK = 128
NORM_EPS = 1e-6
ATTN_HEADS = 16
ATTN_HEAD_DIM = 64
SSD_HEAD_DIM = 64
SSD_GROUPS = 2
SSD_STATE = 128
CONV_K = 4
LRU_BLOCKS = 16
LRU_C = 8.0
D_FF = 2816
ADAM_LR = 0.001
ADAM_B1 = 0.9
ADAM_B2 = 0.999
ADAM_EPS = 1e-08
ADAM_WD = 0.01
ADAM_STEP = 10
SMALL_W = 128
_ROWWISE_TILE_ELEMS = 512 * 1024

WEIGHTS = ['meta_tokens', 'ffn1_norm', 'ffn1_w_gate_up', 'ffn1_w_down', 'mix_norm', 'w_in', 'fox_forget_bias',
           'ssd_conv_w', 'ssd_conv_b', 'ssd_dt_bias', 'ssd_a_log', 'ssd_d', 'ssd_norm', 'lru_conv_w', 'lru_conv_b',
           'lru_w_a', 'lru_b_a', 'lru_w_x', 'lru_b_x', 'lru_lambda', 'w_branch_attn', 'w_branch_ssd', 'w_branch_lru',
           'w_out', 'ffn2_norm', 'ffn2_w_gate_up', 'ffn2_w_down', 'final_norm']
BIG = {'ffn1_w_gate_up': 1, 'ffn1_w_down': 0, 'w_in': 1, 'w_branch_attn': 0, 'w_branch_ssd': 0, 'w_branch_lru': 0,
       'w_out': 0, 'ffn2_w_gate_up': 1, 'ffn2_w_down': 0}
BIG_NAMES = [n for n in WEIGHTS if n in BIG]
COLSHARD_SMALL = ['meta_tokens', 'ssd_conv_w', 'lru_conv_w']
SMALL_NAMES = [n for n in WEIGHTS if n not in BIG]


def _pick(n, cands):
    for c in cands:
        if n % c == 0:
            return c
    raise ValueError(f"no tile for {n} in {cands}")


def _pcall(body, **kw):
    return pl.pallas_call(body, **kw)


def _cparams(sem):
    return pltpu.CompilerParams(dimension_semantics=sem, vmem_limit_bytes=VMEM_LIMIT_BYTES)


def _sds(shape, dtype):
    return jax.ShapeDtypeStruct(tuple(shape), dtype)


def _dot(a, b, hi=False):
    return jnp.dot(a, b, precision=HI if hi else None, preferred_element_type=F32)


def _dot_nt(a, b):
    return lax.dot_general(a, b, (((1,), (1,)), ((), ())), preferred_element_type=F32)


def _dot_tn(a, b):
    return lax.dot_general(a, b, (((0,), (0,)), ((), ())), preferred_element_type=F32)


def _sigmoid(x):
    return 1.0 / (1.0 + jnp.exp(-x))


def _softplus(x):
    return jnp.maximum(x, 0.0) + jnp.log1p(jnp.exp(-jnp.abs(x)))


def _silu(x):
    return x * _sigmoid(x)


def _dsilu(x):
    s = _sigmoid(x)
    return s * (1.0 + x * (1.0 - s))


_GELU_C = math.sqrt(2.0 / math.pi)


def _gelu(x):
    return 0.5 * x * (1.0 + jnp.tanh(_GELU_C * (x + 0.044715 * x * x * x)))


def _dgelu(x):
    t = jnp.tanh(_GELU_C * (x + 0.044715 * x * x * x))
    return 0.5 * (1.0 + t) + 0.5 * x * (1.0 - t * t) * _GELU_C * (1.0 + 3.0 * 0.044715 * x * x)


def _expm1(x):
    series = x * (1.0 + x * 0.5 * (1.0 + x * (1.0 / 3.0) * (1.0 + x * 0.25 * (1.0 + x * 0.2))))
    return jnp.where(jnp.abs(x) < 0.05, series, jnp.exp(x) - 1.0)


def _rowwise(fn, ins, outs, *, bcast=(), reds=(), tm=None, name, period=None):
    t_rows = ins[0].shape[0]
    if tm is None:
        widest = max([a.shape[1] for a in ins] + [c for c, _ in outs])
        tm = _pick(math.gcd(t_rows, period or t_rows),
                   [c for c in (384, 256, 128, 64, 32, 16, 8) if c * widest <= _ROWWISE_TILE_ELEMS or c == 8])
    nt = t_rows // tm
    assert t_rows % tm == 0 and (period is None or period % tm == 0)
    n_in, n_out = len(ins) + len(bcast), len(outs)

    def body(*refs):
        i = pl.program_id(0)
        pos = None
        if period is not None:
            pos = (i * tm) % period + lax.broadcasted_iota(jnp.int32, (tm, 1), 0)
        res = fn(pos, *[r[...] for r in refs[:n_in]])
        res = res if isinstance(res, tuple) else (res,)
        for r, v in zip(refs[n_in:n_in + n_out], res[:n_out]):
            r[...] = v.astype(r.dtype)
        red_refs = refs[n_in + n_out:]
        if red_refs:
            @pl.when(i == 0)
            def _():
                for r in red_refs:
                    r[...] = jnp.zeros_like(r)
            for r, v in zip(red_refs, res[n_out:]):
                r[...] += v

    in_specs = [pl.BlockSpec((tm, a.shape[1]), lambda i: (i, 0)) for a in ins]
    in_specs += [pl.BlockSpec(b.shape, lambda i, n=b.ndim: (0,) * n) for b in bcast]
    out_specs = [pl.BlockSpec((tm, c), lambda i: (i, 0)) for c, _ in outs]
    out_specs += [pl.BlockSpec(s, lambda i: (0, 0)) for s in reds]
    out_shape = [_sds((t_rows, c), dt) for c, dt in outs] + [_sds(s, F32) for s in reds]
    res = _pcall(body, name=name, grid=(nt,), in_specs=in_specs, out_specs=out_specs, out_shape=out_shape,
                 compiler_params=_cparams(("arbitrary",) if reds else ("parallel",)))(*ins, *bcast)
    return res


_TM = (768, 384, 256, 128)
_TN = (1536, 1408, 1024, 768, 512, 640, 384, 256, 128)
_TK = (1536, 1024, 1408, 512, 384, 256, 128)


def _mm_nn(a, b, out_dtype, *, res=None, alpha=1.0, name):
    m, k = a.shape
    k2, n = b.shape
    assert k == k2
    tm, tn, tk = _pick(m, _TM), _pick(n, _TN), _pick(k, _TK)
    nk = k // tk

    def body(*refs):
        if res is None:
            a_ref, b_ref, o_ref, acc = refs
            r_ref = None
        else:
            a_ref, b_ref, r_ref, o_ref, acc = refs
        kk = pl.program_id(2)

        @pl.when(kk == 0)
        def _():
            acc[...] = jnp.zeros_like(acc)

        acc[...] += _dot(a_ref[...].astype(BF16), b_ref[...].astype(BF16))

        @pl.when(kk == nk - 1)
        def _():
            v = acc[...]
            if alpha != 1.0:
                v = v * alpha
            if r_ref is not None:
                v = r_ref[...].astype(F32) + v
            o_ref[...] = v.astype(o_ref.dtype)

    in_specs = [pl.BlockSpec((tm, tk), lambda j, i, kk: (i, kk)), pl.BlockSpec((tk, tn), lambda j, i, kk: (kk, j))]
    args = [a, b]
    if res is not None:
        in_specs.append(pl.BlockSpec((tm, tn), lambda j, i, kk: (i, j)))
        args.append(res)
    return _pcall(body, name=name, grid=(n // tn, m // tm, nk), in_specs=in_specs,
                  out_specs=pl.BlockSpec((tm, tn), lambda j, i, kk: (i, j)), out_shape=_sds((m, n), out_dtype),
                  scratch_shapes=[pltpu.VMEM((tm, tn), F32)],
                  compiler_params=_cparams(("parallel", "parallel", "arbitrary")))(*args)


def _mm_tn(a, b, *, alpha=1.0, name):
    m, k = a.shape
    m2, n = b.shape
    assert m == m2
    tm, tn, tko = _pick(m, _TM), _pick(n, _TN), _pick(k, _TK)
    nm = m // tm

    def body(a_ref, b_ref, o_ref, acc):
        mm = pl.program_id(2)

        @pl.when(mm == 0)
        def _():
            acc[...] = jnp.zeros_like(acc)

        acc[...] += _dot_tn(a_ref[...].astype(BF16), b_ref[...].astype(BF16))

        @pl.when(mm == nm - 1)
        def _():
            v = acc[...]
            o_ref[...] = v * alpha if alpha != 1.0 else v

    return _pcall(body, name=name, grid=(k // tko, n // tn, nm),
                  in_specs=[pl.BlockSpec((tm, tko), lambda i, j, mm: (mm, i)),
                            pl.BlockSpec((tm, tn), lambda i, j, mm: (mm, j))],
                  out_specs=pl.BlockSpec((tko, tn), lambda i, j, mm: (i, j)), out_shape=_sds((k, n), F32),
                  scratch_shapes=[pltpu.VMEM((tko, tn), F32)],
                  compiler_params=_cparams(("parallel", "parallel", "arbitrary")))(a, b)


def _rms_fwd(h, g, name):
    def fn(_, hv, gv):
        r = lax.rsqrt(jnp.mean(hv * hv, axis=1, keepdims=True) + NORM_EPS)
        return hv * r * gv
    return _rowwise(fn, [h], [(h.shape[1], BF16)], bcast=[g], name=name)[0]


def _rms_bwd(h, dxn, dres, g, name):
    d = h.shape[1]

    def fn(_, hv, dv, rv, gv):
        r = lax.rsqrt(jnp.mean(hv * hv, axis=1, keepdims=True) + NORM_EPS)
        xh = hv * r
        dxh = dv * gv
        dh = r * (dxh - xh * jnp.mean(dxh * xh, axis=1, keepdims=True))
        return rv + dh, jnp.sum(dv * xh, axis=0, keepdims=True)
    return _rowwise(fn, [h, dxn, dres], [(d, F32)], bcast=[g], reds=[(1, d)], name=name)


def _swiglu_fwd(gu, name):
    f = gu.shape[1] // 2

    def fn(_, v):
        v = v.astype(F32)
        return _silu(v[:, :f]) * v[:, f:]
    return _rowwise(fn, [gu], [(f, BF16)], name=name)[0]


def _swiglu_bwd(gu, dact, name):
    f = gu.shape[1] // 2

    def fn(_, v, dv):
        v = v.astype(F32)
        dv = dv.astype(F32)
        g, u = v[:, :f], v[:, f:]
        return jnp.concatenate([dv * u * _dsilu(g), dv * _silu(g)], axis=1)
    return _rowwise(fn, [gu, dact], [(2 * f, BF16)], name=name)[0]


def _merge_fwd(mg, ba, bb, bc, name):
    d = ba.shape[1]

    def fn(_, m, a, b, c):
        g = _sigmoid(m.astype(F32))
        return g[:, :d] * a.astype(F32) + g[:, d:2 * d] * b.astype(F32) + g[:, 2 * d:] * c.astype(F32)
    return _rowwise(fn, [mg, ba, bb, bc], [(d, BF16)], name=name)[0]


def _merge_bwd(mg, ba, bb, bc, dmix, name):
    d = ba.shape[1]

    def fn(_, m, a, b, c, dm):
        g = _sigmoid(m.astype(F32))
        dm = dm.astype(F32)
        br = (a.astype(F32), b.astype(F32), c.astype(F32))
        douts, dgs = [], []
        for j in range(3):
            gj = g[:, j * d:(j + 1) * d]
            douts.append(dm * gj)
            dgs.append(dm * br[j] * gj * (1.0 - gj))
        return (*douts, jnp.concatenate(dgs, axis=1))
    return _rowwise(fn, [mg, ba, bb, bc, dmix], [(d, BF16)] * 3 + [(3 * d, BF16)], name=name)


def _loss_head(h, tgt, g, seq_len, name):
    d = h.shape[1]

    def fn(pos, hv, tv, gv):
        r = lax.rsqrt(jnp.mean(hv * hv, axis=1, keepdims=True) + NORM_EPS)
        xh = hv * r
        real = (pos >= N_META) & (pos < N_META + SEQ)
        e = jnp.where(real, xh * gv - tv, 0.0)
        part = jnp.sum(jnp.sum(e * e, axis=1, keepdims=True), axis=0, keepdims=True) * (0.5 / d)
        dy = e * (1.0 / d)
        dxh = dy * gv
        dh = r * (dxh - xh * jnp.mean(dxh * xh, axis=1, keepdims=True))
        return dh, jnp.broadcast_to(part, (1, 128)), jnp.sum(dy * xh, axis=0, keepdims=True)
    return _rowwise(fn, [h, tgt], [(d, F32)], bcast=[g], reds=[(1, 128), (1, d)], name=name, period=seq_len)


def _adamw(w, g, m, v, name):
    c1 = 1.0 - ADAM_B1 ** ADAM_STEP
    c2 = 1.0 - ADAM_B2 ** ADAM_STEP
    wd = w.shape[1]

    def fn(_, wv, gv, mv, vv):
        mn = ADAM_B1 * mv + (1.0 - ADAM_B1) * gv
        vn = ADAM_B2 * vv + (1.0 - ADAM_B2) * (gv * gv)
        delta = -ADAM_LR * ((mn / c1) / (jnp.sqrt(vn / c2) + ADAM_EPS) + ADAM_WD * wv)
        return delta, mn, vn
    return _rowwise(fn, [w, g, m, v], [(wd, F32)] * 3, name=name, tm=_pick(w.shape[0], (256, 128, 64, 32, 16, 8)))


def _sum_rows(parts, out_dtype, name):
    def fn(_, *vs):
        acc = vs[0].astype(F32)
        for v in vs[1:]:
            acc = acc + v.astype(F32)
        return acc
    return _rowwise(fn, list(parts), [(parts[0].shape[1], out_dtype)], name=name,
                    tm=_pick(parts[0].shape[0], (256, 128, 64, 32, 16, 8)))[0]


def _cumsum_seq(x, reverse, name):
    b, l, w = x.shape
    q = 128
    nc = l // q

    def body(x_ref, o_ref):
        row = lax.broadcasted_iota(jnp.int32, (q, q), 0)
        col = lax.broadcasted_iota(jnp.int32, (q, q), 1)
        tri = ((row <= col) if reverse else (row >= col)).astype(F32)
        rsel = lax.broadcasted_iota(jnp.int32, (q, w), 0) == (0 if reverse else q - 1)

        def step(i, carry):
            j = (nc - 1 - i) if reverse else i
            start = pl.multiple_of(j * q, q)
            cs = _dot(tri, x_ref[pl.ds(start, q), :], hi=True) + carry
            o_ref[pl.ds(start, q), :] = cs
            return jnp.sum(jnp.where(rsel, cs, 0.0), axis=0, keepdims=True)

        lax.fori_loop(0, nc, step, jnp.zeros((1, w), F32))

    return _pcall(body, name=name, grid=(b,), in_specs=[pl.BlockSpec((None, l, w), lambda i: (i, 0, 0))],
                  out_specs=pl.BlockSpec((None, l, w), lambda i: (i, 0, 0)), out_shape=_sds(x.shape, F32),
                  compiler_params=_cparams(("parallel",)))(x)


_HALO = 16


def _conv_tiles(l, c):
    return _pick(l, (384, 256, 128)), _pick(c, (512, 256, 128))


def _conv_fwd(x, w, bias, out_dtype, name):
    b, l, c = x.shape
    tt, cw = _conv_tiles(l, c)

    def body(x_ref, h_ref, w_ref, b_ref, o_ref):
        t = pl.program_id(2)
        halo = jnp.where(t == 0, 0.0, h_ref[...].astype(F32))
        xe = jnp.concatenate([halo, x_ref[...].astype(F32)], axis=0)
        wv = w_ref[...]
        acc = b_ref[...] + wv[CONV_K - 1:CONV_K, :] * xe[_HALO:]
        for j in range(CONV_K - 1):
            acc = acc + wv[j:j + 1, :] * pltpu.roll(xe, CONV_K - 1 - j, 0)[_HALO:]
        o_ref[...] = acc.astype(o_ref.dtype)

    return _pcall(body, name=name, grid=(b, c // cw, l // tt),
                  in_specs=[pl.BlockSpec((None, tt, cw), lambda i, j, t: (i, t, j)),
                            pl.BlockSpec((None, _HALO, cw), lambda i, j, t: (i, jnp.maximum(t * (tt // _HALO) - 1, 0), j)),
                            pl.BlockSpec((CONV_K, cw), lambda i, j, t: (0, j)),
                            pl.BlockSpec((1, cw), lambda i, j, t: (0, j))],
                  out_specs=pl.BlockSpec((None, tt, cw), lambda i, j, t: (i, t, j)), out_shape=_sds(x.shape, out_dtype),
                  compiler_params=_cparams(("parallel", "parallel", "parallel")))(x, x, w, bias)


def _conv_bwd(x, dy, w, name):
    b, l, c = x.shape
    tt, cw = _conv_tiles(l, c)
    nt = l // tt

    def body(x_ref, xh_ref, d_ref, dh_ref, w_ref, dx_ref, dw_ref):
        i, t = pl.program_id(1), pl.program_id(2)
        halo = jnp.where(t == 0, 0.0, xh_ref[...].astype(F32))
        xe = jnp.concatenate([halo, x_ref[...].astype(F32)], axis=0)
        dv = d_ref[...].astype(F32)
        nxt = jnp.where(t == nt - 1, 0.0, dh_ref[...].astype(F32))
        de = jnp.concatenate([dv, nxt], axis=0)
        wv = w_ref[...]
        dx = wv[CONV_K - 1:CONV_K, :] * dv
        rowid = lax.broadcasted_iota(jnp.int32, (8, 1), 0)
        part = jnp.where(rowid == CONV_K, jnp.sum(dv, axis=0, keepdims=True), 0.0)
        part = part + jnp.where(rowid == CONV_K - 1, jnp.sum(dv * xe[_HALO:], axis=0, keepdims=True), 0.0)
        for j in range(CONV_K - 1):
            s = CONV_K - 1 - j
            dx = dx + wv[j:j + 1, :] * pltpu.roll(de, tt + _HALO - s, 0)[:tt]
            xs = pltpu.roll(xe, s, 0)[_HALO:]
            part = part + jnp.where(rowid == j, jnp.sum(dv * xs, axis=0, keepdims=True), 0.0)
        dx_ref[...] = dx.astype(dx_ref.dtype)

        @pl.when((i == 0) & (t == 0))
        def _():
            dw_ref[...] = jnp.zeros_like(dw_ref)
        dw_ref[...] += part

    return _pcall(body, name=name, grid=(c // cw, b, nt),
                  in_specs=[pl.BlockSpec((None, tt, cw), lambda j, i, t: (i, t, j)),
                            pl.BlockSpec((None, _HALO, cw), lambda j, i, t: (i, jnp.maximum(t * (tt // _HALO) - 1, 0), j)),
                            pl.BlockSpec((None, tt, cw), lambda j, i, t: (i, t, j)),
                            pl.BlockSpec((None, _HALO, cw),
                                         lambda j, i, t: (i, jnp.minimum((t + 1) * (tt // _HALO), l // _HALO - 1), j)),
                            pl.BlockSpec((CONV_K, cw), lambda j, i, t: (0, j))],
                  out_specs=[pl.BlockSpec((None, tt, cw), lambda j, i, t: (i, t, j)),
                             pl.BlockSpec((8, cw), lambda j, i, t: (0, j))],
                  out_shape=[_sds(x.shape, BF16), _sds((8, c), F32)],
                  compiler_params=_cparams(("parallel", "arbitrary", "arbitrary")))(x, x, dy, dy, w)


def _linear_scan(a, u, reverse, name):
    b, l, c = a.shape
    tt = 128
    cw = _pick(c, (512, 256, 128))
    nt = l // tt

    def body(a_ref, u_ref, h_ref, carry):
        t = pl.program_id(2)

        @pl.when(t == 0)
        def _():
            carry[...] = jnp.zeros_like(carry)

        av, uv = a_ref[...], u_ref[...]
        row = lax.broadcasted_iota(jnp.int32, (tt, cw), 0)
        k = 1
        while k < tt:
            if reverse:
                keep = row < tt - k
                a_sh = jnp.where(keep, pltpu.roll(av, tt - k, 0), 1.0)
                u_sh = jnp.where(keep, pltpu.roll(uv, tt - k, 0), 0.0)
            else:
                keep = row >= k
                a_sh = jnp.where(keep, pltpu.roll(av, k, 0), 1.0)
                u_sh = jnp.where(keep, pltpu.roll(uv, k, 0), 0.0)
            uv = uv + av * u_sh
            av = av * a_sh
            k *= 2
        hv = uv + av * carry[0:1, :]
        h_ref[...] = hv
        edge = jnp.sum(jnp.where(row == (0 if reverse else tt - 1), hv, 0.0), axis=0, keepdims=True)
        carry[...] = jnp.broadcast_to(edge, carry.shape)

    tmap = (lambda i, j, t: (i, nt - 1 - t, j)) if reverse else (lambda i, j, t: (i, t, j))
    spec = pl.BlockSpec((None, tt, cw), tmap)
    return _pcall(body, name=name, grid=(b, c // cw, nt), in_specs=[spec, spec], out_specs=spec,
                  out_shape=_sds(a.shape, F32), scratch_shapes=[pltpu.VMEM((8, cw), F32)],
                  compiler_params=_cparams(("parallel", "parallel", "arbitrary")))(a, u)


ATTN_W = 128
_AUG_C = 0
_AUG_ONE = 3
_AUG_LSE = 6


def _attn_blk(l):
    return _pick(l, (384, 256, 128))


def _split3(x):
    x1 = x.astype(BF16).astype(F32)
    x2 = (x - x1).astype(BF16).astype(F32)
    x3 = (x - x1 - x2).astype(BF16).astype(F32)
    return x1, x2, x3


def _aug_lanes(lane, base, vals):
    out = 0.0
    for k, v in enumerate(vals):
        out = jnp.where(lane == base + k, v, out)
    return out


def _pair_specs(blk, nb):
    at = lambda ww: pl.BlockSpec((None, 2, blk, ww), lambda bi, p, i: (bi, p, i, 0))
    whole = pl.BlockSpec((None, 2, nb, blk, ATTN_W), lambda bi, p, i: (bi, p, 0, 0, 0))
    rows = pl.BlockSpec((None, blk, ATTN_W), lambda bi, p, i: (bi, i, p))
    return at, whole, rows


def _attn_prep(pm3, cum, col0, name):
    b, l, _ = pm3.shape
    dh, nh = ATTN_HEAD_DIM, ATTN_HEADS
    blk = _attn_blk(l)
    scale = dh ** -0.5

    def body(q_ref, k_ref, v_ref, c_ref, qa_ref, ka_ref, va_ref):
        pair = pl.program_id(1)
        lane = lax.broadcasted_iota(jnp.int32, (1, ATTN_W), 1)
        head = lane < dh
        cv = c_ref[...]
        qf, kf, vf = (r[...].astype(F32) for r in (q_ref, k_ref, v_ref))
        for e in range(2):
            c1, c2, c3 = _split3(jnp.sum(jnp.where(lane == 2 * pair + e, cv, 0.0), axis=1, keepdims=True))
            qe, ke, ve = (pltpu.roll(t, dh, 1) for t in (qf, kf, vf)) if e else (qf, kf, vf)
            qa_ref[e] = jnp.where(head, qe * scale, _aug_lanes(lane, dh, (c1, c2, c3, 1.0, 1.0, 1.0))).astype(BF16)
            ka_ref[e] = jnp.where(head, ke, _aug_lanes(lane, dh, (1.0, 1.0, 1.0, -c1, -c2, -c3, 1.0, 1.0, 1.0))).astype(BF16)
            va_ref[e] = jnp.where(head, ve, _aug_lanes(lane, dh, (1.0, 1.0, 1.0))).astype(BF16)

    at, _, _ = _pair_specs(blk, l // blk)
    cols = lambda c0: pl.BlockSpec((None, blk, ATTN_W), lambda bi, p, i, c0=c0: (bi, i, c0 // ATTN_W + p))
    return _pcall(body, name=name, grid=(b, nh // 2, l // blk),
                  in_specs=[cols(col0[0]), cols(col0[1]), cols(col0[2]),
                            pl.BlockSpec((None, blk, SMALL_W), lambda bi, p, i: (bi, i, 0))],
                  out_specs=[at(ATTN_W)] * 3, out_shape=[_sds((b, nh, l, ATTN_W), BF16)] * 3,
                  compiler_params=_cparams(("parallel", "parallel", "parallel")))(pm3, pm3, pm3, cum)


def _attn_prep_bwd(dy3, y3, qa, lse, name):
    b, nh, l, _ = qa.shape
    dh = ATTN_HEAD_DIM
    blk = _attn_blk(l)

    def body(dy_ref, y_ref, qa_ref, lse_ref, qa2_ref, doa_ref):
        lane = lax.broadcasted_iota(jnp.int32, (1, ATTN_W), 1)
        dyf = dy_ref[...].astype(F32)
        prod = dyf * y_ref[...].astype(F32)
        for e in range(2):
            mine = (lane >= dh) if e else (lane < dh)
            d1, d2, d3 = _split3(jnp.sum(jnp.where(mine, prod, 0.0), axis=1, keepdims=True))
            l1, l2, l3 = _split3(lse_ref[e])
            dye = pltpu.roll(dyf, dh, 1) if e else dyf
            doa_ref[e] = jnp.where(lane < dh, dye, _aug_lanes(lane, dh, (-d1, -d2, -d3))).astype(BF16)
            on_lse = (lane >= dh + _AUG_LSE) & (lane < dh + _AUG_LSE + 3)
            qa2_ref[e] = jnp.where(on_lse, _aug_lanes(lane, dh + _AUG_LSE, (-l1, -l2, -l3)),
                                   qa_ref[e].astype(F32)).astype(BF16)

    at, _, rows = _pair_specs(blk, l // blk)
    return _pcall(body, name=name, grid=(b, nh // 2, l // blk), in_specs=[rows, rows, at(ATTN_W), at(1)],
                  out_specs=[at(ATTN_W)] * 2, out_shape=[_sds(qa.shape, BF16)] * 2,
                  compiler_params=_cparams(("parallel", "parallel", "parallel")))(dy3, y3, qa, lse)


def _flash_fwd(qa, ka, va, d_model, name):
    b, h, l, w = qa.shape
    dh = ATTN_HEAD_DIM
    blk = _attn_blk(l)
    nb = l // blk
    kr, vr = ka.reshape(b, h, nb, blk, w), va.reshape(b, h, nb, blk, w)

    def body(q_ref, k_ref, v_ref, o_ref, lse_ref):
        i = pl.program_id(2)
        row = lax.broadcasted_iota(jnp.int32, (blk, blk), 0)
        col = lax.broadcasted_iota(jnp.int32, (blk, blk), 1)

        def scores(e, j):
            return _dot_nt(q_ref[e], k_ref[e, j])

        def consume(e, j, s, m, acc):
            mn = jnp.maximum(m, jnp.max(s, axis=1, keepdims=True))
            return mn, jnp.exp(m - mn) * acc + _dot(jnp.exp(s - mn).astype(BF16), v_ref[e, j])

        def step(j, carry):
            out = []
            for e in range(2):
                m, acc, s = carry[3 * e:3 * e + 3]
                s_next = scores(e, j + 1)
                out += [*consume(e, j, s, m, acc), s_next]
            return tuple(out)

        init = tuple(t for e in range(2)
                     for t in (jnp.full((blk, 1), NEG, F32), jnp.zeros((blk, w), F32), scores(e, 0)))
        carry = lax.fori_loop(0, i, step, init)
        m0, a0 = consume(0, i, jnp.where(col <= row, carry[2], NEG), carry[0], carry[1])
        m1, a1 = consume(1, i, jnp.where(col <= row, carry[5], NEG), carry[3], carry[4])
        l0, l1 = a0[:, dh:dh + 1], a1[:, dh:dh + 1]
        lane = lax.broadcasted_iota(jnp.int32, (1, w), 1)
        o_ref[...] = jnp.where(lane < dh, a0 / l0, pltpu.roll(a1 / l1, dh, 1)).astype(o_ref.dtype)
        lse_ref[0] = m0 + jnp.log(l0)
        lse_ref[1] = m1 + jnp.log(l1)

    at, whole, rows = _pair_specs(blk, nb)
    return _pcall(body, name=name, grid=(b, h // 2, nb), in_specs=[at(w), whole, whole],
                  out_specs=[rows, at(1)], out_shape=[_sds((b, l, d_model), BF16), _sds((b, h, l, 1), F32)],
                  compiler_params=_cparams(("parallel", "parallel", "parallel")))(qa, kr, vr)


def _flash_bwd_kv(qa, ka, va, doa, d_model, name):
    b, h, l, w = qa.shape
    dh = ATTN_HEAD_DIM
    blk = _attn_blk(l)
    nb = l // blk
    r5 = lambda t: t.reshape(b, h, nb, blk, w)

    def body(k_ref, v_ref, q_ref, do_ref, dk_ref, dv_ref, dc_ref):
        j = pl.program_id(2)
        row = lax.broadcasted_iota(jnp.int32, (blk, blk), 0)
        col = lax.broadcasted_iota(jnp.int32, (blk, blk), 1)

        def scores(e, i):
            return _dot_nt(q_ref[e, i], k_ref[e]), _dot_nt(do_ref[e, i], v_ref[e])

        def consume(e, i, sl, dpd, masked, dk, dv):
            p = jnp.exp(sl)
            if masked:
                p = jnp.where(col <= row, p, 0.0)
            ds = p * dpd
            return dk + _dot_tn(ds.astype(BF16), q_ref[e, i]), dv + _dot_tn(p.astype(BF16), do_ref[e, i])

        def step(t, carry):
            i = nb - 1 - t
            out = []
            for e in range(2):
                dk, dv, sl, dpd = carry[4 * e:4 * e + 4]
                nxt = scores(e, i - 1)
                out += [*consume(e, i, sl, dpd, False, dk, dv), *nxt]
            return tuple(out)

        zero = jnp.zeros((blk, w), F32)
        init = tuple(t for e in range(2) for t in (zero, zero, *scores(e, nb - 1)))
        carry = lax.fori_loop(0, nb - 1 - j, step, init)
        dk0, dv0 = consume(0, j, carry[2], carry[3], True, carry[0], carry[1])
        dk1, dv1 = consume(1, j, carry[6], carry[7], True, carry[4], carry[5])
        lane = lax.broadcasted_iota(jnp.int32, (1, w), 1)
        dk_ref[...] = jnp.where(lane < dh, dk0, pltpu.roll(dk1, dh, 1)).astype(dk_ref.dtype)
        dv_ref[...] = jnp.where(lane < dh, dv0, pltpu.roll(dv1, dh, 1)).astype(dv_ref.dtype)
        for e, dk in enumerate((dk0, dk1)):
            dc_ref[e] = jnp.sum(jnp.where(lane == dh + _AUG_ONE, dk, 0.0), axis=1, keepdims=True)

    at, whole, rows = _pair_specs(blk, nb)
    return _pcall(body, name=name, grid=(b, h // 2, nb), in_specs=[at(w), at(w), whole, whole],
                  out_specs=[rows, rows, at(1)],
                  out_shape=[_sds((b, l, d_model), BF16), _sds((b, l, d_model), BF16), _sds((b, h, l, 1), F32)],
                  compiler_params=_cparams(("parallel", "parallel", "parallel")))(ka, va, r5(qa), r5(doa))


def _flash_bwd_q(qa, ka, va, doa, d_model, name):
    b, h, l, w = qa.shape
    dh = ATTN_HEAD_DIM
    blk = _attn_blk(l)
    nb = l // blk
    scale = dh ** -0.5
    kr, vr = ka.reshape(b, h, nb, blk, w), va.reshape(b, h, nb, blk, w)

    def body(q_ref, do_ref, k_ref, v_ref, dq_ref, dc_ref):
        i = pl.program_id(2)
        row = lax.broadcasted_iota(jnp.int32, (blk, blk), 0)
        col = lax.broadcasted_iota(jnp.int32, (blk, blk), 1)

        def scores(e, j):
            return _dot_nt(q_ref[e], k_ref[e, j]), _dot_nt(do_ref[e], v_ref[e, j])

        def consume(e, j, sl, dpd, masked, dq):
            p = jnp.exp(sl)
            if masked:
                p = jnp.where(col <= row, p, 0.0)
            return dq + _dot((p * dpd).astype(BF16), k_ref[e, j])

        def step(j, carry):
            out = []
            for e in range(2):
                dq, sl, dpd = carry[3 * e:3 * e + 3]
                nxt = scores(e, j + 1)
                out += [consume(e, j, sl, dpd, False, dq), *nxt]
            return tuple(out)

        init = tuple(t for e in range(2) for t in (jnp.zeros((blk, w), F32), *scores(e, 0)))
        carry = lax.fori_loop(0, i, step, init)
        dq0 = consume(0, i, carry[1], carry[2], True, carry[0])
        dq1 = consume(1, i, carry[4], carry[5], True, carry[3])
        lane = lax.broadcasted_iota(jnp.int32, (1, w), 1)
        dq_ref[...] = (jnp.where(lane < dh, dq0, pltpu.roll(dq1, dh, 1)) * scale).astype(dq_ref.dtype)
        for e, dq in enumerate((dq0, dq1)):
            dc_ref[e] = jnp.sum(jnp.where(lane == dh + _AUG_C, dq, 0.0), axis=1, keepdims=True)

    at, whole, rows = _pair_specs(blk, nb)
    return _pcall(body, name=name, grid=(b, h // 2, nb), in_specs=[at(w), at(w), whole, whole],
                  out_specs=[rows, at(1)], out_shape=[_sds((b, l, d_model), BF16), _sds((b, h, l, 1), F32)],
                  compiler_params=_cparams(("parallel", "parallel", "parallel")))(qa, doa, kr, vr)


def _ssd_dims(d_ssd):
    heads = d_ssd // SSD_HEAD_DIM
    return heads, heads // SSD_GROUPS, d_ssd // SSD_GROUPS


def _ssd_specs(l, ds, seq_map):
    q = SSD_CHUNK
    gn = SSD_GROUPS * SSD_STATE
    row3 = lambda w, cb: pl.BlockSpec((None, q, w), lambda i, c, cb=cb: (i, seq_map(c), cb))
    return dict(
        xs=row3(ds, 0), bm=row3(gn, ds // gn), cm=row3(gn, ds // gn + 1), z=row3(ds, 0), dt=row3(SMALL_W, 0),
        da=row3(SMALL_W, 0), dat=pl.BlockSpec((None, SMALL_W, q), lambda i, c: (i, 0, seq_map(c))),
        e=pl.BlockSpec((SMALL_W, ds), lambda i, c: (0, 0)), et=pl.BlockSpec((ds, SMALL_W), lambda i, c: (0, 0)),
        vec=pl.BlockSpec((1, ds), lambda i, c: (0, 0)), vec128=pl.BlockSpec((1, SMALL_W), lambda i, c: (0, 0)),
        hin=pl.BlockSpec((None, None, SSD_STATE, ds), lambda i, c: (i, seq_map(c), 0, 0)))


def _ssd_common(da, dat, dt, e_mat, xs):
    q = SSD_CHUNK
    row = lax.broadcasted_iota(jnp.int32, (q, q), 0)
    col = lax.broadcasted_iota(jnp.int32, (q, q), 1)
    lower = row >= col
    cs = _dot(lower.astype(F32), da, hi=True)
    cst = _dot(dat, (row <= col).astype(F32), hi=True)
    dtx = _dot(dt, e_mat, hi=True)
    csx = _dot(cs, e_mat, hi=True)
    rowx = lax.broadcasted_iota(jnp.int32, csx.shape, 0)
    totx = jnp.sum(jnp.where(rowx == q - 1, csx, 0.0), axis=0, keepdims=True)
    xf = xs.astype(F32)
    return lower, cs, cst, dtx, csx, totx, xf, xf * dtx


def _ssd_fwd(xbc, z, dt, da, dat, e_mat, dx, nw, name):
    b, l, _ = xbc.shape
    ds = z.shape[2]
    heads, hpg, gw = _ssd_dims(ds)
    q, n = SSD_CHUNK, SSD_STATE
    nc = l // q
    hcol0 = ATTN_HEADS

    def body(xs_ref, bm_ref, cm_ref, z_ref, dt_ref, da_ref, dat_ref, e_ref, dx_ref, nw_ref, y_ref, yraw_ref, hin_ref,
             hst, ydiag):
        c = pl.program_id(1)

        @pl.when(c == 0)
        def _():
            hst[...] = jnp.zeros_like(hst)

        hin = hst[...]
        hin_ref[...] = hin
        lower, cs, cst, dtx, csx, totx, xf, xdt = _ssd_common(da_ref[...], dat_ref[...], dt_ref[...], e_ref[...],
                                                               xs_ref[...])
        bm, cm = bm_ref[...], cm_ref[...]
        dec_end = jnp.exp(totx - csx)
        for g in range(SSD_GROUPS):
            gs = slice(g * gw, (g + 1) * gw)
            bg, cg = bm[:, g * n:(g + 1) * n], cm[:, g * n:(g + 1) * n]
            cb = _dot_nt(cg, bg)
            for e in range(hpg):
                hh = g * hpg + e
                cc = hcol0 + hh
                lm = jnp.exp(jnp.where(lower, cs[:, cc:cc + 1] - cst[cc:cc + 1, :], NEG))
                hs = slice(hh * SSD_HEAD_DIM, (hh + 1) * SSD_HEAD_DIM)
                ydiag[:, hs] = _dot((cb * lm).astype(BF16), xdt[:, hs].astype(BF16))
            sg = _dot_tn(bg, (xdt[:, gs] * dec_end[:, gs]).astype(BF16))
            hst[:, gs] = jnp.exp(totx[:, gs]) * hin[:, gs] + sg
            ydiag[:, gs] += _dot(cg, hin[:, gs].astype(BF16)) * jnp.exp(csx[:, gs])
        yraw = ydiag[...] + dx_ref[...] * xf
        yraw_ref[...] = yraw.astype(yraw_ref.dtype)
        yg = yraw * _silu(z_ref[...].astype(F32))
        nwv = nw_ref[...]
        for g in range(SSD_GROUPS):
            gs = slice(g * gw, (g + 1) * gw)
            r = lax.rsqrt(jnp.mean(yg[:, gs] * yg[:, gs], axis=1, keepdims=True) + NORM_EPS)
            y_ref[:, gs] = (yg[:, gs] * r * nwv[:, gs]).astype(y_ref.dtype)

    sp = _ssd_specs(l, ds, lambda c: c)
    return _pcall(body, name=name, grid=(b, nc),
                  in_specs=[sp['xs'], sp['bm'], sp['cm'], sp['z'], sp['dt'], sp['da'], sp['dat'], sp['e'], sp['vec'],
                            sp['vec']],
                  out_specs=[sp['z'], sp['z'], sp['hin']],
                  out_shape=[_sds((b, l, ds), BF16), _sds((b, l, ds), BF16), _sds((b, nc, n, ds), F32)],
                  scratch_shapes=[pltpu.VMEM((n, ds), F32), pltpu.VMEM((q, ds), F32)],
                  compiler_params=_cparams(("parallel", "arbitrary")))(xbc, xbc, xbc, z, dt, da, dat, e_mat, dx, nw)


def _ssd_bwd(xbc, z, dt, da, dat, e_mat, et_mat, dx, nw, a128, yraw, hin, dy, name):
    b, l, dxw = xbc.shape
    ds = z.shape[2]
    heads, hpg, gw = _ssd_dims(ds)
    q, n = SSD_CHUNK, SSD_STATE
    gn = SSD_GROUPS * n
    nc = l // q
    hcol0 = ATTN_HEADS

    def body(xs_ref, bm_ref, cm_ref, z_ref, dt_ref, da_ref, dat_ref, e_ref, et_ref, dx_ref, nw_ref, a_ref, yraw_ref,
             hin_ref, dy_ref, dxs_ref, dbm_ref, dcm_ref, dz_ref, ddt_ref, dd_ref, dnw_ref, dap_ref, dhs, dxdt, dcsx,
             dtotx):
        i, c = pl.program_id(0), pl.program_id(1)

        @pl.when(c == 0)
        def _():
            dhs[...] = jnp.zeros_like(dhs)

        @pl.when((i == 0) & (c == 0))
        def _():
            dd_ref[...] = jnp.zeros_like(dd_ref)
            dnw_ref[...] = jnp.zeros_like(dnw_ref)
            dap_ref[...] = jnp.zeros_like(dap_ref)

        dtv = dt_ref[...]
        lower, cs, cst, dtx, csx, totx, xf, xdt = _ssd_common(da_ref[...], dat_ref[...], dtv, e_ref[...], xs_ref[...])
        upper = jnp.logical_not(lower) | (lax.broadcasted_iota(jnp.int32, (q, q), 0)
                                          == lax.broadcasted_iota(jnp.int32, (q, q), 1))
        bm, cm = bm_ref[...], cm_ref[...]
        ecs, dec_end, etot = jnp.exp(csx), jnp.exp(totx - csx), jnp.exp(totx)
        yraw = yraw_ref[...].astype(F32)
        zv = z_ref[...].astype(F32)
        sz = _silu(zv)
        yg = yraw * sz
        dyn_ = dy_ref[...].astype(F32)
        nwv = nw_ref[...]
        dygs, dnws = [], []
        for g in range(SSD_GROUPS):
            gs = slice(g * gw, (g + 1) * gw)
            r = lax.rsqrt(jnp.mean(yg[:, gs] * yg[:, gs], axis=1, keepdims=True) + NORM_EPS)
            yn = yg[:, gs] * r
            dn = dyn_[:, gs] * nwv[:, gs]
            dnws.append(jnp.sum(dyn_[:, gs] * yn, axis=0, keepdims=True))
            dygs.append(r * (dn - yn * jnp.mean(dn * yn, axis=1, keepdims=True)))
        dyg = jnp.concatenate(dygs, axis=1)
        dnw_ref[...] += jnp.concatenate(dnws, axis=1)
        dz_ref[...] = (dyg * yraw * _dsilu(zv)).astype(dz_ref.dtype)
        dyv = dyg * sz
        dd_ref[...] += jnp.sum(dyv * xf, axis=0, keepdims=True)
        hin, dh = hin_ref[...], dhs[...]
        lane128 = lax.broadcasted_iota(jnp.int32, (1, SMALL_W), 1)
        dcs = jnp.zeros((q, SMALL_W), F32)
        for g in range(SSD_GROUPS):
            gs = slice(g * gw, (g + 1) * gw)
            bg, cg = bm[:, g * n:(g + 1) * n], cm[:, g * n:(g + 1) * n]
            hg, dhg = hin[:, gs], dh[:, gs]
            hgb, dsb = hg.astype(BF16), dhg.astype(BF16)
            yoff = _dot(cg, hgb) * ecs[:, gs]
            dch = (dyv[:, gs] * ecs[:, gs]).astype(BF16)
            dcg = _dot_nt(dch, hgb)
            dhs[:, gs] = _dot_tn(cg, dch) + etot[:, gs] * dhg
            zg = xdt[:, gs] * dec_end[:, gs]
            dzz = _dot(bg, dsb)
            dbg = _dot_nt(zg.astype(BF16), dsb)
            dxdt_g = dzz * dec_end[:, gs]
            w_end = dzz * zg
            dtotx[:, gs] = jnp.sum(dhg * hg, axis=0, keepdims=True) * etot[:, gs] + jnp.sum(w_end, axis=0, keepdims=True)
            dcsx[:, gs] = dyv[:, gs] * yoff - w_end
            cb, cbt = _dot_nt(cg, bg), _dot_nt(bg, cg)
            dgm = jnp.zeros((q, q), F32)
            for e in range(hpg):
                hh = g * hpg + e
                cc = hcol0 + hh
                ccol, crow = cs[:, cc:cc + 1], cst[cc:cc + 1, :]
                lm = jnp.exp(jnp.where(lower, ccol - crow, NEG))
                lmt = jnp.exp(jnp.where(upper, crow - ccol, NEG))
                mm, mt = cb * lm, cbt * lmt
                hs = slice(hh * SSD_HEAD_DIM, (hh + 1) * SSD_HEAD_DIM)
                dye, xe = dyv[:, hs].astype(BF16), xdt[:, hs].astype(BF16)
                dm, dmt = _dot_nt(dye, xe), _dot_nt(xe, dye)
                dxdt[:, hs] = dxdt_g[:, e * SSD_HEAD_DIM:(e + 1) * SSD_HEAD_DIM] + _dot(mt.astype(BF16), dye)
                dgm = dgm + dm * lm
                rs = jnp.sum(dm * mm, axis=1, keepdims=True) - jnp.sum(dmt * mt, axis=1, keepdims=True)
                dcs = dcs + rs * (lane128 == cc).astype(F32)
            dgb = dgm.astype(BF16)
            dcm_ref[:, g * n:(g + 1) * n] = (dcg + _dot(dgb, bg)).astype(dcm_ref.dtype)
            dbm_ref[:, g * n:(g + 1) * n] = (dbg + _dot_tn(dgb, cg)).astype(dbm_ref.dtype)
        dxd = dxdt[...]
        dxs_ref[...] = (dx_ref[...] * dyv + dxd * dtx).astype(dxs_ref.dtype)
        et = et_ref[...]
        ddt = _dot(dxd * xf, et, hi=True)
        dtot128 = _dot(jnp.broadcast_to(dtotx[...], (8, ds)), et, hi=True)[0:1, :]
        row128 = lax.broadcasted_iota(jnp.int32, (q, SMALL_W), 0)
        dcs = dcs + _dot(dcsx[...], et, hi=True) + jnp.where(row128 == q - 1, dtot128, 0.0)
        dda = _dot(upper.astype(F32), dcs, hi=True)
        ddt_ref[...] = ddt + dda * a_ref[...]
        dap_ref[...] += jnp.sum(dda * dtv, axis=0, keepdims=True)

    rev = lambda c: nc - 1 - c
    sp = _ssd_specs(l, ds, rev)
    row3 = lambda w: pl.BlockSpec((None, q, w), lambda i, c: (i, rev(c), 0))
    acc = lambda w: pl.BlockSpec((1, w), lambda i, c: (0, 0))
    return _pcall(body, name=name, grid=(b, nc),
                  in_specs=[sp['xs'], sp['bm'], sp['cm'], sp['z'], sp['dt'], sp['da'], sp['dat'], sp['e'], sp['et'],
                            sp['vec'], sp['vec'], sp['vec128'], sp['z'], sp['hin'], sp['z']],
                  out_specs=[row3(ds), row3(gn), row3(gn), row3(ds), row3(SMALL_W), acc(ds), acc(ds), acc(SMALL_W)],
                  out_shape=[_sds((b, l, ds), BF16), _sds((b, l, gn), BF16), _sds((b, l, gn), BF16), _sds((b, l, ds), BF16),
                             _sds((b, l, SMALL_W), F32), _sds((1, ds), F32), _sds((1, ds), F32), _sds((1, SMALL_W), F32)],
                  scratch_shapes=[pltpu.VMEM((n, ds), F32), pltpu.VMEM((q, ds), F32), pltpu.VMEM((q, ds), F32),
                                  pltpu.VMEM((1, ds), F32)],
                  compiler_params=_cparams(("arbitrary", "arbitrary")))(
                      xbc, xbc, xbc, z, dt, da, dat, e_mat, et_mat, dx, nw, a128, yraw, hin, dy)


_GROUP_SIZE = {'c': 2, 'xy': 4, 'xyc': 8}
_LOCAL_SPLIT = 16


def _exchange(src, group, scatter, name, nsplit=1, copy_own=True):
    n = _GROUP_SIZE[group]
    rows, width = src.shape[-2:]
    assert src.ndim == (3 if scatter else 2)
    while rows % (8 * nsplit):
        nsplit //= 2
    crow = rows // nsplit
    nlocal = _LOCAL_SPLIT
    while rows % (8 * nlocal):
        nlocal //= 2
    lrow = rows // nlocal

    def body(src_ref, out_ref, send_sems, recv_sems, local_sems):
        x, y, c = lax.axis_index("x"), lax.axis_index("y"), lax.axis_index("c")
        if group == 'c':
            rank = c
            dev = lambda r: (x, y, r)
        elif group == 'xy':
            rank = 2 * x + y
            dev = lambda r: (r // 2, r % 2, c)
        else:
            rank = 4 * x + 2 * y + c
            dev = lambda r: (r // 4, (r // 2) % 2, r % 2)

        def mine_for(r, ck):
            piece = src_ref.at[r] if scatter else src_ref
            return piece.at[pl.ds(ck * crow, crow)]

        def copy(k, ck, pr, dst_rank):
            return pltpu.make_async_remote_copy(
                src_ref=mine_for(pr, ck), dst_ref=out_ref.at[dst_rank].at[pl.ds(ck * crow, crow)],
                send_sem=send_sems.at[k * nsplit + ck], recv_sem=recv_sems.at[k * nsplit + ck], device_id=dev(pr),
                device_id_type=pl.DeviceIdType.MESH)

        locals_ = []
        if copy_own:
            own = src_ref.at[rank] if scatter else src_ref
            for ck in range(nlocal):
                rs = pl.ds(ck * lrow, lrow)
                locals_.append(pltpu.make_async_copy(own.at[rs], out_ref.at[rank].at[rs], local_sems.at[ck]))
                locals_[-1].start()
        peers = [jnp.bitwise_xor(rank, k + 1) for k in range(n - 1)]
        sends = [copy(k, ck, pr, rank) for ck in range(nsplit) for k, pr in enumerate(peers)]
        for cp in sends:
            cp.start()
        for ck in range(nsplit):
            for k, pr in enumerate(peers):
                copy(k, ck, pr, pr).wait_recv()
        for cp in sends:
            cp.wait_send()
        for cp in locals_:
            cp.wait()

    return _pcall(body, name=name, in_specs=[pl.BlockSpec(memory_space=pl.ANY)],
                  out_specs=pl.BlockSpec(memory_space=pl.ANY), out_shape=_sds((n, rows, width), src.dtype),
                  scratch_shapes=[pltpu.SemaphoreType.DMA(((n - 1) * nsplit,)),
                                  pltpu.SemaphoreType.DMA(((n - 1) * nsplit,)),
                                  pltpu.SemaphoreType.DMA((nlocal,))])(src)


def _dims():
    d = D_MODEL
    h = ATTN_HEADS
    d_ssd = d
    d_xbc = d_ssd + 2 * SSD_GROUPS * SSD_STATE
    sizes = (d, d, d, h, d_ssd, d_xbc, d_ssd // SSD_HEAD_DIM, d, d, 3 * d)
    return d, h, d_ssd, d_xbc, sizes


def _w_in_split(w):
    d, h, d_ssd, d_xbc, sizes = _dims()
    off = np.concatenate([[0], np.cumsum(sizes)])
    seg = lambda i: w[..., off[i]:off[i + 1]]
    main = jnp.concatenate([seg(0), seg(1), seg(2), seg(4), seg(5), seg(7), seg(8), seg(9)], axis=-1)
    pad = jnp.zeros(w.shape[:-1] + (SMALL_W - sizes[3] - sizes[6],), w.dtype)
    small = jnp.concatenate([seg(3), seg(6), pad], axis=-1)
    return main, small


def _w_in_merge(main, small):
    d, h, d_ssd, d_xbc, sizes = _dims()
    order = (0, 1, 2, 4, 5, 7, 8, 9)
    moff = np.concatenate([[0], np.cumsum([sizes[i] for i in order])])
    pieces = {i: main[..., moff[j]:moff[j + 1]] for j, i in enumerate(order)}
    pieces[3] = small[..., :sizes[3]]
    pieces[6] = small[..., sizes[3]:sizes[3] + sizes[6]]
    return jnp.concatenate([pieces[i] for i in range(10)], axis=-1)


def _main_offsets():
    d, h, d_ssd, d_xbc, sizes = _dims()
    names = ('q', 'k', 'v', 'z', 'xbc', 'xr', 'gate', 'merge')
    widths = (d, d, d, d_ssd, d_xbc, d, d, 3 * d)
    off = np.concatenate([[0], np.cumsum(widths)])
    return {nm: (int(off[i]), int(off[i + 1])) for i, nm in enumerate(names)}


def _block_diag(w):
    nb, s, _ = w.shape
    eye = jnp.eye(nb, dtype=w.dtype)
    return (eye[:, None, :, None] * w[:, :, None, :]).reshape(nb * s, nb * s)


def _diag_blocks(wd, nb):
    s = wd.shape[0] // nb
    return jnp.stack([wd[i * s:(i + 1) * s, i * s:(i + 1) * s] for i in range(nb)])


def _vec128(*parts):
    v = jnp.concatenate([p.astype(F32) for p in parts])
    return jnp.pad(v, (0, SMALL_W - v.shape[0]))[None, :]


def _ffn_fwd(h, gnorm, wgu, wd, tag):
    xn = _rms_fwd(h, gnorm[None, :], f"{tag}_norm")
    gu = _mm_nn(xn, wgu, BF16, name=f"{tag}_gu")
    act = _swiglu_fwd(gu, f"{tag}_act")
    out = _mm_nn(act, wd, F32, res=h, alpha=0.5, name=f"{tag}_down")
    return out, (h, xn, gu, act)


def _ffn_bwd(dout, saved, gnorm, wgu_t, wd_t, tag):
    h, xn, gu, act = saved
    dact = _mm_nn(dout, wd_t, BF16, alpha=0.5, name=f"{tag}_dact")
    dwd = _mm_tn(act, dout, alpha=0.5, name=f"{tag}_dwd")
    dgu = _swiglu_bwd(gu, dact, f"{tag}_dgu")
    dwgu = _mm_tn(xn, dgu, name=f"{tag}_dwgu")
    dxn = _mm_nn(dgu, wgu_t, F32, name=f"{tag}_dxn")
    dh, dg = _rms_bwd(h, dxn, dout, gnorm[None, :], f"{tag}_dnorm")
    return dh, dg[0], dwgu, dwd


def _mixer_fwd(h, p, b, l):
    d, nh, d_ssd, d_xbc, sizes = _dims()
    t = b * l
    off = _main_offsets()
    xn = _rms_fwd(h, p['mix_norm'][None, :], "mix_norm")
    pm = _mm_nn(xn, p['w_main'], BF16, name="mix_in_main")
    ps = _mm_nn(xn, p['w_small'], F32, name="mix_in_small")
    col = lambda nm: pm[:, off[nm][0]:off[nm][1]]
    heads_ssd = d_ssd // SSD_HEAD_DIM
    a_neg = -jnp.exp(p['ssd_a_log'])
    fb = _vec128(p['fox_forget_bias'])
    dtb = _vec128(jnp.zeros((nh,), F32), p['ssd_dt_bias'])
    a128 = _vec128(jnp.zeros((nh,), F32), a_neg)

    def prep(_, v, fbv, dtbv, av):
        lane = lax.broadcasted_iota(jnp.int32, (1, SMALL_W), 1)
        logf = jnp.where(lane < nh, -_softplus(-(v + fbv)), 0.0)
        dtv = jnp.where((lane >= nh) & (lane < nh + heads_ssd), _softplus(v + dtbv), 0.0)
        return logf, dtv, dtv * av
    logf, dt, da = _rowwise(prep, [ps], [(SMALL_W, F32)] * 3, bcast=[fb, dtb, a128], name="mix_prep")

    cum = _cumsum_seq(logf.reshape(b, l, SMALL_W), False, "fox_cumsum")
    qa, ka, va = _attn_prep(pm.reshape(b, l, -1), cum, (off['q'][0], off['k'][0], off['v'][0]), "fox_prep")
    y_a3, lse = _flash_fwd(qa, ka, va, d, "fox_fwd")
    y_a = y_a3.reshape(t, d)

    xbc = col('xbc').reshape(b, l, d_xbc)
    pre_b = _conv_fwd(xbc, p['ssd_conv_w'], p['ssd_conv_b'][None, :], BF16, "ssd_conv")
    xbc_act = _rowwise(lambda _, v: _silu(v.astype(F32)), [pre_b.reshape(t, d_xbc)], [(d_xbc, BF16)],
                       name="ssd_conv_act")[0].reshape(b, l, d_xbc)
    z = col('z').reshape(b, l, d_ssd)
    dt3, da3 = dt.reshape(b, l, SMALL_W), da.reshape(b, l, SMALL_W)
    dat3 = da3.transpose(0, 2, 1)
    e_mat = _expand_matrix(nh, heads_ssd)
    dx = jnp.repeat(p['ssd_d'], SSD_HEAD_DIM)[None, :]
    nw = p['ssd_norm'][None, :]
    y_b3, yraw, hin = _ssd_fwd(xbc_act, z, dt3, da3, dat3, e_mat, dx, nw, "ssd_fwd")
    y_b = y_b3.reshape(t, d_ssd)

    xr = col('xr').reshape(b, l, d)
    xc = _conv_fwd(xr, p['lru_conv_w'], p['lru_conv_b'][None, :], F32, "lru_conv").reshape(t, d)
    pre_ri = _mm_nn(xc, p['lru_w_ri'], F32, name="lru_gates")
    lvec = (p['lru_b_a'][None, :], p['lru_b_x'][None, :], p['lru_lambda'][None, :])
    a_l, u_l = _rowwise(_lru_point_fwd, [pre_ri, xc], [(d, F32)] * 2, bcast=lvec, name="lru_point", period=l)
    hs = _linear_scan(a_l.reshape(b, l, d), u_l.reshape(b, l, d), False, "lru_scan").reshape(t, d)
    gate = col('gate')
    y_c = _rowwise(lambda _, hv, gv: hv * _gelu(gv.astype(F32)), [hs, gate], [(d, BF16)], name="lru_out")[0]

    ba = _mm_nn(y_a, p['w_branch_attn'], BF16, name="branch_attn")
    bb = _mm_nn(y_b, p['w_branch_ssd'], BF16, name="branch_ssd")
    bc = _mm_nn(y_c, p['w_branch_lru'], BF16, name="branch_lru")
    mg = col('merge')
    mixed = _merge_fwd(mg, ba, bb, bc, "merge")
    out = _mm_nn(mixed, p['w_out'], F32, res=h, name="mix_out")
    saved = dict(h=h, xn=xn, ps=ps, fb=fb, dtb=dtb, a128=a128, qa=qa, ka=ka, va=va, lse=lse,
                 xbc=xbc, pre_b=pre_b, xbc_act=xbc_act, z=z, dt3=dt3, da3=da3, dat3=dat3, e_mat=e_mat, dx=dx, nw=nw,
                 yraw=yraw, hin=hin, xr=xr, xc=xc, pre_ri=pre_ri, lvec=lvec, a_l=a_l, hs=hs, gate=gate, y_a=y_a, y_b=y_b,
                 y_c=y_c, ba=ba, bb=bb, bc=bc, mg=mg, mixed=mixed)
    return out, saved


def _expand_matrix(nh, heads_ssd):
    e = np.zeros((SMALL_W, heads_ssd * SSD_HEAD_DIM), np.float32)
    for hh in range(heads_ssd):
        e[nh + hh, hh * SSD_HEAD_DIM:(hh + 1) * SSD_HEAD_DIM] = 1.0
    return jnp.asarray(e)


def _lru_gates(pre, xc, bav, bxv, lamv, pos):
    d = xc.shape[1]
    r = _sigmoid(pre[:, :d] + bav)
    i = _sigmoid(pre[:, d:] + bxv)
    ls = -_softplus(-lamv)
    la = LRU_C * r * ls
    a = jnp.exp(la)
    mult = jnp.where(pos == 0, 1.0, jnp.sqrt(-_expm1(2.0 * la)))
    return r, i, ls, a, mult


def _lru_point_fwd(pos, pre, xc, bav, bxv, lamv):
    r, i, ls, a, mult = _lru_gates(pre, xc, bav, bxv, lamv, pos)
    return a, mult * (i * xc)


def _lru_point_bwd(pos, g, hprev, pre, xc, bav, bxv, lamv):
    r, i, ls, a, mult = _lru_gates(pre, xc, bav, bxv, lamv, pos)
    da = g * hprev
    di = g * mult * xc
    dxc = g * mult * i
    dmult = jnp.where(pos == 0, 0.0, g * i * xc)
    dla = da * a - dmult * (a * a) / mult
    dpre_r = dla * (LRU_C * ls) * r * (1.0 - r)
    dpre_i = di * i * (1.0 - i)
    dlam = jnp.sum(dla * (LRU_C * r), axis=0, keepdims=True) * _sigmoid(-lamv)
    return (jnp.concatenate([dpre_r, dpre_i], axis=1), dxc, dlam, jnp.sum(dpre_r, axis=0, keepdims=True),
            jnp.sum(dpre_i, axis=0, keepdims=True))


def _mixer_bwd(dout, s, p, b, l):
    d, nh, d_ssd, d_xbc, sizes = _dims()
    t = b * l
    heads_ssd = d_ssd // SSD_HEAD_DIM
    g = {}
    dmixed = _mm_nn(dout, p['w_out_t'], BF16, name="mix_out_dx")
    g['w_out'] = _mm_tn(s['mixed'], dout, name="mix_out_dw")
    dba, dbb, dbc, dmerge = _merge_bwd(s['mg'], s['ba'], s['bb'], s['bc'], dmixed, "merge_bwd")
    g['w_branch_attn'] = _mm_tn(s['y_a'], dba, name="branch_attn_dw")
    g['w_branch_ssd'] = _mm_tn(s['y_b'], dbb, name="branch_ssd_dw")
    g['w_branch_lru'] = _mm_tn(s['y_c'], dbc, name="branch_lru_dw")
    dy_a = _mm_nn(dba, p['w_branch_attn_t'], BF16, name="branch_attn_dx")
    dy_b = _mm_nn(dbb, p['w_branch_ssd_t'], BF16, name="branch_ssd_dx")
    dy_c = _mm_nn(dbc, p['w_branch_lru_t'], F32, name="branch_lru_dx")

    dgate, dhs = _rowwise(lambda _, dv, hv, gv: (dv * hv * _dgelu(gv.astype(F32)), dv * _gelu(gv.astype(F32))),
                          [dy_c, s['hs'], s['gate']], [(d, BF16), (d, F32)], name="lru_out_bwd")
    a3 = s['a_l'].reshape(b, l, d)
    a_next = jnp.concatenate([a3[:, 1:], jnp.zeros((b, 1, d), F32)], axis=1)
    gs = _linear_scan(a_next, dhs.reshape(b, l, d), True, "lru_scan_bwd").reshape(t, d)
    h3 = s['hs'].reshape(b, l, d)
    hprev = jnp.concatenate([jnp.zeros((b, 1, d), F32), h3[:, :-1]], axis=1).reshape(t, d)
    dpre_ri, dxc0, dlam, dba_, dbx_ = _rowwise(_lru_point_bwd, [gs, hprev, s['pre_ri'], s['xc']],
                                               [(2 * d, BF16), (d, F32)], bcast=s['lvec'],
                                               reds=[(1, d)] * 3, name="lru_point_bwd", period=l)
    g['lru_lambda'], g['lru_b_a'], g['lru_b_x'] = dlam[0], dba_[0], dbx_[0]
    dxc = _mm_nn(dpre_ri, p['lru_w_ri_t'], BF16, res=dxc0, name="lru_gates_dx")
    dw_ri = _mm_tn(s['xc'], dpre_ri, name="lru_gates_dw")
    g['lru_w_a'] = _diag_blocks(dw_ri[:, :d], LRU_BLOCKS)
    g['lru_w_x'] = _diag_blocks(dw_ri[:, d:], LRU_BLOCKS)
    dxr, dwl = _conv_bwd(s['xr'], dxc.reshape(b, l, d), p['lru_conv_w'], "lru_conv_bwd")
    g['lru_conv_w'], g['lru_conv_b'] = dwl[:CONV_K], dwl[CONV_K]

    et_mat = s['e_mat'].T
    dxs, dbm, dcm, dz, ddt, dd_l, dnw, dap = _ssd_bwd(s['xbc_act'], s['z'], s['dt3'], s['da3'], s['dat3'], s['e_mat'],
                                                      et_mat, s['dx'], s['nw'], s['a128'], s['yraw'], s['hin'],
                                                      dy_b.reshape(b, l, d_ssd), "ssd_bwd")
    g['ssd_d'] = dd_l.reshape(heads_ssd, SSD_HEAD_DIM).sum(axis=1)
    g['ssd_norm'] = dnw[0]
    g['ssd_a_log'] = dap[0, nh:nh + heads_ssd] * (-jnp.exp(p['ssd_a_log']))
    dxbc_act = jnp.concatenate([dxs, dbm, dcm], axis=2).reshape(t, d_xbc)
    dpre_b = _rowwise(lambda _, dv, pv: dv.astype(F32) * _dsilu(pv.astype(F32)),
                      [dxbc_act, s['pre_b'].reshape(t, d_xbc)], [(d_xbc, BF16)], name="ssd_conv_act_bwd")[0]
    dxbc, dws = _conv_bwd(s['xbc'], dpre_b.reshape(b, l, d_xbc), p['ssd_conv_w'], "ssd_conv_bwd")
    g['ssd_conv_w'], g['ssd_conv_b'] = dws[:CONV_K], dws[CONV_K]

    qa2, doa = _attn_prep_bwd(dy_a.reshape(b, l, d), s['y_a'].reshape(b, l, d), s['qa'], s['lse'], "fox_prep_bwd")
    dk3, dv3, dck = _flash_bwd_kv(qa2, s['ka'], s['va'], doa, d, "fox_bwd_kv")
    dq3, dcq = _flash_bwd_q(qa2, s['ka'], s['va'], doa, d, "fox_bwd_q")
    dcum = jnp.pad((dcq - dck)[..., 0].transpose(0, 2, 1), ((0, 0), (0, 0), (0, SMALL_W - nh)))
    dlogf = _cumsum_seq(dcum, True, "fox_cumsum_bwd").reshape(t, SMALL_W)

    def prep_bwd(_, v, dlf, ddtv, fbv, dtbv):
        a_ = dlf * _sigmoid(-(v + fbv))
        b_ = ddtv * _sigmoid(v + dtbv)
        return a_ + b_, jnp.sum(a_, axis=0, keepdims=True), jnp.sum(b_, axis=0, keepdims=True)
    dps, dfb, ddtb = _rowwise(prep_bwd, [s['ps'], dlogf, ddt.reshape(t, SMALL_W)], [(SMALL_W, F32)],
                              bcast=[s['fb'], s['dtb']], reds=[(1, SMALL_W)] * 2, name="mix_prep_bwd")
    g['fox_forget_bias'] = dfb[0, :nh]
    g['ssd_dt_bias'] = ddtb[0, nh:nh + heads_ssd]

    dpm = jnp.concatenate([dq3.reshape(t, d), dk3.reshape(t, d), dv3.reshape(t, d),
                           dz.reshape(t, d_ssd), dxbc.reshape(t, d_xbc), dxr.reshape(t, d), dgate, dmerge], axis=1)
    dxn = _mm_nn(dps, p['w_small_t'], F32, name="mix_in_small_dx")
    dxn = _mm_nn(dpm, p['w_main_t'], F32, res=dxn, name="mix_in_main_dx")
    g['w_main'] = _mm_tn(s['xn'], dpm, name="mix_in_main_dw")
    g['w_small'] = _mm_tn(s['xn'], dps, name="mix_in_small_dw")
    dh, dg = _rms_bwd(s['h'], dxn, dout, p['mix_norm'][None, :], "mix_norm_bwd")
    g['mix_norm'] = dg[0]
    return dh, g


def _layer_params(w, li):
    p = {n: w[n][li] for n in WEIGHTS if n not in ('meta_tokens', 'final_norm')}
    bf = lambda a: a.astype(BF16)
    p['ffn1_gu'], p['ffn1_gu_t'] = bf(p['ffn1_w_gate_up']), bf(p['ffn1_w_gate_up']).T
    p['ffn1_d'], p['ffn1_d_t'] = bf(p['ffn1_w_down']), bf(p['ffn1_w_down']).T
    p['ffn2_gu'], p['ffn2_gu_t'] = bf(p['ffn2_w_gate_up']), bf(p['ffn2_w_gate_up']).T
    p['ffn2_d'], p['ffn2_d_t'] = bf(p['ffn2_w_down']), bf(p['ffn2_w_down']).T
    wm, ws = _w_in_split(bf(p['w_in']))
    p['w_main'], p['w_main_t'], p['w_small'], p['w_small_t'] = wm, wm.T, ws, ws.T
    for n in ('w_branch_attn', 'w_branch_ssd', 'w_branch_lru', 'w_out'):
        p[n + '_t'] = bf(p[n]).T
        p[n] = bf(p[n])
    wri = jnp.concatenate([_block_diag(p['lru_w_a']), _block_diag(p['lru_w_x'])], axis=1)
    p['lru_w_ri'], p['lru_w_ri_t'] = bf(wri), bf(wri).T
    return p


def _local_step(x, loss_target, w):
    b, seq, d = x.shape
    length = N_META + seq
    l = -(-length // Q_BLOCK) * Q_BLOCK
    t = b * l
    meta = jnp.broadcast_to(w['meta_tokens'].astype(F32)[None], (b, N_META, d))
    h = jnp.concatenate([meta, x, jnp.zeros((b, l - length, d), F32)], axis=1).reshape(t, d)
    tgt = jnp.concatenate([jnp.zeros((b, N_META, d), F32), loss_target, jnp.zeros((b, l - length, d), F32)],
                          axis=1).reshape(t, d)
    params, saves = [], []
    for li in range(DEPTH):
        p = _layer_params(w, li)
        h, s1 = _ffn_fwd(h, p['ffn1_norm'], p['ffn1_gu'], p['ffn1_d'], "ffn1")
        h, sm = _mixer_fwd(h, p, b, l)
        h, s2 = _ffn_fwd(h, p['ffn2_norm'], p['ffn2_gu'], p['ffn2_d'], "ffn2")
        params.append(p)
        saves.append((s1, sm, s2))
    dh, loss, dgf = _loss_head(h, tgt, w['final_norm'][None, :], l, "loss_head")
    layer_grads = [None] * DEPTH
    for li in reversed(range(DEPTH)):
        p = params[li]
        s1, sm, s2 = saves[li]
        g = {}
        dh, g['ffn2_norm'], g['ffn2_w_gate_up'], g['ffn2_w_down'] = _ffn_bwd(dh, s2, p['ffn2_norm'], p['ffn2_gu_t'],
                                                                              p['ffn2_d_t'], "ffn2b")
        dh, gm = _mixer_bwd(dh, sm, p, b, l)
        g.update(gm)
        g['w_in'] = _w_in_merge(g.pop('w_main'), g.pop('w_small'))
        dh, g['ffn1_norm'], g['ffn1_w_gate_up'], g['ffn1_w_down'] = _ffn_bwd(dh, s1, p['ffn1_norm'], p['ffn1_gu_t'],
                                                                              p['ffn1_d_t'], "ffn1b")
        layer_grads[li] = g
    grads = {n: jnp.stack([layer_grads[li][n] for li in range(DEPTH)]) for n in layer_grads[0]}
    for n in ('lru_w_a', 'lru_w_x'):
        grads[n] = grads[n].reshape(w[n].shape)
    dh3 = dh.reshape(b, l, d)
    grads['meta_tokens'] = jnp.sum(dh3[:, :N_META], axis=0)
    grads['final_norm'] = dgf[0]
    return loss, dh3[:, N_META:N_META + seq], grads


_ROW_W = 1024
_D2D_SPLIT = 16


def _half_size(total, row_quant):
    q = _ROW_W * row_quant
    return -(-(-(-total // 2)) // q) * q


def _flat_shard(parts, half):
    v = jnp.concatenate([p.reshape(-1) for p in parts])
    return jnp.pad(v, (0, 2 * half - v.shape[0]))


def _chip_shards(full, axis, nchip):
    return jnp.split(full, nchip, axis=axis + 1)


def _unflatten(flat, shapes):
    out, o = [], 0
    for sh in shapes:
        n = int(np.prod(sh))
        out.append(flat[o:o + n].reshape(sh))
        o += n
    return out


def kernel(x, meta_tokens, ffn1_norm, ffn1_w_gate_up, ffn1_w_down, mix_norm, w_in, fox_forget_bias, ssd_conv_w, ssd_conv_b, ssd_dt_bias, ssd_a_log, ssd_d, ssd_norm, lru_conv_w, lru_conv_b, lru_w_a, lru_b_a, lru_w_x, lru_b_x, lru_lambda, w_branch_attn, w_branch_ssd, w_branch_lru, w_out, ffn2_norm, ffn2_w_gate_up, ffn2_w_down, final_norm, loss_target, m_meta_tokens, m_ffn1_norm, m_ffn1_w_gate_up, m_ffn1_w_down, m_mix_norm, m_w_in, m_fox_forget_bias, m_ssd_conv_w, m_ssd_conv_b, m_ssd_dt_bias, m_ssd_a_log, m_ssd_d, m_ssd_norm, m_lru_conv_w, m_lru_conv_b, m_lru_w_a, m_lru_b_a, m_lru_w_x, m_lru_b_x, m_lru_lambda, m_w_branch_attn, m_w_branch_ssd, m_w_branch_lru, m_w_out, m_ffn2_norm, m_ffn2_w_gate_up, m_ffn2_w_down, m_final_norm, v_meta_tokens, v_ffn1_norm, v_ffn1_w_gate_up, v_ffn1_w_down, v_mix_norm, v_w_in, v_fox_forget_bias, v_ssd_conv_w, v_ssd_conv_b, v_ssd_dt_bias, v_ssd_a_log, v_ssd_d, v_ssd_norm, v_lru_conv_w, v_lru_conv_b, v_lru_w_a, v_lru_b_a, v_lru_w_x, v_lru_b_x, v_lru_lambda, v_w_branch_attn, v_w_branch_ssd, v_w_branch_lru, v_w_out, v_ffn2_norm, v_ffn2_w_gate_up, v_ffn2_w_down, v_final_norm):
    args = locals()
    wloc = {n: args[n] for n in WEIGHTS}
    mloc = {n: args['m_' + n] for n in WEIGHTS}
    vloc = {n: args['v_' + n] for n in WEIGHTS}
    nchip = 4
    chip = 2 * lax.axis_index("x") + lax.axis_index("y")
    core = lax.axis_index("c")
    row_quant = 256 if D_MODEL >= 1024 else 8

    big_shapes = [wloc[n].shape for n in BIG_NAMES]
    total = sum(int(np.prod(s)) for s in big_shapes)
    half = _half_size(total, row_quant)
    hrows = half // _ROW_W
    wflat = _flat_shard([wloc[n].astype(BF16) for n in BIG_NAMES], half).reshape(2, hrows, _ROW_W)
    my_half = lax.dynamic_index_in_dim(wflat, core, axis=0, keepdims=False)
    own = lambda out, mine, rank: lax.dynamic_update_index_in_dim(out, mine, rank, 0)
    got = _exchange(my_half, 'xy', False, "gather_w_chips", copy_own=False)
    got = own(got, my_half, chip).reshape(nchip * hrows, _ROW_W)
    both = own(_exchange(got, 'c', False, "gather_w_cores", nsplit=_D2D_SPLIT, copy_own=False), got, core)
    wall = both.reshape(2, nchip, hrows, _ROW_W).transpose(1, 0, 2, 3).reshape(nchip, 2 * half)
    full = {}
    per_chip = [_unflatten(wall[j], big_shapes) for j in range(nchip)]
    for i, n in enumerate(BIG_NAMES):
        full[n] = jnp.concatenate([per_chip[j][i] for j in range(nchip)], axis=BIG[n] + 1)
    cs_shapes = [wloc[n].shape for n in COLSHARD_SMALL]
    cs_total = sum(int(np.prod(s)) for s in cs_shapes)
    cs_rows = -(-cs_total // (8 * 128)) * 8
    cs_flat = jnp.concatenate([wloc[n].reshape(-1) for n in COLSHARD_SMALL])
    cs_flat = jnp.pad(cs_flat, (0, cs_rows * 128 - cs_total)).reshape(cs_rows, 128)
    cs_all = _exchange(cs_flat, 'xy', False, "gather_small").reshape(nchip, -1)
    cs_chip = [_unflatten(cs_all[j], cs_shapes) for j in range(nchip)]
    for i, n in enumerate(COLSHARD_SMALL):
        full[n] = jnp.concatenate([cs_chip[j][i] for j in range(nchip)], axis=-1)
    for n in SMALL_NAMES:
        if n not in COLSHARD_SMALL:
            full[n] = wloc[n]

    loss_part, grad_x, grads = _local_step(x, loss_target, full)

    gsh = [[s for s in _chip_shards(grads[n], BIG[n], nchip)] for n in BIG_NAMES]
    gflat = jnp.stack([_flat_shard([gsh[i][j] for i in range(len(BIG_NAMES))], half) for j in range(nchip)])
    gflat = gflat.reshape(nchip, 2, hrows, _ROW_W).transpose(1, 0, 2, 3).reshape(2, nchip * hrows, _ROW_W)
    give = lax.dynamic_index_in_dim(gflat, 1 - core, axis=0, keepdims=False)
    keep = lax.dynamic_index_in_dim(gflat, core, axis=0, keepdims=False)
    pair = _exchange(give, 'c', False, "reduce_cores", nsplit=_D2D_SPLIT, copy_own=False)
    theirs = lax.dynamic_index_in_dim(pair, 1 - core, axis=0, keepdims=False)
    psum = _sum_rows([keep, theirs], BF16, "reduce_cores_sum").reshape(nchip, hrows, _ROW_W)
    parts = _exchange(psum, 'xy', True, "reduce_chips", copy_own=False)
    parts = own(parts, lax.dynamic_index_in_dim(psum, chip, axis=0, keepdims=False), chip)
    rsum = _sum_rows([parts[j] for j in range(nchip)], F32, "reduce_chips_sum")
    halves = own(_exchange(rsum, 'c', False, "reduce_share", nsplit=_D2D_SPLIT, copy_own=False), rsum, core)
    gshard = halves.reshape(-1)
    gbig = dict(zip(BIG_NAMES, _unflatten(gshard, big_shapes)))

    sm_shapes = [grads[n].shape for n in SMALL_NAMES]
    sm_total = sum(int(np.prod(s)) for s in sm_shapes) + 128
    sm_rows = -(-sm_total // (8 * 128)) * 8
    sm_flat = jnp.concatenate([loss_part.reshape(-1)] + [grads[n].reshape(-1) for n in SMALL_NAMES])
    sm_flat = jnp.pad(sm_flat, (0, sm_rows * 128 - sm_total)).reshape(sm_rows, 128)
    sm_all = _exchange(sm_flat, 'xyc', False, "gather_small_grads")
    sm_sum = _sum_rows([sm_all[j] for j in range(8)], F32, "small_grads_sum").reshape(-1)
    loss = sm_sum[0]
    gsmall_full = dict(zip(SMALL_NAMES, _unflatten(sm_sum[128:], sm_shapes)))
    gsmall = {}
    for n in SMALL_NAMES:
        gfull = gsmall_full[n]
        if n in COLSHARD_SMALL:
            wcols = wloc[n].shape[-1]
            gfull = lax.dynamic_slice_in_dim(gfull, chip * wcols, wcols, axis=gfull.ndim - 1)
        gsmall[n] = gfull

    def flat_big(dct):
        return _flat_shard([dct[n] for n in BIG_NAMES], half).reshape(2 * hrows, _ROW_W)
    dl, mn, vn = _adamw(flat_big(wloc), gshard.reshape(2 * hrows, _ROW_W), flat_big(mloc), flat_big(vloc), "adamw_big")
    big_out = [dict(zip(BIG_NAMES, _unflatten(a.reshape(-1), big_shapes))) for a in (dl, mn, vn)]
    loc_shapes = [wloc[n].shape for n in SMALL_NAMES]
    loc_total = sum(int(np.prod(s)) for s in loc_shapes)
    loc_rows = -(-loc_total // (8 * 128)) * 8

    def flat_small(dct):
        v = jnp.concatenate([dct[n].reshape(-1) for n in SMALL_NAMES])
        return jnp.pad(v, (0, loc_rows * 128 - loc_total)).reshape(loc_rows, 128)
    dls, mns, vns = _adamw(flat_small(wloc), flat_small(gsmall), flat_small(mloc), flat_small(vloc), "adamw_small")
    small_out = [dict(zip(SMALL_NAMES, _unflatten(a.reshape(-1), loc_shapes))) for a in (dls, mns, vns)]

    grad_w = {**gbig, **gsmall}
    outs = [loss, grad_x] + [grad_w[n] for n in WEIGHTS]
    for k in range(3):
        merged = {**big_out[k], **small_out[k]}
        outs += [merged[n] for n in WEIGHTS]
    return tuple(outs)
```

```python
import functools
import math

import numpy as np
import jax
import jax.numpy as jnp
from jax import lax
from jax.experimental import pallas as pl
from jax.experimental.pallas import tpu as pltpu

F32 = jnp.float32
BF16 = jnp.bfloat16
HI = lax.Precision.HIGHEST
VMEM_LIMIT_BYTES = 56 * 1024 * 1024
NEG = -1e30

D_MODEL = 1024
SEQ = 4096
DEPTH = 4
N_META = 16
Q_BLOCK = 128
SSD_CHUNK = 128
NORM_EPS = 1e-6
ATTN_HEADS = 16
ATTN_HEAD_DIM = 64
SSD_HEAD_DIM = 64
SSD_GROUPS = 2
SSD_STATE = 128
CONV_K = 4
LRU_BLOCKS = 16
LRU_C = 8.0
D_FF = 2816
ADAM_LR = 0.001
ADAM_B1 = 0.9
ADAM_B2 = 0.999
ADAM_EPS = 1e-08
ADAM_WD = 0.01
ADAM_STEP = 10
SMALL_W = 128
_ROWWISE_TILE_ELEMS = 512 * 1024

WEIGHTS = ['meta_tokens', 'ffn1_norm', 'ffn1_w_gate_up', 'ffn1_w_down', 'mix_norm', 'w_in', 'fox_forget_bias',
           'ssd_conv_w', 'ssd_conv_b', 'ssd_dt_bias', 'ssd_a_log', 'ssd_d', 'ssd_norm', 'lru_conv_w', 'lru_conv_b',
           'lru_w_a', 'lru_b_a', 'lru_w_x', 'lru_b_x', 'lru_lambda', 'w_branch_attn', 'w_branch_ssd', 'w_branch_lru',
           'w_out', 'ffn2_norm', 'ffn2_w_gate_up', 'ffn2_w_down', 'final_norm']
BIG = {'ffn1_w_gate_up': 1, 'ffn1_w_down': 0, 'w_in': 1, 'w_branch_attn': 0, 'w_branch_ssd': 0, 'w_branch_lru': 0,
       'w_out': 0, 'ffn2_w_gate_up': 1, 'ffn2_w_down': 0}
BIG_NAMES = [n for n in WEIGHTS if n in BIG]
COLSHARD_SMALL = ['meta_tokens', 'ssd_conv_w', 'lru_conv_w']
SMALL_NAMES = [n for n in WEIGHTS if n not in BIG]


def _pick(n, cands):
    for c in cands:
        if n % c == 0:
            return c
    raise ValueError(f"no tile for {n} in {cands}")


def _pcall(body, **kw):
    return pl.pallas_call(body, **kw)


def _cparams(sem):
    return pltpu.CompilerParams(dimension_semantics=sem, vmem_limit_bytes=VMEM_LIMIT_BYTES)


def _sds(shape, dtype):
    return jax.ShapeDtypeStruct(tuple(shape), dtype)


def _dot(a, b, hi=False):
    return jnp.dot(a, b, precision=HI if hi else None, preferred_element_type=F32)


def _dot_nt(a, b):
    return lax.dot_general(a, b, (((1,), (1,)), ((), ())), preferred_element_type=F32)


def _dot_tn(a, b):
    return lax.dot_general(a, b, (((0,), (0,)), ((), ())), preferred_element_type=F32)


def _sigmoid(x):
    return 1.0 / (1.0 + jnp.exp(-x))


def _softplus(x):
    return jnp.maximum(x, 0.0) + jnp.log1p(jnp.exp(-jnp.abs(x)))


def _silu(x):
    return x * _sigmoid(x)


def _dsilu(x):
    s = _sigmoid(x)
    return s * (1.0 + x * (1.0 - s))


_GELU_C = math.sqrt(2.0 / math.pi)


def _gelu(x):
    return 0.5 * x * (1.0 + jnp.tanh(_GELU_C * (x + 0.044715 * x * x * x)))


def _dgelu(x):
    t = jnp.tanh(_GELU_C * (x + 0.044715 * x * x * x))
    return 0.5 * (1.0 + t) + 0.5 * x * (1.0 - t * t) * _GELU_C * (1.0 + 3.0 * 0.044715 * x * x)


def _expm1(x):
    series = x * (1.0 + x * 0.5 * (1.0 + x * (1.0 / 3.0) * (1.0 + x * 0.25 * (1.0 + x * 0.2))))
    return jnp.where(jnp.abs(x) < 0.05, series, jnp.exp(x) - 1.0)


def _rowwise(fn, ins, outs, *, bcast=(), reds=(), tm=None, name, period=None):
    t_rows = ins[0].shape[0]
    if tm is None:
        widest = max([a.shape[1] for a in ins] + [c for c, _ in outs])
        tm = _pick(math.gcd(t_rows, period or t_rows),
                   [c for c in (384, 256, 128, 64, 32, 16, 8) if c * widest <= _ROWWISE_TILE_ELEMS or c == 8])
    nt = t_rows // tm
    assert t_rows % tm == 0 and (period is None or period % tm == 0)
    n_in, n_out = len(ins) + len(bcast), len(outs)

    def body(*refs):
        i = pl.program_id(0)
        pos = None
        if period is not None:
            pos = (i * tm) % period + lax.broadcasted_iota(jnp.int32, (tm, 1), 0)
        res = fn(pos, *[r[...] for r in refs[:n_in]])
        res = res if isinstance(res, tuple) else (res,)
        for r, v in zip(refs[n_in:n_in + n_out], res[:n_out]):
            r[...] = v.astype(r.dtype)
        red_refs = refs[n_in + n_out:]
        if red_refs:
            @pl.when(i == 0)
            def _():
                for r in red_refs:
                    r[...] = jnp.zeros_like(r)
            for r, v in zip(red_refs, res[n_out:]):
                r[...] += v

    in_specs = [pl.BlockSpec((tm, a.shape[1]), lambda i: (i, 0)) for a in ins]
    in_specs += [pl.BlockSpec(b.shape, lambda i, n=b.ndim: (0,) * n) for b in bcast]
    out_specs = [pl.BlockSpec((tm, c), lambda i: (i, 0)) for c, _ in outs]
    out_specs += [pl.BlockSpec(s, lambda i: (0, 0)) for s in reds]
    out_shape = [_sds((t_rows, c), dt) for c, dt in outs] + [_sds(s, F32) for s in reds]
    res = _pcall(body, name=name, grid=(nt,), in_specs=in_specs, out_specs=out_specs, out_shape=out_shape,
                 compiler_params=_cparams(("arbitrary",) if reds else ("parallel",)))(*ins, *bcast)
    return res


_TM = (768, 384, 256, 128)
_TN = (1536, 1408, 1024, 768, 512, 640, 384, 256, 128)
_TK = (1536, 1024, 1408, 512, 384, 256, 128)


def _mm_nn(a, b, out_dtype, *, res=None, alpha=1.0, name):
    m, k = a.shape
    k2, n = b.shape
    assert k == k2
    tm, tn, tk = _pick(m, _TM), _pick(n, _TN), _pick(k, _TK)
    nk = k // tk

    def body(*refs):
        if res is None:
            a_ref, b_ref, o_ref, acc = refs
            r_ref = None
        else:
            a_ref, b_ref, r_ref, o_ref, acc = refs
        kk = pl.program_id(2)

        @pl.when(kk == 0)
        def _():
            acc[...] = jnp.zeros_like(acc)

        acc[...] += _dot(a_ref[...].astype(BF16), b_ref[...].astype(BF16))

        @pl.when(kk == nk - 1)
        def _():
            v = acc[...]
            if alpha != 1.0:
                v = v * alpha
            if r_ref is not None:
                v = r_ref[...].astype(F32) + v
            o_ref[...] = v.astype(o_ref.dtype)

    in_specs = [pl.BlockSpec((tm, tk), lambda j, i, kk: (i, kk)), pl.BlockSpec((tk, tn), lambda j, i, kk: (kk, j))]
    args = [a, b]
    if res is not None:
        in_specs.append(pl.BlockSpec((tm, tn), lambda j, i, kk: (i, j)))
        args.append(res)
    return _pcall(body, name=name, grid=(n // tn, m // tm, nk), in_specs=in_specs,
                  out_specs=pl.BlockSpec((tm, tn), lambda j, i, kk: (i, j)), out_shape=_sds((m, n), out_dtype),
                  scratch_shapes=[pltpu.VMEM((tm, tn), F32)],
                  compiler_params=_cparams(("parallel", "parallel", "arbitrary")))(*args)


def _mm_swiglu(a, wg, wu, name):
    m, k = a.shape
    f = wg.shape[1]
    tm, tn, tk = _pick(m, _TM), _pick(f, _TN), _pick(k, _TK)
    nk = k // tk

    def body(a_ref, g_w, u_w, g_ref, u_ref, act_ref, accg, accu):
        kk = pl.program_id(2)

        @pl.when(kk == 0)
        def _():
            accg[...] = jnp.zeros_like(accg)
            accu[...] = jnp.zeros_like(accu)

        av = a_ref[...].astype(BF16)
        accg[...] += _dot(av, g_w[...])
        accu[...] += _dot(av, u_w[...])

        @pl.when(kk == nk - 1)
        def _():
            g, u = accg[...], accu[...]
            g_ref[...] = g.astype(g_ref.dtype)
            u_ref[...] = u.astype(u_ref.dtype)
            act_ref[...] = (_silu(g) * u).astype(act_ref.dtype)

    wspec = pl.BlockSpec((tk, tn), lambda j, i, kk: (kk, j))
    ospec = pl.BlockSpec((tm, tn), lambda j, i, kk: (i, j))
    return _pcall(body, name=name, grid=(f // tn, m // tm, nk),
                  in_specs=[pl.BlockSpec((tm, tk), lambda j, i, kk: (i, kk)), wspec, wspec],
                  out_specs=[ospec] * 3, out_shape=[_sds((m, f), BF16)] * 3,
                  scratch_shapes=[pltpu.VMEM((tm, tn), F32)] * 2,
                  compiler_params=_cparams(("parallel", "parallel", "arbitrary")))(a, wg, wu)


def _mm_dswiglu(dout, wd_t, g, u, alpha, name):
    m, k = dout.shape
    f = wd_t.shape[1]
    tm, tn, tk = _pick(m, _TM), _pick(f, _TN), _pick(k, _TK)
    nk = k // tk

    def body(a_ref, w_ref, g_ref, u_ref, dg_ref, du_ref, acc):
        kk = pl.program_id(2)

        @pl.when(kk == 0)
        def _():
            acc[...] = jnp.zeros_like(acc)

        acc[...] += _dot(a_ref[...].astype(BF16), w_ref[...])

        @pl.when(kk == nk - 1)
        def _():
            dact = acc[...] * alpha
            gv, uv = g_ref[...].astype(F32), u_ref[...].astype(F32)
            dg_ref[...] = (dact * uv * _dsilu(gv)).astype(dg_ref.dtype)
            du_ref[...] = (dact * _silu(gv)).astype(du_ref.dtype)

    ospec = pl.BlockSpec((tm, tn), lambda j, i, kk: (i, j))
    return _pcall(body, name=name, grid=(f // tn, m // tm, nk),
                  in_specs=[pl.BlockSpec((tm, tk), lambda j, i, kk: (i, kk)),
                            pl.BlockSpec((tk, tn), lambda j, i, kk: (kk, j)), ospec, ospec],
                  out_specs=[ospec] * 2, out_shape=[_sds((m, f), BF16)] * 2,
                  scratch_shapes=[pltpu.VMEM((tm, tn), F32)],
                  compiler_params=_cparams(("parallel", "parallel", "arbitrary")))(dout, wd_t, g, u)


def _mm_tn(a, b, *, alpha=1.0, name):
    m, k = a.shape
    m2, n = b.shape
    assert m == m2
    tm, tn, tko = _pick(m, _TM), _pick(n, _TN), _pick(k, _TK)
    nm = m // tm

    def body(a_ref, b_ref, o_ref, acc):
        mm = pl.program_id(2)

        @pl.when(mm == 0)
        def _():
            acc[...] = jnp.zeros_like(acc)

        acc[...] += _dot_tn(a_ref[...].astype(BF16), b_ref[...].astype(BF16))

        @pl.when(mm == nm - 1)
        def _():
            v = acc[...]
            o_ref[...] = v * alpha if alpha != 1.0 else v

    return _pcall(body, name=name, grid=(k // tko, n // tn, nm),
                  in_specs=[pl.BlockSpec((tm, tko), lambda i, j, mm: (mm, i)),
                            pl.BlockSpec((tm, tn), lambda i, j, mm: (mm, j))],
                  out_specs=pl.BlockSpec((tko, tn), lambda i, j, mm: (i, j)), out_shape=_sds((k, n), F32),
                  scratch_shapes=[pltpu.VMEM((tko, tn), F32)],
                  compiler_params=_cparams(("parallel", "parallel", "arbitrary")))(a, b)


def _rms_fwd(h, g, name):
    def fn(_, hv, gv):
        r = lax.rsqrt(jnp.mean(hv * hv, axis=1, keepdims=True) + NORM_EPS)
        return hv * r * gv
    return _rowwise(fn, [h], [(h.shape[1], BF16)], bcast=[g], name=name)[0]


def _rms_bwd(h, dxn, dres, g, name):
    d = h.shape[1]

    def fn(_, hv, dv, rv, gv):
        r = lax.rsqrt(jnp.mean(hv * hv, axis=1, keepdims=True) + NORM_EPS)
        xh = hv * r
        dxh = dv * gv
        dh = r * (dxh - xh * jnp.mean(dxh * xh, axis=1, keepdims=True))
        return rv + dh, jnp.sum(dv * xh, axis=0, keepdims=True)
    return _rowwise(fn, [h, dxn, dres], [(d, F32)], bcast=[g], reds=[(1, d)], name=name)


def _merge_fwd(mg, ba, bb, bc, name):
    d = ba.shape[1]

    def fn(_, m, a, b, c):
        g = _sigmoid(m.astype(F32))
        return g[:, :d] * a.astype(F32) + g[:, d:2 * d] * b.astype(F32) + g[:, 2 * d:] * c.astype(F32)
    return _rowwise(fn, [mg, ba, bb, bc], [(d, BF16)], name=name)[0]


def _merge_bwd(mg, ba, bb, bc, dmix, name):
    d = ba.shape[1]

    def fn(_, m, a, b, c, dm):
        g = _sigmoid(m.astype(F32))
        dm = dm.astype(F32)
        br = (a.astype(F32), b.astype(F32), c.astype(F32))
        douts, dgs = [], []
        for j in range(3):
            gj = g[:, j * d:(j + 1) * d]
            douts.append(dm * gj)
            dgs.append(dm * br[j] * gj * (1.0 - gj))
        return (*douts, jnp.concatenate(dgs, axis=1))
    return _rowwise(fn, [mg, ba, bb, bc, dmix], [(d, BF16)] * 3 + [(3 * d, BF16)], name=name)


def _loss_head(h, tgt, g, seq_len, name):
    d = h.shape[1]

    def fn(pos, hv, tv, gv):
        r = lax.rsqrt(jnp.mean(hv * hv, axis=1, keepdims=True) + NORM_EPS)
        xh = hv * r
        real = (pos >= N_META) & (pos < N_META + SEQ)
        e = jnp.where(real, xh * gv - tv, 0.0)
        part = jnp.sum(jnp.sum(e * e, axis=1, keepdims=True), axis=0, keepdims=True) * (0.5 / d)
        dy = e * (1.0 / d)
        dxh = dy * gv
        dh = r * (dxh - xh * jnp.mean(dxh * xh, axis=1, keepdims=True))
        return dh, jnp.broadcast_to(part, (1, 128)), jnp.sum(dy * xh, axis=0, keepdims=True)
    return _rowwise(fn, [h, tgt], [(d, F32)], bcast=[g], reds=[(1, 128), (1, d)], name=name, period=seq_len)


def _adamw(w, g, m, v, name):
    c1 = 1.0 - ADAM_B1 ** ADAM_STEP
    c2 = 1.0 - ADAM_B2 ** ADAM_STEP
    wd = w.shape[1]

    def fn(_, wv, gv, mv, vv):
        mn = ADAM_B1 * mv + (1.0 - ADAM_B1) * gv
        vn = ADAM_B2 * vv + (1.0 - ADAM_B2) * (gv * gv)
        delta = -ADAM_LR * ((mn / c1) / (jnp.sqrt(vn / c2) + ADAM_EPS) + ADAM_WD * wv)
        return delta, mn, vn
    return _rowwise(fn, [w, g, m, v], [(wd, F32)] * 3, name=name, tm=_pick(w.shape[0], (256, 128, 64, 32, 16, 8)))


def _sum_rows(parts, out_dtype, name):
    def fn(_, *vs):
        acc = vs[0].astype(F32)
        for v in vs[1:]:
            acc = acc + v.astype(F32)
        return acc
    return _rowwise(fn, list(parts), [(parts[0].shape[1], out_dtype)], name=name,
                    tm=_pick(parts[0].shape[0], (256, 128, 64, 32, 16, 8)))[0]


def _cumsum_seq(x, reverse, name):
    b, l, w = x.shape
    q = 128
    nc = l // q

    def body(x_ref, o_ref):
        row = lax.broadcasted_iota(jnp.int32, (q, q), 0)
        col = lax.broadcasted_iota(jnp.int32, (q, q), 1)
        tri = ((row <= col) if reverse else (row >= col)).astype(F32)
        rsel = lax.broadcasted_iota(jnp.int32, (q, w), 0) == (0 if reverse else q - 1)

        def step(i, carry):
            j = (nc - 1 - i) if reverse else i
            start = pl.multiple_of(j * q, q)
            cs = _dot(tri, x_ref[pl.ds(start, q), :], hi=True) + carry
            o_ref[pl.ds(start, q), :] = cs
            return jnp.sum(jnp.where(rsel, cs, 0.0), axis=0, keepdims=True)

        lax.fori_loop(0, nc, step, jnp.zeros((1, w), F32))

    return _pcall(body, name=name, grid=(b,), in_specs=[pl.BlockSpec((None, l, w), lambda i: (i, 0, 0))],
                  out_specs=pl.BlockSpec((None, l, w), lambda i: (i, 0, 0)), out_shape=_sds(x.shape, F32),
                  compiler_params=_cparams(("parallel",)))(x)


_HALO = 16


def _conv_tiles(l, c):
    return _pick(l, (384, 256, 128)), _pick(c, (512, 256, 128))


def _conv_fwd(x, w, bias, out_dtype, name):
    b, l, c = x.shape
    tt, cw = _conv_tiles(l, c)

    def body(x_ref, h_ref, w_ref, b_ref, o_ref):
        t = pl.program_id(2)
        halo = jnp.where(t == 0, 0.0, h_ref[...].astype(F32))
        xe = jnp.concatenate([halo, x_ref[...].astype(F32)], axis=0)
        wv = w_ref[...]
        acc = b_ref[...] + wv[CONV_K - 1:CONV_K, :] * xe[_HALO:]
        for j in range(CONV_K - 1):
            acc = acc + wv[j:j + 1, :] * pltpu.roll(xe, CONV_K - 1 - j, 0)[_HALO:]
        o_ref[...] = acc.astype(o_ref.dtype)

    return _pcall(body, name=name, grid=(b, c // cw, l // tt),
                  in_specs=[pl.BlockSpec((None, tt, cw), lambda i, j, t: (i, t, j)),
                            pl.BlockSpec((None, _HALO, cw), lambda i, j, t: (i, jnp.maximum(t * (tt // _HALO) - 1, 0), j)),
                            pl.BlockSpec((CONV_K, cw), lambda i, j, t: (0, j)),
                            pl.BlockSpec((1, cw), lambda i, j, t: (0, j))],
                  out_specs=pl.BlockSpec((None, tt, cw), lambda i, j, t: (i, t, j)), out_shape=_sds(x.shape, out_dtype),
                  compiler_params=_cparams(("parallel", "parallel", "parallel")))(x, x, w, bias)


def _conv_bwd(x, dy, w, name):
    b, l, c = x.shape
    tt, cw = _conv_tiles(l, c)
    nt = l // tt

    def body(x_ref, xh_ref, d_ref, dh_ref, w_ref, dx_ref, dw_ref):
        i, t = pl.program_id(1), pl.program_id(2)
        halo = jnp.where(t == 0, 0.0, xh_ref[...].astype(F32))
        xe = jnp.concatenate([halo, x_ref[...].astype(F32)], axis=0)
        dv = d_ref[...].astype(F32)
        nxt = jnp.where(t == nt - 1, 0.0, dh_ref[...].astype(F32))
        de = jnp.concatenate([dv, nxt], axis=0)
        wv = w_ref[...]
        dx = wv[CONV_K - 1:CONV_K, :] * dv
        rowid = lax.broadcasted_iota(jnp.int32, (8, 1), 0)
        part = jnp.where(rowid == CONV_K, jnp.sum(dv, axis=0, keepdims=True), 0.0)
        part = part + jnp.where(rowid == CONV_K - 1, jnp.sum(dv * xe[_HALO:], axis=0, keepdims=True), 0.0)
        for j in range(CONV_K - 1):
            s = CONV_K - 1 - j
            dx = dx + wv[j:j + 1, :] * pltpu.roll(de, tt + _HALO - s, 0)[:tt]
            xs = pltpu.roll(xe, s, 0)[_HALO:]
            part = part + jnp.where(rowid == j, jnp.sum(dv * xs, axis=0, keepdims=True), 0.0)
        dx_ref[...] = dx.astype(dx_ref.dtype)

        @pl.when((i == 0) & (t == 0))
        def _():
            dw_ref[...] = jnp.zeros_like(dw_ref)
        dw_ref[...] += part

    return _pcall(body, name=name, grid=(c // cw, b, nt),
                  in_specs=[pl.BlockSpec((None, tt, cw), lambda j, i, t: (i, t, j)),
                            pl.BlockSpec((None, _HALO, cw), lambda j, i, t: (i, jnp.maximum(t * (tt // _HALO) - 1, 0), j)),
                            pl.BlockSpec((None, tt, cw), lambda j, i, t: (i, t, j)),
                            pl.BlockSpec((None, _HALO, cw),
                                         lambda j, i, t: (i, jnp.minimum((t + 1) * (tt // _HALO), l // _HALO - 1), j)),
                            pl.BlockSpec((CONV_K, cw), lambda j, i, t: (0, j))],
                  out_specs=[pl.BlockSpec((None, tt, cw), lambda j, i, t: (i, t, j)),
                             pl.BlockSpec((8, cw), lambda j, i, t: (0, j))],
                  out_shape=[_sds(x.shape, BF16), _sds((8, c), F32)],
                  compiler_params=_cparams(("parallel", "arbitrary", "arbitrary")))(x, x, dy, dy, w)


def _linear_scan(a, u, reverse, name):
    b, l, c = a.shape
    tt = 128
    cw = _pick(c, (512, 256, 128))
    nt = l // tt

    def body(a_ref, u_ref, h_ref, carry):
        t = pl.program_id(2)

        @pl.when(t == 0)
        def _():
            carry[...] = jnp.zeros_like(carry)

        av, uv = a_ref[...], u_ref[...]
        row = lax.broadcasted_iota(jnp.int32, (tt, cw), 0)
        k = 1
        while k < tt:
            if reverse:
                keep = row < tt - k
                a_sh = jnp.where(keep, pltpu.roll(av, tt - k, 0), 1.0)
                u_sh = jnp.where(keep, pltpu.roll(uv, tt - k, 0), 0.0)
            else:
                keep = row >= k
                a_sh = jnp.where(keep, pltpu.roll(av, k, 0), 1.0)
                u_sh = jnp.where(keep, pltpu.roll(uv, k, 0), 0.0)
            uv = uv + av * u_sh
            av = av * a_sh
            k *= 2
        hv = uv + av * carry[0:1, :]
        h_ref[...] = hv
        edge = jnp.sum(jnp.where(row == (0 if reverse else tt - 1), hv, 0.0), axis=0, keepdims=True)
        carry[...] = jnp.broadcast_to(edge, carry.shape)

    tmap = (lambda i, j, t: (i, nt - 1 - t, j)) if reverse else (lambda i, j, t: (i, t, j))
    spec = pl.BlockSpec((None, tt, cw), tmap)
    return _pcall(body, name=name, grid=(b, c // cw, nt), in_specs=[spec, spec], out_specs=spec,
                  out_shape=_sds(a.shape, F32), scratch_shapes=[pltpu.VMEM((8, cw), F32)],
                  compiler_params=_cparams(("parallel", "parallel", "arbitrary")))(a, u)


ATTN_W = 128
_AUG_C = 0
_AUG_ONE = 3
_AUG_LSE = 6


def _attn_blk(l):
    return _pick(l, (384, 256, 128))


def _split3(x):
    x1 = x.astype(BF16).astype(F32)
    x2 = (x - x1).astype(BF16).astype(F32)
    x3 = (x - x1 - x2).astype(BF16).astype(F32)
    return x1, x2, x3


def _aug_lanes(lane, base, vals):
    out = 0.0
    for k, v in enumerate(vals):
        out = jnp.where(lane == base + k, v, out)
    return out


def _pair_specs(blk, nb):
    at = lambda ww: pl.BlockSpec((None, 2, blk, ww), lambda bi, p, i: (bi, p, i, 0))
    whole = pl.BlockSpec((None, 2, nb, blk, ATTN_W), lambda bi, p, i: (bi, p, 0, 0, 0))
    rows = pl.BlockSpec((None, blk, ATTN_W), lambda bi, p, i: (bi, i, p))
    return at, whole, rows


def _attn_prep(pm3, cum, col0, name):
    b, l, _ = pm3.shape
    dh, nh = ATTN_HEAD_DIM, ATTN_HEADS
    blk = _attn_blk(l)
    scale = dh ** -0.5

    def body(q_ref, k_ref, v_ref, c_ref, qa_ref, ka_ref, va_ref):
        pair = pl.program_id(1)
        lane = lax.broadcasted_iota(jnp.int32, (1, ATTN_W), 1)
        head = lane < dh
        cv = c_ref[...]
        qf, kf, vf = (r[...].astype(F32) for r in (q_ref, k_ref, v_ref))
        for e in range(2):
            c1, c2, c3 = _split3(jnp.sum(jnp.where(lane == 2 * pair + e, cv, 0.0), axis=1, keepdims=True))
            qe, ke, ve = (pltpu.roll(t, dh, 1) for t in (qf, kf, vf)) if e else (qf, kf, vf)
            qa_ref[e] = jnp.where(head, qe * scale, _aug_lanes(lane, dh, (c1, c2, c3, 1.0, 1.0, 1.0))).astype(BF16)
            ka_ref[e] = jnp.where(head, ke, _aug_lanes(lane, dh, (1.0, 1.0, 1.0, -c1, -c2, -c3, 1.0, 1.0, 1.0))).astype(BF16)
            va_ref[e] = jnp.where(head, ve, _aug_lanes(lane, dh, (1.0, 1.0, 1.0))).astype(BF16)

    at, _, _ = _pair_specs(blk, l // blk)
    cols = lambda c0: pl.BlockSpec((None, blk, ATTN_W), lambda bi, p, i, c0=c0: (bi, i, c0 // ATTN_W + p))
    return _pcall(body, name=name, grid=(b, nh // 2, l // blk),
                  in_specs=[cols(col0[0]), cols(col0[1]), cols(col0[2]),
                            pl.BlockSpec((None, blk, SMALL_W), lambda bi, p, i: (bi, i, 0))],
                  out_specs=[at(ATTN_W)] * 3, out_shape=[_sds((b, nh, l, ATTN_W), BF16)] * 3,
                  compiler_params=_cparams(("parallel", "parallel", "parallel")))(pm3, pm3, pm3, cum)


def _attn_prep_bwd(dy3, y3, qa, lse, name):
    b, nh, l, _ = qa.shape
    dh = ATTN_HEAD_DIM
    blk = _attn_blk(l)

    def body(dy_ref, y_ref, qa_ref, lse_ref, qa2_ref, doa_ref):
        lane = lax.broadcasted_iota(jnp.int32, (1, ATTN_W), 1)
        dyf = dy_ref[...].astype(F32)
        prod = dyf * y_ref[...].astype(F32)
        for e in range(2):
            mine = (lane >= dh) if e else (lane < dh)
            d1, d2, d3 = _split3(jnp.sum(jnp.where(mine, prod, 0.0), axis=1, keepdims=True))
            l1, l2, l3 = _split3(lse_ref[e])
            dye = pltpu.roll(dyf, dh, 1) if e else dyf
            doa_ref[e] = jnp.where(lane < dh, dye, _aug_lanes(lane, dh, (-d1, -d2, -d3))).astype(BF16)
            on_lse = (lane >= dh + _AUG_LSE) & (lane < dh + _AUG_LSE + 3)
            qa2_ref[e] = jnp.where(on_lse, _aug_lanes(lane, dh + _AUG_LSE, (-l1, -l2, -l3)),
                                   qa_ref[e].astype(F32)).astype(BF16)

    at, _, rows = _pair_specs(blk, l // blk)
    return _pcall(body, name=name, grid=(b, nh // 2, l // blk), in_specs=[rows, rows, at(ATTN_W), at(1)],
                  out_specs=[at(ATTN_W)] * 2, out_shape=[_sds(qa.shape, BF16)] * 2,
                  compiler_params=_cparams(("parallel", "parallel", "parallel")))(dy3, y3, qa, lse)


def _flash_fwd(qa, ka, va, d_model, name):
    b, h, l, w = qa.shape
    dh = ATTN_HEAD_DIM
    blk = _attn_blk(l)
    nb = l // blk
    kr, vr = ka.reshape(b, h, nb, blk, w), va.reshape(b, h, nb, blk, w)

    def body(q_ref, k_ref, v_ref, o_ref, lse_ref):
        i = pl.program_id(2)
        row = lax.broadcasted_iota(jnp.int32, (blk, blk), 0)
        col = lax.broadcasted_iota(jnp.int32, (blk, blk), 1)

        def scores(e, j):
            return _dot_nt(q_ref[e], k_ref[e, j])

        def consume(e, j, s, m, acc):
            mn = jnp.maximum(m, jnp.max(s, axis=1, keepdims=True))
            return mn, jnp.exp(m - mn) * acc + _dot(jnp.exp(s - mn).astype(BF16), v_ref[e, j])

        def step(j, carry):
            out = []
            for e in range(2):
                m, acc, s = carry[3 * e:3 * e + 3]
                s_next = scores(e, j + 1)
                out += [*consume(e, j, s, m, acc), s_next]
            return tuple(out)

        init = tuple(t for e in range(2)
                     for t in (jnp.full((blk, 1), NEG, F32), jnp.zeros((blk, w), F32), scores(e, 0)))
        carry = lax.fori_loop(0, i, step, init)
        m0, a0 = consume(0, i, jnp.where(col <= row, carry[2], NEG), carry[0], carry[1])
        m1, a1 = consume(1, i, jnp.where(col <= row, carry[5], NEG), carry[3], carry[4])
        l0, l1 = a0[:, dh:dh + 1], a1[:, dh:dh + 1]
        lane = lax.broadcasted_iota(jnp.int32, (1, w), 1)
        o_ref[...] = jnp.where(lane < dh, a0 / l0, pltpu.roll(a1 / l1, dh, 1)).astype(o_ref.dtype)
        lse_ref[0] = m0 + jnp.log(l0)
        lse_ref[1] = m1 + jnp.log(l1)

    at, whole, rows = _pair_specs(blk, nb)
    return _pcall(body, name=name, grid=(b, h // 2, nb), in_specs=[at(w), whole, whole],
                  out_specs=[rows, at(1)], out_shape=[_sds((b, l, d_model), BF16), _sds((b, h, l, 1), F32)],
                  compiler_params=_cparams(("parallel", "parallel", "parallel")))(qa, kr, vr)


def _flash_bwd(qa, ka, va, doa, d_model, name):
    b, h, l, w = qa.shape
    dh = ATTN_HEAD_DIM
    blk = _attn_blk(l)
    nb = l // blk
    scale = dh ** -0.5
    r5 = lambda t: t.reshape(b, h, nb, blk, w)

    def body(k_ref, v_ref, q_ref, do_ref, dq_ref, dk_ref, dv_ref, dcq_ref, dck_ref, dq_acc):
        j = pl.program_id(2)
        row = lax.broadcasted_iota(jnp.int32, (blk, blk), 0)
        col = lax.broadcasted_iota(jnp.int32, (blk, blk), 1)
        lane = lax.broadcasted_iota(jnp.int32, (1, w), 1)

        @pl.when(j == 0)
        def _():
            dq_acc[...] = jnp.zeros_like(dq_acc)

        def contrib(i, masked, carry):
            out = []
            for e in range(2):
                qv, dov = q_ref[e, i], do_ref[e, i]
                p = jnp.exp(_dot_nt(qv, k_ref[e]))
                if masked:
                    p = jnp.where(col <= row, p, 0.0)
                ds = (p * _dot_nt(dov, v_ref[e])).astype(BF16)
                dq_acc[e, i] += _dot(ds, k_ref[e])
                out += [carry[2 * e] + _dot_tn(ds, qv), carry[2 * e + 1] + _dot_tn(p.astype(BF16), dov)]
            return tuple(out)

        zero = (jnp.zeros((blk, w), F32),) * 4
        dk0, dv0, dk1, dv1 = lax.fori_loop(j + 1, nb, lambda i, c: contrib(i, False, c), contrib(j, True, zero))
        dk_ref[...] = jnp.where(lane < dh, dk0, pltpu.roll(dk1, dh, 1)).astype(dk_ref.dtype)
        dv_ref[...] = jnp.where(lane < dh, dv0, pltpu.roll(dv1, dh, 1)).astype(dv_ref.dtype)
        for e, dk in enumerate((dk0, dk1)):
            dck_ref[e] = jnp.sum(jnp.where(lane == dh + _AUG_ONE, dk, 0.0), axis=1, keepdims=True)

        @pl.when(j == nb - 1)
        def _():
            for ib in range(nb):
                rs = pl.ds(ib * blk, blk)
                dq0, dq1 = dq_acc[0, ib], dq_acc[1, ib]
                dq_ref[rs, :] = (jnp.where(lane < dh, dq0, pltpu.roll(dq1, dh, 1)) * scale).astype(dq_ref.dtype)
                for e, dq in enumerate((dq0, dq1)):
                    dcq_ref[e, rs, :] = jnp.sum(jnp.where(lane == dh + _AUG_C, dq, 0.0), axis=1, keepdims=True)

    at, whole, rows = _pair_specs(blk, nb)
    seq_rows = pl.BlockSpec((None, l, ATTN_W), lambda bi, p, j: (bi, 0, p))
    seq_col = pl.BlockSpec((None, 2, l, 1), lambda bi, p, j: (bi, p, 0, 0))
    act = _sds((b, l, d_model), BF16)
    col1 = _sds((b, h, l, 1), F32)
    return _pcall(body, name=name, grid=(b, h // 2, nb), in_specs=[at(w), at(w), whole, whole],
                  out_specs=[seq_rows, rows, rows, seq_col, at(1)], out_shape=[act, act, act, col1, col1],
                  scratch_shapes=[pltpu.VMEM((2, nb, blk, w), F32)],
                  compiler_params=_cparams(("parallel", "parallel", "arbitrary")))(ka, va, r5(qa), r5(doa))


def _ssd_dims(d_ssd):
    heads = d_ssd // SSD_HEAD_DIM
    return heads, heads // SSD_GROUPS, d_ssd // SSD_GROUPS


def _ssd_specs(l, ds, seq_map):
    q = SSD_CHUNK
    gn = SSD_GROUPS * SSD_STATE
    row3 = lambda w, cb: pl.BlockSpec((None, q, w), lambda i, c, cb=cb: (i, seq_map(c), cb))
    return dict(
        xs=row3(ds, 0), bm=row3(gn, ds // gn), cm=row3(gn, ds // gn + 1), z=row3(ds, 0), dt=row3(SMALL_W, 0),
        da=row3(SMALL_W, 0), dat=pl.BlockSpec((None, SMALL_W, q), lambda i, c: (i, 0, seq_map(c))),
        e=pl.BlockSpec((SMALL_W, ds), lambda i, c: (0, 0)), et=pl.BlockSpec((ds, SMALL_W), lambda i, c: (0, 0)),
        vec=pl.BlockSpec((1, ds), lambda i, c: (0, 0)), vec128=pl.BlockSpec((1, SMALL_W), lambda i, c: (0, 0)),
        hin=pl.BlockSpec((None, None, SSD_STATE, ds), lambda i, c: (i, seq_map(c), 0, 0)))


def _ssd_common(da, dat, dt, e_mat, xs):
    q = SSD_CHUNK
    row = lax.broadcasted_iota(jnp.int32, (q, q), 0)
    col = lax.broadcasted_iota(jnp.int32, (q, q), 1)
    lower = row >= col
    cs = _dot(lower.astype(F32), da, hi=True)
    cst = _dot(dat, (row <= col).astype(F32), hi=True)
    dtx = _dot(dt, e_mat, hi=True)
    csx = _dot(cs, e_mat, hi=True)
    rowx = lax.broadcasted_iota(jnp.int32, csx.shape, 0)
    totx = jnp.sum(jnp.where(rowx == q - 1, csx, 0.0), axis=0, keepdims=True)
    xf = xs.astype(F32)
    return lower, cs, cst, dtx, csx, totx, xf, xf * dtx


def _ssd_fwd(xbc, z, dt, da, dat, e_mat, dx, nw, name):
    b, l, _ = xbc.shape
    ds = z.shape[2]
    heads, hpg, gw = _ssd_dims(ds)
    q, n = SSD_CHUNK, SSD_STATE
    nc = l // q
    hcol0 = ATTN_HEADS

    def body(xs_ref, bm_ref, cm_ref, z_ref, dt_ref, da_ref, dat_ref, e_ref, dx_ref, nw_ref, y_ref, yraw_ref, hin_ref,
             hst, ydiag):
        c = pl.program_id(1)

        @pl.when(c == 0)
        def _():
            hst[...] = jnp.zeros_like(hst)

        hin = hst[...]
        hin_ref[...] = hin
        lower, cs, cst, dtx, csx, totx, xf, xdt = _ssd_common(da_ref[...], dat_ref[...], dt_ref[...], e_ref[...],
                                                               xs_ref[...])
        bm, cm = bm_ref[...], cm_ref[...]
        dec_end = jnp.exp(totx - csx)
        for g in range(SSD_GROUPS):
            gs = slice(g * gw, (g + 1) * gw)
            bg, cg = bm[:, g * n:(g + 1) * n], cm[:, g * n:(g + 1) * n]
            cb = _dot_nt(cg, bg)
            for e in range(hpg):
                hh = g * hpg + e
                cc = hcol0 + hh
                lm = jnp.exp(jnp.where(lower, cs[:, cc:cc + 1] - cst[cc:cc + 1, :], NEG))
                hs = slice(hh * SSD_HEAD_DIM, (hh + 1) * SSD_HEAD_DIM)
                ydiag[:, hs] = _dot((cb * lm).astype(BF16), xdt[:, hs].astype(BF16))
            sg = _dot_tn(bg, (xdt[:, gs] * dec_end[:, gs]).astype(BF16))
            hst[:, gs] = jnp.exp(totx[:, gs]) * hin[:, gs] + sg
            ydiag[:, gs] += _dot(cg, hin[:, gs].astype(BF16)) * jnp.exp(csx[:, gs])
        yraw = ydiag[...] + dx_ref[...] * xf
        yraw_ref[...] = yraw.astype(yraw_ref.dtype)
        yg = yraw * _silu(z_ref[...].astype(F32))
        nwv = nw_ref[...]
        for g in range(SSD_GROUPS):
            gs = slice(g * gw, (g + 1) * gw)
            r = lax.rsqrt(jnp.mean(yg[:, gs] * yg[:, gs], axis=1, keepdims=True) + NORM_EPS)
            y_ref[:, gs] = (yg[:, gs] * r * nwv[:, gs]).astype(y_ref.dtype)

    sp = _ssd_specs(l, ds, lambda c: c)
    return _pcall(body, name=name, grid=(b, nc),
                  in_specs=[sp['xs'], sp['bm'], sp['cm'], sp['z'], sp['dt'], sp['da'], sp['dat'], sp['e'], sp['vec'],
                            sp['vec']],
                  out_specs=[sp['z'], sp['z'], sp['hin']],
                  out_shape=[_sds((b, l, ds), BF16), _sds((b, l, ds), BF16), _sds((b, nc, n, ds), F32)],
                  scratch_shapes=[pltpu.VMEM((n, ds), F32), pltpu.VMEM((q, ds), F32)],
                  compiler_params=_cparams(("parallel", "arbitrary")))(xbc, xbc, xbc, z, dt, da, dat, e_mat, dx, nw)


def _ssd_bwd(xbc, z, dt, da, dat, e_mat, et_mat, dx, nw, a128, yraw, hin, dy, name):
    b, l, dxw = xbc.shape
    ds = z.shape[2]
    heads, hpg, gw = _ssd_dims(ds)
    q, n = SSD_CHUNK, SSD_STATE
    gn = SSD_GROUPS * n
    nc = l // q
    hcol0 = ATTN_HEADS

    def body(xs_ref, bm_ref, cm_ref, z_ref, dt_ref, da_ref, dat_ref, e_ref, et_ref, dx_ref, nw_ref, a_ref, yraw_ref,
             hin_ref, dy_ref, dxs_ref, dbm_ref, dcm_ref, dz_ref, ddt_ref, dd_ref, dnw_ref, dap_ref, dhs, dxdt, dcsx,
             dtotx):
        i, c = pl.program_id(0), pl.program_id(1)

        @pl.when(c == 0)
        def _():
            dhs[...] = jnp.zeros_like(dhs)

        @pl.when((i == 0) & (c == 0))
        def _():
            dd_ref[...] = jnp.zeros_like(dd_ref)
            dnw_ref[...] = jnp.zeros_like(dnw_ref)
            dap_ref[...] = jnp.zeros_like(dap_ref)

        dtv = dt_ref[...]
        lower, cs, cst, dtx, csx, totx, xf, xdt = _ssd_common(da_ref[...], dat_ref[...], dtv, e_ref[...], xs_ref[...])
        upper = jnp.logical_not(lower) | (lax.broadcasted_iota(jnp.int32, (q, q), 0)
                                          == lax.broadcasted_iota(jnp.int32, (q, q), 1))
        bm, cm = bm_ref[...], cm_ref[...]
        ecs, dec_end, etot = jnp.exp(csx), jnp.exp(totx - csx), jnp.exp(totx)
        yraw = yraw_ref[...].astype(F32)
        zv = z_ref[...].astype(F32)
        sz = _silu(zv)
        yg = yraw * sz
        dyn_ = dy_ref[...].astype(F32)
        nwv = nw_ref[...]
        dygs, dnws = [], []
        for g in range(SSD_GROUPS):
            gs = slice(g * gw, (g + 1) * gw)
            r = lax.rsqrt(jnp.mean(yg[:, gs] * yg[:, gs], axis=1, keepdims=True) + NORM_EPS)
            yn = yg[:, gs] * r
            dn = dyn_[:, gs] * nwv[:, gs]
            dnws.append(jnp.sum(dyn_[:, gs] * yn, axis=0, keepdims=True))
            dygs.append(r * (dn - yn * jnp.mean(dn * yn, axis=1, keepdims=True)))
        dyg = jnp.concatenate(dygs, axis=1)
        dnw_ref[...] += jnp.concatenate(dnws, axis=1)
        dz_ref[...] = (dyg * yraw * _dsilu(zv)).astype(dz_ref.dtype)
        dyv = dyg * sz
        dd_ref[...] += jnp.sum(dyv * xf, axis=0, keepdims=True)
        hin, dh = hin_ref[...], dhs[...]
        lane128 = lax.broadcasted_iota(jnp.int32, (1, SMALL_W), 1)
        dcs = jnp.zeros((q, SMALL_W), F32)
        for g in range(SSD_GROUPS):
            gs = slice(g * gw, (g + 1) * gw)
            bg, cg = bm[:, g * n:(g + 1) * n], cm[:, g * n:(g + 1) * n]
            hg, dhg = hin[:, gs], dh[:, gs]
            hgb, dsb = hg.astype(BF16), dhg.astype(BF16)
            yoff = _dot(cg, hgb) * ecs[:, gs]
            dch = (dyv[:, gs] * ecs[:, gs]).astype(BF16)
            dcg = _dot_nt(dch, hgb)
            dhs[:, gs] = _dot_tn(cg, dch) + etot[:, gs] * dhg
            zg = xdt[:, gs] * dec_end[:, gs]
            dzz = _dot(bg, dsb)
            dbg = _dot_nt(zg.astype(BF16), dsb)
            dxdt_g = dzz * dec_end[:, gs]
            w_end = dzz * zg
            dtotx[:, gs] = jnp.sum(dhg * hg, axis=0, keepdims=True) * etot[:, gs] + jnp.sum(w_end, axis=0, keepdims=True)
            dcsx[:, gs] = dyv[:, gs] * yoff - w_end
            cb, cbt = _dot_nt(cg, bg), _dot_nt(bg, cg)
            dgm = jnp.zeros((q, q), F32)
            for e in range(hpg):
                hh = g * hpg + e
                cc = hcol0 + hh
                ccol, crow = cs[:, cc:cc + 1], cst[cc:cc + 1, :]
                lm = jnp.exp(jnp.where(lower, ccol - crow, NEG))
                lmt = jnp.exp(jnp.where(upper, crow - ccol, NEG))
                mm, mt = cb * lm, cbt * lmt
                hs = slice(hh * SSD_HEAD_DIM, (hh + 1) * SSD_HEAD_DIM)
                dye, xe = dyv[:, hs].astype(BF16), xdt[:, hs].astype(BF16)
                dm, dmt = _dot_nt(dye, xe), _dot_nt(xe, dye)
                dxdt[:, hs] = dxdt_g[:, e * SSD_HEAD_DIM:(e + 1) * SSD_HEAD_DIM] + _dot(mt.astype(BF16), dye)
                dgm = dgm + dm * lm
                rs = jnp.sum(dm * mm, axis=1, keepdims=True) - jnp.sum(dmt * mt, axis=1, keepdims=True)
                dcs = dcs + rs * (lane128 == cc).astype(F32)
            dgb = dgm.astype(BF16)
            dcm_ref[:, g * n:(g + 1) * n] = (dcg + _dot(dgb, bg)).astype(dcm_ref.dtype)
            dbm_ref[:, g * n:(g + 1) * n] = (dbg + _dot_tn(dgb, cg)).astype(dbm_ref.dtype)
        dxd = dxdt[...]
        dxs_ref[...] = (dx_ref[...] * dyv + dxd * dtx).astype(dxs_ref.dtype)
        et = et_ref[...]
        ddt = _dot(dxd * xf, et, hi=True)
        dtot128 = _dot(jnp.broadcast_to(dtotx[...], (8, ds)), et, hi=True)[0:1, :]
        row128 = lax.broadcasted_iota(jnp.int32, (q, SMALL_W), 0)
        dcs = dcs + _dot(dcsx[...], et, hi=True) + jnp.where(row128 == q - 1, dtot128, 0.0)
        dda = _dot(upper.astype(F32), dcs, hi=True)
        ddt_ref[...] = ddt + dda * a_ref[...]
        dap_ref[...] += jnp.sum(dda * dtv, axis=0, keepdims=True)

    rev = lambda c: nc - 1 - c
    sp = _ssd_specs(l, ds, rev)
    row3 = lambda w: pl.BlockSpec((None, q, w), lambda i, c: (i, rev(c), 0))
    acc = lambda w: pl.BlockSpec((1, w), lambda i, c: (0, 0))
    return _pcall(body, name=name, grid=(b, nc),
                  in_specs=[sp['xs'], sp['bm'], sp['cm'], sp['z'], sp['dt'], sp['da'], sp['dat'], sp['e'], sp['et'],
                            sp['vec'], sp['vec'], sp['vec128'], sp['z'], sp['hin'], sp['z']],
                  out_specs=[row3(ds), row3(gn), row3(gn), row3(ds), row3(SMALL_W), acc(ds), acc(ds), acc(SMALL_W)],
                  out_shape=[_sds((b, l, ds), BF16), _sds((b, l, gn), BF16), _sds((b, l, gn), BF16), _sds((b, l, ds), BF16),
                             _sds((b, l, SMALL_W), F32), _sds((1, ds), F32), _sds((1, ds), F32), _sds((1, SMALL_W), F32)],
                  scratch_shapes=[pltpu.VMEM((n, ds), F32), pltpu.VMEM((q, ds), F32), pltpu.VMEM((q, ds), F32),
                                  pltpu.VMEM((1, ds), F32)],
                  compiler_params=_cparams(("arbitrary", "arbitrary")))(
                      xbc, xbc, xbc, z, dt, da, dat, e_mat, et_mat, dx, nw, a128, yraw, hin, dy)


_GROUP_SIZE = {'c': 2, 'xy': 4, 'xyc': 8}
_LOCAL_SPLIT = 16


def _exchange(src, group, scatter, name, nsplit=1, copy_own=True):
    n = _GROUP_SIZE[group]
    rows, width = src.shape[-2:]
    assert src.ndim == (3 if scatter else 2)
    while rows % (8 * nsplit):
        nsplit //= 2
    crow = rows // nsplit
    nlocal = _LOCAL_SPLIT
    while rows % (8 * nlocal):
        nlocal //= 2
    lrow = rows // nlocal

    def body(src_ref, out_ref, send_sems, recv_sems, local_sems):
        x, y, c = lax.axis_index("x"), lax.axis_index("y"), lax.axis_index("c")
        if group == 'c':
            rank = c
            dev = lambda r: (x, y, r)
        elif group == 'xy':
            rank = 2 * x + y
            dev = lambda r: (r // 2, r % 2, c)
        else:
            rank = 4 * x + 2 * y + c
            dev = lambda r: (r // 4, (r // 2) % 2, r % 2)

        def mine_for(r, ck):
            piece = src_ref.at[r] if scatter else src_ref
            return piece.at[pl.ds(ck * crow, crow)]

        def copy(k, ck, pr, dst_rank):
            return pltpu.make_async_remote_copy(
                src_ref=mine_for(pr, ck), dst_ref=out_ref.at[dst_rank].at[pl.ds(ck * crow, crow)],
                send_sem=send_sems.at[k * nsplit + ck], recv_sem=recv_sems.at[k * nsplit + ck], device_id=dev(pr),
                device_id_type=pl.DeviceIdType.MESH)

        locals_ = []
        if copy_own:
            own = src_ref.at[rank] if scatter else src_ref
            for ck in range(nlocal):
                rs = pl.ds(ck * lrow, lrow)
                locals_.append(pltpu.make_async_copy(own.at[rs], out_ref.at[rank].at[rs], local_sems.at[ck]))
                locals_[-1].start()
        peers = [jnp.bitwise_xor(rank, k + 1) for k in range(n - 1)]
        sends = [copy(k, ck, pr, rank) for ck in range(nsplit) for k, pr in enumerate(peers)]
        for cp in sends:
            cp.start()
        for ck in range(nsplit):
            for k, pr in enumerate(peers):
                copy(k, ck, pr, pr).wait_recv()
        for cp in sends:
            cp.wait_send()
        for cp in locals_:
            cp.wait()

    return _pcall(body, name=name, in_specs=[pl.BlockSpec(memory_space=pl.ANY)],
                  out_specs=pl.BlockSpec(memory_space=pl.ANY), out_shape=_sds((n, rows, width), src.dtype),
                  scratch_shapes=[pltpu.SemaphoreType.DMA(((n - 1) * nsplit,)),
                                  pltpu.SemaphoreType.DMA(((n - 1) * nsplit,)),
                                  pltpu.SemaphoreType.DMA((nlocal,))])(src)


def _dims():
    d = D_MODEL
    h = ATTN_HEADS
    d_ssd = d
    d_xbc = d_ssd + 2 * SSD_GROUPS * SSD_STATE
    sizes = (d, d, d, h, d_ssd, d_xbc, d_ssd // SSD_HEAD_DIM, d, d, 3 * d)
    return d, h, d_ssd, d_xbc, sizes


def _w_in_split(w):
    d, h, d_ssd, d_xbc, sizes = _dims()
    off = np.concatenate([[0], np.cumsum(sizes)])
    seg = lambda i: w[..., off[i]:off[i + 1]]
    main = jnp.concatenate([seg(0), seg(1), seg(2), seg(4), seg(5), seg(7), seg(8), seg(9)], axis=-1)
    pad = jnp.zeros(w.shape[:-1] + (SMALL_W - sizes[3] - sizes[6],), w.dtype)
    small = jnp.concatenate([seg(3), seg(6), pad], axis=-1)
    return main, small


def _w_in_merge(main, small):
    d, h, d_ssd, d_xbc, sizes = _dims()
    order = (0, 1, 2, 4, 5, 7, 8, 9)
    moff = np.concatenate([[0], np.cumsum([sizes[i] for i in order])])
    pieces = {i: main[..., moff[j]:moff[j + 1]] for j, i in enumerate(order)}
    pieces[3] = small[..., :sizes[3]]
    pieces[6] = small[..., sizes[3]:sizes[3] + sizes[6]]
    return jnp.concatenate([pieces[i] for i in range(10)], axis=-1)


def _main_offsets():
    d, h, d_ssd, d_xbc, sizes = _dims()
    names = ('q', 'k', 'v', 'z', 'xbc', 'xr', 'gate', 'merge')
    widths = (d, d, d, d_ssd, d_xbc, d, d, 3 * d)
    off = np.concatenate([[0], np.cumsum(widths)])
    return {nm: (int(off[i]), int(off[i + 1])) for i, nm in enumerate(names)}


def _block_diag(w):
    nb, s, _ = w.shape
    eye = jnp.eye(nb, dtype=w.dtype)
    return (eye[:, None, :, None] * w[:, :, None, :]).reshape(nb * s, nb * s)


def _diag_blocks(wd, nb):
    s = wd.shape[0] // nb
    return jnp.stack([wd[i * s:(i + 1) * s, i * s:(i + 1) * s] for i in range(nb)])


def _vec128(*parts):
    v = jnp.concatenate([p.astype(F32) for p in parts])
    return jnp.pad(v, (0, SMALL_W - v.shape[0]))[None, :]


def _ffn_fwd(h, gnorm, w, tag):
    xn = _rms_fwd(h, gnorm[None, :], f"{tag}_norm")
    g, u, act = _mm_swiglu(xn, w['wg'], w['wu'], f"{tag}_gu")
    out = _mm_nn(act, w['wd'], F32, res=h, alpha=0.5, name=f"{tag}_down")
    return out, (h, xn, g, u, act)


def _ffn_bwd(dout, saved, gnorm, w, tag):
    h, xn, g, u, act = saved
    dg, du = _mm_dswiglu(dout, w['wd_t'], g, u, 0.5, f"{tag}_dgu")
    dwd = _mm_tn(act, dout, alpha=0.5, name=f"{tag}_dwd")
    dwgu = jnp.concatenate([_mm_tn(xn, dg, name=f"{tag}_dwg"), _mm_tn(xn, du, name=f"{tag}_dwu")], axis=1)
    dxn = _mm_nn(dg, w['wg_t'], F32, name=f"{tag}_dxn_g")
    dxn = _mm_nn(du, w['wu_t'], F32, res=dxn, name=f"{tag}_dxn_u")
    dh, dgn = _rms_bwd(h, dxn, dout, gnorm[None, :], f"{tag}_dnorm")
    return dh, dgn[0], dwgu, dwd


def _mixer_fwd(h, p, b, l):
    d, nh, d_ssd, d_xbc, sizes = _dims()
    t = b * l
    off = _main_offsets()
    xn = _rms_fwd(h, p['mix_norm'][None, :], "mix_norm")
    pm = _mm_nn(xn, p['w_main'], BF16, name="mix_in_main")
    ps = _mm_nn(xn, p['w_small'], F32, name="mix_in_small")
    col = lambda nm: pm[:, off[nm][0]:off[nm][1]]
    heads_ssd = d_ssd // SSD_HEAD_DIM
    a_neg = -jnp.exp(p['ssd_a_log'])
    fb = _vec128(p['fox_forget_bias'])
    dtb = _vec128(jnp.zeros((nh,), F32), p['ssd_dt_bias'])
    a128 = _vec128(jnp.zeros((nh,), F32), a_neg)

    def prep(_, v, fbv, dtbv, av):
        lane = lax.broadcasted_iota(jnp.int32, (1, SMALL_W), 1)
        logf = jnp.where(lane < nh, -_softplus(-(v + fbv)), 0.0)
        dtv = jnp.where((lane >= nh) & (lane < nh + heads_ssd), _softplus(v + dtbv), 0.0)
        return logf, dtv, dtv * av
    logf, dt, da = _rowwise(prep, [ps], [(SMALL_W, F32)] * 3, bcast=[fb, dtb, a128], name="mix_prep")

    cum = _cumsum_seq(logf.reshape(b, l, SMALL_W), False, "fox_cumsum")
    qa, ka, va = _attn_prep(pm.reshape(b, l, -1), cum, (off['q'][0], off['k'][0], off['v'][0]), "fox_prep")
    y_a3, lse = _flash_fwd(qa, ka, va, d, "fox_fwd")
    y_a = y_a3.reshape(t, d)

    xbc = col('xbc').reshape(b, l, d_xbc)
    pre_b = _conv_fwd(xbc, p['ssd_conv_w'], p['ssd_conv_b'][None, :], BF16, "ssd_conv")
    xbc_act = _rowwise(lambda _, v: _silu(v.astype(F32)), [pre_b.reshape(t, d_xbc)], [(d_xbc, BF16)],
                       name="ssd_conv_act")[0].reshape(b, l, d_xbc)
    z = col('z').reshape(b, l, d_ssd)
    dt3, da3 = dt.reshape(b, l, SMALL_W), da.reshape(b, l, SMALL_W)
    dat3 = da3.transpose(0, 2, 1)
    e_mat = _expand_matrix(nh, heads_ssd)
    dx = jnp.repeat(p['ssd_d'], SSD_HEAD_DIM)[None, :]
    nw = p['ssd_norm'][None, :]
    y_b3, yraw, hin = _ssd_fwd(xbc_act, z, dt3, da3, dat3, e_mat, dx, nw, "ssd_fwd")
    y_b = y_b3.reshape(t, d_ssd)

    xr = col('xr').reshape(b, l, d)
    xc = _conv_fwd(xr, p['lru_conv_w'], p['lru_conv_b'][None, :], F32, "lru_conv").reshape(t, d)
    pre_ri = _mm_nn(xc, p['lru_w_ri'], F32, name="lru_gates")
    lvec = (p['lru_b_a'][None, :], p['lru_b_x'][None, :], p['lru_lambda'][None, :])
    a_l, u_l = _rowwise(_lru_point_fwd, [pre_ri, xc], [(d, F32)] * 2, bcast=lvec, name="lru_point", period=l)
    hs = _linear_scan(a_l.reshape(b, l, d), u_l.reshape(b, l, d), False, "lru_scan").reshape(t, d)
    gate = col('gate')
    y_c = _rowwise(lambda _, hv, gv: hv * _gelu(gv.astype(F32)), [hs, gate], [(d, BF16)], name="lru_out")[0]

    ba = _mm_nn(y_a, p['w_branch_attn'], BF16, name="branch_attn")
    bb = _mm_nn(y_b, p['w_branch_ssd'], BF16, name="branch_ssd")
    bc = _mm_nn(y_c, p['w_branch_lru'], BF16, name="branch_lru")
    mg = col('merge')
    mixed = _merge_fwd(mg, ba, bb, bc, "merge")
    out = _mm_nn(mixed, p['w_out'], F32, res=h, name="mix_out")
    saved = dict(h=h, xn=xn, ps=ps, fb=fb, dtb=dtb, a128=a128, qa=qa, ka=ka, va=va, lse=lse,
                 xbc=xbc, pre_b=pre_b, xbc_act=xbc_act, z=z, dt3=dt3, da3=da3, dat3=dat3, e_mat=e_mat, dx=dx, nw=nw,
                 yraw=yraw, hin=hin, xr=xr, xc=xc, pre_ri=pre_ri, lvec=lvec, a_l=a_l, hs=hs, gate=gate, y_a=y_a, y_b=y_b,
                 y_c=y_c, ba=ba, bb=bb, bc=bc, mg=mg, mixed=mixed)
    return out, saved


def _expand_matrix(nh, heads_ssd):
    e = np.zeros((SMALL_W, heads_ssd * SSD_HEAD_DIM), np.float32)
    for hh in range(heads_ssd):
        e[nh + hh, hh * SSD_HEAD_DIM:(hh + 1) * SSD_HEAD_DIM] = 1.0
    return jnp.asarray(e)


def _lru_gates(pre, xc, bav, bxv, lamv, pos):
    d = xc.shape[1]
    r = _sigmoid(pre[:, :d] + bav)
    i = _sigmoid(pre[:, d:] + bxv)
    ls = -_softplus(-lamv)
    la = LRU_C * r * ls
    a = jnp.exp(la)
    mult = jnp.where(pos == 0, 1.0, jnp.sqrt(-_expm1(2.0 * la)))
    return r, i, ls, a, mult


def _lru_point_fwd(pos, pre, xc, bav, bxv, lamv):
    r, i, ls, a, mult = _lru_gates(pre, xc, bav, bxv, lamv, pos)
    return a, mult * (i * xc)


def _lru_point_bwd(pos, g, hprev, pre, xc, bav, bxv, lamv):
    r, i, ls, a, mult = _lru_gates(pre, xc, bav, bxv, lamv, pos)
    da = g * hprev
    di = g * mult * xc
    dxc = g * mult * i
    dmult = jnp.where(pos == 0, 0.0, g * i * xc)
    dla = da * a - dmult * (a * a) / mult
    dpre_r = dla * (LRU_C * ls) * r * (1.0 - r)
    dpre_i = di * i * (1.0 - i)
    dlam = jnp.sum(dla * (LRU_C * r), axis=0, keepdims=True) * _sigmoid(-lamv)
    return (jnp.concatenate([dpre_r, dpre_i], axis=1), dxc, dlam, jnp.sum(dpre_r, axis=0, keepdims=True),
            jnp.sum(dpre_i, axis=0, keepdims=True))


def _mixer_bwd(dout, s, p, b, l):
    d, nh, d_ssd, d_xbc, sizes = _dims()
    t = b * l
    heads_ssd = d_ssd // SSD_HEAD_DIM
    g = {}
    dmixed = _mm_nn(dout, p['w_out_t'], BF16, name="mix_out_dx")
    g['w_out'] = _mm_tn(s['mixed'], dout, name="mix_out_dw")
    dba, dbb, dbc, dmerge = _merge_bwd(s['mg'], s['ba'], s['bb'], s['bc'], dmixed, "merge_bwd")
    g['w_branch_attn'] = _mm_tn(s['y_a'], dba, name="branch_attn_dw")
    g['w_branch_ssd'] = _mm_tn(s['y_b'], dbb, name="branch_ssd_dw")
    g['w_branch_lru'] = _mm_tn(s['y_c'], dbc, name="branch_lru_dw")
    dy_a = _mm_nn(dba, p['w_branch_attn_t'], BF16, name="branch_attn_dx")
    dy_b = _mm_nn(dbb, p['w_branch_ssd_t'], BF16, name="branch_ssd_dx")
    dy_c = _mm_nn(dbc, p['w_branch_lru_t'], F32, name="branch_lru_dx")

    dgate, dhs = _rowwise(lambda _, dv, hv, gv: (dv * hv * _dgelu(gv.astype(F32)), dv * _gelu(gv.astype(F32))),
                          [dy_c, s['hs'], s['gate']], [(d, BF16), (d, F32)], name="lru_out_bwd")
    a3 = s['a_l'].reshape(b, l, d)
    a_next = jnp.concatenate([a3[:, 1:], jnp.zeros((b, 1, d), F32)], axis=1)
    gs = _linear_scan(a_next, dhs.reshape(b, l, d), True, "lru_scan_bwd").reshape(t, d)
    h3 = s['hs'].reshape(b, l, d)
    hprev = jnp.concatenate([jnp.zeros((b, 1, d), F32), h3[:, :-1]], axis=1).reshape(t, d)
    dpre_ri, dxc0, dlam, dba_, dbx_ = _rowwise(_lru_point_bwd, [gs, hprev, s['pre_ri'], s['xc']],
                                               [(2 * d, BF16), (d, F32)], bcast=s['lvec'],
                                               reds=[(1, d)] * 3, name="lru_point_bwd", period=l)
    g['lru_lambda'], g['lru_b_a'], g['lru_b_x'] = dlam[0], dba_[0], dbx_[0]
    dxc = _mm_nn(dpre_ri, p['lru_w_ri_t'], BF16, res=dxc0, name="lru_gates_dx")
    dw_ri = _mm_tn(s['xc'], dpre_ri, name="lru_gates_dw")
    g['lru_w_a'] = _diag_blocks(dw_ri[:, :d], LRU_BLOCKS)
    g['lru_w_x'] = _diag_blocks(dw_ri[:, d:], LRU_BLOCKS)
    dxr, dwl = _conv_bwd(s['xr'], dxc.reshape(b, l, d), p['lru_conv_w'], "lru_conv_bwd")
    g['lru_conv_w'], g['lru_conv_b'] = dwl[:CONV_K], dwl[CONV_K]

    et_mat = s['e_mat'].T
    dxs, dbm, dcm, dz, ddt, dd_l, dnw, dap = _ssd_bwd(s['xbc_act'], s['z'], s['dt3'], s['da3'], s['dat3'], s['e_mat'],
                                                      et_mat, s['dx'], s['nw'], s['a128'], s['yraw'], s['hin'],
                                                      dy_b.reshape(b, l, d_ssd), "ssd_bwd")
    g['ssd_d'] = dd_l.reshape(heads_ssd, SSD_HEAD_DIM).sum(axis=1)
    g['ssd_norm'] = dnw[0]
    g['ssd_a_log'] = dap[0, nh:nh + heads_ssd] * (-jnp.exp(p['ssd_a_log']))
    dxbc_act = jnp.concatenate([dxs, dbm, dcm], axis=2).reshape(t, d_xbc)
    dpre_b = _rowwise(lambda _, dv, pv: dv.astype(F32) * _dsilu(pv.astype(F32)),
                      [dxbc_act, s['pre_b'].reshape(t, d_xbc)], [(d_xbc, BF16)], name="ssd_conv_act_bwd")[0]
    dxbc, dws = _conv_bwd(s['xbc'], dpre_b.reshape(b, l, d_xbc), p['ssd_conv_w'], "ssd_conv_bwd")
    g['ssd_conv_w'], g['ssd_conv_b'] = dws[:CONV_K], dws[CONV_K]

    qa2, doa = _attn_prep_bwd(dy_a.reshape(b, l, d), s['y_a'].reshape(b, l, d), s['qa'], s['lse'], "fox_prep_bwd")
    dq3, dk3, dv3, dcq, dck = _flash_bwd(qa2, s['ka'], s['va'], doa, d, "fox_bwd")
    dcum = jnp.pad((dcq - dck)[..., 0].transpose(0, 2, 1), ((0, 0), (0, 0), (0, SMALL_W - nh)))
    dlogf = _cumsum_seq(dcum, True, "fox_cumsum_bwd").reshape(t, SMALL_W)

    def prep_bwd(_, v, dlf, ddtv, fbv, dtbv):
        a_ = dlf * _sigmoid(-(v + fbv))
        b_ = ddtv * _sigmoid(v + dtbv)
        return a_ + b_, jnp.sum(a_, axis=0, keepdims=True), jnp.sum(b_, axis=0, keepdims=True)
    dps, dfb, ddtb = _rowwise(prep_bwd, [s['ps'], dlogf, ddt.reshape(t, SMALL_W)], [(SMALL_W, F32)],
                              bcast=[s['fb'], s['dtb']], reds=[(1, SMALL_W)] * 2, name="mix_prep_bwd")
    g['fox_forget_bias'] = dfb[0, :nh]
    g['ssd_dt_bias'] = ddtb[0, nh:nh + heads_ssd]

    dpm = jnp.concatenate([dq3.reshape(t, d), dk3.reshape(t, d), dv3.reshape(t, d),
                           dz.reshape(t, d_ssd), dxbc.reshape(t, d_xbc), dxr.reshape(t, d), dgate, dmerge], axis=1)
    dxn = _mm_nn(dps, p['w_small_t'], F32, name="mix_in_small_dx")
    dxn = _mm_nn(dpm, p['w_main_t'], F32, res=dxn, name="mix_in_main_dx")
    g['w_main'] = _mm_tn(s['xn'], dpm, name="mix_in_main_dw")
    g['w_small'] = _mm_tn(s['xn'], dps, name="mix_in_small_dw")
    dh, dg = _rms_bwd(s['h'], dxn, dout, p['mix_norm'][None, :], "mix_norm_bwd")
    g['mix_norm'] = dg[0]
    return dh, g


def _layer_params(w, li):
    p = {n: w[n][li] for n in WEIGHTS if n not in ('meta_tokens', 'final_norm')}
    bf = lambda a: a.astype(BF16)
    for tag in ('ffn1', 'ffn2'):
        wgu, wd = bf(p[tag + '_w_gate_up']), bf(p[tag + '_w_down'])
        f = wd.shape[0]
        p[tag] = dict(wg=wgu[:, :f], wu=wgu[:, f:], wg_t=wgu[:, :f].T, wu_t=wgu[:, f:].T, wd=wd, wd_t=wd.T)
    wm, ws = _w_in_split(bf(p['w_in']))
    p['w_main'], p['w_main_t'], p['w_small'], p['w_small_t'] = wm, wm.T, ws, ws.T
    for n in ('w_branch_attn', 'w_branch_ssd', 'w_branch_lru', 'w_out'):
        p[n + '_t'] = bf(p[n]).T
        p[n] = bf(p[n])
    wri = jnp.concatenate([_block_diag(p['lru_w_a']), _block_diag(p['lru_w_x'])], axis=1)
    p['lru_w_ri'], p['lru_w_ri_t'] = bf(wri), bf(wri).T
    return p


def _local_step(x, loss_target, w):
    b, seq, d = x.shape
    length = N_META + seq
    l = -(-length // Q_BLOCK) * Q_BLOCK
    t = b * l
    meta = jnp.broadcast_to(w['meta_tokens'].astype(F32)[None], (b, N_META, d))
    h = jnp.concatenate([meta, x, jnp.zeros((b, l - length, d), F32)], axis=1).reshape(t, d)
    tgt = jnp.concatenate([jnp.zeros((b, N_META, d), F32), loss_target, jnp.zeros((b, l - length, d), F32)],
                          axis=1).reshape(t, d)
    params, saves = [], []
    for li in range(DEPTH):
        p = _layer_params(w, li)
        h, s1 = _ffn_fwd(h, p['ffn1_norm'], p['ffn1'], "ffn1")
        h, sm = _mixer_fwd(h, p, b, l)
        h, s2 = _ffn_fwd(h, p['ffn2_norm'], p['ffn2'], "ffn2")
        params.append(p)
        saves.append((s1, sm, s2))
    dh, loss, dgf = _loss_head(h, tgt, w['final_norm'][None, :], l, "loss_head")
    layer_grads = [None] * DEPTH
    for li in reversed(range(DEPTH)):
        p = params[li]
        s1, sm, s2 = saves[li]
        g = {}
        dh, g['ffn2_norm'], g['ffn2_w_gate_up'], g['ffn2_w_down'] = _ffn_bwd(dh, s2, p['ffn2_norm'], p['ffn2'], "ffn2b")
        dh, gm = _mixer_bwd(dh, sm, p, b, l)
        g.update(gm)
        g['w_in'] = _w_in_merge(g.pop('w_main'), g.pop('w_small'))
        dh, g['ffn1_norm'], g['ffn1_w_gate_up'], g['ffn1_w_down'] = _ffn_bwd(dh, s1, p['ffn1_norm'], p['ffn1'], "ffn1b")
        layer_grads[li] = g
    grads = {n: jnp.stack([layer_grads[li][n] for li in range(DEPTH)]) for n in layer_grads[0]}
    for n in ('lru_w_a', 'lru_w_x'):
        grads[n] = grads[n].reshape(w[n].shape)
    dh3 = dh.reshape(b, l, d)
    grads['meta_tokens'] = jnp.sum(dh3[:, :N_META], axis=0)
    grads['final_norm'] = dgf[0]
    return loss, dh3[:, N_META:N_META + seq], grads


_ROW_W = 1024
_D2D_SPLIT = 16


def _half_size(total, row_quant):
    q = _ROW_W * row_quant
    return -(-(-(-total // 2)) // q) * q


def _flat_shard(parts, half):
    v = jnp.concatenate([p.reshape(-1) for p in parts])
    return jnp.pad(v, (0, 2 * half - v.shape[0]))


def _chip_shards(full, axis, nchip):
    return jnp.split(full, nchip, axis=axis + 1)


def _unflatten(flat, shapes):
    out, o = [], 0
    for sh in shapes:
        n = int(np.prod(sh))
        out.append(flat[o:o + n].reshape(sh))
        o += n
    return out


def kernel(x, meta_tokens, ffn1_norm, ffn1_w_gate_up, ffn1_w_down, mix_norm, w_in, fox_forget_bias, ssd_conv_w, ssd_conv_b, ssd_dt_bias, ssd_a_log, ssd_d, ssd_norm, lru_conv_w, lru_conv_b, lru_w_a, lru_b_a, lru_w_x, lru_b_x, lru_lambda, w_branch_attn, w_branch_ssd, w_branch_lru, w_out, ffn2_norm, ffn2_w_gate_up, ffn2_w_down, final_norm, loss_target, m_meta_tokens, m_ffn1_norm, m_ffn1_w_gate_up, m_ffn1_w_down, m_mix_norm, m_w_in, m_fox_forget_bias, m_ssd_conv_w, m_ssd_conv_b, m_ssd_dt_bias, m_ssd_a_log, m_ssd_d, m_ssd_norm, m_lru_conv_w, m_lru_conv_b, m_lru_w_a, m_lru_b_a, m_lru_w_x, m_lru_b_x, m_lru_lambda, m_w_branch_attn, m_w_branch_ssd, m_w_branch_lru, m_w_out, m_ffn2_norm, m_ffn2_w_gate_up, m_ffn2_w_down, m_final_norm, v_meta_tokens, v_ffn1_norm, v_ffn1_w_gate_up, v_ffn1_w_down, v_mix_norm, v_w_in, v_fox_forget_bias, v_ssd_conv_w, v_ssd_conv_b, v_ssd_dt_bias, v_ssd_a_log, v_ssd_d, v_ssd_norm, v_lru_conv_w, v_lru_conv_b, v_lru_w_a, v_lru_b_a, v_lru_w_x, v_lru_b_x, v_lru_lambda, v_w_branch_attn, v_w_branch_ssd, v_w_branch_lru, v_w_out, v_ffn2_norm, v_ffn2_w_gate_up, v_ffn2_w_down, v_final_norm):
    args = locals()
    wloc = {n: args[n] for n in WEIGHTS}
    mloc = {n: args['m_' + n] for n in WEIGHTS}
    vloc = {n: args['v_' + n] for n in WEIGHTS}
    nchip = 4
    chip = 2 * lax.axis_index("x") + lax.axis_index("y")
    core = lax.axis_index("c")
    row_quant = 256 if D_MODEL >= 1024 else 8

    big_shapes = [wloc[n].shape for n in BIG_NAMES]
    total = sum(int(np.prod(s)) for s in big_shapes)
    half = _half_size(total, row_quant)
    hrows = half // _ROW_W
    wflat = _flat_shard([wloc[n].astype(BF16) for n in BIG_NAMES], half).reshape(2, hrows, _ROW_W)
    my_half = lax.dynamic_index_in_dim(wflat, core, axis=0, keepdims=False)
    own = lambda out, mine, rank: lax.dynamic_update_index_in_dim(out, mine, rank, 0)
    got = _exchange(my_half, 'xy', False, "gather_w_chips", copy_own=False)
    got = own(got, my_half, chip).reshape(nchip * hrows, _ROW_W)
    both = own(_exchange(got, 'c', False, "gather_w_cores", nsplit=_D2D_SPLIT, copy_own=False), got, core)
    wall = both.reshape(2, nchip, hrows, _ROW_W).transpose(1, 0, 2, 3).reshape(nchip, 2 * half)
    full = {}
    per_chip = [_unflatten(wall[j], big_shapes) for j in range(nchip)]
    for i, n in enumerate(BIG_NAMES):
        full[n] = jnp.concatenate([per_chip[j][i] for j in range(nchip)], axis=BIG[n] + 1)
    cs_shapes = [wloc[n].shape for n in COLSHARD_SMALL]
    cs_total = sum(int(np.prod(s)) for s in cs_shapes)
    cs_rows = -(-cs_total // (8 * 128)) * 8
    cs_flat = jnp.concatenate([wloc[n].reshape(-1) for n in COLSHARD_SMALL])
    cs_flat = jnp.pad(cs_flat, (0, cs_rows * 128 - cs_total)).reshape(cs_rows, 128)
    cs_all = _exchange(cs_flat, 'xy', False, "gather_small").reshape(nchip, -1)
    cs_chip = [_unflatten(cs_all[j], cs_shapes) for j in range(nchip)]
    for i, n in enumerate(COLSHARD_SMALL):
        full[n] = jnp.concatenate([cs_chip[j][i] for j in range(nchip)], axis=-1)
    for n in SMALL_NAMES:
        if n not in COLSHARD_SMALL:
            full[n] = wloc[n]

    loss_part, grad_x, grads = _local_step(x, loss_target, full)

    gsh = [[s for s in _chip_shards(grads[n], BIG[n], nchip)] for n in BIG_NAMES]
    gflat = jnp.stack([_flat_shard([gsh[i][j] for i in range(len(BIG_NAMES))], half) for j in range(nchip)])
    gflat = gflat.reshape(nchip, 2, hrows, _ROW_W).transpose(1, 0, 2, 3).reshape(2, nchip * hrows, _ROW_W)
    give = lax.dynamic_index_in_dim(gflat, 1 - core, axis=0, keepdims=False)
    keep = lax.dynamic_index_in_dim(gflat, core, axis=0, keepdims=False)
    pair = _exchange(give, 'c', False, "reduce_cores", nsplit=_D2D_SPLIT, copy_own=False)
    theirs = lax.dynamic_index_in_dim(pair, 1 - core, axis=0, keepdims=False)
    psum = _sum_rows([keep, theirs], BF16, "reduce_cores_sum").reshape(nchip, hrows, _ROW_W)
    parts = _exchange(psum, 'xy', True, "reduce_chips", copy_own=False)
    parts = own(parts, lax.dynamic_index_in_dim(psum, chip, axis=0, keepdims=False), chip)
    rsum = _sum_rows([parts[j] for j in range(nchip)], F32, "reduce_chips_sum")
    halves = own(_exchange(rsum, 'c', False, "reduce_share", nsplit=_D2D_SPLIT, copy_own=False), rsum, core)
    gshard = halves.reshape(-1)
    gbig = dict(zip(BIG_NAMES, _unflatten(gshard, big_shapes)))

    sm_shapes = [grads[n].shape for n in SMALL_NAMES]
    sm_total = sum(int(np.prod(s)) for s in sm_shapes) + 128
    sm_rows = -(-sm_total // (8 * 128)) * 8
    sm_flat = jnp.concatenate([loss_part.reshape(-1)] + [grads[n].reshape(-1) for n in SMALL_NAMES])
    sm_flat = jnp.pad(sm_flat, (0, sm_rows * 128 - sm_total)).reshape(sm_rows, 128)
    sm_all = _exchange(sm_flat, 'xyc', False, "gather_small_grads")
    sm_sum = _sum_rows([sm_all[j] for j in range(8)], F32, "small_grads_sum").reshape(-1)
    loss = sm_sum[0]
    gsmall_full = dict(zip(SMALL_NAMES, _unflatten(sm_sum[128:], sm_shapes)))
    gsmall = {}
    for n in SMALL_NAMES:
        gfull = gsmall_full[n]
        if n in COLSHARD_SMALL:
            wcols = wloc[n].shape[-1]
            gfull = lax.dynamic_slice_in_dim(gfull, chip * wcols, wcols, axis=gfull.ndim - 1)
        gsmall[n] = gfull

    def flat_big(dct):
        return _flat_shard([dct[n] for n in BIG_NAMES], half).reshape(2 * hrows, _ROW_W)
    dl, mn, vn = _adamw(flat_big(wloc), gshard.reshape(2 * hrows, _ROW_W), flat_big(mloc), flat_big(vloc), "adamw_big")
    big_out = [dict(zip(BIG_NAMES, _unflatten(a.reshape(-1), big_shapes))) for a in (dl, mn, vn)]
    loc_shapes = [wloc[n].shape for n in SMALL_NAMES]
    loc_total = sum(int(np.prod(s)) for s in loc_shapes)
    loc_rows = -(-loc_total // (8 * 128)) * 8

    def flat_small(dct):
        v = jnp.concatenate([dct[n].reshape(-1) for n in SMALL_NAMES])
        return jnp.pad(v, (0, loc_rows * 128 - loc_total)).reshape(loc_rows, 128)
    dls, mns, vns = _adamw(flat_small(wloc), flat_small(gsmall), flat_small(mloc), flat_small(vloc), "adamw_small")
    small_out = [dict(zip(SMALL_NAMES, _unflatten(a.reshape(-1), loc_shapes))) for a in (dls, mns, vns)]

    grad_w = {**gbig, **gsmall}
    outs = [loss, grad_x] + [grad_w[n] for n in WEIGHTS]
    for k in range(3):
        merged = {**big_out[k], **small_out[k]}
        outs += [merged[n] for n in WEIGHTS]
    return tuple(outs)
```

```python
import functools
import math

import numpy as np
import jax
import jax.numpy as jnp
from jax import lax
from jax.experimental import pallas as pl
from jax.experimental.pallas import tpu as pltpu

F32 = jnp.float32
BF16 = jnp.bfloat16
HI = lax.Precision.HIGHEST
VMEM_LIMIT_BYTES = 56 * 1024 * 1024
NEG = -1e30

D_MODEL = 1024
SEQ = 4096
DEPTH = 4
N_META = 16
Q_BLOCK = 128
SSD_CHUNK = 128
NORM_EPS = 1e-6
ATTN_HEADS = 16
ATTN_HEAD_DIM = 64
SSD_HEAD_DIM = 64
SSD_GROUPS = 2
SSD_STATE = 128
CONV_K = 4
LRU_BLOCKS = 16
LRU_C = 8.0
D_FF = 2816
ADAM_LR = 0.001
ADAM_B1 = 0.9
ADAM_B2 = 0.999
ADAM_EPS = 1e-08
ADAM_WD = 0.01
ADAM_STEP = 10
SMALL_W = 128
_ROWWISE_TILE_ELEMS = 512 * 1024

WEIGHTS = ['meta_tokens', 'ffn1_norm', 'ffn1_w_gate_up', 'ffn1_w_down', 'mix_norm', 'w_in', 'fox_forget_bias',
           'ssd_conv_w', 'ssd_conv_b', 'ssd_dt_bias', 'ssd_a_log', 'ssd_d', 'ssd_norm', 'lru_conv_w', 'lru_conv_b',
           'lru_w_a', 'lru_b_a', 'lru_w_x', 'lru_b_x', 'lru_lambda', 'w_branch_attn', 'w_branch_ssd', 'w_branch_lru',
           'w_out', 'ffn2_norm', 'ffn2_w_gate_up', 'ffn2_w_down', 'final_norm']
BIG = {'ffn1_w_gate_up': 1, 'ffn1_w_down': 0, 'w_in': 1, 'w_branch_attn': 0, 'w_branch_ssd': 0, 'w_branch_lru': 0,
       'w_out': 0, 'ffn2_w_gate_up': 1, 'ffn2_w_down': 0}
BIG_NAMES = [n for n in WEIGHTS if n in BIG]
COLSHARD_SMALL = ['meta_tokens', 'ssd_conv_w', 'lru_conv_w']
SMALL_NAMES = [n for n in WEIGHTS if n not in BIG]


def _pick(n, cands):
    for c in cands:
        if n % c == 0:
            return c
    raise ValueError(f"no tile for {n} in {cands}")


def _pcall(body, **kw):
    return pl.pallas_call(body, **kw)


def _cparams(sem):
    return pltpu.CompilerParams(dimension_semantics=sem, vmem_limit_bytes=VMEM_LIMIT_BYTES)


def _sds(shape, dtype):
    return jax.ShapeDtypeStruct(tuple(shape), dtype)


def _dot(a, b, hi=False):
    return jnp.dot(a, b, precision=HI if hi else None, preferred_element_type=F32)


def _dot_nt(a, b):
    return lax.dot_general(a, b, (((1,), (1,)), ((), ())), preferred_element_type=F32)


def _dot_tn(a, b):
    return lax.dot_general(a, b, (((0,), (0,)), ((), ())), preferred_element_type=F32)


def _sigmoid(x):
    return 1.0 / (1.0 + jnp.exp(-x))


def _softplus(x):
    return jnp.maximum(x, 0.0) + jnp.log1p(jnp.exp(-jnp.abs(x)))


def _silu(x):
    return x * _sigmoid(x)


def _dsilu(x):
    s = _sigmoid(x)
    return s * (1.0 + x * (1.0 - s))


_GELU_C = math.sqrt(2.0 / math.pi)


def _gelu(x):
    return 0.5 * x * (1.0 + jnp.tanh(_GELU_C * (x + 0.044715 * x * x * x)))


def _dgelu(x):
    t = jnp.tanh(_GELU_C * (x + 0.044715 * x * x * x))
    return 0.5 * (1.0 + t) + 0.5 * x * (1.0 - t * t) * _GELU_C * (1.0 + 3.0 * 0.044715 * x * x)


def _expm1(x):
    series = x * (1.0 + x * 0.5 * (1.0 + x * (1.0 / 3.0) * (1.0 + x * 0.25 * (1.0 + x * 0.2))))
    return jnp.where(jnp.abs(x) < 0.05, series, jnp.exp(x) - 1.0)


def _rowwise(fn, ins, outs, *, bcast=(), reds=(), tm=None, name, period=None):
    t_rows = ins[0].shape[0]
    if tm is None:
        widest = max([a.shape[1] for a in ins] + [c for c, _ in outs])
        tm = _pick(math.gcd(t_rows, period or t_rows),
                   [c for c in (384, 256, 128, 64, 32, 16, 8) if c * widest <= _ROWWISE_TILE_ELEMS or c == 8])
    nt = t_rows // tm
    assert t_rows % tm == 0 and (period is None or period % tm == 0)
    n_in, n_out = len(ins) + len(bcast), len(outs)

    def body(*refs):
        i = pl.program_id(0)
        pos = None
        if period is not None:
            pos = (i * tm) % period + lax.broadcasted_iota(jnp.int32, (tm, 1), 0)
        res = fn(pos, *[r[...] for r in refs[:n_in]])
        res = res if isinstance(res, tuple) else (res,)
        for r, v in zip(refs[n_in:n_in + n_out], res[:n_out]):
            r[...] = v.astype(r.dtype)
        red_refs = refs[n_in + n_out:]
        if red_refs:
            @pl.when(i == 0)
            def _():
                for r in red_refs:
                    r[...] = jnp.zeros_like(r)
            for r, v in zip(red_refs, res[n_out:]):
                r[...] += v

    in_specs = [pl.BlockSpec((tm, a.shape[1]), lambda i: (i, 0)) for a in ins]
    in_specs += [pl.BlockSpec(b.shape, lambda i, n=b.ndim: (0,) * n) for b in bcast]
    out_specs = [pl.BlockSpec((tm, c), lambda i: (i, 0)) for c, _ in outs]
    out_specs += [pl.BlockSpec(s, lambda i: (0, 0)) for s in reds]
    out_shape = [_sds((t_rows, c), dt) for c, dt in outs] + [_sds(s, F32) for s in reds]
    res = _pcall(body, name=name, grid=(nt,), in_specs=in_specs, out_specs=out_specs, out_shape=out_shape,
                 compiler_params=_cparams(("arbitrary",) if reds else ("parallel",)))(*ins, *bcast)
    return res


_TM = (768, 384, 256, 128)
_TN = (1536, 1408, 1024, 768, 512, 640, 384, 256, 128)
_TK = (1536, 1024, 1408, 512, 384, 256, 128)


def _mm_nn(a, b, out_dtype, *, res=None, alpha=1.0, name):
    m, k = a.shape
    k2, n = b.shape
    assert k == k2
    tm, tn, tk = _pick(m, _TM), _pick(n, _TN), _pick(k, _TK)
    nk = k // tk

    def body(*refs):
        if res is None:
            a_ref, b_ref, o_ref, acc = refs
            r_ref = None
        else:
            a_ref, b_ref, r_ref, o_ref, acc = refs
        kk = pl.program_id(2)

        @pl.when(kk == 0)
        def _():
            acc[...] = jnp.zeros_like(acc)

        acc[...] += _dot(a_ref[...].astype(BF16), b_ref[...].astype(BF16))

        @pl.when(kk == nk - 1)
        def _():
            v = acc[...]
            if alpha != 1.0:
                v = v * alpha
            if r_ref is not None:
                v = r_ref[...].astype(F32) + v
            o_ref[...] = v.astype(o_ref.dtype)

    in_specs = [pl.BlockSpec((tm, tk), lambda j, i, kk: (i, kk)), pl.BlockSpec((tk, tn), lambda j, i, kk: (kk, j))]
    args = [a, b]
    if res is not None:
        in_specs.append(pl.BlockSpec((tm, tn), lambda j, i, kk: (i, j)))
        args.append(res)
    return _pcall(body, name=name, grid=(n // tn, m // tm, nk), in_specs=in_specs,
                  out_specs=pl.BlockSpec((tm, tn), lambda j, i, kk: (i, j)), out_shape=_sds((m, n), out_dtype),
                  scratch_shapes=[pltpu.VMEM((tm, tn), F32)],
                  compiler_params=_cparams(("parallel", "parallel", "arbitrary")))(*args)


def _mm_swiglu(a, wg, wu, name):
    m, k = a.shape
    f = wg.shape[1]
    tm, tn, tk = _pick(m, _TM), _pick(f, _TN), _pick(k, _TK)
    nk = k // tk

    def body(a_ref, g_w, u_w, g_ref, u_ref, act_ref, accg, accu):
        kk = pl.program_id(2)

        @pl.when(kk == 0)
        def _():
            accg[...] = jnp.zeros_like(accg)
            accu[...] = jnp.zeros_like(accu)

        av = a_ref[...].astype(BF16)
        accg[...] += _dot(av, g_w[...])
        accu[...] += _dot(av, u_w[...])

        @pl.when(kk == nk - 1)
        def _():
            g, u = accg[...], accu[...]
            g_ref[...] = g.astype(g_ref.dtype)
            u_ref[...] = u.astype(u_ref.dtype)
            act_ref[...] = (_silu(g) * u).astype(act_ref.dtype)

    wspec = pl.BlockSpec((tk, tn), lambda j, i, kk: (kk, j))
    ospec = pl.BlockSpec((tm, tn), lambda j, i, kk: (i, j))
    return _pcall(body, name=name, grid=(f // tn, m // tm, nk),
                  in_specs=[pl.BlockSpec((tm, tk), lambda j, i, kk: (i, kk)), wspec, wspec],
                  out_specs=[ospec] * 3, out_shape=[_sds((m, f), BF16)] * 3,
                  scratch_shapes=[pltpu.VMEM((tm, tn), F32)] * 2,
                  compiler_params=_cparams(("parallel", "parallel", "arbitrary")))(a, wg, wu)


def _mm_dswiglu(dout, wd_t, g, u, alpha, name):
    m, k = dout.shape
    f = wd_t.shape[1]
    tm, tn, tk = _pick(m, _TM), _pick(f, _TN), _pick(k, _TK)
    nk = k // tk

    def body(a_ref, w_ref, g_ref, u_ref, dg_ref, du_ref, acc):
        kk = pl.program_id(2)

        @pl.when(kk == 0)
        def _():
            acc[...] = jnp.zeros_like(acc)

        acc[...] += _dot(a_ref[...].astype(BF16), w_ref[...])

        @pl.when(kk == nk - 1)
        def _():
            dact = acc[...] * alpha
            gv, uv = g_ref[...].astype(F32), u_ref[...].astype(F32)
            dg_ref[...] = (dact * uv * _dsilu(gv)).astype(dg_ref.dtype)
            du_ref[...] = (dact * _silu(gv)).astype(du_ref.dtype)

    ospec = pl.BlockSpec((tm, tn), lambda j, i, kk: (i, j))
    return _pcall(body, name=name, grid=(f // tn, m // tm, nk),
                  in_specs=[pl.BlockSpec((tm, tk), lambda j, i, kk: (i, kk)),
                            pl.BlockSpec((tk, tn), lambda j, i, kk: (kk, j)), ospec, ospec],
                  out_specs=[ospec] * 2, out_shape=[_sds((m, f), BF16)] * 2,
                  scratch_shapes=[pltpu.VMEM((tm, tn), F32)],
                  compiler_params=_cparams(("parallel", "parallel", "arbitrary")))(dout, wd_t, g, u)


def _mm_tn(a, b, *, alpha=1.0, name):
    m, k = a.shape
    m2, n = b.shape
    assert m == m2
    tm, tn, tko = _pick(m, _TM), _pick(n, _TN), _pick(k, _TK)
    nm = m // tm

    def body(a_ref, b_ref, o_ref, acc):
        mm = pl.program_id(2)

        @pl.when(mm == 0)
        def _():
            acc[...] = jnp.zeros_like(acc)

        acc[...] += _dot_tn(a_ref[...].astype(BF16), b_ref[...].astype(BF16))

        @pl.when(mm == nm - 1)
        def _():
            v = acc[...]
            o_ref[...] = v * alpha if alpha != 1.0 else v

    return _pcall(body, name=name, grid=(k // tko, n // tn, nm),
                  in_specs=[pl.BlockSpec((tm, tko), lambda i, j, mm: (mm, i)),
                            pl.BlockSpec((tm, tn), lambda i, j, mm: (mm, j))],
                  out_specs=pl.BlockSpec((tko, tn), lambda i, j, mm: (i, j)), out_shape=_sds((k, n), F32),
                  scratch_shapes=[pltpu.VMEM((tko, tn), F32)],
                  compiler_params=_cparams(("parallel", "parallel", "arbitrary")))(a, b)


def _rms_fwd(h, g, name):
    def fn(_, hv, gv):
        r = lax.rsqrt(jnp.mean(hv * hv, axis=1, keepdims=True) + NORM_EPS)
        return hv * r * gv
    return _rowwise(fn, [h], [(h.shape[1], BF16)], bcast=[g], name=name)[0]


def _rms_bwd(h, dxn, dres, g, name):
    d = h.shape[1]

    def fn(_, hv, dv, rv, gv):
        r = lax.rsqrt(jnp.mean(hv * hv, axis=1, keepdims=True) + NORM_EPS)
        xh = hv * r
        dxh = dv * gv
        dh = r * (dxh - xh * jnp.mean(dxh * xh, axis=1, keepdims=True))
        return rv + dh, jnp.sum(dv * xh, axis=0, keepdims=True)
    return _rowwise(fn, [h, dxn, dres], [(d, F32)], bcast=[g], reds=[(1, d)], name=name)


def _merge_fwd(mg, ba, bb, bc, name):
    d = ba.shape[1]

    def fn(_, m, a, b, c):
        g = _sigmoid(m.astype(F32))
        return g[:, :d] * a.astype(F32) + g[:, d:2 * d] * b.astype(F32) + g[:, 2 * d:] * c.astype(F32)
    return _rowwise(fn, [mg, ba, bb, bc], [(d, BF16)], name=name)[0]


def _merge_bwd(mg, ba, bb, bc, dmix, name):
    d = ba.shape[1]

    def fn(_, m, a, b, c, dm):
        g = _sigmoid(m.astype(F32))
        dm = dm.astype(F32)
        br = (a.astype(F32), b.astype(F32), c.astype(F32))
        douts, dgs = [], []
        for j in range(3):
            gj = g[:, j * d:(j + 1) * d]
            douts.append(dm * gj)
            dgs.append(dm * br[j] * gj * (1.0 - gj))
        return (*douts, jnp.concatenate(dgs, axis=1))
    return _rowwise(fn, [mg, ba, bb, bc, dmix], [(d, BF16)] * 3 + [(3 * d, BF16)], name=name)


def _loss_head(h, tgt, g, seq_len, name):
    d = h.shape[1]

    def fn(pos, hv, tv, gv):
        r = lax.rsqrt(jnp.mean(hv * hv, axis=1, keepdims=True) + NORM_EPS)
        xh = hv * r
        real = (pos >= N_META) & (pos < N_META + SEQ)
        e = jnp.where(real, xh * gv - tv, 0.0)
        part = jnp.sum(jnp.sum(e * e, axis=1, keepdims=True), axis=0, keepdims=True) * (0.5 / d)
        dy = e * (1.0 / d)
        dxh = dy * gv
        dh = r * (dxh - xh * jnp.mean(dxh * xh, axis=1, keepdims=True))
        return dh, jnp.broadcast_to(part, (1, 128)), jnp.sum(dy * xh, axis=0, keepdims=True)
    return _rowwise(fn, [h, tgt], [(d, F32)], bcast=[g], reds=[(1, 128), (1, d)], name=name, period=seq_len)


def _adamw(w, g, m, v, name):
    c1 = 1.0 - ADAM_B1 ** ADAM_STEP
    c2 = 1.0 - ADAM_B2 ** ADAM_STEP
    wd = w.shape[1]

    def fn(_, wv, gv, mv, vv):
        mn = ADAM_B1 * mv + (1.0 - ADAM_B1) * gv
        vn = ADAM_B2 * vv + (1.0 - ADAM_B2) * (gv * gv)
        delta = -ADAM_LR * ((mn / c1) / (jnp.sqrt(vn / c2) + ADAM_EPS) + ADAM_WD * wv)
        return delta, mn, vn
    return _rowwise(fn, [w, g, m, v], [(wd, F32)] * 3, name=name)


def _sum_rows(parts, out_dtype, name):
    def fn(_, *vs):
        acc = vs[0].astype(F32)
        for v in vs[1:]:
            acc = acc + v.astype(F32)
        return acc
    return _rowwise(fn, list(parts), [(parts[0].shape[1], out_dtype)], name=name)[0]


def _cumsum_seq(x, reverse, name):
    b, l, w = x.shape
    q = 128
    nc = l // q

    def body(x_ref, o_ref):
        row = lax.broadcasted_iota(jnp.int32, (q, q), 0)
        col = lax.broadcasted_iota(jnp.int32, (q, q), 1)
        tri = ((row <= col) if reverse else (row >= col)).astype(F32)
        rsel = lax.broadcasted_iota(jnp.int32, (q, w), 0) == (0 if reverse else q - 1)

        def step(i, carry):
            j = (nc - 1 - i) if reverse else i
            start = pl.multiple_of(j * q, q)
            cs = _dot(tri, x_ref[pl.ds(start, q), :], hi=True) + carry
            o_ref[pl.ds(start, q), :] = cs
            return jnp.sum(jnp.where(rsel, cs, 0.0), axis=0, keepdims=True)

        lax.fori_loop(0, nc, step, jnp.zeros((1, w), F32))

    return _pcall(body, name=name, grid=(b,), in_specs=[pl.BlockSpec((None, l, w), lambda i: (i, 0, 0))],
                  out_specs=pl.BlockSpec((None, l, w), lambda i: (i, 0, 0)), out_shape=_sds(x.shape, F32),
                  compiler_params=_cparams(("parallel",)))(x)


_HALO = 16


def _conv_tiles(l, c):
    return _pick(l, (384, 256, 128)), _pick(c, (512, 256, 128))


def _conv_fwd(x, w, bias, out_dtype, name):
    b, l, c = x.shape
    tt, cw = _conv_tiles(l, c)

    def body(x_ref, h_ref, w_ref, b_ref, o_ref):
        t = pl.program_id(2)
        halo = jnp.where(t == 0, 0.0, h_ref[...].astype(F32))
        xe = jnp.concatenate([halo, x_ref[...].astype(F32)], axis=0)
        wv = w_ref[...]
        acc = b_ref[...] + wv[CONV_K - 1:CONV_K, :] * xe[_HALO:]
        for j in range(CONV_K - 1):
            acc = acc + wv[j:j + 1, :] * pltpu.roll(xe, CONV_K - 1 - j, 0)[_HALO:]
        o_ref[...] = acc.astype(o_ref.dtype)

    return _pcall(body, name=name, grid=(b, c // cw, l // tt),
                  in_specs=[pl.BlockSpec((None, tt, cw), lambda i, j, t: (i, t, j)),
                            pl.BlockSpec((None, _HALO, cw), lambda i, j, t: (i, jnp.maximum(t * (tt // _HALO) - 1, 0), j)),
                            pl.BlockSpec((CONV_K, cw), lambda i, j, t: (0, j)),
                            pl.BlockSpec((1, cw), lambda i, j, t: (0, j))],
                  out_specs=pl.BlockSpec((None, tt, cw), lambda i, j, t: (i, t, j)), out_shape=_sds(x.shape, out_dtype),
                  compiler_params=_cparams(("parallel", "parallel", "parallel")))(x, x, w, bias)


def _conv_bwd(x, dy, w, name):
    b, l, c = x.shape
    tt, cw = _conv_tiles(l, c)
    nt = l // tt

    def body(x_ref, xh_ref, d_ref, dh_ref, w_ref, dx_ref, dw_ref):
        i, t = pl.program_id(1), pl.program_id(2)
        halo = jnp.where(t == 0, 0.0, xh_ref[...].astype(F32))
        xe = jnp.concatenate([halo, x_ref[...].astype(F32)], axis=0)
        dv = d_ref[...].astype(F32)
        nxt = jnp.where(t == nt - 1, 0.0, dh_ref[...].astype(F32))
        de = jnp.concatenate([dv, nxt], axis=0)
        wv = w_ref[...]
        dx = wv[CONV_K - 1:CONV_K, :] * dv
        rowid = lax.broadcasted_iota(jnp.int32, (8, 1), 0)
        part = jnp.where(rowid == CONV_K, jnp.sum(dv, axis=0, keepdims=True), 0.0)
        part = part + jnp.where(rowid == CONV_K - 1, jnp.sum(dv * xe[_HALO:], axis=0, keepdims=True), 0.0)
        for j in range(CONV_K - 1):
            s = CONV_K - 1 - j
            dx = dx + wv[j:j + 1, :] * pltpu.roll(de, tt + _HALO - s, 0)[:tt]
            xs = pltpu.roll(xe, s, 0)[_HALO:]
            part = part + jnp.where(rowid == j, jnp.sum(dv * xs, axis=0, keepdims=True), 0.0)
        dx_ref[...] = dx.astype(dx_ref.dtype)

        @pl.when((i == 0) & (t == 0))
        def _():
            dw_ref[...] = jnp.zeros_like(dw_ref)
        dw_ref[...] += part

    return _pcall(body, name=name, grid=(c // cw, b, nt),
                  in_specs=[pl.BlockSpec((None, tt, cw), lambda j, i, t: (i, t, j)),
                            pl.BlockSpec((None, _HALO, cw), lambda j, i, t: (i, jnp.maximum(t * (tt // _HALO) - 1, 0), j)),
                            pl.BlockSpec((None, tt, cw), lambda j, i, t: (i, t, j)),
                            pl.BlockSpec((None, _HALO, cw),
                                         lambda j, i, t: (i, jnp.minimum((t + 1) * (tt // _HALO), l // _HALO - 1), j)),
                            pl.BlockSpec((CONV_K, cw), lambda j, i, t: (0, j))],
                  out_specs=[pl.BlockSpec((None, tt, cw), lambda j, i, t: (i, t, j)),
                             pl.BlockSpec((8, cw), lambda j, i, t: (0, j))],
                  out_shape=[_sds(x.shape, BF16), _sds((8, c), F32)],
                  compiler_params=_cparams(("parallel", "arbitrary", "arbitrary")))(x, x, dy, dy, w)


def _linear_scan(a, u, reverse, name):
    b, l, c = a.shape
    tt = 128
    cw = _pick(c, (512, 256, 128))
    nt = l // tt

    def body(a_ref, u_ref, h_ref, carry):
        t = pl.program_id(2)

        @pl.when(t == 0)
        def _():
            carry[...] = jnp.zeros_like(carry)

        av, uv = a_ref[...], u_ref[...]
        row = lax.broadcasted_iota(jnp.int32, (tt, cw), 0)
        k = 1
        while k < tt:
            if reverse:
                keep = row < tt - k
                a_sh = jnp.where(keep, pltpu.roll(av, tt - k, 0), 1.0)
                u_sh = jnp.where(keep, pltpu.roll(uv, tt - k, 0), 0.0)
            else:
                keep = row >= k
                a_sh = jnp.where(keep, pltpu.roll(av, k, 0), 1.0)
                u_sh = jnp.where(keep, pltpu.roll(uv, k, 0), 0.0)
            uv = uv + av * u_sh
            av = av * a_sh
            k *= 2
        hv = uv + av * carry[0:1, :]
        h_ref[...] = hv
        edge = jnp.sum(jnp.where(row == (0 if reverse else tt - 1), hv, 0.0), axis=0, keepdims=True)
        carry[...] = jnp.broadcast_to(edge, carry.shape)

    tmap = (lambda i, j, t: (i, nt - 1 - t, j)) if reverse else (lambda i, j, t: (i, t, j))
    spec = pl.BlockSpec((None, tt, cw), tmap)
    return _pcall(body, name=name, grid=(b, c // cw, nt), in_specs=[spec, spec], out_specs=spec,
                  out_shape=_sds(a.shape, F32), scratch_shapes=[pltpu.VMEM((8, cw), F32)],
                  compiler_params=_cparams(("parallel", "parallel", "arbitrary")))(a, u)


ATTN_W = 128
_AUG_C = 0
_AUG_ONE = 3
_AUG_LSE = 6


def _attn_blk(l):
    return _pick(l, (384, 256, 128))


def _split3(x):
    x1 = x.astype(BF16).astype(F32)
    x2 = (x - x1).astype(BF16).astype(F32)
    x3 = (x - x1 - x2).astype(BF16).astype(F32)
    return x1, x2, x3


def _aug_lanes(lane, base, vals):
    out = 0.0
    for k, v in enumerate(vals):
        out = jnp.where(lane == base + k, v, out)
    return out


def _pair_specs(blk, nb):
    at = lambda ww: pl.BlockSpec((None, 2, blk, ww), lambda bi, p, i: (bi, p, i, 0))
    whole = pl.BlockSpec((None, 2, nb, blk, ATTN_W), lambda bi, p, i: (bi, p, 0, 0, 0))
    rows = pl.BlockSpec((None, blk, ATTN_W), lambda bi, p, i: (bi, i, p))
    return at, whole, rows


def _attn_prep(pm3, cum, col0, name):
    b, l, _ = pm3.shape
    dh, nh = ATTN_HEAD_DIM, ATTN_HEADS
    blk = _attn_blk(l)
    scale = dh ** -0.5

    def body(q_ref, k_ref, v_ref, c_ref, qa_ref, ka_ref, va_ref):
        pair = pl.program_id(1)
        lane = lax.broadcasted_iota(jnp.int32, (1, ATTN_W), 1)
        head = lane < dh
        cv = c_ref[...]
        qf, kf, vf = (r[...].astype(F32) for r in (q_ref, k_ref, v_ref))
        for e in range(2):
            c1, c2, c3 = _split3(jnp.sum(jnp.where(lane == 2 * pair + e, cv, 0.0), axis=1, keepdims=True))
            qe, ke, ve = (pltpu.roll(t, dh, 1) for t in (qf, kf, vf)) if e else (qf, kf, vf)
            qa_ref[e] = jnp.where(head, qe * scale, _aug_lanes(lane, dh, (c1, c2, c3, 1.0, 1.0, 1.0))).astype(BF16)
            ka_ref[e] = jnp.where(head, ke, _aug_lanes(lane, dh, (1.0, 1.0, 1.0, -c1, -c2, -c3, 1.0, 1.0, 1.0))).astype(BF16)
            va_ref[e] = jnp.where(head, ve, _aug_lanes(lane, dh, (1.0, 1.0, 1.0))).astype(BF16)

    at, _, _ = _pair_specs(blk, l // blk)
    cols = lambda c0: pl.BlockSpec((None, blk, ATTN_W), lambda bi, p, i, c0=c0: (bi, i, c0 // ATTN_W + p))
    return _pcall(body, name=name, grid=(b, nh // 2, l // blk),
                  in_specs=[cols(col0[0]), cols(col0[1]), cols(col0[2]),
                            pl.BlockSpec((None, blk, SMALL_W), lambda bi, p, i: (bi, i, 0))],
                  out_specs=[at(ATTN_W)] * 3, out_shape=[_sds((b, nh, l, ATTN_W), BF16)] * 3,
                  compiler_params=_cparams(("parallel", "parallel", "parallel")))(pm3, pm3, pm3, cum)


def _attn_prep_bwd(dy3, y3, qa, lse, name):
    b, nh, l, _ = qa.shape
    dh = ATTN_HEAD_DIM
    blk = _attn_blk(l)

    def body(dy_ref, y_ref, qa_ref, lse_ref, qa2_ref, doa_ref):
        lane = lax.broadcasted_iota(jnp.int32, (1, ATTN_W), 1)
        dyf = dy_ref[...].astype(F32)
        prod = dyf * y_ref[...].astype(F32)
        for e in range(2):
            mine = (lane >= dh) if e else (lane < dh)
            d1, d2, d3 = _split3(jnp.sum(jnp.where(mine, prod, 0.0), axis=1, keepdims=True))
            l1, l2, l3 = _split3(lse_ref[e])
            dye = pltpu.roll(dyf, dh, 1) if e else dyf
            doa_ref[e] = jnp.where(lane < dh, dye, _aug_lanes(lane, dh, (-d1, -d2, -d3))).astype(BF16)
            on_lse = (lane >= dh + _AUG_LSE) & (lane < dh + _AUG_LSE + 3)
            qa2_ref[e] = jnp.where(on_lse, _aug_lanes(lane, dh + _AUG_LSE, (-l1, -l2, -l3)),
                                   qa_ref[e].astype(F32)).astype(BF16)

    at, _, rows = _pair_specs(blk, l // blk)
    return _pcall(body, name=name, grid=(b, nh // 2, l // blk), in_specs=[rows, rows, at(ATTN_W), at(1)],
                  out_specs=[at(ATTN_W)] * 2, out_shape=[_sds(qa.shape, BF16)] * 2,
                  compiler_params=_cparams(("parallel", "parallel", "parallel")))(dy3, y3, qa, lse)


def _flash_fwd(qa, ka, va, d_model, name):
    b, h, l, w = qa.shape
    dh = ATTN_HEAD_DIM
    blk = _attn_blk(l)
    nb = l // blk
    kr, vr = ka.reshape(b, h, nb, blk, w), va.reshape(b, h, nb, blk, w)

    def body(q_ref, k_ref, v_ref, o_ref, lse_ref):
        i = pl.program_id(2)
        row = lax.broadcasted_iota(jnp.int32, (blk, blk), 0)
        col = lax.broadcasted_iota(jnp.int32, (blk, blk), 1)

        def scores(e, j):
            return _dot_nt(q_ref[e], k_ref[e, j])

        def consume(e, j, s, m, acc):
            mn = jnp.maximum(m, jnp.max(s, axis=1, keepdims=True))
            return mn, jnp.exp(m - mn) * acc + _dot(jnp.exp(s - mn).astype(BF16), v_ref[e, j])

        def step(j, carry):
            out = []
            for e in range(2):
                m, acc, s = carry[3 * e:3 * e + 3]
                s_next = scores(e, j + 1)
                out += [*consume(e, j, s, m, acc), s_next]
            return tuple(out)

        init = tuple(t for e in range(2)
                     for t in (jnp.full((blk, 1), NEG, F32), jnp.zeros((blk, w), F32), scores(e, 0)))
        carry = lax.fori_loop(0, i, step, init)
        m0, a0 = consume(0, i, jnp.where(col <= row, carry[2], NEG), carry[0], carry[1])
        m1, a1 = consume(1, i, jnp.where(col <= row, carry[5], NEG), carry[3], carry[4])
        l0, l1 = a0[:, dh:dh + 1], a1[:, dh:dh + 1]
        lane = lax.broadcasted_iota(jnp.int32, (1, w), 1)
        o_ref[...] = jnp.where(lane < dh, a0 / l0, pltpu.roll(a1 / l1, dh, 1)).astype(o_ref.dtype)
        lse_ref[0] = m0 + jnp.log(l0)
        lse_ref[1] = m1 + jnp.log(l1)

    at, whole, rows = _pair_specs(blk, nb)
    return _pcall(body, name=name, grid=(b, h // 2, nb), in_specs=[at(w), whole, whole],
                  out_specs=[rows, at(1)], out_shape=[_sds((b, l, d_model), BF16), _sds((b, h, l, 1), F32)],
                  compiler_params=_cparams(("parallel", "parallel", "parallel")))(qa, kr, vr)


def _flash_bwd(qa, ka, va, doa, d_model, name):
    b, h, l, w = qa.shape
    dh = ATTN_HEAD_DIM
    blk = _attn_blk(l)
    nb = l // blk
    scale = dh ** -0.5
    r5 = lambda t: t.reshape(b, h, nb, blk, w)

    def body(k_ref, v_ref, q_ref, do_ref, dq_ref, dk_ref, dv_ref, dcq_ref, dck_ref, dq_acc):
        j = pl.program_id(2)
        row = lax.broadcasted_iota(jnp.int32, (blk, blk), 0)
        col = lax.broadcasted_iota(jnp.int32, (blk, blk), 1)
        lane = lax.broadcasted_iota(jnp.int32, (1, w), 1)

        @pl.when(j == 0)
        def _():
            dq_acc[...] = jnp.zeros_like(dq_acc)

        def contrib(i, masked, carry):
            out = []
            for e in range(2):
                qv, dov = q_ref[e, i], do_ref[e, i]
                p = jnp.exp(_dot_nt(qv, k_ref[e]))
                if masked:
                    p = jnp.where(col <= row, p, 0.0)
                ds = (p * _dot_nt(dov, v_ref[e])).astype(BF16)
                dq_acc[e, i] += _dot(ds, k_ref[e])
                out += [carry[2 * e] + _dot_tn(ds, qv), carry[2 * e + 1] + _dot_tn(p.astype(BF16), dov)]
            return tuple(out)

        zero = (jnp.zeros((blk, w), F32),) * 4
        dk0, dv0, dk1, dv1 = lax.fori_loop(j + 1, nb, lambda i, c: contrib(i, False, c), contrib(j, True, zero))
        dk_ref[...] = jnp.where(lane < dh, dk0, pltpu.roll(dk1, dh, 1)).astype(dk_ref.dtype)
        dv_ref[...] = jnp.where(lane < dh, dv0, pltpu.roll(dv1, dh, 1)).astype(dv_ref.dtype)
        for e, dk in enumerate((dk0, dk1)):
            dck_ref[e] = jnp.sum(jnp.where(lane == dh + _AUG_ONE, dk, 0.0), axis=1, keepdims=True)

        @pl.when(j == nb - 1)
        def _():
            for ib in range(nb):
                rs = pl.ds(ib * blk, blk)
                dq0, dq1 = dq_acc[0, ib], dq_acc[1, ib]
                dq_ref[rs, :] = (jnp.where(lane < dh, dq0, pltpu.roll(dq1, dh, 1)) * scale).astype(dq_ref.dtype)
                for e, dq in enumerate((dq0, dq1)):
                    dcq_ref[e, rs, :] = jnp.sum(jnp.where(lane == dh + _AUG_C, dq, 0.0), axis=1, keepdims=True)

    at, whole, rows = _pair_specs(blk, nb)
    seq_rows = pl.BlockSpec((None, l, ATTN_W), lambda bi, p, j: (bi, 0, p))
    seq_col = pl.BlockSpec((None, 2, l, 1), lambda bi, p, j: (bi, p, 0, 0))
    act = _sds((b, l, d_model), BF16)
    col1 = _sds((b, h, l, 1), F32)
    return _pcall(body, name=name, grid=(b, h // 2, nb), in_specs=[at(w), at(w), whole, whole],
                  out_specs=[seq_rows, rows, rows, seq_col, at(1)], out_shape=[act, act, act, col1, col1],
                  scratch_shapes=[pltpu.VMEM((2, nb, blk, w), F32)],
                  compiler_params=_cparams(("parallel", "parallel", "arbitrary")))(ka, va, r5(qa), r5(doa))


def _ssd_dims(d_ssd):
    heads = d_ssd // SSD_HEAD_DIM
    return heads, heads // SSD_GROUPS, d_ssd // SSD_GROUPS


def _ssd_specs(l, ds, seq_map):
    q = SSD_CHUNK
    gn = SSD_GROUPS * SSD_STATE
    row3 = lambda w, cb: pl.BlockSpec((None, q, w), lambda i, c, cb=cb: (i, seq_map(c), cb))
    return dict(
        xs=row3(ds, 0), bm=row3(gn, ds // gn), cm=row3(gn, ds // gn + 1), z=row3(ds, 0), dt=row3(SMALL_W, 0),
        da=row3(SMALL_W, 0), dat=pl.BlockSpec((None, SMALL_W, q), lambda i, c: (i, 0, seq_map(c))),
        e=pl.BlockSpec((SMALL_W, ds), lambda i, c: (0, 0)), et=pl.BlockSpec((ds, SMALL_W), lambda i, c: (0, 0)),
        vec=pl.BlockSpec((1, ds), lambda i, c: (0, 0)), vec128=pl.BlockSpec((1, SMALL_W), lambda i, c: (0, 0)),
        hin=pl.BlockSpec((None, None, SSD_STATE, ds), lambda i, c: (i, seq_map(c), 0, 0)))


def _ssd_common(da, dat, dt, e_mat, xs):
    q = SSD_CHUNK
    row = lax.broadcasted_iota(jnp.int32, (q, q), 0)
    col = lax.broadcasted_iota(jnp.int32, (q, q), 1)
    lower = row >= col
    cs = _dot(lower.astype(F32), da, hi=True)
    cst = _dot(dat, (row <= col).astype(F32), hi=True)
    dtx = _dot(dt, e_mat, hi=True)
    csx = _dot(cs, e_mat, hi=True)
    rowx = lax.broadcasted_iota(jnp.int32, csx.shape, 0)
    totx = jnp.sum(jnp.where(rowx == q - 1, csx, 0.0), axis=0, keepdims=True)
    xf = xs.astype(F32)
    return lower, cs, cst, dtx, csx, totx, xf, xf * dtx


def _ssd_fwd(xbc, z, dt, da, dat, e_mat, dx, nw, name):
    b, l, _ = xbc.shape
    ds = z.shape[2]
    heads, hpg, gw = _ssd_dims(ds)
    q, n = SSD_CHUNK, SSD_STATE
    nc = l // q
    hcol0 = ATTN_HEADS

    def body(xs_ref, bm_ref, cm_ref, z_ref, dt_ref, da_ref, dat_ref, e_ref, dx_ref, nw_ref, y_ref, yraw_ref, hin_ref,
             hst, ydiag):
        c = pl.program_id(1)

        @pl.when(c == 0)
        def _():
            hst[...] = jnp.zeros_like(hst)

        hin = hst[...]
        hin_ref[...] = hin
        lower, cs, cst, dtx, csx, totx, xf, xdt = _ssd_common(da_ref[...], dat_ref[...], dt_ref[...], e_ref[...],
                                                               xs_ref[...])
        bm, cm = bm_ref[...], cm_ref[...]
        dec_end = jnp.exp(totx - csx)
        for g in range(SSD_GROUPS):
            gs = slice(g * gw, (g + 1) * gw)
            bg, cg = bm[:, g * n:(g + 1) * n], cm[:, g * n:(g + 1) * n]
            cb = _dot_nt(cg, bg)
            for e in range(hpg):
                hh = g * hpg + e
                cc = hcol0 + hh
                lm = jnp.exp(jnp.where(lower, cs[:, cc:cc + 1] - cst[cc:cc + 1, :], NEG))
                hs = slice(hh * SSD_HEAD_DIM, (hh + 1) * SSD_HEAD_DIM)
                ydiag[:, hs] = _dot((cb * lm).astype(BF16), xdt[:, hs].astype(BF16))
            sg = _dot_tn(bg, (xdt[:, gs] * dec_end[:, gs]).astype(BF16))
            hst[:, gs] = jnp.exp(totx[:, gs]) * hin[:, gs] + sg
            ydiag[:, gs] += _dot(cg, hin[:, gs].astype(BF16)) * jnp.exp(csx[:, gs])
        yraw = ydiag[...] + dx_ref[...] * xf
        yraw_ref[...] = yraw.astype(yraw_ref.dtype)
        yg = yraw * _silu(z_ref[...].astype(F32))
        nwv = nw_ref[...]
        for g in range(SSD_GROUPS):
            gs = slice(g * gw, (g + 1) * gw)
            r = lax.rsqrt(jnp.mean(yg[:, gs] * yg[:, gs], axis=1, keepdims=True) + NORM_EPS)
            y_ref[:, gs] = (yg[:, gs] * r * nwv[:, gs]).astype(y_ref.dtype)

    sp = _ssd_specs(l, ds, lambda c: c)
    return _pcall(body, name=name, grid=(b, nc),
                  in_specs=[sp['xs'], sp['bm'], sp['cm'], sp['z'], sp['dt'], sp['da'], sp['dat'], sp['e'], sp['vec'],
                            sp['vec']],
                  out_specs=[sp['z'], sp['z'], sp['hin']],
                  out_shape=[_sds((b, l, ds), BF16), _sds((b, l, ds), BF16), _sds((b, nc, n, ds), F32)],
                  scratch_shapes=[pltpu.VMEM((n, ds), F32), pltpu.VMEM((q, ds), F32)],
                  compiler_params=_cparams(("parallel", "arbitrary")))(xbc, xbc, xbc, z, dt, da, dat, e_mat, dx, nw)


def _ssd_bwd(xbc, z, dt, da, dat, e_mat, et_mat, dx, nw, a128, yraw, hin, dy, name):
    b, l, dxw = xbc.shape
    ds = z.shape[2]
    heads, hpg, gw = _ssd_dims(ds)
    q, n = SSD_CHUNK, SSD_STATE
    gn = SSD_GROUPS * n
    nc = l // q
    hcol0 = ATTN_HEADS

    def body(xs_ref, bm_ref, cm_ref, z_ref, dt_ref, da_ref, dat_ref, e_ref, et_ref, dx_ref, nw_ref, a_ref, yraw_ref,
             hin_ref, dy_ref, dxs_ref, dbm_ref, dcm_ref, dz_ref, ddt_ref, dd_ref, dnw_ref, dap_ref, dhs, dxdt, dcsx,
             dtotx):
        i, c = pl.program_id(0), pl.program_id(1)

        @pl.when(c == 0)
        def _():
            dhs[...] = jnp.zeros_like(dhs)

        @pl.when((i == 0) & (c == 0))
        def _():
            dd_ref[...] = jnp.zeros_like(dd_ref)
            dnw_ref[...] = jnp.zeros_like(dnw_ref)
            dap_ref[...] = jnp.zeros_like(dap_ref)

        dtv = dt_ref[...]
        lower, cs, cst, dtx, csx, totx, xf, xdt = _ssd_common(da_ref[...], dat_ref[...], dtv, e_ref[...], xs_ref[...])
        upper = jnp.logical_not(lower) | (lax.broadcasted_iota(jnp.int32, (q, q), 0)
                                          == lax.broadcasted_iota(jnp.int32, (q, q), 1))
        bm, cm = bm_ref[...], cm_ref[...]
        ecs, dec_end, etot = jnp.exp(csx), jnp.exp(totx - csx), jnp.exp(totx)
        yraw = yraw_ref[...].astype(F32)
        zv = z_ref[...].astype(F32)
        sz = _silu(zv)
        yg = yraw * sz
        dyn_ = dy_ref[...].astype(F32)
        nwv = nw_ref[...]
        dygs, dnws = [], []
        for g in range(SSD_GROUPS):
            gs = slice(g * gw, (g + 1) * gw)
            r = lax.rsqrt(jnp.mean(yg[:, gs] * yg[:, gs], axis=1, keepdims=True) + NORM_EPS)
            yn = yg[:, gs] * r
            dn = dyn_[:, gs] * nwv[:, gs]
            dnws.append(jnp.sum(dyn_[:, gs] * yn, axis=0, keepdims=True))
            dygs.append(r * (dn - yn * jnp.mean(dn * yn, axis=1, keepdims=True)))
        dyg = jnp.concatenate(dygs, axis=1)
        dnw_ref[...] += jnp.concatenate(dnws, axis=1)
        dz_ref[...] = (dyg * yraw * _dsilu(zv)).astype(dz_ref.dtype)
        dyv = dyg * sz
        dd_ref[...] += jnp.sum(dyv * xf, axis=0, keepdims=True)
        hin, dh = hin_ref[...], dhs[...]
        lane128 = lax.broadcasted_iota(jnp.int32, (1, SMALL_W), 1)
        dcs = jnp.zeros((q, SMALL_W), F32)
        for g in range(SSD_GROUPS):
            gs = slice(g * gw, (g + 1) * gw)
            bg, cg = bm[:, g * n:(g + 1) * n], cm[:, g * n:(g + 1) * n]
            hg, dhg = hin[:, gs], dh[:, gs]
            hgb, dsb = hg.astype(BF16), dhg.astype(BF16)
            yoff = _dot(cg, hgb) * ecs[:, gs]
            dch = (dyv[:, gs] * ecs[:, gs]).astype(BF16)
            dcg = _dot_nt(dch, hgb)
            dhs[:, gs] = _dot_tn(cg, dch) + etot[:, gs] * dhg
            zg = xdt[:, gs] * dec_end[:, gs]
            dzz = _dot(bg, dsb)
            dbg = _dot_nt(zg.astype(BF16), dsb)
            dxdt_g = dzz * dec_end[:, gs]
            w_end = dzz * zg
            dtotx[:, gs] = jnp.sum(dhg * hg, axis=0, keepdims=True) * etot[:, gs] + jnp.sum(w_end, axis=0, keepdims=True)
            dcsx[:, gs] = dyv[:, gs] * yoff - w_end
            cb, cbt = _dot_nt(cg, bg), _dot_nt(bg, cg)
            dgm = jnp.zeros((q, q), F32)
            for e in range(hpg):
                hh = g * hpg + e
                cc = hcol0 + hh
                ccol, crow = cs[:, cc:cc + 1], cst[cc:cc + 1, :]
                lm = jnp.exp(jnp.where(lower, ccol - crow, NEG))
                lmt = jnp.exp(jnp.where(upper, crow - ccol, NEG))
                mm, mt = cb * lm, cbt * lmt
                hs = slice(hh * SSD_HEAD_DIM, (hh + 1) * SSD_HEAD_DIM)
                dye, xe = dyv[:, hs].astype(BF16), xdt[:, hs].astype(BF16)
                dm, dmt = _dot_nt(dye, xe), _dot_nt(xe, dye)
                dxdt[:, hs] = dxdt_g[:, e * SSD_HEAD_DIM:(e + 1) * SSD_HEAD_DIM] + _dot(mt.astype(BF16), dye)
                dgm = dgm + dm * lm
                rs = jnp.sum(dm * mm, axis=1, keepdims=True) - jnp.sum(dmt * mt, axis=1, keepdims=True)
                dcs = dcs + rs * (lane128 == cc).astype(F32)
            dgb = dgm.astype(BF16)
            dcm_ref[:, g * n:(g + 1) * n] = (dcg + _dot(dgb, bg)).astype(dcm_ref.dtype)
            dbm_ref[:, g * n:(g + 1) * n] = (dbg + _dot_tn(dgb, cg)).astype(dbm_ref.dtype)
        dxd = dxdt[...]
        dxs_ref[...] = (dx_ref[...] * dyv + dxd * dtx).astype(dxs_ref.dtype)
        et = et_ref[...]
        ddt = _dot(dxd * xf, et, hi=True)
        dtot128 = _dot(jnp.broadcast_to(dtotx[...], (8, ds)), et, hi=True)[0:1, :]
        row128 = lax.broadcasted_iota(jnp.int32, (q, SMALL_W), 0)
        dcs = dcs + _dot(dcsx[...], et, hi=True) + jnp.where(row128 == q - 1, dtot128, 0.0)
        dda = _dot(upper.astype(F32), dcs, hi=True)
        ddt_ref[...] = ddt + dda * a_ref[...]
        dap_ref[...] += jnp.sum(dda * dtv, axis=0, keepdims=True)

    rev = lambda c: nc - 1 - c
    sp = _ssd_specs(l, ds, rev)
    row3 = lambda w: pl.BlockSpec((None, q, w), lambda i, c: (i, rev(c), 0))
    acc = lambda w: pl.BlockSpec((1, w), lambda i, c: (0, 0))
    return _pcall(body, name=name, grid=(b, nc),
                  in_specs=[sp['xs'], sp['bm'], sp['cm'], sp['z'], sp['dt'], sp['da'], sp['dat'], sp['e'], sp['et'],
                            sp['vec'], sp['vec'], sp['vec128'], sp['z'], sp['hin'], sp['z']],
                  out_specs=[row3(ds), row3(gn), row3(gn), row3(ds), row3(SMALL_W), acc(ds), acc(ds), acc(SMALL_W)],
                  out_shape=[_sds((b, l, ds), BF16), _sds((b, l, gn), BF16), _sds((b, l, gn), BF16), _sds((b, l, ds), BF16),
                             _sds((b, l, SMALL_W), F32), _sds((1, ds), F32), _sds((1, ds), F32), _sds((1, SMALL_W), F32)],
                  scratch_shapes=[pltpu.VMEM((n, ds), F32), pltpu.VMEM((q, ds), F32), pltpu.VMEM((q, ds), F32),
                                  pltpu.VMEM((1, ds), F32)],
                  compiler_params=_cparams(("arbitrary", "arbitrary")))(
                      xbc, xbc, xbc, z, dt, da, dat, e_mat, et_mat, dx, nw, a128, yraw, hin, dy)


_GROUP_SIZE = {'c': 2, 'xy': 4, 'xyc': 8}
_LOCAL_SPLIT = 16


def _exchange(src, group, scatter, name, nsplit=1, copy_own=True):
    n = _GROUP_SIZE[group]
    rows, width = src.shape[-2:]
    assert src.ndim == (3 if scatter else 2)
    while rows % (8 * nsplit):
        nsplit //= 2
    crow = rows // nsplit
    nlocal = _LOCAL_SPLIT
    while rows % (8 * nlocal):
        nlocal //= 2
    lrow = rows // nlocal

    def body(src_ref, out_ref, send_sems, recv_sems, local_sems):
        x, y, c = lax.axis_index("x"), lax.axis_index("y"), lax.axis_index("c")
        if group == 'c':
            rank = c
            dev = lambda r: (x, y, r)
        elif group == 'xy':
            rank = 2 * x + y
            dev = lambda r: (r // 2, r % 2, c)
        else:
            rank = 4 * x + 2 * y + c
            dev = lambda r: (r // 4, (r // 2) % 2, r % 2)

        def mine_for(r, ck):
            piece = src_ref.at[r] if scatter else src_ref
            return piece.at[pl.ds(ck * crow, crow)]

        def copy(k, ck, pr, dst_rank):
            return pltpu.make_async_remote_copy(
                src_ref=mine_for(pr, ck), dst_ref=out_ref.at[dst_rank].at[pl.ds(ck * crow, crow)],
                send_sem=send_sems.at[k * nsplit + ck], recv_sem=recv_sems.at[k * nsplit + ck], device_id=dev(pr),
                device_id_type=pl.DeviceIdType.MESH)

        locals_ = []
        if copy_own:
            own = src_ref.at[rank] if scatter else src_ref
            for ck in range(nlocal):
                rs = pl.ds(ck * lrow, lrow)
                locals_.append(pltpu.make_async_copy(own.at[rs], out_ref.at[rank].at[rs], local_sems.at[ck]))
                locals_[-1].start()
        peers = [jnp.bitwise_xor(rank, k + 1) for k in range(n - 1)]
        sends = [copy(k, ck, pr, rank) for ck in range(nsplit) for k, pr in enumerate(peers)]
        for cp in sends:
            cp.start()
        for ck in range(nsplit):
            for k, pr in enumerate(peers):
                copy(k, ck, pr, pr).wait_recv()
        for cp in sends:
            cp.wait_send()
        for cp in locals_:
            cp.wait()

    return _pcall(body, name=name, in_specs=[pl.BlockSpec(memory_space=pl.ANY)],
                  out_specs=pl.BlockSpec(memory_space=pl.ANY), out_shape=_sds((n, rows, width), src.dtype),
                  scratch_shapes=[pltpu.SemaphoreType.DMA(((n - 1) * nsplit,)),
                                  pltpu.SemaphoreType.DMA(((n - 1) * nsplit,)),
                                  pltpu.SemaphoreType.DMA((nlocal,))])(src)


def _exchange_multi(srcs, group, scatter, name, single=False, min_copies=16):
    n = _GROUP_SIZE[group]
    assert not single or n == 2
    na = len(srcs)
    shapes = [tuple(s.shape[-2:]) for s in srcs]
    want = max(1, -(-min_copies // (na * (n - 1))))
    splits = []
    for (rows, _), s in zip(shapes, srcs):
        quant = 8 * (4 // s.dtype.itemsize)
        k = want
        while k > 1 and rows % (quant * k):
            k -= 1
        splits.append(k)
    offs = [int(v) for v in np.cumsum([0] + [(n - 1) * k for k in splits])]

    def body(*refs):
        src_refs, out_refs = refs[:na], refs[na:2 * na]
        send_sems, recv_sems = refs[2 * na:]
        x, y, c = lax.axis_index("x"), lax.axis_index("y"), lax.axis_index("c")
        if group == 'c':
            rank = c
            dev = lambda r: (x, y, r)
        elif group == 'xy':
            rank = 2 * x + y
            dev = lambda r: (r // 2, r % 2, c)
        else:
            rank = 4 * x + 2 * y + c
            dev = lambda r: (r // 4, (r // 2) % 2, r % 2)
        peers = [jnp.bitwise_xor(rank, k + 1) for k in range(n - 1)]

        def copy(a, k, ck, dst_rank):
            crow = shapes[a][0] // splits[a]
            rs = pl.ds(ck * crow, crow)
            piece = src_refs[a].at[peers[k]] if scatter else src_refs[a]
            dst = out_refs[a] if single else out_refs[a].at[dst_rank]
            sem = offs[a] + k * splits[a] + ck
            return pltpu.make_async_remote_copy(src_ref=piece.at[rs], dst_ref=dst.at[rs], send_sem=send_sems.at[sem],
                                                recv_sem=recv_sems.at[sem], device_id=dev(peers[k]),
                                                device_id_type=pl.DeviceIdType.MESH)

        todo = [(a, k, ck) for a in range(na) for ck in range(splits[a]) for k in range(n - 1)]
        sends = [copy(a, k, ck, rank) for a, k, ck in todo]
        for cp in sends:
            cp.start()
        for a, k, ck in todo:
            copy(a, k, ck, peers[k]).wait_recv()
        for cp in sends:
            cp.wait_send()

    any_spec = pl.BlockSpec(memory_space=pl.ANY)
    out_shape = [_sds(sh if single else (n,) + sh, s.dtype) for sh, s in zip(shapes, srcs)]
    return _pcall(body, name=name, in_specs=[any_spec] * na, out_specs=[any_spec] * na, out_shape=out_shape,
                  scratch_shapes=[pltpu.SemaphoreType.DMA((offs[-1],)), pltpu.SemaphoreType.DMA((offs[-1],))])(*srcs)


def _sum_slots(arr, out_dtype, name):
    n, rows, cols = arr.shape
    tm = _pick(rows, [c for c in (384, 256, 128, 64, 32, 16, 8) if c * cols <= _ROWWISE_TILE_ELEMS or c == 8])

    def body(*refs):
        acc = refs[0][...].astype(F32)
        for r in refs[1:n]:
            acc = acc + r[...].astype(F32)
        refs[n][...] = acc.astype(refs[n].dtype)

    return _pcall(body, name=name, grid=(rows // tm,),
                  in_specs=[pl.BlockSpec((None, tm, cols), lambda i, j=j: (j, i, 0)) for j in range(n)],
                  out_specs=pl.BlockSpec((tm, cols), lambda i: (i, 0)), out_shape=_sds((rows, cols), out_dtype),
                  compiler_params=_cparams(("parallel",)))(*([arr] * n))


def _dims():
    d = D_MODEL
    h = ATTN_HEADS
    d_ssd = d
    d_xbc = d_ssd + 2 * SSD_GROUPS * SSD_STATE
    sizes = (d, d, d, h, d_ssd, d_xbc, d_ssd // SSD_HEAD_DIM, d, d, 3 * d)
    return d, h, d_ssd, d_xbc, sizes


def _w_in_split(w):
    d, h, d_ssd, d_xbc, sizes = _dims()
    off = np.concatenate([[0], np.cumsum(sizes)])
    seg = lambda i: w[..., off[i]:off[i + 1]]
    main = jnp.concatenate([seg(0), seg(1), seg(2), seg(4), seg(5), seg(7), seg(8), seg(9)], axis=-1)
    pad = jnp.zeros(w.shape[:-1] + (SMALL_W - sizes[3] - sizes[6],), w.dtype)
    small = jnp.concatenate([seg(3), seg(6), pad], axis=-1)
    return main, small


def _w_in_merge(main, small):
    d, h, d_ssd, d_xbc, sizes = _dims()
    order = (0, 1, 2, 4, 5, 7, 8, 9)
    moff = np.concatenate([[0], np.cumsum([sizes[i] for i in order])])
    pieces = {i: main[..., moff[j]:moff[j + 1]] for j, i in enumerate(order)}
    pieces[3] = small[..., :sizes[3]]
    pieces[6] = small[..., sizes[3]:sizes[3] + sizes[6]]
    return jnp.concatenate([pieces[i] for i in range(10)], axis=-1)


def _main_offsets():
    d, h, d_ssd, d_xbc, sizes = _dims()
    names = ('q', 'k', 'v', 'z', 'xbc', 'xr', 'gate', 'merge')
    widths = (d, d, d, d_ssd, d_xbc, d, d, 3 * d)
    off = np.concatenate([[0], np.cumsum(widths)])
    return {nm: (int(off[i]), int(off[i + 1])) for i, nm in enumerate(names)}


def _block_diag(w):
    nb, s, _ = w.shape
    eye = jnp.eye(nb, dtype=w.dtype)
    return (eye[:, None, :, None] * w[:, :, None, :]).reshape(nb * s, nb * s)


def _diag_blocks(wd, nb):
    s = wd.shape[0] // nb
    return jnp.stack([wd[i * s:(i + 1) * s, i * s:(i + 1) * s] for i in range(nb)])


def _vec128(*parts):
    v = jnp.concatenate([p.astype(F32) for p in parts])
    return jnp.pad(v, (0, SMALL_W - v.shape[0]))[None, :]


def _ffn_fwd(h, gnorm, w, tag):
    xn = _rms_fwd(h, gnorm[None, :], f"{tag}_norm")
    g, u, act = _mm_swiglu(xn, w['wg'], w['wu'], f"{tag}_gu")
    out = _mm_nn(act, w['wd'], F32, res=h, alpha=0.5, name=f"{tag}_down")
    return out, (h, xn, g, u, act)


def _ffn_bwd(dout, saved, gnorm, w, tag):
    h, xn, g, u, act = saved
    dg, du = _mm_dswiglu(dout, w['wd_t'], g, u, 0.5, f"{tag}_dgu")
    dwd = _mm_tn(act, dout, alpha=0.5, name=f"{tag}_dwd")
    dwgu = jnp.concatenate([_mm_tn(xn, dg, name=f"{tag}_dwg"), _mm_tn(xn, du, name=f"{tag}_dwu")], axis=1)
    dxn = _mm_nn(dg, w['wg_t'], F32, name=f"{tag}_dxn_g")
    dxn = _mm_nn(du, w['wu_t'], F32, res=dxn, name=f"{tag}_dxn_u")
    dh, dgn = _rms_bwd(h, dxn, dout, gnorm[None, :], f"{tag}_dnorm")
    return dh, dgn[0], dwgu, dwd


def _mixer_fwd(h, p, b, l):
    d, nh, d_ssd, d_xbc, sizes = _dims()
    t = b * l
    off = _main_offsets()
    xn = _rms_fwd(h, p['mix_norm'][None, :], "mix_norm")
    pm = _mm_nn(xn, p['w_main'], BF16, name="mix_in_main")
    ps = _mm_nn(xn, p['w_small'], F32, name="mix_in_small")
    col = lambda nm: pm[:, off[nm][0]:off[nm][1]]
    heads_ssd = d_ssd // SSD_HEAD_DIM
    a_neg = -jnp.exp(p['ssd_a_log'])
    fb = _vec128(p['fox_forget_bias'])
    dtb = _vec128(jnp.zeros((nh,), F32), p['ssd_dt_bias'])
    a128 = _vec128(jnp.zeros((nh,), F32), a_neg)

    def prep(_, v, fbv, dtbv, av):
        lane = lax.broadcasted_iota(jnp.int32, (1, SMALL_W), 1)
        logf = jnp.where(lane < nh, -_softplus(-(v + fbv)), 0.0)
        dtv = jnp.where((lane >= nh) & (lane < nh + heads_ssd), _softplus(v + dtbv), 0.0)
        return logf, dtv, dtv * av
    logf, dt, da = _rowwise(prep, [ps], [(SMALL_W, F32)] * 3, bcast=[fb, dtb, a128], name="mix_prep")

    cum = _cumsum_seq(logf.reshape(b, l, SMALL_W), False, "fox_cumsum")
    qa, ka, va = _attn_prep(pm.reshape(b, l, -1), cum, (off['q'][0], off['k'][0], off['v'][0]), "fox_prep")
    y_a3, lse = _flash_fwd(qa, ka, va, d, "fox_fwd")
    y_a = y_a3.reshape(t, d)

    xbc = col('xbc').reshape(b, l, d_xbc)
    pre_b = _conv_fwd(xbc, p['ssd_conv_w'], p['ssd_conv_b'][None, :], BF16, "ssd_conv")
    xbc_act = _rowwise(lambda _, v: _silu(v.astype(F32)), [pre_b.reshape(t, d_xbc)], [(d_xbc, BF16)],
                       name="ssd_conv_act")[0].reshape(b, l, d_xbc)
    z = col('z').reshape(b, l, d_ssd)
    dt3, da3 = dt.reshape(b, l, SMALL_W), da.reshape(b, l, SMALL_W)
    dat3 = da3.transpose(0, 2, 1)
    e_mat = _expand_matrix(nh, heads_ssd)
    dx = jnp.repeat(p['ssd_d'], SSD_HEAD_DIM)[None, :]
    nw = p['ssd_norm'][None, :]
    y_b3, yraw, hin = _ssd_fwd(xbc_act, z, dt3, da3, dat3, e_mat, dx, nw, "ssd_fwd")
    y_b = y_b3.reshape(t, d_ssd)

    xr = col('xr').reshape(b, l, d)
    xc = _conv_fwd(xr, p['lru_conv_w'], p['lru_conv_b'][None, :], F32, "lru_conv").reshape(t, d)
    pre_ri = _mm_nn(xc, p['lru_w_ri'], F32, name="lru_gates")
    lvec = (p['lru_b_a'][None, :], p['lru_b_x'][None, :], p['lru_lambda'][None, :])
    a_l, u_l = _rowwise(_lru_point_fwd, [pre_ri, xc], [(d, F32)] * 2, bcast=lvec, name="lru_point", period=l)
    hs = _linear_scan(a_l.reshape(b, l, d), u_l.reshape(b, l, d), False, "lru_scan").reshape(t, d)
    gate = col('gate')
    y_c = _rowwise(lambda _, hv, gv: hv * _gelu(gv.astype(F32)), [hs, gate], [(d, BF16)], name="lru_out")[0]

    ba = _mm_nn(y_a, p['w_branch_attn'], BF16, name="branch_attn")
    bb = _mm_nn(y_b, p['w_branch_ssd'], BF16, name="branch_ssd")
    bc = _mm_nn(y_c, p['w_branch_lru'], BF16, name="branch_lru")
    mg = col('merge')
    mixed = _merge_fwd(mg, ba, bb, bc, "merge")
    out = _mm_nn(mixed, p['w_out'], F32, res=h, name="mix_out")
    saved = dict(h=h, xn=xn, ps=ps, fb=fb, dtb=dtb, a128=a128, qa=qa, ka=ka, va=va, lse=lse,
                 xbc=xbc, pre_b=pre_b, xbc_act=xbc_act, z=z, dt3=dt3, da3=da3, dat3=dat3, e_mat=e_mat, dx=dx, nw=nw,
                 yraw=yraw, hin=hin, xr=xr, xc=xc, pre_ri=pre_ri, lvec=lvec, a_l=a_l, hs=hs, gate=gate, y_a=y_a, y_b=y_b,
                 y_c=y_c, ba=ba, bb=bb, bc=bc, mg=mg, mixed=mixed)
    return out, saved


def _expand_matrix(nh, heads_ssd):
    e = np.zeros((SMALL_W, heads_ssd * SSD_HEAD_DIM), np.float32)
    for hh in range(heads_ssd):
        e[nh + hh, hh * SSD_HEAD_DIM:(hh + 1) * SSD_HEAD_DIM] = 1.0
    return jnp.asarray(e)


def _lru_gates(pre, xc, bav, bxv, lamv, pos):
    d = xc.shape[1]
    r = _sigmoid(pre[:, :d] + bav)
    i = _sigmoid(pre[:, d:] + bxv)
    ls = -_softplus(-lamv)
    la = LRU_C * r * ls
    a = jnp.exp(la)
    mult = jnp.where(pos == 0, 1.0, jnp.sqrt(-_expm1(2.0 * la)))
    return r, i, ls, a, mult


def _lru_point_fwd(pos, pre, xc, bav, bxv, lamv):
    r, i, ls, a, mult = _lru_gates(pre, xc, bav, bxv, lamv, pos)
    return a, mult * (i * xc)


def _lru_point_bwd(pos, g, hprev, pre, xc, bav, bxv, lamv):
    r, i, ls, a, mult = _lru_gates(pre, xc, bav, bxv, lamv, pos)
    da = g * hprev
    di = g * mult * xc
    dxc = g * mult * i
    dmult = jnp.where(pos == 0, 0.0, g * i * xc)
    dla = da * a - dmult * (a * a) / mult
    dpre_r = dla * (LRU_C * ls) * r * (1.0 - r)
    dpre_i = di * i * (1.0 - i)
    dlam = jnp.sum(dla * (LRU_C * r), axis=0, keepdims=True) * _sigmoid(-lamv)
    return (jnp.concatenate([dpre_r, dpre_i], axis=1), dxc, dlam, jnp.sum(dpre_r, axis=0, keepdims=True),
            jnp.sum(dpre_i, axis=0, keepdims=True))


def _mixer_bwd(dout, s, p, b, l):
    d, nh, d_ssd, d_xbc, sizes = _dims()
    t = b * l
    heads_ssd = d_ssd // SSD_HEAD_DIM
    g = {}
    dmixed = _mm_nn(dout, p['w_out_t'], BF16, name="mix_out_dx")
    g['w_out'] = _mm_tn(s['mixed'], dout, name="mix_out_dw")
    dba, dbb, dbc, dmerge = _merge_bwd(s['mg'], s['ba'], s['bb'], s['bc'], dmixed, "merge_bwd")
    g['w_branch_attn'] = _mm_tn(s['y_a'], dba, name="branch_attn_dw")
    g['w_branch_ssd'] = _mm_tn(s['y_b'], dbb, name="branch_ssd_dw")
    g['w_branch_lru'] = _mm_tn(s['y_c'], dbc, name="branch_lru_dw")
    dy_a = _mm_nn(dba, p['w_branch_attn_t'], BF16, name="branch_attn_dx")
    dy_b = _mm_nn(dbb, p['w_branch_ssd_t'], BF16, name="branch_ssd_dx")
    dy_c = _mm_nn(dbc, p['w_branch_lru_t'], F32, name="branch_lru_dx")

    dgate, dhs = _rowwise(lambda _, dv, hv, gv: (dv * hv * _dgelu(gv.astype(F32)), dv * _gelu(gv.astype(F32))),
                          [dy_c, s['hs'], s['gate']], [(d, BF16), (d, F32)], name="lru_out_bwd")
    a3 = s['a_l'].reshape(b, l, d)
    a_next = jnp.concatenate([a3[:, 1:], jnp.zeros((b, 1, d), F32)], axis=1)
    gs = _linear_scan(a_next, dhs.reshape(b, l, d), True, "lru_scan_bwd").reshape(t, d)
    h3 = s['hs'].reshape(b, l, d)
    hprev = jnp.concatenate([jnp.zeros((b, 1, d), F32), h3[:, :-1]], axis=1).reshape(t, d)
    dpre_ri, dxc0, dlam, dba_, dbx_ = _rowwise(_lru_point_bwd, [gs, hprev, s['pre_ri'], s['xc']],
                                               [(2 * d, BF16), (d, F32)], bcast=s['lvec'],
                                               reds=[(1, d)] * 3, name="lru_point_bwd", period=l)
    g['lru_lambda'], g['lru_b_a'], g['lru_b_x'] = dlam[0], dba_[0], dbx_[0]
    dxc = _mm_nn(dpre_ri, p['lru_w_ri_t'], BF16, res=dxc0, name="lru_gates_dx")
    dw_ri = _mm_tn(s['xc'], dpre_ri, name="lru_gates_dw")
    g['lru_w_a'] = _diag_blocks(dw_ri[:, :d], LRU_BLOCKS)
    g['lru_w_x'] = _diag_blocks(dw_ri[:, d:], LRU_BLOCKS)
    dxr, dwl = _conv_bwd(s['xr'], dxc.reshape(b, l, d), p['lru_conv_w'], "lru_conv_bwd")
    g['lru_conv_w'], g['lru_conv_b'] = dwl[:CONV_K], dwl[CONV_K]

    et_mat = s['e_mat'].T
    dxs, dbm, dcm, dz, ddt, dd_l, dnw, dap = _ssd_bwd(s['xbc_act'], s['z'], s['dt3'], s['da3'], s['dat3'], s['e_mat'],
                                                      et_mat, s['dx'], s['nw'], s['a128'], s['yraw'], s['hin'],
                                                      dy_b.reshape(b, l, d_ssd), "ssd_bwd")
    g['ssd_d'] = dd_l.reshape(heads_ssd, SSD_HEAD_DIM).sum(axis=1)
    g['ssd_norm'] = dnw[0]
    g['ssd_a_log'] = dap[0, nh:nh + heads_ssd] * (-jnp.exp(p['ssd_a_log']))
    dxbc_act = jnp.concatenate([dxs, dbm, dcm], axis=2).reshape(t, d_xbc)
    dpre_b = _rowwise(lambda _, dv, pv: dv.astype(F32) * _dsilu(pv.astype(F32)),
                      [dxbc_act, s['pre_b'].reshape(t, d_xbc)], [(d_xbc, BF16)], name="ssd_conv_act_bwd")[0]
    dxbc, dws = _conv_bwd(s['xbc'], dpre_b.reshape(b, l, d_xbc), p['ssd_conv_w'], "ssd_conv_bwd")
    g['ssd_conv_w'], g['ssd_conv_b'] = dws[:CONV_K], dws[CONV_K]

    qa2, doa = _attn_prep_bwd(dy_a.reshape(b, l, d), s['y_a'].reshape(b, l, d), s['qa'], s['lse'], "fox_prep_bwd")
    dq3, dk3, dv3, dcq, dck = _flash_bwd(qa2, s['ka'], s['va'], doa, d, "fox_bwd")
    dcum = jnp.pad((dcq - dck)[..., 0].transpose(0, 2, 1), ((0, 0), (0, 0), (0, SMALL_W - nh)))
    dlogf = _cumsum_seq(dcum, True, "fox_cumsum_bwd").reshape(t, SMALL_W)

    def prep_bwd(_, v, dlf, ddtv, fbv, dtbv):
        a_ = dlf * _sigmoid(-(v + fbv))
        b_ = ddtv * _sigmoid(v + dtbv)
        return a_ + b_, jnp.sum(a_, axis=0, keepdims=True), jnp.sum(b_, axis=0, keepdims=True)
    dps, dfb, ddtb = _rowwise(prep_bwd, [s['ps'], dlogf, ddt.reshape(t, SMALL_W)], [(SMALL_W, F32)],
                              bcast=[s['fb'], s['dtb']], reds=[(1, SMALL_W)] * 2, name="mix_prep_bwd")
    g['fox_forget_bias'] = dfb[0, :nh]
    g['ssd_dt_bias'] = ddtb[0, nh:nh + heads_ssd]

    dpm = jnp.concatenate([dq3.reshape(t, d), dk3.reshape(t, d), dv3.reshape(t, d),
                           dz.reshape(t, d_ssd), dxbc.reshape(t, d_xbc), dxr.reshape(t, d), dgate, dmerge], axis=1)
    dxn = _mm_nn(dps, p['w_small_t'], F32, name="mix_in_small_dx")
    dxn = _mm_nn(dpm, p['w_main_t'], F32, res=dxn, name="mix_in_main_dx")
    g['w_main'] = _mm_tn(s['xn'], dpm, name="mix_in_main_dw")
    g['w_small'] = _mm_tn(s['xn'], dps, name="mix_in_small_dw")
    dh, dg = _rms_bwd(s['h'], dxn, dout, p['mix_norm'][None, :], "mix_norm_bwd")
    g['mix_norm'] = dg[0]
    return dh, g


def _layer_params(w, li):
    p = {n: w[n][li] for n in WEIGHTS if n not in ('meta_tokens', 'final_norm')}
    bf = lambda a: a.astype(BF16)
    for tag in ('ffn1', 'ffn2'):
        wgu, wd = bf(p[tag + '_w_gate_up']), bf(p[tag + '_w_down'])
        f = wd.shape[0]
        p[tag] = dict(wg=wgu[:, :f], wu=wgu[:, f:], wg_t=wgu[:, :f].T, wu_t=wgu[:, f:].T, wd=wd, wd_t=wd.T)
    wm, ws = _w_in_split(bf(p['w_in']))
    p['w_main'], p['w_main_t'], p['w_small'], p['w_small_t'] = wm, wm.T, ws, ws.T
    for n in ('w_branch_attn', 'w_branch_ssd', 'w_branch_lru', 'w_out'):
        p[n + '_t'] = bf(p[n]).T
        p[n] = bf(p[n])
    wri = jnp.concatenate([_block_diag(p['lru_w_a']), _block_diag(p['lru_w_x'])], axis=1)
    p['lru_w_ri'], p['lru_w_ri_t'] = bf(wri), bf(wri).T
    return p


def _local_step(x, loss_target, w):
    b, seq, d = x.shape
    length = N_META + seq
    l = -(-length // Q_BLOCK) * Q_BLOCK
    t = b * l
    meta = jnp.broadcast_to(w['meta_tokens'].astype(F32)[None], (b, N_META, d))
    h = jnp.concatenate([meta, x, jnp.zeros((b, l - length, d), F32)], axis=1).reshape(t, d)
    tgt = jnp.concatenate([jnp.zeros((b, N_META, d), F32), loss_target, jnp.zeros((b, l - length, d), F32)],
                          axis=1).reshape(t, d)
    params, saves = [], []
    for li in range(DEPTH):
        p = _layer_params(w, li)
        h, s1 = _ffn_fwd(h, p['ffn1_norm'], p['ffn1'], "ffn1")
        h, sm = _mixer_fwd(h, p, b, l)
        h, s2 = _ffn_fwd(h, p['ffn2_norm'], p['ffn2'], "ffn2")
        params.append(p)
        saves.append((s1, sm, s2))
    dh, loss, dgf = _loss_head(h, tgt, w['final_norm'][None, :], l, "loss_head")
    layer_grads = [None] * DEPTH
    for li in reversed(range(DEPTH)):
        p = params[li]
        s1, sm, s2 = saves[li]
        g = {}
        dh, g['ffn2_norm'], g['ffn2_w_gate_up'], g['ffn2_w_down'] = _ffn_bwd(dh, s2, p['ffn2_norm'], p['ffn2'], "ffn2b")
        dh, gm = _mixer_bwd(dh, sm, p, b, l)
        g.update(gm)
        g['w_in'] = _w_in_merge(g.pop('w_main'), g.pop('w_small'))
        dh, g['ffn1_norm'], g['ffn1_w_gate_up'], g['ffn1_w_down'] = _ffn_bwd(dh, s1, p['ffn1_norm'], p['ffn1'], "ffn1b")
        layer_grads[li] = g
    grads = {n: jnp.stack([layer_grads[li][n] for li in range(DEPTH)]) for n in layer_grads[0]}
    for n in ('lru_w_a', 'lru_w_x'):
        grads[n] = grads[n].reshape(w[n].shape)
    dh3 = dh.reshape(b, l, d)
    grads['meta_tokens'] = jnp.sum(dh3[:, :N_META], axis=0)
    grads['final_norm'] = dgf[0]
    return loss, dh3[:, N_META:N_META + seq], grads


def _unflatten(flat, shapes):
    out, o = [], 0
    for sh in shapes:
        n = int(np.prod(sh))
        out.append(flat[o:o + n].reshape(sh))
        o += n
    return out


def kernel(x, meta_tokens, ffn1_norm, ffn1_w_gate_up, ffn1_w_down, mix_norm, w_in, fox_forget_bias, ssd_conv_w, ssd_conv_b, ssd_dt_bias, ssd_a_log, ssd_d, ssd_norm, lru_conv_w, lru_conv_b, lru_w_a, lru_b_a, lru_w_x, lru_b_x, lru_lambda, w_branch_attn, w_branch_ssd, w_branch_lru, w_out, ffn2_norm, ffn2_w_gate_up, ffn2_w_down, final_norm, loss_target, m_meta_tokens, m_ffn1_norm, m_ffn1_w_gate_up, m_ffn1_w_down, m_mix_norm, m_w_in, m_fox_forget_bias, m_ssd_conv_w, m_ssd_conv_b, m_ssd_dt_bias, m_ssd_a_log, m_ssd_d, m_ssd_norm, m_lru_conv_w, m_lru_conv_b, m_lru_w_a, m_lru_b_a, m_lru_w_x, m_lru_b_x, m_lru_lambda, m_w_branch_attn, m_w_branch_ssd, m_w_branch_lru, m_w_out, m_ffn2_norm, m_ffn2_w_gate_up, m_ffn2_w_down, m_final_norm, v_meta_tokens, v_ffn1_norm, v_ffn1_w_gate_up, v_ffn1_w_down, v_mix_norm, v_w_in, v_fox_forget_bias, v_ssd_conv_w, v_ssd_conv_b, v_ssd_dt_bias, v_ssd_a_log, v_ssd_d, v_ssd_norm, v_lru_conv_w, v_lru_conv_b, v_lru_w_a, v_lru_b_a, v_lru_w_x, v_lru_b_x, v_lru_lambda, v_w_branch_attn, v_w_branch_ssd, v_w_branch_lru, v_w_out, v_ffn2_norm, v_ffn2_w_gate_up, v_ffn2_w_down, v_final_norm):
    args = locals()
    wloc = {n: args[n] for n in WEIGHTS}
    mloc = {n: args['m_' + n] for n in WEIGHTS}
    vloc = {n: args['v_' + n] for n in WEIGHTS}
    nchip = 4
    chip = 2 * lax.axis_index("x") + lax.axis_index("y")
    core = lax.axis_index("c")
    hl = DEPTH // 2

    own = lambda out, mine, rank: lax.dynamic_update_index_in_dim(out, mine, rank, 0)
    half_rows = lambda a, which: lax.dynamic_slice_in_dim(a, which * (a.shape[0] // 2), a.shape[0] // 2, axis=0)
    mine = [half_rows(wloc[n].astype(BF16).reshape(-1, wloc[n].shape[-1]), core) for n in BIG_NAMES]
    got = _exchange_multi(mine, 'xy', False, "gather_w_chips")
    got = [own(g_, m_, chip).reshape(nchip * m_.shape[0], m_.shape[1]) for g_, m_ in zip(got, mine)]
    both = _exchange_multi(got, 'c', False, "gather_w_cores")
    full = {}
    for n, b_, g_ in zip(BIG_NAMES, both, got):
        _, r, c = wloc[n].shape
        v = own(b_, g_, core).reshape(2, nchip, hl, r, c)
        if BIG[n] == 1:
            full[n] = v.transpose(0, 2, 3, 1, 4).reshape(DEPTH, r, nchip * c)
        else:
            full[n] = v.transpose(0, 2, 1, 3, 4).reshape(DEPTH, nchip * r, c)
    cs_shapes = [wloc[n].shape for n in COLSHARD_SMALL]
    cs_total = sum(int(np.prod(s)) for s in cs_shapes)
    cs_rows = -(-cs_total // (8 * 128)) * 8
    cs_flat = jnp.concatenate([wloc[n].reshape(-1) for n in COLSHARD_SMALL])
    cs_flat = jnp.pad(cs_flat, (0, cs_rows * 128 - cs_total)).reshape(cs_rows, 128)
    cs_all = _exchange(cs_flat, 'xy', False, "gather_small").reshape(nchip, -1)
    cs_chip = [_unflatten(cs_all[j], cs_shapes) for j in range(nchip)]
    for i, n in enumerate(COLSHARD_SMALL):
        full[n] = jnp.concatenate([cs_chip[j][i] for j in range(nchip)], axis=-1)
    for n in SMALL_NAMES:
        if n not in COLSHARD_SMALL:
            full[n] = wloc[n]

    loss_part, grad_x, grads = _local_step(x, loss_target, full)

    g2d = [grads[n].reshape(-1, grads[n].shape[-1]) for n in BIG_NAMES]
    give = [half_rows(g_, 1 - core) for g_ in g2d]
    keep = [half_rows(g_, core) for g_ in g2d]
    theirs = _exchange_multi(give, 'c', False, "reduce_cores", single=True)
    psums = []
    for n, k_, t_ in zip(BIG_NAMES, keep, theirs):
        s2 = _sum_rows([k_, t_], BF16, "reduce_cores_sum")
        _, r, c = wloc[n].shape
        if BIG[n] == 1:
            psums.append(s2.reshape(s2.shape[0], nchip, c).transpose(1, 0, 2))
        else:
            psums.append(s2.reshape(hl, nchip, r, c).transpose(1, 0, 2, 3).reshape(nchip, hl * r, c))
    parts = _exchange_multi(psums, 'xy', True, "reduce_chips")
    parts = [own(p_, lax.dynamic_index_in_dim(s_, chip, axis=0, keepdims=False), chip) for p_, s_ in zip(parts, psums)]
    rsums = [_sum_slots(p_, F32, "reduce_chips_sum") for p_ in parts]
    halves = _exchange_multi(rsums, 'c', False, "reduce_share")
    gbig = {n: own(h_, r_, core).reshape(wloc[n].shape) for n, h_, r_ in zip(BIG_NAMES, halves, rsums)}

    sm_shapes = [grads[n].shape for n in SMALL_NAMES]
    sm_total = sum(int(np.prod(s)) for s in sm_shapes) + 128
    sm_rows = -(-sm_total // (8 * 128)) * 8
    sm_flat = jnp.concatenate([loss_part.reshape(-1)] + [grads[n].reshape(-1) for n in SMALL_NAMES])
    sm_flat = jnp.pad(sm_flat, (0, sm_rows * 128 - sm_total)).reshape(sm_rows, 128)
    sm_all = _exchange(sm_flat, 'xyc', False, "gather_small_grads")
    sm_sum = _sum_rows([sm_all[j] for j in range(8)], F32, "small_grads_sum").reshape(-1)
    loss = sm_sum[0]
    gsmall_full = dict(zip(SMALL_NAMES, _unflatten(sm_sum[128:], sm_shapes)))
    gsmall = {}
    for n in SMALL_NAMES:
        gfull = gsmall_full[n]
        if n in COLSHARD_SMALL:
            wcols = wloc[n].shape[-1]
            gfull = lax.dynamic_slice_in_dim(gfull, chip * wcols, wcols, axis=gfull.ndim - 1)
        gsmall[n] = gfull

    big_out = [{}, {}, {}]
    for n in BIG_NAMES:
        rows2d = lambda a: a.reshape(-1, a.shape[-1])
        res = _adamw(rows2d(wloc[n]), rows2d(gbig[n]), rows2d(mloc[n]), rows2d(vloc[n]), "adamw_big")
        for k in range(3):
            big_out[k][n] = res[k].reshape(wloc[n].shape)
    loc_shapes = [wloc[n].shape for n in SMALL_NAMES]
    loc_total = sum(int(np.prod(s)) for s in loc_shapes)
    loc_rows = -(-loc_total // (8 * 128)) * 8

    def flat_small(dct):
        v = jnp.concatenate([dct[n].reshape(-1) for n in SMALL_NAMES])
        return jnp.pad(v, (0, loc_rows * 128 - loc_total)).reshape(loc_rows, 128)
    dls, mns, vns = _adamw(flat_small(wloc), flat_small(gsmall), flat_small(mloc), flat_small(vloc), "adamw_small")
    small_out = [dict(zip(SMALL_NAMES, _unflatten(a.reshape(-1), loc_shapes))) for a in (dls, mns, vns)]

    grad_w = {**gbig, **gsmall}
    outs = [loss, grad_x] + [grad_w[n] for n in WEIGHTS]
    for k in range(3):
        merged = {**big_out[k], **small_out[k]}
        outs += [merged[n] for n in WEIGHTS]
    return tuple(outs)
```

```python
import functools
import math

import numpy as np
import jax
import jax.numpy as jnp
from jax import lax
from jax.experimental import pallas as pl
from jax.experimental.pallas import tpu as pltpu

F32 = jnp.float32
BF16 = jnp.bfloat16
HI = lax.Precision.HIGHEST
VMEM_LIMIT_BYTES = 56 * 1024 * 1024
NEG = -1e30

D_MODEL = 1024
SEQ = 4096
DEPTH = 4
N_META = 16
Q_BLOCK = 128
SSD_CHUNK = 128
NORM_EPS = 1e-6
ATTN_HEADS = 16
ATTN_HEAD_DIM = 64
SSD_HEAD_DIM = 64
SSD_GROUPS = 2
SSD_STATE = 128
CONV_K = 4
LRU_BLOCKS = 16
LRU_C = 8.0
D_FF = 2816
ADAM_LR = 0.001
ADAM_B1 = 0.9
ADAM_B2 = 0.999
ADAM_EPS = 1e-08
ADAM_WD = 0.01
ADAM_STEP = 10
SMALL_W = 128
_ROWWISE_TILE_ELEMS = 512 * 1024

WEIGHTS = ['meta_tokens', 'ffn1_norm', 'ffn1_w_gate_up', 'ffn1_w_down', 'mix_norm', 'w_in', 'fox_forget_bias',
           'ssd_conv_w', 'ssd_conv_b', 'ssd_dt_bias', 'ssd_a_log', 'ssd_d', 'ssd_norm', 'lru_conv_w', 'lru_conv_b',
           'lru_w_a', 'lru_b_a', 'lru_w_x', 'lru_b_x', 'lru_lambda', 'w_branch_attn', 'w_branch_ssd', 'w_branch_lru',
           'w_out', 'ffn2_norm', 'ffn2_w_gate_up', 'ffn2_w_down', 'final_norm']
BIG = {'ffn1_w_gate_up': 1, 'ffn1_w_down': 0, 'w_in': 1, 'w_branch_attn': 0, 'w_branch_ssd': 0, 'w_branch_lru': 0,
       'w_out': 0, 'ffn2_w_gate_up': 1, 'ffn2_w_down': 0}
BIG_NAMES = [n for n in WEIGHTS if n in BIG]
COLSHARD_SMALL = ['meta_tokens', 'ssd_conv_w', 'lru_conv_w']
SMALL_NAMES = [n for n in WEIGHTS if n not in BIG]


def _pick(n, cands):
    for c in cands:
        if n % c == 0:
            return c
    raise ValueError(f"no tile for {n} in {cands}")


def _pcall(body, **kw):
    return pl.pallas_call(body, **kw)


def _cparams(sem):
    return pltpu.CompilerParams(dimension_semantics=sem, vmem_limit_bytes=VMEM_LIMIT_BYTES)


def _sds(shape, dtype):
    return jax.ShapeDtypeStruct(tuple(shape), dtype)


def _dot(a, b, hi=False):
    return jnp.dot(a, b, precision=HI if hi else None, preferred_element_type=F32)


def _dot_nt(a, b):
    return lax.dot_general(a, b, (((1,), (1,)), ((), ())), preferred_element_type=F32)


def _dot_tn(a, b):
    return lax.dot_general(a, b, (((0,), (0,)), ((), ())), preferred_element_type=F32)


def _sigmoid(x):
    return 1.0 / (1.0 + jnp.exp(-x))


def _softplus(x):
    return jnp.maximum(x, 0.0) + jnp.log1p(jnp.exp(-jnp.abs(x)))


def _silu(x):
    return x * _sigmoid(x)


def _dsilu(x):
    s = _sigmoid(x)
    return s * (1.0 + x * (1.0 - s))


_GELU_C = math.sqrt(2.0 / math.pi)


def _gelu(x):
    return 0.5 * x * (1.0 + jnp.tanh(_GELU_C * (x + 0.044715 * x * x * x)))


def _dgelu(x):
    t = jnp.tanh(_GELU_C * (x + 0.044715 * x * x * x))
    return 0.5 * (1.0 + t) + 0.5 * x * (1.0 - t * t) * _GELU_C * (1.0 + 3.0 * 0.044715 * x * x)


def _expm1(x):
    series = x * (1.0 + x * 0.5 * (1.0 + x * (1.0 / 3.0) * (1.0 + x * 0.25 * (1.0 + x * 0.2))))
    return jnp.where(jnp.abs(x) < 0.05, series, jnp.exp(x) - 1.0)


def _rowwise(fn, ins, outs, *, bcast=(), reds=(), tm=None, name, period=None):
    t_rows = ins[0].shape[0]
    if tm is None:
        widest = max([a.shape[1] for a in ins] + [c for c, _ in outs])
        tm = _pick(math.gcd(t_rows, period or t_rows),
                   [c for c in (384, 256, 128, 64, 32, 16, 8) if c * widest <= _ROWWISE_TILE_ELEMS or c == 8])
    nt = t_rows // tm
    assert t_rows % tm == 0 and (period is None or period % tm == 0)
    n_in, n_out = len(ins) + len(bcast), len(outs)

    def body(*refs):
        i = pl.program_id(0)
        pos = None
        if period is not None:
            pos = (i * tm) % period + lax.broadcasted_iota(jnp.int32, (tm, 1), 0)
        res = fn(pos, *[r[...] for r in refs[:n_in]])
        res = res if isinstance(res, tuple) else (res,)
        for r, v in zip(refs[n_in:n_in + n_out], res[:n_out]):
            r[...] = v.astype(r.dtype)
        red_refs = refs[n_in + n_out:]
        if red_refs:
            @pl.when(i == 0)
            def _():
                for r in red_refs:
                    r[...] = jnp.zeros_like(r)
            for r, v in zip(red_refs, res[n_out:]):
                r[...] += v

    in_specs = [pl.BlockSpec((tm, a.shape[1]), lambda i: (i, 0)) for a in ins]
    in_specs += [pl.BlockSpec(b.shape, lambda i, n=b.ndim: (0,) * n) for b in bcast]
    out_specs = [pl.BlockSpec((tm, c), lambda i: (i, 0)) for c, _ in outs]
    out_specs += [pl.BlockSpec(s, lambda i: (0, 0)) for s in reds]
    out_shape = [_sds((t_rows, c), dt) for c, dt in outs] + [_sds(s, F32) for s in reds]
    res = _pcall(body, name=name, grid=(nt,), in_specs=in_specs, out_specs=out_specs, out_shape=out_shape,
                 compiler_params=_cparams(("arbitrary",) if reds else ("parallel",)))(*ins, *bcast)
    return res


_TM = (768, 384, 256, 128)
_TN = (1536, 1408, 1024, 768, 512, 640, 384, 256, 128)
_TK = (1536, 1024, 2816, 1408, 512, 384, 256, 128)
_TKO = (1024, 1408, 512, 384, 256, 128)


def _mm_nn(a, b, out_dtype, *, res=None, alpha=1.0, name):
    m, k = a.shape
    k2, n = b.shape
    assert k == k2
    tm, tn, tk = _pick(m, _TM), _pick(n, _TN), _pick(k, _TK)
    nk = k // tk

    def body(*refs):
        if res is None:
            a_ref, b_ref, o_ref, acc = refs
            r_ref = None
        else:
            a_ref, b_ref, r_ref, o_ref, acc = refs
        kk = pl.program_id(2)

        @pl.when(kk == 0)
        def _():
            acc[...] = jnp.zeros_like(acc)

        acc[...] += _dot(a_ref[...].astype(BF16), b_ref[...].astype(BF16))

        @pl.when(kk == nk - 1)
        def _():
            v = acc[...]
            if alpha != 1.0:
                v = v * alpha
            if r_ref is not None:
                v = r_ref[...].astype(F32) + v
            o_ref[...] = v.astype(o_ref.dtype)

    in_specs = [pl.BlockSpec((tm, tk), lambda j, i, kk: (i, kk)), pl.BlockSpec((tk, tn), lambda j, i, kk: (kk, j))]
    args = [a, b]
    if res is not None:
        in_specs.append(pl.BlockSpec((tm, tn), lambda j, i, kk: (i, j)))
        args.append(res)
    return _pcall(body, name=name, grid=(n // tn, m // tm, nk), in_specs=in_specs,
                  out_specs=pl.BlockSpec((tm, tn), lambda j, i, kk: (i, j)), out_shape=_sds((m, n), out_dtype),
                  scratch_shapes=[pltpu.VMEM((tm, tn), F32)],
                  compiler_params=_cparams(("parallel", "parallel", "arbitrary")))(*args)


def _mm_swiglu(a, wg, wu, name):
    m, k = a.shape
    f = wg.shape[1]
    tm, tn, tk = _pick(m, _TM), _pick(f, _TN), _pick(k, _TK)
    nk = k // tk

    def body(a_ref, g_w, u_w, g_ref, u_ref, act_ref, accg, accu):
        kk = pl.program_id(2)

        @pl.when(kk == 0)
        def _():
            accg[...] = jnp.zeros_like(accg)
            accu[...] = jnp.zeros_like(accu)

        av = a_ref[...].astype(BF16)
        accg[...] += _dot(av, g_w[...])
        accu[...] += _dot(av, u_w[...])

        @pl.when(kk == nk - 1)
        def _():
            g, u = accg[...], accu[...]
            g_ref[...] = g.astype(g_ref.dtype)
            u_ref[...] = u.astype(u_ref.dtype)
            act_ref[...] = (_silu(g) * u).astype(act_ref.dtype)

    wspec = pl.BlockSpec((tk, tn), lambda j, i, kk: (kk, j))
    ospec = pl.BlockSpec((tm, tn), lambda j, i, kk: (i, j))
    return _pcall(body, name=name, grid=(f // tn, m // tm, nk),
                  in_specs=[pl.BlockSpec((tm, tk), lambda j, i, kk: (i, kk)), wspec, wspec],
                  out_specs=[ospec] * 3, out_shape=[_sds((m, f), BF16)] * 3,
                  scratch_shapes=[pltpu.VMEM((tm, tn), F32)] * 2,
                  compiler_params=_cparams(("parallel", "parallel", "arbitrary")))(a, wg, wu)


def _mm_dswiglu(dout, wd_t, g, u, alpha, name):
    m, k = dout.shape
    f = wd_t.shape[1]
    tm, tn, tk = _pick(m, _TM), _pick(f, _TN), _pick(k, _TK)
    nk = k // tk

    def body(a_ref, w_ref, g_ref, u_ref, dg_ref, du_ref, acc):
        kk = pl.program_id(2)

        @pl.when(kk == 0)
        def _():
            acc[...] = jnp.zeros_like(acc)

        acc[...] += _dot(a_ref[...].astype(BF16), w_ref[...])

        @pl.when(kk == nk - 1)
        def _():
            dact = acc[...] * alpha
            gv, uv = g_ref[...].astype(F32), u_ref[...].astype(F32)
            dg_ref[...] = (dact * uv * _dsilu(gv)).astype(dg_ref.dtype)
            du_ref[...] = (dact * _silu(gv)).astype(du_ref.dtype)

    ospec = pl.BlockSpec((tm, tn), lambda j, i, kk: (i, j))
    return _pcall(body, name=name, grid=(f // tn, m // tm, nk),
                  in_specs=[pl.BlockSpec((tm, tk), lambda j, i, kk: (i, kk)),
                            pl.BlockSpec((tk, tn), lambda j, i, kk: (kk, j)), ospec, ospec],
                  out_specs=[ospec] * 2, out_shape=[_sds((m, f), BF16)] * 2,
                  scratch_shapes=[pltpu.VMEM((tm, tn), F32)],
                  compiler_params=_cparams(("parallel", "parallel", "arbitrary")))(dout, wd_t, g, u)


def _mm_tn(a, b, *, alpha=1.0, name):
    m, k = a.shape
    m2, n = b.shape
    assert m == m2
    tm, tn, tko = _pick(m, _TM), _pick(n, _TN), _pick(k, _TKO)
    nm = m // tm

    def body(a_ref, b_ref, o_ref, acc):
        mm = pl.program_id(2)

        @pl.when(mm == 0)
        def _():
            acc[...] = jnp.zeros_like(acc)

        acc[...] += _dot_tn(a_ref[...].astype(BF16), b_ref[...].astype(BF16))

        @pl.when(mm == nm - 1)
        def _():
            v = acc[...]
            o_ref[...] = v * alpha if alpha != 1.0 else v

    return _pcall(body, name=name, grid=(k // tko, n // tn, nm),
                  in_specs=[pl.BlockSpec((tm, tko), lambda i, j, mm: (mm, i)),
                            pl.BlockSpec((tm, tn), lambda i, j, mm: (mm, j))],
                  out_specs=pl.BlockSpec((tko, tn), lambda i, j, mm: (i, j)), out_shape=_sds((k, n), F32),
                  scratch_shapes=[pltpu.VMEM((tko, tn), F32)],
                  compiler_params=_cparams(("parallel", "parallel", "arbitrary")))(a, b)


def _rms_fwd(h, g, name):
    def fn(_, hv, gv):
        r = lax.rsqrt(jnp.mean(hv * hv, axis=1, keepdims=True) + NORM_EPS)
        return hv * r * gv
    return _rowwise(fn, [h], [(h.shape[1], BF16)], bcast=[g], name=name)[0]


def _rms_bwd(h, dxn, dres, g, name):
    d = h.shape[1]

    def fn(_, hv, dv, rv, gv):
        r = lax.rsqrt(jnp.mean(hv * hv, axis=1, keepdims=True) + NORM_EPS)
        xh = hv * r
        dxh = dv * gv
        dh = r * (dxh - xh * jnp.mean(dxh * xh, axis=1, keepdims=True))
        return rv + dh, jnp.sum(dv * xh, axis=0, keepdims=True)
    return _rowwise(fn, [h, dxn, dres], [(d, F32)], bcast=[g], reds=[(1, d)], name=name)


def _merge_fwd(mg, ba, bb, bc, name):
    d = ba.shape[1]

    def fn(_, m, a, b, c):
        g = _sigmoid(m.astype(F32))
        return g[:, :d] * a.astype(F32) + g[:, d:2 * d] * b.astype(F32) + g[:, 2 * d:] * c.astype(F32)
    return _rowwise(fn, [mg, ba, bb, bc], [(d, BF16)], name=name)[0]


def _merge_bwd(mg, ba, bb, bc, dmix, name):
    d = ba.shape[1]

    def fn(_, m, a, b, c, dm):
        g = _sigmoid(m.astype(F32))
        dm = dm.astype(F32)
        br = (a.astype(F32), b.astype(F32), c.astype(F32))
        douts, dgs = [], []
        for j in range(3):
            gj = g[:, j * d:(j + 1) * d]
            douts.append(dm * gj)
            dgs.append(dm * br[j] * gj * (1.0 - gj))
        return (*douts, jnp.concatenate(dgs, axis=1))
    return _rowwise(fn, [mg, ba, bb, bc, dmix], [(d, BF16)] * 3 + [(3 * d, BF16)], name=name)


def _loss_head(h, tgt, g, seq_len, name):
    d = h.shape[1]

    def fn(pos, hv, tv, gv):
        r = lax.rsqrt(jnp.mean(hv * hv, axis=1, keepdims=True) + NORM_EPS)
        xh = hv * r
        real = (pos >= N_META) & (pos < N_META + SEQ)
        e = jnp.where(real, xh * gv - tv, 0.0)
        part = jnp.sum(jnp.sum(e * e, axis=1, keepdims=True), axis=0, keepdims=True) * (0.5 / d)
        dy = e * (1.0 / d)
        dxh = dy * gv
        dh = r * (dxh - xh * jnp.mean(dxh * xh, axis=1, keepdims=True))
        return dh, jnp.broadcast_to(part, (1, 128)), jnp.sum(dy * xh, axis=0, keepdims=True)
    return _rowwise(fn, [h, tgt], [(d, F32)], bcast=[g], reds=[(1, 128), (1, d)], name=name, period=seq_len)


def _adamw(w, g, m, v, name):
    c1 = 1.0 - ADAM_B1 ** ADAM_STEP
    c2 = 1.0 - ADAM_B2 ** ADAM_STEP
    wd = w.shape[1]

    def fn(_, wv, gv, mv, vv):
        mn = ADAM_B1 * mv + (1.0 - ADAM_B1) * gv
        vn = ADAM_B2 * vv + (1.0 - ADAM_B2) * (gv * gv)
        delta = -ADAM_LR * ((mn / c1) / (jnp.sqrt(vn / c2) + ADAM_EPS) + ADAM_WD * wv)
        return delta, mn, vn
    return _rowwise(fn, [w, g, m, v], [(wd, F32)] * 3, name=name)


def _sum_rows(parts, out_dtype, name):
    def fn(_, *vs):
        acc = vs[0].astype(F32)
        for v in vs[1:]:
            acc = acc + v.astype(F32)
        return acc
    return _rowwise(fn, list(parts), [(parts[0].shape[1], out_dtype)], name=name)[0]


def _cumsum_seq(x, reverse, name):
    b, l, w = x.shape
    q = 128
    nc = l // q

    def body(x_ref, o_ref):
        row = lax.broadcasted_iota(jnp.int32, (q, q), 0)
        col = lax.broadcasted_iota(jnp.int32, (q, q), 1)
        tri = ((row <= col) if reverse else (row >= col)).astype(F32)
        rsel = lax.broadcasted_iota(jnp.int32, (q, w), 0) == (0 if reverse else q - 1)

        def step(i, carry):
            j = (nc - 1 - i) if reverse else i
            start = pl.multiple_of(j * q, q)
            cs = _dot(tri, x_ref[pl.ds(start, q), :], hi=True) + carry
            o_ref[pl.ds(start, q), :] = cs
            return jnp.sum(jnp.where(rsel, cs, 0.0), axis=0, keepdims=True)

        lax.fori_loop(0, nc, step, jnp.zeros((1, w), F32))

    return _pcall(body, name=name, grid=(b,), in_specs=[pl.BlockSpec((None, l, w), lambda i: (i, 0, 0))],
                  out_specs=pl.BlockSpec((None, l, w), lambda i: (i, 0, 0)), out_shape=_sds(x.shape, F32),
                  compiler_params=_cparams(("parallel",)))(x)


_HALO = 16


def _conv_tiles(l, c):
    return _pick(l, (384, 256, 128)), _pick(c, (512, 256, 128))


def _conv_fwd(x, w, bias, out_dtype, name, with_silu=False):
    b, l, c = x.shape
    tt, cw = _conv_tiles(l, c)

    def body(x_ref, h_ref, w_ref, b_ref, *o_refs):
        t = pl.program_id(2)
        halo = jnp.where(t == 0, 0.0, h_ref[...].astype(F32))
        xe = jnp.concatenate([halo, x_ref[...].astype(F32)], axis=0)
        wv = w_ref[...]
        acc = b_ref[...] + wv[CONV_K - 1:CONV_K, :] * xe[_HALO:]
        for j in range(CONV_K - 1):
            acc = acc + wv[j:j + 1, :] * pltpu.roll(xe, CONV_K - 1 - j, 0)[_HALO:]
        o_refs[0][...] = acc.astype(o_refs[0].dtype)
        if with_silu:
            o_refs[1][...] = _silu(acc.astype(o_refs[0].dtype).astype(F32)).astype(o_refs[1].dtype)

    ospec = pl.BlockSpec((None, tt, cw), lambda i, j, t: (i, t, j))
    n_out = 2 if with_silu else 1
    res = _pcall(body, name=name, grid=(b, c // cw, l // tt),
                 in_specs=[pl.BlockSpec((None, tt, cw), lambda i, j, t: (i, t, j)),
                           pl.BlockSpec((None, _HALO, cw), lambda i, j, t: (i, jnp.maximum(t * (tt // _HALO) - 1, 0), j)),
                           pl.BlockSpec((CONV_K, cw), lambda i, j, t: (0, j)),
                           pl.BlockSpec((1, cw), lambda i, j, t: (0, j))],
                 out_specs=[ospec] * n_out, out_shape=[_sds(x.shape, out_dtype)] * n_out,
                 compiler_params=_cparams(("parallel", "parallel", "parallel")))(x, x, w, bias)
    return res if with_silu else res[0]


def _conv_bwd(x, dy, w, name):
    b, l, c = x.shape
    tt, cw = _conv_tiles(l, c)
    nt = l // tt

    def body(x_ref, xh_ref, d_ref, dh_ref, w_ref, dx_ref, dw_ref):
        i, t = pl.program_id(1), pl.program_id(2)
        halo = jnp.where(t == 0, 0.0, xh_ref[...].astype(F32))
        xe = jnp.concatenate([halo, x_ref[...].astype(F32)], axis=0)
        dv = d_ref[...].astype(F32)
        nxt = jnp.where(t == nt - 1, 0.0, dh_ref[...].astype(F32))
        de = jnp.concatenate([dv, nxt], axis=0)
        wv = w_ref[...]
        dx = wv[CONV_K - 1:CONV_K, :] * dv
        rowid = lax.broadcasted_iota(jnp.int32, (8, 1), 0)
        part = jnp.where(rowid == CONV_K, jnp.sum(dv, axis=0, keepdims=True), 0.0)
        part = part + jnp.where(rowid == CONV_K - 1, jnp.sum(dv * xe[_HALO:], axis=0, keepdims=True), 0.0)
        for j in range(CONV_K - 1):
            s = CONV_K - 1 - j
            dx = dx + wv[j:j + 1, :] * pltpu.roll(de, tt + _HALO - s, 0)[:tt]
            xs = pltpu.roll(xe, s, 0)[_HALO:]
            part = part + jnp.where(rowid == j, jnp.sum(dv * xs, axis=0, keepdims=True), 0.0)
        dx_ref[...] = dx.astype(dx_ref.dtype)

        @pl.when((i == 0) & (t == 0))
        def _():
            dw_ref[...] = jnp.zeros_like(dw_ref)
        dw_ref[...] += part

    return _pcall(body, name=name, grid=(c // cw, b, nt),
                  in_specs=[pl.BlockSpec((None, tt, cw), lambda j, i, t: (i, t, j)),
                            pl.BlockSpec((None, _HALO, cw), lambda j, i, t: (i, jnp.maximum(t * (tt // _HALO) - 1, 0), j)),
                            pl.BlockSpec((None, tt, cw), lambda j, i, t: (i, t, j)),
                            pl.BlockSpec((None, _HALO, cw),
                                         lambda j, i, t: (i, jnp.minimum((t + 1) * (tt // _HALO), l // _HALO - 1), j)),
                            pl.BlockSpec((CONV_K, cw), lambda j, i, t: (0, j))],
                  out_specs=[pl.BlockSpec((None, tt, cw), lambda j, i, t: (i, t, j)),
                             pl.BlockSpec((8, cw), lambda j, i, t: (0, j))],
                  out_shape=[_sds(x.shape, BF16), _sds((8, c), F32)],
                  compiler_params=_cparams(("parallel", "arbitrary", "arbitrary")))(x, x, dy, dy, w)


def _linear_scan(a, u, reverse, name):
    b, l, c = a.shape
    tt = 128
    cw = _pick(c, (512, 256, 128))
    nt = l // tt

    def body(a_ref, u_ref, h_ref, carry):
        t = pl.program_id(2)

        @pl.when(t == 0)
        def _():
            carry[...] = jnp.zeros_like(carry)

        av, uv = a_ref[...], u_ref[...]
        row = lax.broadcasted_iota(jnp.int32, (tt, cw), 0)
        k = 1
        while k < tt:
            if reverse:
                keep = row < tt - k
                a_sh = jnp.where(keep, pltpu.roll(av, tt - k, 0), 1.0)
                u_sh = jnp.where(keep, pltpu.roll(uv, tt - k, 0), 0.0)
            else:
                keep = row >= k
                a_sh = jnp.where(keep, pltpu.roll(av, k, 0), 1.0)
                u_sh = jnp.where(keep, pltpu.roll(uv, k, 0), 0.0)
            uv = uv + av * u_sh
            av = av * a_sh
            k *= 2
        hv = uv + av * carry[0:1, :]
        h_ref[...] = hv
        edge = jnp.sum(jnp.where(row == (0 if reverse else tt - 1), hv, 0.0), axis=0, keepdims=True)
        carry[...] = jnp.broadcast_to(edge, carry.shape)

    tmap = (lambda i, j, t: (i, nt - 1 - t, j)) if reverse else (lambda i, j, t: (i, t, j))
    spec = pl.BlockSpec((None, tt, cw), tmap)
    return _pcall(body, name=name, grid=(b, c // cw, nt), in_specs=[spec, spec], out_specs=spec,
                  out_shape=_sds(a.shape, F32), scratch_shapes=[pltpu.VMEM((8, cw), F32)],
                  compiler_params=_cparams(("parallel", "parallel", "arbitrary")))(a, u)


ATTN_W = 128
_AUG_C = 0
_AUG_ONE = 3
_AUG_LSE = 6


def _attn_blk(l):
    return _pick(l, (384, 256, 128))


def _split3(x):
    x1 = x.astype(BF16).astype(F32)
    x2 = (x - x1).astype(BF16).astype(F32)
    x3 = (x - x1 - x2).astype(BF16).astype(F32)
    return x1, x2, x3


def _aug_lanes(lane, base, cols, ones_at=()):
    out = jnp.zeros(lane.shape, F32)
    for o in ones_at:
        out = out + ((lane >= o) & (lane < o + 3)).astype(F32)
    for k, v in enumerate(cols):
        out = jnp.where(lane == base + k, v, out)
    return out


def _pair_specs(blk, nb):
    at = lambda ww: pl.BlockSpec((None, 2, blk, ww), lambda bi, p, i: (bi, p, i, 0))
    whole = pl.BlockSpec((None, 2, nb, blk, ATTN_W), lambda bi, p, i: (bi, p, 0, 0, 0))
    rows = pl.BlockSpec((None, blk, ATTN_W), lambda bi, p, i: (bi, i, p))
    return at, whole, rows


def _attn_prep(pm3, cum, col0, name):
    b, l, _ = pm3.shape
    dh, nh = ATTN_HEAD_DIM, ATTN_HEADS
    blk = _attn_blk(l)
    scale = dh ** -0.5

    def body(q_ref, k_ref, v_ref, c_ref, qa_ref, ka_ref, va_ref):
        pair = pl.program_id(1)
        lane = lax.broadcasted_iota(jnp.int32, (1, ATTN_W), 1)
        head = lane < dh
        cv = c_ref[...]
        qf, kf, vf = (r[...].astype(F32) for r in (q_ref, k_ref, v_ref))
        for e in range(2):
            c1, c2, c3 = _split3(jnp.sum(jnp.where(lane == 2 * pair + e, cv, 0.0), axis=1, keepdims=True))
            qe, ke, ve = (pltpu.roll(t, dh, 1) for t in (qf, kf, vf)) if e else (qf, kf, vf)
            qa_ref[e] = jnp.where(head, qe * scale,
                                  _aug_lanes(lane, dh + _AUG_C, (c1, c2, c3), (dh + _AUG_ONE,))).astype(BF16)
            ka_ref[e] = jnp.where(head, ke, _aug_lanes(lane, dh + _AUG_ONE, (-c1, -c2, -c3),
                                                       (dh + _AUG_C, dh + _AUG_LSE))).astype(BF16)
            va_ref[e] = jnp.where(head, ve, _aug_lanes(lane, dh, (), (dh,))).astype(BF16)

    at, _, _ = _pair_specs(blk, l // blk)
    cols = lambda c0: pl.BlockSpec((None, blk, ATTN_W), lambda bi, p, i, c0=c0: (bi, i, c0 // ATTN_W + p))
    return _pcall(body, name=name, grid=(b, nh // 2, l // blk),
                  in_specs=[cols(col0[0]), cols(col0[1]), cols(col0[2]),
                            pl.BlockSpec((None, blk, SMALL_W), lambda bi, p, i: (bi, i, 0))],
                  out_specs=[at(ATTN_W)] * 3, out_shape=[_sds((b, nh, l, ATTN_W), BF16)] * 3,
                  compiler_params=_cparams(("parallel", "parallel", "parallel")))(pm3, pm3, pm3, cum)


def _attn_prep_bwd(dy3, y3, qa, lse, name):
    b, nh, l, _ = qa.shape
    dh = ATTN_HEAD_DIM
    blk = _attn_blk(l)

    def body(dy_ref, y_ref, qa_ref, lse_ref, qa2_ref, doa_ref):
        lane = lax.broadcasted_iota(jnp.int32, (1, ATTN_W), 1)
        dyf = dy_ref[...].astype(F32)
        prod = dyf * y_ref[...].astype(F32)
        for e in range(2):
            mine = (lane >= dh) if e else (lane < dh)
            d1, d2, d3 = _split3(jnp.sum(jnp.where(mine, prod, 0.0), axis=1, keepdims=True))
            l1, l2, l3 = _split3(lse_ref[e])
            dye = pltpu.roll(dyf, dh, 1) if e else dyf
            doa_ref[e] = jnp.where(lane < dh, dye, _aug_lanes(lane, dh, (-d1, -d2, -d3))).astype(BF16)
            qa2 = qa_ref[e].astype(F32)
            for k, lv in enumerate((l1, l2, l3)):
                qa2 = jnp.where(lane == dh + _AUG_LSE + k, -lv, qa2)
            qa2_ref[e] = qa2.astype(BF16)

    at, _, rows = _pair_specs(blk, l // blk)
    return _pcall(body, name=name, grid=(b, nh // 2, l // blk), in_specs=[rows, rows, at(ATTN_W), at(1)],
                  out_specs=[at(ATTN_W)] * 2, out_shape=[_sds(qa.shape, BF16)] * 2,
                  compiler_params=_cparams(("parallel", "parallel", "parallel")))(dy3, y3, qa, lse)


def _flash_fwd(qa, ka, va, d_model, name):
    b, h, l, w = qa.shape
    dh = ATTN_HEAD_DIM
    blk = _attn_blk(l)
    nb = l // blk
    kr, vr = ka.reshape(b, h, nb, blk, w), va.reshape(b, h, nb, blk, w)

    def body(q_ref, k_ref, v_ref, o_ref, lse_ref):
        i = pl.program_id(2)
        row = lax.broadcasted_iota(jnp.int32, (blk, blk), 0)
        col = lax.broadcasted_iota(jnp.int32, (blk, blk), 1)

        def scores(e, j):
            return _dot_nt(q_ref[e], k_ref[e, j])

        def consume(e, j, s, m, acc):
            mn = jnp.maximum(m, jnp.max(s, axis=1, keepdims=True))
            return mn, jnp.exp(m - mn) * acc + _dot(jnp.exp(s - mn).astype(BF16), v_ref[e, j])

        def step(j, carry):
            out = []
            for e in range(2):
                m, acc, s = carry[3 * e:3 * e + 3]
                s_next = scores(e, j + 1)
                out += [*consume(e, j, s, m, acc), s_next]
            return tuple(out)

        init = tuple(t for e in range(2)
                     for t in (jnp.full((blk, 1), NEG, F32), jnp.zeros((blk, w), F32), scores(e, 0)))
        carry = lax.fori_loop(0, i, step, init)
        m0, a0 = consume(0, i, jnp.where(col <= row, carry[2], NEG), carry[0], carry[1])
        m1, a1 = consume(1, i, jnp.where(col <= row, carry[5], NEG), carry[3], carry[4])
        l0, l1 = a0[:, dh:dh + 1], a1[:, dh:dh + 1]
        lane = lax.broadcasted_iota(jnp.int32, (1, w), 1)
        o_ref[...] = jnp.where(lane < dh, a0 / l0, pltpu.roll(a1 / l1, dh, 1)).astype(o_ref.dtype)
        lse_ref[0] = m0 + jnp.log(l0)
        lse_ref[1] = m1 + jnp.log(l1)

    at, whole, rows = _pair_specs(blk, nb)
    return _pcall(body, name=name, grid=(b, h // 2, nb), in_specs=[at(w), whole, whole],
                  out_specs=[rows, at(1)], out_shape=[_sds((b, l, d_model), BF16), _sds((b, h, l, 1), F32)],
                  compiler_params=_cparams(("parallel", "parallel", "parallel")))(qa, kr, vr)


def _flash_bwd(qa, ka, va, doa, d_model, name):
    b, h, l, w = qa.shape
    dh = ATTN_HEAD_DIM
    blk = _attn_blk(l)
    nb = l // blk
    scale = dh ** -0.5
    r5 = lambda t: t.reshape(b, h, nb, blk, w)

    def body(k_ref, v_ref, q_ref, do_ref, dq_ref, dk_ref, dv_ref, dcq_ref, dck_ref, dq_acc):
        j = pl.program_id(2)
        row = lax.broadcasted_iota(jnp.int32, (blk, blk), 0)
        col = lax.broadcasted_iota(jnp.int32, (blk, blk), 1)
        lane = lax.broadcasted_iota(jnp.int32, (1, w), 1)

        @pl.when(j == 0)
        def _():
            dq_acc[...] = jnp.zeros_like(dq_acc)

        def contrib(i, masked, carry):
            out = []
            for e in range(2):
                qv, dov = q_ref[e, i], do_ref[e, i]
                p = jnp.exp(_dot_nt(qv, k_ref[e]))
                if masked:
                    p = jnp.where(col <= row, p, 0.0)
                ds = (p * _dot_nt(dov, v_ref[e])).astype(BF16)
                dq_acc[e, i] += _dot(ds, k_ref[e])
                out += [carry[2 * e] + _dot_tn(ds, qv), carry[2 * e + 1] + _dot_tn(p.astype(BF16), dov)]
            return tuple(out)

        zero = (jnp.zeros((blk, w), F32),) * 4
        dk0, dv0, dk1, dv1 = lax.fori_loop(j + 1, nb, lambda i, c: contrib(i, False, c), contrib(j, True, zero))
        dk_ref[...] = jnp.where(lane < dh, dk0, pltpu.roll(dk1, dh, 1)).astype(dk_ref.dtype)
        dv_ref[...] = jnp.where(lane < dh, dv0, pltpu.roll(dv1, dh, 1)).astype(dv_ref.dtype)
        for e, dk in enumerate((dk0, dk1)):
            dck_ref[e] = jnp.sum(jnp.where(lane == dh + _AUG_ONE, dk, 0.0), axis=1, keepdims=True)

        @pl.when(j == nb - 1)
        def _():
            for ib in range(nb):
                rs = pl.ds(ib * blk, blk)
                dq0, dq1 = dq_acc[0, ib], dq_acc[1, ib]
                dq_ref[rs, :] = (jnp.where(lane < dh, dq0, pltpu.roll(dq1, dh, 1)) * scale).astype(dq_ref.dtype)
                for e, dq in enumerate((dq0, dq1)):
                    dcq_ref[e, rs, :] = jnp.sum(jnp.where(lane == dh + _AUG_C, dq, 0.0), axis=1, keepdims=True)

    at, whole, rows = _pair_specs(blk, nb)
    seq_rows = pl.BlockSpec((None, l, ATTN_W), lambda bi, p, j: (bi, 0, p))
    seq_col = pl.BlockSpec((None, 2, l, 1), lambda bi, p, j: (bi, p, 0, 0))
    act = _sds((b, l, d_model), BF16)
    col1 = _sds((b, h, l, 1), F32)
    return _pcall(body, name=name, grid=(b, h // 2, nb), in_specs=[at(w), at(w), whole, whole],
                  out_specs=[seq_rows, rows, rows, seq_col, at(1)], out_shape=[act, act, act, col1, col1],
                  scratch_shapes=[pltpu.VMEM((2, nb, blk, w), F32)],
                  compiler_params=_cparams(("parallel", "parallel", "arbitrary")))(ka, va, r5(qa), r5(doa))


def _ssd_dims(d_ssd):
    heads = d_ssd // SSD_HEAD_DIM
    return heads, heads // SSD_GROUPS, d_ssd // SSD_GROUPS


def _ssd_specs(l, ds, seq_map):
    q = SSD_CHUNK
    gn = SSD_GROUPS * SSD_STATE
    row3 = lambda w, cb: pl.BlockSpec((None, q, w), lambda i, c, cb=cb: (i, seq_map(c), cb))
    return dict(
        xs=row3(ds, 0), bm=row3(gn, ds // gn), cm=row3(gn, ds // gn + 1), z=row3(ds, 0), dt=row3(SMALL_W, 0),
        da=row3(SMALL_W, 0), dat=pl.BlockSpec((None, SMALL_W, q), lambda i, c: (i, 0, seq_map(c))),
        e=pl.BlockSpec((SMALL_W, ds), lambda i, c: (0, 0)), et=pl.BlockSpec((ds, SMALL_W), lambda i, c: (0, 0)),
        vec=pl.BlockSpec((1, ds), lambda i, c: (0, 0)), vec128=pl.BlockSpec((1, SMALL_W), lambda i, c: (0, 0)),
        hin=pl.BlockSpec((None, None, SSD_STATE, ds), lambda i, c: (i, seq_map(c), 0, 0)))


def _ssd_common(da, dat, dt, e_mat, xs):
    q = SSD_CHUNK
    row = lax.broadcasted_iota(jnp.int32, (q, q), 0)
    col = lax.broadcasted_iota(jnp.int32, (q, q), 1)
    lower = row >= col
    cs = _dot(lower.astype(F32), da, hi=True)
    cst = _dot(dat, (row <= col).astype(F32), hi=True)
    dtx = _dot(dt, e_mat, hi=True)
    csx = _dot(cs, e_mat, hi=True)
    rowx = lax.broadcasted_iota(jnp.int32, csx.shape, 0)
    totx = jnp.sum(jnp.where(rowx == q - 1, csx, 0.0), axis=0, keepdims=True)
    xf = xs.astype(F32)
    return lower, cs, cst, dtx, csx, totx, xf, xf * dtx


def _ssd_fwd(xbc, z, dt, da, dat, e_mat, dx, nw, name):
    b, l, _ = xbc.shape
    ds = z.shape[2]
    heads, hpg, gw = _ssd_dims(ds)
    q, n = SSD_CHUNK, SSD_STATE
    nc = l // q
    hcol0 = ATTN_HEADS

    def body(xs_ref, bm_ref, cm_ref, z_ref, dt_ref, da_ref, dat_ref, e_ref, dx_ref, nw_ref, y_ref, yraw_ref, hin_ref,
             hst, ydiag):
        c = pl.program_id(1)

        @pl.when(c == 0)
        def _():
            hst[...] = jnp.zeros_like(hst)

        hin = hst[...]
        hin_ref[...] = hin
        lower, cs, cst, dtx, csx, totx, xf, xdt = _ssd_common(da_ref[...], dat_ref[...], dt_ref[...], e_ref[...],
                                                               xs_ref[...])
        bm, cm = bm_ref[...], cm_ref[...]
        dec_end = jnp.exp(totx - csx)
        for g in range(SSD_GROUPS):
            gs = slice(g * gw, (g + 1) * gw)
            bg, cg = bm[:, g * n:(g + 1) * n], cm[:, g * n:(g + 1) * n]
            cb = _dot_nt(cg, bg)
            for e in range(hpg):
                hh = g * hpg + e
                cc = hcol0 + hh
                lm = jnp.exp(jnp.where(lower, cs[:, cc:cc + 1] - cst[cc:cc + 1, :], NEG))
                hs = slice(hh * SSD_HEAD_DIM, (hh + 1) * SSD_HEAD_DIM)
                ydiag[:, hs] = _dot((cb * lm).astype(BF16), xdt[:, hs].astype(BF16))
            sg = _dot_tn(bg, (xdt[:, gs] * dec_end[:, gs]).astype(BF16))
            hst[:, gs] = jnp.exp(totx[:, gs]) * hin[:, gs] + sg
            ydiag[:, gs] += _dot(cg, hin[:, gs].astype(BF16)) * jnp.exp(csx[:, gs])
        yraw = ydiag[...] + dx_ref[...] * xf
        yraw_ref[...] = yraw.astype(yraw_ref.dtype)
        yg = yraw * _silu(z_ref[...].astype(F32))
        nwv = nw_ref[...]
        for g in range(SSD_GROUPS):
            gs = slice(g * gw, (g + 1) * gw)
            r = lax.rsqrt(jnp.mean(yg[:, gs] * yg[:, gs], axis=1, keepdims=True) + NORM_EPS)
            y_ref[:, gs] = (yg[:, gs] * r * nwv[:, gs]).astype(y_ref.dtype)

    sp = _ssd_specs(l, ds, lambda c: c)
    return _pcall(body, name=name, grid=(b, nc),
                  in_specs=[sp['xs'], sp['bm'], sp['cm'], sp['z'], sp['dt'], sp['da'], sp['dat'], sp['e'], sp['vec'],
                            sp['vec']],
                  out_specs=[sp['z'], sp['z'], sp['hin']],
                  out_shape=[_sds((b, l, ds), BF16), _sds((b, l, ds), BF16), _sds((b, nc, n, ds), F32)],
                  scratch_shapes=[pltpu.VMEM((n, ds), F32), pltpu.VMEM((q, ds), F32)],
                  compiler_params=_cparams(("parallel", "arbitrary")))(xbc, xbc, xbc, z, dt, da, dat, e_mat, dx, nw)


def _ssd_bwd(xbc, z, dt, da, dat, e_mat, et_mat, dx, nw, a128, yraw, hin, dy, name):
    b, l, dxw = xbc.shape
    ds = z.shape[2]
    heads, hpg, gw = _ssd_dims(ds)
    q, n = SSD_CHUNK, SSD_STATE
    gn = SSD_GROUPS * n
    nc = l // q
    hcol0 = ATTN_HEADS

    def body(xs_ref, bm_ref, cm_ref, z_ref, dt_ref, da_ref, dat_ref, e_ref, et_ref, dx_ref, nw_ref, a_ref, yraw_ref,
             hin_ref, dy_ref, dxs_ref, dbm_ref, dcm_ref, dz_ref, ddt_ref, dd_ref, dnw_ref, dap_ref, dhs, dxdt, dcsx,
             dtotx):
        i, c = pl.program_id(0), pl.program_id(1)

        @pl.when(c == 0)
        def _():
            dhs[...] = jnp.zeros_like(dhs)

        @pl.when((i == 0) & (c == 0))
        def _():
            dd_ref[...] = jnp.zeros_like(dd_ref)
            dnw_ref[...] = jnp.zeros_like(dnw_ref)
            dap_ref[...] = jnp.zeros_like(dap_ref)

        dtv = dt_ref[...]
        lower, cs, cst, dtx, csx, totx, xf, xdt = _ssd_common(da_ref[...], dat_ref[...], dtv, e_ref[...], xs_ref[...])
        upper = jnp.logical_not(lower) | (lax.broadcasted_iota(jnp.int32, (q, q), 0)
                                          == lax.broadcasted_iota(jnp.int32, (q, q), 1))
        bm, cm = bm_ref[...], cm_ref[...]
        ecs, dec_end, etot = jnp.exp(csx), jnp.exp(totx - csx), jnp.exp(totx)
        yraw = yraw_ref[...].astype(F32)
        zv = z_ref[...].astype(F32)
        sz = _silu(zv)
        yg = yraw * sz
        dyn_ = dy_ref[...].astype(F32)
        nwv = nw_ref[...]
        dygs, dnws = [], []
        for g in range(SSD_GROUPS):
            gs = slice(g * gw, (g + 1) * gw)
            r = lax.rsqrt(jnp.mean(yg[:, gs] * yg[:, gs], axis=1, keepdims=True) + NORM_EPS)
            yn = yg[:, gs] * r
            dn = dyn_[:, gs] * nwv[:, gs]
            dnws.append(jnp.sum(dyn_[:, gs] * yn, axis=0, keepdims=True))
            dygs.append(r * (dn - yn * jnp.mean(dn * yn, axis=1, keepdims=True)))
        dyg = jnp.concatenate(dygs, axis=1)
        dnw_ref[...] += jnp.concatenate(dnws, axis=1)
        dz_ref[...] = (dyg * yraw * _dsilu(zv)).astype(dz_ref.dtype)
        dyv = dyg * sz
        dd_ref[...] += jnp.sum(dyv * xf, axis=0, keepdims=True)
        hin, dh = hin_ref[...], dhs[...]
        lane128 = lax.broadcasted_iota(jnp.int32, (1, SMALL_W), 1)
        dcs = jnp.zeros((q, SMALL_W), F32)
        for g in range(SSD_GROUPS):
            gs = slice(g * gw, (g + 1) * gw)
            bg, cg = bm[:, g * n:(g + 1) * n], cm[:, g * n:(g + 1) * n]
            hg, dhg = hin[:, gs], dh[:, gs]
            hgb, dsb = hg.astype(BF16), dhg.astype(BF16)
            yoff = _dot(cg, hgb) * ecs[:, gs]
            dch = (dyv[:, gs] * ecs[:, gs]).astype(BF16)
            dcg = _dot_nt(dch, hgb)
            dhs[:, gs] = _dot_tn(cg, dch) + etot[:, gs] * dhg
            zg = xdt[:, gs] * dec_end[:, gs]
            dzz = _dot(bg, dsb)
            dbg = _dot_nt(zg.astype(BF16), dsb)
            dxdt_g = dzz * dec_end[:, gs]
            w_end = dzz * zg
            dtotx[:, gs] = jnp.sum(dhg * hg, axis=0, keepdims=True) * etot[:, gs] + jnp.sum(w_end, axis=0, keepdims=True)
            dcsx[:, gs] = dyv[:, gs] * yoff - w_end
            cb, cbt = _dot_nt(cg, bg), _dot_nt(bg, cg)
            dgm = jnp.zeros((q, q), F32)
            for e in range(hpg):
                hh = g * hpg + e
                cc = hcol0 + hh
                ccol, crow = cs[:, cc:cc + 1], cst[cc:cc + 1, :]
                lm = jnp.exp(jnp.where(lower, ccol - crow, NEG))
                lmt = jnp.exp(jnp.where(upper, crow - ccol, NEG))
                mm, mt = cb * lm, cbt * lmt
                hs = slice(hh * SSD_HEAD_DIM, (hh + 1) * SSD_HEAD_DIM)
                dye, xe = dyv[:, hs].astype(BF16), xdt[:, hs].astype(BF16)
                dm, dmt = _dot_nt(dye, xe), _dot_nt(xe, dye)
                dxdt[:, hs] = dxdt_g[:, e * SSD_HEAD_DIM:(e + 1) * SSD_HEAD_DIM] + _dot(mt.astype(BF16), dye)
                dgm = dgm + dm * lm
                rs = jnp.sum(dm * mm, axis=1, keepdims=True) - jnp.sum(dmt * mt, axis=1, keepdims=True)
                dcs = dcs + rs * (lane128 == cc).astype(F32)
            dgb = dgm.astype(BF16)
            dcm_ref[:, g * n:(g + 1) * n] = (dcg + _dot(dgb, bg)).astype(dcm_ref.dtype)
            dbm_ref[:, g * n:(g + 1) * n] = (dbg + _dot_tn(dgb, cg)).astype(dbm_ref.dtype)
        dxd = dxdt[...]
        dxs_ref[...] = (dx_ref[...] * dyv + dxd * dtx).astype(dxs_ref.dtype)
        et = et_ref[...]
        ddt = _dot(dxd * xf, et, hi=True)
        dtot128 = _dot(jnp.broadcast_to(dtotx[...], (8, ds)), et, hi=True)[0:1, :]
        row128 = lax.broadcasted_iota(jnp.int32, (q, SMALL_W), 0)
        dcs = dcs + _dot(dcsx[...], et, hi=True) + jnp.where(row128 == q - 1, dtot128, 0.0)
        dda = _dot(upper.astype(F32), dcs, hi=True)
        ddt_ref[...] = ddt + dda * a_ref[...]
        dap_ref[...] += jnp.sum(dda * dtv, axis=0, keepdims=True)

    rev = lambda c: nc - 1 - c
    sp = _ssd_specs(l, ds, rev)
    row3 = lambda w: pl.BlockSpec((None, q, w), lambda i, c: (i, rev(c), 0))
    acc = lambda w: pl.BlockSpec((1, w), lambda i, c: (0, 0))
    return _pcall(body, name=name, grid=(b, nc),
                  in_specs=[sp['xs'], sp['bm'], sp['cm'], sp['z'], sp['dt'], sp['da'], sp['dat'], sp['e'], sp['et'],
                            sp['vec'], sp['vec'], sp['vec128'], sp['z'], sp['hin'], sp['z']],
                  out_specs=[row3(ds), row3(gn), row3(gn), row3(ds), row3(SMALL_W), acc(ds), acc(ds), acc(SMALL_W)],
                  out_shape=[_sds((b, l, ds), BF16), _sds((b, l, gn), BF16), _sds((b, l, gn), BF16), _sds((b, l, ds), BF16),
                             _sds((b, l, SMALL_W), F32), _sds((1, ds), F32), _sds((1, ds), F32), _sds((1, SMALL_W), F32)],
                  scratch_shapes=[pltpu.VMEM((n, ds), F32), pltpu.VMEM((q, ds), F32), pltpu.VMEM((q, ds), F32),
                                  pltpu.VMEM((1, ds), F32)],
                  compiler_params=_cparams(("arbitrary", "arbitrary")))(
                      xbc, xbc, xbc, z, dt, da, dat, e_mat, et_mat, dx, nw, a128, yraw, hin, dy)


_GROUP_SIZE = {'c': 2, 'xy': 4, 'xyc': 8}
_LOCAL_SPLIT = 16


def _exchange(src, group, scatter, name, nsplit=1, copy_own=True):
    n = _GROUP_SIZE[group]
    rows, width = src.shape[-2:]
    assert src.ndim == (3 if scatter else 2)
    while rows % (8 * nsplit):
        nsplit //= 2
    crow = rows // nsplit
    nlocal = _LOCAL_SPLIT
    while rows % (8 * nlocal):
        nlocal //= 2
    lrow = rows // nlocal

    def body(src_ref, out_ref, send_sems, recv_sems, local_sems):
        x, y, c = lax.axis_index("x"), lax.axis_index("y"), lax.axis_index("c")
        if group == 'c':
            rank = c
            dev = lambda r: (x, y, r)
        elif group == 'xy':
            rank = 2 * x + y
            dev = lambda r: (r // 2, r % 2, c)
        else:
            rank = 4 * x + 2 * y + c
            dev = lambda r: (r // 4, (r // 2) % 2, r % 2)

        def mine_for(r, ck):
            piece = src_ref.at[r] if scatter else src_ref
            return piece.at[pl.ds(ck * crow, crow)]

        def copy(k, ck, pr, dst_rank):
            return pltpu.make_async_remote_copy(
                src_ref=mine_for(pr, ck), dst_ref=out_ref.at[dst_rank].at[pl.ds(ck * crow, crow)],
                send_sem=send_sems.at[k * nsplit + ck], recv_sem=recv_sems.at[k * nsplit + ck], device_id=dev(pr),
                device_id_type=pl.DeviceIdType.MESH)

        locals_ = []
        if copy_own:
            own = src_ref.at[rank] if scatter else src_ref
            for ck in range(nlocal):
                rs = pl.ds(ck * lrow, lrow)
                locals_.append(pltpu.make_async_copy(own.at[rs], out_ref.at[rank].at[rs], local_sems.at[ck]))
                locals_[-1].start()
        peers = [jnp.bitwise_xor(rank, k + 1) for k in range(n - 1)]
        sends = [copy(k, ck, pr, rank) for ck in range(nsplit) for k, pr in enumerate(peers)]
        for cp in sends:
            cp.start()
        for ck in range(nsplit):
            for k, pr in enumerate(peers):
                copy(k, ck, pr, pr).wait_recv()
        for cp in sends:
            cp.wait_send()
        for cp in locals_:
            cp.wait()

    return _pcall(body, name=name, in_specs=[pl.BlockSpec(memory_space=pl.ANY)],
                  out_specs=pl.BlockSpec(memory_space=pl.ANY), out_shape=_sds((n, rows, width), src.dtype),
                  scratch_shapes=[pltpu.SemaphoreType.DMA(((n - 1) * nsplit,)),
                                  pltpu.SemaphoreType.DMA(((n - 1) * nsplit,)),
                                  pltpu.SemaphoreType.DMA((nlocal,))])(src)


def _exchange_multi(srcs, group, scatter, name, single=False, min_copies=16):
    n = _GROUP_SIZE[group]
    assert not single or n == 2
    na = len(srcs)
    shapes = [tuple(s.shape[-2:]) for s in srcs]
    want = max(1, -(-min_copies // (na * (n - 1))))
    splits = []
    for (rows, _), s in zip(shapes, srcs):
        quant = 8 * (4 // s.dtype.itemsize)
        k = want
        while k > 1 and rows % (quant * k):
            k -= 1
        splits.append(k)
    offs = [int(v) for v in np.cumsum([0] + [(n - 1) * k for k in splits])]

    def body(*refs):
        src_refs, out_refs = refs[:na], refs[na:2 * na]
        send_sems, recv_sems = refs[2 * na:]
        x, y, c = lax.axis_index("x"), lax.axis_index("y"), lax.axis_index("c")
        if group == 'c':
            rank = c
            dev = lambda r: (x, y, r)
        elif group == 'xy':
            rank = 2 * x + y
            dev = lambda r: (r // 2, r % 2, c)
        else:
            rank = 4 * x + 2 * y + c
            dev = lambda r: (r // 4, (r // 2) % 2, r % 2)
        peers = [jnp.bitwise_xor(rank, k + 1) for k in range(n - 1)]

        def copy(a, k, ck, dst_rank):
            crow = shapes[a][0] // splits[a]
            rs = pl.ds(ck * crow, crow)
            piece = src_refs[a].at[peers[k]] if scatter else src_refs[a]
            dst = out_refs[a] if single else out_refs[a].at[dst_rank]
            sem = offs[a] + k * splits[a] + ck
            return pltpu.make_async_remote_copy(src_ref=piece.at[rs], dst_ref=dst.at[rs], send_sem=send_sems.at[sem],
                                                recv_sem=recv_sems.at[sem], device_id=dev(peers[k]),
                                                device_id_type=pl.DeviceIdType.MESH)

        todo = [(a, k, ck) for a in range(na) for ck in range(splits[a]) for k in range(n - 1)]
        sends = [copy(a, k, ck, rank) for a, k, ck in todo]
        for cp in sends:
            cp.start()
        for a, k, ck in todo:
            copy(a, k, ck, peers[k]).wait_recv()
        for cp in sends:
            cp.wait_send()

    any_spec = pl.BlockSpec(memory_space=pl.ANY)
    out_shape = [_sds(sh if single else (n,) + sh, s.dtype) for sh, s in zip(shapes, srcs)]
    return _pcall(body, name=name, in_specs=[any_spec] * na, out_specs=[any_spec] * na, out_shape=out_shape,
                  scratch_shapes=[pltpu.SemaphoreType.DMA((offs[-1],)), pltpu.SemaphoreType.DMA((offs[-1],))])(*srcs)


def _sum_slots(arr, out_dtype, name):
    n, rows, cols = arr.shape
    tm = _pick(rows, [c for c in (384, 256, 128, 64, 32, 16, 8) if c * cols <= _ROWWISE_TILE_ELEMS or c == 8])

    def body(*refs):
        acc = refs[0][...].astype(F32)
        for r in refs[1:n]:
            acc = acc + r[...].astype(F32)
        refs[n][...] = acc.astype(refs[n].dtype)

    return _pcall(body, name=name, grid=(rows // tm,),
                  in_specs=[pl.BlockSpec((None, tm, cols), lambda i, j=j: (j, i, 0)) for j in range(n)],
                  out_specs=pl.BlockSpec((tm, cols), lambda i: (i, 0)), out_shape=_sds((rows, cols), out_dtype),
                  compiler_params=_cparams(("parallel",)))(*([arr] * n))


def _dims():
    d = D_MODEL
    h = ATTN_HEADS
    d_ssd = d
    d_xbc = d_ssd + 2 * SSD_GROUPS * SSD_STATE
    sizes = (d, d, d, h, d_ssd, d_xbc, d_ssd // SSD_HEAD_DIM, d, d, 3 * d)
    return d, h, d_ssd, d_xbc, sizes


def _w_in_split(w):
    d, h, d_ssd, d_xbc, sizes = _dims()
    off = np.concatenate([[0], np.cumsum(sizes)])
    seg = lambda i: w[..., off[i]:off[i + 1]]
    main = jnp.concatenate([seg(0), seg(1), seg(2), seg(4), seg(5), seg(7), seg(8), seg(9)], axis=-1)
    pad = jnp.zeros(w.shape[:-1] + (SMALL_W - sizes[3] - sizes[6],), w.dtype)
    small = jnp.concatenate([seg(3), seg(6), pad], axis=-1)
    return main, small


def _w_in_merge(main, small):
    d, h, d_ssd, d_xbc, sizes = _dims()
    order = (0, 1, 2, 4, 5, 7, 8, 9)
    moff = np.concatenate([[0], np.cumsum([sizes[i] for i in order])])
    pieces = {i: main[..., moff[j]:moff[j + 1]] for j, i in enumerate(order)}
    pieces[3] = small[..., :sizes[3]]
    pieces[6] = small[..., sizes[3]:sizes[3] + sizes[6]]
    return jnp.concatenate([pieces[i] for i in range(10)], axis=-1)


def _main_offsets():
    d, h, d_ssd, d_xbc, sizes = _dims()
    names = ('q', 'k', 'v', 'z', 'xbc', 'xr', 'gate', 'merge')
    widths = (d, d, d, d_ssd, d_xbc, d, d, 3 * d)
    off = np.concatenate([[0], np.cumsum(widths)])
    return {nm: (int(off[i]), int(off[i + 1])) for i, nm in enumerate(names)}


def _block_diag(w):
    nb, s, _ = w.shape
    eye = jnp.eye(nb, dtype=w.dtype)
    return (eye[:, None, :, None] * w[:, :, None, :]).reshape(nb * s, nb * s)


def _diag_blocks(wd, nb):
    s = wd.shape[0] // nb
    return jnp.stack([wd[i * s:(i + 1) * s, i * s:(i + 1) * s] for i in range(nb)])


def _vec128(*parts):
    v = jnp.concatenate([p.astype(F32) for p in parts])
    return jnp.pad(v, (0, SMALL_W - v.shape[0]))[None, :]


def _ffn_fwd(h, gnorm, w, tag):
    xn = _rms_fwd(h, gnorm[None, :], f"{tag}_norm")
    g, u, act = _mm_swiglu(xn, w['wg'], w['wu'], f"{tag}_gu")
    out = _mm_nn(act, w['wd'], F32, res=h, alpha=0.5, name=f"{tag}_down")
    return out, (h, xn, g, u, act)


def _ffn_bwd(dout, saved, gnorm, w, tag):
    h, xn, g, u, act = saved
    dg, du = _mm_dswiglu(dout, w['wd_t'], g, u, 0.5, f"{tag}_dgu")
    dwd = _mm_tn(act, dout, alpha=0.5, name=f"{tag}_dwd")
    dwgu = jnp.concatenate([_mm_tn(xn, dg, name=f"{tag}_dwg"), _mm_tn(xn, du, name=f"{tag}_dwu")], axis=1)
    dxn = _mm_nn(dg, w['wg_t'], F32, name=f"{tag}_dxn_g")
    dxn = _mm_nn(du, w['wu_t'], F32, res=dxn, name=f"{tag}_dxn_u")
    dh, dgn = _rms_bwd(h, dxn, dout, gnorm[None, :], f"{tag}_dnorm")
    return dh, dgn[0], dwgu, dwd


def _mixer_fwd(h, p, b, l):
    d, nh, d_ssd, d_xbc, sizes = _dims()
    t = b * l
    off = _main_offsets()
    xn = _rms_fwd(h, p['mix_norm'][None, :], "mix_norm")
    pm = _mm_nn(xn, p['w_main'], BF16, name="mix_in_main")
    ps = _mm_nn(xn, p['w_small'], F32, name="mix_in_small")
    col = lambda nm: pm[:, off[nm][0]:off[nm][1]]
    heads_ssd = d_ssd // SSD_HEAD_DIM
    a_neg = -jnp.exp(p['ssd_a_log'])
    fb = _vec128(p['fox_forget_bias'])
    dtb = _vec128(jnp.zeros((nh,), F32), p['ssd_dt_bias'])
    a128 = _vec128(jnp.zeros((nh,), F32), a_neg)

    def prep(_, v, fbv, dtbv, av):
        lane = lax.broadcasted_iota(jnp.int32, (1, SMALL_W), 1)
        logf = jnp.where(lane < nh, -_softplus(-(v + fbv)), 0.0)
        dtv = jnp.where((lane >= nh) & (lane < nh + heads_ssd), _softplus(v + dtbv), 0.0)
        return logf, dtv, dtv * av
    logf, dt, da = _rowwise(prep, [ps], [(SMALL_W, F32)] * 3, bcast=[fb, dtb, a128], name="mix_prep")

    cum = _cumsum_seq(logf.reshape(b, l, SMALL_W), False, "fox_cumsum")
    qa, ka, va = _attn_prep(pm.reshape(b, l, -1), cum, (off['q'][0], off['k'][0], off['v'][0]), "fox_prep")
    y_a3, lse = _flash_fwd(qa, ka, va, d, "fox_fwd")
    y_a = y_a3.reshape(t, d)

    xbc = col('xbc').reshape(b, l, d_xbc)
    pre_b, xbc_act = _conv_fwd(xbc, p['ssd_conv_w'], p['ssd_conv_b'][None, :], BF16, "ssd_conv", with_silu=True)
    z = col('z').reshape(b, l, d_ssd)
    dt3, da3 = dt.reshape(b, l, SMALL_W), da.reshape(b, l, SMALL_W)
    dat3 = da3.transpose(0, 2, 1)
    e_mat = _expand_matrix(nh, heads_ssd)
    dx = jnp.repeat(p['ssd_d'], SSD_HEAD_DIM)[None, :]
    nw = p['ssd_norm'][None, :]
    y_b3, yraw, hin = _ssd_fwd(xbc_act, z, dt3, da3, dat3, e_mat, dx, nw, "ssd_fwd")
    y_b = y_b3.reshape(t, d_ssd)

    xr = col('xr').reshape(b, l, d)
    xc = _conv_fwd(xr, p['lru_conv_w'], p['lru_conv_b'][None, :], F32, "lru_conv").reshape(t, d)
    pre_ri = _mm_nn(xc, p['lru_w_ri'], F32, name="lru_gates")
    lvec = (p['lru_b_a'][None, :], p['lru_b_x'][None, :], p['lru_lambda'][None, :])
    a_l, u_l = _rowwise(_lru_point_fwd, [pre_ri, xc], [(d, F32)] * 2, bcast=lvec, name="lru_point", period=l)
    hs = _linear_scan(a_l.reshape(b, l, d), u_l.reshape(b, l, d), False, "lru_scan").reshape(t, d)
    gate = col('gate')
    y_c = _rowwise(lambda _, hv, gv: hv * _gelu(gv.astype(F32)), [hs, gate], [(d, BF16)], name="lru_out")[0]

    ba = _mm_nn(y_a, p['w_branch_attn'], BF16, name="branch_attn")
    bb = _mm_nn(y_b, p['w_branch_ssd'], BF16, name="branch_ssd")
    bc = _mm_nn(y_c, p['w_branch_lru'], BF16, name="branch_lru")
    mg = col('merge')
    mixed = _merge_fwd(mg, ba, bb, bc, "merge")
    out = _mm_nn(mixed, p['w_out'], F32, res=h, name="mix_out")
    saved = dict(h=h, xn=xn, ps=ps, fb=fb, dtb=dtb, a128=a128, qa=qa, ka=ka, va=va, lse=lse,
                 xbc=xbc, pre_b=pre_b, xbc_act=xbc_act, z=z, dt3=dt3, da3=da3, dat3=dat3, e_mat=e_mat, dx=dx, nw=nw,
                 yraw=yraw, hin=hin, xr=xr, xc=xc, pre_ri=pre_ri, lvec=lvec, a_l=a_l, hs=hs, gate=gate, y_a=y_a, y_b=y_b,
                 y_c=y_c, ba=ba, bb=bb, bc=bc, mg=mg, mixed=mixed)
    return out, saved


def _expand_matrix(nh, heads_ssd):
    e = np.zeros((SMALL_W, heads_ssd * SSD_HEAD_DIM), np.float32)
    for hh in range(heads_ssd):
        e[nh + hh, hh * SSD_HEAD_DIM:(hh + 1) * SSD_HEAD_DIM] = 1.0
    return jnp.asarray(e)


def _lru_gates(pre, xc, bav, bxv, lamv, pos):
    d = xc.shape[1]
    r = _sigmoid(pre[:, :d] + bav)
    i = _sigmoid(pre[:, d:] + bxv)
    ls = -_softplus(-lamv)
    la = LRU_C * r * ls
    a = jnp.exp(la)
    mult = jnp.where(pos == 0, 1.0, jnp.sqrt(-_expm1(2.0 * la)))
    return r, i, ls, a, mult


def _lru_point_fwd(pos, pre, xc, bav, bxv, lamv):
    r, i, ls, a, mult = _lru_gates(pre, xc, bav, bxv, lamv, pos)
    return a, mult * (i * xc)


def _lru_point_bwd(pos, g, hprev, pre, xc, bav, bxv, lamv):
    r, i, ls, a, mult = _lru_gates(pre, xc, bav, bxv, lamv, pos)
    da = g * hprev
    di = g * mult * xc
    dxc = g * mult * i
    dmult = jnp.where(pos == 0, 0.0, g * i * xc)
    dla = da * a - dmult * (a * a) / mult
    dpre_r = dla * (LRU_C * ls) * r * (1.0 - r)
    dpre_i = di * i * (1.0 - i)
    dlam = jnp.sum(dla * (LRU_C * r), axis=0, keepdims=True) * _sigmoid(-lamv)
    return (jnp.concatenate([dpre_r, dpre_i], axis=1), dxc, dlam, jnp.sum(dpre_r, axis=0, keepdims=True),
            jnp.sum(dpre_i, axis=0, keepdims=True))


def _mixer_bwd(dout, s, p, b, l):
    d, nh, d_ssd, d_xbc, sizes = _dims()
    t = b * l
    heads_ssd = d_ssd // SSD_HEAD_DIM
    g = {}
    dmixed = _mm_nn(dout, p['w_out_t'], BF16, name="mix_out_dx")
    g['w_out'] = _mm_tn(s['mixed'], dout, name="mix_out_dw")
    dba, dbb, dbc, dmerge = _merge_bwd(s['mg'], s['ba'], s['bb'], s['bc'], dmixed, "merge_bwd")
    g['w_branch_attn'] = _mm_tn(s['y_a'], dba, name="branch_attn_dw")
    g['w_branch_ssd'] = _mm_tn(s['y_b'], dbb, name="branch_ssd_dw")
    g['w_branch_lru'] = _mm_tn(s['y_c'], dbc, name="branch_lru_dw")
    dy_a = _mm_nn(dba, p['w_branch_attn_t'], BF16, name="branch_attn_dx")
    dy_b = _mm_nn(dbb, p['w_branch_ssd_t'], BF16, name="branch_ssd_dx")
    dy_c = _mm_nn(dbc, p['w_branch_lru_t'], F32, name="branch_lru_dx")

    dgate, dhs = _rowwise(lambda _, dv, hv, gv: (dv * hv * _dgelu(gv.astype(F32)), dv * _gelu(gv.astype(F32))),
                          [dy_c, s['hs'], s['gate']], [(d, BF16), (d, F32)], name="lru_out_bwd")
    a3 = s['a_l'].reshape(b, l, d)
    a_next = jnp.concatenate([a3[:, 1:], jnp.zeros((b, 1, d), F32)], axis=1)
    gs = _linear_scan(a_next, dhs.reshape(b, l, d), True, "lru_scan_bwd").reshape(t, d)
    h3 = s['hs'].reshape(b, l, d)
    hprev = jnp.concatenate([jnp.zeros((b, 1, d), F32), h3[:, :-1]], axis=1).reshape(t, d)
    dpre_ri, dxc0, dlam, dba_, dbx_ = _rowwise(_lru_point_bwd, [gs, hprev, s['pre_ri'], s['xc']],
                                               [(2 * d, BF16), (d, F32)], bcast=s['lvec'],
                                               reds=[(1, d)] * 3, name="lru_point_bwd", period=l)
    g['lru_lambda'], g['lru_b_a'], g['lru_b_x'] = dlam[0], dba_[0], dbx_[0]
    dxc = _mm_nn(dpre_ri, p['lru_w_ri_t'], BF16, res=dxc0, name="lru_gates_dx")
    dw_ri = _mm_tn(s['xc'], dpre_ri, name="lru_gates_dw")
    g['lru_w_a'] = _diag_blocks(dw_ri[:, :d], LRU_BLOCKS)
    g['lru_w_x'] = _diag_blocks(dw_ri[:, d:], LRU_BLOCKS)
    dxr, dwl = _conv_bwd(s['xr'], dxc.reshape(b, l, d), p['lru_conv_w'], "lru_conv_bwd")
    g['lru_conv_w'], g['lru_conv_b'] = dwl[:CONV_K], dwl[CONV_K]

    et_mat = s['e_mat'].T
    dxs, dbm, dcm, dz, ddt, dd_l, dnw, dap = _ssd_bwd(s['xbc_act'], s['z'], s['dt3'], s['da3'], s['dat3'], s['e_mat'],
                                                      et_mat, s['dx'], s['nw'], s['a128'], s['yraw'], s['hin'],
                                                      dy_b.reshape(b, l, d_ssd), "ssd_bwd")
    g['ssd_d'] = dd_l.reshape(heads_ssd, SSD_HEAD_DIM).sum(axis=1)
    g['ssd_norm'] = dnw[0]
    g['ssd_a_log'] = dap[0, nh:nh + heads_ssd] * (-jnp.exp(p['ssd_a_log']))
    dxbc_act = jnp.concatenate([dxs, dbm, dcm], axis=2).reshape(t, d_xbc)
    dpre_b = _rowwise(lambda _, dv, pv: dv.astype(F32) * _dsilu(pv.astype(F32)),
                      [dxbc_act, s['pre_b'].reshape(t, d_xbc)], [(d_xbc, BF16)], name="ssd_conv_act_bwd")[0]
    dxbc, dws = _conv_bwd(s['xbc'], dpre_b.reshape(b, l, d_xbc), p['ssd_conv_w'], "ssd_conv_bwd")
    g['ssd_conv_w'], g['ssd_conv_b'] = dws[:CONV_K], dws[CONV_K]

    qa2, doa = _attn_prep_bwd(dy_a.reshape(b, l, d), s['y_a'].reshape(b, l, d), s['qa'], s['lse'], "fox_prep_bwd")
    dq3, dk3, dv3, dcq, dck = _flash_bwd(qa2, s['ka'], s['va'], doa, d, "fox_bwd")
    dcum = jnp.pad((dcq - dck)[..., 0].transpose(0, 2, 1), ((0, 0), (0, 0), (0, SMALL_W - nh)))
    dlogf = _cumsum_seq(dcum, True, "fox_cumsum_bwd").reshape(t, SMALL_W)

    def prep_bwd(_, v, dlf, ddtv, fbv, dtbv):
        a_ = dlf * _sigmoid(-(v + fbv))
        b_ = ddtv * _sigmoid(v + dtbv)
        return a_ + b_, jnp.sum(a_, axis=0, keepdims=True), jnp.sum(b_, axis=0, keepdims=True)
    dps, dfb, ddtb = _rowwise(prep_bwd, [s['ps'], dlogf, ddt.reshape(t, SMALL_W)], [(SMALL_W, F32)],
                              bcast=[s['fb'], s['dtb']], reds=[(1, SMALL_W)] * 2, name="mix_prep_bwd")
    g['fox_forget_bias'] = dfb[0, :nh]
    g['ssd_dt_bias'] = ddtb[0, nh:nh + heads_ssd]

    dpm = jnp.concatenate([dq3.reshape(t, d), dk3.reshape(t, d), dv3.reshape(t, d),
                           dz.reshape(t, d_ssd), dxbc.reshape(t, d_xbc), dxr.reshape(t, d), dgate, dmerge], axis=1)
    dxn = _mm_nn(dps, p['w_small_t'], F32, name="mix_in_small_dx")
    dxn = _mm_nn(dpm, p['w_main_t'], F32, res=dxn, name="mix_in_main_dx")
    g['w_main'] = _mm_tn(s['xn'], dpm, name="mix_in_main_dw")
    g['w_small'] = _mm_tn(s['xn'], dps, name="mix_in_small_dw")
    dh, dg = _rms_bwd(s['h'], dxn, dout, p['mix_norm'][None, :], "mix_norm_bwd")
    g['mix_norm'] = dg[0]
    return dh, g


def _layer_params(w, li):
    p = {n: w[n][li] for n in WEIGHTS if n not in ('meta_tokens', 'final_norm')}
    bf = lambda a: a.astype(BF16)
    for tag in ('ffn1', 'ffn2'):
        wgu, wd = bf(p[tag + '_w_gate_up']), bf(p[tag + '_w_down'])
        f = wd.shape[0]
        p[tag] = dict(wg=wgu[:, :f], wu=wgu[:, f:], wg_t=wgu[:, :f].T, wu_t=wgu[:, f:].T, wd=wd, wd_t=wd.T)
    wm, ws = _w_in_split(bf(p['w_in']))
    p['w_main'], p['w_main_t'], p['w_small'], p['w_small_t'] = wm, wm.T, ws, ws.T
    for n in ('w_branch_attn', 'w_branch_ssd', 'w_branch_lru', 'w_out'):
        p[n + '_t'] = bf(p[n]).T
        p[n] = bf(p[n])
    wri = jnp.concatenate([_block_diag(p['lru_w_a']), _block_diag(p['lru_w_x'])], axis=1)
    p['lru_w_ri'], p['lru_w_ri_t'] = bf(wri), bf(wri).T
    return p


def _local_step(x, loss_target, w):
    b, seq, d = x.shape
    length = N_META + seq
    l = -(-length // Q_BLOCK) * Q_BLOCK
    t = b * l
    meta = jnp.broadcast_to(w['meta_tokens'].astype(F32)[None], (b, N_META, d))
    h = jnp.concatenate([meta, x, jnp.zeros((b, l - length, d), F32)], axis=1).reshape(t, d)
    tgt = jnp.concatenate([jnp.zeros((b, N_META, d), F32), loss_target, jnp.zeros((b, l - length, d), F32)],
                          axis=1).reshape(t, d)
    params, saves = [], []
    for li in range(DEPTH):
        p = _layer_params(w, li)
        h, s1 = _ffn_fwd(h, p['ffn1_norm'], p['ffn1'], "ffn1")
        h, sm = _mixer_fwd(h, p, b, l)
        h, s2 = _ffn_fwd(h, p['ffn2_norm'], p['ffn2'], "ffn2")
        params.append(p)
        saves.append((s1, sm, s2))
    dh, loss, dgf = _loss_head(h, tgt, w['final_norm'][None, :], l, "loss_head")
    layer_grads = [None] * DEPTH
    for li in reversed(range(DEPTH)):
        p = params[li]
        s1, sm, s2 = saves[li]
        g = {}
        dh, g['ffn2_norm'], g['ffn2_w_gate_up'], g['ffn2_w_down'] = _ffn_bwd(dh, s2, p['ffn2_norm'], p['ffn2'], "ffn2b")
        dh, gm = _mixer_bwd(dh, sm, p, b, l)
        g.update(gm)
        g['w_in'] = _w_in_merge(g.pop('w_main'), g.pop('w_small'))
        dh, g['ffn1_norm'], g['ffn1_w_gate_up'], g['ffn1_w_down'] = _ffn_bwd(dh, s1, p['ffn1_norm'], p['ffn1'], "ffn1b")
        layer_grads[li] = g
    grads = {n: jnp.stack([layer_grads[li][n] for li in range(DEPTH)]) for n in layer_grads[0]}
    for n in ('lru_w_a', 'lru_w_x'):
        grads[n] = grads[n].reshape(w[n].shape)
    dh3 = dh.reshape(b, l, d)
    grads['meta_tokens'] = jnp.sum(dh3[:, :N_META], axis=0)
    grads['final_norm'] = dgf[0]
    return loss, dh3[:, N_META:N_META + seq], grads


def _unflatten(flat, shapes):
    out, o = [], 0
    for sh in shapes:
        n = int(np.prod(sh))
        out.append(flat[o:o + n].reshape(sh))
        o += n
    return out


def kernel(x, meta_tokens, ffn1_norm, ffn1_w_gate_up, ffn1_w_down, mix_norm, w_in, fox_forget_bias, ssd_conv_w, ssd_conv_b, ssd_dt_bias, ssd_a_log, ssd_d, ssd_norm, lru_conv_w, lru_conv_b, lru_w_a, lru_b_a, lru_w_x, lru_b_x, lru_lambda, w_branch_attn, w_branch_ssd, w_branch_lru, w_out, ffn2_norm, ffn2_w_gate_up, ffn2_w_down, final_norm, loss_target, m_meta_tokens, m_ffn1_norm, m_ffn1_w_gate_up, m_ffn1_w_down, m_mix_norm, m_w_in, m_fox_forget_bias, m_ssd_conv_w, m_ssd_conv_b, m_ssd_dt_bias, m_ssd_a_log, m_ssd_d, m_ssd_norm, m_lru_conv_w, m_lru_conv_b, m_lru_w_a, m_lru_b_a, m_lru_w_x, m_lru_b_x, m_lru_lambda, m_w_branch_attn, m_w_branch_ssd, m_w_branch_lru, m_w_out, m_ffn2_norm, m_ffn2_w_gate_up, m_ffn2_w_down, m_final_norm, v_meta_tokens, v_ffn1_norm, v_ffn1_w_gate_up, v_ffn1_w_down, v_mix_norm, v_w_in, v_fox_forget_bias, v_ssd_conv_w, v_ssd_conv_b, v_ssd_dt_bias, v_ssd_a_log, v_ssd_d, v_ssd_norm, v_lru_conv_w, v_lru_conv_b, v_lru_w_a, v_lru_b_a, v_lru_w_x, v_lru_b_x, v_lru_lambda, v_w_branch_attn, v_w_branch_ssd, v_w_branch_lru, v_w_out, v_ffn2_norm, v_ffn2_w_gate_up, v_ffn2_w_down, v_final_norm):
    args = locals()
    wloc = {n: args[n] for n in WEIGHTS}
    mloc = {n: args['m_' + n] for n in WEIGHTS}
    vloc = {n: args['v_' + n] for n in WEIGHTS}
    nchip = 4
    chip = 2 * lax.axis_index("x") + lax.axis_index("y")
    core = lax.axis_index("c")
    hl = DEPTH // 2

    own = lambda out, mine, rank: lax.dynamic_update_index_in_dim(out, mine, rank, 0)
    half_rows = lambda a, which: lax.dynamic_slice_in_dim(a, which * (a.shape[0] // 2), a.shape[0] // 2, axis=0)
    mine = [half_rows(wloc[n].astype(BF16).reshape(-1, wloc[n].shape[-1]), core) for n in BIG_NAMES]
    got = _exchange_multi(mine, 'xy', False, "gather_w_chips")
    got = [own(g_, m_, chip).reshape(nchip * m_.shape[0], m_.shape[1]) for g_, m_ in zip(got, mine)]
    both = _exchange_multi(got, 'c', False, "gather_w_cores")
    full = {}
    for n, b_, g_ in zip(BIG_NAMES, both, got):
        _, r, c = wloc[n].shape
        v = own(b_, g_, core).reshape(2, nchip, hl, r, c)
        if BIG[n] == 1:
            full[n] = v.transpose(0, 2, 3, 1, 4).reshape(DEPTH, r, nchip * c)
        else:
            full[n] = v.transpose(0, 2, 1, 3, 4).reshape(DEPTH, nchip * r, c)
    cs_shapes = [wloc[n].shape for n in COLSHARD_SMALL]
    cs_total = sum(int(np.prod(s)) for s in cs_shapes)
    cs_rows = -(-cs_total // (8 * 128)) * 8
    cs_flat = jnp.concatenate([wloc[n].reshape(-1) for n in COLSHARD_SMALL])
    cs_flat = jnp.pad(cs_flat, (0, cs_rows * 128 - cs_total)).reshape(cs_rows, 128)
    cs_all = _exchange(cs_flat, 'xy', False, "gather_small").reshape(nchip, -1)
    cs_chip = [_unflatten(cs_all[j], cs_shapes) for j in range(nchip)]
    for i, n in enumerate(COLSHARD_SMALL):
        full[n] = jnp.concatenate([cs_chip[j][i] for j in range(nchip)], axis=-1)
    for n in SMALL_NAMES:
        if n not in COLSHARD_SMALL:
            full[n] = wloc[n]

    loss_part, grad_x, grads = _local_step(x, loss_target, full)

    g2d = [grads[n].reshape(-1, grads[n].shape[-1]) for n in BIG_NAMES]
    give = [half_rows(g_, 1 - core) for g_ in g2d]
    keep = [half_rows(g_, core) for g_ in g2d]
    theirs = _exchange_multi(give, 'c', False, "reduce_cores", single=True)
    psums = []
    for n, k_, t_ in zip(BIG_NAMES, keep, theirs):
        s2 = _sum_rows([k_, t_], BF16, "reduce_cores_sum")
        _, r, c = wloc[n].shape
        if BIG[n] == 1:
            psums.append(s2.reshape(s2.shape[0], nchip, c).transpose(1, 0, 2))
        else:
            psums.append(s2.reshape(hl, nchip, r, c).transpose(1, 0, 2, 3).reshape(nchip, hl * r, c))
    parts = _exchange_multi(psums, 'xy', True, "reduce_chips")
    parts = [own(p_, lax.dynamic_index_in_dim(s_, chip, axis=0, keepdims=False), chip) for p_, s_ in zip(parts, psums)]
    rsums = [_sum_slots(p_, F32, "reduce_chips_sum") for p_ in parts]
    halves = _exchange_multi(rsums, 'c', False, "reduce_share")
    gbig = {n: own(h_, r_, core).reshape(wloc[n].shape) for n, h_, r_ in zip(BIG_NAMES, halves, rsums)}

    sm_shapes = [grads[n].shape for n in SMALL_NAMES]
    sm_total = sum(int(np.prod(s)) for s in sm_shapes) + 128
    sm_rows = -(-sm_total // (8 * 128)) * 8
    sm_flat = jnp.concatenate([loss_part.reshape(-1)] + [grads[n].reshape(-1) for n in SMALL_NAMES])
    sm_flat = jnp.pad(sm_flat, (0, sm_rows * 128 - sm_total)).reshape(sm_rows, 128)
    sm_all = _exchange(sm_flat, 'xyc', False, "gather_small_grads")
    sm_sum = _sum_rows([sm_all[j] for j in range(8)], F32, "small_grads_sum").reshape(-1)
    loss = sm_sum[0]
    gsmall_full = dict(zip(SMALL_NAMES, _unflatten(sm_sum[128:], sm_shapes)))
    gsmall = {}
    for n in SMALL_NAMES:
        gfull = gsmall_full[n]
        if n in COLSHARD_SMALL:
            wcols = wloc[n].shape[-1]
            gfull = lax.dynamic_slice_in_dim(gfull, chip * wcols, wcols, axis=gfull.ndim - 1)
        gsmall[n] = gfull

    big_out = [{}, {}, {}]
    for n in BIG_NAMES:
        rows2d = lambda a: a.reshape(-1, a.shape[-1])
        res = _adamw(rows2d(wloc[n]), rows2d(gbig[n]), rows2d(mloc[n]), rows2d(vloc[n]), "adamw_big")
        for k in range(3):
            big_out[k][n] = res[k].reshape(wloc[n].shape)
    loc_shapes = [wloc[n].shape for n in SMALL_NAMES]
    loc_total = sum(int(np.prod(s)) for s in loc_shapes)
    loc_rows = -(-loc_total // (8 * 128)) * 8

    def flat_small(dct):
        v = jnp.concatenate([dct[n].reshape(-1) for n in SMALL_NAMES])
        return jnp.pad(v, (0, loc_rows * 128 - loc_total)).reshape(loc_rows, 128)
    dls, mns, vns = _adamw(flat_small(wloc), flat_small(gsmall), flat_small(mloc), flat_small(vloc), "adamw_small")
    small_out = [dict(zip(SMALL_NAMES, _unflatten(a.reshape(-1), loc_shapes))) for a in (dls, mns, vns)]

    grad_w = {**gbig, **gsmall}
    outs = [loss, grad_x] + [grad_w[n] for n in WEIGHTS]
    for k in range(3):
        merged = {**big_out[k], **small_out[k]}
        outs += [merged[n] for n in WEIGHTS]
    return tuple(outs)
```

```python
import functools
import math

import numpy as np
import jax
import jax.numpy as jnp
from jax import lax
from jax.experimental import pallas as pl
from jax.experimental.pallas import tpu as pltpu

F32 = jnp.float32
BF16 = jnp.bfloat16
HI = lax.Precision.HIGHEST
VMEM_LIMIT_BYTES = 56 * 1024 * 1024
NEG = -1e30

D_MODEL = 1024
SEQ = 4096
DEPTH = 4
N_META = 16
Q_BLOCK = 128
SSD_CHUNK = 128
NORM_EPS = 1e-6
ATTN_HEADS = 16
ATTN_HEAD_DIM = 64
SSD_HEAD_DIM = 64
SSD_GROUPS = 2
SSD_STATE = 128
CONV_K = 4
LRU_BLOCKS = 16
LRU_C = 8.0
D_FF = 2816
ADAM_LR = 0.001
ADAM_B1 = 0.9
ADAM_B2 = 0.999
ADAM_EPS = 1e-08
ADAM_WD = 0.01
ADAM_STEP = 10
SMALL_W = 128
_ROWWISE_TILE_ELEMS = 512 * 1024

WEIGHTS = ['meta_tokens', 'ffn1_norm', 'ffn1_w_gate_up', 'ffn1_w_down', 'mix_norm', 'w_in', 'fox_forget_bias',
           'ssd_conv_w', 'ssd_conv_b', 'ssd_dt_bias', 'ssd_a_log', 'ssd_d', 'ssd_norm', 'lru_conv_w', 'lru_conv_b',
           'lru_w_a', 'lru_b_a', 'lru_w_x', 'lru_b_x', 'lru_lambda', 'w_branch_attn', 'w_branch_ssd', 'w_branch_lru',
           'w_out', 'ffn2_norm', 'ffn2_w_gate_up', 'ffn2_w_down', 'final_norm']
BIG = {'ffn1_w_gate_up': 1, 'ffn1_w_down': 0, 'w_in': 1, 'w_branch_attn': 0, 'w_branch_ssd': 0, 'w_branch_lru': 0,
       'w_out': 0, 'ffn2_w_gate_up': 1, 'ffn2_w_down': 0}
BIG_NAMES = [n for n in WEIGHTS if n in BIG]
COLSHARD_SMALL = ['meta_tokens', 'ssd_conv_w', 'lru_conv_w']
SMALL_NAMES = [n for n in WEIGHTS if n not in BIG]


def _pick(n, cands):
    for c in cands:
        if n % c == 0:
            return c
    raise ValueError(f"no tile for {n} in {cands}")


def _pcall(body, **kw):
    return pl.pallas_call(body, **kw)


def _cparams(sem):
    return pltpu.CompilerParams(dimension_semantics=sem, vmem_limit_bytes=VMEM_LIMIT_BYTES)


def _sds(shape, dtype):
    return jax.ShapeDtypeStruct(tuple(shape), dtype)


def _dot(a, b, hi=False):
    return jnp.dot(a, b, precision=HI if hi else None, preferred_element_type=F32)


def _dot_nt(a, b):
    return lax.dot_general(a, b, (((1,), (1,)), ((), ())), preferred_element_type=F32)


def _dot_tn(a, b):
    return lax.dot_general(a, b, (((0,), (0,)), ((), ())), preferred_element_type=F32)


def _sigmoid(x):
    return 1.0 / (1.0 + jnp.exp(-x))


def _softplus(x):
    return jnp.maximum(x, 0.0) + jnp.log1p(jnp.exp(-jnp.abs(x)))


def _silu(x):
    return x * _sigmoid(x)


def _dsilu(x):
    s = _sigmoid(x)
    return s * (1.0 + x * (1.0 - s))


_GELU_C = math.sqrt(2.0 / math.pi)


def _gelu(x):
    return 0.5 * x * (1.0 + jnp.tanh(_GELU_C * (x + 0.044715 * x * x * x)))


def _dgelu(x):
    t = jnp.tanh(_GELU_C * (x + 0.044715 * x * x * x))
    return 0.5 * (1.0 + t) + 0.5 * x * (1.0 - t * t) * _GELU_C * (1.0 + 3.0 * 0.044715 * x * x)


def _expm1(x):
    series = x * (1.0 + x * 0.5 * (1.0 + x * (1.0 / 3.0) * (1.0 + x * 0.25 * (1.0 + x * 0.2))))
    return jnp.where(jnp.abs(x) < 0.05, series, jnp.exp(x) - 1.0)


def _rowwise(fn, ins, outs, *, bcast=(), reds=(), tm=None, name, period=None):
    t_rows = ins[0].shape[0]
    if tm is None:
        widest = max([a.shape[1] for a in ins] + [c for c, _ in outs])
        tm = _pick(math.gcd(t_rows, period or t_rows),
                   [c for c in (384, 256, 128, 64, 32, 16, 8) if c * widest <= _ROWWISE_TILE_ELEMS or c == 8])
    nt = t_rows // tm
    assert t_rows % tm == 0 and (period is None or period % tm == 0)
    n_in, n_out = len(ins) + len(bcast), len(outs)

    def body(*refs):
        i = pl.program_id(0)
        pos = None
        if period is not None:
            pos = (i * tm) % period + lax.broadcasted_iota(jnp.int32, (tm, 1), 0)
        res = fn(pos, *[r[...] for r in refs[:n_in]])
        res = res if isinstance(res, tuple) else (res,)
        for r, v in zip(refs[n_in:n_in + n_out], res[:n_out]):
            r[...] = v.astype(r.dtype)
        red_refs = refs[n_in + n_out:]
        if red_refs:
            @pl.when(i == 0)
            def _():
                for r in red_refs:
                    r[...] = jnp.zeros_like(r)
            for r, v in zip(red_refs, res[n_out:]):
                r[...] += v

    in_specs = [pl.BlockSpec((tm, a.shape[1]), lambda i: (i, 0)) for a in ins]
    in_specs += [pl.BlockSpec(b.shape, lambda i, n=b.ndim: (0,) * n) for b in bcast]
    out_specs = [pl.BlockSpec((tm, c), lambda i: (i, 0)) for c, _ in outs]
    out_specs += [pl.BlockSpec(s, lambda i: (0, 0)) for s in reds]
    out_shape = [_sds((t_rows, c), dt) for c, dt in outs] + [_sds(s, F32) for s in reds]
    res = _pcall(body, name=name, grid=(nt,), in_specs=in_specs, out_specs=out_specs, out_shape=out_shape,
                 compiler_params=_cparams(("arbitrary",) if reds else ("parallel",)))(*ins, *bcast)
    return res


_TM = (768, 384, 256, 128)
_TN = (1536, 1408, 1024, 768, 512, 640, 384, 256, 128)
_TK = (1536, 1024, 2816, 1408, 512, 384, 256, 128)
_TKO = (1024, 1408, 512, 384, 256, 128)


def _mm_nn(a, b, out_dtype, *, res=None, alpha=1.0, name):
    m, k = a.shape
    k2, n = b.shape
    assert k == k2
    tm, tn, tk = _pick(m, _TM), _pick(n, _TN), _pick(k, _TK)
    nk = k // tk

    def body(*refs):
        if res is None:
            a_ref, b_ref, o_ref, acc = refs
            r_ref = None
        else:
            a_ref, b_ref, r_ref, o_ref, acc = refs
        kk = pl.program_id(2)

        @pl.when(kk == 0)
        def _():
            acc[...] = jnp.zeros_like(acc)

        acc[...] += _dot(a_ref[...].astype(BF16), b_ref[...].astype(BF16))

        @pl.when(kk == nk - 1)
        def _():
            v = acc[...]
            if alpha != 1.0:
                v = v * alpha
            if r_ref is not None:
                v = r_ref[...].astype(F32) + v
            o_ref[...] = v.astype(o_ref.dtype)

    in_specs = [pl.BlockSpec((tm, tk), lambda j, i, kk: (i, kk)), pl.BlockSpec((tk, tn), lambda j, i, kk: (kk, j))]
    args = [a, b]
    if res is not None:
        in_specs.append(pl.BlockSpec((tm, tn), lambda j, i, kk: (i, j)))
        args.append(res)
    return _pcall(body, name=name, grid=(n // tn, m // tm, nk), in_specs=in_specs,
                  out_specs=pl.BlockSpec((tm, tn), lambda j, i, kk: (i, j)), out_shape=_sds((m, n), out_dtype),
                  scratch_shapes=[pltpu.VMEM((tm, tn), F32)],
                  compiler_params=_cparams(("parallel", "parallel", "arbitrary")))(*args)


def _mm_swiglu(a, wg, wu, name):
    m, k = a.shape
    f = wg.shape[1]
    tm, tn, tk = _pick(m, _TM), _pick(f, _TN), _pick(k, _TK)
    nk = k // tk

    def body(a_ref, g_w, u_w, g_ref, u_ref, act_ref, accg, accu):
        kk = pl.program_id(2)

        @pl.when(kk == 0)
        def _():
            accg[...] = jnp.zeros_like(accg)
            accu[...] = jnp.zeros_like(accu)

        av = a_ref[...].astype(BF16)
        accg[...] += _dot(av, g_w[...])
        accu[...] += _dot(av, u_w[...])

        @pl.when(kk == nk - 1)
        def _():
            g, u = accg[...], accu[...]
            g_ref[...] = g.astype(g_ref.dtype)
            u_ref[...] = u.astype(u_ref.dtype)
            act_ref[...] = (_silu(g) * u).astype(act_ref.dtype)

    wspec = pl.BlockSpec((tk, tn), lambda j, i, kk: (kk, j))
    ospec = pl.BlockSpec((tm, tn), lambda j, i, kk: (i, j))
    return _pcall(body, name=name, grid=(f // tn, m // tm, nk),
                  in_specs=[pl.BlockSpec((tm, tk), lambda j, i, kk: (i, kk)), wspec, wspec],
                  out_specs=[ospec] * 3, out_shape=[_sds((m, f), BF16)] * 3,
                  scratch_shapes=[pltpu.VMEM((tm, tn), F32)] * 2,
                  compiler_params=_cparams(("parallel", "parallel", "arbitrary")))(a, wg, wu)


def _mm_dswiglu(dout, wd_t, g, u, alpha, name):
    m, k = dout.shape
    f = wd_t.shape[1]
    tm, tn, tk = _pick(m, _TM), _pick(f, _TN), _pick(k, _TK)
    nk = k // tk

    def body(a_ref, w_ref, g_ref, u_ref, dg_ref, du_ref, acc):
        kk = pl.program_id(2)

        @pl.when(kk == 0)
        def _():
            acc[...] = jnp.zeros_like(acc)

        acc[...] += _dot(a_ref[...].astype(BF16), w_ref[...])

        @pl.when(kk == nk - 1)
        def _():
            dact = acc[...] * alpha
            gv, uv = g_ref[...].astype(F32), u_ref[...].astype(F32)
            dg_ref[...] = (dact * uv * _dsilu(gv)).astype(dg_ref.dtype)
            du_ref[...] = (dact * _silu(gv)).astype(du_ref.dtype)

    ospec = pl.BlockSpec((tm, tn), lambda j, i, kk: (i, j))
    return _pcall(body, name=name, grid=(f // tn, m // tm, nk),
                  in_specs=[pl.BlockSpec((tm, tk), lambda j, i, kk: (i, kk)),
                            pl.BlockSpec((tk, tn), lambda j, i, kk: (kk, j)), ospec, ospec],
                  out_specs=[ospec] * 2, out_shape=[_sds((m, f), BF16)] * 2,
                  scratch_shapes=[pltpu.VMEM((tm, tn), F32)],
                  compiler_params=_cparams(("parallel", "parallel", "arbitrary")))(dout, wd_t, g, u)


def _mm_tn(a, b, *, alpha=1.0, name):
    m, k = a.shape
    m2, n = b.shape
    assert m == m2
    tm, tn, tko = _pick(m, _TM), _pick(n, _TN), _pick(k, _TKO)
    nm = m // tm

    def body(a_ref, b_ref, o_ref, acc):
        mm = pl.program_id(2)

        @pl.when(mm == 0)
        def _():
            acc[...] = jnp.zeros_like(acc)

        acc[...] += _dot_tn(a_ref[...].astype(BF16), b_ref[...].astype(BF16))

        @pl.when(mm == nm - 1)
        def _():
            v = acc[...]
            o_ref[...] = v * alpha if alpha != 1.0 else v

    return _pcall(body, name=name, grid=(k // tko, n // tn, nm),
                  in_specs=[pl.BlockSpec((tm, tko), lambda i, j, mm: (mm, i)),
                            pl.BlockSpec((tm, tn), lambda i, j, mm: (mm, j))],
                  out_specs=pl.BlockSpec((tko, tn), lambda i, j, mm: (i, j)), out_shape=_sds((k, n), F32),
                  scratch_shapes=[pltpu.VMEM((tko, tn), F32)],
                  compiler_params=_cparams(("parallel", "parallel", "arbitrary")))(a, b)


def _rms_fwd(h, g, name):
    def fn(_, hv, gv):
        r = lax.rsqrt(jnp.mean(hv * hv, axis=1, keepdims=True) + NORM_EPS)
        return hv * r * gv
    return _rowwise(fn, [h], [(h.shape[1], BF16)], bcast=[g], name=name)[0]


def _rms_bwd(h, dxn, dres, g, name):
    d = h.shape[1]

    def fn(_, hv, dv, rv, gv):
        r = lax.rsqrt(jnp.mean(hv * hv, axis=1, keepdims=True) + NORM_EPS)
        xh = hv * r
        dxh = dv * gv
        dh = r * (dxh - xh * jnp.mean(dxh * xh, axis=1, keepdims=True))
        return rv + dh, jnp.sum(dv * xh, axis=0, keepdims=True)
    return _rowwise(fn, [h, dxn, dres], [(d, F32)], bcast=[g], reds=[(1, d)], name=name)


def _merge_fwd(mg, ba, bb, bc, name):
    d = ba.shape[1]

    def fn(_, m, a, b, c):
        g = _sigmoid(m.astype(F32))
        return g[:, :d] * a.astype(F32) + g[:, d:2 * d] * b.astype(F32) + g[:, 2 * d:] * c.astype(F32)
    return _rowwise(fn, [mg, ba, bb, bc], [(d, BF16)], name=name)[0]


def _merge_bwd(mg, ba, bb, bc, dmix, name):
    d = ba.shape[1]

    def fn(_, m, a, b, c, dm):
        g = _sigmoid(m.astype(F32))
        dm = dm.astype(F32)
        br = (a.astype(F32), b.astype(F32), c.astype(F32))
        douts, dgs = [], []
        for j in range(3):
            gj = g[:, j * d:(j + 1) * d]
            douts.append(dm * gj)
            dgs.append(dm * br[j] * gj * (1.0 - gj))
        return (*douts, jnp.concatenate(dgs, axis=1))
    return _rowwise(fn, [mg, ba, bb, bc, dmix], [(d, BF16)] * 3 + [(3 * d, BF16)], name=name)


def _loss_head(h, tgt, g, seq_len, name):
    d = h.shape[1]

    def fn(pos, hv, tv, gv):
        r = lax.rsqrt(jnp.mean(hv * hv, axis=1, keepdims=True) + NORM_EPS)
        xh = hv * r
        real = (pos >= N_META) & (pos < N_META + SEQ)
        e = jnp.where(real, xh * gv - tv, 0.0)
        part = jnp.sum(jnp.sum(e * e, axis=1, keepdims=True), axis=0, keepdims=True) * (0.5 / d)
        dy = e * (1.0 / d)
        dxh = dy * gv
        dh = r * (dxh - xh * jnp.mean(dxh * xh, axis=1, keepdims=True))
        return dh, jnp.broadcast_to(part, (1, 128)), jnp.sum(dy * xh, axis=0, keepdims=True)
    return _rowwise(fn, [h, tgt], [(d, F32)], bcast=[g], reds=[(1, 128), (1, d)], name=name, period=seq_len)


def _adamw(w, g, m, v, name):
    c1 = 1.0 - ADAM_B1 ** ADAM_STEP
    c2 = 1.0 - ADAM_B2 ** ADAM_STEP
    wd = w.shape[1]

    def fn(_, wv, gv, mv, vv):
        mn = ADAM_B1 * mv + (1.0 - ADAM_B1) * gv
        vn = ADAM_B2 * vv + (1.0 - ADAM_B2) * (gv * gv)
        delta = -ADAM_LR * ((mn / c1) / (jnp.sqrt(vn / c2) + ADAM_EPS) + ADAM_WD * wv)
        return delta, mn, vn
    return _rowwise(fn, [w, g, m, v], [(wd, F32)] * 3, name=name)


def _sum_rows(parts, out_dtype, name):
    def fn(_, *vs):
        acc = vs[0].astype(F32)
        for v in vs[1:]:
            acc = acc + v.astype(F32)
        return acc
    return _rowwise(fn, list(parts), [(parts[0].shape[1], out_dtype)], name=name)[0]


def _cumsum_seq(x, reverse, name):
    b, l, w = x.shape
    q = 128
    nc = l // q

    def body(x_ref, o_ref):
        row = lax.broadcasted_iota(jnp.int32, (q, q), 0)
        col = lax.broadcasted_iota(jnp.int32, (q, q), 1)
        tri = ((row <= col) if reverse else (row >= col)).astype(F32)
        rsel = lax.broadcasted_iota(jnp.int32, (q, w), 0) == (0 if reverse else q - 1)

        def step(i, carry):
            j = (nc - 1 - i) if reverse else i
            start = pl.multiple_of(j * q, q)
            cs = _dot(tri, x_ref[pl.ds(start, q), :], hi=True) + carry
            o_ref[pl.ds(start, q), :] = cs
            return jnp.sum(jnp.where(rsel, cs, 0.0), axis=0, keepdims=True)

        lax.fori_loop(0, nc, step, jnp.zeros((1, w), F32))

    return _pcall(body, name=name, grid=(b,), in_specs=[pl.BlockSpec((None, l, w), lambda i: (i, 0, 0))],
                  out_specs=pl.BlockSpec((None, l, w), lambda i: (i, 0, 0)), out_shape=_sds(x.shape, F32),
                  compiler_params=_cparams(("parallel",)))(x)


_HALO = 16


def _conv_tiles(l, c):
    return _pick(l, (384, 256, 128)), _pick(c, (512, 256, 128))


def _conv_fwd(x, w, bias, out_dtype, name, with_silu=False):
    b, l, c = x.shape
    tt, cw = _conv_tiles(l, c)

    def body(x_ref, h_ref, w_ref, b_ref, *o_refs):
        t = pl.program_id(2)
        halo = jnp.where(t == 0, 0.0, h_ref[...].astype(F32))
        xe = jnp.concatenate([halo, x_ref[...].astype(F32)], axis=0)
        wv = w_ref[...]
        acc = b_ref[...] + wv[CONV_K - 1:CONV_K, :] * xe[_HALO:]
        for j in range(CONV_K - 1):
            acc = acc + wv[j:j + 1, :] * pltpu.roll(xe, CONV_K - 1 - j, 0)[_HALO:]
        o_refs[0][...] = acc.astype(o_refs[0].dtype)
        if with_silu:
            o_refs[1][...] = _silu(acc.astype(o_refs[0].dtype).astype(F32)).astype(o_refs[1].dtype)

    ospec = pl.BlockSpec((None, tt, cw), lambda i, j, t: (i, t, j))
    n_out = 2 if with_silu else 1
    res = _pcall(body, name=name, grid=(b, c // cw, l // tt),
                 in_specs=[pl.BlockSpec((None, tt, cw), lambda i, j, t: (i, t, j)),
                           pl.BlockSpec((None, _HALO, cw), lambda i, j, t: (i, jnp.maximum(t * (tt // _HALO) - 1, 0), j)),
                           pl.BlockSpec((CONV_K, cw), lambda i, j, t: (0, j)),
                           pl.BlockSpec((1, cw), lambda i, j, t: (0, j))],
                 out_specs=[ospec] * n_out, out_shape=[_sds(x.shape, out_dtype)] * n_out,
                 compiler_params=_cparams(("parallel", "parallel", "parallel")))(x, x, w, bias)
    return res if with_silu else res[0]


def _conv_bwd(x, dy, w, name):
    b, l, c = x.shape
    tt, cw = _conv_tiles(l, c)
    nt = l // tt

    def body(x_ref, xh_ref, d_ref, dh_ref, w_ref, dx_ref, dw_ref):
        i, t = pl.program_id(1), pl.program_id(2)
        halo = jnp.where(t == 0, 0.0, xh_ref[...].astype(F32))
        xe = jnp.concatenate([halo, x_ref[...].astype(F32)], axis=0)
        dv = d_ref[...].astype(F32)
        nxt = jnp.where(t == nt - 1, 0.0, dh_ref[...].astype(F32))
        de = jnp.concatenate([dv, nxt], axis=0)
        wv = w_ref[...]
        dx = wv[CONV_K - 1:CONV_K, :] * dv
        rowid = lax.broadcasted_iota(jnp.int32, (8, 1), 0)
        part = jnp.where(rowid == CONV_K, jnp.sum(dv, axis=0, keepdims=True), 0.0)
        part = part + jnp.where(rowid == CONV_K - 1, jnp.sum(dv * xe[_HALO:], axis=0, keepdims=True), 0.0)
        for j in range(CONV_K - 1):
            s = CONV_K - 1 - j
            dx = dx + wv[j:j + 1, :] * pltpu.roll(de, tt + _HALO - s, 0)[:tt]
            xs = pltpu.roll(xe, s, 0)[_HALO:]
            part = part + jnp.where(rowid == j, jnp.sum(dv * xs, axis=0, keepdims=True), 0.0)
        dx_ref[...] = dx.astype(dx_ref.dtype)

        @pl.when((i == 0) & (t == 0))
        def _():
            dw_ref[...] = jnp.zeros_like(dw_ref)
        dw_ref[...] += part

    return _pcall(body, name=name, grid=(c // cw, b, nt),
                  in_specs=[pl.BlockSpec((None, tt, cw), lambda j, i, t: (i, t, j)),
                            pl.BlockSpec((None, _HALO, cw), lambda j, i, t: (i, jnp.maximum(t * (tt // _HALO) - 1, 0), j)),
                            pl.BlockSpec((None, tt, cw), lambda j, i, t: (i, t, j)),
                            pl.BlockSpec((None, _HALO, cw),
                                         lambda j, i, t: (i, jnp.minimum((t + 1) * (tt // _HALO), l // _HALO - 1), j)),
                            pl.BlockSpec((CONV_K, cw), lambda j, i, t: (0, j))],
                  out_specs=[pl.BlockSpec((None, tt, cw), lambda j, i, t: (i, t, j)),
                             pl.BlockSpec((8, cw), lambda j, i, t: (0, j))],
                  out_shape=[_sds(x.shape, BF16), _sds((8, c), F32)],
                  compiler_params=_cparams(("parallel", "arbitrary", "arbitrary")))(x, x, dy, dy, w)


_SUBLANES = 8


def _linear_scan(a, u, reverse, name, gate=None):
    b, l, c = a.shape
    tt = 128
    cw = _pick(c, (512, 256, 128))
    nt = l // tt
    groups = tt // _SUBLANES

    def body(*refs):
        if gate is None:
            a_ref, u_ref, h_ref, carry = refs
        else:
            a_ref, u_ref, g_ref, h_ref, y_ref, carry = refs
        t = pl.program_id(2)

        @pl.when(t == 0)
        def _():
            carry[...] = jnp.zeros_like(carry)

        av, uv = a_ref[...], u_ref[...]
        sub = jnp.bitwise_and(lax.broadcasted_iota(jnp.int32, (tt, cw), 0), _SUBLANES - 1)
        k = 1
        while k < _SUBLANES:
            keep = (sub < _SUBLANES - k) if reverse else (sub >= k)
            shift = tt - k if reverse else k
            a_sh = jnp.where(keep, pltpu.roll(av, shift, 0), 1.0)
            u_sh = jnp.where(keep, pltpu.roll(uv, shift, 0), 0.0)
            uv = uv + av * u_sh
            av = av * a_sh
            k *= 2
        edge = carry[0:1, :]
        for g in (range(groups - 1, -1, -1) if reverse else range(groups)):
            rows = slice(g * _SUBLANES, (g + 1) * _SUBLANES)
            hg = uv[rows] + av[rows] * edge
            h_ref[rows, :] = hg
            edge = hg[0:1] if reverse else hg[_SUBLANES - 1:_SUBLANES]
        carry[...] = jnp.broadcast_to(edge, carry.shape)
        if gate is not None:
            y_ref[...] = (h_ref[...] * _gelu(g_ref[...].astype(F32))).astype(y_ref.dtype)

    tmap = (lambda i, j, t: (i, nt - 1 - t, j)) if reverse else (lambda i, j, t: (i, t, j))
    spec = pl.BlockSpec((None, tt, cw), tmap)
    ins, outs, shapes = [a, u], [spec], [_sds(a.shape, F32)]
    if gate is not None:
        ins, outs, shapes = [a, u, gate], [spec, spec], [_sds(a.shape, F32), _sds(a.shape, BF16)]
    res = _pcall(body, name=name, grid=(b, c // cw, nt), in_specs=[spec] * len(ins), out_specs=outs,
                 out_shape=shapes, scratch_shapes=[pltpu.VMEM((8, cw), F32)],
                 compiler_params=_cparams(("parallel", "parallel", "arbitrary")))(*ins)
    return res if gate is not None else res[0]


ATTN_W = 128
_AUG_C = 0
_AUG_ONE = 3
_AUG_LSE = 6


def _attn_blk(l):
    return _pick(l, (384, 256, 128))


def _prep_rows(l):
    return _pick(l, (1408, 384, 256, 128))


def _split3(x):
    x1 = x.astype(BF16).astype(F32)
    x2 = (x - x1).astype(BF16).astype(F32)
    x3 = (x - x1 - x2).astype(BF16).astype(F32)
    return x1, x2, x3


def _aug_lanes(lane, base, cols, ones_at=()):
    out = jnp.zeros(lane.shape, F32)
    for o in ones_at:
        out = out + ((lane >= o) & (lane < o + 3)).astype(F32)
    for k, v in enumerate(cols):
        out = jnp.where(lane == base + k, v, out)
    return out


def _pair_specs(blk, nb):
    at = lambda ww: pl.BlockSpec((None, 2, blk, ww), lambda bi, p, i: (bi, p, i, 0))
    whole = pl.BlockSpec((None, 2, nb, blk, ATTN_W), lambda bi, p, i: (bi, p, 0, 0, 0))
    rows = pl.BlockSpec((None, blk, ATTN_W), lambda bi, p, i: (bi, i, p))
    return at, whole, rows


def _attn_prep(pm3, cum, col0, name):
    b, l, _ = pm3.shape
    dh, nh = ATTN_HEAD_DIM, ATTN_HEADS
    blk = _prep_rows(l)
    scale = dh ** -0.5

    def body(q_ref, k_ref, v_ref, c_ref, qa_ref, ka_ref, va_ref):
        pair = pl.program_id(1)
        lane = lax.broadcasted_iota(jnp.int32, (1, ATTN_W), 1)
        head = lane < dh
        cv = c_ref[...]
        qf, kf, vf = (r[...].astype(F32) for r in (q_ref, k_ref, v_ref))
        for e in range(2):
            c1, c2, c3 = _split3(jnp.sum(jnp.where(lane == 2 * pair + e, cv, 0.0), axis=1, keepdims=True))
            qe, ke, ve = (pltpu.roll(t, dh, 1) for t in (qf, kf, vf)) if e else (qf, kf, vf)
            qa_ref[e] = jnp.where(head, qe * scale,
                                  _aug_lanes(lane, dh + _AUG_C, (c1, c2, c3), (dh + _AUG_ONE,))).astype(BF16)
            ka_ref[e] = jnp.where(head, ke, _aug_lanes(lane, dh + _AUG_ONE, (-c1, -c2, -c3),
                                                       (dh + _AUG_C, dh + _AUG_LSE))).astype(BF16)
            va_ref[e] = jnp.where(head, ve, _aug_lanes(lane, dh, (), (dh,))).astype(BF16)

    at, _, _ = _pair_specs(blk, l // blk)
    cols = lambda c0: pl.BlockSpec((None, blk, ATTN_W), lambda bi, p, i, c0=c0: (bi, i, c0 // ATTN_W + p))
    return _pcall(body, name=name, grid=(b, nh // 2, l // blk),
                  in_specs=[cols(col0[0]), cols(col0[1]), cols(col0[2]),
                            pl.BlockSpec((None, blk, SMALL_W), lambda bi, p, i: (bi, i, 0))],
                  out_specs=[at(ATTN_W)] * 3, out_shape=[_sds((b, nh, l, ATTN_W), BF16)] * 3,
                  compiler_params=_cparams(("parallel", "parallel", "parallel")))(pm3, pm3, pm3, cum)


def _attn_prep_bwd(dy3, y3, qa, lse, name):
    b, nh, l, _ = qa.shape
    dh = ATTN_HEAD_DIM
    blk = _prep_rows(l)

    def body(dy_ref, y_ref, qa_ref, lse_ref, qa2_ref, doa_ref):
        lane = lax.broadcasted_iota(jnp.int32, (1, ATTN_W), 1)
        dyf = dy_ref[...].astype(F32)
        prod = dyf * y_ref[...].astype(F32)
        for e in range(2):
            mine = (lane >= dh) if e else (lane < dh)
            d1, d2, d3 = _split3(jnp.sum(jnp.where(mine, prod, 0.0), axis=1, keepdims=True))
            l1, l2, l3 = _split3(lse_ref[e])
            dye = pltpu.roll(dyf, dh, 1) if e else dyf
            doa_ref[e] = jnp.where(lane < dh, dye, _aug_lanes(lane, dh, (-d1, -d2, -d3))).astype(BF16)
            qa2 = qa_ref[e].astype(F32)
            for k, lv in enumerate((l1, l2, l3)):
                qa2 = jnp.where(lane == dh + _AUG_LSE + k, -lv, qa2)
            qa2_ref[e] = qa2.astype(BF16)

    at, _, rows = _pair_specs(blk, l // blk)
    return _pcall(body, name=name, grid=(b, nh // 2, l // blk), in_specs=[rows, rows, at(ATTN_W), at(1)],
                  out_specs=[at(ATTN_W)] * 2, out_shape=[_sds(qa.shape, BF16)] * 2,
                  compiler_params=_cparams(("parallel", "parallel", "parallel")))(dy3, y3, qa, lse)


def _flash_fwd(qa, ka, va, d_model, name):
    b, h, l, w = qa.shape
    dh = ATTN_HEAD_DIM
    blk = _attn_blk(l)
    nb = l // blk
    kr, vr = ka.reshape(b, h, nb, blk, w), va.reshape(b, h, nb, blk, w)

    def body(q_ref, k_ref, v_ref, o_ref, lse_ref):
        i = pl.program_id(2)
        row = lax.broadcasted_iota(jnp.int32, (blk, blk), 0)
        col = lax.broadcasted_iota(jnp.int32, (blk, blk), 1)

        def scores(e, j):
            return _dot_nt(q_ref[e], k_ref[e, j])

        def consume(e, j, s, m, acc):
            mn = jnp.maximum(m, jnp.max(s, axis=1, keepdims=True))
            return mn, jnp.exp(m - mn) * acc + _dot(jnp.exp(s - mn).astype(BF16), v_ref[e, j])

        def step(j, carry):
            out = []
            for e in range(2):
                m, acc, s = carry[3 * e:3 * e + 3]
                s_next = scores(e, j + 1)
                out += [*consume(e, j, s, m, acc), s_next]
            return tuple(out)

        init = tuple(t for e in range(2)
                     for t in (jnp.full((blk, 1), NEG, F32), jnp.zeros((blk, w), F32), scores(e, 0)))
        carry = lax.fori_loop(0, i, step, init)
        m0, a0 = consume(0, i, jnp.where(col <= row, carry[2], NEG), carry[0], carry[1])
        m1, a1 = consume(1, i, jnp.where(col <= row, carry[5], NEG), carry[3], carry[4])
        l0, l1 = a0[:, dh:dh + 1], a1[:, dh:dh + 1]
        lane = lax.broadcasted_iota(jnp.int32, (1, w), 1)
        o_ref[...] = jnp.where(lane < dh, a0 / l0, pltpu.roll(a1 / l1, dh, 1)).astype(o_ref.dtype)
        lse_ref[0] = m0 + jnp.log(l0)
        lse_ref[1] = m1 + jnp.log(l1)

    at, whole, rows = _pair_specs(blk, nb)
    return _pcall(body, name=name, grid=(b, h // 2, nb), in_specs=[at(w), whole, whole],
                  out_specs=[rows, at(1)], out_shape=[_sds((b, l, d_model), BF16), _sds((b, h, l, 1), F32)],
                  compiler_params=_cparams(("parallel", "parallel", "parallel")))(qa, kr, vr)


def _flash_bwd(qa, ka, va, doa, d_model, name):
    b, h, l, w = qa.shape
    dh = ATTN_HEAD_DIM
    blk = _attn_blk(l)
    nb = l // blk
    scale = dh ** -0.5
    r5 = lambda t: t.reshape(b, h, nb, blk, w)

    def body(k_ref, v_ref, q_ref, do_ref, dq_ref, dk_ref, dv_ref, dcq_ref, dck_ref, dq_acc):
        j = pl.program_id(2)
        row = lax.broadcasted_iota(jnp.int32, (blk, blk), 0)
        col = lax.broadcasted_iota(jnp.int32, (blk, blk), 1)
        lane = lax.broadcasted_iota(jnp.int32, (1, w), 1)

        @pl.when(j == 0)
        def _():
            dq_acc[...] = jnp.zeros_like(dq_acc)

        def contrib(i, masked, carry):
            out = []
            for e in range(2):
                qv, dov = q_ref[e, i], do_ref[e, i]
                p = jnp.exp(_dot_nt(qv, k_ref[e]))
                if masked:
                    p = jnp.where(col <= row, p, 0.0)
                ds = (p * _dot_nt(dov, v_ref[e])).astype(BF16)
                dq_acc[e, i] += _dot(ds, k_ref[e])
                out += [carry[2 * e] + _dot_tn(ds, qv), carry[2 * e + 1] + _dot_tn(p.astype(BF16), dov)]
            return tuple(out)

        zero = (jnp.zeros((blk, w), F32),) * 4
        dk0, dv0, dk1, dv1 = lax.fori_loop(j + 1, nb, lambda i, c: contrib(i, False, c), contrib(j, True, zero))
        dk_ref[...] = jnp.where(lane < dh, dk0, pltpu.roll(dk1, dh, 1)).astype(dk_ref.dtype)
        dv_ref[...] = jnp.where(lane < dh, dv0, pltpu.roll(dv1, dh, 1)).astype(dv_ref.dtype)
        for e, dk in enumerate((dk0, dk1)):
            dck_ref[e] = jnp.sum(jnp.where(lane == dh + _AUG_ONE, dk, 0.0), axis=1, keepdims=True)

        @pl.when(j == nb - 1)
        def _():
            for ib in range(nb):
                rs = pl.ds(ib * blk, blk)
                dq0, dq1 = dq_acc[0, ib], dq_acc[1, ib]
                dq_ref[rs, :] = (jnp.where(lane < dh, dq0, pltpu.roll(dq1, dh, 1)) * scale).astype(dq_ref.dtype)
                for e, dq in enumerate((dq0, dq1)):
                    dcq_ref[e, rs, :] = jnp.sum(jnp.where(lane == dh + _AUG_C, dq, 0.0), axis=1, keepdims=True)

    at, whole, rows = _pair_specs(blk, nb)
    seq_rows = pl.BlockSpec((None, l, ATTN_W), lambda bi, p, j: (bi, 0, p))
    seq_col = pl.BlockSpec((None, 2, l, 1), lambda bi, p, j: (bi, p, 0, 0))
    act = _sds((b, l, d_model), BF16)
    col1 = _sds((b, h, l, 1), F32)
    return _pcall(body, name=name, grid=(b, h // 2, nb), in_specs=[at(w), at(w), whole, whole],
                  out_specs=[seq_rows, rows, rows, seq_col, at(1)], out_shape=[act, act, act, col1, col1],
                  scratch_shapes=[pltpu.VMEM((2, nb, blk, w), F32)],
                  compiler_params=_cparams(("parallel", "parallel", "arbitrary")))(ka, va, r5(qa), r5(doa))


def _ssd_dims(d_ssd):
    heads = d_ssd // SSD_HEAD_DIM
    return heads, heads // SSD_GROUPS, d_ssd // SSD_GROUPS


def _ssd_specs(l, ds, seq_map):
    q = SSD_CHUNK
    gn = SSD_GROUPS * SSD_STATE
    row3 = lambda w, cb: pl.BlockSpec((None, q, w), lambda i, c, cb=cb: (i, seq_map(c), cb))
    return dict(
        xs=row3(ds, 0), bm=row3(gn, ds // gn), cm=row3(gn, ds // gn + 1), z=row3(ds, 0), dt=row3(SMALL_W, 0),
        da=row3(SMALL_W, 0), dat=pl.BlockSpec((None, SMALL_W, q), lambda i, c: (i, 0, seq_map(c))),
        e=pl.BlockSpec((SMALL_W, ds), lambda i, c: (0, 0)), et=pl.BlockSpec((ds, SMALL_W), lambda i, c: (0, 0)),
        vec=pl.BlockSpec((1, ds), lambda i, c: (0, 0)), vec128=pl.BlockSpec((1, SMALL_W), lambda i, c: (0, 0)),
        hin=pl.BlockSpec((None, None, SSD_STATE, ds), lambda i, c: (i, seq_map(c), 0, 0)))


def _ssd_common(da, dat, dt, e_mat, xs):
    q = SSD_CHUNK
    row = lax.broadcasted_iota(jnp.int32, (q, q), 0)
    col = lax.broadcasted_iota(jnp.int32, (q, q), 1)
    lower = row >= col
    cs = _dot(lower.astype(F32), da, hi=True)
    cst = _dot(dat, (row <= col).astype(F32), hi=True)
    dtx = _dot(dt, e_mat, hi=True)
    csx = _dot(cs, e_mat, hi=True)
    rowx = lax.broadcasted_iota(jnp.int32, csx.shape, 0)
    totx = jnp.sum(jnp.where(rowx == q - 1, csx, 0.0), axis=0, keepdims=True)
    xf = xs.astype(F32)
    return lower, cs, cst, dtx, csx, totx, xf, xf * dtx


def _ssd_fwd(xbc, z, dt, da, dat, e_mat, dx, nw, name):
    b, l, _ = xbc.shape
    ds = z.shape[2]
    heads, hpg, gw = _ssd_dims(ds)
    q, n = SSD_CHUNK, SSD_STATE
    nc = l // q
    hcol0 = ATTN_HEADS

    def body(xs_ref, bm_ref, cm_ref, z_ref, dt_ref, da_ref, dat_ref, e_ref, dx_ref, nw_ref, y_ref, yraw_ref, hin_ref,
             hst, ydiag):
        c = pl.program_id(1)

        @pl.when(c == 0)
        def _():
            hst[...] = jnp.zeros_like(hst)

        hin = hst[...]
        hin_ref[...] = hin
        lower, cs, cst, dtx, csx, totx, xf, xdt = _ssd_common(da_ref[...], dat_ref[...], dt_ref[...], e_ref[...],
                                                               xs_ref[...])
        bm, cm = bm_ref[...], cm_ref[...]
        dec_end = jnp.exp(totx - csx)
        for g in range(SSD_GROUPS):
            gs = slice(g * gw, (g + 1) * gw)
            bg, cg = bm[:, g * n:(g + 1) * n], cm[:, g * n:(g + 1) * n]
            cb = _dot_nt(cg, bg)
            for e in range(hpg):
                hh = g * hpg + e
                cc = hcol0 + hh
                lm = jnp.exp(jnp.where(lower, cs[:, cc:cc + 1] - cst[cc:cc + 1, :], NEG))
                hs = slice(hh * SSD_HEAD_DIM, (hh + 1) * SSD_HEAD_DIM)
                ydiag[:, hs] = _dot((cb * lm).astype(BF16), xdt[:, hs].astype(BF16))
            sg = _dot_tn(bg, (xdt[:, gs] * dec_end[:, gs]).astype(BF16))
            hst[:, gs] = jnp.exp(totx[:, gs]) * hin[:, gs] + sg
            ydiag[:, gs] += _dot(cg, hin[:, gs].astype(BF16)) * jnp.exp(csx[:, gs])
        yraw = ydiag[...] + dx_ref[...] * xf
        yraw_ref[...] = yraw.astype(yraw_ref.dtype)
        yg = yraw * _silu(z_ref[...].astype(F32))
        nwv = nw_ref[...]
        for g in range(SSD_GROUPS):
            gs = slice(g * gw, (g + 1) * gw)
            r = lax.rsqrt(jnp.mean(yg[:, gs] * yg[:, gs], axis=1, keepdims=True) + NORM_EPS)
            y_ref[:, gs] = (yg[:, gs] * r * nwv[:, gs]).astype(y_ref.dtype)

    sp = _ssd_specs(l, ds, lambda c: c)
    return _pcall(body, name=name, grid=(b, nc),
                  in_specs=[sp['xs'], sp['bm'], sp['cm'], sp['z'], sp['dt'], sp['da'], sp['dat'], sp['e'], sp['vec'],
                            sp['vec']],
                  out_specs=[sp['z'], sp['z'], sp['hin']],
                  out_shape=[_sds((b, l, ds), BF16), _sds((b, l, ds), BF16), _sds((b, nc, n, ds), F32)],
                  scratch_shapes=[pltpu.VMEM((n, ds), F32), pltpu.VMEM((q, ds), F32)],
                  compiler_params=_cparams(("parallel", "arbitrary")))(xbc, xbc, xbc, z, dt, da, dat, e_mat, dx, nw)


def _ssd_bwd(xbc, z, dt, da, dat, e_mat, et_mat, dx, nw, a128, yraw, hin, dy, name):
    b, l, dxw = xbc.shape
    ds = z.shape[2]
    heads, hpg, gw = _ssd_dims(ds)
    q, n = SSD_CHUNK, SSD_STATE
    gn = SSD_GROUPS * n
    nc = l // q
    hcol0 = ATTN_HEADS

    def body(xs_ref, bm_ref, cm_ref, z_ref, dt_ref, da_ref, dat_ref, e_ref, et_ref, dx_ref, nw_ref, a_ref, yraw_ref,
             hin_ref, dy_ref, dxs_ref, dbm_ref, dcm_ref, dz_ref, ddt_ref, dd_ref, dnw_ref, dap_ref, dhs, dxdt, dcsx,
             dtotx):
        i, c = pl.program_id(0), pl.program_id(1)

        @pl.when(c == 0)
        def _():
            dhs[...] = jnp.zeros_like(dhs)

        @pl.when((i == 0) & (c == 0))
        def _():
            dd_ref[...] = jnp.zeros_like(dd_ref)
            dnw_ref[...] = jnp.zeros_like(dnw_ref)
            dap_ref[...] = jnp.zeros_like(dap_ref)

        dtv = dt_ref[...]
        lower, cs, cst, dtx, csx, totx, xf, xdt = _ssd_common(da_ref[...], dat_ref[...], dtv, e_ref[...], xs_ref[...])
        upper = jnp.logical_not(lower) | (lax.broadcasted_iota(jnp.int32, (q, q), 0)
                                          == lax.broadcasted_iota(jnp.int32, (q, q), 1))
        bm, cm = bm_ref[...], cm_ref[...]
        ecs, dec_end, etot = jnp.exp(csx), jnp.exp(totx - csx), jnp.exp(totx)
        yraw = yraw_ref[...].astype(F32)
        zv = z_ref[...].astype(F32)
        sz = _silu(zv)
        yg = yraw * sz
        dyn_ = dy_ref[...].astype(F32)
        nwv = nw_ref[...]
        dygs, dnws = [], []
        for g in range(SSD_GROUPS):
            gs = slice(g * gw, (g + 1) * gw)
            r = lax.rsqrt(jnp.mean(yg[:, gs] * yg[:, gs], axis=1, keepdims=True) + NORM_EPS)
            yn = yg[:, gs] * r
            dn = dyn_[:, gs] * nwv[:, gs]
            dnws.append(jnp.sum(dyn_[:, gs] * yn, axis=0, keepdims=True))
            dygs.append(r * (dn - yn * jnp.mean(dn * yn, axis=1, keepdims=True)))
        dyg = jnp.concatenate(dygs, axis=1)
        dnw_ref[...] += jnp.concatenate(dnws, axis=1)
        dz_ref[...] = (dyg * yraw * _dsilu(zv)).astype(dz_ref.dtype)
        dyv = dyg * sz
        dd_ref[...] += jnp.sum(dyv * xf, axis=0, keepdims=True)
        hin, dh = hin_ref[...], dhs[...]
        lane128 = lax.broadcasted_iota(jnp.int32, (1, SMALL_W), 1)
        dcs = jnp.zeros((q, SMALL_W), F32)
        for g in range(SSD_GROUPS):
            gs = slice(g * gw, (g + 1) * gw)
            bg, cg = bm[:, g * n:(g + 1) * n], cm[:, g * n:(g + 1) * n]
            hg, dhg = hin[:, gs], dh[:, gs]
            hgb, dsb = hg.astype(BF16), dhg.astype(BF16)
            yoff = _dot(cg, hgb) * ecs[:, gs]
            dch = (dyv[:, gs] * ecs[:, gs]).astype(BF16)
            dcg = _dot_nt(dch, hgb)
            dhs[:, gs] = _dot_tn(cg, dch) + etot[:, gs] * dhg
            zg = xdt[:, gs] * dec_end[:, gs]
            dzz = _dot(bg, dsb)
            dbg = _dot_nt(zg.astype(BF16), dsb)
            dxdt_g = dzz * dec_end[:, gs]
            w_end = dzz * zg
            dtotx[:, gs] = jnp.sum(dhg * hg, axis=0, keepdims=True) * etot[:, gs] + jnp.sum(w_end, axis=0, keepdims=True)
            dcsx[:, gs] = dyv[:, gs] * yoff - w_end
            cb, cbt = _dot_nt(cg, bg), _dot_nt(bg, cg)
            dgm = jnp.zeros((q, q), F32)
            for e in range(hpg):
                hh = g * hpg + e
                cc = hcol0 + hh
                ccol, crow = cs[:, cc:cc + 1], cst[cc:cc + 1, :]
                lm = jnp.exp(jnp.where(lower, ccol - crow, NEG))
                lmt = jnp.exp(jnp.where(upper, crow - ccol, NEG))
                mm, mt = cb * lm, cbt * lmt
                hs = slice(hh * SSD_HEAD_DIM, (hh + 1) * SSD_HEAD_DIM)
                dye, xe = dyv[:, hs].astype(BF16), xdt[:, hs].astype(BF16)
                dm, dmt = _dot_nt(dye, xe), _dot_nt(xe, dye)
                dxdt[:, hs] = dxdt_g[:, e * SSD_HEAD_DIM:(e + 1) * SSD_HEAD_DIM] + _dot(mt.astype(BF16), dye)
                dgm = dgm + dm * lm
                rs = jnp.sum(dm * mm, axis=1, keepdims=True) - jnp.sum(dmt * mt, axis=1, keepdims=True)
                dcs = dcs + rs * (lane128 == cc).astype(F32)
            dgb = dgm.astype(BF16)
            dcm_ref[:, g * n:(g + 1) * n] = (dcg + _dot(dgb, bg)).astype(dcm_ref.dtype)
            dbm_ref[:, g * n:(g + 1) * n] = (dbg + _dot_tn(dgb, cg)).astype(dbm_ref.dtype)
        dxd = dxdt[...]
        dxs_ref[...] = (dx_ref[...] * dyv + dxd * dtx).astype(dxs_ref.dtype)
        et = et_ref[...]
        ddt = _dot(dxd * xf, et, hi=True)
        dtot128 = _dot(jnp.broadcast_to(dtotx[...], (8, ds)), et, hi=True)[0:1, :]
        row128 = lax.broadcasted_iota(jnp.int32, (q, SMALL_W), 0)
        dcs = dcs + _dot(dcsx[...], et, hi=True) + jnp.where(row128 == q - 1, dtot128, 0.0)
        dda = _dot(upper.astype(F32), dcs, hi=True)
        ddt_ref[...] = ddt + dda * a_ref[...]
        dap_ref[...] += jnp.sum(dda * dtv, axis=0, keepdims=True)

    rev = lambda c: nc - 1 - c
    sp = _ssd_specs(l, ds, rev)
    row3 = lambda w: pl.BlockSpec((None, q, w), lambda i, c: (i, rev(c), 0))
    acc = lambda w: pl.BlockSpec((1, w), lambda i, c: (0, 0))
    return _pcall(body, name=name, grid=(b, nc),
                  in_specs=[sp['xs'], sp['bm'], sp['cm'], sp['z'], sp['dt'], sp['da'], sp['dat'], sp['e'], sp['et'],
                            sp['vec'], sp['vec'], sp['vec128'], sp['z'], sp['hin'], sp['z']],
                  out_specs=[row3(ds), row3(gn), row3(gn), row3(ds), row3(SMALL_W), acc(ds), acc(ds), acc(SMALL_W)],
                  out_shape=[_sds((b, l, ds), BF16), _sds((b, l, gn), BF16), _sds((b, l, gn), BF16), _sds((b, l, ds), BF16),
                             _sds((b, l, SMALL_W), F32), _sds((1, ds), F32), _sds((1, ds), F32), _sds((1, SMALL_W), F32)],
                  scratch_shapes=[pltpu.VMEM((n, ds), F32), pltpu.VMEM((q, ds), F32), pltpu.VMEM((q, ds), F32),
                                  pltpu.VMEM((1, ds), F32)],
                  compiler_params=_cparams(("arbitrary", "arbitrary")))(
                      xbc, xbc, xbc, z, dt, da, dat, e_mat, et_mat, dx, nw, a128, yraw, hin, dy)


_GROUP_SIZE = {'c': 2, 'xy': 4, 'xyc': 8}
_LOCAL_SPLIT = 16


def _exchange(src, group, scatter, name, nsplit=1, copy_own=True):
    n = _GROUP_SIZE[group]
    rows, width = src.shape[-2:]
    assert src.ndim == (3 if scatter else 2)
    while rows % (8 * nsplit):
        nsplit //= 2
    crow = rows // nsplit
    nlocal = _LOCAL_SPLIT
    while rows % (8 * nlocal):
        nlocal //= 2
    lrow = rows // nlocal

    def body(src_ref, out_ref, send_sems, recv_sems, local_sems):
        x, y, c = lax.axis_index("x"), lax.axis_index("y"), lax.axis_index("c")
        if group == 'c':
            rank = c
            dev = lambda r: (x, y, r)
        elif group == 'xy':
            rank = 2 * x + y
            dev = lambda r: (r // 2, r % 2, c)
        else:
            rank = 4 * x + 2 * y + c
            dev = lambda r: (r // 4, (r // 2) % 2, r % 2)

        def mine_for(r, ck):
            piece = src_ref.at[r] if scatter else src_ref
            return piece.at[pl.ds(ck * crow, crow)]

        def copy(k, ck, pr, dst_rank):
            return pltpu.make_async_remote_copy(
                src_ref=mine_for(pr, ck), dst_ref=out_ref.at[dst_rank].at[pl.ds(ck * crow, crow)],
                send_sem=send_sems.at[k * nsplit + ck], recv_sem=recv_sems.at[k * nsplit + ck], device_id=dev(pr),
                device_id_type=pl.DeviceIdType.MESH)

        locals_ = []
        if copy_own:
            own = src_ref.at[rank] if scatter else src_ref
            for ck in range(nlocal):
                rs = pl.ds(ck * lrow, lrow)
                locals_.append(pltpu.make_async_copy(own.at[rs], out_ref.at[rank].at[rs], local_sems.at[ck]))
                locals_[-1].start()
        peers = [jnp.bitwise_xor(rank, k + 1) for k in range(n - 1)]
        sends = [copy(k, ck, pr, rank) for ck in range(nsplit) for k, pr in enumerate(peers)]
        for cp in sends:
            cp.start()
        for ck in range(nsplit):
            for k, pr in enumerate(peers):
                copy(k, ck, pr, pr).wait_recv()
        for cp in sends:
            cp.wait_send()
        for cp in locals_:
            cp.wait()

    return _pcall(body, name=name, in_specs=[pl.BlockSpec(memory_space=pl.ANY)],
                  out_specs=pl.BlockSpec(memory_space=pl.ANY), out_shape=_sds((n, rows, width), src.dtype),
                  scratch_shapes=[pltpu.SemaphoreType.DMA(((n - 1) * nsplit,)),
                                  pltpu.SemaphoreType.DMA(((n - 1) * nsplit,)),
                                  pltpu.SemaphoreType.DMA((nlocal,))])(src)


def _exchange_multi(srcs, group, scatter, name, single=False, min_copies=16):
    n = _GROUP_SIZE[group]
    assert not single or n == 2
    na = len(srcs)
    shapes = [tuple(s.shape[-2:]) for s in srcs]
    want = max(1, -(-min_copies // (na * (n - 1))))
    splits = []
    for (rows, _), s in zip(shapes, srcs):
        quant = 8 * (4 // s.dtype.itemsize)
        k = want
        while k > 1 and rows % (quant * k):
            k -= 1
        splits.append(k)
    offs = [int(v) for v in np.cumsum([0] + [(n - 1) * k for k in splits])]

    def body(*refs):
        src_refs, out_refs = refs[:na], refs[na:2 * na]
        send_sems, recv_sems = refs[2 * na:]
        x, y, c = lax.axis_index("x"), lax.axis_index("y"), lax.axis_index("c")
        if group == 'c':
            rank = c
            dev = lambda r: (x, y, r)
        elif group == 'xy':
            rank = 2 * x + y
            dev = lambda r: (r // 2, r % 2, c)
        else:
            rank = 4 * x + 2 * y + c
            dev = lambda r: (r // 4, (r // 2) % 2, r % 2)
        peers = [jnp.bitwise_xor(rank, k + 1) for k in range(n - 1)]

        def copy(a, k, ck, dst_rank):
            crow = shapes[a][0] // splits[a]
            rs = pl.ds(ck * crow, crow)
            piece = src_refs[a].at[peers[k]] if scatter else src_refs[a]
            dst = out_refs[a] if single else out_refs[a].at[dst_rank]
            sem = offs[a] + k * splits[a] + ck
            return pltpu.make_async_remote_copy(src_ref=piece.at[rs], dst_ref=dst.at[rs], send_sem=send_sems.at[sem],
                                                recv_sem=recv_sems.at[sem], device_id=dev(peers[k]),
                                                device_id_type=pl.DeviceIdType.MESH)

        todo = [(a, k, ck) for a in range(na) for ck in range(splits[a]) for k in range(n - 1)]
        sends = [copy(a, k, ck, rank) for a, k, ck in todo]
        for cp in sends:
            cp.start()
        for a, k, ck in todo:
            copy(a, k, ck, peers[k]).wait_recv()
        for cp in sends:
            cp.wait_send()

    any_spec = pl.BlockSpec(memory_space=pl.ANY)
    out_shape = [_sds(sh if single else (n,) + sh, s.dtype) for sh, s in zip(shapes, srcs)]
    return _pcall(body, name=name, in_specs=[any_spec] * na, out_specs=[any_spec] * na, out_shape=out_shape,
                  scratch_shapes=[pltpu.SemaphoreType.DMA((offs[-1],)), pltpu.SemaphoreType.DMA((offs[-1],))])(*srcs)


def _sum_slots(arr, out_dtype, name):
    n, rows, cols = arr.shape
    tm = _pick(rows, [c for c in (384, 256, 128, 64, 32, 16, 8) if c * cols <= _ROWWISE_TILE_ELEMS or c == 8])

    def body(*refs):
        acc = refs[0][...].astype(F32)
        for r in refs[1:n]:
            acc = acc + r[...].astype(F32)
        refs[n][...] = acc.astype(refs[n].dtype)

    return _pcall(body, name=name, grid=(rows // tm,),
                  in_specs=[pl.BlockSpec((None, tm, cols), lambda i, j=j: (j, i, 0)) for j in range(n)],
                  out_specs=pl.BlockSpec((tm, cols), lambda i: (i, 0)), out_shape=_sds((rows, cols), out_dtype),
                  compiler_params=_cparams(("parallel",)))(*([arr] * n))


def _dims():
    d = D_MODEL
    h = ATTN_HEADS
    d_ssd = d
    d_xbc = d_ssd + 2 * SSD_GROUPS * SSD_STATE
    sizes = (d, d, d, h, d_ssd, d_xbc, d_ssd // SSD_HEAD_DIM, d, d, 3 * d)
    return d, h, d_ssd, d_xbc, sizes


def _w_in_split(w):
    d, h, d_ssd, d_xbc, sizes = _dims()
    off = np.concatenate([[0], np.cumsum(sizes)])
    seg = lambda i: w[..., off[i]:off[i + 1]]
    main = jnp.concatenate([seg(0), seg(1), seg(2), seg(4), seg(5), seg(7), seg(8), seg(9)], axis=-1)
    pad = jnp.zeros(w.shape[:-1] + (SMALL_W - sizes[3] - sizes[6],), w.dtype)
    small = jnp.concatenate([seg(3), seg(6), pad], axis=-1)
    return main, small


def _w_in_merge(main, small):
    d, h, d_ssd, d_xbc, sizes = _dims()
    order = (0, 1, 2, 4, 5, 7, 8, 9)
    moff = np.concatenate([[0], np.cumsum([sizes[i] for i in order])])
    pieces = {i: main[..., moff[j]:moff[j + 1]] for j, i in enumerate(order)}
    pieces[3] = small[..., :sizes[3]]
    pieces[6] = small[..., sizes[3]:sizes[3] + sizes[6]]
    return jnp.concatenate([pieces[i] for i in range(10)], axis=-1)


def _main_offsets():
    d, h, d_ssd, d_xbc, sizes = _dims()
    names = ('q', 'k', 'v', 'z', 'xbc', 'xr', 'gate', 'merge')
    widths = (d, d, d, d_ssd, d_xbc, d, d, 3 * d)
    off = np.concatenate([[0], np.cumsum(widths)])
    return {nm: (int(off[i]), int(off[i + 1])) for i, nm in enumerate(names)}


def _block_diag(w):
    nb, s, _ = w.shape
    eye = jnp.eye(nb, dtype=w.dtype)
    return (eye[:, None, :, None] * w[:, :, None, :]).reshape(nb * s, nb * s)


def _diag_blocks(wd, nb):
    s = wd.shape[0] // nb
    return jnp.stack([wd[i * s:(i + 1) * s, i * s:(i + 1) * s] for i in range(nb)])


def _vec128(*parts):
    v = jnp.concatenate([p.astype(F32) for p in parts])
    return jnp.pad(v, (0, SMALL_W - v.shape[0]))[None, :]


def _ffn_fwd(h, gnorm, w, tag):
    xn = _rms_fwd(h, gnorm[None, :], f"{tag}_norm")
    g, u, act = _mm_swiglu(xn, w['wg'], w['wu'], f"{tag}_gu")
    out = _mm_nn(act, w['wd'], F32, res=h, alpha=0.5, name=f"{tag}_down")
    return out, (h, xn, g, u, act)


def _ffn_bwd(dout, saved, gnorm, w, tag):
    h, xn, g, u, act = saved
    dg, du = _mm_dswiglu(dout, w['wd_t'], g, u, 0.5, f"{tag}_dgu")
    dwd = _mm_tn(act, dout, alpha=0.5, name=f"{tag}_dwd")
    dwgu = jnp.concatenate([_mm_tn(xn, dg, name=f"{tag}_dwg"), _mm_tn(xn, du, name=f"{tag}_dwu")], axis=1)
    dxn = _mm_nn(dg, w['wg_t'], F32, name=f"{tag}_dxn_g")
    dxn = _mm_nn(du, w['wu_t'], F32, res=dxn, name=f"{tag}_dxn_u")
    dh, dgn = _rms_bwd(h, dxn, dout, gnorm[None, :], f"{tag}_dnorm")
    return dh, dgn[0], dwgu, dwd


def _mixer_fwd(h, p, b, l):
    d, nh, d_ssd, d_xbc, sizes = _dims()
    t = b * l
    off = _main_offsets()
    xn = _rms_fwd(h, p['mix_norm'][None, :], "mix_norm")
    pm = _mm_nn(xn, p['w_main'], BF16, name="mix_in_main")
    ps = _mm_nn(xn, p['w_small'], F32, name="mix_in_small")
    col = lambda nm: pm[:, off[nm][0]:off[nm][1]]
    heads_ssd = d_ssd // SSD_HEAD_DIM
    a_neg = -jnp.exp(p['ssd_a_log'])
    fb = _vec128(p['fox_forget_bias'])
    dtb = _vec128(jnp.zeros((nh,), F32), p['ssd_dt_bias'])
    a128 = _vec128(jnp.zeros((nh,), F32), a_neg)

    def prep(_, v, fbv, dtbv, av):
        lane = lax.broadcasted_iota(jnp.int32, (1, SMALL_W), 1)
        logf = jnp.where(lane < nh, -_softplus(-(v + fbv)), 0.0)
        dtv = jnp.where((lane >= nh) & (lane < nh + heads_ssd), _softplus(v + dtbv), 0.0)
        return logf, dtv, dtv * av
    logf, dt, da = _rowwise(prep, [ps], [(SMALL_W, F32)] * 3, bcast=[fb, dtb, a128], name="mix_prep")

    cum = _cumsum_seq(logf.reshape(b, l, SMALL_W), False, "fox_cumsum")
    qa, ka, va = _attn_prep(pm.reshape(b, l, -1), cum, (off['q'][0], off['k'][0], off['v'][0]), "fox_prep")
    y_a3, lse = _flash_fwd(qa, ka, va, d, "fox_fwd")
    y_a = y_a3.reshape(t, d)

    xbc = col('xbc').reshape(b, l, d_xbc)
    pre_b, xbc_act = _conv_fwd(xbc, p['ssd_conv_w'], p['ssd_conv_b'][None, :], BF16, "ssd_conv", with_silu=True)
    z = col('z').reshape(b, l, d_ssd)
    dt3, da3 = dt.reshape(b, l, SMALL_W), da.reshape(b, l, SMALL_W)
    dat3 = da3.transpose(0, 2, 1)
    e_mat = _expand_matrix(nh, heads_ssd)
    dx = jnp.repeat(p['ssd_d'], SSD_HEAD_DIM)[None, :]
    nw = p['ssd_norm'][None, :]
    y_b3, yraw, hin = _ssd_fwd(xbc_act, z, dt3, da3, dat3, e_mat, dx, nw, "ssd_fwd")
    y_b = y_b3.reshape(t, d_ssd)

    xr = col('xr').reshape(b, l, d)
    xc = _conv_fwd(xr, p['lru_conv_w'], p['lru_conv_b'][None, :], F32, "lru_conv").reshape(t, d)
    pre_ri = _mm_nn(xc, p['lru_w_ri'], F32, name="lru_gates")
    lvec = (p['lru_b_a'][None, :], p['lru_b_x'][None, :], p['lru_lambda'][None, :])
    a_l, u_l = _rowwise(_lru_point_fwd, [pre_ri, xc], [(d, F32)] * 2, bcast=lvec, name="lru_point", period=l)
    gate = col('gate')
    hs, y_c = _linear_scan(a_l.reshape(b, l, d), u_l.reshape(b, l, d), False, "lru_scan", gate=gate.reshape(b, l, d))
    hs, y_c = hs.reshape(t, d), y_c.reshape(t, d)

    ba = _mm_nn(y_a, p['w_branch_attn'], BF16, name="branch_attn")
    bb = _mm_nn(y_b, p['w_branch_ssd'], BF16, name="branch_ssd")
    bc = _mm_nn(y_c, p['w_branch_lru'], BF16, name="branch_lru")
    mg = col('merge')
    mixed = _merge_fwd(mg, ba, bb, bc, "merge")
    out = _mm_nn(mixed, p['w_out'], F32, res=h, name="mix_out")
    saved = dict(h=h, xn=xn, ps=ps, fb=fb, dtb=dtb, a128=a128, qa=qa, ka=ka, va=va, lse=lse,
                 xbc=xbc, pre_b=pre_b, xbc_act=xbc_act, z=z, dt3=dt3, da3=da3, dat3=dat3, e_mat=e_mat, dx=dx, nw=nw,
                 yraw=yraw, hin=hin, xr=xr, xc=xc, pre_ri=pre_ri, lvec=lvec, a_l=a_l, hs=hs, gate=gate, y_a=y_a, y_b=y_b,
                 y_c=y_c, ba=ba, bb=bb, bc=bc, mg=mg, mixed=mixed)
    return out, saved


def _expand_matrix(nh, heads_ssd):
    e = np.zeros((SMALL_W, heads_ssd * SSD_HEAD_DIM), np.float32)
    for hh in range(heads_ssd):
        e[nh + hh, hh * SSD_HEAD_DIM:(hh + 1) * SSD_HEAD_DIM] = 1.0
    return jnp.asarray(e)


def _lru_gates(pre, xc, bav, bxv, lamv, pos):
    d = xc.shape[1]
    r = _sigmoid(pre[:, :d] + bav)
    i = _sigmoid(pre[:, d:] + bxv)
    ls = -_softplus(-lamv)
    la = LRU_C * r * ls
    a = jnp.exp(la)
    mult = jnp.where(pos == 0, 1.0, jnp.sqrt(-_expm1(2.0 * la)))
    return r, i, ls, a, mult


def _lru_point_fwd(pos, pre, xc, bav, bxv, lamv):
    r, i, ls, a, mult = _lru_gates(pre, xc, bav, bxv, lamv, pos)
    return a, mult * (i * xc)


def _lru_point_bwd(pos, g, hprev, pre, xc, bav, bxv, lamv):
    r, i, ls, a, mult = _lru_gates(pre, xc, bav, bxv, lamv, pos)
    da = g * hprev
    di = g * mult * xc
    dxc = g * mult * i
    dmult = jnp.where(pos == 0, 0.0, g * i * xc)
    dla = da * a - dmult * (a * a) / mult
    dpre_r = dla * (LRU_C * ls) * r * (1.0 - r)
    dpre_i = di * i * (1.0 - i)
    dlam = jnp.sum(dla * (LRU_C * r), axis=0, keepdims=True) * _sigmoid(-lamv)
    return (jnp.concatenate([dpre_r, dpre_i], axis=1), dxc, dlam, jnp.sum(dpre_r, axis=0, keepdims=True),
            jnp.sum(dpre_i, axis=0, keepdims=True))


def _mixer_bwd(dout, s, p, b, l):
    d, nh, d_ssd, d_xbc, sizes = _dims()
    t = b * l
    heads_ssd = d_ssd // SSD_HEAD_DIM
    g = {}
    dmixed = _mm_nn(dout, p['w_out_t'], BF16, name="mix_out_dx")
    g['w_out'] = _mm_tn(s['mixed'], dout, name="mix_out_dw")
    dba, dbb, dbc, dmerge = _merge_bwd(s['mg'], s['ba'], s['bb'], s['bc'], dmixed, "merge_bwd")
    g['w_branch_attn'] = _mm_tn(s['y_a'], dba, name="branch_attn_dw")
    g['w_branch_ssd'] = _mm_tn(s['y_b'], dbb, name="branch_ssd_dw")
    g['w_branch_lru'] = _mm_tn(s['y_c'], dbc, name="branch_lru_dw")
    dy_a = _mm_nn(dba, p['w_branch_attn_t'], BF16, name="branch_attn_dx")
    dy_b = _mm_nn(dbb, p['w_branch_ssd_t'], BF16, name="branch_ssd_dx")
    dy_c = _mm_nn(dbc, p['w_branch_lru_t'], F32, name="branch_lru_dx")

    dgate, dhs = _rowwise(lambda _, dv, hv, gv: (dv * hv * _dgelu(gv.astype(F32)), dv * _gelu(gv.astype(F32))),
                          [dy_c, s['hs'], s['gate']], [(d, BF16), (d, F32)], name="lru_out_bwd")
    a3 = s['a_l'].reshape(b, l, d)
    a_next = jnp.concatenate([a3[:, 1:], jnp.zeros((b, 1, d), F32)], axis=1)
    gs = _linear_scan(a_next, dhs.reshape(b, l, d), True, "lru_scan_bwd").reshape(t, d)
    h3 = s['hs'].reshape(b, l, d)
    hprev = jnp.concatenate([jnp.zeros((b, 1, d), F32), h3[:, :-1]], axis=1).reshape(t, d)
    dpre_ri, dxc0, dlam, dba_, dbx_ = _rowwise(_lru_point_bwd, [gs, hprev, s['pre_ri'], s['xc']],
                                               [(2 * d, BF16), (d, F32)], bcast=s['lvec'],
                                               reds=[(1, d)] * 3, name="lru_point_bwd", period=l)
    g['lru_lambda'], g['lru_b_a'], g['lru_b_x'] = dlam[0], dba_[0], dbx_[0]
    dxc = _mm_nn(dpre_ri, p['lru_w_ri_t'], BF16, res=dxc0, name="lru_gates_dx")
    dw_ri = _mm_tn(s['xc'], dpre_ri, name="lru_gates_dw")
    g['lru_w_a'] = _diag_blocks(dw_ri[:, :d], LRU_BLOCKS)
    g['lru_w_x'] = _diag_blocks(dw_ri[:, d:], LRU_BLOCKS)
    dxr, dwl = _conv_bwd(s['xr'], dxc.reshape(b, l, d), p['lru_conv_w'], "lru_conv_bwd")
    g['lru_conv_w'], g['lru_conv_b'] = dwl[:CONV_K], dwl[CONV_K]

    et_mat = s['e_mat'].T
    dxs, dbm, dcm, dz, ddt, dd_l, dnw, dap = _ssd_bwd(s['xbc_act'], s['z'], s['dt3'], s['da3'], s['dat3'], s['e_mat'],
                                                      et_mat, s['dx'], s['nw'], s['a128'], s['yraw'], s['hin'],
                                                      dy_b.reshape(b, l, d_ssd), "ssd_bwd")
    g['ssd_d'] = dd_l.reshape(heads_ssd, SSD_HEAD_DIM).sum(axis=1)
    g['ssd_norm'] = dnw[0]
    g['ssd_a_log'] = dap[0, nh:nh + heads_ssd] * (-jnp.exp(p['ssd_a_log']))
    dxbc_act = jnp.concatenate([dxs, dbm, dcm], axis=2).reshape(t, d_xbc)
    dpre_b = _rowwise(lambda _, dv, pv: dv.astype(F32) * _dsilu(pv.astype(F32)),
                      [dxbc_act, s['pre_b'].reshape(t, d_xbc)], [(d_xbc, BF16)], name="ssd_conv_act_bwd")[0]
    dxbc, dws = _conv_bwd(s['xbc'], dpre_b.reshape(b, l, d_xbc), p['ssd_conv_w'], "ssd_conv_bwd")
    g['ssd_conv_w'], g['ssd_conv_b'] = dws[:CONV_K], dws[CONV_K]

    qa2, doa = _attn_prep_bwd(dy_a.reshape(b, l, d), s['y_a'].reshape(b, l, d), s['qa'], s['lse'], "fox_prep_bwd")
    dq3, dk3, dv3, dcq, dck = _flash_bwd(qa2, s['ka'], s['va'], doa, d, "fox_bwd")
    dcum = jnp.pad((dcq - dck)[..., 0].transpose(0, 2, 1), ((0, 0), (0, 0), (0, SMALL_W - nh)))
    dlogf = _cumsum_seq(dcum, True, "fox_cumsum_bwd").reshape(t, SMALL_W)

    def prep_bwd(_, v, dlf, ddtv, fbv, dtbv):
        a_ = dlf * _sigmoid(-(v + fbv))
        b_ = ddtv * _sigmoid(v + dtbv)
        return a_ + b_, jnp.sum(a_, axis=0, keepdims=True), jnp.sum(b_, axis=0, keepdims=True)
    dps, dfb, ddtb = _rowwise(prep_bwd, [s['ps'], dlogf, ddt.reshape(t, SMALL_W)], [(SMALL_W, F32)],
                              bcast=[s['fb'], s['dtb']], reds=[(1, SMALL_W)] * 2, name="mix_prep_bwd")
    g['fox_forget_bias'] = dfb[0, :nh]
    g['ssd_dt_bias'] = ddtb[0, nh:nh + heads_ssd]

    dpm = jnp.concatenate([dq3.reshape(t, d), dk3.reshape(t, d), dv3.reshape(t, d),
                           dz.reshape(t, d_ssd), dxbc.reshape(t, d_xbc), dxr.reshape(t, d), dgate, dmerge], axis=1)
    dxn = _mm_nn(dps, p['w_small_t'], F32, name="mix_in_small_dx")
    dxn = _mm_nn(dpm, p['w_main_t'], F32, res=dxn, name="mix_in_main_dx")
    g['w_main'] = _mm_tn(s['xn'], dpm, name="mix_in_main_dw")
    g['w_small'] = _mm_tn(s['xn'], dps, name="mix_in_small_dw")
    dh, dg = _rms_bwd(s['h'], dxn, dout, p['mix_norm'][None, :], "mix_norm_bwd")
    g['mix_norm'] = dg[0]
    return dh, g


def _layer_params(w, li):
    p = {n: w[n][li] for n in WEIGHTS if n not in ('meta_tokens', 'final_norm')}
    bf = lambda a: a.astype(BF16)
    for tag in ('ffn1', 'ffn2'):
        wgu, wd = bf(p[tag + '_w_gate_up']), bf(p[tag + '_w_down'])
        f = wd.shape[0]
        p[tag] = dict(wg=wgu[:, :f], wu=wgu[:, f:], wg_t=wgu[:, :f].T, wu_t=wgu[:, f:].T, wd=wd, wd_t=wd.T)
    wm, ws = _w_in_split(bf(p['w_in']))
    p['w_main'], p['w_main_t'], p['w_small'], p['w_small_t'] = wm, wm.T, ws, ws.T
    for n in ('w_branch_attn', 'w_branch_ssd', 'w_branch_lru', 'w_out'):
        p[n + '_t'] = bf(p[n]).T
        p[n] = bf(p[n])
    wri = jnp.concatenate([_block_diag(p['lru_w_a']), _block_diag(p['lru_w_x'])], axis=1)
    p['lru_w_ri'], p['lru_w_ri_t'] = bf(wri), bf(wri).T
    return p


def _local_step(x, loss_target, w):
    b, seq, d = x.shape
    length = N_META + seq
    l = -(-length // Q_BLOCK) * Q_BLOCK
    t = b * l
    meta = jnp.broadcast_to(w['meta_tokens'].astype(F32)[None], (b, N_META, d))
    h = jnp.concatenate([meta, x, jnp.zeros((b, l - length, d), F32)], axis=1).reshape(t, d)
    tgt = jnp.concatenate([jnp.zeros((b, N_META, d), F32), loss_target, jnp.zeros((b, l - length, d), F32)],
                          axis=1).reshape(t, d)
    params, saves = [], []
    for li in range(DEPTH):
        p = _layer_params(w, li)
        h, s1 = _ffn_fwd(h, p['ffn1_norm'], p['ffn1'], "ffn1")
        h, sm = _mixer_fwd(h, p, b, l)
        h, s2 = _ffn_fwd(h, p['ffn2_norm'], p['ffn2'], "ffn2")
        params.append(p)
        saves.append((s1, sm, s2))
    dh, loss, dgf = _loss_head(h, tgt, w['final_norm'][None, :], l, "loss_head")
    layer_grads = [None] * DEPTH
    for li in reversed(range(DEPTH)):
        p = params[li]
        s1, sm, s2 = saves[li]
        g = {}
        dh, g['ffn2_norm'], g['ffn2_w_gate_up'], g['ffn2_w_down'] = _ffn_bwd(dh, s2, p['ffn2_norm'], p['ffn2'], "ffn2b")
        dh, gm = _mixer_bwd(dh, sm, p, b, l)
        g.update(gm)
        g['w_in'] = _w_in_merge(g.pop('w_main'), g.pop('w_small'))
        dh, g['ffn1_norm'], g['ffn1_w_gate_up'], g['ffn1_w_down'] = _ffn_bwd(dh, s1, p['ffn1_norm'], p['ffn1'], "ffn1b")
        layer_grads[li] = g
    grads = {n: jnp.stack([layer_grads[li][n] for li in range(DEPTH)]) for n in layer_grads[0]}
    for n in ('lru_w_a', 'lru_w_x'):
        grads[n] = grads[n].reshape(w[n].shape)
    dh3 = dh.reshape(b, l, d)
    grads['meta_tokens'] = jnp.sum(dh3[:, :N_META], axis=0)
    grads['final_norm'] = dgf[0]
    return loss, dh3[:, N_META:N_META + seq], grads


def _unflatten(flat, shapes):
    out, o = [], 0
    for sh in shapes:
        n = int(np.prod(sh))
        out.append(flat[o:o + n].reshape(sh))
        o += n
    return out


def kernel(x, meta_tokens, ffn1_norm, ffn1_w_gate_up, ffn1_w_down, mix_norm, w_in, fox_forget_bias, ssd_conv_w, ssd_conv_b, ssd_dt_bias, ssd_a_log, ssd_d, ssd_norm, lru_conv_w, lru_conv_b, lru_w_a, lru_b_a, lru_w_x, lru_b_x, lru_lambda, w_branch_attn, w_branch_ssd, w_branch_lru, w_out, ffn2_norm, ffn2_w_gate_up, ffn2_w_down, final_norm, loss_target, m_meta_tokens, m_ffn1_norm, m_ffn1_w_gate_up, m_ffn1_w_down, m_mix_norm, m_w_in, m_fox_forget_bias, m_ssd_conv_w, m_ssd_conv_b, m_ssd_dt_bias, m_ssd_a_log, m_ssd_d, m_ssd_norm, m_lru_conv_w, m_lru_conv_b, m_lru_w_a, m_lru_b_a, m_lru_w_x, m_lru_b_x, m_lru_lambda, m_w_branch_attn, m_w_branch_ssd, m_w_branch_lru, m_w_out, m_ffn2_norm, m_ffn2_w_gate_up, m_ffn2_w_down, m_final_norm, v_meta_tokens, v_ffn1_norm, v_ffn1_w_gate_up, v_ffn1_w_down, v_mix_norm, v_w_in, v_fox_forget_bias, v_ssd_conv_w, v_ssd_conv_b, v_ssd_dt_bias, v_ssd_a_log, v_ssd_d, v_ssd_norm, v_lru_conv_w, v_lru_conv_b, v_lru_w_a, v_lru_b_a, v_lru_w_x, v_lru_b_x, v_lru_lambda, v_w_branch_attn, v_w_branch_ssd, v_w_branch_lru, v_w_out, v_ffn2_norm, v_ffn2_w_gate_up, v_ffn2_w_down, v_final_norm):
    args = locals()
    wloc = {n: args[n] for n in WEIGHTS}
    mloc = {n: args['m_' + n] for n in WEIGHTS}
    vloc = {n: args['v_' + n] for n in WEIGHTS}
    nchip = 4
    chip = 2 * lax.axis_index("x") + lax.axis_index("y")
    core = lax.axis_index("c")
    hl = DEPTH // 2

    own = lambda out, mine, rank: lax.dynamic_update_index_in_dim(out, mine, rank, 0)
    half_rows = lambda a, which: lax.dynamic_slice_in_dim(a, which * (a.shape[0] // 2), a.shape[0] // 2, axis=0)
    mine = [half_rows(wloc[n].astype(BF16).reshape(-1, wloc[n].shape[-1]), core) for n in BIG_NAMES]
    got = _exchange_multi(mine, 'xy', False, "gather_w_chips")
    got = [own(g_, m_, chip).reshape(nchip * m_.shape[0], m_.shape[1]) for g_, m_ in zip(got, mine)]
    both = _exchange_multi(got, 'c', False, "gather_w_cores")
    full = {}
    for n, b_, g_ in zip(BIG_NAMES, both, got):
        _, r, c = wloc[n].shape
        v = own(b_, g_, core).reshape(2, nchip, hl, r, c)
        if BIG[n] == 1:
            full[n] = v.transpose(0, 2, 3, 1, 4).reshape(DEPTH, r, nchip * c)
        else:
            full[n] = v.transpose(0, 2, 1, 3, 4).reshape(DEPTH, nchip * r, c)
    cs_shapes = [wloc[n].shape for n in COLSHARD_SMALL]
    cs_total = sum(int(np.prod(s)) for s in cs_shapes)
    cs_rows = -(-cs_total // (8 * 128)) * 8
    cs_flat = jnp.concatenate([wloc[n].reshape(-1) for n in COLSHARD_SMALL])
    cs_flat = jnp.pad(cs_flat, (0, cs_rows * 128 - cs_total)).reshape(cs_rows, 128)
    cs_all = _exchange(cs_flat, 'xy', False, "gather_small").reshape(nchip, -1)
    cs_chip = [_unflatten(cs_all[j], cs_shapes) for j in range(nchip)]
    for i, n in enumerate(COLSHARD_SMALL):
        full[n] = jnp.concatenate([cs_chip[j][i] for j in range(nchip)], axis=-1)
    for n in SMALL_NAMES:
        if n not in COLSHARD_SMALL:
            full[n] = wloc[n]

    loss_part, grad_x, grads = _local_step(x, loss_target, full)

    g2d = [grads[n].reshape(-1, grads[n].shape[-1]) for n in BIG_NAMES]
    give = [half_rows(g_, 1 - core) for g_ in g2d]
    keep = [half_rows(g_, core) for g_ in g2d]
    theirs = _exchange_multi(give, 'c', False, "reduce_cores", single=True)
    psums = []
    for n, k_, t_ in zip(BIG_NAMES, keep, theirs):
        s2 = _sum_rows([k_, t_], BF16, "reduce_cores_sum")
        _, r, c = wloc[n].shape
        if BIG[n] == 1:
            psums.append(s2.reshape(s2.shape[0], nchip, c).transpose(1, 0, 2))
        else:
            psums.append(s2.reshape(hl, nchip, r, c).transpose(1, 0, 2, 3).reshape(nchip, hl * r, c))
    parts = _exchange_multi(psums, 'xy', True, "reduce_chips")
    parts = [own(p_, lax.dynamic_index_in_dim(s_, chip, axis=0, keepdims=False), chip) for p_, s_ in zip(parts, psums)]
    rsums = [_sum_slots(p_, F32, "reduce_chips_sum") for p_ in parts]
    halves = _exchange_multi(rsums, 'c', False, "reduce_share")
    gbig = {n: own(h_, r_, core).reshape(wloc[n].shape) for n, h_, r_ in zip(BIG_NAMES, halves, rsums)}

    sm_shapes = [grads[n].shape for n in SMALL_NAMES]
    sm_total = sum(int(np.prod(s)) for s in sm_shapes) + 128
    sm_rows = -(-sm_total // (8 * 128)) * 8
    sm_flat = jnp.concatenate([loss_part.reshape(-1)] + [grads[n].reshape(-1) for n in SMALL_NAMES])
    sm_flat = jnp.pad(sm_flat, (0, sm_rows * 128 - sm_total)).reshape(sm_rows, 128)
    sm_all = _exchange(sm_flat, 'xyc', False, "gather_small_grads")
    sm_sum = _sum_rows([sm_all[j] for j in range(8)], F32, "small_grads_sum").reshape(-1)
    loss = sm_sum[0]
    gsmall_full = dict(zip(SMALL_NAMES, _unflatten(sm_sum[128:], sm_shapes)))
    gsmall = {}
    for n in SMALL_NAMES:
        gfull = gsmall_full[n]
        if n in COLSHARD_SMALL:
            wcols = wloc[n].shape[-1]
            gfull = lax.dynamic_slice_in_dim(gfull, chip * wcols, wcols, axis=gfull.ndim - 1)
        gsmall[n] = gfull

    big_out = [{}, {}, {}]
    for n in BIG_NAMES:
        rows2d = lambda a: a.reshape(-1, a.shape[-1])
        res = _adamw(rows2d(wloc[n]), rows2d(gbig[n]), rows2d(mloc[n]), rows2d(vloc[n]), "adamw_big")
        for k in range(3):
            big_out[k][n] = res[k].reshape(wloc[n].shape)
    loc_shapes = [wloc[n].shape for n in SMALL_NAMES]
    loc_total = sum(int(np.prod(s)) for s in loc_shapes)
    loc_rows = -(-loc_total // (8 * 128)) * 8

    def flat_small(dct):
        v = jnp.concatenate([dct[n].reshape(-1) for n in SMALL_NAMES])
        return jnp.pad(v, (0, loc_rows * 128 - loc_total)).reshape(loc_rows, 128)
    dls, mns, vns = _adamw(flat_small(wloc), flat_small(gsmall), flat_small(mloc), flat_small(vloc), "adamw_small")
    small_out = [dict(zip(SMALL_NAMES, _unflatten(a.reshape(-1), loc_shapes))) for a in (dls, mns, vns)]

    grad_w = {**gbig, **gsmall}
    outs = [loss, grad_x] + [grad_w[n] for n in WEIGHTS]
    for k in range(3):
        merged = {**big_out[k], **small_out[k]}
        outs += [merged[n] for n in WEIGHTS]
    return tuple(outs)
```

```python
import functools
import math

import numpy as np
import jax
import jax.numpy as jnp
from jax import lax
from jax.experimental import pallas as pl
from jax.experimental.pallas import tpu as pltpu

F32 = jnp.float32
BF16 = jnp.bfloat16
HI = lax.Precision.HIGHEST
VMEM_LIMIT_BYTES = 56 * 1024 * 1024
NEG = -1e30

D_MODEL = 1024
SEQ = 4096
DEPTH = 4
N_META = 16
Q_BLOCK = 128
SSD_CHUNK = 128
NORM_EPS = 1e-6
ATTN_HEADS = 16
ATTN_HEAD_DIM = 64
SSD_HEAD_DIM = 64
SSD_GROUPS = 2
SSD_STATE = 128
CONV_K = 4
LRU_BLOCKS = 16
LRU_C = 8.0
D_FF = 2816
ADAM_LR = 0.001
ADAM_B1 = 0.9
ADAM_B2 = 0.999
ADAM_EPS = 1e-08
ADAM_WD = 0.01
ADAM_STEP = 10
SMALL_W = 128
_ROWWISE_TILE_ELEMS = 512 * 1024

WEIGHTS = ['meta_tokens', 'ffn1_norm', 'ffn1_w_gate_up', 'ffn1_w_down', 'mix_norm', 'w_in', 'fox_forget_bias',
           'ssd_conv_w', 'ssd_conv_b', 'ssd_dt_bias', 'ssd_a_log', 'ssd_d', 'ssd_norm', 'lru_conv_w', 'lru_conv_b',
           'lru_w_a', 'lru_b_a', 'lru_w_x', 'lru_b_x', 'lru_lambda', 'w_branch_attn', 'w_branch_ssd', 'w_branch_lru',
           'w_out', 'ffn2_norm', 'ffn2_w_gate_up', 'ffn2_w_down', 'final_norm']
BIG = {'ffn1_w_gate_up': 1, 'ffn1_w_down': 0, 'w_in': 1, 'w_branch_attn': 0, 'w_branch_ssd': 0, 'w_branch_lru': 0,
       'w_out': 0, 'ffn2_w_gate_up': 1, 'ffn2_w_down': 0}
BIG_NAMES = [n for n in WEIGHTS if n in BIG]
COLSHARD_SMALL = ['meta_tokens', 'ssd_conv_w', 'lru_conv_w']
SMALL_NAMES = [n for n in WEIGHTS if n not in BIG]


def _pick(n, cands):
    for c in cands:
        if n % c == 0:
            return c
    raise ValueError(f"no tile for {n} in {cands}")


def _pcall(body, **kw):
    return pl.pallas_call(body, **kw)


def _cparams(sem):
    return pltpu.CompilerParams(dimension_semantics=sem, vmem_limit_bytes=VMEM_LIMIT_BYTES)


def _sds(shape, dtype):
    return jax.ShapeDtypeStruct(tuple(shape), dtype)


def _dot(a, b, hi=False):
    return jnp.dot(a, b, precision=HI if hi else None, preferred_element_type=F32)


def _dot_nt(a, b):
    return lax.dot_general(a, b, (((1,), (1,)), ((), ())), preferred_element_type=F32)


def _dot_tn(a, b):
    return lax.dot_general(a, b, (((0,), (0,)), ((), ())), preferred_element_type=F32)


def _sigmoid(x):
    return 1.0 / (1.0 + jnp.exp(-x))


def _sigmoid_tanh(x):
    return 0.5 * jnp.tanh(0.5 * x) + 0.5


def _softplus(x):
    return jnp.maximum(x, 0.0) + jnp.log1p(jnp.exp(-jnp.abs(x)))


def _silu(x):
    return x * _sigmoid(x)


def _dsilu(x):
    s = _sigmoid(x)
    return s * (1.0 + x * (1.0 - s))


_GELU_C = math.sqrt(2.0 / math.pi)


def _gelu(x):
    return 0.5 * x * (1.0 + jnp.tanh(_GELU_C * (x + 0.044715 * x * x * x)))


def _dgelu(x):
    t = jnp.tanh(_GELU_C * (x + 0.044715 * x * x * x))
    return 0.5 * (1.0 + t) + 0.5 * x * (1.0 - t * t) * _GELU_C * (1.0 + 3.0 * 0.044715 * x * x)


def _expm1(x):
    series = x * (1.0 + x * 0.5 * (1.0 + x * (1.0 / 3.0) * (1.0 + x * 0.25 * (1.0 + x * 0.2))))
    return jnp.where(jnp.abs(x) < 0.05, series, jnp.exp(x) - 1.0)


def _rowwise(fn, ins, outs, *, bcast=(), reds=(), tm=None, name, period=None):
    t_rows = ins[0].shape[0]
    if tm is None:
        widest = max([a.shape[1] for a in ins] + [c for c, _ in outs])
        tm = _pick(math.gcd(t_rows, period or t_rows),
                   [c for c in (384, 256, 128, 64, 32, 16, 8) if c * widest <= _ROWWISE_TILE_ELEMS or c == 8])
    nt = t_rows // tm
    assert t_rows % tm == 0 and (period is None or period % tm == 0)
    n_in, n_out = len(ins) + len(bcast), len(outs)

    def body(*refs):
        i = pl.program_id(0)
        pos = None
        if period is not None:
            pos = (i * tm) % period + lax.broadcasted_iota(jnp.int32, (tm, 1), 0)
        res = fn(pos, *[r[...] for r in refs[:n_in]])
        res = res if isinstance(res, tuple) else (res,)
        for r, v in zip(refs[n_in:n_in + n_out], res[:n_out]):
            r[...] = v.astype(r.dtype)
        red_refs = refs[n_in + n_out:]
        if red_refs:
            @pl.when(i == 0)
            def _():
                for r in red_refs:
                    r[...] = jnp.zeros_like(r)
            for r, v in zip(red_refs, res[n_out:]):
                r[...] += v

    in_specs = [pl.BlockSpec((tm, a.shape[1]), lambda i: (i, 0)) for a in ins]
    in_specs += [pl.BlockSpec(b.shape, lambda i, n=b.ndim: (0,) * n) for b in bcast]
    out_specs = [pl.BlockSpec((tm, c), lambda i: (i, 0)) for c, _ in outs]
    out_specs += [pl.BlockSpec(s, lambda i: (0, 0)) for s in reds]
    out_shape = [_sds((t_rows, c), dt) for c, dt in outs] + [_sds(s, F32) for s in reds]
    res = _pcall(body, name=name, grid=(nt,), in_specs=in_specs, out_specs=out_specs, out_shape=out_shape,
                 compiler_params=_cparams(("arbitrary",) if reds else ("parallel",)))(*ins, *bcast)
    return res


_TM = (768, 384, 256, 128)
_TN = (1536, 1408, 1024, 768, 512, 640, 384, 256, 128)
_TK = (1536, 1024, 2816, 1408, 512, 384, 256, 128)
_TKO = (1024, 1408, 512, 384, 256, 128)


def _mm_nn(a, b, out_dtype, *, res=None, alpha=1.0, name):
    m, k = a.shape
    k2, n = b.shape
    assert k == k2
    tm, tn, tk = _pick(m, _TM), _pick(n, _TN), _pick(k, _TK)
    nk = k // tk

    def body(*refs):
        if res is None:
            a_ref, b_ref, o_ref, acc = refs
            r_ref = None
        else:
            a_ref, b_ref, r_ref, o_ref, acc = refs
        kk = pl.program_id(2)

        @pl.when(kk == 0)
        def _():
            acc[...] = jnp.zeros_like(acc)

        acc[...] += _dot(a_ref[...].astype(BF16), b_ref[...].astype(BF16))

        @pl.when(kk == nk - 1)
        def _():
            v = acc[...]
            if alpha != 1.0:
                v = v * alpha
            if r_ref is not None:
                v = r_ref[...].astype(F32) + v
            o_ref[...] = v.astype(o_ref.dtype)

    in_specs = [pl.BlockSpec((tm, tk), lambda j, i, kk: (i, kk)), pl.BlockSpec((tk, tn), lambda j, i, kk: (kk, j))]
    args = [a, b]
    if res is not None:
        in_specs.append(pl.BlockSpec((tm, tn), lambda j, i, kk: (i, j)))
        args.append(res)
    return _pcall(body, name=name, grid=(n // tn, m // tm, nk), in_specs=in_specs,
                  out_specs=pl.BlockSpec((tm, tn), lambda j, i, kk: (i, j)), out_shape=_sds((m, n), out_dtype),
                  scratch_shapes=[pltpu.VMEM((tm, tn), F32)],
                  compiler_params=_cparams(("parallel", "parallel", "arbitrary")))(*args)


def _mm_swiglu(a, wg, wu, name):
    m, k = a.shape
    f = wg.shape[1]
    tm, tn, tk = _pick(m, _TM), _pick(f, _TN), _pick(k, _TK)
    nk = k // tk

    def body(a_ref, g_w, u_w, g_ref, u_ref, act_ref, accg, accu):
        kk = pl.program_id(2)

        @pl.when(kk == 0)
        def _():
            accg[...] = jnp.zeros_like(accg)
            accu[...] = jnp.zeros_like(accu)

        av = a_ref[...].astype(BF16)
        accg[...] += _dot(av, g_w[...])
        accu[...] += _dot(av, u_w[...])

        @pl.when(kk == nk - 1)
        def _():
            g, u = accg[...], accu[...]
            g_ref[...] = g.astype(g_ref.dtype)
            u_ref[...] = u.astype(u_ref.dtype)
            act_ref[...] = (g * _sigmoid_tanh(g) * u).astype(act_ref.dtype)

    wspec = pl.BlockSpec((tk, tn), lambda j, i, kk: (kk, j))
    ospec = pl.BlockSpec((tm, tn), lambda j, i, kk: (i, j))
    return _pcall(body, name=name, grid=(f // tn, m // tm, nk),
                  in_specs=[pl.BlockSpec((tm, tk), lambda j, i, kk: (i, kk)), wspec, wspec],
                  out_specs=[ospec] * 3, out_shape=[_sds((m, f), BF16)] * 3,
                  scratch_shapes=[pltpu.VMEM((tm, tn), F32)] * 2,
                  compiler_params=_cparams(("parallel", "parallel", "arbitrary")))(a, wg, wu)


def _mm_dswiglu(dout, wd_t, g, u, alpha, name):
    m, k = dout.shape
    f = wd_t.shape[1]
    tm, tn, tk = _pick(m, _TM), _pick(f, _TN), _pick(k, _TK)
    nk = k // tk

    def body(a_ref, w_ref, g_ref, u_ref, dg_ref, du_ref, acc):
        kk = pl.program_id(2)

        @pl.when(kk == 0)
        def _():
            acc[...] = jnp.zeros_like(acc)

        acc[...] += _dot(a_ref[...].astype(BF16), w_ref[...])

        @pl.when(kk == nk - 1)
        def _():
            dact = acc[...] * alpha
            gv, uv = g_ref[...].astype(F32), u_ref[...].astype(F32)
            sg = _sigmoid_tanh(gv)
            dg_ref[...] = (dact * uv * (sg * (1.0 + gv * (1.0 - sg)))).astype(dg_ref.dtype)
            du_ref[...] = (dact * (gv * sg)).astype(du_ref.dtype)

    ospec = pl.BlockSpec((tm, tn), lambda j, i, kk: (i, j))
    return _pcall(body, name=name, grid=(f // tn, m // tm, nk),
                  in_specs=[pl.BlockSpec((tm, tk), lambda j, i, kk: (i, kk)),
                            pl.BlockSpec((tk, tn), lambda j, i, kk: (kk, j)), ospec, ospec],
                  out_specs=[ospec] * 2, out_shape=[_sds((m, f), BF16)] * 2,
                  scratch_shapes=[pltpu.VMEM((tm, tn), F32)],
                  compiler_params=_cparams(("parallel", "parallel", "arbitrary")))(dout, wd_t, g, u)


def _mm_tn(a, b, *, alpha=1.0, name):
    m, k = a.shape
    m2, n = b.shape
    assert m == m2
    tm, tn, tko = _pick(m, _TM), _pick(n, _TN), _pick(k, _TKO)
    nm = m // tm

    def body(a_ref, b_ref, o_ref, acc):
        mm = pl.program_id(2)

        @pl.when(mm == 0)
        def _():
            acc[...] = jnp.zeros_like(acc)

        acc[...] += _dot_tn(a_ref[...].astype(BF16), b_ref[...].astype(BF16))

        @pl.when(mm == nm - 1)
        def _():
            v = acc[...]
            o_ref[...] = v * alpha if alpha != 1.0 else v

    return _pcall(body, name=name, grid=(k // tko, n // tn, nm),
                  in_specs=[pl.BlockSpec((tm, tko), lambda i, j, mm: (mm, i)),
                            pl.BlockSpec((tm, tn), lambda i, j, mm: (mm, j))],
                  out_specs=pl.BlockSpec((tko, tn), lambda i, j, mm: (i, j)), out_shape=_sds((k, n), F32),
                  scratch_shapes=[pltpu.VMEM((tko, tn), F32)],
                  compiler_params=_cparams(("parallel", "parallel", "arbitrary")))(a, b)


def _rms_fwd(h, g, name):
    def fn(_, hv, gv):
        r = lax.rsqrt(jnp.mean(hv * hv, axis=1, keepdims=True) + NORM_EPS)
        return hv * r * gv
    return _rowwise(fn, [h], [(h.shape[1], BF16)], bcast=[g], name=name)[0]


def _rms_bwd(h, dxn, dres, g, name):
    d = h.shape[1]

    def fn(_, hv, dv, rv, gv):
        r = lax.rsqrt(jnp.mean(hv * hv, axis=1, keepdims=True) + NORM_EPS)
        xh = hv * r
        dxh = dv * gv
        dh = r * (dxh - xh * jnp.mean(dxh * xh, axis=1, keepdims=True))
        return rv + dh, jnp.sum(dv * xh, axis=0, keepdims=True)
    return _rowwise(fn, [h, dxn, dres], [(d, F32)], bcast=[g], reds=[(1, d)], name=name)


def _merge_fwd(mg, ba, bb, bc, name):
    d = ba.shape[1]

    def fn(_, m, a, b, c):
        g = _sigmoid(m.astype(F32))
        return g[:, :d] * a.astype(F32) + g[:, d:2 * d] * b.astype(F32) + g[:, 2 * d:] * c.astype(F32)
    return _rowwise(fn, [mg, ba, bb, bc], [(d, BF16)], name=name)[0]


def _merge_bwd(mg, ba, bb, bc, dmix, name):
    d = ba.shape[1]

    def fn(_, m, a, b, c, dm):
        g = _sigmoid(m.astype(F32))
        dm = dm.astype(F32)
        br = (a.astype(F32), b.astype(F32), c.astype(F32))
        douts, dgs = [], []
        for j in range(3):
            gj = g[:, j * d:(j + 1) * d]
            douts.append(dm * gj)
            dgs.append(dm * br[j] * gj * (1.0 - gj))
        return (*douts, jnp.concatenate(dgs, axis=1))
    return _rowwise(fn, [mg, ba, bb, bc, dmix], [(d, BF16)] * 3 + [(3 * d, BF16)], name=name)


def _loss_head(h, tgt, g, seq_len, name):
    d = h.shape[1]

    def fn(pos, hv, tv, gv):
        r = lax.rsqrt(jnp.mean(hv * hv, axis=1, keepdims=True) + NORM_EPS)
        xh = hv * r
        real = (pos >= N_META) & (pos < N_META + SEQ)
        e = jnp.where(real, xh * gv - tv, 0.0)
        part = jnp.sum(jnp.sum(e * e, axis=1, keepdims=True), axis=0, keepdims=True) * (0.5 / d)
        dy = e * (1.0 / d)
        dxh = dy * gv
        dh = r * (dxh - xh * jnp.mean(dxh * xh, axis=1, keepdims=True))
        return dh, jnp.broadcast_to(part, (1, 128)), jnp.sum(dy * xh, axis=0, keepdims=True)
    return _rowwise(fn, [h, tgt], [(d, F32)], bcast=[g], reds=[(1, 128), (1, d)], name=name, period=seq_len)


def _adamw(w, g, m, v, name):
    c1 = 1.0 - ADAM_B1 ** ADAM_STEP
    c2 = 1.0 - ADAM_B2 ** ADAM_STEP
    wd = w.shape[1]

    def fn(_, wv, gv, mv, vv):
        mn = ADAM_B1 * mv + (1.0 - ADAM_B1) * gv
        vn = ADAM_B2 * vv + (1.0 - ADAM_B2) * (gv * gv)
        delta = -ADAM_LR * ((mn / c1) / (jnp.sqrt(vn / c2) + ADAM_EPS) + ADAM_WD * wv)
        return delta, mn, vn
    return _rowwise(fn, [w, g, m, v], [(wd, F32)] * 3, name=name)


def _sum_rows(parts, out_dtype, name):
    def fn(_, *vs):
        acc = vs[0].astype(F32)
        for v in vs[1:]:
            acc = acc + v.astype(F32)
        return acc
    return _rowwise(fn, list(parts), [(parts[0].shape[1], out_dtype)], name=name)[0]


def _cumsum_seq(x, reverse, name):
    b, l, w = x.shape
    q = 128
    nc = l // q

    def body(x_ref, o_ref):
        row = lax.broadcasted_iota(jnp.int32, (q, q), 0)
        col = lax.broadcasted_iota(jnp.int32, (q, q), 1)
        tri = ((row <= col) if reverse else (row >= col)).astype(F32)
        rsel = lax.broadcasted_iota(jnp.int32, (q, w), 0) == (0 if reverse else q - 1)

        def step(i, carry):
            j = (nc - 1 - i) if reverse else i
            start = pl.multiple_of(j * q, q)
            cs = _dot(tri, x_ref[pl.ds(start, q), :], hi=True) + carry
            o_ref[pl.ds(start, q), :] = cs
            return jnp.sum(jnp.where(rsel, cs, 0.0), axis=0, keepdims=True)

        lax.fori_loop(0, nc, step, jnp.zeros((1, w), F32))

    return _pcall(body, name=name, grid=(b,), in_specs=[pl.BlockSpec((None, l, w), lambda i: (i, 0, 0))],
                  out_specs=pl.BlockSpec((None, l, w), lambda i: (i, 0, 0)), out_shape=_sds(x.shape, F32),
                  compiler_params=_cparams(("parallel",)))(x)


_HALO = 16


def _conv_tiles(l, c):
    return _pick(l, (384, 256, 128)), _pick(c, (512, 256, 128))


def _conv_fwd(x, w, bias, out_dtype, name, with_silu=False):
    b, l, c = x.shape
    tt, cw = _conv_tiles(l, c)

    def body(x_ref, h_ref, w_ref, b_ref, *o_refs):
        t = pl.program_id(2)
        halo = jnp.where(t == 0, 0.0, h_ref[...].astype(F32))
        xe = jnp.concatenate([halo, x_ref[...].astype(F32)], axis=0)
        wv = w_ref[...]
        acc = b_ref[...] + wv[CONV_K - 1:CONV_K, :] * xe[_HALO:]
        for j in range(CONV_K - 1):
            acc = acc + wv[j:j + 1, :] * pltpu.roll(xe, CONV_K - 1 - j, 0)[_HALO:]
        o_refs[0][...] = acc.astype(o_refs[0].dtype)
        if with_silu:
            o_refs[1][...] = _silu(acc.astype(o_refs[0].dtype).astype(F32)).astype(o_refs[1].dtype)

    ospec = pl.BlockSpec((None, tt, cw), lambda i, j, t: (i, t, j))
    n_out = 2 if with_silu else 1
    res = _pcall(body, name=name, grid=(b, c // cw, l // tt),
                 in_specs=[pl.BlockSpec((None, tt, cw), lambda i, j, t: (i, t, j)),
                           pl.BlockSpec((None, _HALO, cw), lambda i, j, t: (i, jnp.maximum(t * (tt // _HALO) - 1, 0), j)),
                           pl.BlockSpec((CONV_K, cw), lambda i, j, t: (0, j)),
                           pl.BlockSpec((1, cw), lambda i, j, t: (0, j))],
                 out_specs=[ospec] * n_out, out_shape=[_sds(x.shape, out_dtype)] * n_out,
                 compiler_params=_cparams(("parallel", "parallel", "parallel")))(x, x, w, bias)
    return res if with_silu else res[0]


def _conv_bwd(x, dy, w, name):
    b, l, c = x.shape
    tt, cw = _conv_tiles(l, c)
    nt = l // tt

    def body(x_ref, xh_ref, d_ref, dh_ref, w_ref, dx_ref, dw_ref):
        i, t = pl.program_id(1), pl.program_id(2)
        halo = jnp.where(t == 0, 0.0, xh_ref[...].astype(F32))
        xe = jnp.concatenate([halo, x_ref[...].astype(F32)], axis=0)
        dv = d_ref[...].astype(F32)
        nxt = jnp.where(t == nt - 1, 0.0, dh_ref[...].astype(F32))
        de = jnp.concatenate([dv, nxt], axis=0)
        wv = w_ref[...]
        dx = wv[CONV_K - 1:CONV_K, :] * dv
        rowid = lax.broadcasted_iota(jnp.int32, (8, 1), 0)
        part = jnp.where(rowid == CONV_K, jnp.sum(dv, axis=0, keepdims=True), 0.0)
        part = part + jnp.where(rowid == CONV_K - 1, jnp.sum(dv * xe[_HALO:], axis=0, keepdims=True), 0.0)
        for j in range(CONV_K - 1):
            s = CONV_K - 1 - j
            dx = dx + wv[j:j + 1, :] * pltpu.roll(de, tt + _HALO - s, 0)[:tt]
            xs = pltpu.roll(xe, s, 0)[_HALO:]
            part = part + jnp.where(rowid == j, jnp.sum(dv * xs, axis=0, keepdims=True), 0.0)
        dx_ref[...] = dx.astype(dx_ref.dtype)

        @pl.when((i == 0) & (t == 0))
        def _():
            dw_ref[...] = jnp.zeros_like(dw_ref)
        dw_ref[...] += part

    return _pcall(body, name=name, grid=(c // cw, b, nt),
                  in_specs=[pl.BlockSpec((None, tt, cw), lambda j, i, t: (i, t, j)),
                            pl.BlockSpec((None, _HALO, cw), lambda j, i, t: (i, jnp.maximum(t * (tt // _HALO) - 1, 0), j)),
                            pl.BlockSpec((None, tt, cw), lambda j, i, t: (i, t, j)),
                            pl.BlockSpec((None, _HALO, cw),
                                         lambda j, i, t: (i, jnp.minimum((t + 1) * (tt // _HALO), l // _HALO - 1), j)),
                            pl.BlockSpec((CONV_K, cw), lambda j, i, t: (0, j))],
                  out_specs=[pl.BlockSpec((None, tt, cw), lambda j, i, t: (i, t, j)),
                             pl.BlockSpec((8, cw), lambda j, i, t: (0, j))],
                  out_shape=[_sds(x.shape, BF16), _sds((8, c), F32)],
                  compiler_params=_cparams(("parallel", "arbitrary", "arbitrary")))(x, x, dy, dy, w)


_SUBLANES = 8


def _linear_scan(a, u, reverse, name, gate=None):
    b, l, c = a.shape
    tt = 128
    cw = _pick(c, (512, 256, 128))
    nt = l // tt
    groups = tt // _SUBLANES

    def body(*refs):
        if gate is None:
            a_ref, u_ref, h_ref, carry = refs
        else:
            a_ref, u_ref, g_ref, h_ref, y_ref, carry = refs
        t = pl.program_id(2)

        @pl.when(t == 0)
        def _():
            carry[...] = jnp.zeros_like(carry)

        av, uv = a_ref[...], u_ref[...]
        sub = jnp.bitwise_and(lax.broadcasted_iota(jnp.int32, (tt, cw), 0), _SUBLANES - 1)
        k = 1
        while k < _SUBLANES:
            keep = (sub < _SUBLANES - k) if reverse else (sub >= k)
            shift = tt - k if reverse else k
            a_sh = jnp.where(keep, pltpu.roll(av, shift, 0), 1.0)
            u_sh = jnp.where(keep, pltpu.roll(uv, shift, 0), 0.0)
            uv = uv + av * u_sh
            av = av * a_sh
            k *= 2
        edge = carry[0:1, :]
        for g in (range(groups - 1, -1, -1) if reverse else range(groups)):
            rows = slice(g * _SUBLANES, (g + 1) * _SUBLANES)
            hg = uv[rows] + av[rows] * edge
            h_ref[rows, :] = hg
            edge = hg[0:1] if reverse else hg[_SUBLANES - 1:_SUBLANES]
        carry[...] = jnp.broadcast_to(edge, carry.shape)
        if gate is not None:
            y_ref[...] = (h_ref[...] * _gelu(g_ref[...].astype(F32))).astype(y_ref.dtype)

    tmap = (lambda i, j, t: (i, nt - 1 - t, j)) if reverse else (lambda i, j, t: (i, t, j))
    spec = pl.BlockSpec((None, tt, cw), tmap)
    ins, outs, shapes = [a, u], [spec], [_sds(a.shape, F32)]
    if gate is not None:
        ins, outs, shapes = [a, u, gate], [spec, spec], [_sds(a.shape, F32), _sds(a.shape, BF16)]
    res = _pcall(body, name=name, grid=(b, c // cw, nt), in_specs=[spec] * len(ins), out_specs=outs,
                 out_shape=shapes, scratch_shapes=[pltpu.VMEM((8, cw), F32)],
                 compiler_params=_cparams(("parallel", "parallel", "arbitrary")))(*ins)
    return res if gate is not None else res[0]


ATTN_W = 128
_AUG_C = 0
_AUG_ONE = 3
_AUG_LSE = 6


def _attn_blk(l):
    return _pick(l, (384, 256, 128))


_FWD_HEADS_PER_STEP = 4


def _prep_rows(l):
    return _pick(l, (1408, 384, 256, 128))


def _split3(x):
    x1 = x.astype(BF16).astype(F32)
    x2 = (x - x1).astype(BF16).astype(F32)
    x3 = (x - x1 - x2).astype(BF16).astype(F32)
    return x1, x2, x3


def _aug_lanes(lane, base, cols, ones_at=()):
    out = jnp.zeros(lane.shape, F32)
    for o in ones_at:
        out = out + ((lane >= o) & (lane < o + 3)).astype(F32)
    for k, v in enumerate(cols):
        out = jnp.where(lane == base + k, v, out)
    return out


def _pair_specs(blk, nb):
    at = lambda ww: pl.BlockSpec((None, 2, blk, ww), lambda bi, p, i: (bi, p, i, 0))
    whole = pl.BlockSpec((None, 2, nb, blk, ATTN_W), lambda bi, p, i: (bi, p, 0, 0, 0))
    rows = pl.BlockSpec((None, blk, ATTN_W), lambda bi, p, i: (bi, i, p))
    return at, whole, rows


def _attn_prep(pm3, cum, col0, name):
    b, l, _ = pm3.shape
    dh, nh = ATTN_HEAD_DIM, ATTN_HEADS
    blk = _prep_rows(l)
    scale = dh ** -0.5

    def body(q_ref, k_ref, v_ref, c_ref, qa_ref, ka_ref, va_ref):
        pair = pl.program_id(1)
        lane = lax.broadcasted_iota(jnp.int32, (1, ATTN_W), 1)
        head = lane < dh
        cv = c_ref[...]
        qf, kf, vf = (r[...].astype(F32) for r in (q_ref, k_ref, v_ref))
        for e in range(2):
            c1, c2, c3 = _split3(jnp.sum(jnp.where(lane == 2 * pair + e, cv, 0.0), axis=1, keepdims=True))
            qe, ke, ve = (pltpu.roll(t, dh, 1) for t in (qf, kf, vf)) if e else (qf, kf, vf)
            qa_ref[e] = jnp.where(head, qe * scale,
                                  _aug_lanes(lane, dh + _AUG_C, (c1, c2, c3), (dh + _AUG_ONE,))).astype(BF16)
            ka_ref[e] = jnp.where(head, ke, _aug_lanes(lane, dh + _AUG_ONE, (-c1, -c2, -c3),
                                                       (dh + _AUG_C, dh + _AUG_LSE))).astype(BF16)
            va_ref[e] = jnp.where(head, ve, _aug_lanes(lane, dh, (), (dh,))).astype(BF16)

    at, _, _ = _pair_specs(blk, l // blk)
    cols = lambda c0: pl.BlockSpec((None, blk, ATTN_W), lambda bi, p, i, c0=c0: (bi, i, c0 // ATTN_W + p))
    return _pcall(body, name=name, grid=(b, nh // 2, l // blk),
                  in_specs=[cols(col0[0]), cols(col0[1]), cols(col0[2]),
                            pl.BlockSpec((None, blk, SMALL_W), lambda bi, p, i: (bi, i, 0))],
                  out_specs=[at(ATTN_W)] * 3, out_shape=[_sds((b, nh, l, ATTN_W), BF16)] * 3,
                  compiler_params=_cparams(("parallel", "parallel", "parallel")))(pm3, pm3, pm3, cum)


def _attn_prep_bwd(dy3, y3, qa, lse, name):
    b, nh, l, _ = qa.shape
    dh = ATTN_HEAD_DIM
    blk = _prep_rows(l)

    def body(dy_ref, y_ref, qa_ref, lse_ref, qa2_ref, doa_ref):
        lane = lax.broadcasted_iota(jnp.int32, (1, ATTN_W), 1)
        dyf = dy_ref[...].astype(F32)
        prod = dyf * y_ref[...].astype(F32)
        for e in range(2):
            mine = (lane >= dh) if e else (lane < dh)
            d1, d2, d3 = _split3(jnp.sum(jnp.where(mine, prod, 0.0), axis=1, keepdims=True))
            l1, l2, l3 = _split3(lse_ref[e])
            dye = pltpu.roll(dyf, dh, 1) if e else dyf
            doa_ref[e] = jnp.where(lane < dh, dye, _aug_lanes(lane, dh, (-d1, -d2, -d3))).astype(BF16)
            qa2 = qa_ref[e].astype(F32)
            for k, lv in enumerate((l1, l2, l3)):
                qa2 = jnp.where(lane == dh + _AUG_LSE + k, -lv, qa2)
            qa2_ref[e] = qa2.astype(BF16)

    at, _, rows = _pair_specs(blk, l // blk)
    return _pcall(body, name=name, grid=(b, nh // 2, l // blk), in_specs=[rows, rows, at(ATTN_W), at(1)],
                  out_specs=[at(ATTN_W)] * 2, out_shape=[_sds(qa.shape, BF16)] * 2,
                  compiler_params=_cparams(("parallel", "parallel", "parallel")))(dy3, y3, qa, lse)


def _flash_fwd(qa, ka, va, d_model, name):
    b, h, l, w = qa.shape
    dh = ATTN_HEAD_DIM
    blk = _attn_blk(l)
    nb = l // blk
    hps = _FWD_HEADS_PER_STEP if h % _FWD_HEADS_PER_STEP == 0 else 2
    kr, vr = ka.reshape(b, h, nb, blk, w), va.reshape(b, h, nb, blk, w)

    def body(q_ref, k_ref, v_ref, o_ref, lse_ref):
        i = pl.program_id(2)
        row = lax.broadcasted_iota(jnp.int32, (blk, blk), 0)
        col = lax.broadcasted_iota(jnp.int32, (blk, blk), 1)

        def scores(e, j):
            return _dot_nt(q_ref[e], k_ref[e, j])

        def consume(e, j, s, m, acc):
            mn = jnp.maximum(m, jnp.max(s, axis=1, keepdims=True))
            return mn, jnp.exp(m - mn) * acc + _dot(jnp.exp(s - mn).astype(BF16), v_ref[e, j])

        def step(j, carry):
            out = []
            for e in range(hps):
                m, acc, s = carry[3 * e:3 * e + 3]
                s_next = scores(e, j + 1)
                out += [*consume(e, j, s, m, acc), s_next]
            return tuple(out)

        init = tuple(t for e in range(hps)
                     for t in (jnp.full((blk, 1), NEG, F32), jnp.zeros((blk, w), F32), scores(e, 0)))
        carry = lax.fori_loop(0, i, step, init)
        lane = lax.broadcasted_iota(jnp.int32, (1, w), 1)
        outs = []
        for e in range(hps):
            m, acc = consume(e, i, jnp.where(col <= row, carry[3 * e + 2], NEG), carry[3 * e], carry[3 * e + 1])
            lsum = acc[:, dh:dh + 1]
            lse_ref[e] = m + jnp.log(lsum)
            outs.append(acc / lsum)
        for pr in range(hps // 2):
            o_ref[:, pr * w:(pr + 1) * w] = jnp.where(lane < dh, outs[2 * pr],
                                                      pltpu.roll(outs[2 * pr + 1], dh, 1)).astype(o_ref.dtype)

    at = lambda ww: pl.BlockSpec((None, hps, blk, ww), lambda bi, p, i: (bi, p, i, 0))
    whole = pl.BlockSpec((None, hps, nb, blk, w), lambda bi, p, i: (bi, p, 0, 0, 0))
    rows = pl.BlockSpec((None, blk, hps * dh), lambda bi, p, i: (bi, i, p))
    return _pcall(body, name=name, grid=(b, h // hps, nb), in_specs=[at(w), whole, whole],
                  out_specs=[rows, at(1)], out_shape=[_sds((b, l, d_model), BF16), _sds((b, h, l, 1), F32)],
                  compiler_params=_cparams(("parallel", "parallel", "parallel")))(qa, kr, vr)


def _flash_bwd(qa, ka, va, doa, d_model, name):
    b, h, l, w = qa.shape
    dh = ATTN_HEAD_DIM
    blk = _attn_blk(l)
    nb = l // blk
    scale = dh ** -0.5
    r5 = lambda t: t.reshape(b, h, nb, blk, w)

    def body(k_ref, v_ref, q_ref, do_ref, dq_ref, dk_ref, dv_ref, dcq_ref, dck_ref, dq_acc):
        j = pl.program_id(2)
        row = lax.broadcasted_iota(jnp.int32, (blk, blk), 0)
        col = lax.broadcasted_iota(jnp.int32, (blk, blk), 1)
        lane = lax.broadcasted_iota(jnp.int32, (1, w), 1)

        @pl.when(j == 0)
        def _():
            dq_acc[...] = jnp.zeros_like(dq_acc)

        def contrib(i, masked, carry):
            out = []
            for e in range(2):
                qv, dov = q_ref[e, i], do_ref[e, i]
                p = jnp.exp(_dot_nt(qv, k_ref[e]))
                if masked:
                    p = jnp.where(col <= row, p, 0.0)
                ds = (p * _dot_nt(dov, v_ref[e])).astype(BF16)
                dq_acc[e, i] += _dot(ds, k_ref[e])
                out += [carry[2 * e] + _dot_tn(ds, qv), carry[2 * e + 1] + _dot_tn(p.astype(BF16), dov)]
            return tuple(out)

        zero = (jnp.zeros((blk, w), F32),) * 4
        dk0, dv0, dk1, dv1 = lax.fori_loop(j + 1, nb, lambda i, c: contrib(i, False, c), contrib(j, True, zero))
        dk_ref[...] = jnp.where(lane < dh, dk0, pltpu.roll(dk1, dh, 1)).astype(dk_ref.dtype)
        dv_ref[...] = jnp.where(lane < dh, dv0, pltpu.roll(dv1, dh, 1)).astype(dv_ref.dtype)
        for e, dk in enumerate((dk0, dk1)):
            dck_ref[e] = jnp.sum(jnp.where(lane == dh + _AUG_ONE, dk, 0.0), axis=1, keepdims=True)

        @pl.when(j == nb - 1)
        def _():
            for ib in range(nb):
                rs = pl.ds(ib * blk, blk)
                dq0, dq1 = dq_acc[0, ib], dq_acc[1, ib]
                dq_ref[rs, :] = (jnp.where(lane < dh, dq0, pltpu.roll(dq1, dh, 1)) * scale).astype(dq_ref.dtype)
                for e, dq in enumerate((dq0, dq1)):
                    dcq_ref[e, rs, :] = jnp.sum(jnp.where(lane == dh + _AUG_C, dq, 0.0), axis=1, keepdims=True)

    at, whole, rows = _pair_specs(blk, nb)
    seq_rows = pl.BlockSpec((None, l, ATTN_W), lambda bi, p, j: (bi, 0, p))
    seq_col = pl.BlockSpec((None, 2, l, 1), lambda bi, p, j: (bi, p, 0, 0))
    act = _sds((b, l, d_model), BF16)
    col1 = _sds((b, h, l, 1), F32)
    return _pcall(body, name=name, grid=(b, h // 2, nb), in_specs=[at(w), at(w), whole, whole],
                  out_specs=[seq_rows, rows, rows, seq_col, at(1)], out_shape=[act, act, act, col1, col1],
                  scratch_shapes=[pltpu.VMEM((2, nb, blk, w), F32)],
                  compiler_params=_cparams(("parallel", "parallel", "arbitrary")))(ka, va, r5(qa), r5(doa))


def _ssd_dims(d_ssd):
    heads = d_ssd // SSD_HEAD_DIM
    return heads, heads // SSD_GROUPS, d_ssd // SSD_GROUPS


def _ssd_specs(l, ds, seq_map):
    q = SSD_CHUNK
    gn = SSD_GROUPS * SSD_STATE
    row3 = lambda w, cb: pl.BlockSpec((None, q, w), lambda i, c, cb=cb: (i, seq_map(c), cb))
    return dict(
        xs=row3(ds, 0), bm=row3(gn, ds // gn), cm=row3(gn, ds // gn + 1), z=row3(ds, 0), dt=row3(SMALL_W, 0),
        da=row3(SMALL_W, 0), dat=pl.BlockSpec((None, SMALL_W, q), lambda i, c: (i, 0, seq_map(c))),
        e=pl.BlockSpec((SMALL_W, ds), lambda i, c: (0, 0)), et=pl.BlockSpec((ds, SMALL_W), lambda i, c: (0, 0)),
        vec=pl.BlockSpec((1, ds), lambda i, c: (0, 0)), vec128=pl.BlockSpec((1, SMALL_W), lambda i, c: (0, 0)),
        hin=pl.BlockSpec((None, None, SSD_STATE, ds), lambda i, c: (i, seq_map(c), 0, 0)))


def _ssd_common(da, dat, dt, e_mat, xs):
    q = SSD_CHUNK
    row = lax.broadcasted_iota(jnp.int32, (q, q), 0)
    col = lax.broadcasted_iota(jnp.int32, (q, q), 1)
    lower = row >= col
    cs = _dot(lower.astype(F32), da, hi=True)
    cst = _dot(dat, (row <= col).astype(F32), hi=True)
    dtx = _dot(dt, e_mat, hi=True)
    csx = _dot(cs, e_mat, hi=True)
    rowx = lax.broadcasted_iota(jnp.int32, csx.shape, 0)
    totx = jnp.sum(jnp.where(rowx == q - 1, csx, 0.0), axis=0, keepdims=True)
    xf = xs.astype(F32)
    return lower, cs, cst, dtx, csx, totx, xf, xf * dtx


def _ssd_fwd(xbc, z, dt, da, dat, e_mat, dx, nw, name):
    b, l, _ = xbc.shape
    ds = z.shape[2]
    heads, hpg, gw = _ssd_dims(ds)
    q, n = SSD_CHUNK, SSD_STATE
    nc = l // q
    hcol0 = ATTN_HEADS

    def body(xs_ref, bm_ref, cm_ref, z_ref, dt_ref, da_ref, dat_ref, e_ref, dx_ref, nw_ref, y_ref, yraw_ref, hin_ref,
             hst, ydiag):
        c = pl.program_id(1)

        @pl.when(c == 0)
        def _():
            hst[...] = jnp.zeros_like(hst)

        hin = hst[...]
        hin_ref[...] = hin
        lower, cs, cst, dtx, csx, totx, xf, xdt = _ssd_common(da_ref[...], dat_ref[...], dt_ref[...], e_ref[...],
                                                               xs_ref[...])
        bm, cm = bm_ref[...], cm_ref[...]
        dec_end = jnp.exp(totx - csx)
        for g in range(SSD_GROUPS):
            gs = slice(g * gw, (g + 1) * gw)
            bg, cg = bm[:, g * n:(g + 1) * n], cm[:, g * n:(g + 1) * n]
            cb = _dot_nt(cg, bg)
            for e in range(hpg):
                hh = g * hpg + e
                cc = hcol0 + hh
                lm = jnp.exp(jnp.where(lower, cs[:, cc:cc + 1] - cst[cc:cc + 1, :], NEG))
                hs = slice(hh * SSD_HEAD_DIM, (hh + 1) * SSD_HEAD_DIM)
                ydiag[:, hs] = _dot((cb * lm).astype(BF16), xdt[:, hs].astype(BF16))
            sg = _dot_tn(bg, (xdt[:, gs] * dec_end[:, gs]).astype(BF16))
            hst[:, gs] = jnp.exp(totx[:, gs]) * hin[:, gs] + sg
            ydiag[:, gs] += _dot(cg, hin[:, gs].astype(BF16)) * jnp.exp(csx[:, gs])
        yraw = ydiag[...] + dx_ref[...] * xf
        yraw_ref[...] = yraw.astype(yraw_ref.dtype)
        yg = yraw * _silu(z_ref[...].astype(F32))
        nwv = nw_ref[...]
        for g in range(SSD_GROUPS):
            gs = slice(g * gw, (g + 1) * gw)
            r = lax.rsqrt(jnp.mean(yg[:, gs] * yg[:, gs], axis=1, keepdims=True) + NORM_EPS)
            y_ref[:, gs] = (yg[:, gs] * r * nwv[:, gs]).astype(y_ref.dtype)

    sp = _ssd_specs(l, ds, lambda c: c)
    return _pcall(body, name=name, grid=(b, nc),
                  in_specs=[sp['xs'], sp['bm'], sp['cm'], sp['z'], sp['dt'], sp['da'], sp['dat'], sp['e'], sp['vec'],
                            sp['vec']],
                  out_specs=[sp['z'], sp['z'], sp['hin']],
                  out_shape=[_sds((b, l, ds), BF16), _sds((b, l, ds), BF16), _sds((b, nc, n, ds), F32)],
                  scratch_shapes=[pltpu.VMEM((n, ds), F32), pltpu.VMEM((q, ds), F32)],
                  compiler_params=_cparams(("parallel", "arbitrary")))(xbc, xbc, xbc, z, dt, da, dat, e_mat, dx, nw)


def _ssd_bwd(xbc, z, dt, da, dat, e_mat, et_mat, dx, nw, a128, yraw, hin, dy, name):
    b, l, dxw = xbc.shape
    ds = z.shape[2]
    heads, hpg, gw = _ssd_dims(ds)
    q, n = SSD_CHUNK, SSD_STATE
    gn = SSD_GROUPS * n
    nc = l // q
    hcol0 = ATTN_HEADS

    def body(xs_ref, bm_ref, cm_ref, z_ref, dt_ref, da_ref, dat_ref, e_ref, et_ref, dx_ref, nw_ref, a_ref, yraw_ref,
             hin_ref, dy_ref, dxs_ref, dbm_ref, dcm_ref, dz_ref, ddt_ref, dd_ref, dnw_ref, dap_ref, dhs, dxdt, dcsx,
             dtotx):
        i, c = pl.program_id(0), pl.program_id(1)

        @pl.when(c == 0)
        def _():
            dhs[...] = jnp.zeros_like(dhs)

        @pl.when((i == 0) & (c == 0))
        def _():
            dd_ref[...] = jnp.zeros_like(dd_ref)
            dnw_ref[...] = jnp.zeros_like(dnw_ref)
            dap_ref[...] = jnp.zeros_like(dap_ref)

        dtv = dt_ref[...]
        lower, cs, cst, dtx, csx, totx, xf, xdt = _ssd_common(da_ref[...], dat_ref[...], dtv, e_ref[...], xs_ref[...])
        upper = jnp.logical_not(lower) | (lax.broadcasted_iota(jnp.int32, (q, q), 0)
                                          == lax.broadcasted_iota(jnp.int32, (q, q), 1))
        bm, cm = bm_ref[...], cm_ref[...]
        ecs, dec_end, etot = jnp.exp(csx), jnp.exp(totx - csx), jnp.exp(totx)
        yraw = yraw_ref[...].astype(F32)
        zv = z_ref[...].astype(F32)
        sz = _silu(zv)
        yg = yraw * sz
        dyn_ = dy_ref[...].astype(F32)
        nwv = nw_ref[...]
        dygs, dnws = [], []
        for g in range(SSD_GROUPS):
            gs = slice(g * gw, (g + 1) * gw)
            r = lax.rsqrt(jnp.mean(yg[:, gs] * yg[:, gs], axis=1, keepdims=True) + NORM_EPS)
            yn = yg[:, gs] * r
            dn = dyn_[:, gs] * nwv[:, gs]
            dnws.append(jnp.sum(dyn_[:, gs] * yn, axis=0, keepdims=True))
            dygs.append(r * (dn - yn * jnp.mean(dn * yn, axis=1, keepdims=True)))
        dyg = jnp.concatenate(dygs, axis=1)
        dnw_ref[...] += jnp.concatenate(dnws, axis=1)
        dz_ref[...] = (dyg * yraw * _dsilu(zv)).astype(dz_ref.dtype)
        dyv = dyg * sz
        dd_ref[...] += jnp.sum(dyv * xf, axis=0, keepdims=True)
        hin, dh = hin_ref[...], dhs[...]
        lane128 = lax.broadcasted_iota(jnp.int32, (1, SMALL_W), 1)
        dcs = jnp.zeros((q, SMALL_W), F32)
        for g in range(SSD_GROUPS):
            gs = slice(g * gw, (g + 1) * gw)
            bg, cg = bm[:, g * n:(g + 1) * n], cm[:, g * n:(g + 1) * n]
            hg, dhg = hin[:, gs], dh[:, gs]
            hgb, dsb = hg.astype(BF16), dhg.astype(BF16)
            yoff = _dot(cg, hgb) * ecs[:, gs]
            dch = (dyv[:, gs] * ecs[:, gs]).astype(BF16)
            dcg = _dot_nt(dch, hgb)
            dhs[:, gs] = _dot_tn(cg, dch) + etot[:, gs] * dhg
            zg = xdt[:, gs] * dec_end[:, gs]
            dzz = _dot(bg, dsb)
            dbg = _dot_nt(zg.astype(BF16), dsb)
            dxdt_g = dzz * dec_end[:, gs]
            w_end = dzz * zg
            dtotx[:, gs] = jnp.sum(dhg * hg, axis=0, keepdims=True) * etot[:, gs] + jnp.sum(w_end, axis=0, keepdims=True)
            dcsx[:, gs] = dyv[:, gs] * yoff - w_end
            cb, cbt = _dot_nt(cg, bg), _dot_nt(bg, cg)
            dgm = jnp.zeros((q, q), F32)
            for e in range(hpg):
                hh = g * hpg + e
                cc = hcol0 + hh
                ccol, crow = cs[:, cc:cc + 1], cst[cc:cc + 1, :]
                lm = jnp.exp(jnp.where(lower, ccol - crow, NEG))
                lmt = jnp.exp(jnp.where(upper, crow - ccol, NEG))
                mm, mt = cb * lm, cbt * lmt
                hs = slice(hh * SSD_HEAD_DIM, (hh + 1) * SSD_HEAD_DIM)
                dye, xe = dyv[:, hs].astype(BF16), xdt[:, hs].astype(BF16)
                dm, dmt = _dot_nt(dye, xe), _dot_nt(xe, dye)
                dxdt[:, hs] = dxdt_g[:, e * SSD_HEAD_DIM:(e + 1) * SSD_HEAD_DIM] + _dot(mt.astype(BF16), dye)
                dgm = dgm + dm * lm
                rs = jnp.sum(dm * mm, axis=1, keepdims=True) - jnp.sum(dmt * mt, axis=1, keepdims=True)
                dcs = dcs + rs * (lane128 == cc).astype(F32)
            dgb = dgm.astype(BF16)
            dcm_ref[:, g * n:(g + 1) * n] = (dcg + _dot(dgb, bg)).astype(dcm_ref.dtype)
            dbm_ref[:, g * n:(g + 1) * n] = (dbg + _dot_tn(dgb, cg)).astype(dbm_ref.dtype)
        dxd = dxdt[...]
        dxs_ref[...] = (dx_ref[...] * dyv + dxd * dtx).astype(dxs_ref.dtype)
        et = et_ref[...]
        ddt = _dot(dxd * xf, et, hi=True)
        dtot128 = _dot(jnp.broadcast_to(dtotx[...], (8, ds)), et, hi=True)[0:1, :]
        row128 = lax.broadcasted_iota(jnp.int32, (q, SMALL_W), 0)
        dcs = dcs + _dot(dcsx[...], et, hi=True) + jnp.where(row128 == q - 1, dtot128, 0.0)
        dda = _dot(upper.astype(F32), dcs, hi=True)
        ddt_ref[...] = ddt + dda * a_ref[...]
        dap_ref[...] += jnp.sum(dda * dtv, axis=0, keepdims=True)

    rev = lambda c: nc - 1 - c
    sp = _ssd_specs(l, ds, rev)
    row3 = lambda w: pl.BlockSpec((None, q, w), lambda i, c: (i, rev(c), 0))
    acc = lambda w: pl.BlockSpec((1, w), lambda i, c: (0, 0))
    return _pcall(body, name=name, grid=(b, nc),
                  in_specs=[sp['xs'], sp['bm'], sp['cm'], sp['z'], sp['dt'], sp['da'], sp['dat'], sp['e'], sp['et'],
                            sp['vec'], sp['vec'], sp['vec128'], sp['z'], sp['hin'], sp['z']],
                  out_specs=[row3(ds), row3(gn), row3(gn), row3(ds), row3(SMALL_W), acc(ds), acc(ds), acc(SMALL_W)],
                  out_shape=[_sds((b, l, ds), BF16), _sds((b, l, gn), BF16), _sds((b, l, gn), BF16), _sds((b, l, ds), BF16),
                             _sds((b, l, SMALL_W), F32), _sds((1, ds), F32), _sds((1, ds), F32), _sds((1, SMALL_W), F32)],
                  scratch_shapes=[pltpu.VMEM((n, ds), F32), pltpu.VMEM((q, ds), F32), pltpu.VMEM((q, ds), F32),
                                  pltpu.VMEM((1, ds), F32)],
                  compiler_params=_cparams(("arbitrary", "arbitrary")))(
                      xbc, xbc, xbc, z, dt, da, dat, e_mat, et_mat, dx, nw, a128, yraw, hin, dy)


_GROUP_SIZE = {'c': 2, 'xy': 4, 'xyc': 8}
_LOCAL_SPLIT = 16


def _exchange(src, group, scatter, name, nsplit=1, copy_own=True):
    n = _GROUP_SIZE[group]
    rows, width = src.shape[-2:]
    assert src.ndim == (3 if scatter else 2)
    while rows % (8 * nsplit):
        nsplit //= 2
    crow = rows // nsplit
    nlocal = _LOCAL_SPLIT
    while rows % (8 * nlocal):
        nlocal //= 2
    lrow = rows // nlocal

    def body(src_ref, out_ref, send_sems, recv_sems, local_sems):
        x, y, c = lax.axis_index("x"), lax.axis_index("y"), lax.axis_index("c")
        if group == 'c':
            rank = c
            dev = lambda r: (x, y, r)
        elif group == 'xy':
            rank = 2 * x + y
            dev = lambda r: (r // 2, r % 2, c)
        else:
            rank = 4 * x + 2 * y + c
            dev = lambda r: (r // 4, (r // 2) % 2, r % 2)

        def mine_for(r, ck):
            piece = src_ref.at[r] if scatter else src_ref
            return piece.at[pl.ds(ck * crow, crow)]

        def copy(k, ck, pr, dst_rank):
            return pltpu.make_async_remote_copy(
                src_ref=mine_for(pr, ck), dst_ref=out_ref.at[dst_rank].at[pl.ds(ck * crow, crow)],
                send_sem=send_sems.at[k * nsplit + ck], recv_sem=recv_sems.at[k * nsplit + ck], device_id=dev(pr),
                device_id_type=pl.DeviceIdType.MESH)

        locals_ = []
        if copy_own:
            own = src_ref.at[rank] if scatter else src_ref
            for ck in range(nlocal):
                rs = pl.ds(ck * lrow, lrow)
                locals_.append(pltpu.make_async_copy(own.at[rs], out_ref.at[rank].at[rs], local_sems.at[ck]))
                locals_[-1].start()
        peers = [jnp.bitwise_xor(rank, k + 1) for k in range(n - 1)]
        sends = [copy(k, ck, pr, rank) for ck in range(nsplit) for k, pr in enumerate(peers)]
        for cp in sends:
            cp.start()
        for ck in range(nsplit):
            for k, pr in enumerate(peers):
                copy(k, ck, pr, pr).wait_recv()
        for cp in sends:
            cp.wait_send()
        for cp in locals_:
            cp.wait()

    return _pcall(body, name=name, in_specs=[pl.BlockSpec(memory_space=pl.ANY)],
                  out_specs=pl.BlockSpec(memory_space=pl.ANY), out_shape=_sds((n, rows, width), src.dtype),
                  scratch_shapes=[pltpu.SemaphoreType.DMA(((n - 1) * nsplit,)),
                                  pltpu.SemaphoreType.DMA(((n - 1) * nsplit,)),
                                  pltpu.SemaphoreType.DMA((nlocal,))])(src)


def _exchange_multi(srcs, group, scatter, name, single=False, min_copies=16):
    n = _GROUP_SIZE[group]
    assert not single or n == 2
    na = len(srcs)
    shapes = [tuple(s.shape[-2:]) for s in srcs]
    want = max(1, -(-min_copies // (na * (n - 1))))
    splits = []
    for (rows, _), s in zip(shapes, srcs):
        quant = 8 * (4 // s.dtype.itemsize)
        k = want
        while k > 1 and rows % (quant * k):
            k -= 1
        splits.append(k)
    offs = [int(v) for v in np.cumsum([0] + [(n - 1) * k for k in splits])]

    def body(*refs):
        src_refs, out_refs = refs[:na], refs[na:2 * na]
        send_sems, recv_sems = refs[2 * na:]
        x, y, c = lax.axis_index("x"), lax.axis_index("y"), lax.axis_index("c")
        if group == 'c':
            rank = c
            dev = lambda r: (x, y, r)
        elif group == 'xy':
            rank = 2 * x + y
            dev = lambda r: (r // 2, r % 2, c)
        else:
            rank = 4 * x + 2 * y + c
            dev = lambda r: (r // 4, (r // 2) % 2, r % 2)
        peers = [jnp.bitwise_xor(rank, k + 1) for k in range(n - 1)]

        def copy(a, k, ck, dst_rank):
            crow = shapes[a][0] // splits[a]
            rs = pl.ds(ck * crow, crow)
            piece = src_refs[a].at[peers[k]] if scatter else src_refs[a]
            dst = out_refs[a] if single else out_refs[a].at[dst_rank]
            sem = offs[a] + k * splits[a] + ck
            return pltpu.make_async_remote_copy(src_ref=piece.at[rs], dst_ref=dst.at[rs], send_sem=send_sems.at[sem],
                                                recv_sem=recv_sems.at[sem], device_id=dev(peers[k]),
                                                device_id_type=pl.DeviceIdType.MESH)

        todo = [(a, k, ck) for a in range(na) for ck in range(splits[a]) for k in range(n - 1)]
        sends = [copy(a, k, ck, rank) for a, k, ck in todo]
        for cp in sends:
            cp.start()
        for a, k, ck in todo:
            copy(a, k, ck, peers[k]).wait_recv()
        for cp in sends:
            cp.wait_send()

    any_spec = pl.BlockSpec(memory_space=pl.ANY)
    out_shape = [_sds(sh if single else (n,) + sh, s.dtype) for sh, s in zip(shapes, srcs)]
    return _pcall(body, name=name, in_specs=[any_spec] * na, out_specs=[any_spec] * na, out_shape=out_shape,
                  scratch_shapes=[pltpu.SemaphoreType.DMA((offs[-1],)), pltpu.SemaphoreType.DMA((offs[-1],))])(*srcs)


def _sum_slots(arr, out_dtype, name):
    n, rows, cols = arr.shape
    tm = _pick(rows, [c for c in (384, 256, 128, 64, 32, 16, 8) if c * cols <= _ROWWISE_TILE_ELEMS or c == 8])

    def body(*refs):
        acc = refs[0][...].astype(F32)
        for r in refs[1:n]:
            acc = acc + r[...].astype(F32)
        refs[n][...] = acc.astype(refs[n].dtype)

    return _pcall(body, name=name, grid=(rows // tm,),
                  in_specs=[pl.BlockSpec((None, tm, cols), lambda i, j=j: (j, i, 0)) for j in range(n)],
                  out_specs=pl.BlockSpec((tm, cols), lambda i: (i, 0)), out_shape=_sds((rows, cols), out_dtype),
                  compiler_params=_cparams(("parallel",)))(*([arr] * n))


def _dims():
    d = D_MODEL
    h = ATTN_HEADS
    d_ssd = d
    d_xbc = d_ssd + 2 * SSD_GROUPS * SSD_STATE
    sizes = (d, d, d, h, d_ssd, d_xbc, d_ssd // SSD_HEAD_DIM, d, d, 3 * d)
    return d, h, d_ssd, d_xbc, sizes


def _w_in_split(w):
    d, h, d_ssd, d_xbc, sizes = _dims()
    off = np.concatenate([[0], np.cumsum(sizes)])
    seg = lambda i: w[..., off[i]:off[i + 1]]
    main = jnp.concatenate([seg(0), seg(1), seg(2), seg(4), seg(5), seg(7), seg(8), seg(9)], axis=-1)
    pad = jnp.zeros(w.shape[:-1] + (SMALL_W - sizes[3] - sizes[6],), w.dtype)
    small = jnp.concatenate([seg(3), seg(6), pad], axis=-1)
    return main, small


def _w_in_merge(main, small):
    d, h, d_ssd, d_xbc, sizes = _dims()
    order = (0, 1, 2, 4, 5, 7, 8, 9)
    moff = np.concatenate([[0], np.cumsum([sizes[i] for i in order])])
    pieces = {i: main[..., moff[j]:moff[j + 1]] for j, i in enumerate(order)}
    pieces[3] = small[..., :sizes[3]]
    pieces[6] = small[..., sizes[3]:sizes[3] + sizes[6]]
    return jnp.concatenate([pieces[i] for i in range(10)], axis=-1)


def _main_offsets():
    d, h, d_ssd, d_xbc, sizes = _dims()
    names = ('q', 'k', 'v', 'z', 'xbc', 'xr', 'gate', 'merge')
    widths = (d, d, d, d_ssd, d_xbc, d, d, 3 * d)
    off = np.concatenate([[0], np.cumsum(widths)])
    return {nm: (int(off[i]), int(off[i + 1])) for i, nm in enumerate(names)}


def _block_diag(w):
    nb, s, _ = w.shape
    eye = jnp.eye(nb, dtype=w.dtype)
    return (eye[:, None, :, None] * w[:, :, None, :]).reshape(nb * s, nb * s)


def _diag_blocks(wd, nb):
    s = wd.shape[0] // nb
    return jnp.stack([wd[i * s:(i + 1) * s, i * s:(i + 1) * s] for i in range(nb)])


def _vec128(*parts):
    v = jnp.concatenate([p.astype(F32) for p in parts])
    return jnp.pad(v, (0, SMALL_W - v.shape[0]))[None, :]


def _ffn_fwd(h, gnorm, w, tag):
    xn = _rms_fwd(h, gnorm[None, :], f"{tag}_norm")
    g, u, act = _mm_swiglu(xn, w['wg'], w['wu'], f"{tag}_gu")
    out = _mm_nn(act, w['wd'], F32, res=h, alpha=0.5, name=f"{tag}_down")
    return out, (h, xn, g, u, act)


def _ffn_bwd(dout, saved, gnorm, w, tag):
    h, xn, g, u, act = saved
    dg, du = _mm_dswiglu(dout, w['wd_t'], g, u, 0.5, f"{tag}_dgu")
    dwd = _mm_tn(act, dout, alpha=0.5, name=f"{tag}_dwd")
    dwgu = jnp.concatenate([_mm_tn(xn, dg, name=f"{tag}_dwg"), _mm_tn(xn, du, name=f"{tag}_dwu")], axis=1)
    dxn = _mm_nn(dg, w['wg_t'], F32, name=f"{tag}_dxn_g")
    dxn = _mm_nn(du, w['wu_t'], F32, res=dxn, name=f"{tag}_dxn_u")
    dh, dgn = _rms_bwd(h, dxn, dout, gnorm[None, :], f"{tag}_dnorm")
    return dh, dgn[0], dwgu, dwd


def _mixer_fwd(h, p, b, l):
    d, nh, d_ssd, d_xbc, sizes = _dims()
    t = b * l
    off = _main_offsets()
    xn = _rms_fwd(h, p['mix_norm'][None, :], "mix_norm")
    pm = _mm_nn(xn, p['w_main'], BF16, name="mix_in_main")
    ps = _mm_nn(xn, p['w_small'], F32, name="mix_in_small")
    col = lambda nm: pm[:, off[nm][0]:off[nm][1]]
    heads_ssd = d_ssd // SSD_HEAD_DIM
    a_neg = -jnp.exp(p['ssd_a_log'])
    fb = _vec128(p['fox_forget_bias'])
    dtb = _vec128(jnp.zeros((nh,), F32), p['ssd_dt_bias'])
    a128 = _vec128(jnp.zeros((nh,), F32), a_neg)

    def prep(_, v, fbv, dtbv, av):
        lane = lax.broadcasted_iota(jnp.int32, (1, SMALL_W), 1)
        logf = jnp.where(lane < nh, -_softplus(-(v + fbv)), 0.0)
        dtv = jnp.where((lane >= nh) & (lane < nh + heads_ssd), _softplus(v + dtbv), 0.0)
        return logf, dtv, dtv * av
    logf, dt, da = _rowwise(prep, [ps], [(SMALL_W, F32)] * 3, bcast=[fb, dtb, a128], name="mix_prep")

    cum = _cumsum_seq(logf.reshape(b, l, SMALL_W), False, "fox_cumsum")
    qa, ka, va = _attn_prep(pm.reshape(b, l, -1), cum, (off['q'][0], off['k'][0], off['v'][0]), "fox_prep")
    y_a3, lse = _flash_fwd(qa, ka, va, d, "fox_fwd")
    y_a = y_a3.reshape(t, d)

    xbc = col('xbc').reshape(b, l, d_xbc)
    pre_b, xbc_act = _conv_fwd(xbc, p['ssd_conv_w'], p['ssd_conv_b'][None, :], BF16, "ssd_conv", with_silu=True)
    z = col('z').reshape(b, l, d_ssd)
    dt3, da3 = dt.reshape(b, l, SMALL_W), da.reshape(b, l, SMALL_W)
    dat3 = da3.transpose(0, 2, 1)
    e_mat = _expand_matrix(nh, heads_ssd)
    dx = jnp.repeat(p['ssd_d'], SSD_HEAD_DIM)[None, :]
    nw = p['ssd_norm'][None, :]
    y_b3, yraw, hin = _ssd_fwd(xbc_act, z, dt3, da3, dat3, e_mat, dx, nw, "ssd_fwd")
    y_b = y_b3.reshape(t, d_ssd)

    xr = col('xr').reshape(b, l, d)
    xc = _conv_fwd(xr, p['lru_conv_w'], p['lru_conv_b'][None, :], F32, "lru_conv").reshape(t, d)
    pre_ri = _mm_nn(xc, p['lru_w_ri'], F32, name="lru_gates")
    lvec = (p['lru_b_a'][None, :], p['lru_b_x'][None, :], p['lru_lambda'][None, :])
    a_l, u_l = _rowwise(_lru_point_fwd, [pre_ri, xc], [(d, F32)] * 2, bcast=lvec, name="lru_point", period=l)
    gate = col('gate')
    hs, y_c = _linear_scan(a_l.reshape(b, l, d), u_l.reshape(b, l, d), False, "lru_scan", gate=gate.reshape(b, l, d))
    hs, y_c = hs.reshape(t, d), y_c.reshape(t, d)

    ba = _mm_nn(y_a, p['w_branch_attn'], BF16, name="branch_attn")
    bb = _mm_nn(y_b, p['w_branch_ssd'], BF16, name="branch_ssd")
    bc = _mm_nn(y_c, p['w_branch_lru'], BF16, name="branch_lru")
    mg = col('merge')
    mixed = _merge_fwd(mg, ba, bb, bc, "merge")
    out = _mm_nn(mixed, p['w_out'], F32, res=h, name="mix_out")
    saved = dict(h=h, xn=xn, ps=ps, fb=fb, dtb=dtb, a128=a128, qa=qa, ka=ka, va=va, lse=lse,
                 xbc=xbc, pre_b=pre_b, xbc_act=xbc_act, z=z, dt3=dt3, da3=da3, dat3=dat3, e_mat=e_mat, dx=dx, nw=nw,
                 yraw=yraw, hin=hin, xr=xr, xc=xc, pre_ri=pre_ri, lvec=lvec, a_l=a_l, hs=hs, gate=gate, y_a=y_a, y_b=y_b,
                 y_c=y_c, ba=ba, bb=bb, bc=bc, mg=mg, mixed=mixed)
    return out, saved


def _expand_matrix(nh, heads_ssd):
    e = np.zeros((SMALL_W, heads_ssd * SSD_HEAD_DIM), np.float32)
    for hh in range(heads_ssd):
        e[nh + hh, hh * SSD_HEAD_DIM:(hh + 1) * SSD_HEAD_DIM] = 1.0
    return jnp.asarray(e)


def _lru_gates(pre, xc, bav, bxv, lamv, pos):
    d = xc.shape[1]
    r = _sigmoid(pre[:, :d] + bav)
    i = _sigmoid(pre[:, d:] + bxv)
    ls = -_softplus(-lamv)
    la = LRU_C * r * ls
    a = jnp.exp(la)
    mult = jnp.where(pos == 0, 1.0, jnp.sqrt(-_expm1(2.0 * la)))
    return r, i, ls, a, mult


def _lru_point_fwd(pos, pre, xc, bav, bxv, lamv):
    r, i, ls, a, mult = _lru_gates(pre, xc, bav, bxv, lamv, pos)
    return a, mult * (i * xc)


def _lru_point_bwd(pos, g, hprev, pre, xc, bav, bxv, lamv):
    r, i, ls, a, mult = _lru_gates(pre, xc, bav, bxv, lamv, pos)
    da = g * hprev
    di = g * mult * xc
    dxc = g * mult * i
    dmult = jnp.where(pos == 0, 0.0, g * i * xc)
    dla = da * a - dmult * (a * a) / mult
    dpre_r = dla * (LRU_C * ls) * r * (1.0 - r)
    dpre_i = di * i * (1.0 - i)
    dlam = jnp.sum(dla * (LRU_C * r), axis=0, keepdims=True) * _sigmoid(-lamv)
    return (jnp.concatenate([dpre_r, dpre_i], axis=1), dxc, dlam, jnp.sum(dpre_r, axis=0, keepdims=True),
            jnp.sum(dpre_i, axis=0, keepdims=True))


def _mixer_bwd(dout, s, p, b, l):
    d, nh, d_ssd, d_xbc, sizes = _dims()
    t = b * l
    heads_ssd = d_ssd // SSD_HEAD_DIM
    g = {}
    dmixed = _mm_nn(dout, p['w_out_t'], BF16, name="mix_out_dx")
    g['w_out'] = _mm_tn(s['mixed'], dout, name="mix_out_dw")
    dba, dbb, dbc, dmerge = _merge_bwd(s['mg'], s['ba'], s['bb'], s['bc'], dmixed, "merge_bwd")
    g['w_branch_attn'] = _mm_tn(s['y_a'], dba, name="branch_attn_dw")
    g['w_branch_ssd'] = _mm_tn(s['y_b'], dbb, name="branch_ssd_dw")
    g['w_branch_lru'] = _mm_tn(s['y_c'], dbc, name="branch_lru_dw")
    dy_a = _mm_nn(dba, p['w_branch_attn_t'], BF16, name="branch_attn_dx")
    dy_b = _mm_nn(dbb, p['w_branch_ssd_t'], BF16, name="branch_ssd_dx")
    dy_c = _mm_nn(dbc, p['w_branch_lru_t'], F32, name="branch_lru_dx")

    dgate, dhs = _rowwise(lambda _, dv, hv, gv: (dv * hv * _dgelu(gv.astype(F32)), dv * _gelu(gv.astype(F32))),
                          [dy_c, s['hs'], s['gate']], [(d, BF16), (d, F32)], name="lru_out_bwd")
    a3 = s['a_l'].reshape(b, l, d)
    a_next = jnp.concatenate([a3[:, 1:], jnp.zeros((b, 1, d), F32)], axis=1)
    gs = _linear_scan(a_next, dhs.reshape(b, l, d), True, "lru_scan_bwd").reshape(t, d)
    h3 = s['hs'].reshape(b, l, d)
    hprev = jnp.concatenate([jnp.zeros((b, 1, d), F32), h3[:, :-1]], axis=1).reshape(t, d)
    dpre_ri, dxc0, dlam, dba_, dbx_ = _rowwise(_lru_point_bwd, [gs, hprev, s['pre_ri'], s['xc']],
                                               [(2 * d, BF16), (d, F32)], bcast=s['lvec'],
                                               reds=[(1, d)] * 3, name="lru_point_bwd", period=l)
    g['lru_lambda'], g['lru_b_a'], g['lru_b_x'] = dlam[0], dba_[0], dbx_[0]
    dxc = _mm_nn(dpre_ri, p['lru_w_ri_t'], BF16, res=dxc0, name="lru_gates_dx")
    dw_ri = _mm_tn(s['xc'], dpre_ri, name="lru_gates_dw")
    g['lru_w_a'] = _diag_blocks(dw_ri[:, :d], LRU_BLOCKS)
    g['lru_w_x'] = _diag_blocks(dw_ri[:, d:], LRU_BLOCKS)
    dxr, dwl = _conv_bwd(s['xr'], dxc.reshape(b, l, d), p['lru_conv_w'], "lru_conv_bwd")
    g['lru_conv_w'], g['lru_conv_b'] = dwl[:CONV_K], dwl[CONV_K]

    et_mat = s['e_mat'].T
    dxs, dbm, dcm, dz, ddt, dd_l, dnw, dap = _ssd_bwd(s['xbc_act'], s['z'], s['dt3'], s['da3'], s['dat3'], s['e_mat'],
                                                      et_mat, s['dx'], s['nw'], s['a128'], s['yraw'], s['hin'],
                                                      dy_b.reshape(b, l, d_ssd), "ssd_bwd")
    g['ssd_d'] = dd_l.reshape(heads_ssd, SSD_HEAD_DIM).sum(axis=1)
    g['ssd_norm'] = dnw[0]
    g['ssd_a_log'] = dap[0, nh:nh + heads_ssd] * (-jnp.exp(p['ssd_a_log']))
    dxbc_act = jnp.concatenate([dxs, dbm, dcm], axis=2).reshape(t, d_xbc)
    dpre_b = _rowwise(lambda _, dv, pv: dv.astype(F32) * _dsilu(pv.astype(F32)),
                      [dxbc_act, s['pre_b'].reshape(t, d_xbc)], [(d_xbc, BF16)], name="ssd_conv_act_bwd")[0]
    dxbc, dws = _conv_bwd(s['xbc'], dpre_b.reshape(b, l, d_xbc), p['ssd_conv_w'], "ssd_conv_bwd")
    g['ssd_conv_w'], g['ssd_conv_b'] = dws[:CONV_K], dws[CONV_K]

    qa2, doa = _attn_prep_bwd(dy_a.reshape(b, l, d), s['y_a'].reshape(b, l, d), s['qa'], s['lse'], "fox_prep_bwd")
    dq3, dk3, dv3, dcq, dck = _flash_bwd(qa2, s['ka'], s['va'], doa, d, "fox_bwd")
    dcum = jnp.pad((dcq - dck)[..., 0].transpose(0, 2, 1), ((0, 0), (0, 0), (0, SMALL_W - nh)))
    dlogf = _cumsum_seq(dcum, True, "fox_cumsum_bwd").reshape(t, SMALL_W)

    def prep_bwd(_, v, dlf, ddtv, fbv, dtbv):
        a_ = dlf * _sigmoid(-(v + fbv))
        b_ = ddtv * _sigmoid(v + dtbv)
        return a_ + b_, jnp.sum(a_, axis=0, keepdims=True), jnp.sum(b_, axis=0, keepdims=True)
    dps, dfb, ddtb = _rowwise(prep_bwd, [s['ps'], dlogf, ddt.reshape(t, SMALL_W)], [(SMALL_W, F32)],
                              bcast=[s['fb'], s['dtb']], reds=[(1, SMALL_W)] * 2, name="mix_prep_bwd")
    g['fox_forget_bias'] = dfb[0, :nh]
    g['ssd_dt_bias'] = ddtb[0, nh:nh + heads_ssd]

    dpm = jnp.concatenate([dq3.reshape(t, d), dk3.reshape(t, d), dv3.reshape(t, d),
                           dz.reshape(t, d_ssd), dxbc.reshape(t, d_xbc), dxr.reshape(t, d), dgate, dmerge], axis=1)
    dxn = _mm_nn(dps, p['w_small_t'], F32, name="mix_in_small_dx")
    dxn = _mm_nn(dpm, p['w_main_t'], F32, res=dxn, name="mix_in_main_dx")
    g['w_main'] = _mm_tn(s['xn'], dpm, name="mix_in_main_dw")
    g['w_small'] = _mm_tn(s['xn'], dps, name="mix_in_small_dw")
    dh, dg = _rms_bwd(s['h'], dxn, dout, p['mix_norm'][None, :], "mix_norm_bwd")
    g['mix_norm'] = dg[0]
    return dh, g


def _layer_params(w, li):
    p = {n: w[n][li] for n in WEIGHTS if n not in ('meta_tokens', 'final_norm')}
    bf = lambda a: a.astype(BF16)
    for tag in ('ffn1', 'ffn2'):
        wgu, wd = bf(p[tag + '_w_gate_up']), bf(p[tag + '_w_down'])
        f = wd.shape[0]
        p[tag] = dict(wg=wgu[:, :f], wu=wgu[:, f:], wg_t=wgu[:, :f].T, wu_t=wgu[:, f:].T, wd=wd, wd_t=wd.T)
    wm, ws = _w_in_split(bf(p['w_in']))
    p['w_main'], p['w_main_t'], p['w_small'], p['w_small_t'] = wm, wm.T, ws, ws.T
    for n in ('w_branch_attn', 'w_branch_ssd', 'w_branch_lru', 'w_out'):
        p[n + '_t'] = bf(p[n]).T
        p[n] = bf(p[n])
    wri = jnp.concatenate([_block_diag(p['lru_w_a']), _block_diag(p['lru_w_x'])], axis=1)
    p['lru_w_ri'], p['lru_w_ri_t'] = bf(wri), bf(wri).T
    return p


def _local_step(x, loss_target, w):
    b, seq, d = x.shape
    length = N_META + seq
    l = -(-length // Q_BLOCK) * Q_BLOCK
    t = b * l
    meta = jnp.broadcast_to(w['meta_tokens'].astype(F32)[None], (b, N_META, d))
    h = jnp.concatenate([meta, x, jnp.zeros((b, l - length, d), F32)], axis=1).reshape(t, d)
    tgt = jnp.concatenate([jnp.zeros((b, N_META, d), F32), loss_target, jnp.zeros((b, l - length, d), F32)],
                          axis=1).reshape(t, d)
    params, saves = [], []
    for li in range(DEPTH):
        p = _layer_params(w, li)
        h, s1 = _ffn_fwd(h, p['ffn1_norm'], p['ffn1'], "ffn1")
        h, sm = _mixer_fwd(h, p, b, l)
        h, s2 = _ffn_fwd(h, p['ffn2_norm'], p['ffn2'], "ffn2")
        params.append(p)
        saves.append((s1, sm, s2))
    dh, loss, dgf = _loss_head(h, tgt, w['final_norm'][None, :], l, "loss_head")
    layer_grads = [None] * DEPTH
    for li in reversed(range(DEPTH)):
        p = params[li]
        s1, sm, s2 = saves[li]
        g = {}
        dh, g['ffn2_norm'], g['ffn2_w_gate_up'], g['ffn2_w_down'] = _ffn_bwd(dh, s2, p['ffn2_norm'], p['ffn2'], "ffn2b")
        dh, gm = _mixer_bwd(dh, sm, p, b, l)
        g.update(gm)
        g['w_in'] = _w_in_merge(g.pop('w_main'), g.pop('w_small'))
        dh, g['ffn1_norm'], g['ffn1_w_gate_up'], g['ffn1_w_down'] = _ffn_bwd(dh, s1, p['ffn1_norm'], p['ffn1'], "ffn1b")
        layer_grads[li] = g
    grads = {n: jnp.stack([layer_grads[li][n] for li in range(DEPTH)]) for n in layer_grads[0]}
    for n in ('lru_w_a', 'lru_w_x'):
        grads[n] = grads[n].reshape(w[n].shape)
    dh3 = dh.reshape(b, l, d)
    grads['meta_tokens'] = jnp.sum(dh3[:, :N_META], axis=0)
    grads['final_norm'] = dgf[0]
    return loss, dh3[:, N_META:N_META + seq], grads


def _unflatten(flat, shapes):
    out, o = [], 0
    for sh in shapes:
        n = int(np.prod(sh))
        out.append(flat[o:o + n].reshape(sh))
        o += n
    return out


def kernel(x, meta_tokens, ffn1_norm, ffn1_w_gate_up, ffn1_w_down, mix_norm, w_in, fox_forget_bias, ssd_conv_w, ssd_conv_b, ssd_dt_bias, ssd_a_log, ssd_d, ssd_norm, lru_conv_w, lru_conv_b, lru_w_a, lru_b_a, lru_w_x, lru_b_x, lru_lambda, w_branch_attn, w_branch_ssd, w_branch_lru, w_out, ffn2_norm, ffn2_w_gate_up, ffn2_w_down, final_norm, loss_target, m_meta_tokens, m_ffn1_norm, m_ffn1_w_gate_up, m_ffn1_w_down, m_mix_norm, m_w_in, m_fox_forget_bias, m_ssd_conv_w, m_ssd_conv_b, m_ssd_dt_bias, m_ssd_a_log, m_ssd_d, m_ssd_norm, m_lru_conv_w, m_lru_conv_b, m_lru_w_a, m_lru_b_a, m_lru_w_x, m_lru_b_x, m_lru_lambda, m_w_branch_attn, m_w_branch_ssd, m_w_branch_lru, m_w_out, m_ffn2_norm, m_ffn2_w_gate_up, m_ffn2_w_down, m_final_norm, v_meta_tokens, v_ffn1_norm, v_ffn1_w_gate_up, v_ffn1_w_down, v_mix_norm, v_w_in, v_fox_forget_bias, v_ssd_conv_w, v_ssd_conv_b, v_ssd_dt_bias, v_ssd_a_log, v_ssd_d, v_ssd_norm, v_lru_conv_w, v_lru_conv_b, v_lru_w_a, v_lru_b_a, v_lru_w_x, v_lru_b_x, v_lru_lambda, v_w_branch_attn, v_w_branch_ssd, v_w_branch_lru, v_w_out, v_ffn2_norm, v_ffn2_w_gate_up, v_ffn2_w_down, v_final_norm):
    args = locals()
    wloc = {n: args[n] for n in WEIGHTS}
    mloc = {n: args['m_' + n] for n in WEIGHTS}
    vloc = {n: args['v_' + n] for n in WEIGHTS}
    nchip = 4
    chip = 2 * lax.axis_index("x") + lax.axis_index("y")
    core = lax.axis_index("c")
    hl = DEPTH // 2

    own = lambda out, mine, rank: lax.dynamic_update_index_in_dim(out, mine, rank, 0)
    half_rows = lambda a, which: lax.dynamic_slice_in_dim(a, which * (a.shape[0] // 2), a.shape[0] // 2, axis=0)
    mine = [half_rows(wloc[n].astype(BF16).reshape(-1, wloc[n].shape[-1]), core) for n in BIG_NAMES]
    got = _exchange_multi(mine, 'xy', False, "gather_w_chips")
    got = [own(g_, m_, chip).reshape(nchip * m_.shape[0], m_.shape[1]) for g_, m_ in zip(got, mine)]
    both = _exchange_multi(got, 'c', False, "gather_w_cores")
    full = {}
    for n, b_, g_ in zip(BIG_NAMES, both, got):
        _, r, c = wloc[n].shape
        v = own(b_, g_, core).reshape(2, nchip, hl, r, c)
        if BIG[n] == 1:
            full[n] = v.transpose(0, 2, 3, 1, 4).reshape(DEPTH, r, nchip * c)
        else:
            full[n] = v.transpose(0, 2, 1, 3, 4).reshape(DEPTH, nchip * r, c)
    cs_shapes = [wloc[n].shape for n in COLSHARD_SMALL]
    cs_total = sum(int(np.prod(s)) for s in cs_shapes)
    cs_rows = -(-cs_total // (8 * 128)) * 8
    cs_flat = jnp.concatenate([wloc[n].reshape(-1) for n in COLSHARD_SMALL])
    cs_flat = jnp.pad(cs_flat, (0, cs_rows * 128 - cs_total)).reshape(cs_rows, 128)
    cs_all = _exchange(cs_flat, 'xy', False, "gather_small").reshape(nchip, -1)
    cs_chip = [_unflatten(cs_all[j], cs_shapes) for j in range(nchip)]
    for i, n in enumerate(COLSHARD_SMALL):
        full[n] = jnp.concatenate([cs_chip[j][i] for j in range(nchip)], axis=-1)
    for n in SMALL_NAMES:
        if n not in COLSHARD_SMALL:
            full[n] = wloc[n]

    loss_part, grad_x, grads = _local_step(x, loss_target, full)

    g2d = [grads[n].reshape(-1, grads[n].shape[-1]) for n in BIG_NAMES]
    give = [half_rows(g_, 1 - core) for g_ in g2d]
    keep = [half_rows(g_, core) for g_ in g2d]
    theirs = _exchange_multi(give, 'c', False, "reduce_cores", single=True)
    psums = []
    for n, k_, t_ in zip(BIG_NAMES, keep, theirs):
        s2 = _sum_rows([k_, t_], BF16, "reduce_cores_sum")
        _, r, c = wloc[n].shape
        if BIG[n] == 1:
            psums.append(s2.reshape(s2.shape[0], nchip, c).transpose(1, 0, 2))
        else:
            psums.append(s2.reshape(hl, nchip, r, c).transpose(1, 0, 2, 3).reshape(nchip, hl * r, c))
    parts = _exchange_multi(psums, 'xy', True, "reduce_chips")
    parts = [own(p_, lax.dynamic_index_in_dim(s_, chip, axis=0, keepdims=False), chip) for p_, s_ in zip(parts, psums)]
    rsums = [_sum_slots(p_, F32, "reduce_chips_sum") for p_ in parts]
    halves = _exchange_multi(rsums, 'c', False, "reduce_share")
    gbig = {n: own(h_, r_, core).reshape(wloc[n].shape) for n, h_, r_ in zip(BIG_NAMES, halves, rsums)}

    sm_shapes = [grads[n].shape for n in SMALL_NAMES]
    sm_total = sum(int(np.prod(s)) for s in sm_shapes) + 128
    sm_rows = -(-sm_total // (8 * 128)) * 8
    sm_flat = jnp.concatenate([loss_part.reshape(-1)] + [grads[n].reshape(-1) for n in SMALL_NAMES])
    sm_flat = jnp.pad(sm_flat, (0, sm_rows * 128 - sm_total)).reshape(sm_rows, 128)
    sm_all = _exchange(sm_flat, 'xyc', False, "gather_small_grads")
    sm_sum = _sum_rows([sm_all[j] for j in range(8)], F32, "small_grads_sum").reshape(-1)
    loss = sm_sum[0]
    gsmall_full = dict(zip(SMALL_NAMES, _unflatten(sm_sum[128:], sm_shapes)))
    gsmall = {}
    for n in SMALL_NAMES:
        gfull = gsmall_full[n]
        if n in COLSHARD_SMALL:
            wcols = wloc[n].shape[-1]
            gfull = lax.dynamic_slice_in_dim(gfull, chip * wcols, wcols, axis=gfull.ndim - 1)
        gsmall[n] = gfull

    big_out = [{}, {}, {}]
    for n in BIG_NAMES:
        rows2d = lambda a: a.reshape(-1, a.shape[-1])
        res = _adamw(rows2d(wloc[n]), rows2d(gbig[n]), rows2d(mloc[n]), rows2d(vloc[n]), "adamw_big")
        for k in range(3):
            big_out[k][n] = res[k].reshape(wloc[n].shape)
    loc_shapes = [wloc[n].shape for n in SMALL_NAMES]
    loc_total = sum(int(np.prod(s)) for s in loc_shapes)
    loc_rows = -(-loc_total // (8 * 128)) * 8

    def flat_small(dct):
        v = jnp.concatenate([dct[n].reshape(-1) for n in SMALL_NAMES])
        return jnp.pad(v, (0, loc_rows * 128 - loc_total)).reshape(loc_rows, 128)
    dls, mns, vns = _adamw(flat_small(wloc), flat_small(gsmall), flat_small(mloc), flat_small(vloc), "adamw_small")
    small_out = [dict(zip(SMALL_NAMES, _unflatten(a.reshape(-1), loc_shapes))) for a in (dls, mns, vns)]

    grad_w = {**gbig, **gsmall}
    outs = [loss, grad_x] + [grad_w[n] for n in WEIGHTS]
    for k in range(3):
        merged = {**big_out[k], **small_out[k]}
        outs += [merged[n] for n in WEIGHTS]
    return tuple(outs)
```

```python
import functools
import math

import numpy as np
import jax
import jax.numpy as jnp
from jax import lax
from jax.experimental import pallas as pl
from jax.experimental.pallas import tpu as pltpu

F32 = jnp.float32
BF16 = jnp.bfloat16
HI = lax.Precision.HIGHEST
VMEM_LIMIT_BYTES = 56 * 1024 * 1024
NEG = -1e30

D_MODEL = 1024
SEQ = 4096
DEPTH = 4
N_META = 16
Q_BLOCK = 128
SSD_CHUNK = 128
NORM_EPS = 1e-6
ATTN_HEADS = 16
ATTN_HEAD_DIM = 64
SSD_HEAD_DIM = 64
SSD_GROUPS = 2
SSD_STATE = 128
CONV_K = 4
LRU_BLOCKS = 16
LRU_C = 8.0
D_FF = 2816
ADAM_LR = 0.001
ADAM_B1 = 0.9
ADAM_B2 = 0.999
ADAM_EPS = 1e-08
ADAM_WD = 0.01
ADAM_STEP = 10
SMALL_W = 128
_ROWWISE_TILE_ELEMS = 512 * 1024

WEIGHTS = ['meta_tokens', 'ffn1_norm', 'ffn1_w_gate_up', 'ffn1_w_down', 'mix_norm', 'w_in', 'fox_forget_bias',
           'ssd_conv_w', 'ssd_conv_b', 'ssd_dt_bias', 'ssd_a_log', 'ssd_d', 'ssd_norm', 'lru_conv_w', 'lru_conv_b',
           'lru_w_a', 'lru_b_a', 'lru_w_x', 'lru_b_x', 'lru_lambda', 'w_branch_attn', 'w_branch_ssd', 'w_branch_lru',
           'w_out', 'ffn2_norm', 'ffn2_w_gate_up', 'ffn2_w_down', 'final_norm']
BIG = {'ffn1_w_gate_up': 1, 'ffn1_w_down': 0, 'w_in': 1, 'w_branch_attn': 0, 'w_branch_ssd': 0, 'w_branch_lru': 0,
       'w_out': 0, 'ffn2_w_gate_up': 1, 'ffn2_w_down': 0}
BIG_NAMES = [n for n in WEIGHTS if n in BIG]
COLSHARD_SMALL = ['meta_tokens', 'ssd_conv_w', 'lru_conv_w']
SMALL_NAMES = [n for n in WEIGHTS if n not in BIG]


def _pick(n, cands):
    for c in cands:
        if n % c == 0:
            return c
    raise ValueError(f"no tile for {n} in {cands}")


def _pcall(body, **kw):
    return pl.pallas_call(body, **kw)


def _cparams(sem):
    return pltpu.CompilerParams(dimension_semantics=sem, vmem_limit_bytes=VMEM_LIMIT_BYTES)


def _sds(shape, dtype):
    return jax.ShapeDtypeStruct(tuple(shape), dtype)


def _dot(a, b, hi=False):
    return jnp.dot(a, b, precision=HI if hi else None, preferred_element_type=F32)


def _dot_nt(a, b):
    return lax.dot_general(a, b, (((1,), (1,)), ((), ())), preferred_element_type=F32)


def _dot_sel(a, b, mask):
    if mask == 'a':
        am = a.astype(BF16)
        return sum(_dot(am, p.astype(BF16)) for p in _split3(b))
    bm = b.astype(BF16)
    return sum(_dot(p.astype(BF16), bm) for p in _split3(a))


def _dot_tn(a, b):
    return lax.dot_general(a, b, (((0,), (0,)), ((), ())), preferred_element_type=F32)


def _sigmoid(x):
    return 1.0 / (1.0 + jnp.exp(-x))


def _sigmoid_tanh(x):
    return 0.5 * jnp.tanh(0.5 * x) + 0.5


def _softplus(x):
    return jnp.maximum(x, 0.0) + jnp.log1p(jnp.exp(-jnp.abs(x)))


def _silu(x):
    return x * _sigmoid(x)


def _dsilu(x):
    s = _sigmoid(x)
    return s * (1.0 + x * (1.0 - s))


_GELU_C = math.sqrt(2.0 / math.pi)


def _gelu(x):
    return 0.5 * x * (1.0 + jnp.tanh(_GELU_C * (x + 0.044715 * x * x * x)))


def _dgelu(x):
    t = jnp.tanh(_GELU_C * (x + 0.044715 * x * x * x))
    return 0.5 * (1.0 + t) + 0.5 * x * (1.0 - t * t) * _GELU_C * (1.0 + 3.0 * 0.044715 * x * x)


def _expm1(x):
    series = x * (1.0 + x * 0.5 * (1.0 + x * (1.0 / 3.0) * (1.0 + x * 0.25 * (1.0 + x * 0.2))))
    return jnp.where(jnp.abs(x) < 0.05, series, jnp.exp(x) - 1.0)


def _rowwise(fn, ins, outs, *, bcast=(), reds=(), tm=None, name, period=None):
    t_rows = ins[0].shape[0]
    if tm is None:
        widest = max([a.shape[1] for a in ins] + [c for c, _ in outs])
        tm = _pick(math.gcd(t_rows, period or t_rows),
                   [c for c in (384, 256, 128, 64, 32, 16, 8) if c * widest <= _ROWWISE_TILE_ELEMS or c == 8])
    nt = t_rows // tm
    assert t_rows % tm == 0 and (period is None or period % tm == 0)
    n_in, n_out = len(ins) + len(bcast), len(outs)

    def body(*refs):
        i = pl.program_id(0)
        pos = None
        if period is not None:
            pos = (i * tm) % period + lax.broadcasted_iota(jnp.int32, (tm, 1), 0)
        res = fn(pos, *[r[...] for r in refs[:n_in]])
        res = res if isinstance(res, tuple) else (res,)
        for r, v in zip(refs[n_in:n_in + n_out], res[:n_out]):
            r[...] = v.astype(r.dtype)
        red_refs = refs[n_in + n_out:]
        if red_refs:
            @pl.when(i == 0)
            def _():
                for r in red_refs:
                    r[...] = jnp.zeros_like(r)
            for r, v in zip(red_refs, res[n_out:]):
                r[...] += v

    in_specs = [pl.BlockSpec((tm, a.shape[1]), lambda i: (i, 0)) for a in ins]
    in_specs += [pl.BlockSpec(b.shape, lambda i, n=b.ndim: (0,) * n) for b in bcast]
    out_specs = [pl.BlockSpec((tm, c), lambda i: (i, 0)) for c, _ in outs]
    out_specs += [pl.BlockSpec(s, lambda i: (0, 0)) for s in reds]
    out_shape = [_sds((t_rows, c), dt) for c, dt in outs] + [_sds(s, F32) for s in reds]
    res = _pcall(body, name=name, grid=(nt,), in_specs=in_specs, out_specs=out_specs, out_shape=out_shape,
                 compiler_params=_cparams(("arbitrary",) if reds else ("parallel",)))(*ins, *bcast)
    return res


_TM = (768, 384, 256, 128)
_TN = (1536, 1408, 1024, 768, 512, 640, 384, 256, 128)
_TK = (1536, 1024, 2816, 1408, 512, 384, 256, 128)
_TKO = (1024, 1408, 512, 384, 256, 128)


def _mm_nn(a, b, out_dtype, *, res=None, alpha=1.0, name):
    m, k = a.shape
    k2, n = b.shape
    assert k == k2
    tm, tn, tk = _pick(m, _TM), _pick(n, _TN), _pick(k, _TK)
    nk = k // tk

    def body(*refs):
        if res is None:
            a_ref, b_ref, o_ref, acc = refs
            r_ref = None
        else:
            a_ref, b_ref, r_ref, o_ref, acc = refs
        kk = pl.program_id(2)

        @pl.when(kk == 0)
        def _():
            acc[...] = jnp.zeros_like(acc)

        acc[...] += _dot(a_ref[...].astype(BF16), b_ref[...].astype(BF16))

        @pl.when(kk == nk - 1)
        def _():
            v = acc[...]
            if alpha != 1.0:
                v = v * alpha
            if r_ref is not None:
                v = r_ref[...].astype(F32) + v
            o_ref[...] = v.astype(o_ref.dtype)

    in_specs = [pl.BlockSpec((tm, tk), lambda j, i, kk: (i, kk)), pl.BlockSpec((tk, tn), lambda j, i, kk: (kk, j))]
    args = [a, b]
    if res is not None:
        in_specs.append(pl.BlockSpec((tm, tn), lambda j, i, kk: (i, j)))
        args.append(res)
    return _pcall(body, name=name, grid=(n // tn, m // tm, nk), in_specs=in_specs,
                  out_specs=pl.BlockSpec((tm, tn), lambda j, i, kk: (i, j)), out_shape=_sds((m, n), out_dtype),
                  scratch_shapes=[pltpu.VMEM((tm, tn), F32)],
                  compiler_params=_cparams(("parallel", "parallel", "arbitrary")))(*args)


def _mm_swiglu(a, wg, wu, name):
    m, k = a.shape
    f = wg.shape[1]
    tm, tn, tk = _pick(m, _TM), _pick(f, _TN), _pick(k, _TK)
    nk = k // tk

    def body(a_ref, g_w, u_w, g_ref, u_ref, act_ref, accg, accu):
        kk = pl.program_id(2)

        @pl.when(kk == 0)
        def _():
            accg[...] = jnp.zeros_like(accg)
            accu[...] = jnp.zeros_like(accu)

        av = a_ref[...].astype(BF16)
        accg[...] += _dot(av, g_w[...])
        accu[...] += _dot(av, u_w[...])

        @pl.when(kk == nk - 1)
        def _():
            g, u = accg[...], accu[...]
            g_ref[...] = g.astype(g_ref.dtype)
            u_ref[...] = u.astype(u_ref.dtype)
            act_ref[...] = (g * _sigmoid_tanh(g) * u).astype(act_ref.dtype)

    wspec = pl.BlockSpec((tk, tn), lambda j, i, kk: (kk, j))
    ospec = pl.BlockSpec((tm, tn), lambda j, i, kk: (i, j))
    return _pcall(body, name=name, grid=(f // tn, m // tm, nk),
                  in_specs=[pl.BlockSpec((tm, tk), lambda j, i, kk: (i, kk)), wspec, wspec],
                  out_specs=[ospec] * 3, out_shape=[_sds((m, f), BF16)] * 3,
                  scratch_shapes=[pltpu.VMEM((tm, tn), F32)] * 2,
                  compiler_params=_cparams(("parallel", "parallel", "arbitrary")))(a, wg, wu)


def _mm_dswiglu(dout, wd_t, g, u, alpha, name):
    m, k = dout.shape
    f = wd_t.shape[1]
    tm, tn, tk = _pick(m, _TM), _pick(f, _TN), _pick(k, _TK)
    nk = k // tk

    def body(a_ref, w_ref, g_ref, u_ref, dg_ref, du_ref, acc):
        kk = pl.program_id(2)

        @pl.when(kk == 0)
        def _():
            acc[...] = jnp.zeros_like(acc)

        acc[...] += _dot(a_ref[...].astype(BF16), w_ref[...])

        @pl.when(kk == nk - 1)
        def _():
            dact = acc[...] * alpha
            gv, uv = g_ref[...].astype(F32), u_ref[...].astype(F32)
            sg = _sigmoid_tanh(gv)
            dg_ref[...] = (dact * uv * (sg * (1.0 + gv * (1.0 - sg)))).astype(dg_ref.dtype)
            du_ref[...] = (dact * (gv * sg)).astype(du_ref.dtype)

    ospec = pl.BlockSpec((tm, tn), lambda j, i, kk: (i, j))
    return _pcall(body, name=name, grid=(f // tn, m // tm, nk),
                  in_specs=[pl.BlockSpec((tm, tk), lambda j, i, kk: (i, kk)),
                            pl.BlockSpec((tk, tn), lambda j, i, kk: (kk, j)), ospec, ospec],
                  out_specs=[ospec] * 2, out_shape=[_sds((m, f), BF16)] * 2,
                  scratch_shapes=[pltpu.VMEM((tm, tn), F32)],
                  compiler_params=_cparams(("parallel", "parallel", "arbitrary")))(dout, wd_t, g, u)


def _mm_tn(a, b, *, alpha=1.0, name):
    m, k = a.shape
    m2, n = b.shape
    assert m == m2
    tm, tn, tko = _pick(m, _TM), _pick(n, _TN), _pick(k, _TKO)
    nm = m // tm

    def body(a_ref, b_ref, o_ref, acc):
        mm = pl.program_id(2)

        @pl.when(mm == 0)
        def _():
            acc[...] = jnp.zeros_like(acc)

        acc[...] += _dot_tn(a_ref[...].astype(BF16), b_ref[...].astype(BF16))

        @pl.when(mm == nm - 1)
        def _():
            v = acc[...]
            o_ref[...] = v * alpha if alpha != 1.0 else v

    return _pcall(body, name=name, grid=(k // tko, n // tn, nm),
                  in_specs=[pl.BlockSpec((tm, tko), lambda i, j, mm: (mm, i)),
                            pl.BlockSpec((tm, tn), lambda i, j, mm: (mm, j))],
                  out_specs=pl.BlockSpec((tko, tn), lambda i, j, mm: (i, j)), out_shape=_sds((k, n), F32),
                  scratch_shapes=[pltpu.VMEM((tko, tn), F32)],
                  compiler_params=_cparams(("parallel", "parallel", "arbitrary")))(a, b)


def _rms_fwd(h, g, name):
    def fn(_, hv, gv):
        r = lax.rsqrt(jnp.mean(hv * hv, axis=1, keepdims=True) + NORM_EPS)
        return hv * r * gv
    return _rowwise(fn, [h], [(h.shape[1], BF16)], bcast=[g], name=name)[0]


def _rms_bwd(h, dxn, dres, g, name):
    d = h.shape[1]

    def fn(_, hv, dv, rv, gv):
        r = lax.rsqrt(jnp.mean(hv * hv, axis=1, keepdims=True) + NORM_EPS)
        xh = hv * r
        dxh = dv * gv
        dh = r * (dxh - xh * jnp.mean(dxh * xh, axis=1, keepdims=True))
        return rv + dh, jnp.sum(dv * xh, axis=0, keepdims=True)
    return _rowwise(fn, [h, dxn, dres], [(d, F32)], bcast=[g], reds=[(1, d)], name=name)


def _merge_fwd(mg, ba, bb, bc, name):
    d = ba.shape[1]

    def fn(_, m, a, b, c):
        g = _sigmoid(m.astype(F32))
        return g[:, :d] * a.astype(F32) + g[:, d:2 * d] * b.astype(F32) + g[:, 2 * d:] * c.astype(F32)
    return _rowwise(fn, [mg, ba, bb, bc], [(d, BF16)], name=name)[0]


def _merge_bwd(mg, ba, bb, bc, dmix, name):
    d = ba.shape[1]

    def fn(_, m, a, b, c, dm):
        g = _sigmoid(m.astype(F32))
        dm = dm.astype(F32)
        br = (a.astype(F32), b.astype(F32), c.astype(F32))
        douts, dgs = [], []
        for j in range(3):
            gj = g[:, j * d:(j + 1) * d]
            douts.append(dm * gj)
            dgs.append(dm * br[j] * gj * (1.0 - gj))
        return (*douts, jnp.concatenate(dgs, axis=1))
    return _rowwise(fn, [mg, ba, bb, bc, dmix], [(d, BF16)] * 3 + [(3 * d, BF16)], name=name)


def _loss_head(h, tgt, g, seq_len, name):
    d = h.shape[1]

    def fn(pos, hv, tv, gv):
        r = lax.rsqrt(jnp.mean(hv * hv, axis=1, keepdims=True) + NORM_EPS)
        xh = hv * r
        real = (pos >= N_META) & (pos < N_META + SEQ)
        e = jnp.where(real, xh * gv - tv, 0.0)
        part = jnp.sum(jnp.sum(e * e, axis=1, keepdims=True), axis=0, keepdims=True) * (0.5 / d)
        dy = e * (1.0 / d)
        dxh = dy * gv
        dh = r * (dxh - xh * jnp.mean(dxh * xh, axis=1, keepdims=True))
        return dh, jnp.broadcast_to(part, (1, 128)), jnp.sum(dy * xh, axis=0, keepdims=True)
    return _rowwise(fn, [h, tgt], [(d, F32)], bcast=[g], reds=[(1, 128), (1, d)], name=name, period=seq_len)


def _adamw(w, g, m, v, name):
    c1 = 1.0 - ADAM_B1 ** ADAM_STEP
    c2 = 1.0 - ADAM_B2 ** ADAM_STEP
    wd = w.shape[1]

    def fn(_, wv, gv, mv, vv):
        mn = ADAM_B1 * mv + (1.0 - ADAM_B1) * gv
        vn = ADAM_B2 * vv + (1.0 - ADAM_B2) * (gv * gv)
        delta = -ADAM_LR * ((mn / c1) / (jnp.sqrt(vn / c2) + ADAM_EPS) + ADAM_WD * wv)
        return delta, mn, vn
    return _rowwise(fn, [w, g, m, v], [(wd, F32)] * 3, name=name)


def _sum_rows(parts, out_dtype, name):
    def fn(_, *vs):
        acc = vs[0].astype(F32)
        for v in vs[1:]:
            acc = acc + v.astype(F32)
        return acc
    return _rowwise(fn, list(parts), [(parts[0].shape[1], out_dtype)], name=name)[0]


def _cumsum_seq(x, reverse, name):
    b, l, w = x.shape
    q = 128
    nc = l // q

    def body(x_ref, o_ref):
        row = lax.broadcasted_iota(jnp.int32, (q, q), 0)
        col = lax.broadcasted_iota(jnp.int32, (q, q), 1)
        tri = ((row <= col) if reverse else (row >= col)).astype(F32)
        rsel = lax.broadcasted_iota(jnp.int32, (q, w), 0) == (0 if reverse else q - 1)

        def step(i, carry):
            j = (nc - 1 - i) if reverse else i
            start = pl.multiple_of(j * q, q)
            cs = _dot_sel(tri, x_ref[pl.ds(start, q), :], 'a') + carry
            o_ref[pl.ds(start, q), :] = cs
            return jnp.sum(jnp.where(rsel, cs, 0.0), axis=0, keepdims=True)

        lax.fori_loop(0, nc, step, jnp.zeros((1, w), F32))

    return _pcall(body, name=name, grid=(b,), in_specs=[pl.BlockSpec((None, l, w), lambda i: (i, 0, 0))],
                  out_specs=pl.BlockSpec((None, l, w), lambda i: (i, 0, 0)), out_shape=_sds(x.shape, F32),
                  compiler_params=_cparams(("parallel",)))(x)


_HALO = 16


def _conv_tiles(l, c):
    return _pick(l, (384, 256, 128)), _pick(c, (512, 256, 128))


def _conv_fwd(x, w, bias, out_dtype, name, with_silu=False):
    b, l, c = x.shape
    tt, cw = _conv_tiles(l, c)

    def body(x_ref, h_ref, w_ref, b_ref, *o_refs):
        t = pl.program_id(2)
        halo = jnp.where(t == 0, 0.0, h_ref[...].astype(F32))
        xe = jnp.concatenate([halo, x_ref[...].astype(F32)], axis=0)
        wv = w_ref[...]
        acc = b_ref[...] + wv[CONV_K - 1:CONV_K, :] * xe[_HALO:]
        for j in range(CONV_K - 1):
            acc = acc + wv[j:j + 1, :] * pltpu.roll(xe, CONV_K - 1 - j, 0)[_HALO:]
        o_refs[0][...] = acc.astype(o_refs[0].dtype)
        if with_silu:
            o_refs[1][...] = _silu(acc.astype(o_refs[0].dtype).astype(F32)).astype(o_refs[1].dtype)

    ospec = pl.BlockSpec((None, tt, cw), lambda i, j, t: (i, t, j))
    n_out = 2 if with_silu else 1
    res = _pcall(body, name=name, grid=(b, c // cw, l // tt),
                 in_specs=[pl.BlockSpec((None, tt, cw), lambda i, j, t: (i, t, j)),
                           pl.BlockSpec((None, _HALO, cw), lambda i, j, t: (i, jnp.maximum(t * (tt // _HALO) - 1, 0), j)),
                           pl.BlockSpec((CONV_K, cw), lambda i, j, t: (0, j)),
                           pl.BlockSpec((1, cw), lambda i, j, t: (0, j))],
                 out_specs=[ospec] * n_out, out_shape=[_sds(x.shape, out_dtype)] * n_out,
                 compiler_params=_cparams(("parallel", "parallel", "parallel")))(x, x, w, bias)
    return res if with_silu else res[0]


def _conv_bwd(x, dy, w, name):
    b, l, c = x.shape
    tt, cw = _conv_tiles(l, c)
    nt = l // tt

    def body(x_ref, xh_ref, d_ref, dh_ref, w_ref, dx_ref, dw_ref):
        i, t = pl.program_id(1), pl.program_id(2)
        halo = jnp.where(t == 0, 0.0, xh_ref[...].astype(F32))
        xe = jnp.concatenate([halo, x_ref[...].astype(F32)], axis=0)
        dv = d_ref[...].astype(F32)
        nxt = jnp.where(t == nt - 1, 0.0, dh_ref[...].astype(F32))
        de = jnp.concatenate([dv, nxt], axis=0)
        wv = w_ref[...]
        dx = wv[CONV_K - 1:CONV_K, :] * dv
        rowid = lax.broadcasted_iota(jnp.int32, (8, 1), 0)
        part = jnp.where(rowid == CONV_K, jnp.sum(dv, axis=0, keepdims=True), 0.0)
        part = part + jnp.where(rowid == CONV_K - 1, jnp.sum(dv * xe[_HALO:], axis=0, keepdims=True), 0.0)
        for j in range(CONV_K - 1):
            s = CONV_K - 1 - j
            dx = dx + wv[j:j + 1, :] * pltpu.roll(de, tt + _HALO - s, 0)[:tt]
            xs = pltpu.roll(xe, s, 0)[_HALO:]
            part = part + jnp.where(rowid == j, jnp.sum(dv * xs, axis=0, keepdims=True), 0.0)
        dx_ref[...] = dx.astype(dx_ref.dtype)

        @pl.when((i == 0) & (t == 0))
        def _():
            dw_ref[...] = jnp.zeros_like(dw_ref)
        dw_ref[...] += part

    return _pcall(body, name=name, grid=(c // cw, b, nt),
                  in_specs=[pl.BlockSpec((None, tt, cw), lambda j, i, t: (i, t, j)),
                            pl.BlockSpec((None, _HALO, cw), lambda j, i, t: (i, jnp.maximum(t * (tt // _HALO) - 1, 0), j)),
                            pl.BlockSpec((None, tt, cw), lambda j, i, t: (i, t, j)),
                            pl.BlockSpec((None, _HALO, cw),
                                         lambda j, i, t: (i, jnp.minimum((t + 1) * (tt // _HALO), l // _HALO - 1), j)),
                            pl.BlockSpec((CONV_K, cw), lambda j, i, t: (0, j))],
                  out_specs=[pl.BlockSpec((None, tt, cw), lambda j, i, t: (i, t, j)),
                             pl.BlockSpec((8, cw), lambda j, i, t: (0, j))],
                  out_shape=[_sds(x.shape, BF16), _sds((8, c), F32)],
                  compiler_params=_cparams(("parallel", "arbitrary", "arbitrary")))(x, x, dy, dy, w)


_SUBLANES = 8


def _linear_scan(a, u, reverse, name, gate=None):
    b, l, c = a.shape
    tt = 128
    cw = _pick(c, (512, 256, 128))
    nt = l // tt
    groups = tt // _SUBLANES

    def body(*refs):
        if gate is None:
            a_ref, u_ref, h_ref, carry = refs
        else:
            a_ref, u_ref, g_ref, h_ref, y_ref, carry = refs
        t = pl.program_id(2)

        @pl.when(t == 0)
        def _():
            carry[...] = jnp.zeros_like(carry)

        av, uv = a_ref[...], u_ref[...]
        sub = jnp.bitwise_and(lax.broadcasted_iota(jnp.int32, (tt, cw), 0), _SUBLANES - 1)
        k = 1
        while k < _SUBLANES:
            keep = (sub < _SUBLANES - k) if reverse else (sub >= k)
            shift = tt - k if reverse else k
            a_sh = jnp.where(keep, pltpu.roll(av, shift, 0), 1.0)
            u_sh = jnp.where(keep, pltpu.roll(uv, shift, 0), 0.0)
            uv = uv + av * u_sh
            av = av * a_sh
            k *= 2
        edge = carry[0:1, :]
        for g in (range(groups - 1, -1, -1) if reverse else range(groups)):
            rows = slice(g * _SUBLANES, (g + 1) * _SUBLANES)
            hg = uv[rows] + av[rows] * edge
            h_ref[rows, :] = hg
            edge = hg[0:1] if reverse else hg[_SUBLANES - 1:_SUBLANES]
        carry[...] = jnp.broadcast_to(edge, carry.shape)
        if gate is not None:
            y_ref[...] = (h_ref[...] * _gelu(g_ref[...].astype(F32))).astype(y_ref.dtype)

    tmap = (lambda i, j, t: (i, nt - 1 - t, j)) if reverse else (lambda i, j, t: (i, t, j))
    spec = pl.BlockSpec((None, tt, cw), tmap)
    ins, outs, shapes = [a, u], [spec], [_sds(a.shape, F32)]
    if gate is not None:
        ins, outs, shapes = [a, u, gate], [spec, spec], [_sds(a.shape, F32), _sds(a.shape, BF16)]
    res = _pcall(body, name=name, grid=(b, c // cw, nt), in_specs=[spec] * len(ins), out_specs=outs,
                 out_shape=shapes, scratch_shapes=[pltpu.VMEM((8, cw), F32)],
                 compiler_params=_cparams(("parallel", "parallel", "arbitrary")))(*ins)
    return res if gate is not None else res[0]


ATTN_W = 128
_AUG_C = 0
_AUG_ONE = 3
_AUG_LSE = 6


def _attn_blk(l):
    return _pick(l, (384, 256, 128))


_FWD_HEADS_PER_STEP = 4


def _prep_rows(l):
    return _pick(l, (1408, 384, 256, 128))


def _split3(x):
    x1 = x.astype(BF16).astype(F32)
    x2 = (x - x1).astype(BF16).astype(F32)
    x3 = (x - x1 - x2).astype(BF16).astype(F32)
    return x1, x2, x3


def _aug_lanes(lane, base, cols, ones_at=()):
    out = jnp.zeros(lane.shape, F32)
    for o in ones_at:
        out = out + ((lane >= o) & (lane < o + 3)).astype(F32)
    for k, v in enumerate(cols):
        out = jnp.where(lane == base + k, v, out)
    return out


def _pair_specs(blk, nb):
    at = lambda ww: pl.BlockSpec((None, 2, blk, ww), lambda bi, p, i: (bi, p, i, 0))
    whole = pl.BlockSpec((None, 2, nb, blk, ATTN_W), lambda bi, p, i: (bi, p, 0, 0, 0))
    rows = pl.BlockSpec((None, blk, ATTN_W), lambda bi, p, i: (bi, i, p))
    return at, whole, rows


def _attn_prep(pm3, cum, col0, name):
    b, l, _ = pm3.shape
    dh, nh = ATTN_HEAD_DIM, ATTN_HEADS
    blk = _prep_rows(l)
    scale = dh ** -0.5

    def body(q_ref, k_ref, v_ref, c_ref, qa_ref, ka_ref, va_ref):
        pair = pl.program_id(1)
        lane = lax.broadcasted_iota(jnp.int32, (1, ATTN_W), 1)
        head = lane < dh
        cv = c_ref[...]
        qf, kf, vf = (r[...].astype(F32) for r in (q_ref, k_ref, v_ref))
        for e in range(2):
            c1, c2, c3 = _split3(jnp.sum(jnp.where(lane == 2 * pair + e, cv, 0.0), axis=1, keepdims=True))
            qe, ke, ve = (pltpu.roll(t, dh, 1) for t in (qf, kf, vf)) if e else (qf, kf, vf)
            qa_ref[e] = jnp.where(head, qe * scale,
                                  _aug_lanes(lane, dh + _AUG_C, (c1, c2, c3), (dh + _AUG_ONE,))).astype(BF16)
            ka_ref[e] = jnp.where(head, ke, _aug_lanes(lane, dh + _AUG_ONE, (-c1, -c2, -c3),
                                                       (dh + _AUG_C, dh + _AUG_LSE))).astype(BF16)
            va_ref[e] = jnp.where(head, ve, _aug_lanes(lane, dh, (), (dh,))).astype(BF16)

    at, _, _ = _pair_specs(blk, l // blk)
    cols = lambda c0: pl.BlockSpec((None, blk, ATTN_W), lambda bi, p, i, c0=c0: (bi, i, c0 // ATTN_W + p))
    return _pcall(body, name=name, grid=(b, nh // 2, l // blk),
                  in_specs=[cols(col0[0]), cols(col0[1]), cols(col0[2]),
                            pl.BlockSpec((None, blk, SMALL_W), lambda bi, p, i: (bi, i, 0))],
                  out_specs=[at(ATTN_W)] * 3, out_shape=[_sds((b, nh, l, ATTN_W), BF16)] * 3,
                  compiler_params=_cparams(("parallel", "parallel", "parallel")))(pm3, pm3, pm3, cum)


def _attn_prep_bwd(dy3, y3, qa, lse, name):
    b, nh, l, _ = qa.shape
    dh = ATTN_HEAD_DIM
    blk = _prep_rows(l)

    def body(dy_ref, y_ref, qa_ref, lse_ref, qa2_ref, doa_ref):
        lane = lax.broadcasted_iota(jnp.int32, (1, ATTN_W), 1)
        dyf = dy_ref[...].astype(F32)
        prod = dyf * y_ref[...].astype(F32)
        for e in range(2):
            mine = (lane >= dh) if e else (lane < dh)
            d1, d2, d3 = _split3(jnp.sum(jnp.where(mine, prod, 0.0), axis=1, keepdims=True))
            l1, l2, l3 = _split3(lse_ref[e])
            dye = pltpu.roll(dyf, dh, 1) if e else dyf
            doa_ref[e] = jnp.where(lane < dh, dye, _aug_lanes(lane, dh, (-d1, -d2, -d3))).astype(BF16)
            qa2 = qa_ref[e].astype(F32)
            for k, lv in enumerate((l1, l2, l3)):
                qa2 = jnp.where(lane == dh + _AUG_LSE + k, -lv, qa2)
            qa2_ref[e] = qa2.astype(BF16)

    at, _, rows = _pair_specs(blk, l // blk)
    return _pcall(body, name=name, grid=(b, nh // 2, l // blk), in_specs=[rows, rows, at(ATTN_W), at(1)],
                  out_specs=[at(ATTN_W)] * 2, out_shape=[_sds(qa.shape, BF16)] * 2,
                  compiler_params=_cparams(("parallel", "parallel", "parallel")))(dy3, y3, qa, lse)


def _flash_fwd(qa, ka, va, d_model, name):
    b, h, l, w = qa.shape
    dh = ATTN_HEAD_DIM
    blk = _attn_blk(l)
    nb = l // blk
    hps = _FWD_HEADS_PER_STEP if h % _FWD_HEADS_PER_STEP == 0 else 2
    kr, vr = ka.reshape(b, h, nb, blk, w), va.reshape(b, h, nb, blk, w)

    def body(q_ref, k_ref, v_ref, o_ref, lse_ref):
        i = pl.program_id(2)
        row = lax.broadcasted_iota(jnp.int32, (blk, blk), 0)
        col = lax.broadcasted_iota(jnp.int32, (blk, blk), 1)

        def scores(e, j):
            return _dot_nt(q_ref[e], k_ref[e, j])

        def consume(e, j, s, m, acc):
            mn = jnp.maximum(m, jnp.max(s, axis=1, keepdims=True))
            return mn, jnp.exp(m - mn) * acc + _dot(jnp.exp(s - mn).astype(BF16), v_ref[e, j])

        def step(j, carry):
            out = []
            for e in range(hps):
                m, acc, s = carry[3 * e:3 * e + 3]
                s_next = scores(e, j + 1)
                out += [*consume(e, j, s, m, acc), s_next]
            return tuple(out)

        init = tuple(t for e in range(hps)
                     for t in (jnp.full((blk, 1), NEG, F32), jnp.zeros((blk, w), F32), scores(e, 0)))
        carry = lax.fori_loop(0, i, step, init)
        lane = lax.broadcasted_iota(jnp.int32, (1, w), 1)
        outs = []
        for e in range(hps):
            m, acc = consume(e, i, jnp.where(col <= row, carry[3 * e + 2], NEG), carry[3 * e], carry[3 * e + 1])
            lsum = acc[:, dh:dh + 1]
            lse_ref[e] = m + jnp.log(lsum)
            outs.append(acc / lsum)
        for pr in range(hps // 2):
            o_ref[:, pr * w:(pr + 1) * w] = jnp.where(lane < dh, outs[2 * pr],
                                                      pltpu.roll(outs[2 * pr + 1], dh, 1)).astype(o_ref.dtype)

    at = lambda ww: pl.BlockSpec((None, hps, blk, ww), lambda bi, p, i: (bi, p, i, 0))
    whole = pl.BlockSpec((None, hps, nb, blk, w), lambda bi, p, i: (bi, p, 0, 0, 0))
    rows = pl.BlockSpec((None, blk, hps * dh), lambda bi, p, i: (bi, i, p))
    return _pcall(body, name=name, grid=(b, h // hps, nb), in_specs=[at(w), whole, whole],
                  out_specs=[rows, at(1)], out_shape=[_sds((b, l, d_model), BF16), _sds((b, h, l, 1), F32)],
                  compiler_params=_cparams(("parallel", "parallel", "parallel")))(qa, kr, vr)


def _flash_bwd(qa, ka, va, doa, d_model, name):
    b, h, l, w = qa.shape
    dh = ATTN_HEAD_DIM
    blk = _attn_blk(l)
    nb = l // blk
    scale = dh ** -0.5
    r5 = lambda t: t.reshape(b, h, nb, blk, w)

    def body(k_ref, v_ref, q_ref, do_ref, dq_ref, dk_ref, dv_ref, dcq_ref, dck_ref, dq_acc):
        j = pl.program_id(2)
        row = lax.broadcasted_iota(jnp.int32, (blk, blk), 0)
        col = lax.broadcasted_iota(jnp.int32, (blk, blk), 1)
        lane = lax.broadcasted_iota(jnp.int32, (1, w), 1)

        @pl.when(j == 0)
        def _():
            dq_acc[...] = jnp.zeros_like(dq_acc)

        def contrib(i, masked, carry):
            out = []
            for e in range(2):
                qv, dov = q_ref[e, i], do_ref[e, i]
                p = jnp.exp(_dot_nt(qv, k_ref[e]))
                if masked:
                    p = jnp.where(col <= row, p, 0.0)
                ds = (p * _dot_nt(dov, v_ref[e])).astype(BF16)
                dq_acc[e, i] += _dot(ds, k_ref[e])
                out += [carry[2 * e] + _dot_tn(ds, qv), carry[2 * e + 1] + _dot_tn(p.astype(BF16), dov)]
            return tuple(out)

        zero = (jnp.zeros((blk, w), F32),) * 4
        dk0, dv0, dk1, dv1 = lax.fori_loop(j + 1, nb, lambda i, c: contrib(i, False, c), contrib(j, True, zero))
        dk_ref[...] = jnp.where(lane < dh, dk0, pltpu.roll(dk1, dh, 1)).astype(dk_ref.dtype)
        dv_ref[...] = jnp.where(lane < dh, dv0, pltpu.roll(dv1, dh, 1)).astype(dv_ref.dtype)
        for e, dk in enumerate((dk0, dk1)):
            dck_ref[e] = jnp.sum(jnp.where(lane == dh + _AUG_ONE, dk, 0.0), axis=1, keepdims=True)

        @pl.when(j == nb - 1)
        def _():
            for ib in range(nb):
                rs = pl.ds(ib * blk, blk)
                dq0, dq1 = dq_acc[0, ib], dq_acc[1, ib]
                dq_ref[rs, :] = (jnp.where(lane < dh, dq0, pltpu.roll(dq1, dh, 1)) * scale).astype(dq_ref.dtype)
                for e, dq in enumerate((dq0, dq1)):
                    dcq_ref[e, rs, :] = jnp.sum(jnp.where(lane == dh + _AUG_C, dq, 0.0), axis=1, keepdims=True)

    at, whole, rows = _pair_specs(blk, nb)
    seq_rows = pl.BlockSpec((None, l, ATTN_W), lambda bi, p, j: (bi, 0, p))
    seq_col = pl.BlockSpec((None, 2, l, 1), lambda bi, p, j: (bi, p, 0, 0))
    act = _sds((b, l, d_model), BF16)
    col1 = _sds((b, h, l, 1), F32)
    return _pcall(body, name=name, grid=(b, h // 2, nb), in_specs=[at(w), at(w), whole, whole],
                  out_specs=[seq_rows, rows, rows, seq_col, at(1)], out_shape=[act, act, act, col1, col1],
                  scratch_shapes=[pltpu.VMEM((2, nb, blk, w), F32)],
                  compiler_params=_cparams(("parallel", "parallel", "arbitrary")))(ka, va, r5(qa), r5(doa))


def _ssd_dims(d_ssd):
    heads = d_ssd // SSD_HEAD_DIM
    return heads, heads // SSD_GROUPS, d_ssd // SSD_GROUPS


def _ssd_specs(l, ds, seq_map):
    q = SSD_CHUNK
    gn = SSD_GROUPS * SSD_STATE
    row3 = lambda w, cb: pl.BlockSpec((None, q, w), lambda i, c, cb=cb: (i, seq_map(c), cb))
    return dict(
        xs=row3(ds, 0), bm=row3(gn, ds // gn), cm=row3(gn, ds // gn + 1), z=row3(ds, 0), dt=row3(SMALL_W, 0),
        da=row3(SMALL_W, 0), dat=pl.BlockSpec((None, SMALL_W, q), lambda i, c: (i, 0, seq_map(c))),
        e=pl.BlockSpec((SMALL_W, ds), lambda i, c: (0, 0)), et=pl.BlockSpec((ds, SMALL_W), lambda i, c: (0, 0)),
        vec=pl.BlockSpec((1, ds), lambda i, c: (0, 0)), vec128=pl.BlockSpec((1, SMALL_W), lambda i, c: (0, 0)),
        hin=pl.BlockSpec((None, None, SSD_STATE, ds), lambda i, c: (i, seq_map(c), 0, 0)))


def _ssd_common(da, dat, dt, e_mat, xs):
    q = SSD_CHUNK
    row = lax.broadcasted_iota(jnp.int32, (q, q), 0)
    col = lax.broadcasted_iota(jnp.int32, (q, q), 1)
    lower = row >= col
    cs = _dot_sel(lower.astype(F32), da, 'a')
    cst = _dot_sel(dat, (row <= col).astype(F32), 'b')
    dtx = _dot_sel(dt, e_mat, 'b')
    csx = _dot_sel(cs, e_mat, 'b')
    rowx = lax.broadcasted_iota(jnp.int32, csx.shape, 0)
    totx = jnp.sum(jnp.where(rowx == q - 1, csx, 0.0), axis=0, keepdims=True)
    xf = xs.astype(F32)
    return lower, cs, cst, dtx, csx, totx, xf, xf * dtx


def _ssd_fwd(xbc, z, dt, da, dat, e_mat, dx, nw, name):
    b, l, _ = xbc.shape
    ds = z.shape[2]
    heads, hpg, gw = _ssd_dims(ds)
    q, n = SSD_CHUNK, SSD_STATE
    nc = l // q
    hcol0 = ATTN_HEADS

    def body(xs_ref, bm_ref, cm_ref, z_ref, dt_ref, da_ref, dat_ref, e_ref, dx_ref, nw_ref, y_ref, yraw_ref, hin_ref,
             hst, ydiag):
        c = pl.program_id(1)

        @pl.when(c == 0)
        def _():
            hst[...] = jnp.zeros_like(hst)

        hin = hst[...]
        hin_ref[...] = hin
        lower, cs, cst, dtx, csx, totx, xf, xdt = _ssd_common(da_ref[...], dat_ref[...], dt_ref[...], e_ref[...],
                                                               xs_ref[...])
        bm, cm = bm_ref[...], cm_ref[...]
        dec_end = jnp.exp(totx - csx)
        for g in range(SSD_GROUPS):
            gs = slice(g * gw, (g + 1) * gw)
            bg, cg = bm[:, g * n:(g + 1) * n], cm[:, g * n:(g + 1) * n]
            cb = _dot_nt(cg, bg)
            for e in range(hpg):
                hh = g * hpg + e
                cc = hcol0 + hh
                lm = jnp.exp(jnp.where(lower, cs[:, cc:cc + 1] - cst[cc:cc + 1, :], NEG))
                hs = slice(hh * SSD_HEAD_DIM, (hh + 1) * SSD_HEAD_DIM)
                ydiag[:, hs] = _dot((cb * lm).astype(BF16), xdt[:, hs].astype(BF16))
            sg = _dot_tn(bg, (xdt[:, gs] * dec_end[:, gs]).astype(BF16))
            hst[:, gs] = jnp.exp(totx[:, gs]) * hin[:, gs] + sg
            ydiag[:, gs] += _dot(cg, hin[:, gs].astype(BF16)) * jnp.exp(csx[:, gs])
        yraw = ydiag[...] + dx_ref[...] * xf
        yraw_ref[...] = yraw.astype(yraw_ref.dtype)
        yg = yraw * _silu(z_ref[...].astype(F32))
        nwv = nw_ref[...]
        for g in range(SSD_GROUPS):
            gs = slice(g * gw, (g + 1) * gw)
            r = lax.rsqrt(jnp.mean(yg[:, gs] * yg[:, gs], axis=1, keepdims=True) + NORM_EPS)
            y_ref[:, gs] = (yg[:, gs] * r * nwv[:, gs]).astype(y_ref.dtype)

    sp = _ssd_specs(l, ds, lambda c: c)
    return _pcall(body, name=name, grid=(b, nc),
                  in_specs=[sp['xs'], sp['bm'], sp['cm'], sp['z'], sp['dt'], sp['da'], sp['dat'], sp['e'], sp['vec'],
                            sp['vec']],
                  out_specs=[sp['z'], sp['z'], sp['hin']],
                  out_shape=[_sds((b, l, ds), BF16), _sds((b, l, ds), BF16), _sds((b, nc, n, ds), F32)],
                  scratch_shapes=[pltpu.VMEM((n, ds), F32), pltpu.VMEM((q, ds), F32)],
                  compiler_params=_cparams(("parallel", "arbitrary")))(xbc, xbc, xbc, z, dt, da, dat, e_mat, dx, nw)


def _ssd_bwd(xbc, z, dt, da, dat, e_mat, et_mat, dx, nw, a128, yraw, hin, dy, name):
    b, l, dxw = xbc.shape
    ds = z.shape[2]
    heads, hpg, gw = _ssd_dims(ds)
    q, n = SSD_CHUNK, SSD_STATE
    gn = SSD_GROUPS * n
    nc = l // q
    hcol0 = ATTN_HEADS

    def body(xs_ref, bm_ref, cm_ref, z_ref, dt_ref, da_ref, dat_ref, e_ref, et_ref, dx_ref, nw_ref, a_ref, yraw_ref,
             hin_ref, dy_ref, dxs_ref, dbm_ref, dcm_ref, dz_ref, ddt_ref, dd_ref, dnw_ref, dap_ref, dhs, dxdt, dcsx,
             dtotx):
        i, c = pl.program_id(0), pl.program_id(1)

        @pl.when(c == 0)
        def _():
            dhs[...] = jnp.zeros_like(dhs)

        @pl.when((i == 0) & (c == 0))
        def _():
            dd_ref[...] = jnp.zeros_like(dd_ref)
            dnw_ref[...] = jnp.zeros_like(dnw_ref)
            dap_ref[...] = jnp.zeros_like(dap_ref)

        dtv = dt_ref[...]
        lower, cs, cst, dtx, csx, totx, xf, xdt = _ssd_common(da_ref[...], dat_ref[...], dtv, e_ref[...], xs_ref[...])
        upper = jnp.logical_not(lower) | (lax.broadcasted_iota(jnp.int32, (q, q), 0)
                                          == lax.broadcasted_iota(jnp.int32, (q, q), 1))
        bm, cm = bm_ref[...], cm_ref[...]
        ecs, dec_end, etot = jnp.exp(csx), jnp.exp(totx - csx), jnp.exp(totx)
        yraw = yraw_ref[...].astype(F32)
        zv = z_ref[...].astype(F32)
        sz = _silu(zv)
        yg = yraw * sz
        dyn_ = dy_ref[...].astype(F32)
        nwv = nw_ref[...]
        dygs, dnws = [], []
        for g in range(SSD_GROUPS):
            gs = slice(g * gw, (g + 1) * gw)
            r = lax.rsqrt(jnp.mean(yg[:, gs] * yg[:, gs], axis=1, keepdims=True) + NORM_EPS)
            yn = yg[:, gs] * r
            dn = dyn_[:, gs] * nwv[:, gs]
            dnws.append(jnp.sum(dyn_[:, gs] * yn, axis=0, keepdims=True))
            dygs.append(r * (dn - yn * jnp.mean(dn * yn, axis=1, keepdims=True)))
        dyg = jnp.concatenate(dygs, axis=1)
        dnw_ref[...] += jnp.concatenate(dnws, axis=1)
        dz_ref[...] = (dyg * yraw * _dsilu(zv)).astype(dz_ref.dtype)
        dyv = dyg * sz
        dd_ref[...] += jnp.sum(dyv * xf, axis=0, keepdims=True)
        hin, dh = hin_ref[...], dhs[...]
        lane128 = lax.broadcasted_iota(jnp.int32, (1, SMALL_W), 1)
        dcs = jnp.zeros((q, SMALL_W), F32)
        for g in range(SSD_GROUPS):
            gs = slice(g * gw, (g + 1) * gw)
            bg, cg = bm[:, g * n:(g + 1) * n], cm[:, g * n:(g + 1) * n]
            hg, dhg = hin[:, gs], dh[:, gs]
            hgb, dsb = hg.astype(BF16), dhg.astype(BF16)
            yoff = _dot(cg, hgb) * ecs[:, gs]
            dch = (dyv[:, gs] * ecs[:, gs]).astype(BF16)
            dcg = _dot_nt(dch, hgb)
            dhs[:, gs] = _dot_tn(cg, dch) + etot[:, gs] * dhg
            zg = xdt[:, gs] * dec_end[:, gs]
            dzz = _dot(bg, dsb)
            dbg = _dot_nt(zg.astype(BF16), dsb)
            dxdt_g = dzz * dec_end[:, gs]
            w_end = dzz * zg
            dtotx[:, gs] = jnp.sum(dhg * hg, axis=0, keepdims=True) * etot[:, gs] + jnp.sum(w_end, axis=0, keepdims=True)
            dcsx[:, gs] = dyv[:, gs] * yoff - w_end
            cb, cbt = _dot_nt(cg, bg), _dot_nt(bg, cg)
            dgm = jnp.zeros((q, q), F32)
            for e in range(hpg):
                hh = g * hpg + e
                cc = hcol0 + hh
                ccol, crow = cs[:, cc:cc + 1], cst[cc:cc + 1, :]
                lm = jnp.exp(jnp.where(lower, ccol - crow, NEG))
                lmt = jnp.exp(jnp.where(upper, crow - ccol, NEG))
                mm, mt = cb * lm, cbt * lmt
                hs = slice(hh * SSD_HEAD_DIM, (hh + 1) * SSD_HEAD_DIM)
                dye, xe = dyv[:, hs].astype(BF16), xdt[:, hs].astype(BF16)
                dm, dmt = _dot_nt(dye, xe), _dot_nt(xe, dye)
                dxdt[:, hs] = dxdt_g[:, e * SSD_HEAD_DIM:(e + 1) * SSD_HEAD_DIM] + _dot(mt.astype(BF16), dye)
                dgm = dgm + dm * lm
                rs = jnp.sum(dm * mm, axis=1, keepdims=True) - jnp.sum(dmt * mt, axis=1, keepdims=True)
                dcs = dcs + rs * (lane128 == cc).astype(F32)
            dgb = dgm.astype(BF16)
            dcm_ref[:, g * n:(g + 1) * n] = (dcg + _dot(dgb, bg)).astype(dcm_ref.dtype)
            dbm_ref[:, g * n:(g + 1) * n] = (dbg + _dot_tn(dgb, cg)).astype(dbm_ref.dtype)
        dxd = dxdt[...]
        dxs_ref[...] = (dx_ref[...] * dyv + dxd * dtx).astype(dxs_ref.dtype)
        et = et_ref[...]
        ddt = _dot_sel(dxd * xf, et, 'b')
        dtot128 = _dot_sel(jnp.broadcast_to(dtotx[...], (8, ds)), et, 'b')[0:1, :]
        row128 = lax.broadcasted_iota(jnp.int32, (q, SMALL_W), 0)
        dcs = dcs + _dot_sel(dcsx[...], et, 'b') + jnp.where(row128 == q - 1, dtot128, 0.0)
        dda = _dot_sel(upper.astype(F32), dcs, 'a')
        ddt_ref[...] = ddt + dda * a_ref[...]
        dap_ref[...] += jnp.sum(dda * dtv, axis=0, keepdims=True)

    rev = lambda c: nc - 1 - c
    sp = _ssd_specs(l, ds, rev)
    row3 = lambda w: pl.BlockSpec((None, q, w), lambda i, c: (i, rev(c), 0))
    acc = lambda w: pl.BlockSpec((1, w), lambda i, c: (0, 0))
    return _pcall(body, name=name, grid=(b, nc),
                  in_specs=[sp['xs'], sp['bm'], sp['cm'], sp['z'], sp['dt'], sp['da'], sp['dat'], sp['e'], sp['et'],
                            sp['vec'], sp['vec'], sp['vec128'], sp['z'], sp['hin'], sp['z']],
                  out_specs=[row3(ds), row3(gn), row3(gn), row3(ds), row3(SMALL_W), acc(ds), acc(ds), acc(SMALL_W)],
                  out_shape=[_sds((b, l, ds), BF16), _sds((b, l, gn), BF16), _sds((b, l, gn), BF16), _sds((b, l, ds), BF16),
                             _sds((b, l, SMALL_W), F32), _sds((1, ds), F32), _sds((1, ds), F32), _sds((1, SMALL_W), F32)],
                  scratch_shapes=[pltpu.VMEM((n, ds), F32), pltpu.VMEM((q, ds), F32), pltpu.VMEM((q, ds), F32),
                                  pltpu.VMEM((1, ds), F32)],
                  compiler_params=_cparams(("arbitrary", "arbitrary")))(
                      xbc, xbc, xbc, z, dt, da, dat, e_mat, et_mat, dx, nw, a128, yraw, hin, dy)


_GROUP_SIZE = {'c': 2, 'xy': 4, 'xyc': 8}
_LOCAL_SPLIT = 16


def _exchange(src, group, scatter, name, nsplit=1, copy_own=True):
    n = _GROUP_SIZE[group]
    rows, width = src.shape[-2:]
    assert src.ndim == (3 if scatter else 2)
    while rows % (8 * nsplit):
        nsplit //= 2
    crow = rows // nsplit
    nlocal = _LOCAL_SPLIT
    while rows % (8 * nlocal):
        nlocal //= 2
    lrow = rows // nlocal

    def body(src_ref, out_ref, send_sems, recv_sems, local_sems):
        x, y, c = lax.axis_index("x"), lax.axis_index("y"), lax.axis_index("c")
        if group == 'c':
            rank = c
            dev = lambda r: (x, y, r)
        elif group == 'xy':
            rank = 2 * x + y
            dev = lambda r: (r // 2, r % 2, c)
        else:
            rank = 4 * x + 2 * y + c
            dev = lambda r: (r // 4, (r // 2) % 2, r % 2)

        def mine_for(r, ck):
            piece = src_ref.at[r] if scatter else src_ref
            return piece.at[pl.ds(ck * crow, crow)]

        def copy(k, ck, pr, dst_rank):
            return pltpu.make_async_remote_copy(
                src_ref=mine_for(pr, ck), dst_ref=out_ref.at[dst_rank].at[pl.ds(ck * crow, crow)],
                send_sem=send_sems.at[k * nsplit + ck], recv_sem=recv_sems.at[k * nsplit + ck], device_id=dev(pr),
                device_id_type=pl.DeviceIdType.MESH)

        locals_ = []
        if copy_own:
            own = src_ref.at[rank] if scatter else src_ref
            for ck in range(nlocal):
                rs = pl.ds(ck * lrow, lrow)
                locals_.append(pltpu.make_async_copy(own.at[rs], out_ref.at[rank].at[rs], local_sems.at[ck]))
                locals_[-1].start()
        peers = [jnp.bitwise_xor(rank, k + 1) for k in range(n - 1)]
        sends = [copy(k, ck, pr, rank) for ck in range(nsplit) for k, pr in enumerate(peers)]
        for cp in sends:
            cp.start()
        for ck in range(nsplit):
            for k, pr in enumerate(peers):
                copy(k, ck, pr, pr).wait_recv()
        for cp in sends:
            cp.wait_send()
        for cp in locals_:
            cp.wait()

    return _pcall(body, name=name, in_specs=[pl.BlockSpec(memory_space=pl.ANY)],
                  out_specs=pl.BlockSpec(memory_space=pl.ANY), out_shape=_sds((n, rows, width), src.dtype),
                  scratch_shapes=[pltpu.SemaphoreType.DMA(((n - 1) * nsplit,)),
                                  pltpu.SemaphoreType.DMA(((n - 1) * nsplit,)),
                                  pltpu.SemaphoreType.DMA((nlocal,))])(src)


def _exchange_multi(srcs, group, scatter, name, single=False, min_copies=16):
    n = _GROUP_SIZE[group]
    assert not single or n == 2
    na = len(srcs)
    shapes = [tuple(s.shape[-2:]) for s in srcs]
    want = max(1, -(-min_copies // (na * (n - 1))))
    splits = []
    for (rows, _), s in zip(shapes, srcs):
        quant = 8 * (4 // s.dtype.itemsize)
        k = want
        while k > 1 and rows % (quant * k):
            k -= 1
        splits.append(k)
    offs = [int(v) for v in np.cumsum([0] + [(n - 1) * k for k in splits])]

    def body(*refs):
        src_refs, out_refs = refs[:na], refs[na:2 * na]
        send_sems, recv_sems = refs[2 * na:]
        x, y, c = lax.axis_index("x"), lax.axis_index("y"), lax.axis_index("c")
        if group == 'c':
            rank = c
            dev = lambda r: (x, y, r)
        elif group == 'xy':
            rank = 2 * x + y
            dev = lambda r: (r // 2, r % 2, c)
        else:
            rank = 4 * x + 2 * y + c
            dev = lambda r: (r // 4, (r // 2) % 2, r % 2)
        peers = [jnp.bitwise_xor(rank, k + 1) for k in range(n - 1)]

        def copy(a, k, ck, dst_rank):
            crow = shapes[a][0] // splits[a]
            rs = pl.ds(ck * crow, crow)
            piece = src_refs[a].at[peers[k]] if scatter else src_refs[a]
            dst = out_refs[a] if single else out_refs[a].at[dst_rank]
            sem = offs[a] + k * splits[a] + ck
            return pltpu.make_async_remote_copy(src_ref=piece.at[rs], dst_ref=dst.at[rs], send_sem=send_sems.at[sem],
                                                recv_sem=recv_sems.at[sem], device_id=dev(peers[k]),
                                                device_id_type=pl.DeviceIdType.MESH)

        todo = [(a, k, ck) for a in range(na) for ck in range(splits[a]) for k in range(n - 1)]
        sends = [copy(a, k, ck, rank) for a, k, ck in todo]
        for cp in sends:
            cp.start()
        for a, k, ck in todo:
            copy(a, k, ck, peers[k]).wait_recv()
        for cp in sends:
            cp.wait_send()

    any_spec = pl.BlockSpec(memory_space=pl.ANY)
    out_shape = [_sds(sh if single else (n,) + sh, s.dtype) for sh, s in zip(shapes, srcs)]
    return _pcall(body, name=name, in_specs=[any_spec] * na, out_specs=[any_spec] * na, out_shape=out_shape,
                  scratch_shapes=[pltpu.SemaphoreType.DMA((offs[-1],)), pltpu.SemaphoreType.DMA((offs[-1],))])(*srcs)


def _sum_slots(arr, out_dtype, name):
    n, rows, cols = arr.shape
    tm = _pick(rows, [c for c in (384, 256, 128, 64, 32, 16, 8) if c * cols <= _ROWWISE_TILE_ELEMS or c == 8])

    def body(*refs):
        acc = refs[0][...].astype(F32)
        for r in refs[1:n]:
            acc = acc + r[...].astype(F32)
        refs[n][...] = acc.astype(refs[n].dtype)

    return _pcall(body, name=name, grid=(rows // tm,),
                  in_specs=[pl.BlockSpec((None, tm, cols), lambda i, j=j: (j, i, 0)) for j in range(n)],
                  out_specs=pl.BlockSpec((tm, cols), lambda i: (i, 0)), out_shape=_sds((rows, cols), out_dtype),
                  compiler_params=_cparams(("parallel",)))(*([arr] * n))


def _dims():
    d = D_MODEL
    h = ATTN_HEADS
    d_ssd = d
    d_xbc = d_ssd + 2 * SSD_GROUPS * SSD_STATE
    sizes = (d, d, d, h, d_ssd, d_xbc, d_ssd // SSD_HEAD_DIM, d, d, 3 * d)
    return d, h, d_ssd, d_xbc, sizes


def _w_in_split(w):
    d, h, d_ssd, d_xbc, sizes = _dims()
    off = np.concatenate([[0], np.cumsum(sizes)])
    seg = lambda i: w[..., off[i]:off[i + 1]]
    main = jnp.concatenate([seg(0), seg(1), seg(2), seg(4), seg(5), seg(7), seg(8), seg(9)], axis=-1)
    pad = jnp.zeros(w.shape[:-1] + (SMALL_W - sizes[3] - sizes[6],), w.dtype)
    small = jnp.concatenate([seg(3), seg(6), pad], axis=-1)
    return main, small


def _w_in_merge(main, small):
    d, h, d_ssd, d_xbc, sizes = _dims()
    order = (0, 1, 2, 4, 5, 7, 8, 9)
    moff = np.concatenate([[0], np.cumsum([sizes[i] for i in order])])
    pieces = {i: main[..., moff[j]:moff[j + 1]] for j, i in enumerate(order)}
    pieces[3] = small[..., :sizes[3]]
    pieces[6] = small[..., sizes[3]:sizes[3] + sizes[6]]
    return jnp.concatenate([pieces[i] for i in range(10)], axis=-1)


def _main_offsets():
    d, h, d_ssd, d_xbc, sizes = _dims()
    names = ('q', 'k', 'v', 'z', 'xbc', 'xr', 'gate', 'merge')
    widths = (d, d, d, d_ssd, d_xbc, d, d, 3 * d)
    off = np.concatenate([[0], np.cumsum(widths)])
    return {nm: (int(off[i]), int(off[i + 1])) for i, nm in enumerate(names)}


def _block_diag(w):
    nb, s, _ = w.shape
    eye = jnp.eye(nb, dtype=w.dtype)
    return (eye[:, None, :, None] * w[:, :, None, :]).reshape(nb * s, nb * s)


def _diag_blocks(wd, nb):
    s = wd.shape[0] // nb
    return jnp.stack([wd[i * s:(i + 1) * s, i * s:(i + 1) * s] for i in range(nb)])


def _vec128(*parts):
    v = jnp.concatenate([p.astype(F32) for p in parts])
    return jnp.pad(v, (0, SMALL_W - v.shape[0]))[None, :]


def _ffn_fwd(h, gnorm, w, tag):
    xn = _rms_fwd(h, gnorm[None, :], f"{tag}_norm")
    g, u, act = _mm_swiglu(xn, w['wg'], w['wu'], f"{tag}_gu")
    out = _mm_nn(act, w['wd'], F32, res=h, alpha=0.5, name=f"{tag}_down")
    return out, (h, xn, g, u, act)


def _ffn_bwd(dout, saved, gnorm, w, tag):
    h, xn, g, u, act = saved
    dg, du = _mm_dswiglu(dout, w['wd_t'], g, u, 0.5, f"{tag}_dgu")
    dwd = _mm_tn(act, dout, alpha=0.5, name=f"{tag}_dwd")
    dwgu = jnp.concatenate([_mm_tn(xn, dg, name=f"{tag}_dwg"), _mm_tn(xn, du, name=f"{tag}_dwu")], axis=1)
    dxn = _mm_nn(dg, w['wg_t'], F32, name=f"{tag}_dxn_g")
    dxn = _mm_nn(du, w['wu_t'], F32, res=dxn, name=f"{tag}_dxn_u")
    dh, dgn = _rms_bwd(h, dxn, dout, gnorm[None, :], f"{tag}_dnorm")
    return dh, dgn[0], dwgu, dwd


def _mixer_fwd(h, p, b, l):
    d, nh, d_ssd, d_xbc, sizes = _dims()
    t = b * l
    off = _main_offsets()
    xn = _rms_fwd(h, p['mix_norm'][None, :], "mix_norm")
    pm = _mm_nn(xn, p['w_main'], BF16, name="mix_in_main")
    ps = _mm_nn(xn, p['w_small'], F32, name="mix_in_small")
    col = lambda nm: pm[:, off[nm][0]:off[nm][1]]
    heads_ssd = d_ssd // SSD_HEAD_DIM
    a_neg = -jnp.exp(p['ssd_a_log'])
    fb = _vec128(p['fox_forget_bias'])
    dtb = _vec128(jnp.zeros((nh,), F32), p['ssd_dt_bias'])
    a128 = _vec128(jnp.zeros((nh,), F32), a_neg)

    def prep(_, v, fbv, dtbv, av):
        lane = lax.broadcasted_iota(jnp.int32, (1, SMALL_W), 1)
        logf = jnp.where(lane < nh, -_softplus(-(v + fbv)), 0.0)
        dtv = jnp.where((lane >= nh) & (lane < nh + heads_ssd), _softplus(v + dtbv), 0.0)
        return logf, dtv, dtv * av
    logf, dt, da = _rowwise(prep, [ps], [(SMALL_W, F32)] * 3, bcast=[fb, dtb, a128], name="mix_prep")

    cum = _cumsum_seq(logf.reshape(b, l, SMALL_W), False, "fox_cumsum")
    qa, ka, va = _attn_prep(pm.reshape(b, l, -1), cum, (off['q'][0], off['k'][0], off['v'][0]), "fox_prep")
    y_a3, lse = _flash_fwd(qa, ka, va, d, "fox_fwd")
    y_a = y_a3.reshape(t, d)

    xbc = col('xbc').reshape(b, l, d_xbc)
    pre_b, xbc_act = _conv_fwd(xbc, p['ssd_conv_w'], p['ssd_conv_b'][None, :], BF16, "ssd_conv", with_silu=True)
    z = col('z').reshape(b, l, d_ssd)
    dt3, da3 = dt.reshape(b, l, SMALL_W), da.reshape(b, l, SMALL_W)
    dat3 = da3.transpose(0, 2, 1)
    e_mat = _expand_matrix(nh, heads_ssd)
    dx = jnp.repeat(p['ssd_d'], SSD_HEAD_DIM)[None, :]
    nw = p['ssd_norm'][None, :]
    y_b3, yraw, hin = _ssd_fwd(xbc_act, z, dt3, da3, dat3, e_mat, dx, nw, "ssd_fwd")
    y_b = y_b3.reshape(t, d_ssd)

    xr = col('xr').reshape(b, l, d)
    xc = _conv_fwd(xr, p['lru_conv_w'], p['lru_conv_b'][None, :], F32, "lru_conv").reshape(t, d)
    pre_ri = _mm_nn(xc, p['lru_w_ri'], F32, name="lru_gates")
    lvec = (p['lru_b_a'][None, :], p['lru_b_x'][None, :], p['lru_lambda'][None, :])
    a_l, u_l = _rowwise(_lru_point_fwd, [pre_ri, xc], [(d, F32)] * 2, bcast=lvec, name="lru_point", period=l)
    gate = col('gate')
    hs, y_c = _linear_scan(a_l.reshape(b, l, d), u_l.reshape(b, l, d), False, "lru_scan", gate=gate.reshape(b, l, d))
    hs, y_c = hs.reshape(t, d), y_c.reshape(t, d)

    ba = _mm_nn(y_a, p['w_branch_attn'], BF16, name="branch_attn")
    bb = _mm_nn(y_b, p['w_branch_ssd'], BF16, name="branch_ssd")
    bc = _mm_nn(y_c, p['w_branch_lru'], BF16, name="branch_lru")
    mg = col('merge')
    mixed = _merge_fwd(mg, ba, bb, bc, "merge")
    out = _mm_nn(mixed, p['w_out'], F32, res=h, name="mix_out")
    saved = dict(h=h, xn=xn, ps=ps, fb=fb, dtb=dtb, a128=a128, qa=qa, ka=ka, va=va, lse=lse,
                 xbc=xbc, pre_b=pre_b, xbc_act=xbc_act, z=z, dt3=dt3, da3=da3, dat3=dat3, e_mat=e_mat, dx=dx, nw=nw,
                 yraw=yraw, hin=hin, xr=xr, xc=xc, pre_ri=pre_ri, lvec=lvec, a_l=a_l, hs=hs, gate=gate, y_a=y_a, y_b=y_b,
                 y_c=y_c, ba=ba, bb=bb, bc=bc, mg=mg, mixed=mixed)
    return out, saved


def _expand_matrix(nh, heads_ssd):
    e = np.zeros((SMALL_W, heads_ssd * SSD_HEAD_DIM), np.float32)
    for hh in range(heads_ssd):
        e[nh + hh, hh * SSD_HEAD_DIM:(hh + 1) * SSD_HEAD_DIM] = 1.0
    return jnp.asarray(e)


def _lru_gates(pre, xc, bav, bxv, lamv, pos):
    d = xc.shape[1]
    r = _sigmoid(pre[:, :d] + bav)
    i = _sigmoid(pre[:, d:] + bxv)
    ls = -_softplus(-lamv)
    la = LRU_C * r * ls
    a = jnp.exp(la)
    mult = jnp.where(pos == 0, 1.0, jnp.sqrt(-_expm1(2.0 * la)))
    return r, i, ls, a, mult


def _lru_point_fwd(pos, pre, xc, bav, bxv, lamv):
    r, i, ls, a, mult = _lru_gates(pre, xc, bav, bxv, lamv, pos)
    return a, mult * (i * xc)


def _lru_point_bwd(pos, g, hprev, pre, xc, bav, bxv, lamv):
    r, i, ls, a, mult = _lru_gates(pre, xc, bav, bxv, lamv, pos)
    da = g * hprev
    di = g * mult * xc
    dxc = g * mult * i
    dmult = jnp.where(pos == 0, 0.0, g * i * xc)
    dla = da * a - dmult * (a * a) / mult
    dpre_r = dla * (LRU_C * ls) * r * (1.0 - r)
    dpre_i = di * i * (1.0 - i)
    dlam = jnp.sum(dla * (LRU_C * r), axis=0, keepdims=True) * _sigmoid(-lamv)
    return (jnp.concatenate([dpre_r, dpre_i], axis=1), dxc, dlam, jnp.sum(dpre_r, axis=0, keepdims=True),
            jnp.sum(dpre_i, axis=0, keepdims=True))


def _mixer_bwd(dout, s, p, b, l):
    d, nh, d_ssd, d_xbc, sizes = _dims()
    t = b * l
    heads_ssd = d_ssd // SSD_HEAD_DIM
    g = {}
    dmixed = _mm_nn(dout, p['w_out_t'], BF16, name="mix_out_dx")
    g['w_out'] = _mm_tn(s['mixed'], dout, name="mix_out_dw")
    dba, dbb, dbc, dmerge = _merge_bwd(s['mg'], s['ba'], s['bb'], s['bc'], dmixed, "merge_bwd")
    g['w_branch_attn'] = _mm_tn(s['y_a'], dba, name="branch_attn_dw")
    g['w_branch_ssd'] = _mm_tn(s['y_b'], dbb, name="branch_ssd_dw")
    g['w_branch_lru'] = _mm_tn(s['y_c'], dbc, name="branch_lru_dw")
    dy_a = _mm_nn(dba, p['w_branch_attn_t'], BF16, name="branch_attn_dx")
    dy_b = _mm_nn(dbb, p['w_branch_ssd_t'], BF16, name="branch_ssd_dx")
    dy_c = _mm_nn(dbc, p['w_branch_lru_t'], F32, name="branch_lru_dx")

    dgate, dhs = _rowwise(lambda _, dv, hv, gv: (dv * hv * _dgelu(gv.astype(F32)), dv * _gelu(gv.astype(F32))),
                          [dy_c, s['hs'], s['gate']], [(d, BF16), (d, F32)], name="lru_out_bwd")
    a3 = s['a_l'].reshape(b, l, d)
    a_next = jnp.concatenate([a3[:, 1:], jnp.zeros((b, 1, d), F32)], axis=1)
    gs = _linear_scan(a_next, dhs.reshape(b, l, d), True, "lru_scan_bwd").reshape(t, d)
    h3 = s['hs'].reshape(b, l, d)
    hprev = jnp.concatenate([jnp.zeros((b, 1, d), F32), h3[:, :-1]], axis=1).reshape(t, d)
    dpre_ri, dxc0, dlam, dba_, dbx_ = _rowwise(_lru_point_bwd, [gs, hprev, s['pre_ri'], s['xc']],
                                               [(2 * d, BF16), (d, F32)], bcast=s['lvec'],
                                               reds=[(1, d)] * 3, name="lru_point_bwd", period=l)
    g['lru_lambda'], g['lru_b_a'], g['lru_b_x'] = dlam[0], dba_[0], dbx_[0]
    dxc = _mm_nn(dpre_ri, p['lru_w_ri_t'], BF16, res=dxc0, name="lru_gates_dx")
    dw_ri = _mm_tn(s['xc'], dpre_ri, name="lru_gates_dw")
    g['lru_w_a'] = _diag_blocks(dw_ri[:, :d], LRU_BLOCKS)
    g['lru_w_x'] = _diag_blocks(dw_ri[:, d:], LRU_BLOCKS)
    dxr, dwl = _conv_bwd(s['xr'], dxc.reshape(b, l, d), p['lru_conv_w'], "lru_conv_bwd")
    g['lru_conv_w'], g['lru_conv_b'] = dwl[:CONV_K], dwl[CONV_K]

    et_mat = s['e_mat'].T
    dxs, dbm, dcm, dz, ddt, dd_l, dnw, dap = _ssd_bwd(s['xbc_act'], s['z'], s['dt3'], s['da3'], s['dat3'], s['e_mat'],
                                                      et_mat, s['dx'], s['nw'], s['a128'], s['yraw'], s['hin'],
                                                      dy_b.reshape(b, l, d_ssd), "ssd_bwd")
    g['ssd_d'] = dd_l.reshape(heads_ssd, SSD_HEAD_DIM).sum(axis=1)
    g['ssd_norm'] = dnw[0]
    g['ssd_a_log'] = dap[0, nh:nh + heads_ssd] * (-jnp.exp(p['ssd_a_log']))
    dxbc_act = jnp.concatenate([dxs, dbm, dcm], axis=2).reshape(t, d_xbc)
    dpre_b = _rowwise(lambda _, dv, pv: dv.astype(F32) * _dsilu(pv.astype(F32)),
                      [dxbc_act, s['pre_b'].reshape(t, d_xbc)], [(d_xbc, BF16)], name="ssd_conv_act_bwd")[0]
    dxbc, dws = _conv_bwd(s['xbc'], dpre_b.reshape(b, l, d_xbc), p['ssd_conv_w'], "ssd_conv_bwd")
    g['ssd_conv_w'], g['ssd_conv_b'] = dws[:CONV_K], dws[CONV_K]

    qa2, doa = _attn_prep_bwd(dy_a.reshape(b, l, d), s['y_a'].reshape(b, l, d), s['qa'], s['lse'], "fox_prep_bwd")
    dq3, dk3, dv3, dcq, dck = _flash_bwd(qa2, s['ka'], s['va'], doa, d, "fox_bwd")
    dcum = jnp.pad((dcq - dck)[..., 0].transpose(0, 2, 1), ((0, 0), (0, 0), (0, SMALL_W - nh)))
    dlogf = _cumsum_seq(dcum, True, "fox_cumsum_bwd").reshape(t, SMALL_W)

    def prep_bwd(_, v, dlf, ddtv, fbv, dtbv):
        a_ = dlf * _sigmoid(-(v + fbv))
        b_ = ddtv * _sigmoid(v + dtbv)
        return a_ + b_, jnp.sum(a_, axis=0, keepdims=True), jnp.sum(b_, axis=0, keepdims=True)
    dps, dfb, ddtb = _rowwise(prep_bwd, [s['ps'], dlogf, ddt.reshape(t, SMALL_W)], [(SMALL_W, F32)],
                              bcast=[s['fb'], s['dtb']], reds=[(1, SMALL_W)] * 2, name="mix_prep_bwd")
    g['fox_forget_bias'] = dfb[0, :nh]
    g['ssd_dt_bias'] = ddtb[0, nh:nh + heads_ssd]

    dpm = jnp.concatenate([dq3.reshape(t, d), dk3.reshape(t, d), dv3.reshape(t, d),
                           dz.reshape(t, d_ssd), dxbc.reshape(t, d_xbc), dxr.reshape(t, d), dgate, dmerge], axis=1)
    dxn = _mm_nn(dps, p['w_small_t'], F32, name="mix_in_small_dx")
    dxn = _mm_nn(dpm, p['w_main_t'], F32, res=dxn, name="mix_in_main_dx")
    g['w_main'] = _mm_tn(s['xn'], dpm, name="mix_in_main_dw")
    g['w_small'] = _mm_tn(s['xn'], dps, name="mix_in_small_dw")
    dh, dg = _rms_bwd(s['h'], dxn, dout, p['mix_norm'][None, :], "mix_norm_bwd")
    g['mix_norm'] = dg[0]
    return dh, g


def _layer_params(w, li):
    p = {n: w[n][li] for n in WEIGHTS if n not in ('meta_tokens', 'final_norm')}
    bf = lambda a: a.astype(BF16)
    for tag in ('ffn1', 'ffn2'):
        wgu, wd = bf(p[tag + '_w_gate_up']), bf(p[tag + '_w_down'])
        f = wd.shape[0]
        p[tag] = dict(wg=wgu[:, :f], wu=wgu[:, f:], wg_t=wgu[:, :f].T, wu_t=wgu[:, f:].T, wd=wd, wd_t=wd.T)
    wm, ws = _w_in_split(bf(p['w_in']))
    p['w_main'], p['w_main_t'], p['w_small'], p['w_small_t'] = wm, wm.T, ws, ws.T
    for n in ('w_branch_attn', 'w_branch_ssd', 'w_branch_lru', 'w_out'):
        p[n + '_t'] = bf(p[n]).T
        p[n] = bf(p[n])
    wri = jnp.concatenate([_block_diag(p['lru_w_a']), _block_diag(p['lru_w_x'])], axis=1)
    p['lru_w_ri'], p['lru_w_ri_t'] = bf(wri), bf(wri).T
    return p


def _local_step(x, loss_target, w):
    b, seq, d = x.shape
    length = N_META + seq
    l = -(-length // Q_BLOCK) * Q_BLOCK
    t = b * l
    meta = jnp.broadcast_to(w['meta_tokens'].astype(F32)[None], (b, N_META, d))
    h = jnp.concatenate([meta, x, jnp.zeros((b, l - length, d), F32)], axis=1).reshape(t, d)
    tgt = jnp.concatenate([jnp.zeros((b, N_META, d), F32), loss_target, jnp.zeros((b, l - length, d), F32)],
                          axis=1).reshape(t, d)
    params, saves = [], []
    for li in range(DEPTH):
        p = _layer_params(w, li)
        h, s1 = _ffn_fwd(h, p['ffn1_norm'], p['ffn1'], "ffn1")
        h, sm = _mixer_fwd(h, p, b, l)
        h, s2 = _ffn_fwd(h, p['ffn2_norm'], p['ffn2'], "ffn2")
        params.append(p)
        saves.append((s1, sm, s2))
    dh, loss, dgf = _loss_head(h, tgt, w['final_norm'][None, :], l, "loss_head")
    layer_grads = [None] * DEPTH
    for li in reversed(range(DEPTH)):
        p = params[li]
        s1, sm, s2 = saves[li]
        g = {}
        dh, g['ffn2_norm'], g['ffn2_w_gate_up'], g['ffn2_w_down'] = _ffn_bwd(dh, s2, p['ffn2_norm'], p['ffn2'], "ffn2b")
        dh, gm = _mixer_bwd(dh, sm, p, b, l)
        g.update(gm)
        g['w_in'] = _w_in_merge(g.pop('w_main'), g.pop('w_small'))
        dh, g['ffn1_norm'], g['ffn1_w_gate_up'], g['ffn1_w_down'] = _ffn_bwd(dh, s1, p['ffn1_norm'], p['ffn1'], "ffn1b")
        layer_grads[li] = g
    grads = {n: jnp.stack([layer_grads[li][n] for li in range(DEPTH)]) for n in layer_grads[0]}
    for n in ('lru_w_a', 'lru_w_x'):
        grads[n] = grads[n].reshape(w[n].shape)
    dh3 = dh.reshape(b, l, d)
    grads['meta_tokens'] = jnp.sum(dh3[:, :N_META], axis=0)
    grads['final_norm'] = dgf[0]
    return loss, dh3[:, N_META:N_META + seq], grads


def _unflatten(flat, shapes):
    out, o = [], 0
    for sh in shapes:
        n = int(np.prod(sh))
        out.append(flat[o:o + n].reshape(sh))
        o += n
    return out


def kernel(x, meta_tokens, ffn1_norm, ffn1_w_gate_up, ffn1_w_down, mix_norm, w_in, fox_forget_bias, ssd_conv_w, ssd_conv_b, ssd_dt_bias, ssd_a_log, ssd_d, ssd_norm, lru_conv_w, lru_conv_b, lru_w_a, lru_b_a, lru_w_x, lru_b_x, lru_lambda, w_branch_attn, w_branch_ssd, w_branch_lru, w_out, ffn2_norm, ffn2_w_gate_up, ffn2_w_down, final_norm, loss_target, m_meta_tokens, m_ffn1_norm, m_ffn1_w_gate_up, m_ffn1_w_down, m_mix_norm, m_w_in, m_fox_forget_bias, m_ssd_conv_w, m_ssd_conv_b, m_ssd_dt_bias, m_ssd_a_log, m_ssd_d, m_ssd_norm, m_lru_conv_w, m_lru_conv_b, m_lru_w_a, m_lru_b_a, m_lru_w_x, m_lru_b_x, m_lru_lambda, m_w_branch_attn, m_w_branch_ssd, m_w_branch_lru, m_w_out, m_ffn2_norm, m_ffn2_w_gate_up, m_ffn2_w_down, m_final_norm, v_meta_tokens, v_ffn1_norm, v_ffn1_w_gate_up, v_ffn1_w_down, v_mix_norm, v_w_in, v_fox_forget_bias, v_ssd_conv_w, v_ssd_conv_b, v_ssd_dt_bias, v_ssd_a_log, v_ssd_d, v_ssd_norm, v_lru_conv_w, v_lru_conv_b, v_lru_w_a, v_lru_b_a, v_lru_w_x, v_lru_b_x, v_lru_lambda, v_w_branch_attn, v_w_branch_ssd, v_w_branch_lru, v_w_out, v_ffn2_norm, v_ffn2_w_gate_up, v_ffn2_w_down, v_final_norm):
    args = locals()
    wloc = {n: args[n] for n in WEIGHTS}
    mloc = {n: args['m_' + n] for n in WEIGHTS}
    vloc = {n: args['v_' + n] for n in WEIGHTS}
    nchip = 4
    chip = 2 * lax.axis_index("x") + lax.axis_index("y")
    core = lax.axis_index("c")
    hl = DEPTH // 2

    own = lambda out, mine, rank: lax.dynamic_update_index_in_dim(out, mine, rank, 0)
    half_rows = lambda a, which: lax.dynamic_slice_in_dim(a, which * (a.shape[0] // 2), a.shape[0] // 2, axis=0)
    mine = [half_rows(wloc[n].astype(BF16).reshape(-1, wloc[n].shape[-1]), core) for n in BIG_NAMES]
    got = _exchange_multi(mine, 'xy', False, "gather_w_chips")
    got = [own(g_, m_, chip).reshape(nchip * m_.shape[0], m_.shape[1]) for g_, m_ in zip(got, mine)]
    both = _exchange_multi(got, 'c', False, "gather_w_cores")
    full = {}
    for n, b_, g_ in zip(BIG_NAMES, both, got):
        _, r, c = wloc[n].shape
        v = own(b_, g_, core).reshape(2, nchip, hl, r, c)
        if BIG[n] == 1:
            full[n] = v.transpose(0, 2, 3, 1, 4).reshape(DEPTH, r, nchip * c)
        else:
            full[n] = v.transpose(0, 2, 1, 3, 4).reshape(DEPTH, nchip * r, c)
    cs_shapes = [wloc[n].shape for n in COLSHARD_SMALL]
    cs_total = sum(int(np.prod(s)) for s in cs_shapes)
    cs_rows = -(-cs_total // (8 * 128)) * 8
    cs_flat = jnp.concatenate([wloc[n].reshape(-1) for n in COLSHARD_SMALL])
    cs_flat = jnp.pad(cs_flat, (0, cs_rows * 128 - cs_total)).reshape(cs_rows, 128)
    cs_all = _exchange(cs_flat, 'xy', False, "gather_small").reshape(nchip, -1)
    cs_chip = [_unflatten(cs_all[j], cs_shapes) for j in range(nchip)]
    for i, n in enumerate(COLSHARD_SMALL):
        full[n] = jnp.concatenate([cs_chip[j][i] for j in range(nchip)], axis=-1)
    for n in SMALL_NAMES:
        if n not in COLSHARD_SMALL:
            full[n] = wloc[n]

    loss_part, grad_x, grads = _local_step(x, loss_target, full)

    g2d = [grads[n].reshape(-1, grads[n].shape[-1]) for n in BIG_NAMES]
    give = [half_rows(g_, 1 - core) for g_ in g2d]
    keep = [half_rows(g_, core) for g_ in g2d]
    theirs = _exchange_multi(give, 'c', False, "reduce_cores", single=True)
    psums = []
    for n, k_, t_ in zip(BIG_NAMES, keep, theirs):
        s2 = _sum_rows([k_, t_], BF16, "reduce_cores_sum")
        _, r, c = wloc[n].shape
        if BIG[n] == 1:
            psums.append(s2.reshape(s2.shape[0], nchip, c).transpose(1, 0, 2))
        else:
            psums.append(s2.reshape(hl, nchip, r, c).transpose(1, 0, 2, 3).reshape(nchip, hl * r, c))
    parts = _exchange_multi(psums, 'xy', True, "reduce_chips")
    parts = [own(p_, lax.dynamic_index_in_dim(s_, chip, axis=0, keepdims=False), chip) for p_, s_ in zip(parts, psums)]
    rsums = [_sum_slots(p_, F32, "reduce_chips_sum") for p_ in parts]
    halves = _exchange_multi(rsums, 'c', False, "reduce_share")
    gbig = {n: own(h_, r_, core).reshape(wloc[n].shape) for n, h_, r_ in zip(BIG_NAMES, halves, rsums)}

    sm_shapes = [grads[n].shape for n in SMALL_NAMES]
    sm_total = sum(int(np.prod(s)) for s in sm_shapes) + 128
    sm_rows = -(-sm_total // (8 * 128)) * 8
    sm_flat = jnp.concatenate([loss_part.reshape(-1)] + [grads[n].reshape(-1) for n in SMALL_NAMES])
    sm_flat = jnp.pad(sm_flat, (0, sm_rows * 128 - sm_total)).reshape(sm_rows, 128)
    sm_all = _exchange(sm_flat, 'xyc', False, "gather_small_grads")
    sm_sum = _sum_rows([sm_all[j] for j in range(8)], F32, "small_grads_sum").reshape(-1)
    loss = sm_sum[0]
    gsmall_full = dict(zip(SMALL_NAMES, _unflatten(sm_sum[128:], sm_shapes)))
    gsmall = {}
    for n in SMALL_NAMES:
        gfull = gsmall_full[n]
        if n in COLSHARD_SMALL:
            wcols = wloc[n].shape[-1]
            gfull = lax.dynamic_slice_in_dim(gfull, chip * wcols, wcols, axis=gfull.ndim - 1)
        gsmall[n] = gfull

    big_out = [{}, {}, {}]
    for n in BIG_NAMES:
        rows2d = lambda a: a.reshape(-1, a.shape[-1])
        res = _adamw(rows2d(wloc[n]), rows2d(gbig[n]), rows2d(mloc[n]), rows2d(vloc[n]), "adamw_big")
        for k in range(3):
            big_out[k][n] = res[k].reshape(wloc[n].shape)
    loc_shapes = [wloc[n].shape for n in SMALL_NAMES]
    loc_total = sum(int(np.prod(s)) for s in loc_shapes)
    loc_rows = -(-loc_total // (8 * 128)) * 8

    def flat_small(dct):
        v = jnp.concatenate([dct[n].reshape(-1) for n in SMALL_NAMES])
        return jnp.pad(v, (0, loc_rows * 128 - loc_total)).reshape(loc_rows, 128)
    dls, mns, vns = _adamw(flat_small(wloc), flat_small(gsmall), flat_small(mloc), flat_small(vloc), "adamw_small")
    small_out = [dict(zip(SMALL_NAMES, _unflatten(a.reshape(-1), loc_shapes))) for a in (dls, mns, vns)]

    grad_w = {**gbig, **gsmall}
    outs = [loss, grad_x] + [grad_w[n] for n in WEIGHTS]
    for k in range(3):
        merged = {**big_out[k], **small_out[k]}
        outs += [merged[n] for n in WEIGHTS]
    return tuple(outs)
```

```python
import functools
import math

import numpy as np
import jax
import jax.numpy as jnp
from jax import lax
from jax.experimental import pallas as pl
from jax.experimental.pallas import tpu as pltpu

F32 = jnp.float32
BF16 = jnp.bfloat16
HI = lax.Precision.HIGHEST
VMEM_LIMIT_BYTES = 56 * 1024 * 1024
NEG = -1e30

D_MODEL = 1024
SEQ = 4096
DEPTH = 4
N_META = 16
Q_BLOCK = 128
SSD_CHUNK = 128
NORM_EPS = 1e-6
ATTN_HEADS = 16
ATTN_HEAD_DIM = 64
SSD_HEAD_DIM = 64
SSD_GROUPS = 2
SSD_STATE = 128
CONV_K = 4
LRU_BLOCKS = 16
LRU_C = 8.0
D_FF = 2816
ADAM_LR = 0.001
ADAM_B1 = 0.9
ADAM_B2 = 0.999
ADAM_EPS = 1e-08
ADAM_WD = 0.01
ADAM_STEP = 10
SMALL_W = 128
_ROWWISE_TILE_ELEMS = 512 * 1024

WEIGHTS = ['meta_tokens', 'ffn1_norm', 'ffn1_w_gate_up', 'ffn1_w_down', 'mix_norm', 'w_in', 'fox_forget_bias',
           'ssd_conv_w', 'ssd_conv_b', 'ssd_dt_bias', 'ssd_a_log', 'ssd_d', 'ssd_norm', 'lru_conv_w', 'lru_conv_b',
           'lru_w_a', 'lru_b_a', 'lru_w_x', 'lru_b_x', 'lru_lambda', 'w_branch_attn', 'w_branch_ssd', 'w_branch_lru',
           'w_out', 'ffn2_norm', 'ffn2_w_gate_up', 'ffn2_w_down', 'final_norm']
BIG = {'ffn1_w_gate_up': 1, 'ffn1_w_down': 0, 'w_in': 1, 'w_branch_attn': 0, 'w_branch_ssd': 0, 'w_branch_lru': 0,
       'w_out': 0, 'ffn2_w_gate_up': 1, 'ffn2_w_down': 0}
BIG_NAMES = [n for n in WEIGHTS if n in BIG]
COLSHARD_SMALL = ['meta_tokens', 'ssd_conv_w', 'lru_conv_w']
SMALL_NAMES = [n for n in WEIGHTS if n not in BIG]


def _pick(n, cands):
    for c in cands:
        if n % c == 0:
            return c
    raise ValueError(f"no tile for {n} in {cands}")


def _pcall(body, **kw):
    return pl.pallas_call(body, **kw)


def _cparams(sem):
    return pltpu.CompilerParams(dimension_semantics=sem, vmem_limit_bytes=VMEM_LIMIT_BYTES)


def _sds(shape, dtype):
    return jax.ShapeDtypeStruct(tuple(shape), dtype)


def _dot(a, b, hi=False):
    return jnp.dot(a, b, precision=HI if hi else None, preferred_element_type=F32)


def _dot_nt(a, b):
    return lax.dot_general(a, b, (((1,), (1,)), ((), ())), preferred_element_type=F32)


def _dot_sel(a, b, mask):
    if mask == 'a':
        am = a.astype(BF16)
        return sum(_dot(am, p.astype(BF16)) for p in _split3(b))
    bm = b.astype(BF16)
    return sum(_dot(p.astype(BF16), bm) for p in _split3(a))


def _dot_tn(a, b):
    return lax.dot_general(a, b, (((0,), (0,)), ((), ())), preferred_element_type=F32)


def _sigmoid(x):
    return 1.0 / (1.0 + jnp.exp(-x))


def _sigmoid_tanh(x):
    return 0.5 * jnp.tanh(0.5 * x) + 0.5


def _softplus(x):
    return jnp.maximum(x, 0.0) + jnp.log1p(jnp.exp(-jnp.abs(x)))


def _silu(x):
    return x * _sigmoid(x)


def _dsilu(x):
    s = _sigmoid(x)
    return s * (1.0 + x * (1.0 - s))


_GELU_C = math.sqrt(2.0 / math.pi)


def _gelu(x):
    return 0.5 * x * (1.0 + jnp.tanh(_GELU_C * (x + 0.044715 * x * x * x)))


def _dgelu(x):
    t = jnp.tanh(_GELU_C * (x + 0.044715 * x * x * x))
    return 0.5 * (1.0 + t) + 0.5 * x * (1.0 - t * t) * _GELU_C * (1.0 + 3.0 * 0.044715 * x * x)


def _expm1(x):
    series = x * (1.0 + x * 0.5 * (1.0 + x * (1.0 / 3.0) * (1.0 + x * 0.25 * (1.0 + x * 0.2))))
    return jnp.where(jnp.abs(x) < 0.05, series, jnp.exp(x) - 1.0)


def _rowwise(fn, ins, outs, *, bcast=(), reds=(), tm=None, name, period=None):
    t_rows = ins[0].shape[0]
    if tm is None:
        widest = max([a.shape[1] for a in ins] + [c for c, _ in outs])
        tm = _pick(math.gcd(t_rows, period or t_rows),
                   [c for c in (384, 256, 128, 64, 32, 16, 8) if c * widest <= _ROWWISE_TILE_ELEMS or c == 8])
    nt = t_rows // tm
    assert t_rows % tm == 0 and (period is None or period % tm == 0)
    n_in, n_out = len(ins) + len(bcast), len(outs)

    def body(*refs):
        i = pl.program_id(0)
        pos = None
        if period is not None:
            pos = (i * tm) % period + lax.broadcasted_iota(jnp.int32, (tm, 1), 0)
        res = fn(pos, *[r[...] for r in refs[:n_in]])
        res = res if isinstance(res, tuple) else (res,)
        for r, v in zip(refs[n_in:n_in + n_out], res[:n_out]):
            r[...] = v.astype(r.dtype)
        red_refs = refs[n_in + n_out:]
        if red_refs:
            @pl.when(i == 0)
            def _():
                for r in red_refs:
                    r[...] = jnp.zeros_like(r)
            for r, v in zip(red_refs, res[n_out:]):
                r[...] += v

    in_specs = [pl.BlockSpec((tm, a.shape[1]), lambda i: (i, 0)) for a in ins]
    in_specs += [pl.BlockSpec(b.shape, lambda i, n=b.ndim: (0,) * n) for b in bcast]
    out_specs = [pl.BlockSpec((tm, c), lambda i: (i, 0)) for c, _ in outs]
    out_specs += [pl.BlockSpec(s, lambda i: (0, 0)) for s in reds]
    out_shape = [_sds((t_rows, c), dt) for c, dt in outs] + [_sds(s, F32) for s in reds]
    res = _pcall(body, name=name, grid=(nt,), in_specs=in_specs, out_specs=out_specs, out_shape=out_shape,
                 compiler_params=_cparams(("arbitrary",) if reds else ("parallel",)))(*ins, *bcast)
    return res


_TM = (768, 384, 256, 128)
_TN = (1536, 1408, 1024, 768, 512, 640, 384, 256, 128)
_TK = (1536, 1024, 2816, 1408, 512, 384, 256, 128)
_TKO = (1024, 1408, 512, 384, 256, 128)


def _mm_nn(a, b, out_dtype, *, res=None, alpha=1.0, name):
    m, k = a.shape
    k2, n = b.shape
    assert k == k2
    tm, tn, tk = _pick(m, _TM), _pick(n, _TN), _pick(k, _TK)
    nk = k // tk

    def body(*refs):
        if res is None:
            a_ref, b_ref, o_ref, acc = refs
            r_ref = None
        else:
            a_ref, b_ref, r_ref, o_ref, acc = refs
        kk = pl.program_id(2)

        @pl.when(kk == 0)
        def _():
            acc[...] = jnp.zeros_like(acc)

        acc[...] += _dot(a_ref[...].astype(BF16), b_ref[...].astype(BF16))

        @pl.when(kk == nk - 1)
        def _():
            v = acc[...]
            if alpha != 1.0:
                v = v * alpha
            if r_ref is not None:
                v = r_ref[...].astype(F32) + v
            o_ref[...] = v.astype(o_ref.dtype)

    in_specs = [pl.BlockSpec((tm, tk), lambda j, i, kk: (i, kk)), pl.BlockSpec((tk, tn), lambda j, i, kk: (kk, j))]
    args = [a, b]
    if res is not None:
        in_specs.append(pl.BlockSpec((tm, tn), lambda j, i, kk: (i, j)))
        args.append(res)
    return _pcall(body, name=name, grid=(n // tn, m // tm, nk), in_specs=in_specs,
                  out_specs=pl.BlockSpec((tm, tn), lambda j, i, kk: (i, j)), out_shape=_sds((m, n), out_dtype),
                  scratch_shapes=[pltpu.VMEM((tm, tn), F32)],
                  compiler_params=_cparams(("parallel", "parallel", "arbitrary")))(*args)


def _mm_swiglu(a, wg, wu, name):
    m, k = a.shape
    f = wg.shape[1]
    tm, tn, tk = _pick(m, _TM), _pick(f, _TN), _pick(k, _TK)
    nk = k // tk

    def body(a_ref, g_w, u_w, g_ref, u_ref, act_ref, accg, accu):
        kk = pl.program_id(2)

        @pl.when(kk == 0)
        def _():
            accg[...] = jnp.zeros_like(accg)
            accu[...] = jnp.zeros_like(accu)

        av = a_ref[...].astype(BF16)
        accg[...] += _dot(av, g_w[...])
        accu[...] += _dot(av, u_w[...])

        @pl.when(kk == nk - 1)
        def _():
            g, u = accg[...], accu[...]
            g_ref[...] = g.astype(g_ref.dtype)
            u_ref[...] = u.astype(u_ref.dtype)
            act_ref[...] = (g * _sigmoid_tanh(g) * u).astype(act_ref.dtype)

    wspec = pl.BlockSpec((tk, tn), lambda j, i, kk: (kk, j))
    ospec = pl.BlockSpec((tm, tn), lambda j, i, kk: (i, j))
    return _pcall(body, name=name, grid=(f // tn, m // tm, nk),
                  in_specs=[pl.BlockSpec((tm, tk), lambda j, i, kk: (i, kk)), wspec, wspec],
                  out_specs=[ospec] * 3, out_shape=[_sds((m, f), BF16)] * 3,
                  scratch_shapes=[pltpu.VMEM((tm, tn), F32)] * 2,
                  compiler_params=_cparams(("parallel", "parallel", "arbitrary")))(a, wg, wu)


def _mm_dswiglu(dout, wd_t, g, u, alpha, name):
    m, k = dout.shape
    f = wd_t.shape[1]
    tm, tn, tk = _pick(m, _TM), _pick(f, _TN), _pick(k, _TK)
    nk = k // tk

    def body(a_ref, w_ref, g_ref, u_ref, dg_ref, du_ref, acc):
        kk = pl.program_id(2)

        @pl.when(kk == 0)
        def _():
            acc[...] = jnp.zeros_like(acc)

        acc[...] += _dot(a_ref[...].astype(BF16), w_ref[...])

        @pl.when(kk == nk - 1)
        def _():
            dact = acc[...] * alpha
            gv, uv = g_ref[...].astype(F32), u_ref[...].astype(F32)
            sg = _sigmoid_tanh(gv)
            dg_ref[...] = (dact * uv * (sg * (1.0 + gv * (1.0 - sg)))).astype(dg_ref.dtype)
            du_ref[...] = (dact * (gv * sg)).astype(du_ref.dtype)

    ospec = pl.BlockSpec((tm, tn), lambda j, i, kk: (i, j))
    return _pcall(body, name=name, grid=(f // tn, m // tm, nk),
                  in_specs=[pl.BlockSpec((tm, tk), lambda j, i, kk: (i, kk)),
                            pl.BlockSpec((tk, tn), lambda j, i, kk: (kk, j)), ospec, ospec],
                  out_specs=[ospec] * 2, out_shape=[_sds((m, f), BF16)] * 2,
                  scratch_shapes=[pltpu.VMEM((tm, tn), F32)],
                  compiler_params=_cparams(("parallel", "parallel", "arbitrary")))(dout, wd_t, g, u)


def _mm_tn(a, b, *, alpha=1.0, name):
    m, k = a.shape
    m2, n = b.shape
    assert m == m2
    tm, tn, tko = _pick(m, _TM), _pick(n, _TN), _pick(k, _TKO)
    nm = m // tm

    def body(a_ref, b_ref, o_ref, acc):
        mm = pl.program_id(2)

        @pl.when(mm == 0)
        def _():
            acc[...] = jnp.zeros_like(acc)

        acc[...] += _dot_tn(a_ref[...].astype(BF16), b_ref[...].astype(BF16))

        @pl.when(mm == nm - 1)
        def _():
            v = acc[...]
            o_ref[...] = v * alpha if alpha != 1.0 else v

    return _pcall(body, name=name, grid=(k // tko, n // tn, nm),
                  in_specs=[pl.BlockSpec((tm, tko), lambda i, j, mm: (mm, i)),
                            pl.BlockSpec((tm, tn), lambda i, j, mm: (mm, j))],
                  out_specs=pl.BlockSpec((tko, tn), lambda i, j, mm: (i, j)), out_shape=_sds((k, n), F32),
                  scratch_shapes=[pltpu.VMEM((tko, tn), F32)],
                  compiler_params=_cparams(("parallel", "parallel", "arbitrary")))(a, b)


def _rms_fwd(h, g, name):
    def fn(_, hv, gv):
        r = lax.rsqrt(jnp.mean(hv * hv, axis=1, keepdims=True) + NORM_EPS)
        return hv * r * gv
    return _rowwise(fn, [h], [(h.shape[1], BF16)], bcast=[g], name=name)[0]


def _rms_bwd(h, dxn, dres, g, name):
    d = h.shape[1]

    def fn(_, hv, dv, rv, gv):
        r = lax.rsqrt(jnp.mean(hv * hv, axis=1, keepdims=True) + NORM_EPS)
        xh = hv * r
        dxh = dv * gv
        dh = r * (dxh - xh * jnp.mean(dxh * xh, axis=1, keepdims=True))
        return rv + dh, jnp.sum(dv * xh, axis=0, keepdims=True)
    return _rowwise(fn, [h, dxn, dres], [(d, F32)], bcast=[g], reds=[(1, d)], name=name)


def _merge_fwd(mg, ba, bb, bc, name):
    d = ba.shape[1]

    def fn(_, m, a, b, c):
        g = _sigmoid(m.astype(F32))
        return g[:, :d] * a.astype(F32) + g[:, d:2 * d] * b.astype(F32) + g[:, 2 * d:] * c.astype(F32)
    return _rowwise(fn, [mg, ba, bb, bc], [(d, BF16)], name=name)[0]


def _merge_bwd(mg, ba, bb, bc, dmix, name):
    d = ba.shape[1]

    def fn(_, m, a, b, c, dm):
        g = _sigmoid(m.astype(F32))
        dm = dm.astype(F32)
        br = (a.astype(F32), b.astype(F32), c.astype(F32))
        douts, dgs = [], []
        for j in range(3):
            gj = g[:, j * d:(j + 1) * d]
            douts.append(dm * gj)
            dgs.append(dm * br[j] * gj * (1.0 - gj))
        return (*douts, jnp.concatenate(dgs, axis=1))
    return _rowwise(fn, [mg, ba, bb, bc, dmix], [(d, BF16)] * 3 + [(3 * d, BF16)], name=name)


def _loss_head(h, tgt, g, seq_len, name):
    d = h.shape[1]

    def fn(pos, hv, tv, gv):
        r = lax.rsqrt(jnp.mean(hv * hv, axis=1, keepdims=True) + NORM_EPS)
        xh = hv * r
        real = (pos >= N_META) & (pos < N_META + SEQ)
        e = jnp.where(real, xh * gv - tv, 0.0)
        part = jnp.sum(jnp.sum(e * e, axis=1, keepdims=True), axis=0, keepdims=True) * (0.5 / d)
        dy = e * (1.0 / d)
        dxh = dy * gv
        dh = r * (dxh - xh * jnp.mean(dxh * xh, axis=1, keepdims=True))
        return dh, jnp.broadcast_to(part, (1, 128)), jnp.sum(dy * xh, axis=0, keepdims=True)
    return _rowwise(fn, [h, tgt], [(d, F32)], bcast=[g], reds=[(1, 128), (1, d)], name=name, period=seq_len)


def _adamw(w, g, m, v, name):
    c1 = 1.0 - ADAM_B1 ** ADAM_STEP
    c2 = 1.0 - ADAM_B2 ** ADAM_STEP
    wd = w.shape[1]

    def fn(_, wv, gv, mv, vv):
        mn = ADAM_B1 * mv + (1.0 - ADAM_B1) * gv
        vn = ADAM_B2 * vv + (1.0 - ADAM_B2) * (gv * gv)
        delta = -ADAM_LR * ((mn / c1) / (jnp.sqrt(vn / c2) + ADAM_EPS) + ADAM_WD * wv)
        return delta, mn, vn
    return _rowwise(fn, [w, g, m, v], [(wd, F32)] * 3, name=name)


def _sum_rows(parts, out_dtype, name):
    def fn(_, *vs):
        acc = vs[0].astype(F32)
        for v in vs[1:]:
            acc = acc + v.astype(F32)
        return acc
    return _rowwise(fn, list(parts), [(parts[0].shape[1], out_dtype)], name=name)[0]


def _cumsum_seq(x, reverse, name):
    b, l, w = x.shape
    q = 128
    nc = l // q

    def body(x_ref, o_ref):
        row = lax.broadcasted_iota(jnp.int32, (q, q), 0)
        col = lax.broadcasted_iota(jnp.int32, (q, q), 1)
        tri = ((row <= col) if reverse else (row >= col)).astype(F32)
        rsel = lax.broadcasted_iota(jnp.int32, (q, w), 0) == (0 if reverse else q - 1)

        def step(i, carry):
            j = (nc - 1 - i) if reverse else i
            start = pl.multiple_of(j * q, q)
            cs = _dot_sel(tri, x_ref[pl.ds(start, q), :], 'a') + carry
            o_ref[pl.ds(start, q), :] = cs
            return jnp.sum(jnp.where(rsel, cs, 0.0), axis=0, keepdims=True)

        lax.fori_loop(0, nc, step, jnp.zeros((1, w), F32))

    return _pcall(body, name=name, grid=(b,), in_specs=[pl.BlockSpec((None, l, w), lambda i: (i, 0, 0))],
                  out_specs=pl.BlockSpec((None, l, w), lambda i: (i, 0, 0)), out_shape=_sds(x.shape, F32),
                  compiler_params=_cparams(("parallel",)))(x)


_HALO = 16


def _conv_tiles(l, c):
    return _pick(l, (384, 256, 128)), _pick(c, (512, 256, 128))


def _conv_fwd(x, w, bias, out_dtype, name, with_silu=False):
    b, l, c = x.shape
    tt, cw = _conv_tiles(l, c)

    def body(x_ref, h_ref, w_ref, b_ref, *o_refs):
        t = pl.program_id(2)
        halo = jnp.where(t == 0, 0.0, h_ref[...].astype(F32))
        xe = jnp.concatenate([halo, x_ref[...].astype(F32)], axis=0)
        wv = w_ref[...]
        acc = b_ref[...] + wv[CONV_K - 1:CONV_K, :] * xe[_HALO:]
        for j in range(CONV_K - 1):
            acc = acc + wv[j:j + 1, :] * pltpu.roll(xe, CONV_K - 1 - j, 0)[_HALO:]
        o_refs[0][...] = acc.astype(o_refs[0].dtype)
        if with_silu:
            o_refs[1][...] = _silu(acc.astype(o_refs[0].dtype).astype(F32)).astype(o_refs[1].dtype)

    ospec = pl.BlockSpec((None, tt, cw), lambda i, j, t: (i, t, j))
    n_out = 2 if with_silu else 1
    res = _pcall(body, name=name, grid=(b, c // cw, l // tt),
                 in_specs=[pl.BlockSpec((None, tt, cw), lambda i, j, t: (i, t, j)),
                           pl.BlockSpec((None, _HALO, cw), lambda i, j, t: (i, jnp.maximum(t * (tt // _HALO) - 1, 0), j)),
                           pl.BlockSpec((CONV_K, cw), lambda i, j, t: (0, j)),
                           pl.BlockSpec((1, cw), lambda i, j, t: (0, j))],
                 out_specs=[ospec] * n_out, out_shape=[_sds(x.shape, out_dtype)] * n_out,
                 compiler_params=_cparams(("parallel", "parallel", "parallel")))(x, x, w, bias)
    return res if with_silu else res[0]


def _conv_bwd(x, dy, w, name):
    b, l, c = x.shape
    tt, cw = _conv_tiles(l, c)
    nt = l // tt

    def body(x_ref, xh_ref, d_ref, dh_ref, w_ref, dx_ref, dw_ref):
        i, t = pl.program_id(1), pl.program_id(2)
        halo = jnp.where(t == 0, 0.0, xh_ref[...].astype(F32))
        xe = jnp.concatenate([halo, x_ref[...].astype(F32)], axis=0)
        dv = d_ref[...].astype(F32)
        nxt = jnp.where(t == nt - 1, 0.0, dh_ref[...].astype(F32))
        de = jnp.concatenate([dv, nxt], axis=0)
        wv = w_ref[...]
        dx = wv[CONV_K - 1:CONV_K, :] * dv
        rowid = lax.broadcasted_iota(jnp.int32, (8, 1), 0)
        part = jnp.where(rowid == CONV_K, jnp.sum(dv, axis=0, keepdims=True), 0.0)
        part = part + jnp.where(rowid == CONV_K - 1, jnp.sum(dv * xe[_HALO:], axis=0, keepdims=True), 0.0)
        for j in range(CONV_K - 1):
            s = CONV_K - 1 - j
            dx = dx + wv[j:j + 1, :] * pltpu.roll(de, tt + _HALO - s, 0)[:tt]
            xs = pltpu.roll(xe, s, 0)[_HALO:]
            part = part + jnp.where(rowid == j, jnp.sum(dv * xs, axis=0, keepdims=True), 0.0)
        dx_ref[...] = dx.astype(dx_ref.dtype)

        @pl.when((i == 0) & (t == 0))
        def _():
            dw_ref[...] = jnp.zeros_like(dw_ref)
        dw_ref[...] += part

    return _pcall(body, name=name, grid=(c // cw, b, nt),
                  in_specs=[pl.BlockSpec((None, tt, cw), lambda j, i, t: (i, t, j)),
                            pl.BlockSpec((None, _HALO, cw), lambda j, i, t: (i, jnp.maximum(t * (tt // _HALO) - 1, 0), j)),
                            pl.BlockSpec((None, tt, cw), lambda j, i, t: (i, t, j)),
                            pl.BlockSpec((None, _HALO, cw),
                                         lambda j, i, t: (i, jnp.minimum((t + 1) * (tt // _HALO), l // _HALO - 1), j)),
                            pl.BlockSpec((CONV_K, cw), lambda j, i, t: (0, j))],
                  out_specs=[pl.BlockSpec((None, tt, cw), lambda j, i, t: (i, t, j)),
                             pl.BlockSpec((8, cw), lambda j, i, t: (0, j))],
                  out_shape=[_sds(x.shape, BF16), _sds((8, c), F32)],
                  compiler_params=_cparams(("parallel", "arbitrary", "arbitrary")))(x, x, dy, dy, w)


_SUBLANES = 8


def _linear_scan(a, u, reverse, name, gate=None):
    b, l, c = a.shape
    tt = 128
    cw = _pick(c, (512, 256, 128))
    nt = l // tt
    groups = tt // _SUBLANES

    def body(*refs):
        if gate is None:
            a_ref, u_ref, h_ref, carry = refs
        else:
            a_ref, u_ref, g_ref, h_ref, y_ref, carry = refs
        t = pl.program_id(2)

        @pl.when(t == 0)
        def _():
            carry[...] = jnp.zeros_like(carry)

        av, uv = a_ref[...], u_ref[...]
        sub = jnp.bitwise_and(lax.broadcasted_iota(jnp.int32, (tt, cw), 0), _SUBLANES - 1)
        k = 1
        while k < _SUBLANES:
            keep = (sub < _SUBLANES - k) if reverse else (sub >= k)
            shift = tt - k if reverse else k
            a_sh = jnp.where(keep, pltpu.roll(av, shift, 0), 1.0)
            u_sh = jnp.where(keep, pltpu.roll(uv, shift, 0), 0.0)
            uv = uv + av * u_sh
            av = av * a_sh
            k *= 2
        edge = carry[0:1, :]
        for g in (range(groups - 1, -1, -1) if reverse else range(groups)):
            rows = slice(g * _SUBLANES, (g + 1) * _SUBLANES)
            hg = uv[rows] + av[rows] * edge
            h_ref[rows, :] = hg
            edge = hg[0:1] if reverse else hg[_SUBLANES - 1:_SUBLANES]
        carry[...] = jnp.broadcast_to(edge, carry.shape)
        if gate is not None:
            y_ref[...] = (h_ref[...] * _gelu(g_ref[...].astype(F32))).astype(y_ref.dtype)

    tmap = (lambda i, j, t: (i, nt - 1 - t, j)) if reverse else (lambda i, j, t: (i, t, j))
    spec = pl.BlockSpec((None, tt, cw), tmap)
    ins, outs, shapes = [a, u], [spec], [_sds(a.shape, F32)]
    if gate is not None:
        ins, outs, shapes = [a, u, gate], [spec, spec], [_sds(a.shape, F32), _sds(a.shape, BF16)]
    res = _pcall(body, name=name, grid=(b, c // cw, nt), in_specs=[spec] * len(ins), out_specs=outs,
                 out_shape=shapes, scratch_shapes=[pltpu.VMEM((8, cw), F32)],
                 compiler_params=_cparams(("parallel", "parallel", "arbitrary")))(*ins)
    return res if gate is not None else res[0]


ATTN_W = 128
_AUG_C = 0
_AUG_ONE = 3
_AUG_LSE = 6


def _attn_blk(l):
    return _pick(l, (384, 256, 128))


_FWD_HEADS_PER_STEP = 4


def _prep_rows(l):
    return _pick(l, (1408, 384, 256, 128))


def _split3(x):
    x1 = x.astype(BF16).astype(F32)
    x2 = (x - x1).astype(BF16).astype(F32)
    x3 = (x - x1 - x2).astype(BF16).astype(F32)
    return x1, x2, x3


def _aug_lanes(lane, base, cols, ones_at=()):
    out = jnp.zeros(lane.shape, F32)
    for o in ones_at:
        out = out + ((lane >= o) & (lane < o + 3)).astype(F32)
    for k, v in enumerate(cols):
        out = jnp.where(lane == base + k, v, out)
    return out


def _pair_specs(blk, nb):
    at = lambda ww: pl.BlockSpec((None, 2, blk, ww), lambda bi, p, i: (bi, p, i, 0))
    whole = pl.BlockSpec((None, 2, nb, blk, ATTN_W), lambda bi, p, i: (bi, p, 0, 0, 0))
    rows = pl.BlockSpec((None, blk, ATTN_W), lambda bi, p, i: (bi, i, p))
    return at, whole, rows


def _attn_prep(pm3, cum, col0, name):
    b, l, _ = pm3.shape
    dh, nh = ATTN_HEAD_DIM, ATTN_HEADS
    blk = _prep_rows(l)
    scale = dh ** -0.5

    def body(q_ref, k_ref, v_ref, c_ref, qa_ref, ka_ref, va_ref):
        pair = pl.program_id(1)
        lane = lax.broadcasted_iota(jnp.int32, (1, ATTN_W), 1)
        head = lane < dh
        cv = c_ref[...]
        qf, kf, vf = (r[...].astype(F32) for r in (q_ref, k_ref, v_ref))
        for e in range(2):
            c1, c2, c3 = _split3(jnp.sum(jnp.where(lane == 2 * pair + e, cv, 0.0), axis=1, keepdims=True))
            qe, ke, ve = (pltpu.roll(t, dh, 1) for t in (qf, kf, vf)) if e else (qf, kf, vf)
            qa_ref[e] = jnp.where(head, qe * scale,
                                  _aug_lanes(lane, dh + _AUG_C, (c1, c2, c3), (dh + _AUG_ONE,))).astype(BF16)
            ka_ref[e] = jnp.where(head, ke, _aug_lanes(lane, dh + _AUG_ONE, (-c1, -c2, -c3),
                                                       (dh + _AUG_C, dh + _AUG_LSE))).astype(BF16)
            va_ref[e] = jnp.where(head, ve, _aug_lanes(lane, dh, (), (dh,))).astype(BF16)

    at, _, _ = _pair_specs(blk, l // blk)
    cols = lambda c0: pl.BlockSpec((None, blk, ATTN_W), lambda bi, p, i, c0=c0: (bi, i, c0 // ATTN_W + p))
    return _pcall(body, name=name, grid=(b, nh // 2, l // blk),
                  in_specs=[cols(col0[0]), cols(col0[1]), cols(col0[2]),
                            pl.BlockSpec((None, blk, SMALL_W), lambda bi, p, i: (bi, i, 0))],
                  out_specs=[at(ATTN_W)] * 3, out_shape=[_sds((b, nh, l, ATTN_W), BF16)] * 3,
                  compiler_params=_cparams(("parallel", "parallel", "parallel")))(pm3, pm3, pm3, cum)


def _attn_prep_bwd(dy3, y3, qa, lse, name):
    b, nh, l, _ = qa.shape
    dh = ATTN_HEAD_DIM
    blk = _prep_rows(l)

    def body(dy_ref, y_ref, qa_ref, lse_ref, qa2_ref, doa_ref):
        lane = lax.broadcasted_iota(jnp.int32, (1, ATTN_W), 1)
        dyf = dy_ref[...].astype(F32)
        prod = dyf * y_ref[...].astype(F32)
        for e in range(2):
            mine = (lane >= dh) if e else (lane < dh)
            d1, d2, d3 = _split3(jnp.sum(jnp.where(mine, prod, 0.0), axis=1, keepdims=True))
            l1, l2, l3 = _split3(lse_ref[e])
            dye = pltpu.roll(dyf, dh, 1) if e else dyf
            doa_ref[e] = jnp.where(lane < dh, dye, _aug_lanes(lane, dh, (-d1, -d2, -d3))).astype(BF16)
            qa2 = qa_ref[e].astype(F32)
            for k, lv in enumerate((l1, l2, l3)):
                qa2 = jnp.where(lane == dh + _AUG_LSE + k, -lv, qa2)
            qa2_ref[e] = qa2.astype(BF16)

    at, _, rows = _pair_specs(blk, l // blk)
    return _pcall(body, name=name, grid=(b, nh // 2, l // blk), in_specs=[rows, rows, at(ATTN_W), at(1)],
                  out_specs=[at(ATTN_W)] * 2, out_shape=[_sds(qa.shape, BF16)] * 2,
                  compiler_params=_cparams(("parallel", "parallel", "parallel")))(dy3, y3, qa, lse)


def _flash_fwd(qa, ka, va, d_model, name):
    b, h, l, w = qa.shape
    dh = ATTN_HEAD_DIM
    blk = _attn_blk(l)
    nb = l // blk
    hps = _FWD_HEADS_PER_STEP if h % _FWD_HEADS_PER_STEP == 0 else 2
    kr, vr = ka.reshape(b, h, nb, blk, w), va.reshape(b, h, nb, blk, w)

    def body(q_ref, k_ref, v_ref, o_ref, lse_ref):
        i = pl.program_id(2)
        row = lax.broadcasted_iota(jnp.int32, (blk, blk), 0)
        col = lax.broadcasted_iota(jnp.int32, (blk, blk), 1)

        def scores(e, j):
            return _dot_nt(q_ref[e], k_ref[e, j])

        def consume(e, j, s, m, acc):
            mn = jnp.maximum(m, jnp.max(s, axis=1, keepdims=True))
            return mn, jnp.exp(m - mn) * acc + _dot(jnp.exp(s - mn).astype(BF16), v_ref[e, j])

        def step(j, carry):
            out = []
            for e in range(hps):
                m, acc, s = carry[3 * e:3 * e + 3]
                s_next = scores(e, j + 1)
                out += [*consume(e, j, s, m, acc), s_next]
            return tuple(out)

        init = tuple(t for e in range(hps)
                     for t in (jnp.full((blk, 1), NEG, F32), jnp.zeros((blk, w), F32), scores(e, 0)))
        carry = lax.fori_loop(0, i, step, init)
        lane = lax.broadcasted_iota(jnp.int32, (1, w), 1)
        outs = []
        for e in range(hps):
            m, acc = consume(e, i, jnp.where(col <= row, carry[3 * e + 2], NEG), carry[3 * e], carry[3 * e + 1])
            lsum = acc[:, dh:dh + 1]
            lse_ref[e] = m + jnp.log(lsum)
            outs.append(acc / lsum)
        for pr in range(hps // 2):
            o_ref[:, pr * w:(pr + 1) * w] = jnp.where(lane < dh, outs[2 * pr],
                                                      pltpu.roll(outs[2 * pr + 1], dh, 1)).astype(o_ref.dtype)

    at = lambda ww: pl.BlockSpec((None, hps, blk, ww), lambda bi, p, i: (bi, p, i, 0))
    whole = pl.BlockSpec((None, hps, nb, blk, w), lambda bi, p, i: (bi, p, 0, 0, 0))
    rows = pl.BlockSpec((None, blk, hps * dh), lambda bi, p, i: (bi, i, p))
    return _pcall(body, name=name, grid=(b, h // hps, nb), in_specs=[at(w), whole, whole],
                  out_specs=[rows, at(1)], out_shape=[_sds((b, l, d_model), BF16), _sds((b, h, l, 1), F32)],
                  compiler_params=_cparams(("parallel", "parallel", "parallel")))(qa, kr, vr)


def _flash_bwd(qa, ka, va, doa, d_model, name):
    b, h, l, w = qa.shape
    dh = ATTN_HEAD_DIM
    blk = _attn_blk(l)
    nb = l // blk
    scale = dh ** -0.5
    r5 = lambda t: t.reshape(b, h, nb, blk, w)

    def body(k_ref, v_ref, q_ref, do_ref, dq_ref, dk_ref, dv_ref, dcq_ref, dck_ref, dq_acc):
        j = pl.program_id(2)
        row = lax.broadcasted_iota(jnp.int32, (blk, blk), 0)
        col = lax.broadcasted_iota(jnp.int32, (blk, blk), 1)
        lane = lax.broadcasted_iota(jnp.int32, (1, w), 1)

        @pl.when(j == 0)
        def _():
            dq_acc[...] = jnp.zeros_like(dq_acc)

        def contrib(i, masked, carry):
            out = []
            for e in range(2):
                qv, dov = q_ref[e, i], do_ref[e, i]
                p = jnp.exp(_dot_nt(qv, k_ref[e]))
                if masked:
                    p = jnp.where(col <= row, p, 0.0)
                ds = (p * _dot_nt(dov, v_ref[e])).astype(BF16)
                dq_acc[e, i] += _dot(ds, k_ref[e])
                out += [carry[2 * e] + _dot_tn(ds, qv), carry[2 * e + 1] + _dot_tn(p.astype(BF16), dov)]
            return tuple(out)

        zero = (jnp.zeros((blk, w), F32),) * 4
        dk0, dv0, dk1, dv1 = lax.fori_loop(j + 1, nb, lambda i, c: contrib(i, False, c), contrib(j, True, zero))
        dk_ref[...] = jnp.where(lane < dh, dk0, pltpu.roll(dk1, dh, 1)).astype(dk_ref.dtype)
        dv_ref[...] = jnp.where(lane < dh, dv0, pltpu.roll(dv1, dh, 1)).astype(dv_ref.dtype)
        for e, dk in enumerate((dk0, dk1)):
            dck_ref[e] = jnp.sum(jnp.where(lane == dh + _AUG_ONE, dk, 0.0), axis=1, keepdims=True)

        @pl.when(j == nb - 1)
        def _():
            for ib in range(nb):
                rs = pl.ds(ib * blk, blk)
                dq0, dq1 = dq_acc[0, ib], dq_acc[1, ib]
                dq_ref[rs, :] = (jnp.where(lane < dh, dq0, pltpu.roll(dq1, dh, 1)) * scale).astype(dq_ref.dtype)
                for e, dq in enumerate((dq0, dq1)):
                    dcq_ref[e, rs, :] = jnp.sum(jnp.where(lane == dh + _AUG_C, dq, 0.0), axis=1, keepdims=True)

    at, whole, rows = _pair_specs(blk, nb)
    seq_rows = pl.BlockSpec((None, l, ATTN_W), lambda bi, p, j: (bi, 0, p))
    seq_col = pl.BlockSpec((None, 2, l, 1), lambda bi, p, j: (bi, p, 0, 0))
    act = _sds((b, l, d_model), BF16)
    col1 = _sds((b, h, l, 1), F32)
    return _pcall(body, name=name, grid=(b, h // 2, nb), in_specs=[at(w), at(w), whole, whole],
                  out_specs=[seq_rows, rows, rows, seq_col, at(1)], out_shape=[act, act, act, col1, col1],
                  scratch_shapes=[pltpu.VMEM((2, nb, blk, w), F32)],
                  compiler_params=_cparams(("parallel", "parallel", "arbitrary")))(ka, va, r5(qa), r5(doa))


def _ssd_dims(d_ssd):
    heads = d_ssd // SSD_HEAD_DIM
    return heads, heads // SSD_GROUPS, d_ssd // SSD_GROUPS


def _ssd_specs(l, ds, seq_map):
    q = SSD_CHUNK
    gn = SSD_GROUPS * SSD_STATE
    row3 = lambda w, cb: pl.BlockSpec((None, q, w), lambda i, c, cb=cb: (i, seq_map(c), cb))
    return dict(
        xs=row3(ds, 0), bm=row3(gn, ds // gn), cm=row3(gn, ds // gn + 1), z=row3(ds, 0), dt=row3(SMALL_W, 0),
        da=row3(SMALL_W, 0), dat=pl.BlockSpec((None, SMALL_W, q), lambda i, c: (i, 0, seq_map(c))),
        e=pl.BlockSpec((SMALL_W, ds), lambda i, c: (0, 0)), et=pl.BlockSpec((ds, SMALL_W), lambda i, c: (0, 0)),
        vec=pl.BlockSpec((1, ds), lambda i, c: (0, 0)), vec128=pl.BlockSpec((1, SMALL_W), lambda i, c: (0, 0)),
        hin=pl.BlockSpec((None, None, SSD_STATE, ds), lambda i, c: (i, seq_map(c), 0, 0)))


def _ssd_common(da, dat, dt, e_mat, xs):
    q = SSD_CHUNK
    row = lax.broadcasted_iota(jnp.int32, (q, q), 0)
    col = lax.broadcasted_iota(jnp.int32, (q, q), 1)
    lower = row >= col
    cs = _dot_sel(lower.astype(F32), da, 'a')
    cst = _dot_sel(dat, (row <= col).astype(F32), 'b')
    dtx = _dot_sel(dt, e_mat, 'b')
    csx = _dot_sel(cs, e_mat, 'b')
    rowx = lax.broadcasted_iota(jnp.int32, csx.shape, 0)
    totx = jnp.sum(jnp.where(rowx == q - 1, csx, 0.0), axis=0, keepdims=True)
    xf = xs.astype(F32)
    return lower, cs, cst, dtx, csx, totx, xf, xf * dtx


def _ssd_fwd(xbc, z, dt, da, dat, e_mat, dx, nw, name):
    b, l, _ = xbc.shape
    ds = z.shape[2]
    heads, hpg, gw = _ssd_dims(ds)
    q, n = SSD_CHUNK, SSD_STATE
    nc = l // q
    hcol0 = ATTN_HEADS

    def body(xs_ref, bm_ref, cm_ref, z_ref, dt_ref, da_ref, dat_ref, e_ref, dx_ref, nw_ref, y_ref, yraw_ref, hin_ref,
             hst, ydiag):
        c = pl.program_id(1)

        @pl.when(c == 0)
        def _():
            hst[...] = jnp.zeros_like(hst)

        hin = hst[...]
        hin_ref[...] = hin
        lower, cs, cst, dtx, csx, totx, xf, xdt = _ssd_common(da_ref[...], dat_ref[...], dt_ref[...], e_ref[...],
                                                               xs_ref[...])
        bm, cm = bm_ref[...], cm_ref[...]
        dec_end = jnp.exp(totx - csx)
        for g in range(SSD_GROUPS):
            gs = slice(g * gw, (g + 1) * gw)
            bg, cg = bm[:, g * n:(g + 1) * n], cm[:, g * n:(g + 1) * n]
            cb = _dot_nt(cg, bg)
            for e in range(hpg):
                hh = g * hpg + e
                cc = hcol0 + hh
                lm = jnp.exp(jnp.where(lower, cs[:, cc:cc + 1] - cst[cc:cc + 1, :], NEG))
                hs = slice(hh * SSD_HEAD_DIM, (hh + 1) * SSD_HEAD_DIM)
                ydiag[:, hs] = _dot((cb * lm).astype(BF16), xdt[:, hs].astype(BF16))
            sg = _dot_tn(bg, (xdt[:, gs] * dec_end[:, gs]).astype(BF16))
            hst[:, gs] = jnp.exp(totx[:, gs]) * hin[:, gs] + sg
            ydiag[:, gs] += _dot(cg, hin[:, gs].astype(BF16)) * jnp.exp(csx[:, gs])
        yraw = ydiag[...] + dx_ref[...] * xf
        yraw_ref[...] = yraw.astype(yraw_ref.dtype)
        yg = yraw * _silu(z_ref[...].astype(F32))
        nwv = nw_ref[...]
        for g in range(SSD_GROUPS):
            gs = slice(g * gw, (g + 1) * gw)
            r = lax.rsqrt(jnp.mean(yg[:, gs] * yg[:, gs], axis=1, keepdims=True) + NORM_EPS)
            y_ref[:, gs] = (yg[:, gs] * r * nwv[:, gs]).astype(y_ref.dtype)

    sp = _ssd_specs(l, ds, lambda c: c)
    return _pcall(body, name=name, grid=(b, nc),
                  in_specs=[sp['xs'], sp['bm'], sp['cm'], sp['z'], sp['dt'], sp['da'], sp['dat'], sp['e'], sp['vec'],
                            sp['vec']],
                  out_specs=[sp['z'], sp['z'], sp['hin']],
                  out_shape=[_sds((b, l, ds), BF16), _sds((b, l, ds), BF16), _sds((b, nc, n, ds), F32)],
                  scratch_shapes=[pltpu.VMEM((n, ds), F32), pltpu.VMEM((q, ds), F32)],
                  compiler_params=_cparams(("parallel", "arbitrary")))(xbc, xbc, xbc, z, dt, da, dat, e_mat, dx, nw)


def _ssd_bwd(xbc, z, dt, da, dat, e_mat, et_mat, dx, nw, a128, yraw, hin, dy, name):
    b, l, dxw = xbc.shape
    ds = z.shape[2]
    heads, hpg, gw = _ssd_dims(ds)
    q, n = SSD_CHUNK, SSD_STATE
    gn = SSD_GROUPS * n
    nc = l // q
    hcol0 = ATTN_HEADS

    def body(xs_ref, bm_ref, cm_ref, z_ref, dt_ref, da_ref, dat_ref, e_ref, et_ref, dx_ref, nw_ref, a_ref, yraw_ref,
             hin_ref, dy_ref, dxs_ref, dbm_ref, dcm_ref, dz_ref, ddt_ref, dd_ref, dnw_ref, dap_ref, dhs, dxdt, dcsx,
             dtotx):
        i, c = pl.program_id(0), pl.program_id(1)

        @pl.when(c == 0)
        def _():
            dhs[...] = jnp.zeros_like(dhs)

        @pl.when((i == 0) & (c == 0))
        def _():
            dd_ref[...] = jnp.zeros_like(dd_ref)
            dnw_ref[...] = jnp.zeros_like(dnw_ref)
            dap_ref[...] = jnp.zeros_like(dap_ref)

        dtv = dt_ref[...]
        lower, cs, cst, dtx, csx, totx, xf, xdt = _ssd_common(da_ref[...], dat_ref[...], dtv, e_ref[...], xs_ref[...])
        upper = jnp.logical_not(lower) | (lax.broadcasted_iota(jnp.int32, (q, q), 0)
                                          == lax.broadcasted_iota(jnp.int32, (q, q), 1))
        bm, cm = bm_ref[...], cm_ref[...]
        ecs, dec_end, etot = jnp.exp(csx), jnp.exp(totx - csx), jnp.exp(totx)
        yraw = yraw_ref[...].astype(F32)
        zv = z_ref[...].astype(F32)
        sz = _silu(zv)
        yg = yraw * sz
        dyn_ = dy_ref[...].astype(F32)
        nwv = nw_ref[...]
        dygs, dnws = [], []
        for g in range(SSD_GROUPS):
            gs = slice(g * gw, (g + 1) * gw)
            r = lax.rsqrt(jnp.mean(yg[:, gs] * yg[:, gs], axis=1, keepdims=True) + NORM_EPS)
            yn = yg[:, gs] * r
            dn = dyn_[:, gs] * nwv[:, gs]
            dnws.append(jnp.sum(dyn_[:, gs] * yn, axis=0, keepdims=True))
            dygs.append(r * (dn - yn * jnp.mean(dn * yn, axis=1, keepdims=True)))
        dyg = jnp.concatenate(dygs, axis=1)
        dnw_ref[...] += jnp.concatenate(dnws, axis=1)
        dz_ref[...] = (dyg * yraw * _dsilu(zv)).astype(dz_ref.dtype)
        dyv = dyg * sz
        dd_ref[...] += jnp.sum(dyv * xf, axis=0, keepdims=True)
        hin, dh = hin_ref[...], dhs[...]
        lane128 = lax.broadcasted_iota(jnp.int32, (1, SMALL_W), 1)
        dcs = jnp.zeros((q, SMALL_W), F32)
        for g in range(SSD_GROUPS):
            gs = slice(g * gw, (g + 1) * gw)
            bg, cg = bm[:, g * n:(g + 1) * n], cm[:, g * n:(g + 1) * n]
            hg, dhg = hin[:, gs], dh[:, gs]
            hgb, dsb = hg.astype(BF16), dhg.astype(BF16)
            yoff = _dot(cg, hgb) * ecs[:, gs]
            dch = (dyv[:, gs] * ecs[:, gs]).astype(BF16)
            dcg = _dot_nt(dch, hgb)
            dhs[:, gs] = _dot_tn(cg, dch) + etot[:, gs] * dhg
            zg = xdt[:, gs] * dec_end[:, gs]
            dzz = _dot(bg, dsb)
            dbg = _dot_nt(zg.astype(BF16), dsb)
            dxdt_g = dzz * dec_end[:, gs]
            w_end = dzz * zg
            dtotx[:, gs] = jnp.sum(dhg * hg, axis=0, keepdims=True) * etot[:, gs] + jnp.sum(w_end, axis=0, keepdims=True)
            dcsx[:, gs] = dyv[:, gs] * yoff - w_end
            cb, cbt = _dot_nt(cg, bg), _dot_nt(bg, cg)
            dgm = jnp.zeros((q, q), F32)
            for e in range(hpg):
                hh = g * hpg + e
                cc = hcol0 + hh
                ccol, crow = cs[:, cc:cc + 1], cst[cc:cc + 1, :]
                lm = jnp.exp(jnp.where(lower, ccol - crow, NEG))
                lmt = jnp.exp(jnp.where(upper, crow - ccol, NEG))
                mm, mt = cb * lm, cbt * lmt
                hs = slice(hh * SSD_HEAD_DIM, (hh + 1) * SSD_HEAD_DIM)
                dye, xe = dyv[:, hs].astype(BF16), xdt[:, hs].astype(BF16)
                dm, dmt = _dot_nt(dye, xe), _dot_nt(xe, dye)
                dxdt[:, hs] = dxdt_g[:, e * SSD_HEAD_DIM:(e + 1) * SSD_HEAD_DIM] + _dot(mt.astype(BF16), dye)
                dgm = dgm + dm * lm
                rs = jnp.sum(dm * mm, axis=1, keepdims=True) - jnp.sum(dmt * mt, axis=1, keepdims=True)
                dcs = dcs + rs * (lane128 == cc).astype(F32)
            dgb = dgm.astype(BF16)
            dcm_ref[:, g * n:(g + 1) * n] = (dcg + _dot(dgb, bg)).astype(dcm_ref.dtype)
            dbm_ref[:, g * n:(g + 1) * n] = (dbg + _dot_tn(dgb, cg)).astype(dbm_ref.dtype)
        dxd = dxdt[...]
        dxs_ref[...] = (dx_ref[...] * dyv + dxd * dtx).astype(dxs_ref.dtype)
        et = et_ref[...]
        ddt = _dot_sel(dxd * xf, et, 'b')
        dtot128 = _dot_sel(jnp.broadcast_to(dtotx[...], (8, ds)), et, 'b')[0:1, :]
        row128 = lax.broadcasted_iota(jnp.int32, (q, SMALL_W), 0)
        dcs = dcs + _dot_sel(dcsx[...], et, 'b') + jnp.where(row128 == q - 1, dtot128, 0.0)
        dda = _dot_sel(upper.astype(F32), dcs, 'a')
        ddt_ref[...] = ddt + dda * a_ref[...]
        dap_ref[...] += jnp.sum(dda * dtv, axis=0, keepdims=True)

    rev = lambda c: nc - 1 - c
    sp = _ssd_specs(l, ds, rev)
    row3 = lambda w: pl.BlockSpec((None, q, w), lambda i, c: (i, rev(c), 0))
    acc = lambda w: pl.BlockSpec((1, w), lambda i, c: (0, 0))
    return _pcall(body, name=name, grid=(b, nc),
                  in_specs=[sp['xs'], sp['bm'], sp['cm'], sp['z'], sp['dt'], sp['da'], sp['dat'], sp['e'], sp['et'],
                            sp['vec'], sp['vec'], sp['vec128'], sp['z'], sp['hin'], sp['z']],
                  out_specs=[row3(ds), row3(gn), row3(gn), row3(ds), row3(SMALL_W), acc(ds), acc(ds), acc(SMALL_W)],
                  out_shape=[_sds((b, l, ds), BF16), _sds((b, l, gn), BF16), _sds((b, l, gn), BF16), _sds((b, l, ds), BF16),
                             _sds((b, l, SMALL_W), F32), _sds((1, ds), F32), _sds((1, ds), F32), _sds((1, SMALL_W), F32)],
                  scratch_shapes=[pltpu.VMEM((n, ds), F32), pltpu.VMEM((q, ds), F32), pltpu.VMEM((q, ds), F32),
                                  pltpu.VMEM((1, ds), F32)],
                  compiler_params=_cparams(("arbitrary", "arbitrary")))(
                      xbc, xbc, xbc, z, dt, da, dat, e_mat, et_mat, dx, nw, a128, yraw, hin, dy)


_GROUP_SIZE = {'c': 2, 'xy': 4, 'xyc': 8}
_LOCAL_SPLIT = 16


def _exchange(src, group, scatter, name, nsplit=1, copy_own=True):
    n = _GROUP_SIZE[group]
    rows, width = src.shape[-2:]
    assert src.ndim == (3 if scatter else 2)
    while rows % (8 * nsplit):
        nsplit //= 2
    crow = rows // nsplit
    nlocal = _LOCAL_SPLIT
    while rows % (8 * nlocal):
        nlocal //= 2
    lrow = rows // nlocal

    def body(src_ref, out_ref, send_sems, recv_sems, local_sems):
        x, y, c = lax.axis_index("x"), lax.axis_index("y"), lax.axis_index("c")
        if group == 'c':
            rank = c
            dev = lambda r: (x, y, r)
        elif group == 'xy':
            rank = 2 * x + y
            dev = lambda r: (r // 2, r % 2, c)
        else:
            rank = 4 * x + 2 * y + c
            dev = lambda r: (r // 4, (r // 2) % 2, r % 2)

        def mine_for(r, ck):
            piece = src_ref.at[r] if scatter else src_ref
            return piece.at[pl.ds(ck * crow, crow)]

        def copy(k, ck, pr, dst_rank):
            return pltpu.make_async_remote_copy(
                src_ref=mine_for(pr, ck), dst_ref=out_ref.at[dst_rank].at[pl.ds(ck * crow, crow)],
                send_sem=send_sems.at[k * nsplit + ck], recv_sem=recv_sems.at[k * nsplit + ck], device_id=dev(pr),
                device_id_type=pl.DeviceIdType.MESH)

        locals_ = []
        if copy_own:
            own = src_ref.at[rank] if scatter else src_ref
            for ck in range(nlocal):
                rs = pl.ds(ck * lrow, lrow)
                locals_.append(pltpu.make_async_copy(own.at[rs], out_ref.at[rank].at[rs], local_sems.at[ck]))
                locals_[-1].start()
        peers = [jnp.bitwise_xor(rank, k + 1) for k in range(n - 1)]
        sends = [copy(k, ck, pr, rank) for ck in range(nsplit) for k, pr in enumerate(peers)]
        for cp in sends:
            cp.start()
        for ck in range(nsplit):
            for k, pr in enumerate(peers):
                copy(k, ck, pr, pr).wait_recv()
        for cp in sends:
            cp.wait_send()
        for cp in locals_:
            cp.wait()

    return _pcall(body, name=name, in_specs=[pl.BlockSpec(memory_space=pl.ANY)],
                  out_specs=pl.BlockSpec(memory_space=pl.ANY), out_shape=_sds((n, rows, width), src.dtype),
                  scratch_shapes=[pltpu.SemaphoreType.DMA(((n - 1) * nsplit,)),
                                  pltpu.SemaphoreType.DMA(((n - 1) * nsplit,)),
                                  pltpu.SemaphoreType.DMA((nlocal,))])(src)


def _exchange_multi(srcs, group, scatter, name, single=False, min_copies=16):
    n = _GROUP_SIZE[group]
    assert not single or n == 2
    na = len(srcs)
    shapes = [tuple(s.shape[-2:]) for s in srcs]
    want = max(1, -(-min_copies // (na * (n - 1))))
    splits = []
    for (rows, _), s in zip(shapes, srcs):
        quant = 8 * (4 // s.dtype.itemsize)
        k = want
        while k > 1 and rows % (quant * k):
            k -= 1
        splits.append(k)
    offs = [int(v) for v in np.cumsum([0] + [(n - 1) * k for k in splits])]

    def body(*refs):
        src_refs, out_refs = refs[:na], refs[na:2 * na]
        send_sems, recv_sems = refs[2 * na:]
        x, y, c = lax.axis_index("x"), lax.axis_index("y"), lax.axis_index("c")
        if group == 'c':
            rank = c
            dev = lambda r: (x, y, r)
        elif group == 'xy':
            rank = 2 * x + y
            dev = lambda r: (r // 2, r % 2, c)
        else:
            rank = 4 * x + 2 * y + c
            dev = lambda r: (r // 4, (r // 2) % 2, r % 2)
        peers = [jnp.bitwise_xor(rank, k + 1) for k in range(n - 1)]

        def copy(a, k, ck, dst_rank):
            crow = shapes[a][0] // splits[a]
            rs = pl.ds(ck * crow, crow)
            piece = src_refs[a].at[peers[k]] if scatter else src_refs[a]
            dst = out_refs[a] if single else out_refs[a].at[dst_rank]
            sem = offs[a] + k * splits[a] + ck
            return pltpu.make_async_remote_copy(src_ref=piece.at[rs], dst_ref=dst.at[rs], send_sem=send_sems.at[sem],
                                                recv_sem=recv_sems.at[sem], device_id=dev(peers[k]),
                                                device_id_type=pl.DeviceIdType.MESH)

        todo = [(a, k, ck) for a in range(na) for ck in range(splits[a]) for k in range(n - 1)]
        sends = [copy(a, k, ck, rank) for a, k, ck in todo]
        for cp in sends:
            cp.start()
        for a, k, ck in todo:
            copy(a, k, ck, peers[k]).wait_recv()
        for cp in sends:
            cp.wait_send()

    any_spec = pl.BlockSpec(memory_space=pl.ANY)
    out_shape = [_sds(sh if single else (n,) + sh, s.dtype) for sh, s in zip(shapes, srcs)]
    return _pcall(body, name=name, in_specs=[any_spec] * na, out_specs=[any_spec] * na, out_shape=out_shape,
                  scratch_shapes=[pltpu.SemaphoreType.DMA((offs[-1],)), pltpu.SemaphoreType.DMA((offs[-1],))])(*srcs)


def _sum_slots(arr, out_dtype, name):
    n, rows, cols = arr.shape
    tm = _pick(rows, [c for c in (384, 256, 128, 64, 32, 16, 8) if c * cols <= _ROWWISE_TILE_ELEMS or c == 8])

    def body(*refs):
        acc = refs[0][...].astype(F32)
        for r in refs[1:n]:
            acc = acc + r[...].astype(F32)
        refs[n][...] = acc.astype(refs[n].dtype)

    return _pcall(body, name=name, grid=(rows // tm,),
                  in_specs=[pl.BlockSpec((None, tm, cols), lambda i, j=j: (j, i, 0)) for j in range(n)],
                  out_specs=pl.BlockSpec((tm, cols), lambda i: (i, 0)), out_shape=_sds((rows, cols), out_dtype),
                  compiler_params=_cparams(("parallel",)))(*([arr] * n))


def _dims():
    d = D_MODEL
    h = ATTN_HEADS
    d_ssd = d
    d_xbc = d_ssd + 2 * SSD_GROUPS * SSD_STATE
    sizes = (d, d, d, h, d_ssd, d_xbc, d_ssd // SSD_HEAD_DIM, d, d, 3 * d)
    return d, h, d_ssd, d_xbc, sizes


def _w_in_split(w):
    d, h, d_ssd, d_xbc, sizes = _dims()
    off = np.concatenate([[0], np.cumsum(sizes)])
    seg = lambda i: w[..., off[i]:off[i + 1]]
    main = jnp.concatenate([seg(0), seg(1), seg(2), seg(4), seg(5), seg(7), seg(8), seg(9)], axis=-1)
    pad = jnp.zeros(w.shape[:-1] + (SMALL_W - sizes[3] - sizes[6],), w.dtype)
    small = jnp.concatenate([seg(3), seg(6), pad], axis=-1)
    return main, small


def _w_in_merge(main, small):
    d, h, d_ssd, d_xbc, sizes = _dims()
    order = (0, 1, 2, 4, 5, 7, 8, 9)
    moff = np.concatenate([[0], np.cumsum([sizes[i] for i in order])])
    pieces = {i: main[..., moff[j]:moff[j + 1]] for j, i in enumerate(order)}
    pieces[3] = small[..., :sizes[3]]
    pieces[6] = small[..., sizes[3]:sizes[3] + sizes[6]]
    return jnp.concatenate([pieces[i] for i in range(10)], axis=-1)


def _main_offsets():
    d, h, d_ssd, d_xbc, sizes = _dims()
    names = ('q', 'k', 'v', 'z', 'xbc', 'xr', 'gate', 'merge')
    widths = (d, d, d, d_ssd, d_xbc, d, d, 3 * d)
    off = np.concatenate([[0], np.cumsum(widths)])
    return {nm: (int(off[i]), int(off[i + 1])) for i, nm in enumerate(names)}


def _block_diag(w):
    nb, s, _ = w.shape
    eye = jnp.eye(nb, dtype=w.dtype)
    return (eye[:, None, :, None] * w[:, :, None, :]).reshape(nb * s, nb * s)


def _diag_blocks(wd, nb):
    s = wd.shape[0] // nb
    return jnp.stack([wd[i * s:(i + 1) * s, i * s:(i + 1) * s] for i in range(nb)])


def _vec128(*parts):
    v = jnp.concatenate([p.astype(F32) for p in parts])
    return jnp.pad(v, (0, SMALL_W - v.shape[0]))[None, :]


def _ffn_fwd(h, gnorm, w, tag):
    xn = _rms_fwd(h, gnorm[None, :], f"{tag}_norm")
    g, u, act = _mm_swiglu(xn, w['wg'], w['wu'], f"{tag}_gu")
    out = _mm_nn(act, w['wd'], F32, res=h, alpha=0.5, name=f"{tag}_down")
    return out, (h, xn, g, u, act)


def _ffn_bwd(dout, saved, gnorm, w, tag):
    h, xn, g, u, act = saved
    dg, du = _mm_dswiglu(dout, w['wd_t'], g, u, 0.5, f"{tag}_dgu")
    dwd = _mm_tn(act, dout, alpha=0.5, name=f"{tag}_dwd")
    dwgu = jnp.concatenate([_mm_tn(xn, dg, name=f"{tag}_dwg"), _mm_tn(xn, du, name=f"{tag}_dwu")], axis=1)
    dxn = _mm_nn(dg, w['wg_t'], F32, name=f"{tag}_dxn_g")
    dxn = _mm_nn(du, w['wu_t'], F32, res=dxn, name=f"{tag}_dxn_u")
    dh, dgn = _rms_bwd(h, dxn, dout, gnorm[None, :], f"{tag}_dnorm")
    return dh, dgn[0], dwgu, dwd


def _mixer_fwd(h, p, b, l):
    d, nh, d_ssd, d_xbc, sizes = _dims()
    t = b * l
    off = _main_offsets()
    xn = _rms_fwd(h, p['mix_norm'][None, :], "mix_norm")
    pm = _mm_nn(xn, p['w_main'], BF16, name="mix_in_main")
    ps = _mm_nn(xn, p['w_small'], F32, name="mix_in_small")
    col = lambda nm: pm[:, off[nm][0]:off[nm][1]]
    heads_ssd = d_ssd // SSD_HEAD_DIM
    a_neg = -jnp.exp(p['ssd_a_log'])
    fb = _vec128(p['fox_forget_bias'])
    dtb = _vec128(jnp.zeros((nh,), F32), p['ssd_dt_bias'])
    a128 = _vec128(jnp.zeros((nh,), F32), a_neg)

    def prep(_, v, fbv, dtbv, av):
        lane = lax.broadcasted_iota(jnp.int32, (1, SMALL_W), 1)
        logf = jnp.where(lane < nh, -_softplus(-(v + fbv)), 0.0)
        dtv = jnp.where((lane >= nh) & (lane < nh + heads_ssd), _softplus(v + dtbv), 0.0)
        return logf, dtv, dtv * av
    logf, dt, da = _rowwise(prep, [ps], [(SMALL_W, F32)] * 3, bcast=[fb, dtb, a128], name="mix_prep")

    cum = _cumsum_seq(logf.reshape(b, l, SMALL_W), False, "fox_cumsum")
    qa, ka, va = _attn_prep(pm.reshape(b, l, -1), cum, (off['q'][0], off['k'][0], off['v'][0]), "fox_prep")
    y_a3, lse = _flash_fwd(qa, ka, va, d, "fox_fwd")
    y_a = y_a3.reshape(t, d)

    xbc = col('xbc').reshape(b, l, d_xbc)
    pre_b, xbc_act = _conv_fwd(xbc, p['ssd_conv_w'], p['ssd_conv_b'][None, :], BF16, "ssd_conv", with_silu=True)
    z = col('z').reshape(b, l, d_ssd)
    dt3, da3 = dt.reshape(b, l, SMALL_W), da.reshape(b, l, SMALL_W)
    dat3 = da3.transpose(0, 2, 1)
    e_mat = _expand_matrix(nh, heads_ssd)
    dx = jnp.repeat(p['ssd_d'], SSD_HEAD_DIM)[None, :]
    nw = p['ssd_norm'][None, :]
    y_b3, yraw, hin = _ssd_fwd(xbc_act, z, dt3, da3, dat3, e_mat, dx, nw, "ssd_fwd")
    y_b = y_b3.reshape(t, d_ssd)

    xr = col('xr').reshape(b, l, d)
    xc = _conv_fwd(xr, p['lru_conv_w'], p['lru_conv_b'][None, :], F32, "lru_conv").reshape(t, d)
    pre_ri = _mm_nn(xc, p['lru_w_ri'], F32, name="lru_gates")
    lvec = (p['lru_b_a'][None, :], p['lru_b_x'][None, :], p['lru_lambda'][None, :])
    a_l, u_l = _rowwise(_lru_point_fwd, [pre_ri, xc], [(d, F32)] * 2, bcast=lvec, name="lru_point", period=l)
    gate = col('gate')
    hs, y_c = _linear_scan(a_l.reshape(b, l, d), u_l.reshape(b, l, d), False, "lru_scan", gate=gate.reshape(b, l, d))
    hs, y_c = hs.reshape(t, d), y_c.reshape(t, d)

    ba = _mm_nn(y_a, p['w_branch_attn'], BF16, name="branch_attn")
    bb = _mm_nn(y_b, p['w_branch_ssd'], BF16, name="branch_ssd")
    bc = _mm_nn(y_c, p['w_branch_lru'], BF16, name="branch_lru")
    mg = col('merge')
    mixed = _merge_fwd(mg, ba, bb, bc, "merge")
    out = _mm_nn(mixed, p['w_out'], F32, res=h, name="mix_out")
    saved = dict(h=h, xn=xn, ps=ps, fb=fb, dtb=dtb, a128=a128, qa=qa, ka=ka, va=va, lse=lse,
                 xbc=xbc, pre_b=pre_b, xbc_act=xbc_act, z=z, dt3=dt3, da3=da3, dat3=dat3, e_mat=e_mat, dx=dx, nw=nw,
                 yraw=yraw, hin=hin, xr=xr, xc=xc, pre_ri=pre_ri, lvec=lvec, a_l=a_l, hs=hs, gate=gate, y_a=y_a, y_b=y_b,
                 y_c=y_c, ba=ba, bb=bb, bc=bc, mg=mg, mixed=mixed)
    return out, saved


def _expand_matrix(nh, heads_ssd):
    e = np.zeros((SMALL_W, heads_ssd * SSD_HEAD_DIM), np.float32)
    for hh in range(heads_ssd):
        e[nh + hh, hh * SSD_HEAD_DIM:(hh + 1) * SSD_HEAD_DIM] = 1.0
    return jnp.asarray(e)


def _lru_gates(pre, xc, bav, bxv, lamv, pos):
    d = xc.shape[1]
    r = _sigmoid(pre[:, :d] + bav)
    i = _sigmoid(pre[:, d:] + bxv)
    ls = -_softplus(-lamv)
    la = LRU_C * r * ls
    a = jnp.exp(la)
    mult = jnp.where(pos == 0, 1.0, jnp.sqrt(-_expm1(2.0 * la)))
    return r, i, ls, a, mult


def _lru_point_fwd(pos, pre, xc, bav, bxv, lamv):
    r, i, ls, a, mult = _lru_gates(pre, xc, bav, bxv, lamv, pos)
    return a, mult * (i * xc)


def _lru_point_bwd(pos, g, hprev, pre, xc, bav, bxv, lamv):
    r, i, ls, a, mult = _lru_gates(pre, xc, bav, bxv, lamv, pos)
    da = g * hprev
    di = g * mult * xc
    dxc = g * mult * i
    dmult = jnp.where(pos == 0, 0.0, g * i * xc)
    dla = da * a - dmult * (a * a) / mult
    dpre_r = dla * (LRU_C * ls) * r * (1.0 - r)
    dpre_i = di * i * (1.0 - i)
    dlam = jnp.sum(dla * (LRU_C * r), axis=0, keepdims=True) * _sigmoid(-lamv)
    return (jnp.concatenate([dpre_r, dpre_i], axis=1), dxc, dlam, jnp.sum(dpre_r, axis=0, keepdims=True),
            jnp.sum(dpre_i, axis=0, keepdims=True))


def _mixer_bwd(dout, s, p, b, l):
    d, nh, d_ssd, d_xbc, sizes = _dims()
    t = b * l
    heads_ssd = d_ssd // SSD_HEAD_DIM
    g = {}
    dmixed = _mm_nn(dout, p['w_out_t'], BF16, name="mix_out_dx")
    g['w_out'] = _mm_tn(s['mixed'], dout, name="mix_out_dw")
    dba, dbb, dbc, dmerge = _merge_bwd(s['mg'], s['ba'], s['bb'], s['bc'], dmixed, "merge_bwd")
    g['w_branch_attn'] = _mm_tn(s['y_a'], dba, name="branch_attn_dw")
    g['w_branch_ssd'] = _mm_tn(s['y_b'], dbb, name="branch_ssd_dw")
    g['w_branch_lru'] = _mm_tn(s['y_c'], dbc, name="branch_lru_dw")
    dy_a = _mm_nn(dba, p['w_branch_attn_t'], BF16, name="branch_attn_dx")
    dy_b = _mm_nn(dbb, p['w_branch_ssd_t'], BF16, name="branch_ssd_dx")
    dy_c = _mm_nn(dbc, p['w_branch_lru_t'], F32, name="branch_lru_dx")

    dgate, dhs = _rowwise(lambda _, dv, hv, gv: (dv * hv * _dgelu(gv.astype(F32)), dv * _gelu(gv.astype(F32))),
                          [dy_c, s['hs'], s['gate']], [(d, BF16), (d, F32)], name="lru_out_bwd")
    a3 = s['a_l'].reshape(b, l, d)
    a_next = jnp.concatenate([a3[:, 1:], jnp.zeros((b, 1, d), F32)], axis=1)
    gs = _linear_scan(a_next, dhs.reshape(b, l, d), True, "lru_scan_bwd").reshape(t, d)
    h3 = s['hs'].reshape(b, l, d)
    hprev = jnp.concatenate([jnp.zeros((b, 1, d), F32), h3[:, :-1]], axis=1).reshape(t, d)
    dpre_ri, dxc0, dlam, dba_, dbx_ = _rowwise(_lru_point_bwd, [gs, hprev, s['pre_ri'], s['xc']],
                                               [(2 * d, BF16), (d, F32)], bcast=s['lvec'],
                                               reds=[(1, d)] * 3, name="lru_point_bwd", period=l)
    g['lru_lambda'], g['lru_b_a'], g['lru_b_x'] = dlam[0], dba_[0], dbx_[0]
    dxc = _mm_nn(dpre_ri, p['lru_w_ri_t'], BF16, res=dxc0, name="lru_gates_dx")
    dw_ri = _mm_tn(s['xc'], dpre_ri, name="lru_gates_dw")
    g['lru_w_a'] = _diag_blocks(dw_ri[:, :d], LRU_BLOCKS)
    g['lru_w_x'] = _diag_blocks(dw_ri[:, d:], LRU_BLOCKS)
    dxr, dwl = _conv_bwd(s['xr'], dxc.reshape(b, l, d), p['lru_conv_w'], "lru_conv_bwd")
    g['lru_conv_w'], g['lru_conv_b'] = dwl[:CONV_K], dwl[CONV_K]

    et_mat = s['e_mat'].T
    dxs, dbm, dcm, dz, ddt, dd_l, dnw, dap = _ssd_bwd(s['xbc_act'], s['z'], s['dt3'], s['da3'], s['dat3'], s['e_mat'],
                                                      et_mat, s['dx'], s['nw'], s['a128'], s['yraw'], s['hin'],
                                                      dy_b.reshape(b, l, d_ssd), "ssd_bwd")
    g['ssd_d'] = dd_l.reshape(heads_ssd, SSD_HEAD_DIM).sum(axis=1)
    g['ssd_norm'] = dnw[0]
    g['ssd_a_log'] = dap[0, nh:nh + heads_ssd] * (-jnp.exp(p['ssd_a_log']))
    dxbc_act = jnp.concatenate([dxs, dbm, dcm], axis=2).reshape(t, d_xbc)
    dpre_b = _rowwise(lambda _, dv, pv: dv.astype(F32) * _dsilu(pv.astype(F32)),
                      [dxbc_act, s['pre_b'].reshape(t, d_xbc)], [(d_xbc, BF16)], name="ssd_conv_act_bwd")[0]
    dxbc, dws = _conv_bwd(s['xbc'], dpre_b.reshape(b, l, d_xbc), p['ssd_conv_w'], "ssd_conv_bwd")
    g['ssd_conv_w'], g['ssd_conv_b'] = dws[:CONV_K], dws[CONV_K]

    qa2, doa = _attn_prep_bwd(dy_a.reshape(b, l, d), s['y_a'].reshape(b, l, d), s['qa'], s['lse'], "fox_prep_bwd")
    dq3, dk3, dv3, dcq, dck = _flash_bwd(qa2, s['ka'], s['va'], doa, d, "fox_bwd")
    dcum = jnp.pad((dcq - dck)[..., 0].transpose(0, 2, 1), ((0, 0), (0, 0), (0, SMALL_W - nh)))
    dlogf = _cumsum_seq(dcum, True, "fox_cumsum_bwd").reshape(t, SMALL_W)

    def prep_bwd(_, v, dlf, ddtv, fbv, dtbv):
        a_ = dlf * _sigmoid(-(v + fbv))
        b_ = ddtv * _sigmoid(v + dtbv)
        return a_ + b_, jnp.sum(a_, axis=0, keepdims=True), jnp.sum(b_, axis=0, keepdims=True)
    dps, dfb, ddtb = _rowwise(prep_bwd, [s['ps'], dlogf, ddt.reshape(t, SMALL_W)], [(SMALL_W, F32)],
                              bcast=[s['fb'], s['dtb']], reds=[(1, SMALL_W)] * 2, name="mix_prep_bwd")
    g['fox_forget_bias'] = dfb[0, :nh]
    g['ssd_dt_bias'] = ddtb[0, nh:nh + heads_ssd]

    dpm = jnp.concatenate([dq3.reshape(t, d), dk3.reshape(t, d), dv3.reshape(t, d),
                           dz.reshape(t, d_ssd), dxbc.reshape(t, d_xbc), dxr.reshape(t, d), dgate, dmerge], axis=1)
    dxn = _mm_nn(dps, p['w_small_t'], F32, name="mix_in_small_dx")
    dxn = _mm_nn(dpm, p['w_main_t'], F32, res=dxn, name="mix_in_main_dx")
    g['w_main'] = _mm_tn(s['xn'], dpm, name="mix_in_main_dw")
    g['w_small'] = _mm_tn(s['xn'], dps, name="mix_in_small_dw")
    dh, dg = _rms_bwd(s['h'], dxn, dout, p['mix_norm'][None, :], "mix_norm_bwd")
    g['mix_norm'] = dg[0]
    return dh, g


def _layer_params(w, li):
    p = {n: w[n][li] for n in WEIGHTS if n not in ('meta_tokens', 'final_norm')}
    bf = lambda a: a.astype(BF16)
    for tag in ('ffn1', 'ffn2'):
        wgu, wd = bf(p[tag + '_w_gate_up']), bf(p[tag + '_w_down'])
        f = wd.shape[0]
        p[tag] = dict(wg=wgu[:, :f], wu=wgu[:, f:], wg_t=wgu[:, :f].T, wu_t=wgu[:, f:].T, wd=wd, wd_t=wd.T)
    wm, ws = _w_in_split(bf(p['w_in']))
    p['w_main'], p['w_main_t'], p['w_small'], p['w_small_t'] = wm, wm.T, ws, ws.T
    for n in ('w_branch_attn', 'w_branch_ssd', 'w_branch_lru', 'w_out'):
        p[n + '_t'] = bf(p[n]).T
        p[n] = bf(p[n])
    wri = jnp.concatenate([_block_diag(p['lru_w_a']), _block_diag(p['lru_w_x'])], axis=1)
    p['lru_w_ri'], p['lru_w_ri_t'] = bf(wri), bf(wri).T
    return p


def _local_step(x, loss_target, w):
    b, seq, d = x.shape
    length = N_META + seq
    l = -(-length // Q_BLOCK) * Q_BLOCK
    t = b * l
    meta = jnp.broadcast_to(w['meta_tokens'].astype(F32)[None], (b, N_META, d))
    h = jnp.concatenate([meta, x, jnp.zeros((b, l - length, d), F32)], axis=1).reshape(t, d)
    tgt = jnp.concatenate([jnp.zeros((b, N_META, d), F32), loss_target, jnp.zeros((b, l - length, d), F32)],
                          axis=1).reshape(t, d)
    params, saves = [], []
    for li in range(DEPTH):
        p = _layer_params(w, li)
        h, s1 = _ffn_fwd(h, p['ffn1_norm'], p['ffn1'], "ffn1")
        h, sm = _mixer_fwd(h, p, b, l)
        h, s2 = _ffn_fwd(h, p['ffn2_norm'], p['ffn2'], "ffn2")
        params.append(p)
        saves.append((s1, sm, s2))
    dh, loss, dgf = _loss_head(h, tgt, w['final_norm'][None, :], l, "loss_head")
    layer_grads = [None] * DEPTH
    for li in reversed(range(DEPTH)):
        p = params[li]
        s1, sm, s2 = saves[li]
        g = {}
        dh, g['ffn2_norm'], g['ffn2_w_gate_up'], g['ffn2_w_down'] = _ffn_bwd(dh, s2, p['ffn2_norm'], p['ffn2'], "ffn2b")
        dh, gm = _mixer_bwd(dh, sm, p, b, l)
        g.update(gm)
        g['w_in'] = _w_in_merge(g.pop('w_main'), g.pop('w_small'))
        dh, g['ffn1_norm'], g['ffn1_w_gate_up'], g['ffn1_w_down'] = _ffn_bwd(dh, s1, p['ffn1_norm'], p['ffn1'], "ffn1b")
        layer_grads[li] = g
    grads = {n: jnp.stack([layer_grads[li][n] for li in range(DEPTH)]) for n in layer_grads[0]}
    for n in ('lru_w_a', 'lru_w_x'):
        grads[n] = grads[n].reshape(w[n].shape)
    dh3 = dh.reshape(b, l, d)
    grads['meta_tokens'] = jnp.sum(dh3[:, :N_META], axis=0)
    grads['final_norm'] = dgf[0]
    return loss, dh3[:, N_META:N_META + seq], grads


def _unflatten(flat, shapes):
    out, o = [], 0
    for sh in shapes:
        n = int(np.prod(sh))
        out.append(flat[o:o + n].reshape(sh))
        o += n
    return out


def kernel(x, meta_tokens, ffn1_norm, ffn1_w_gate_up, ffn1_w_down, mix_norm, w_in, fox_forget_bias, ssd_conv_w, ssd_conv_b, ssd_dt_bias, ssd_a_log, ssd_d, ssd_norm, lru_conv_w, lru_conv_b, lru_w_a, lru_b_a, lru_w_x, lru_b_x, lru_lambda, w_branch_attn, w_branch_ssd, w_branch_lru, w_out, ffn2_norm, ffn2_w_gate_up, ffn2_w_down, final_norm, loss_target, m_meta_tokens, m_ffn1_norm, m_ffn1_w_gate_up, m_ffn1_w_down, m_mix_norm, m_w_in, m_fox_forget_bias, m_ssd_conv_w, m_ssd_conv_b, m_ssd_dt_bias, m_ssd_a_log, m_ssd_d, m_ssd_norm, m_lru_conv_w, m_lru_conv_b, m_lru_w_a, m_lru_b_a, m_lru_w_x, m_lru_b_x, m_lru_lambda, m_w_branch_attn, m_w_branch_ssd, m_w_branch_lru, m_w_out, m_ffn2_norm, m_ffn2_w_gate_up, m_ffn2_w_down, m_final_norm, v_meta_tokens, v_ffn1_norm, v_ffn1_w_gate_up, v_ffn1_w_down, v_mix_norm, v_w_in, v_fox_forget_bias, v_ssd_conv_w, v_ssd_conv_b, v_ssd_dt_bias, v_ssd_a_log, v_ssd_d, v_ssd_norm, v_lru_conv_w, v_lru_conv_b, v_lru_w_a, v_lru_b_a, v_lru_w_x, v_lru_b_x, v_lru_lambda, v_w_branch_attn, v_w_branch_ssd, v_w_branch_lru, v_w_out, v_ffn2_norm, v_ffn2_w_gate_up, v_ffn2_w_down, v_final_norm):
    args = locals()
    wloc = {n: args[n] for n in WEIGHTS}
    mloc = {n: args['m_' + n] for n in WEIGHTS}
    vloc = {n: args['v_' + n] for n in WEIGHTS}
    nchip = 4
    chip = 2 * lax.axis_index("x") + lax.axis_index("y")
    core = lax.axis_index("c")
    hl = DEPTH // 2

    own = lambda out, mine, rank: lax.dynamic_update_index_in_dim(out, mine, rank, 0)
    half_rows = lambda a, which: lax.dynamic_slice_in_dim(a, which * (a.shape[0] // 2), a.shape[0] // 2, axis=0)
    mine = [half_rows(wloc[n].astype(BF16).reshape(-1, wloc[n].shape[-1]), core) for n in BIG_NAMES]
    got = _exchange_multi(mine, 'xy', False, "gather_w_chips")
    got = [own(g_, m_, chip).reshape(nchip * m_.shape[0], m_.shape[1]) for g_, m_ in zip(got, mine)]
    both = _exchange_multi(got, 'c', False, "gather_w_cores")
    full = {}
    for n, b_, g_ in zip(BIG_NAMES, both, got):
        _, r, c = wloc[n].shape
        v = own(b_, g_, core).reshape(2, nchip, hl, r, c)
        if BIG[n] == 1:
            full[n] = v.transpose(0, 2, 3, 1, 4).reshape(DEPTH, r, nchip * c)
        else:
            full[n] = v.transpose(0, 2, 1, 3, 4).reshape(DEPTH, nchip * r, c)
    cs_shapes = [wloc[n].shape for n in COLSHARD_SMALL]
    cs_total = sum(int(np.prod(s)) for s in cs_shapes)
    cs_rows = -(-cs_total // (8 * 128)) * 8
    cs_flat = jnp.concatenate([wloc[n].reshape(-1) for n in COLSHARD_SMALL])
    cs_flat = jnp.pad(cs_flat, (0, cs_rows * 128 - cs_total)).reshape(cs_rows, 128)
    cs_all = _exchange(cs_flat, 'xy', False, "gather_small").reshape(nchip, -1)
    cs_chip = [_unflatten(cs_all[j], cs_shapes) for j in range(nchip)]
    for i, n in enumerate(COLSHARD_SMALL):
        full[n] = jnp.concatenate([cs_chip[j][i] for j in range(nchip)], axis=-1)
    for n in SMALL_NAMES:
        if n not in COLSHARD_SMALL:
            full[n] = wloc[n]

    loss_part, grad_x, grads = _local_step(x, loss_target, full)

    g2d = [grads[n].reshape(-1, grads[n].shape[-1]) for n in BIG_NAMES]
    give = [half_rows(g_, 1 - core).astype(BF16) for g_ in g2d]
    keep = [half_rows(g_, core) for g_ in g2d]
    theirs = _exchange_multi(give, 'c', False, "reduce_cores", single=True)
    psums = []
    for n, k_, t_ in zip(BIG_NAMES, keep, theirs):
        s2 = _sum_rows([k_, t_], BF16, "reduce_cores_sum")
        _, r, c = wloc[n].shape
        if BIG[n] == 1:
            psums.append(s2.reshape(s2.shape[0], nchip, c).transpose(1, 0, 2))
        else:
            psums.append(s2.reshape(hl, nchip, r, c).transpose(1, 0, 2, 3).reshape(nchip, hl * r, c))
    parts = _exchange_multi(psums, 'xy', True, "reduce_chips")
    parts = [own(p_, lax.dynamic_index_in_dim(s_, chip, axis=0, keepdims=False), chip) for p_, s_ in zip(parts, psums)]
    rsums = [_sum_slots(p_, F32, "reduce_chips_sum") for p_ in parts]
    halves = _exchange_multi(rsums, 'c', False, "reduce_share")
    gbig = {n: own(h_, r_, core).reshape(wloc[n].shape) for n, h_, r_ in zip(BIG_NAMES, halves, rsums)}

    sm_shapes = [grads[n].shape for n in SMALL_NAMES]
    sm_total = sum(int(np.prod(s)) for s in sm_shapes) + 128
    sm_rows = -(-sm_total // (8 * 128)) * 8
    sm_flat = jnp.concatenate([loss_part.reshape(-1)] + [grads[n].reshape(-1) for n in SMALL_NAMES])
    sm_flat = jnp.pad(sm_flat, (0, sm_rows * 128 - sm_total)).reshape(sm_rows, 128)
    sm_all = _exchange(sm_flat, 'xyc', False, "gather_small_grads")
    sm_sum = _sum_rows([sm_all[j] for j in range(8)], F32, "small_grads_sum").reshape(-1)
    loss = sm_sum[0]
    gsmall_full = dict(zip(SMALL_NAMES, _unflatten(sm_sum[128:], sm_shapes)))
    gsmall = {}
    for n in SMALL_NAMES:
        gfull = gsmall_full[n]
        if n in COLSHARD_SMALL:
            wcols = wloc[n].shape[-1]
            gfull = lax.dynamic_slice_in_dim(gfull, chip * wcols, wcols, axis=gfull.ndim - 1)
        gsmall[n] = gfull

    big_out = [{}, {}, {}]
    for n in BIG_NAMES:
        rows2d = lambda a: a.reshape(-1, a.shape[-1])
        res = _adamw(rows2d(wloc[n]), rows2d(gbig[n]), rows2d(mloc[n]), rows2d(vloc[n]), "adamw_big")
        for k in range(3):
            big_out[k][n] = res[k].reshape(wloc[n].shape)
    loc_shapes = [wloc[n].shape for n in SMALL_NAMES]
    loc_total = sum(int(np.prod(s)) for s in loc_shapes)
    loc_rows = -(-loc_total // (8 * 128)) * 8

    def flat_small(dct):
        v = jnp.concatenate([dct[n].reshape(-1) for n in SMALL_NAMES])
        return jnp.pad(v, (0, loc_rows * 128 - loc_total)).reshape(loc_rows, 128)
    dls, mns, vns = _adamw(flat_small(wloc), flat_small(gsmall), flat_small(mloc), flat_small(vloc), "adamw_small")
    small_out = [dict(zip(SMALL_NAMES, _unflatten(a.reshape(-1), loc_shapes))) for a in (dls, mns, vns)]

    grad_w = {**gbig, **gsmall}
    outs = [loss, grad_x] + [grad_w[n] for n in WEIGHTS]
    for k in range(3):
        merged = {**big_out[k], **small_out[k]}
        outs += [merged[n] for n in WEIGHTS]
    return tuple(outs)
```
